```python
import jax, jax.numpy as jnp
from jax import lax
import numpy as np

D_MODEL = 1024
BATCH = 8
SEQ = 4096
DEPTH = 1

MIX_WIDTH = D_MODEL
ATTN_WIDTH = MIX_WIDTH // 2
CONV_WIDTH = MIX_WIDTH - ATTN_WIDTH
HEAD_DIM = 64
N_HEADS = ATTN_WIDTH // HEAD_DIM
DILATED_PATTERNS = ((128, 1), (512, 4), (2048, 16))
BLOCK = 128
ROPE_THETA = 10000.0
CONV_KERNEL = 31
D_FF = 2816
IN_COLS = 3 * ATTN_WIDTH + 2 * CONV_WIDTH
N_MOD = 9
RMS_EPS = 1e-6
LN_EPS = 1e-5

kernel_name = "hybrid_dilated_attn_conformer_conv_macaron_adaln"


def rms_norm(x, g):
    xf = x.astype(jnp.float32)
    y = xf * lax.rsqrt(jnp.mean(xf * xf, axis=-1, keepdims=True) + RMS_EPS)
    return (y * g.astype(jnp.float32)).astype(x.dtype)


def layer_norm(x, g, b):
    xf = x.astype(jnp.float32)
    mu = jnp.mean(xf, axis=-1, keepdims=True)
    var = jnp.mean(jnp.square(xf - mu), axis=-1, keepdims=True)
    y = (xf - mu) * lax.rsqrt(var + LN_EPS)
    return (y * g.astype(jnp.float32) + b.astype(jnp.float32)).astype(x.dtype)


def modulate(x, shift, scale):
    return x * (1.0 + scale) + shift


def swiglu(x, w_gate, w_up, w_down):
    return (jax.nn.silu(x @ w_gate) * (x @ w_up)) @ w_down


def apply_rope(t, cos, sin):
    tf = t.astype(jnp.float32)
    t1, t2 = jnp.split(tf, 2, axis=-1)
    c, s = cos[None, :, None, :], sin[None, :, None, :]
    return jnp.concatenate([t1 * c - t2 * s, t2 * c + t1 * s], axis=-1).astype(t.dtype)


def dilated_branch(q, k, v, window, dilation):
    B, S, H, Dh = q.shape
    L = S // dilation
    steps = window // dilation
    assert steps <= BLOCK
    nb = -(-L // BLOCK)
    Lp = nb * BLOCK

    def to_sub(t):
        t = t.reshape(B, L, dilation, H, Dh).transpose(0, 2, 3, 1, 4)
        t = jnp.pad(t, ((0, 0), (0, 0), (0, 0), (0, Lp - L), (0, 0)))
        return t.reshape(B, dilation, H, nb, BLOCK, Dh)

    def band(tb):
        tp = jnp.pad(tb, ((0, 0), (0, 0), (0, 0), (1, 0), (0, 0), (0, 0)))
        return jnp.concatenate([tp[:, :, :, :-1], tp[:, :, :, 1:]], axis=-2)

    qb = to_sub(q)
    kb = band(to_sub(k))
    vb = band(to_sub(v))
    s = jnp.einsum('brhnqd,brhnkd->brhnqk', qb, kb).astype(jnp.float32) * (Dh ** -0.5)
    qi = jnp.arange(BLOCK)[:, None]
    kj = jnp.arange(2 * BLOCK)[None, :]
    blk = jnp.arange(nb)[:, None, None]
    dist = qi - kj + BLOCK
    key_pos = blk * BLOCK + kj - BLOCK
    valid = (dist >= 0) & (dist <= steps) & (key_pos >= 0)
    s = jnp.where(valid, s, -jnp.inf)
    m = jnp.max(s, axis=-1, keepdims=True)
    p = jnp.exp(s - m)
    den = jnp.sum(p, axis=-1, keepdims=True)
    o = jnp.einsum('brhnqk,brhnkd->brhnqd', p.astype(vb.dtype), vb).astype(jnp.float32) / den
    lse = (m + jnp.log(den))[..., 0]
    o = o.reshape(B, dilation, H, Lp, Dh)[:, :, :, :L].transpose(0, 3, 1, 2, 4).reshape(B, S, H, Dh)
    lse = lse.reshape(B, dilation, H, Lp)[:, :, :, :L].transpose(0, 3, 1, 2).reshape(B, S, H)
    return o, lse


def hybrid_mixer(n, w_in, conv_dw_w, conv_dw_b, conv_ln_g, conv_ln_b,
                 attn_out_g, conv_out_g, w_out):
    B, S, _ = n.shape
    proj = n @ w_in
    q, k, v, glu_a, glu_b = jnp.split(
        proj, [ATTN_WIDTH, 2 * ATTN_WIDTH, 3 * ATTN_WIDTH, 3 * ATTN_WIDTH + CONV_WIDTH], axis=-1)

    pos = jnp.arange(S, dtype=jnp.float32)
    inv_freq = ROPE_THETA ** (-jnp.arange(0, HEAD_DIM, 2, dtype=jnp.float32) / HEAD_DIM)
    ang = pos[:, None] * inv_freq[None, :]
    cos, sin = jnp.cos(ang), jnp.sin(ang)
    q = apply_rope(q.reshape(B, S, N_HEADS, HEAD_DIM), cos, sin)
    k = apply_rope(k.reshape(B, S, N_HEADS, HEAD_DIM), cos, sin)
    v = v.reshape(B, S, N_HEADS, HEAD_DIM)
    outs, lses = [], []
    for window, dilation in DILATED_PATTERNS:
        o, lse = dilated_branch(q, k, v, window, dilation)
        outs.append(o)
        lses.append(lse)
    w_mix = jax.nn.softmax(jnp.stack(lses, axis=0), axis=0)
    attn = jnp.einsum('gbsh,gbshd->bshd', w_mix, jnp.stack(outs, axis=0))
    attn = attn.reshape(B, S, ATTN_WIDTH).astype(n.dtype)

    u = glu_a * jax.nn.sigmoid(glu_b)
    u = lax.conv_general_dilated(
        u, conv_dw_w, window_strides=(1,), padding=[(CONV_KERNEL - 1, 0)],
        dimension_numbers=('NWC', 'WIO', 'NWC'), feature_group_count=CONV_WIDTH) + conv_dw_b
    u = jax.nn.silu(layer_norm(u, conv_ln_g, conv_ln_b))

    y = jnp.concatenate([rms_norm(attn, attn_out_g), rms_norm(u, conv_out_g)], axis=-1)
    return y @ w_out


def _fwd_setup_inputs(seed: int = 0) -> dict:
    key = jax.random.key(seed)
    ks = jax.random.split(key, 24)
    f32 = jnp.float32

    def nrm(k, shape, fan_in, gain=1.0):
        return jax.random.normal(k, shape, f32) * (gain * fan_in ** -0.5)

    def gain(k, shape):
        return 1.0 + 0.02 * jax.random.normal(k, shape, f32)

    def bias(k, shape):
        return 0.02 * jax.random.normal(k, shape, f32)

    L = DEPTH
    return {
        "x": jax.random.normal(ks[0], (BATCH, SEQ, D_MODEL), f32),
        "c": jax.random.normal(ks[1], (BATCH, D_MODEL), f32),
        "w_ada": nrm(ks[2], (L, D_MODEL, N_MOD * D_MODEL), D_MODEL, 0.5),
        "b_ada": bias(ks[3], (L, N_MOD * D_MODEL)),
        "ffn1_norm_g": gain(ks[4], (L, D_MODEL)),
        "ffn1_w_gate": nrm(ks[5], (L, D_MODEL, D_FF), D_MODEL),
        "ffn1_w_up": nrm(ks[6], (L, D_MODEL, D_FF), D_MODEL),
        "ffn1_w_down": nrm(ks[7], (L, D_FF, D_MODEL), D_FF),
        "mix_norm_g": gain(ks[8], (L, D_MODEL)),
        "w_in": nrm(ks[9], (L, D_MODEL, IN_COLS), D_MODEL),
        "conv_dw_w": nrm(ks[10], (L, CONV_KERNEL, 1, CONV_WIDTH), CONV_KERNEL),
        "conv_dw_b": bias(ks[11], (L, CONV_WIDTH)),
        "conv_ln_g": gain(ks[12], (L, CONV_WIDTH)),
        "conv_ln_b": bias(ks[13], (L, CONV_WIDTH)),
        "attn_out_g": gain(ks[14], (L, ATTN_WIDTH)),
        "conv_out_g": gain(ks[15], (L, CONV_WIDTH)),
        "w_out": nrm(ks[16], (L, MIX_WIDTH, D_MODEL), MIX_WIDTH),
        "ffn2_norm_g": gain(ks[17], (L, D_MODEL)),
        "ffn2_w_gate": nrm(ks[18], (L, D_MODEL, D_FF), D_MODEL),
        "ffn2_w_up": nrm(ks[19], (L, D_MODEL, D_FF), D_MODEL),
        "ffn2_w_down": nrm(ks[20], (L, D_FF, D_MODEL), D_FF),
        "final_norm_g": gain(ks[21], (D_MODEL,)),
    }


def _fwd_reference(x, c, w_ada, b_ada, ffn1_norm_g, ffn1_w_gate, ffn1_w_up, ffn1_w_down,
              mix_norm_g, w_in, conv_dw_w, conv_dw_b, conv_ln_g, conv_ln_b,
              attn_out_g, conv_out_g, w_out, ffn2_norm_g, ffn2_w_gate, ffn2_w_up,
              ffn2_w_down, final_norm_g):
    h = x
    for l in range(DEPTH):
        mod = jax.nn.silu(c) @ w_ada[l] + b_ada[l]
        sh1, sc1, g1, sh2, sc2, g2, sh3, sc3, g3 = jnp.split(mod[:, None, :], N_MOD, axis=-1)
        n1 = modulate(rms_norm(h, ffn1_norm_g[l]), sh1, sc1)
        h = h + 0.5 * g1 * swiglu(n1, ffn1_w_gate[l], ffn1_w_up[l], ffn1_w_down[l])
        n2 = modulate(rms_norm(h, mix_norm_g[l]), sh2, sc2)
        h = h + g2 * hybrid_mixer(n2, w_in[l], conv_dw_w[l], conv_dw_b[l], conv_ln_g[l],
                                  conv_ln_b[l], attn_out_g[l], conv_out_g[l], w_out[l])
        n3 = modulate(rms_norm(h, ffn2_norm_g[l]), sh3, sc3)
        h = h + 0.5 * g3 * swiglu(n3, ffn2_w_gate[l], ffn2_w_up[l], ffn2_w_down[l])
    return rms_norm(h, final_norm_g)


import jax as _jax
import jax.numpy as _jnp

TWIN_FORMAT = 'train_step'
FWD_PARAMS = ['x', 'c', 'w_ada', 'b_ada', 'ffn1_norm_g', 'ffn1_w_gate', 'ffn1_w_up', 'ffn1_w_down', 'mix_norm_g', 'w_in', 'conv_dw_w', 'conv_dw_b', 'conv_ln_g', 'conv_ln_b', 'attn_out_g', 'conv_out_g', 'w_out', 'ffn2_norm_g', 'ffn2_w_gate', 'ffn2_w_up', 'ffn2_w_down', 'final_norm_g']
TWIN_WEIGHTS = ['w_ada', 'b_ada', 'ffn1_norm_g', 'ffn1_w_gate', 'ffn1_w_up', 'ffn1_w_down', 'mix_norm_g', 'w_in', 'conv_dw_w', 'conv_dw_b', 'conv_ln_g', 'conv_ln_b', 'attn_out_g', 'conv_out_g', 'w_out', 'ffn2_norm_g', 'ffn2_w_gate', 'ffn2_w_up', 'ffn2_w_down', 'final_norm_g']
TWIN_DIFF_INPUT = 'x'
TWIN_INPUTS = ['x', 'c', 'w_ada', 'b_ada', 'ffn1_norm_g', 'ffn1_w_gate', 'ffn1_w_up', 'ffn1_w_down', 'mix_norm_g', 'w_in', 'conv_dw_w', 'conv_dw_b', 'conv_ln_g', 'conv_ln_b', 'attn_out_g', 'conv_out_g', 'w_out', 'ffn2_norm_g', 'ffn2_w_gate', 'ffn2_w_up', 'ffn2_w_down', 'final_norm_g', 'loss_target', 'm_w_ada', 'm_b_ada', 'm_ffn1_norm_g', 'm_ffn1_w_gate', 'm_ffn1_w_up', 'm_ffn1_w_down', 'm_mix_norm_g', 'm_w_in', 'm_conv_dw_w', 'm_conv_dw_b', 'm_conv_ln_g', 'm_conv_ln_b', 'm_attn_out_g', 'm_conv_out_g', 'm_w_out', 'm_ffn2_norm_g', 'm_ffn2_w_gate', 'm_ffn2_w_up', 'm_ffn2_w_down', 'm_final_norm_g', 'v_w_ada', 'v_b_ada', 'v_ffn1_norm_g', 'v_ffn1_w_gate', 'v_ffn1_w_up', 'v_ffn1_w_down', 'v_mix_norm_g', 'v_w_in', 'v_conv_dw_w', 'v_conv_dw_b', 'v_conv_ln_g', 'v_conv_ln_b', 'v_attn_out_g', 'v_conv_out_g', 'v_w_out', 'v_ffn2_norm_g', 'v_ffn2_w_gate', 'v_ffn2_w_up', 'v_ffn2_w_down', 'v_final_norm_g']
TWIN_OUTPUTS = ['loss', 'grad_x', 'grad_w_ada', 'grad_b_ada', 'grad_ffn1_norm_g', 'grad_ffn1_w_gate', 'grad_ffn1_w_up', 'grad_ffn1_w_down', 'grad_mix_norm_g', 'grad_w_in', 'grad_conv_dw_w', 'grad_conv_dw_b', 'grad_conv_ln_g', 'grad_conv_ln_b', 'grad_attn_out_g', 'grad_conv_out_g', 'grad_w_out', 'grad_ffn2_norm_g', 'grad_ffn2_w_gate', 'grad_ffn2_w_up', 'grad_ffn2_w_down', 'grad_final_norm_g', 'delta_w_ada', 'delta_b_ada', 'delta_ffn1_norm_g', 'delta_ffn1_w_gate', 'delta_ffn1_w_up', 'delta_ffn1_w_down', 'delta_mix_norm_g', 'delta_w_in', 'delta_conv_dw_w', 'delta_conv_dw_b', 'delta_conv_ln_g', 'delta_conv_ln_b', 'delta_attn_out_g', 'delta_conv_out_g', 'delta_w_out', 'delta_ffn2_norm_g', 'delta_ffn2_w_gate', 'delta_ffn2_w_up', 'delta_ffn2_w_down', 'delta_final_norm_g', 'new_m_w_ada', 'new_m_b_ada', 'new_m_ffn1_norm_g', 'new_m_ffn1_w_gate', 'new_m_ffn1_w_up', 'new_m_ffn1_w_down', 'new_m_mix_norm_g', 'new_m_w_in', 'new_m_conv_dw_w', 'new_m_conv_dw_b', 'new_m_conv_ln_g', 'new_m_conv_ln_b', 'new_m_attn_out_g', 'new_m_conv_out_g', 'new_m_w_out', 'new_m_ffn2_norm_g', 'new_m_ffn2_w_gate', 'new_m_ffn2_w_up', 'new_m_ffn2_w_down', 'new_m_final_norm_g', 'new_v_w_ada', 'new_v_b_ada', 'new_v_ffn1_norm_g', 'new_v_ffn1_w_gate', 'new_v_ffn1_w_up', 'new_v_ffn1_w_down', 'new_v_mix_norm_g', 'new_v_w_in', 'new_v_conv_dw_w', 'new_v_conv_dw_b', 'new_v_conv_ln_g', 'new_v_conv_ln_b', 'new_v_attn_out_g', 'new_v_conv_out_g', 'new_v_w_out', 'new_v_ffn2_norm_g', 'new_v_ffn2_w_gate', 'new_v_ffn2_w_up', 'new_v_ffn2_w_down', 'new_v_final_norm_g']
TWIN_LEAF_KINDS = {'loss': 'loss', 'grad_x': 'grad_x', 'grad_w_ada': 'grad_w', 'grad_b_ada': 'grad_w', 'grad_ffn1_norm_g': 'grad_w', 'grad_ffn1_w_gate': 'grad_w', 'grad_ffn1_w_up': 'grad_w', 'grad_ffn1_w_down': 'grad_w', 'grad_mix_norm_g': 'grad_w', 'grad_w_in': 'grad_w', 'grad_conv_dw_w': 'grad_w', 'grad_conv_dw_b': 'grad_w', 'grad_conv_ln_g': 'grad_w', 'grad_conv_ln_b': 'grad_w', 'grad_attn_out_g': 'grad_w', 'grad_conv_out_g': 'grad_w', 'grad_w_out': 'grad_w', 'grad_ffn2_norm_g': 'grad_w', 'grad_ffn2_w_gate': 'grad_w', 'grad_ffn2_w_up': 'grad_w', 'grad_ffn2_w_down': 'grad_w', 'grad_final_norm_g': 'grad_w', 'delta_w_ada': 'delta_w', 'delta_b_ada': 'delta_w', 'delta_ffn1_norm_g': 'delta_w', 'delta_ffn1_w_gate': 'delta_w', 'delta_ffn1_w_up': 'delta_w', 'delta_ffn1_w_down': 'delta_w', 'delta_mix_norm_g': 'delta_w', 'delta_w_in': 'delta_w', 'delta_conv_dw_w': 'delta_w', 'delta_conv_dw_b': 'delta_w', 'delta_conv_ln_g': 'delta_w', 'delta_conv_ln_b': 'delta_w', 'delta_attn_out_g': 'delta_w', 'delta_conv_out_g': 'delta_w', 'delta_w_out': 'delta_w', 'delta_ffn2_norm_g': 'delta_w', 'delta_ffn2_w_gate': 'delta_w', 'delta_ffn2_w_up': 'delta_w', 'delta_ffn2_w_down': 'delta_w', 'delta_final_norm_g': 'delta_w', 'new_m_w_ada': 'new_m', 'new_m_b_ada': 'new_m', 'new_m_ffn1_norm_g': 'new_m', 'new_m_ffn1_w_gate': 'new_m', 'new_m_ffn1_w_up': 'new_m', 'new_m_ffn1_w_down': 'new_m', 'new_m_mix_norm_g': 'new_m', 'new_m_w_in': 'new_m', 'new_m_conv_dw_w': 'new_m', 'new_m_conv_dw_b': 'new_m', 'new_m_conv_ln_g': 'new_m', 'new_m_conv_ln_b': 'new_m', 'new_m_attn_out_g': 'new_m', 'new_m_conv_out_g': 'new_m', 'new_m_w_out': 'new_m', 'new_m_ffn2_norm_g': 'new_m', 'new_m_ffn2_w_gate': 'new_m', 'new_m_ffn2_w_up': 'new_m', 'new_m_ffn2_w_down': 'new_m', 'new_m_final_norm_g': 'new_m', 'new_v_w_ada': 'new_v', 'new_v_b_ada': 'new_v', 'new_v_ffn1_norm_g': 'new_v', 'new_v_ffn1_w_gate': 'new_v', 'new_v_ffn1_w_up': 'new_v', 'new_v_ffn1_w_down': 'new_v', 'new_v_mix_norm_g': 'new_v', 'new_v_w_in': 'new_v', 'new_v_conv_dw_w': 'new_v', 'new_v_conv_dw_b': 'new_v', 'new_v_conv_ln_g': 'new_v', 'new_v_conv_ln_b': 'new_v', 'new_v_attn_out_g': 'new_v', 'new_v_conv_out_g': 'new_v', 'new_v_w_out': 'new_v', 'new_v_ffn2_norm_g': 'new_v', 'new_v_ffn2_w_gate': 'new_v', 'new_v_ffn2_w_up': 'new_v', 'new_v_ffn2_w_down': 'new_v', 'new_v_final_norm_g': 'new_v'}


def _forward(args):
    return _fwd_reference(*[args[k] for k in FWD_PARAMS])


def _output_shape():
    def fwd():
        inp = _fwd_setup_inputs(0)
        return _fwd_reference(*[inp[k] for k in FWD_PARAMS])
    out = _jax.eval_shape(fwd)
    return out.shape, out.dtype

N_MICROBATCH = 1
ADAM_LR = 0.001
ADAM_B1 = 0.9
ADAM_B2 = 0.999
ADAM_EPS = 1e-08
ADAM_WD = 0.01
ADAM_STEP = 10
PER_EXAMPLE_BATCH_AXIS = {'x': 0, 'c': 0, 'loss_target': 0}
SHARED_INPUTS = []
_WEIGHT_DTYPES = {'w_ada': _jnp.float32, 'b_ada': _jnp.float32, 'ffn1_norm_g': _jnp.float32, 'ffn1_w_gate': _jnp.float32, 'ffn1_w_up': _jnp.float32, 'ffn1_w_down': _jnp.float32, 'mix_norm_g': _jnp.float32, 'w_in': _jnp.float32, 'conv_dw_w': _jnp.float32, 'conv_dw_b': _jnp.float32, 'conv_ln_g': _jnp.float32, 'conv_ln_b': _jnp.float32, 'attn_out_g': _jnp.float32, 'conv_out_g': _jnp.float32, 'w_out': _jnp.float32, 'ffn2_norm_g': _jnp.float32, 'ffn2_w_gate': _jnp.float32, 'ffn2_w_up': _jnp.float32, 'ffn2_w_down': _jnp.float32, 'final_norm_g': _jnp.float32}
MOMENT_SCALE = {'w_ada': 5.351281e-02, 'b_ada': 9.145706e-02, 'ffn1_norm_g': 2.626691e-02, 'ffn1_w_gate': 1.169915e-02, 'ffn1_w_up': 1.133257e-02, 'ffn1_w_down': 1.877946e-02, 'mix_norm_g': 5.363096e-02, 'w_in': 3.935964e-02, 'conv_dw_w': 5.336015e-02, 'conv_dw_b': 9.991275e-02, 'conv_ln_g': 6.628828e-02, 'conv_ln_b': 6.494188e-02, 'attn_out_g': 5.606871e-02, 'conv_out_g': 5.490142e-02, 'w_out': 5.686715e-02, 'ffn2_norm_g': 2.489711e-02, 'ffn2_w_gate': 1.128028e-02, 'ffn2_w_up': 1.092928e-02, 'ffn2_w_down': 1.812957e-02, 'final_norm_g': 3.208121e+01}


def _to_microbatches(a, axis):
    t = _jnp.moveaxis(a, axis, 0)
    t = t.reshape((N_MICROBATCH, t.shape[0] // N_MICROBATCH) + t.shape[1:])
    return _jnp.moveaxis(t, 1, axis + 1)


def setup_inputs(seed: int = 0) -> dict:
    inp = _fwd_setup_inputs(seed)
    key = _jax.random.fold_in(_jax.random.key(seed), 7919)
    shape, _ = _output_shape()
    out = dict(inp)
    out["loss_target"] = _jax.random.normal(_jax.random.fold_in(key, 0), shape, _jnp.float32)
    for i, name in enumerate(TWIN_WEIGHTS):
        w = inp[name].astype(_jnp.float32)
        if MOMENT_SCALE is None:
            s = _jnp.sqrt(_jnp.mean(_jnp.square(w)) + 1e-30)
        else:
            s = MOMENT_SCALE[name]
        km, kv = _jax.random.split(_jax.random.fold_in(key, i + 1))
        out[name] = w
        out["m_" + name] = s * _jax.random.normal(km, w.shape, _jnp.float32)
        out["v_" + name] = (s * s) * _jax.random.uniform(kv, w.shape, _jnp.float32, 0.5, 1.5)
    if N_MICROBATCH > 1:
        for name, axis in PER_EXAMPLE_BATCH_AXIS.items():
            out[name] = _to_microbatches(out[name], axis)
    return {'x': out['x'], 'c': out['c'], 'w_ada': out['w_ada'], 'b_ada': out['b_ada'], 'ffn1_norm_g': out['ffn1_norm_g'], 'ffn1_w_gate': out['ffn1_w_gate'], 'ffn1_w_up': out['ffn1_w_up'], 'ffn1_w_down': out['ffn1_w_down'], 'mix_norm_g': out['mix_norm_g'], 'w_in': out['w_in'], 'conv_dw_w': out['conv_dw_w'], 'conv_dw_b': out['conv_dw_b'], 'conv_ln_g': out['conv_ln_g'], 'conv_ln_b': out['conv_ln_b'], 'attn_out_g': out['attn_out_g'], 'conv_out_g': out['conv_out_g'], 'w_out': out['w_out'], 'ffn2_norm_g': out['ffn2_norm_g'], 'ffn2_w_gate': out['ffn2_w_gate'], 'ffn2_w_up': out['ffn2_w_up'], 'ffn2_w_down': out['ffn2_w_down'], 'final_norm_g': out['final_norm_g'], 'loss_target': out['loss_target'], 'm_w_ada': out['m_w_ada'], 'm_b_ada': out['m_b_ada'], 'm_ffn1_norm_g': out['m_ffn1_norm_g'], 'm_ffn1_w_gate': out['m_ffn1_w_gate'], 'm_ffn1_w_up': out['m_ffn1_w_up'], 'm_ffn1_w_down': out['m_ffn1_w_down'], 'm_mix_norm_g': out['m_mix_norm_g'], 'm_w_in': out['m_w_in'], 'm_conv_dw_w': out['m_conv_dw_w'], 'm_conv_dw_b': out['m_conv_dw_b'], 'm_conv_ln_g': out['m_conv_ln_g'], 'm_conv_ln_b': out['m_conv_ln_b'], 'm_attn_out_g': out['m_attn_out_g'], 'm_conv_out_g': out['m_conv_out_g'], 'm_w_out': out['m_w_out'], 'm_ffn2_norm_g': out['m_ffn2_norm_g'], 'm_ffn2_w_gate': out['m_ffn2_w_gate'], 'm_ffn2_w_up': out['m_ffn2_w_up'], 'm_ffn2_w_down': out['m_ffn2_w_down'], 'm_final_norm_g': out['m_final_norm_g'], 'v_w_ada': out['v_w_ada'], 'v_b_ada': out['v_b_ada'], 'v_ffn1_norm_g': out['v_ffn1_norm_g'], 'v_ffn1_w_gate': out['v_ffn1_w_gate'], 'v_ffn1_w_up': out['v_ffn1_w_up'], 'v_ffn1_w_down': out['v_ffn1_w_down'], 'v_mix_norm_g': out['v_mix_norm_g'], 'v_w_in': out['v_w_in'], 'v_conv_dw_w': out['v_conv_dw_w'], 'v_conv_dw_b': out['v_conv_dw_b'], 'v_conv_ln_g': out['v_conv_ln_g'], 'v_conv_ln_b': out['v_conv_ln_b'], 'v_attn_out_g': out['v_attn_out_g'], 'v_conv_out_g': out['v_conv_out_g'], 'v_w_out': out['v_w_out'], 'v_ffn2_norm_g': out['v_ffn2_norm_g'], 'v_ffn2_w_gate': out['v_ffn2_w_gate'], 'v_ffn2_w_up': out['v_ffn2_w_up'], 'v_ffn2_w_down': out['v_ffn2_w_down'], 'v_final_norm_g': out['v_final_norm_g']}


def _loss(weights, diff, rest, loss_target):
    with _jax.named_scope("forward"):
        args = {**rest, TWIN_DIFF_INPUT: diff, **{k: w.astype(_WEIGHT_DTYPES[k]) for k, w in weights.items()}}
        y = _forward(args)
    with _jax.named_scope("loss_head"):
        err = _jnp.square(y.astype(_jnp.float32) - loss_target)
        return 0.5 * _jnp.sum(_jnp.mean(err, axis=-1)) if err.ndim else 0.5 * err


def _adamw(w, g, m, v):
    m = ADAM_B1 * m + (1.0 - ADAM_B1) * g
    v = ADAM_B2 * v + (1.0 - ADAM_B2) * _jnp.square(g)
    m_hat = m / (1.0 - ADAM_B1 ** ADAM_STEP)
    v_hat = v / (1.0 - ADAM_B2 ** ADAM_STEP)
    delta = -ADAM_LR * (m_hat / (_jnp.sqrt(v_hat) + ADAM_EPS) + ADAM_WD * w)
    return delta, m, v


def reference(x, c, w_ada, b_ada, ffn1_norm_g, ffn1_w_gate, ffn1_w_up, ffn1_w_down, mix_norm_g, w_in, conv_dw_w, conv_dw_b, conv_ln_g, conv_ln_b, attn_out_g, conv_out_g, w_out, ffn2_norm_g, ffn2_w_gate, ffn2_w_up, ffn2_w_down, final_norm_g, loss_target, m_w_ada, m_b_ada, m_ffn1_norm_g, m_ffn1_w_gate, m_ffn1_w_up, m_ffn1_w_down, m_mix_norm_g, m_w_in, m_conv_dw_w, m_conv_dw_b, m_conv_ln_g, m_conv_ln_b, m_attn_out_g, m_conv_out_g, m_w_out, m_ffn2_norm_g, m_ffn2_w_gate, m_ffn2_w_up, m_ffn2_w_down, m_final_norm_g, v_w_ada, v_b_ada, v_ffn1_norm_g, v_ffn1_w_gate, v_ffn1_w_up, v_ffn1_w_down, v_mix_norm_g, v_w_in, v_conv_dw_w, v_conv_dw_b, v_conv_ln_g, v_conv_ln_b, v_attn_out_g, v_conv_out_g, v_w_out, v_ffn2_norm_g, v_ffn2_w_gate, v_ffn2_w_up, v_ffn2_w_down, v_final_norm_g):
    given = dict(x=x, c=c, w_ada=w_ada, b_ada=b_ada, ffn1_norm_g=ffn1_norm_g, ffn1_w_gate=ffn1_w_gate, ffn1_w_up=ffn1_w_up, ffn1_w_down=ffn1_w_down, mix_norm_g=mix_norm_g, w_in=w_in, conv_dw_w=conv_dw_w, conv_dw_b=conv_dw_b, conv_ln_g=conv_ln_g, conv_ln_b=conv_ln_b, attn_out_g=attn_out_g, conv_out_g=conv_out_g, w_out=w_out, ffn2_norm_g=ffn2_norm_g, ffn2_w_gate=ffn2_w_gate, ffn2_w_up=ffn2_w_up, ffn2_w_down=ffn2_w_down, final_norm_g=final_norm_g, loss_target=loss_target, m_w_ada=m_w_ada, m_b_ada=m_b_ada, m_ffn1_norm_g=m_ffn1_norm_g, m_ffn1_w_gate=m_ffn1_w_gate, m_ffn1_w_up=m_ffn1_w_up, m_ffn1_w_down=m_ffn1_w_down, m_mix_norm_g=m_mix_norm_g, m_w_in=m_w_in, m_conv_dw_w=m_conv_dw_w, m_conv_dw_b=m_conv_dw_b, m_conv_ln_g=m_conv_ln_g, m_conv_ln_b=m_conv_ln_b, m_attn_out_g=m_attn_out_g, m_conv_out_g=m_conv_out_g, m_w_out=m_w_out, m_ffn2_norm_g=m_ffn2_norm_g, m_ffn2_w_gate=m_ffn2_w_gate, m_ffn2_w_up=m_ffn2_w_up, m_ffn2_w_down=m_ffn2_w_down, m_final_norm_g=m_final_norm_g, v_w_ada=v_w_ada, v_b_ada=v_b_ada, v_ffn1_norm_g=v_ffn1_norm_g, v_ffn1_w_gate=v_ffn1_w_gate, v_ffn1_w_up=v_ffn1_w_up, v_ffn1_w_down=v_ffn1_w_down, v_mix_norm_g=v_mix_norm_g, v_w_in=v_w_in, v_conv_dw_w=v_conv_dw_w, v_conv_dw_b=v_conv_dw_b, v_conv_ln_g=v_conv_ln_g, v_conv_ln_b=v_conv_ln_b, v_attn_out_g=v_attn_out_g, v_conv_out_g=v_conv_out_g, v_w_out=v_w_out, v_ffn2_norm_g=v_ffn2_norm_g, v_ffn2_w_gate=v_ffn2_w_gate, v_ffn2_w_up=v_ffn2_w_up, v_ffn2_w_down=v_ffn2_w_down, v_final_norm_g=v_final_norm_g)
    weights = {n: given[n] for n in TWIN_WEIGHTS}
    shared = {n: given[n] for n in SHARED_INPUTS}
    per_example = {n: given[n] for n in ['x', 'c']}
    grad_fn = _jax.value_and_grad(_loss, argnums=(0, 1))

    def one_microbatch(ex, loss_target):
        ex = dict(ex)
        diff = ex.pop(TWIN_DIFF_INPUT)
        return grad_fn(weights, diff, {**shared, **ex}, loss_target)

    if N_MICROBATCH == 1:
        loss, (grad_w, grad_x) = one_microbatch(per_example, given["loss_target"])
    else:
        def body(carry, xs):
            loss_sum, grad_sum = carry
            l_k, (gw_k, gx_k) = one_microbatch(xs[0], xs[1])
            with _jax.named_scope("update"):
                return (loss_sum + l_k, _jax.tree.map(_jnp.add, grad_sum, gw_k)), gx_k

        init = (_jnp.zeros((), _jnp.float32), _jax.tree.map(_jnp.zeros_like, weights))
        (loss, grad_w), grad_x = _jax.lax.scan(body, init, (per_example, given["loss_target"]))
    with _jax.named_scope("update"):
        delta_w, new_m, new_v = {}, {}, {}
        for n in TWIN_WEIGHTS:
            delta_w[n], new_m[n], new_v[n] = _adamw(weights[n], grad_w[n], given["m_" + n], given["v_" + n])
    return (loss, grad_x, *[grad_w[n] for n in TWIN_WEIGHTS], *[delta_w[n] for n in TWIN_WEIGHTS],
            *[new_m[n] for n in TWIN_WEIGHTS], *[new_v[n] for n in TWIN_WEIGHTS])
```

```python
import jax
import jax.numpy as jnp
from jax import lax
from jax.experimental import pallas as pl
from jax.experimental.pallas import tpu as pltpu

F32 = jnp.float32
BF16 = jnp.bfloat16
MESH = pl.DeviceIdType.MESH

RMS_EPS = 1e-6
LN_EPS = 1e-5
HEAD_DIM = 64
ATTN_BLOCK = 128
DILATIONS = (1, 4, 16)
ROPE_THETA = 10000.0
CONV_KERNEL = 31
HALO = 32
N_CHIPS = 4
N_DEV = 8
ADAM_LR, ADAM_B1, ADAM_B2, ADAM_EPS, ADAM_WD, ADAM_STEP = 0.001, 0.9, 0.999, 1e-08, 0.01, 10
VMEM_LIMIT_BYTES = 48 * 1024 * 1024
NEG = -1e30

NT = (((1,), (1,)), ((), ()))
TN = (((0,), (0,)), ((), ()))


def _params(sem=None):
    return pltpu.CompilerParams(dimension_semantics=sem, vmem_limit_bytes=VMEM_LIMIT_BYTES)


def _row_tile(rows, want):
    t = min(rows, want)
    assert rows % t == 0
    return t


def _sigmoid(x):
    return 1.0 / (1.0 + jnp.exp(-x))


def _vec_spec(d, ngrid):
    if ngrid == 1:
        return pl.BlockSpec((1, d), lambda i: (0, 0))
    return pl.BlockSpec((1, d), lambda i, j: (0, 0))


def _norm_mod(h, gain, sc, sh, name):
    S, D = h.shape
    tr = _row_tile(S, 512)

    def body(h_ref, g_ref, sc_ref, sh_ref, n_ref):
        x = h_ref[...]
        r = lax.rsqrt(jnp.mean(x * x, axis=-1, keepdims=True) + RMS_EPS)
        y = (x * r) * g_ref[...]
        n_ref[...] = (y * (1.0 + sc_ref[...]) + sh_ref[...]).astype(BF16)

    row = pl.BlockSpec((tr, D), lambda i: (i, 0))
    return pl.pallas_call(
        body, name=name, grid=(S // tr,),
        in_specs=[row, _vec_spec(D, 1), _vec_spec(D, 1), _vec_spec(D, 1)],
        out_specs=row, out_shape=jax.ShapeDtypeStruct((S, D), BF16),
        compiler_params=_params(("parallel",)),
    )(h, gain, sc, sh)


def _norm_mod_bwd(dn, h_in, gain, sc, dh_out, name):
    S, D = h_in.shape
    tr = _row_tile(S, 512)

    def body(dn_ref, h_ref, g_ref, sc_ref, dho_ref, dh_ref, dsh_ref, dsc_ref, dg_ref):
        @pl.when(pl.program_id(0) == 0)
        def _():
            dsh_ref[...] = jnp.zeros_like(dsh_ref)
            dsc_ref[...] = jnp.zeros_like(dsc_ref)
            dg_ref[...] = jnp.zeros_like(dg_ref)

        x = h_ref[...]
        dn_ = dn_ref[...]
        g = g_ref[...]
        one_sc = 1.0 + sc_ref[...]
        r = lax.rsqrt(jnp.mean(x * x, axis=-1, keepdims=True) + RMS_EPS)
        xh = x * r
        dsh_ref[...] += jnp.sum(dn_, axis=0, keepdims=True)
        dsc_ref[...] += jnp.sum(dn_ * (xh * g), axis=0, keepdims=True)
        dg_ref[...] += jnp.sum(dn_ * one_sc * xh, axis=0, keepdims=True)
        dxh = dn_ * (g * one_sc)
        dh_ref[...] = dho_ref[...] + r * (dxh - xh * jnp.mean(dxh * xh, axis=-1, keepdims=True))

    row = pl.BlockSpec((tr, D), lambda i: (i, 0))
    vec = _vec_spec(D, 1)
    return pl.pallas_call(
        body, name=name, grid=(S // tr,),
        in_specs=[row, row, vec, vec, row],
        out_specs=[row, vec, vec, vec],
        out_shape=[jax.ShapeDtypeStruct((S, D), F32)] + [jax.ShapeDtypeStruct((1, D), F32)] * 3,
        compiler_params=_params(("arbitrary",)),
    )(dn, h_in, gain, sc, dh_out)


def _gate_bwd(dh, f, gvec, coef, name):
    S, D = dh.shape
    tr = _row_tile(S, 512)

    def body(dh_ref, f_ref, g_ref, df_ref, dg_ref):
        @pl.when(pl.program_id(0) == 0)
        def _():
            dg_ref[...] = jnp.zeros_like(dg_ref)

        dh_ = dh_ref[...]
        df_ref[...] = ((coef * g_ref[...]) * dh_).astype(BF16)
        dg_ref[...] += jnp.sum(coef * dh_ * f_ref[...].astype(F32), axis=0, keepdims=True)

    row = pl.BlockSpec((tr, D), lambda i: (i, 0))
    vec = _vec_spec(D, 1)
    return pl.pallas_call(
        body, name=name, grid=(S // tr,),
        in_specs=[row, row, vec], out_specs=[row, vec],
        out_shape=[jax.ShapeDtypeStruct((S, D), BF16), jax.ShapeDtypeStruct((1, D), F32)],
        compiler_params=_params(("arbitrary",)),
    )(dh, f, gvec)


def _loss_head(h, gain, target, name):
    S, D = h.shape
    tr = _row_tile(S, 512)

    def body(h_ref, g_ref, t_ref, loss_ref, dh_ref, dg_ref):
        @pl.when(pl.program_id(0) == 0)
        def _():
            loss_ref[...] = jnp.zeros_like(loss_ref)
            dg_ref[...] = jnp.zeros_like(dg_ref)

        x = h_ref[...]
        g = g_ref[...]
        r = lax.rsqrt(jnp.mean(x * x, axis=-1, keepdims=True) + RMS_EPS)
        xh = x * r
        err = xh * g - t_ref[...]
        part = 0.5 * jnp.sum(jnp.mean(err * err, axis=-1, keepdims=True), axis=0, keepdims=True)
        loss_ref[...] += jnp.broadcast_to(part, loss_ref.shape)
        dy = err * (1.0 / D)
        dg_ref[...] += jnp.sum(dy * xh, axis=0, keepdims=True)
        dxh = dy * g
        dh_ref[...] = r * (dxh - xh * jnp.mean(dxh * xh, axis=-1, keepdims=True))

    row = pl.BlockSpec((tr, D), lambda i: (i, 0))
    vec = _vec_spec(D, 1)
    return pl.pallas_call(
        body, name=name, grid=(S // tr,),
        in_specs=[row, vec, row],
        out_specs=[pl.BlockSpec((1, 128), lambda i: (0, 0)), row, vec],
        out_shape=[jax.ShapeDtypeStruct((1, 128), F32), jax.ShapeDtypeStruct((S, D), F32),
                   jax.ShapeDtypeStruct((1, D), F32)],
        compiler_params=_params(("arbitrary",)),
    )(h, gain, target)


def _ffn_gate_up(n, wg, wu, name):
    S, D = n.shape
    nk, _, w = wg.shape
    tm = _row_tile(S, 512)

    def body(n_ref, wg_ref, wu_ref, ga_ref, up_ref, act_ref):
        x = n_ref[...]
        ga = jnp.dot(x, wg_ref[...], preferred_element_type=F32)
        up = jnp.dot(x, wu_ref[...], preferred_element_type=F32)
        ga_ref[...] = ga.astype(BF16)
        up_ref[...] = up.astype(BF16)
        act_ref[...] = ((ga * _sigmoid(ga)) * up).astype(BF16)

    wspec = pl.BlockSpec((None, D, w), lambda k, m: (k, 0, 0))
    ospec = pl.BlockSpec((None, tm, w), lambda k, m: (k, m, 0))
    out = jax.ShapeDtypeStruct((nk, S, w), BF16)
    return pl.pallas_call(
        body, name=name, grid=(nk, S // tm),
        in_specs=[pl.BlockSpec((tm, D), lambda k, m: (m, 0)), wspec, wspec],
        out_specs=[ospec, ospec, ospec], out_shape=[out, out, out],
        compiler_params=_params(("parallel", "parallel")),
    )(n, wg, wu)


def _mm_residual(lhs, w, h_in, gvec, coef, name):
    nk, S, kc = lhs.shape
    D = w.shape[2]
    tm = _row_tile(S, 512)

    def body(l_ref, w_ref, h_ref, g_ref, ho_ref, f_ref, acc_ref):
        k = pl.program_id(1)

        @pl.when(k == 0)
        def _():
            acc_ref[...] = jnp.zeros_like(acc_ref)

        acc_ref[...] += jnp.dot(l_ref[...], w_ref[...], preferred_element_type=F32)

        @pl.when(k == nk - 1)
        def _():
            f = acc_ref[...]
            f_ref[...] = f.astype(BF16)
            ho_ref[...] = h_ref[...] + (coef * g_ref[...]) * f

    row = pl.BlockSpec((tm, D), lambda m, k: (m, 0))
    return pl.pallas_call(
        body, name=name, grid=(S // tm, nk),
        in_specs=[pl.BlockSpec((None, tm, kc), lambda m, k: (k, m, 0)),
                  pl.BlockSpec((None, kc, D), lambda m, k: (k, 0, 0)), row, _vec_spec(D, 2)],
        out_specs=[row, row],
        out_shape=[jax.ShapeDtypeStruct((S, D), F32), jax.ShapeDtypeStruct((S, D), BF16)],
        scratch_shapes=[pltpu.VMEM((tm, D), F32)],
        compiler_params=_params(("parallel", "arbitrary")),
    )(lhs, w, h_in, gvec)


def _mm_cols(n, w, name):
    S, D = n.shape
    nk, _, wd = w.shape
    assert wd % 128 == 0
    tm = _row_tile(S, 512)

    def body(n_ref, w_ref, o_ref):
        o_ref[...] = jnp.dot(n_ref[...], w_ref[...], preferred_element_type=F32)

    return pl.pallas_call(
        body, name=name, grid=(nk, S // tm),
        in_specs=[pl.BlockSpec((tm, D), lambda k, m: (m, 0)), pl.BlockSpec((None, D, wd), lambda k, m: (k, 0, 0))],
        out_specs=pl.BlockSpec((tm, wd), lambda k, m: (m, k)),
        out_shape=jax.ShapeDtypeStruct((S, nk * wd), F32),
        compiler_params=_params(("parallel", "parallel")),
    )(n, w)


def _ffn_dact(df, wd, ga, up, name):
    S, D = df.shape
    nk, w, _ = wd.shape
    tm = _row_tile(S, 512)

    def body(df_ref, wd_ref, ga_ref, up_ref, dga_ref, dup_ref):
        dact = lax.dot_general(df_ref[...], wd_ref[...], NT, preferred_element_type=F32)
        ga_ = ga_ref[...].astype(F32)
        up_ = up_ref[...].astype(F32)
        sig = _sigmoid(ga_)
        dga_ref[...] = (dact * up_ * (sig * (1.0 + ga_ * (1.0 - sig)))).astype(BF16)
        dup_ref[...] = (dact * (ga_ * sig)).astype(BF16)

    cspec = pl.BlockSpec((None, tm, w), lambda k, m: (k, m, 0))
    out = jax.ShapeDtypeStruct((nk, S, w), BF16)
    return pl.pallas_call(
        body, name=name, grid=(nk, S // tm),
        in_specs=[pl.BlockSpec((tm, D), lambda k, m: (m, 0)), pl.BlockSpec((None, w, D), lambda k, m: (k, 0, 0)),
                  cspec, cspec],
        out_specs=[cspec, cspec], out_shape=[out, out],
        compiler_params=_params(("parallel", "parallel")),
    )(df, wd, ga, up)


def _mm_nt(d, w, name):
    S, K = d.shape
    N = w.shape[0]
    tm = _row_tile(S, 512)

    def body(d_ref, w_ref, o_ref):
        o_ref[...] = lax.dot_general(d_ref[...], w_ref[...], NT, preferred_element_type=F32)

    return pl.pallas_call(
        body, name=name, grid=(S // tm,),
        in_specs=[pl.BlockSpec((tm, K), lambda m: (m, 0)), pl.BlockSpec((N, K), lambda m: (0, 0))],
        out_specs=pl.BlockSpec((tm, N), lambda m: (m, 0)),
        out_shape=jax.ShapeDtypeStruct((S, N), F32),
        compiler_params=_params(("parallel",)),
    )(d, w)


def _mm_nt_reduce(lhs_list, w_list, chunked3d, name):
    nk, D, kc = w_list[0].shape
    S = lhs_list[0].shape[1] if chunked3d else lhs_list[0].shape[0]
    tm = _row_tile(S, 512)
    npair = len(lhs_list)

    def body(*refs):
        l_refs, w_refs = refs[:npair], refs[npair:2 * npair]
        o_ref, acc_ref = refs[2 * npair], refs[2 * npair + 1]
        k = pl.program_id(1)

        @pl.when(k == 0)
        def _():
            acc_ref[...] = jnp.zeros_like(acc_ref)

        for l_ref, w_ref in zip(l_refs, w_refs):
            acc_ref[...] += lax.dot_general(l_ref[...], w_ref[...], NT, preferred_element_type=F32)

        @pl.when(k == nk - 1)
        def _():
            o_ref[...] = acc_ref[...]

    if chunked3d:
        lspec = pl.BlockSpec((None, tm, kc), lambda m, k: (k, m, 0))
    else:
        lspec = pl.BlockSpec((tm, kc), lambda m, k: (m, k))
    wspec = pl.BlockSpec((None, D, kc), lambda m, k: (k, 0, 0))
    return pl.pallas_call(
        body, name=name, grid=(S // tm, nk),
        in_specs=[lspec] * npair + [wspec] * npair,
        out_specs=pl.BlockSpec((tm, D), lambda m, k: (m, 0)),
        out_shape=jax.ShapeDtypeStruct((S, D), F32),
        scratch_shapes=[pltpu.VMEM((tm, D), F32)],
        compiler_params=_params(("parallel", "arbitrary")),
    )(*lhs_list, *w_list)


def _wgrad_chunk_lhs(lhs, rhs, name):
    nk, S, w = lhs.shape
    D = rhs.shape[1]
    ts = _row_tile(S, 512)
    ns = S // ts

    def body(l_ref, r_ref, o_ref, acc_ref):
        s = pl.program_id(1)

        @pl.when(s == 0)
        def _():
            acc_ref[...] = jnp.zeros_like(acc_ref)

        acc_ref[...] += lax.dot_general(l_ref[...], r_ref[...], TN, preferred_element_type=F32)

        @pl.when(s == ns - 1)
        def _():
            o_ref[...] = acc_ref[...]

    return pl.pallas_call(
        body, name=name, grid=(nk, ns),
        in_specs=[pl.BlockSpec((None, ts, w), lambda k, s: (k, s, 0)), pl.BlockSpec((ts, D), lambda k, s: (s, 0))],
        out_specs=pl.BlockSpec((None, w, D), lambda k, s: (k, 0, 0)),
        out_shape=jax.ShapeDtypeStruct((nk, w, D), F32),
        scratch_shapes=[pltpu.VMEM((w, D), F32)],
        compiler_params=_params(("parallel", "arbitrary")),
    )(lhs, rhs)


def _wgrad_chunk_rhs(lhs, rhs_list, nk, chunked3d, name):
    S, D = lhs.shape
    w = rhs_list[0].shape[2] if chunked3d else rhs_list[0].shape[1] // nk
    ts = _row_tile(S, 512)
    ns = S // ts
    nr = len(rhs_list)

    def body(*refs):
        l_ref, r_refs = refs[0], refs[1:1 + nr]
        o_refs, acc_refs = refs[1 + nr:1 + 2 * nr], refs[1 + 2 * nr:]
        s = pl.program_id(1)

        @pl.when(s == 0)
        def _():
            for acc_ref in acc_refs:
                acc_ref[...] = jnp.zeros_like(acc_ref)

        x = l_ref[...]
        for r_ref, acc_ref in zip(r_refs, acc_refs):
            acc_ref[...] += lax.dot_general(x, r_ref[...], TN, preferred_element_type=F32)

        @pl.when(s == ns - 1)
        def _():
            for o_ref, acc_ref in zip(o_refs, acc_refs):
                o_ref[...] = acc_ref[...]

    if chunked3d:
        rspec = pl.BlockSpec((None, ts, w), lambda k, s: (k, s, 0))
    else:
        rspec = pl.BlockSpec((ts, w), lambda k, s: (s, k))
    ospec = pl.BlockSpec((None, D, w), lambda k, s: (k, 0, 0))
    return pl.pallas_call(
        body, name=name, grid=(nk, ns),
        in_specs=[pl.BlockSpec((ts, D), lambda k, s: (s, 0))] + [rspec] * nr,
        out_specs=[ospec] * nr,
        out_shape=[jax.ShapeDtypeStruct((nk, D, w), F32)] * nr,
        scratch_shapes=[pltpu.VMEM((D, w), F32)] * nr,
        compiler_params=_params(("parallel", "arbitrary")),
    )(lhs, *rhs_list)


def _rope_tables(S):
    pos = jnp.arange(S, dtype=F32)
    inv_freq = ROPE_THETA ** (-jnp.arange(0, HEAD_DIM, 2, dtype=F32) / HEAD_DIM)
    ang = pos[:, None] * inv_freq[None, :]
    cos, sin = jnp.cos(ang), jnp.sin(ang)
    cos2 = jnp.concatenate([cos, cos, cos, cos], axis=1)
    sin2 = jnp.concatenate([-sin, sin, -sin, sin], axis=1)
    return cos2, sin2


def _rotate(t, cos, sin_signed):
    half = HEAD_DIM // 2
    lane = lax.broadcasted_iota(jnp.int32, t.shape, 1)
    first = (lane % HEAD_DIM) < half
    partner = jnp.where(first, pltpu.roll(t, 128 - half, 1), pltpu.roll(t, half, 1))
    return t * cos + partner * sin_signed


def _qkv_rope(proj, cos, sin, name):
    S = proj.shape[0]
    A = 512
    tr = _row_tile(S, 512)
    nb = A // 128
    scale = HEAD_DIM ** -0.5

    def body(q_ref, k_ref, v_ref, c_ref, s_ref, qo_ref, ko_ref, vo_ref):
        c, s = c_ref[...], s_ref[...]
        qo_ref[...] = (_rotate(q_ref[...], c, s) * scale).astype(BF16)
        ko_ref[...] = _rotate(k_ref[...], c, s).astype(BF16)
        vo_ref[...] = v_ref[...].astype(BF16)

    def col(off):
        return pl.BlockSpec((tr, 128), lambda i, j: (i, off + j))

    tab = pl.BlockSpec((tr, 128), lambda i, j: (i, 0))
    out = jax.ShapeDtypeStruct((S, A), BF16)
    return pl.pallas_call(
        body, name=name, grid=(S // tr, nb),
        in_specs=[col(0), col(nb), col(2 * nb), tab, tab],
        out_specs=[col(0), col(0), col(0)], out_shape=[out, out, out],
        compiler_params=_params(("parallel", "parallel")),
    )(proj, proj, proj, cos, sin)


def _rope_bwd(parts, cos, sin, scale, rotate, name):
    S, A = parts[0].shape
    tr = _row_tile(S, 512)

    def body(a_ref, b_ref, c_ref, cos_ref, sin_ref, o_ref):
        t = (a_ref[...] + b_ref[...]) + c_ref[...]
        if rotate:
            t = _rotate(t, cos_ref[...], -sin_ref[...])
        o_ref[...] = (t * scale).astype(BF16)

    col = pl.BlockSpec((tr, 128), lambda i, j: (i, j))
    tab = pl.BlockSpec((tr, 128), lambda i, j: (i, 0))
    return pl.pallas_call(
        body, name=name, grid=(S // tr, A // 128),
        in_specs=[col, col, col, tab, tab], out_specs=col,
        out_shape=jax.ShapeDtypeStruct((S, A), BF16),
        compiler_params=_params(("parallel", "parallel")),
    )(*parts, cos, sin)


def _band_masks(T, has_prev):
    qi = lax.broadcasted_iota(jnp.int32, (T, T), 0)
    kj = lax.broadcasted_iota(jnp.int32, (T, T), 1)
    return qi >= kj, (kj >= qi) & has_prev


def _attn_fwd(q, k, v, dilation, name):
    S, A = q.shape
    L = S // dilation
    cols = dilation * A
    T = min(ATTN_BLOCK, L)
    nb = L // T
    assert L % T == 0 and (T == ATTN_BLOCK or nb == 1)
    q, k, v = (t.reshape(L, cols) for t in (q, k, v))

    def body(q_ref, k_ref, v_ref, o_ref, lse_ref):
        lane = lax.broadcasted_iota(jnp.int32, (1, 128), 1)
        head0 = lane < HEAD_DIM

        def step(n, carry):
            rows = pl.ds(pl.multiple_of(n * T, T), T)
            prev = pl.ds(pl.multiple_of(jnp.maximum(n - 1, 0) * T, T), T)
            qb = q_ref[rows, :]
            kc, vc = k_ref[rows, :], v_ref[rows, :]
            kp, vp = k_ref[prev, :], v_ref[prev, :]
            valid_d, valid_o = _band_masks(T, n > 0)
            outs, lses = [], []
            for hmask in (head0, jnp.logical_not(head0)):
                qh = jnp.where(hmask, qb, jnp.zeros_like(qb))
                sd = jnp.where(valid_d, lax.dot_general(qh, kc, NT, preferred_element_type=F32), NEG)
                so = jnp.where(valid_o, lax.dot_general(qh, kp, NT, preferred_element_type=F32), NEG)
                m = jnp.maximum(jnp.max(sd, axis=-1, keepdims=True), jnp.max(so, axis=-1, keepdims=True))
                pd = jnp.exp(sd - m)
                po = jnp.exp(so - m)
                den = jnp.sum(pd, axis=-1, keepdims=True) + jnp.sum(po, axis=-1, keepdims=True)
                acc = jnp.dot(pd.astype(BF16), vc, preferred_element_type=F32)
                acc += jnp.dot(po.astype(BF16), vp, preferred_element_type=F32)
                outs.append(acc / den)
                lses.append(m + jnp.log(den))
            o_ref[rows, :] = jnp.where(head0, outs[0], outs[1])
            lse_ref[rows, :] = jnp.where(head0, lses[0], lses[1])
            return carry

        lax.fori_loop(0, nb, step, 0)

    blk = pl.BlockSpec((L, 128), lambda j: (0, j))
    out = jax.ShapeDtypeStruct((L, cols), F32)
    o, lse = pl.pallas_call(
        body, name=name, grid=(cols // 128,),
        in_specs=[blk, blk, blk], out_specs=[blk, blk], out_shape=[out, out],
        compiler_params=_params(("parallel",)),
    )(q, k, v)
    return o.reshape(S, A), lse.reshape(S, A)


def _attn_merge(outs, lses, gain, name):
    S, A = outs[0].shape
    tr = _row_tile(S, 256)

    def body(o1, o2, o3, l1, l2, l3, g_ref, attn_ref, lse_ref, y_ref):
        a, b, c = l1[...], l2[...], l3[...]
        m = jnp.maximum(jnp.maximum(a, b), c)
        ea, eb, ec = jnp.exp(a - m), jnp.exp(b - m), jnp.exp(c - m)
        z = (ea + eb) + ec
        attn = ((ea / z) * o1[...] + (eb / z) * o2[...]) + (ec / z) * o3[...]
        attn_ref[...] = attn
        lse_ref[...] = m + jnp.log(z)
        r = lax.rsqrt(jnp.mean(attn * attn, axis=-1, keepdims=True) + RMS_EPS)
        y_ref[...] = ((attn * r) * g_ref[...]).astype(BF16)

    row = pl.BlockSpec((tr, A), lambda i: (i, 0))
    return pl.pallas_call(
        body, name=name, grid=(S // tr,),
        in_specs=[row] * 6 + [_vec_spec(A, 1)], out_specs=[row, row, row],
        out_shape=[jax.ShapeDtypeStruct((S, A), F32), jax.ShapeDtypeStruct((S, A), F32),
                   jax.ShapeDtypeStruct((S, A), BF16)],
        compiler_params=_params(("parallel",)),
    )(*outs, *lses, gain)


def _attn_merge_bwd(dy, attn, gain, name):
    S, A = attn.shape
    tr = _row_tile(S, 256)

    def body(dy_ref, a_ref, g_ref, da_ref, dl_ref, dg_ref):
        @pl.when(pl.program_id(0) == 0)
        def _():
            dg_ref[...] = jnp.zeros_like(dg_ref)

        x = a_ref[...]
        dy_ = dy_ref[...]
        r = lax.rsqrt(jnp.mean(x * x, axis=-1, keepdims=True) + RMS_EPS)
        xh = x * r
        dg_ref[...] += jnp.sum(dy_ * xh, axis=0, keepdims=True)
        dxh = dy_ * g_ref[...]
        dx = r * (dxh - xh * jnp.mean(dxh * xh, axis=-1, keepdims=True))
        da_ref[...] = dx.astype(BF16)
        prod = dx * x
        hi = lax.broadcasted_iota(jnp.int32, (A, A), 0) // HEAD_DIM
        hj = lax.broadcasted_iota(jnp.int32, (A, A), 1) // HEAD_DIM
        same_head = (hi == hj).astype(F32)
        dl_ref[...] = jnp.dot(prod, same_head, preferred_element_type=F32, precision=lax.Precision.HIGHEST)

    row = pl.BlockSpec((tr, A), lambda i: (i, 0))
    vec = _vec_spec(A, 1)
    return pl.pallas_call(
        body, name=name, grid=(S // tr,),
        in_specs=[row, row, vec], out_specs=[row, row, vec],
        out_shape=[jax.ShapeDtypeStruct((S, A), BF16), jax.ShapeDtypeStruct((S, A), F32),
                   jax.ShapeDtypeStruct((1, A), F32)],
        compiler_params=_params(("arbitrary",)),
    )(dy, attn, gain)


def _attn_bwd(q, k, v, da, lse, delta, dilation, name):
    S, A = q.shape
    L = S // dilation
    cols = dilation * A
    T = min(ATTN_BLOCK, L)
    nb = L // T
    assert L % T == 0 and (T == ATTN_BLOCK or nb == 1)
    q, k, v, da, lse, delta = (t.reshape(L, cols) for t in (q, k, v, da, lse, delta))

    def body(q_ref, k_ref, v_ref, da_ref, lse_ref, dl_ref, dq_ref, dk_ref, dv_ref):
        lane = lax.broadcasted_iota(jnp.int32, (1, 128), 1)
        head0 = lane < HEAD_DIM
        dk_ref[...] = jnp.zeros_like(dk_ref)
        dv_ref[...] = jnp.zeros_like(dv_ref)

        def step(n, carry):
            rows = pl.ds(pl.multiple_of(n * T, T), T)
            prev = pl.ds(pl.multiple_of(jnp.maximum(n - 1, 0) * T, T), T)
            qb, dab = q_ref[rows, :], da_ref[rows, :]
            lse_b, dl_b = lse_ref[rows, :], dl_ref[rows, :]
            kc, vc = k_ref[rows, :], v_ref[rows, :]
            kp, vp = k_ref[prev, :], v_ref[prev, :]
            valid_d, valid_o = _band_masks(T, n > 0)
            dqs = []
            dkc = dkp = dvc = dvp = None
            for h, hmask in enumerate((head0, jnp.logical_not(head0))):
                qh = jnp.where(hmask, qb, jnp.zeros_like(qb))
                dah = jnp.where(hmask, dab, jnp.zeros_like(dab))
                c0 = h * HEAD_DIM
                lse_h = lse_b[:, c0:c0 + 1]
                dl_h = dl_b[:, c0:c0 + 1]
                sd = lax.dot_general(qh, kc, NT, preferred_element_type=F32)
                so = lax.dot_general(qh, kp, NT, preferred_element_type=F32)
                pd = jnp.where(valid_d, jnp.exp(sd - lse_h), 0.0)
                po = jnp.where(valid_o, jnp.exp(so - lse_h), 0.0)
                dpd = lax.dot_general(dah, vc, NT, preferred_element_type=F32)
                dpo = lax.dot_general(dah, vp, NT, preferred_element_type=F32)
                dsd = (pd * (dpd - dl_h)).astype(BF16)
                dso = (po * (dpo - dl_h)).astype(BF16)
                pdb, pob = pd.astype(BF16), po.astype(BF16)
                dqs.append(jnp.dot(dsd, kc, preferred_element_type=F32) + jnp.dot(dso, kp, preferred_element_type=F32))
                t_kc = lax.dot_general(dsd, qh, TN, preferred_element_type=F32)
                t_kp = lax.dot_general(dso, qh, TN, preferred_element_type=F32)
                t_vc = lax.dot_general(pdb, dah, TN, preferred_element_type=F32)
                t_vp = lax.dot_general(pob, dah, TN, preferred_element_type=F32)
                if h == 0:
                    dkc, dkp, dvc, dvp = t_kc, t_kp, t_vc, t_vp
                else:
                    dkc, dkp, dvc, dvp = dkc + t_kc, dkp + t_kp, dvc + t_vc, dvp + t_vp
            dq_ref[rows, :] = jnp.where(head0, dqs[0], dqs[1])
            dk_ref[rows, :] += dkc
            dv_ref[rows, :] += dvc
            dk_ref[prev, :] += dkp
            dv_ref[prev, :] += dvp
            return carry

        lax.fori_loop(0, nb, step, 0)

    blk = pl.BlockSpec((L, 128), lambda j: (0, j))
    out = jax.ShapeDtypeStruct((L, cols), F32)
    dq, dk, dv = pl.pallas_call(
        body, name=name, grid=(cols // 128,),
        in_specs=[blk] * 6, out_specs=[blk] * 3, out_shape=[out] * 3,
        compiler_params=_params(("parallel",)),
    )(q, k, v, da, lse, delta)
    return dq.reshape(S, A), dk.reshape(S, A), dv.reshape(S, A)


def _glu_window(a_ref, b_ref, ah_ref, bh_ref, first):
    u0 = a_ref[...] * _sigmoid(b_ref[...])
    u0h = ah_ref[...] * _sigmoid(bh_ref[...])
    u0h = jnp.where(first, jnp.zeros_like(u0h), u0h)
    return jnp.concatenate([u0h, u0], axis=0)


def _conv_norms(u1, lng, lnb):
    mu = jnp.mean(u1, axis=-1, keepdims=True)
    xc = u1 - mu
    rstd = lax.rsqrt(jnp.mean(xc * xc, axis=-1, keepdims=True) + LN_EPS)
    u1h = xc * rstd
    u2 = u1h * lng + lnb
    sig = _sigmoid(u2)
    u3 = u2 * sig
    r = lax.rsqrt(jnp.mean(u3 * u3, axis=-1, keepdims=True) + RMS_EPS)
    return rstd, u1h, u2, sig, u3, r


def _conv_specs(tr, C, col_a, col_b):
    per = tr // HALO

    def tile(col):
        return pl.BlockSpec((tr, C), lambda i: (i, col))

    def halo(col):
        return pl.BlockSpec((HALO, C), lambda i: (jnp.maximum(i * per - 1, 0), col))

    return tile(col_a), tile(col_b), halo(col_a), halo(col_b)


def _conv_fwd(proj, cw, cb, lng, lnb, gco, name):
    S = proj.shape[0]
    C = 512
    tr = _row_tile(S, 256)

    def body(a_ref, b_ref, ah_ref, bh_ref, w_ref, cb_ref, lng_ref, lnb_ref, gco_ref, u1_ref, y_ref):
        win = _glu_window(a_ref, b_ref, ah_ref, bh_ref, pl.program_id(0) == 0)
        acc = jnp.broadcast_to(cb_ref[...], (tr, C))
        for j in range(CONV_KERNEL):
            off = HALO - (CONV_KERNEL - 1) + j
            acc = acc + w_ref[j:j + 1, :] * win[off:off + tr, :]
        u1_ref[...] = acc
        _, _, _, _, u3, r = _conv_norms(acc, lng_ref[...], lnb_ref[...])
        y_ref[...] = ((u3 * r) * gco_ref[...]).astype(BF16)

    ta, tb, ha, hb = _conv_specs(tr, C, 3, 4)
    row = pl.BlockSpec((tr, C), lambda i: (i, 0))
    vec = _vec_spec(C, 1)
    return pl.pallas_call(
        body, name=name, grid=(S // tr,),
        in_specs=[ta, tb, ha, hb, pl.BlockSpec((HALO, C), lambda i: (0, 0)), vec, vec, vec, vec],
        out_specs=[row, row],
        out_shape=[jax.ShapeDtypeStruct((S, C), F32), jax.ShapeDtypeStruct((S, C), BF16)],
        compiler_params=_params(("parallel",)),
    )(proj, proj, proj, proj, cw, cb, lng, lnb, gco)


def _conv_bwd_norms(dy, u1, lng, lnb, gco, name):
    S, C = u1.shape
    tr = _row_tile(S, 256)

    def body(dy_ref, u1_ref, lng_ref, lnb_ref, gco_ref, du1_ref, dgco_ref, dlng_ref, dlnb_ref, dcb_ref):
        @pl.when(pl.program_id(0) == 0)
        def _():
            for ref in (dgco_ref, dlng_ref, dlnb_ref, dcb_ref):
                ref[...] = jnp.zeros_like(ref)

        lng = lng_ref[...]
        rstd, u1h, u2, sig, u3, r = _conv_norms(u1_ref[...], lng, lnb_ref[...])
        dy_ = dy_ref[...]
        u3h = u3 * r
        dgco_ref[...] += jnp.sum(dy_ * u3h, axis=0, keepdims=True)
        du3h = dy_ * gco_ref[...]
        du3 = r * (du3h - u3h * jnp.mean(du3h * u3h, axis=-1, keepdims=True))
        du2 = du3 * (sig * (1.0 + u2 * (1.0 - sig)))
        dlng_ref[...] += jnp.sum(du2 * u1h, axis=0, keepdims=True)
        dlnb_ref[...] += jnp.sum(du2, axis=0, keepdims=True)
        du1h = du2 * lng
        du1 = rstd * (du1h - jnp.mean(du1h, axis=-1, keepdims=True)
                      - u1h * jnp.mean(du1h * u1h, axis=-1, keepdims=True))
        du1_ref[...] = du1
        dcb_ref[...] += jnp.sum(du1, axis=0, keepdims=True)

    row = pl.BlockSpec((tr, C), lambda i: (i, 0))
    vec = _vec_spec(C, 1)
    return pl.pallas_call(
        body, name=name, grid=(S // tr,),
        in_specs=[pl.BlockSpec((tr, C), lambda i: (i, 1)), row, vec, vec, vec],
        out_specs=[row, vec, vec, vec, vec],
        out_shape=[jax.ShapeDtypeStruct((S, C), F32)] + [jax.ShapeDtypeStruct((1, C), F32)] * 4,
        compiler_params=_params(("arbitrary",)),
    )(dy, u1, lng, lnb, gco)


def _conv_bwd_taps(du1, proj, cw, name):
    S, C = du1.shape
    tr = _row_tile(S, 256)
    nt = S // tr
    per = tr // HALO

    def body(du_ref, dun_ref, a_ref, b_ref, ah_ref, bh_ref, w_ref, da_ref, db_ref, dw_ref):
        i = pl.program_id(0)

        @pl.when(i == 0)
        def _():
            dw_ref[...] = jnp.zeros_like(dw_ref)

        win = _glu_window(a_ref, b_ref, ah_ref, bh_ref, i == 0)
        du = du_ref[...]
        nxt = jnp.where(i == nt - 1, jnp.zeros_like(dun_ref[...]), dun_ref[...])
        dwin = jnp.concatenate([du, nxt], axis=0)
        du0 = jnp.zeros((tr, C), F32)
        taps = []
        for j in range(CONV_KERNEL):
            back = CONV_KERNEL - 1 - j
            du0 = du0 + w_ref[j:j + 1, :] * dwin[back:back + tr, :]
            off = HALO - (CONV_KERNEL - 1) + j
            taps.append(jnp.sum(du * win[off:off + tr, :], axis=0, keepdims=True))
        taps.append(jnp.zeros((HALO - CONV_KERNEL, C), F32))
        dw_ref[...] += jnp.concatenate(taps, axis=0)
        a, sig = a_ref[...], _sigmoid(b_ref[...])
        da_ref[...] = (du0 * sig).astype(BF16)
        db_ref[...] = (du0 * a * sig * (1.0 - sig)).astype(BF16)

    ta, tb, ha, hb = _conv_specs(tr, C, 3, 4)
    row = pl.BlockSpec((tr, C), lambda i: (i, 0))
    nxt = pl.BlockSpec((HALO, C), lambda i: (jnp.minimum((i + 1) * per, S // HALO - 1), 0))
    wspec = pl.BlockSpec((HALO, C), lambda i: (0, 0))
    return pl.pallas_call(
        body, name=name, grid=(nt,),
        in_specs=[row, nxt, ta, tb, ha, hb, wspec],
        out_specs=[row, row, wspec],
        out_shape=[jax.ShapeDtypeStruct((S, C), BF16), jax.ShapeDtypeStruct((S, C), BF16),
                   jax.ShapeDtypeStruct((HALO, C), F32)],
        compiler_params=_params(("arbitrary",)),
    )(du1, du1, proj, proj, proj, proj, cw)


def _ada_fwd(c_all, w, b, name):
    B, D = c_all.shape
    N = w.shape[1]
    tn = 768 if N % 768 == 0 else N

    def body(c_ref, w_ref, b_ref, o_ref):
        c = c_ref[...]
        a = (c * _sigmoid(c)).astype(BF16)
        o_ref[...] = jnp.dot(a, w_ref[...].astype(BF16), preferred_element_type=F32) + b_ref[...]

    return pl.pallas_call(
        body, name=name, grid=(N // tn,),
        in_specs=[pl.BlockSpec((B, D), lambda j: (0, 0)), pl.BlockSpec((D, tn), lambda j: (0, j)),
                  pl.BlockSpec((1, tn), lambda j: (0, j))],
        out_specs=pl.BlockSpec((B, tn), lambda j: (0, j)),
        out_shape=jax.ShapeDtypeStruct((B, N), F32),
        compiler_params=_params(("parallel",)),
    )(c_all, w, b)


def _ada_wgrad(c_t, dmod, name):
    D, B = c_t.shape
    N = dmod.shape[1]
    tn = 768 if N % 768 == 0 else N

    def body(c_ref, d_ref, o_ref):
        c = c_ref[...]
        a = (c * _sigmoid(c)).astype(BF16)
        o_ref[...] = jnp.dot(a, d_ref[...].astype(BF16), preferred_element_type=F32)

    return pl.pallas_call(
        body, name=name, grid=(N // tn,),
        in_specs=[pl.BlockSpec((D, B), lambda j: (0, 0)), pl.BlockSpec((B, tn), lambda j: (0, j))],
        out_specs=pl.BlockSpec((D, tn), lambda j: (0, j)),
        out_shape=jax.ShapeDtypeStruct((D, N), F32),
        compiler_params=_params(("parallel",)),
    )(c_t, dmod)


def _sum_rows(x, name):
    R, N = x.shape

    def body(x_ref, o_ref):
        acc = x_ref[0:1, :]
        for r in range(1, R):
            acc = acc + x_ref[r:r + 1, :]
        o_ref[...] = acc

    return pl.pallas_call(
        body, name=name, out_shape=jax.ShapeDtypeStruct((1, N), F32),
        compiler_params=_params(),
    )(x)


def _sum_slots(x, name):
    n, R, C = x.shape
    tr = _row_tile(R, 128) if R % 128 == 0 else R

    def body(x_ref, o_ref):
        acc = x_ref[0]
        for s in range(1, n):
            acc = acc + x_ref[s]
        o_ref[...] = acc

    return pl.pallas_call(
        body, name=name, grid=(R // tr,),
        in_specs=[pl.BlockSpec((n, tr, C), lambda i: (0, i, 0))],
        out_specs=pl.BlockSpec((tr, C), lambda i: (i, 0)),
        out_shape=jax.ShapeDtypeStruct((R, C), F32),
        compiler_params=_params(("parallel",)),
    )(x)


def _add_halves(g, recv, c_idx, name):
    nk, _, R, C = g.shape
    tr = _row_tile(R, 128) if R % 128 == 0 else R

    def body(c_ref, g_ref, r_ref, o_ref):
        o_ref[...] = g_ref[...] + r_ref[...]

    grid_spec = pltpu.PrefetchScalarGridSpec(
        num_scalar_prefetch=1, grid=(nk, R // tr),
        in_specs=[pl.BlockSpec((None, None, tr, C), lambda k, i, c: (k, c[0], i, 0)),
                  pl.BlockSpec((None, tr, C), lambda k, i, c: (k, i, 0))],
        out_specs=pl.BlockSpec((None, tr, C), lambda k, i, c: (k, i, 0)),
    )
    return pl.pallas_call(
        body, name=name, grid_spec=grid_spec,
        out_shape=jax.ShapeDtypeStruct((nk, R, C), F32),
        compiler_params=_params(("parallel", "parallel")),
    )(c_idx, g, recv)


def _adamw(w, g, m, v, name):
    R, C = w.shape
    tr = _row_tile(R, 256) if R % 256 == 0 else R
    bc1 = 1.0 - ADAM_B1 ** ADAM_STEP
    bc2 = 1.0 - ADAM_B2 ** ADAM_STEP

    def body(w_ref, g_ref, m_ref, v_ref, d_ref, mo_ref, vo_ref):
        g_ = g_ref[...]
        m_ = ADAM_B1 * m_ref[...] + (1.0 - ADAM_B1) * g_
        v_ = ADAM_B2 * v_ref[...] + (1.0 - ADAM_B2) * (g_ * g_)
        mo_ref[...] = m_
        vo_ref[...] = v_
        d_ref[...] = -ADAM_LR * ((m_ / bc1) / (jnp.sqrt(v_ / bc2) + ADAM_EPS) + ADAM_WD * w_ref[...])

    row = pl.BlockSpec((tr, C), lambda i: (i, 0))
    out = jax.ShapeDtypeStruct((R, C), F32)
    return pl.pallas_call(
        body, name=name, grid=(R // tr,),
        in_specs=[row] * 4, out_specs=[row] * 3, out_shape=[out] * 3,
        compiler_params=_params(("parallel",)),
    )(w, g, m, v)


def _coords():
    return lax.axis_index("x"), lax.axis_index("y"), lax.axis_index("c")


def _all_gather8(x, name):
    R, N = x.shape
    assert R == 8
    flips = [(fx, fy, fc) for fx in (0, 1) for fy in (0, 1) for fc in (0, 1)][1:]

    def body(x_ref, o_ref, send_sems, recv_sems):
        mx, my, mc = _coords()
        me = 4 * mx + 2 * my + mc

        def rows(dev):
            return o_ref.at[pl.ds(pl.multiple_of(dev * R, R), R), :]

        o_ref[pl.ds(pl.multiple_of(me * R, R), R), :] = x_ref[...]
        copies = []
        for t, (fx, fy, fc) in enumerate(flips):
            peer = (mx ^ fx, my ^ fy, mc ^ fc)
            copies.append(pltpu.make_async_remote_copy(
                src_ref=x_ref, dst_ref=rows(me), send_sem=send_sems.at[t], recv_sem=recv_sems.at[t],
                device_id=peer, device_id_type=MESH))
        for cp in copies:
            cp.start()
        for t, (fx, fy, fc) in enumerate(flips):
            peer_id = 4 * (mx ^ fx) + 2 * (my ^ fy) + (mc ^ fc)
            pltpu.make_async_remote_copy(
                src_ref=x_ref, dst_ref=rows(peer_id), send_sem=send_sems.at[t], recv_sem=recv_sems.at[t],
                device_id=(mx ^ fx, my ^ fy, mc ^ fc), device_id_type=MESH).wait_recv()
        for cp in copies:
            cp.wait_send()

    return pl.pallas_call(
        body, name=name,
        in_specs=[pl.BlockSpec(memory_space=pltpu.VMEM)],
        out_specs=pl.BlockSpec(memory_space=pltpu.VMEM),
        out_shape=jax.ShapeDtypeStruct((N_DEV * R, N), F32),
        scratch_shapes=[pltpu.SemaphoreType.DMA((7,)), pltpu.SemaphoreType.DMA((7,))],
        compiler_params=pltpu.CompilerParams(has_side_effects=True, vmem_limit_bytes=VMEM_LIMIT_BYTES),
    )(x)


CHIP_FLIPS = ((1, 0), (0, 1), (1, 1))


def _gather_chips(shards, name):
    n = len(shards)

    def body(*refs):
        in_refs, out_refs = refs[:n], refs[n:2 * n]
        send_sems, recv_sems, local_sems = refs[2 * n:]
        mx, my, mc = _coords()
        me = 2 * mx + my
        started = []
        for i in range(n):
            loc = pltpu.make_async_copy(in_refs[i], out_refs[i].at[me], local_sems.at[i])
            loc.start()
            started.append(loc)
        sends = []
        for i in range(n):
            for t, (fx, fy) in enumerate(CHIP_FLIPS):
                cp = pltpu.make_async_remote_copy(
                    src_ref=in_refs[i], dst_ref=out_refs[i].at[me],
                    send_sem=send_sems.at[i, t], recv_sem=recv_sems.at[i, t],
                    device_id=(mx ^ fx, my ^ fy, mc), device_id_type=MESH)
                cp.start()
                sends.append(cp)
        for i in range(n):
            for t, (fx, fy) in enumerate(CHIP_FLIPS):
                src_chip = 2 * (mx ^ fx) + (my ^ fy)
                pltpu.make_async_remote_copy(
                    src_ref=in_refs[i], dst_ref=out_refs[i].at[src_chip],
                    send_sem=send_sems.at[i, t], recv_sem=recv_sems.at[i, t],
                    device_id=(mx ^ fx, my ^ fy, mc), device_id_type=MESH).wait_recv()
        for cp in sends:
            cp.wait_send()
        for loc in started:
            loc.wait()

    any_spec = pl.BlockSpec(memory_space=pl.ANY)
    return pl.pallas_call(
        body, name=name,
        in_specs=[any_spec] * n, out_specs=[any_spec] * n,
        out_shape=[jax.ShapeDtypeStruct((N_CHIPS,) + s.shape, s.dtype) for s in shards],
        scratch_shapes=[pltpu.SemaphoreType.DMA((n, 3)), pltpu.SemaphoreType.DMA((n, 3)),
                        pltpu.SemaphoreType.DMA((n,))],
        compiler_params=pltpu.CompilerParams(has_side_effects=True),
    )(*shards)


def _send_other_halves(grads, name):
    n = len(grads)

    def body(*refs):
        in_refs, out_refs = refs[:n], refs[n:2 * n]
        send_sems, recv_sems = refs[2 * n:]
        mx, my, mc = _coords()
        sends = []
        for i in range(n):
            cp = pltpu.make_async_remote_copy(
                src_ref=in_refs[i].at[:, 1 - mc], dst_ref=out_refs[i],
                send_sem=send_sems.at[i], recv_sem=recv_sems.at[i],
                device_id=(mx, my, 1 - mc), device_id_type=MESH)
            cp.start()
            sends.append(cp)
        for cp in sends:
            cp.wait_recv()
        for cp in sends:
            cp.wait_send()

    any_spec = pl.BlockSpec(memory_space=pl.ANY)
    return pl.pallas_call(
        body, name=name,
        in_specs=[any_spec] * n, out_specs=[any_spec] * n,
        out_shape=[jax.ShapeDtypeStruct((g.shape[0],) + g.shape[2:], g.dtype) for g in grads],
        scratch_shapes=[pltpu.SemaphoreType.DMA((n,)), pltpu.SemaphoreType.DMA((n,))],
        compiler_params=pltpu.CompilerParams(has_side_effects=True),
    )(*grads)


def _exchange_chunks(sums, name):
    n = len(sums)

    def body(*refs):
        in_refs, out_refs = refs[:n], refs[n:2 * n]
        send_sems, recv_sems, local_sems = refs[2 * n:]
        mx, my, mc = _coords()
        me = 2 * mx + my
        started = []
        for i in range(n):
            loc = pltpu.make_async_copy(in_refs[i].at[me], out_refs[i].at[me], local_sems.at[i])
            loc.start()
            started.append(loc)
        sends = []
        for i in range(n):
            for t, (fx, fy) in enumerate(CHIP_FLIPS):
                peer_chip = 2 * (mx ^ fx) + (my ^ fy)
                cp = pltpu.make_async_remote_copy(
                    src_ref=in_refs[i].at[peer_chip], dst_ref=out_refs[i].at[me],
                    send_sem=send_sems.at[i, t], recv_sem=recv_sems.at[i, t],
                    device_id=(mx ^ fx, my ^ fy, mc), device_id_type=MESH)
                cp.start()
                sends.append(cp)
        for i in range(n):
            for t, (fx, fy) in enumerate(CHIP_FLIPS):
                peer_chip = 2 * (mx ^ fx) + (my ^ fy)
                pltpu.make_async_remote_copy(
                    src_ref=in_refs[i].at[me], dst_ref=out_refs[i].at[peer_chip],
                    send_sem=send_sems.at[i, t], recv_sem=recv_sems.at[i, t],
                    device_id=(mx ^ fx, my ^ fy, mc), device_id_type=MESH).wait_recv()
        for cp in sends:
            cp.wait_send()
        for loc in started:
            loc.wait()

    any_spec = pl.BlockSpec(memory_space=pl.ANY)
    return pl.pallas_call(
        body, name=name,
        in_specs=[any_spec] * n, out_specs=[any_spec] * n,
        out_shape=[jax.ShapeDtypeStruct(s.shape, s.dtype) for s in sums],
        scratch_shapes=[pltpu.SemaphoreType.DMA((n, 3)), pltpu.SemaphoreType.DMA((n, 3)),
                        pltpu.SemaphoreType.DMA((n,))],
        compiler_params=pltpu.CompilerParams(has_side_effects=True),
    )(*sums)


def _share_halves(halves, name):
    n = len(halves)

    def body(*refs):
        in_refs, out_refs = refs[:n], refs[n:2 * n]
        send_sems, recv_sems, local_sems = refs[2 * n:]
        mx, my, mc = _coords()
        started, sends = [], []
        for i in range(n):
            loc = pltpu.make_async_copy(in_refs[i], out_refs[i].at[mc], local_sems.at[i])
            loc.start()
            started.append(loc)
            cp = pltpu.make_async_remote_copy(
                src_ref=in_refs[i], dst_ref=out_refs[i].at[mc],
                send_sem=send_sems.at[i], recv_sem=recv_sems.at[i],
                device_id=(mx, my, 1 - mc), device_id_type=MESH)
            cp.start()
            sends.append(cp)
        for i in range(n):
            pltpu.make_async_remote_copy(
                src_ref=in_refs[i], dst_ref=out_refs[i].at[1 - mc],
                send_sem=send_sems.at[i], recv_sem=recv_sems.at[i],
                device_id=(mx, my, 1 - mc), device_id_type=MESH).wait_recv()
        for cp in sends:
            cp.wait_send()
        for loc in started:
            loc.wait()

    any_spec = pl.BlockSpec(memory_space=pl.ANY)
    return pl.pallas_call(
        body, name=name,
        in_specs=[any_spec] * n, out_specs=[any_spec] * n,
        out_shape=[jax.ShapeDtypeStruct((2,) + h.shape, h.dtype) for h in halves],
        scratch_shapes=[pltpu.SemaphoreType.DMA((n,)), pltpu.SemaphoreType.DMA((n,)),
                        pltpu.SemaphoreType.DMA((n,))],
        compiler_params=pltpu.CompilerParams(has_side_effects=True),
    )(*halves)


def _reduce_scatter_grads(grads, tag):
    c_idx = lax.axis_index("c").astype(jnp.int32).reshape(1)
    split = [g.reshape(g.shape[0], 2, g.shape[1] // 2, g.shape[2]) for g in grads]
    recv = _send_other_halves(split, f"rs_halves_{tag}")
    chip_sums = [_add_halves(g, r, c_idx, f"rs_add_{tag}_{i}") for i, (g, r) in enumerate(zip(split, recv))]
    slots = _exchange_chunks(chip_sums, f"rs_chunks_{tag}")
    halves = [_sum_slots(s, f"rs_sum_{tag}_{i}") for i, s in enumerate(slots)]
    both = _share_halves(halves, f"rs_share_{tag}")
    return [b.reshape(2 * b.shape[1], b.shape[2]) for b in both]


def _ffn_forward(h, gain, sc, sh, gate, wg, wu, wd, tag):
    n = _norm_mod(h, gain, sc, sh, f"{tag}_norm")
    ga, up, act = _ffn_gate_up(n, wg, wu, f"{tag}_gate_up")
    h_out, f = _mm_residual(act, wd, h, gate, 0.5, f"{tag}_down")
    return h_out, (h, n, ga, up, act, f)


def _ffn_backward(dh_out, saved, gain, sc, gate, wg, wu, wd, tag):
    h, n, ga, up, act, f = saved
    df, d_gate = _gate_bwd(dh_out, f, gate, 0.5, f"{tag}_gate_bwd")
    dga, dup = _ffn_dact(df, wd, ga, up, f"{tag}_dact")
    dwd = _wgrad_chunk_lhs(act, df, f"{tag}_dwd")
    dwg, dwu = _wgrad_chunk_rhs(n, [dga, dup], wg.shape[0], True, f"{tag}_dwgu")
    dn = _mm_nt_reduce([dga, dup], [wg, wu], True, f"{tag}_dn")
    dh_in, d_sh, d_sc, d_gain = _norm_mod_bwd(dn, h, gain, sc, dh_out, f"{tag}_norm_bwd")
    return dh_in, (dwg, dwu, dwd), (d_sh, d_sc, d_gate, d_gain)


def _pad_cols(v, n):
    return jnp.pad(v, ((0, 0), (0, n - v.shape[1])))


def _mixer_forward(h1, mix_norm_g, sc2, sh2, gt2, win, wout, conv_w, conv_dw_b, conv_ln_g, conv_ln_b, attn_out_g,
                   conv_out_g):
    S, D = h1.shape
    n2 = _norm_mod(h1, mix_norm_g, sc2, sh2, "mix_norm")
    proj = _mm_cols(n2, win, "mix_in")
    cos, sin = _rope_tables(S)
    q, k, v = _qkv_rope(proj, cos, sin, "qkv_rope")
    outs, lses = [], []
    for dil in DILATIONS:
        o, lse = _attn_fwd(q, k, v, dil, f"attn_fwd_d{dil}")
        outs.append(o)
        lses.append(lse)
    attn, lse_tot, ya = _attn_merge(outs, lses, attn_out_g, "attn_merge")
    u1, yc = _conv_fwd(proj, conv_w, conv_dw_b, conv_ln_g, conv_ln_b, conv_out_g, "conv_fwd")
    y = jnp.concatenate([ya, yc], axis=1)
    h2, mo = _mm_residual(y[None], wout.reshape(1, D, D), h1, gt2, 1.0, "mix_out")
    return h2, (h1, n2, proj, cos, sin, q, k, v, attn, lse_tot, u1, y, mo)


def _mixer_backward(dh2, saved, mix_norm_g, sc2, gt2, win, wout, conv_w, conv_ln_g, conv_ln_b, attn_out_g,
                    conv_out_g):
    h1, n2, proj, cos, sin, q, k, v, attn, lse_tot, u1, y, mo = saved
    S, D = h1.shape
    dmo, d_gt2 = _gate_bwd(dh2, mo, gt2, 1.0, "mix_gate_bwd")
    dwout = _wgrad_chunk_lhs(y[None], dmo, "mix_dwout")
    dy = _mm_nt(dmo, wout.reshape(D, D), "mix_dy")
    dattn, delta, d_attn_g = _attn_merge_bwd(dy, attn, attn_out_g, "attn_merge_bwd")
    dqs, dks, dvs = [], [], []
    for dil in DILATIONS:
        dq_, dk_, dv_ = _attn_bwd(q, k, v, dattn, lse_tot, delta, dil, f"attn_bwd_d{dil}")
        dqs.append(dq_)
        dks.append(dk_)
        dvs.append(dv_)
    dq = _rope_bwd(dqs, cos, sin, HEAD_DIM ** -0.5, True, "dq_rope")
    dk = _rope_bwd(dks, cos, sin, 1.0, True, "dk_rope")
    dv = _rope_bwd(dvs, cos, sin, 1.0, False, "dv_sum")
    du1, d_gco, d_lng, d_lnb, d_cb = _conv_bwd_norms(dy, u1, conv_ln_g, conv_ln_b, conv_out_g, "conv_bwd_norms")
    dga_, dgb_, d_cw = _conv_bwd_taps(du1, proj, conv_w, "conv_bwd_taps")
    dproj = jnp.concatenate([dq, dk, dv, dga_, dgb_], axis=1)
    (dwin,) = _wgrad_chunk_rhs(n2, [dproj], N_CHIPS, False, "mix_dwin")
    dn2 = _mm_nt_reduce([dproj], [win], False, "mix_dn")
    dh1, d_sh2, d_sc2, d_gain2 = _norm_mod_bwd(dn2, h1, mix_norm_g, sc2, dh2, "mix_norm_bwd")
    small = (d_sh2, d_sc2, d_gt2, d_gain2, d_cb, d_lng, d_lnb, d_attn_g, d_gco, d_cw)
    return dh1, (dwin, dwout), small


def kernel(x, c, w_ada, b_ada, ffn1_norm_g, ffn1_w_gate, ffn1_w_up, ffn1_w_down, mix_norm_g, w_in, conv_dw_w, conv_dw_b, conv_ln_g, conv_ln_b, attn_out_g, conv_out_g, w_out, ffn2_norm_g, ffn2_w_gate, ffn2_w_up, ffn2_w_down, final_norm_g, loss_target, m_w_ada, m_b_ada, m_ffn1_norm_g, m_ffn1_w_gate, m_ffn1_w_up, m_ffn1_w_down, m_mix_norm_g, m_w_in, m_conv_dw_w, m_conv_dw_b, m_conv_ln_g, m_conv_ln_b, m_attn_out_g, m_conv_out_g, m_w_out, m_ffn2_norm_g, m_ffn2_w_gate, m_ffn2_w_up, m_ffn2_w_down, m_final_norm_g, v_w_ada, v_b_ada, v_ffn1_norm_g, v_ffn1_w_gate, v_ffn1_w_up, v_ffn1_w_down, v_mix_norm_g, v_w_in, v_conv_dw_w, v_conv_dw_b, v_conv_ln_g, v_conv_ln_b, v_attn_out_g, v_conv_out_g, v_w_out, v_ffn2_norm_g, v_ffn2_w_gate, v_ffn2_w_up, v_ffn2_w_down, v_final_norm_g):
    S, D = x.shape[1], x.shape[2]
    mx, my, mc = _coords()
    chip = 2 * mx + my
    dev = 4 * mx + 2 * my + mc
    h0 = x[0]
    target = loss_target[0]

    big = [ffn1_w_gate[0], ffn1_w_up[0], ffn1_w_down[0], w_in[0], w_out[0], ffn2_w_gate[0], ffn2_w_up[0],
           ffn2_w_down[0]]
    wg1, wu1, wd1, win, wout, wg2, wu2, wd2 = _gather_chips([w.astype(BF16) for w in big], "gather_weights")

    ncw = CONV_KERNEL * 128
    n0 = -(-(D + ncw) // 1024) * 1024
    pk0 = _pad_cols(jnp.concatenate([c.reshape(1, D), conv_dw_w.reshape(1, ncw)], axis=1), n0)
    g0 = _all_gather8(pk0.reshape(8, n0 // 8), "gather_c").reshape(N_DEV, n0)
    c_all = g0[:, :D]
    conv_w = jnp.concatenate([g0[2 * kc, D:D + ncw].reshape(CONV_KERNEL, 128) for kc in range(N_CHIPS)], axis=1)
    conv_w = jnp.pad(conv_w, ((0, HALO - CONV_KERNEL), (0, 0)))
    nmod = w_ada.shape[2]
    b_shard = lax.dynamic_slice(b_ada, (0, chip * nmod), (1, nmod))
    mod_part = _ada_fwd(c_all, w_ada[0], b_shard, "ada_fwd")
    g1 = _all_gather8(mod_part, "gather_mod")
    mod_all = jnp.concatenate([g1[16 * kc:16 * kc + 8] for kc in range(N_CHIPS)], axis=1)
    mod = lax.dynamic_slice(mod_all, (dev, 0), (1, 9 * D))
    sh1, sc1, gt1, sh2, sc2, gt2, sh3, sc3, gt3 = [mod[:, i * D:(i + 1) * D] for i in range(9)]

    h1, saved1 = _ffn_forward(h0, ffn1_norm_g, sc1, sh1, gt1, wg1, wu1, wd1, "ffn1")
    h2, saved2 = _mixer_forward(h1, mix_norm_g, sc2, sh2, gt2, win, wout, conv_w, conv_dw_b, conv_ln_g, conv_ln_b,
                                attn_out_g, conv_out_g)
    h3, saved3 = _ffn_forward(h2, ffn2_norm_g, sc3, sh3, gt3, wg2, wu2, wd2, "ffn2")
    loss_part, dh3, d_final_g = _loss_head(h3, final_norm_g.reshape(1, D), target, "loss_head")

    dh2, dw_ffn2, (d_sh3, d_sc3, d_gt3, d_gain3) = _ffn_backward(
        dh3, saved3, ffn2_norm_g, sc3, gt3, wg2, wu2, wd2, "ffn2")
    g_wg2, g_wu2, g_wd2 = _reduce_scatter_grads(list(dw_ffn2), "ffn2")
    dh1, (dwin, dwout), small_mix = _mixer_backward(
        dh2, saved2, mix_norm_g, sc2, gt2, win, wout, conv_w, conv_ln_g, conv_ln_b, attn_out_g, conv_out_g)
    d_sh2, d_sc2, d_gt2, d_gain2, d_cb, d_lng, d_lnb, d_attn_g, d_gco, d_cw = small_mix
    g_win, g_wout = _reduce_scatter_grads([dwin, dwout.reshape(N_CHIPS, D // N_CHIPS, D)], "mix")
    dh0, dw_ffn1, (d_sh1, d_sc1, d_gt1, d_gain1) = _ffn_backward(
        dh1, saved1, ffn1_norm_g, sc1, gt1, wg1, wu1, wd1, "ffn1")
    g_wg1, g_wu1, g_wd1 = _reduce_scatter_grads(list(dw_ffn1), "ffn1")

    dmod = jnp.concatenate([d_sh1, d_sc1, d_gt1, d_sh2, d_sc2, d_gt2, d_sh3, d_sc3, d_gt3], axis=1)
    small = [d_gain1, d_gain2, d_gain3, d_final_g, d_cb, d_lng, d_lnb, d_attn_g, d_gco,
             d_cw[:CONV_KERNEL].reshape(1, CONV_KERNEL * 512), loss_part]
    pk1 = jnp.concatenate([dmod] + small, axis=1)
    n1 = -(-pk1.shape[1] // 1024) * 1024
    gathered = _all_gather8(_pad_cols(pk1, n1).reshape(8, n1 // 8), "gather_small").reshape(N_DEV, n1)
    tot = _sum_rows(gathered, "sum_small")
    off = [0]

    def take(nel):
        out = tot[:, off[0]:off[0] + nel]
        off[0] += nel
        return out

    g_b_ada = take(9 * D)
    g_ffn1_norm, g_mix_norm, g_ffn2_norm, g_final = take(D), take(D), take(D), take(D)
    g_cb, g_lng, g_lnb, g_attn_g, g_gco = take(512), take(512), take(512), take(512), take(512)
    g_cw_full = take(CONV_KERNEL * 512).reshape(CONV_KERNEL, 512)
    loss = take(128)[0, 0]
    g_cw = lax.dynamic_slice(g_cw_full, (0, chip * 128), (CONV_KERNEL, 128))

    dmod_shard = lax.dynamic_slice(gathered[:, :9 * D], (0, chip * nmod), (N_DEV, nmod))
    dmod16 = jnp.pad(dmod_shard, ((0, N_DEV), (0, 0)))
    c_t16 = jnp.pad(c_all.T, ((0, 0), (0, N_DEV)))
    g_w_ada = _ada_wgrad(c_t16, dmod16, "ada_wgrad")

    names = ["w_ada", "b_ada", "ffn1_norm_g", "ffn1_w_gate", "ffn1_w_up", "ffn1_w_down", "mix_norm_g", "w_in",
             "conv_dw_w", "conv_dw_b", "conv_ln_g", "conv_ln_b", "attn_out_g", "conv_out_g", "w_out", "ffn2_norm_g",
             "ffn2_w_gate", "ffn2_w_up", "ffn2_w_down", "final_norm_g"]
    weights = [w_ada, b_ada, ffn1_norm_g, ffn1_w_gate, ffn1_w_up, ffn1_w_down, mix_norm_g, w_in, conv_dw_w,
               conv_dw_b, conv_ln_g, conv_ln_b, attn_out_g, conv_out_g, w_out, ffn2_norm_g, ffn2_w_gate, ffn2_w_up,
               ffn2_w_down, final_norm_g]
    ms = [m_w_ada, m_b_ada, m_ffn1_norm_g, m_ffn1_w_gate, m_ffn1_w_up, m_ffn1_w_down, m_mix_norm_g, m_w_in,
          m_conv_dw_w, m_conv_dw_b, m_conv_ln_g, m_conv_ln_b, m_attn_out_g, m_conv_out_g, m_w_out, m_ffn2_norm_g,
          m_ffn2_w_gate, m_ffn2_w_up, m_ffn2_w_down, m_final_norm_g]
    vs = [v_w_ada, v_b_ada, v_ffn1_norm_g, v_ffn1_w_gate, v_ffn1_w_up, v_ffn1_w_down, v_mix_norm_g, v_w_in,
          v_conv_dw_w, v_conv_dw_b, v_conv_ln_g, v_conv_ln_b, v_attn_out_g, v_conv_out_g, v_w_out, v_ffn2_norm_g,
          v_ffn2_w_gate, v_ffn2_w_up, v_ffn2_w_down, v_final_norm_g]
    grads_flat = [g_w_ada, g_b_ada, g_ffn1_norm, g_wg1, g_wu1, g_wd1, g_mix_norm, g_win, g_cw, g_cb, g_lng, g_lnb,
                  g_attn_g, g_gco, g_wout, g_ffn2_norm, g_wg2, g_wu2, g_wd2, g_final]
    big_names = ("w_ada", "ffn1_w_gate", "ffn1_w_up", "ffn1_w_down", "w_in", "w_out", "ffn2_w_gate", "ffn2_w_up",
                 "ffn2_w_down")
    grads, deltas, new_ms, new_vs = {}, {}, {}, {}
    for i, nm in enumerate(names):
        if nm in big_names:
            shape = weights[i].shape
            two_d = (shape[-2], shape[-1])
            g2 = grads_flat[i].reshape(two_d)
            d_, m_, v_ = _adamw(weights[i].reshape(two_d), g2, ms[i].reshape(two_d), vs[i].reshape(two_d),
                                f"adamw_{nm}")
            grads[nm], deltas[nm], new_ms[nm], new_vs[nm] = (t.reshape(shape) for t in (g2, d_, m_, v_))
    small_ix = [i for i, nm in enumerate(names) if nm not in big_names]

    def pack_small(arrs):
        flat = jnp.concatenate([arrs[i].reshape(1, -1) for i in small_ix], axis=1)
        npad = -(-flat.shape[1] // 1024) * 1024
        return _pad_cols(flat, npad).reshape(8, npad // 8)

    d_s, m_s, v_s = _adamw(pack_small(weights), pack_small(grads_flat), pack_small(ms), pack_small(vs),
                           "adamw_small")
    pos = 0
    for i in small_ix:
        nm, shape, nel = names[i], weights[i].shape, weights[i].size
        grads[nm] = grads_flat[i].reshape(shape)
        deltas[nm], new_ms[nm], new_vs[nm] = (t.reshape(1, -1)[:, pos:pos + nel].reshape(shape)
                                              for t in (d_s, m_s, v_s))
        pos += nel

    return (loss, dh0[None], *[grads[nm] for nm in names], *[deltas[nm] for nm in names],
            *[new_ms[nm] for nm in names], *[new_vs[nm] for nm in names])
```

```python
import jax
import jax.numpy as jnp
from jax import lax
from jax.experimental import pallas as pl
from jax.experimental.pallas import tpu as pltpu

F32 = jnp.float32
BF16 = jnp.bfloat16
MESH = pl.DeviceIdType.MESH

RMS_EPS = 1e-6
LN_EPS = 1e-5
HEAD_DIM = 64
ATTN_BLOCK = 128
DILATIONS = (1, 4, 16)
ROPE_THETA = 10000.0
CONV_KERNEL = 31
HALO = 32
N_CHIPS = 4
N_DEV = 8
ADAM_LR, ADAM_B1, ADAM_B2, ADAM_EPS, ADAM_WD, ADAM_STEP = 0.001, 0.9, 0.999, 1e-08, 0.01, 10
VMEM_LIMIT_BYTES = 48 * 1024 * 1024
NEG = -1e30

NT = (((1,), (1,)), ((), ()))
TN = (((0,), (0,)), ((), ()))


def _params(sem=None):
    return pltpu.CompilerParams(dimension_semantics=sem, vmem_limit_bytes=VMEM_LIMIT_BYTES)


def _row_tile(rows, want):
    t = min(rows, want)
    assert rows % t == 0
    return t


def _sigmoid(x):
    return 1.0 / (1.0 + jnp.exp(-x))


def _vec_spec(d, ngrid):
    if ngrid == 1:
        return pl.BlockSpec((1, d), lambda i: (0, 0))
    return pl.BlockSpec((1, d), lambda i, j: (0, 0))


def _norm_mod(h, gain, sc, sh, name):
    S, D = h.shape
    tr = _row_tile(S, 512)

    def body(h_ref, g_ref, sc_ref, sh_ref, n_ref):
        x = h_ref[...]
        r = lax.rsqrt(jnp.mean(x * x, axis=-1, keepdims=True) + RMS_EPS)
        y = (x * r) * g_ref[...]
        n_ref[...] = (y * (1.0 + sc_ref[...]) + sh_ref[...]).astype(BF16)

    row = pl.BlockSpec((tr, D), lambda i: (i, 0))
    return pl.pallas_call(
        body, name=name, grid=(S // tr,),
        in_specs=[row, _vec_spec(D, 1), _vec_spec(D, 1), _vec_spec(D, 1)],
        out_specs=row, out_shape=jax.ShapeDtypeStruct((S, D), BF16),
        compiler_params=_params(("parallel",)),
    )(h, gain, sc, sh)


def _norm_mod_bwd(dn, h_in, gain, sc, dh_out, name):
    S, D = h_in.shape
    tr = _row_tile(S, 512)

    def body(dn_ref, h_ref, g_ref, sc_ref, dho_ref, dh_ref, dsh_ref, dsc_ref, dg_ref):
        @pl.when(pl.program_id(0) == 0)
        def _():
            dsh_ref[...] = jnp.zeros_like(dsh_ref)
            dsc_ref[...] = jnp.zeros_like(dsc_ref)
            dg_ref[...] = jnp.zeros_like(dg_ref)

        x = h_ref[...]
        dn_ = dn_ref[...]
        g = g_ref[...]
        one_sc = 1.0 + sc_ref[...]
        r = lax.rsqrt(jnp.mean(x * x, axis=-1, keepdims=True) + RMS_EPS)
        xh = x * r
        dsh_ref[...] += jnp.sum(dn_, axis=0, keepdims=True)
        dsc_ref[...] += jnp.sum(dn_ * (xh * g), axis=0, keepdims=True)
        dg_ref[...] += jnp.sum(dn_ * one_sc * xh, axis=0, keepdims=True)
        dxh = dn_ * (g * one_sc)
        dh_ref[...] = dho_ref[...] + r * (dxh - xh * jnp.mean(dxh * xh, axis=-1, keepdims=True))

    row = pl.BlockSpec((tr, D), lambda i: (i, 0))
    vec = _vec_spec(D, 1)
    return pl.pallas_call(
        body, name=name, grid=(S // tr,),
        in_specs=[row, row, vec, vec, row],
        out_specs=[row, vec, vec, vec],
        out_shape=[jax.ShapeDtypeStruct((S, D), F32)] + [jax.ShapeDtypeStruct((1, D), F32)] * 3,
        compiler_params=_params(("arbitrary",)),
    )(dn, h_in, gain, sc, dh_out)


def _gate_bwd(dh, f, gvec, coef, name):
    S, D = dh.shape
    tr = _row_tile(S, 512)

    def body(dh_ref, f_ref, g_ref, df_ref, dg_ref):
        @pl.when(pl.program_id(0) == 0)
        def _():
            dg_ref[...] = jnp.zeros_like(dg_ref)

        dh_ = dh_ref[...]
        df_ref[...] = ((coef * g_ref[...]) * dh_).astype(BF16)
        dg_ref[...] += jnp.sum(coef * dh_ * f_ref[...].astype(F32), axis=0, keepdims=True)

    row = pl.BlockSpec((tr, D), lambda i: (i, 0))
    vec = _vec_spec(D, 1)
    return pl.pallas_call(
        body, name=name, grid=(S // tr,),
        in_specs=[row, row, vec], out_specs=[row, vec],
        out_shape=[jax.ShapeDtypeStruct((S, D), BF16), jax.ShapeDtypeStruct((1, D), F32)],
        compiler_params=_params(("arbitrary",)),
    )(dh, f, gvec)


def _loss_head(h, gain, target, name):
    S, D = h.shape
    tr = _row_tile(S, 512)

    def body(h_ref, g_ref, t_ref, loss_ref, dh_ref, dg_ref):
        @pl.when(pl.program_id(0) == 0)
        def _():
            loss_ref[...] = jnp.zeros_like(loss_ref)
            dg_ref[...] = jnp.zeros_like(dg_ref)

        x = h_ref[...]
        g = g_ref[...]
        r = lax.rsqrt(jnp.mean(x * x, axis=-1, keepdims=True) + RMS_EPS)
        xh = x * r
        err = xh * g - t_ref[...]
        part = 0.5 * jnp.sum(jnp.mean(err * err, axis=-1, keepdims=True), axis=0, keepdims=True)
        loss_ref[...] += jnp.broadcast_to(part, loss_ref.shape)
        dy = err * (1.0 / D)
        dg_ref[...] += jnp.sum(dy * xh, axis=0, keepdims=True)
        dxh = dy * g
        dh_ref[...] = r * (dxh - xh * jnp.mean(dxh * xh, axis=-1, keepdims=True))

    row = pl.BlockSpec((tr, D), lambda i: (i, 0))
    vec = _vec_spec(D, 1)
    return pl.pallas_call(
        body, name=name, grid=(S // tr,),
        in_specs=[row, vec, row],
        out_specs=[pl.BlockSpec((1, 128), lambda i: (0, 0)), row, vec],
        out_shape=[jax.ShapeDtypeStruct((1, 128), F32), jax.ShapeDtypeStruct((S, D), F32),
                   jax.ShapeDtypeStruct((1, D), F32)],
        compiler_params=_params(("arbitrary",)),
    )(h, gain, target)


def _ffn_gate_up(n, wg, wu, name):
    S, D = n.shape
    nk, _, w = wg.shape
    tm = _row_tile(S, 512)

    def body(n_ref, wg_ref, wu_ref, ga_ref, up_ref, act_ref):
        x = n_ref[...]
        ga = jnp.dot(x, wg_ref[...], preferred_element_type=F32)
        up = jnp.dot(x, wu_ref[...], preferred_element_type=F32)
        ga_ref[...] = ga.astype(BF16)
        up_ref[...] = up.astype(BF16)
        act_ref[...] = ((ga * _sigmoid(ga)) * up).astype(BF16)

    wspec = pl.BlockSpec((None, D, w), lambda k, m: (k, 0, 0))
    ospec = pl.BlockSpec((None, tm, w), lambda k, m: (k, m, 0))
    out = jax.ShapeDtypeStruct((nk, S, w), BF16)
    return pl.pallas_call(
        body, name=name, grid=(nk, S // tm),
        in_specs=[pl.BlockSpec((tm, D), lambda k, m: (m, 0)), wspec, wspec],
        out_specs=[ospec, ospec, ospec], out_shape=[out, out, out],
        compiler_params=_params(("parallel", "parallel")),
    )(n, wg, wu)


def _mm_residual(lhs, w, h_in, gvec, coef, name):
    nk, S, kc = lhs.shape
    D = w.shape[2]
    tm = _row_tile(S, 512)

    def body(l_ref, w_ref, h_ref, g_ref, ho_ref, f_ref, acc_ref):
        k = pl.program_id(1)

        @pl.when(k == 0)
        def _():
            acc_ref[...] = jnp.zeros_like(acc_ref)

        acc_ref[...] += jnp.dot(l_ref[...], w_ref[...], preferred_element_type=F32)

        @pl.when(k == nk - 1)
        def _():
            f = acc_ref[...]
            f_ref[...] = f.astype(BF16)
            ho_ref[...] = h_ref[...] + (coef * g_ref[...]) * f

    row = pl.BlockSpec((tm, D), lambda m, k: (m, 0))
    return pl.pallas_call(
        body, name=name, grid=(S // tm, nk),
        in_specs=[pl.BlockSpec((None, tm, kc), lambda m, k: (k, m, 0)),
                  pl.BlockSpec((None, kc, D), lambda m, k: (k, 0, 0)), row, _vec_spec(D, 2)],
        out_specs=[row, row],
        out_shape=[jax.ShapeDtypeStruct((S, D), F32), jax.ShapeDtypeStruct((S, D), BF16)],
        scratch_shapes=[pltpu.VMEM((tm, D), F32)],
        compiler_params=_params(("parallel", "arbitrary")),
    )(lhs, w, h_in, gvec)


def _mm_cols(n, w, name):
    S, D = n.shape
    nk, _, wd = w.shape
    assert wd % 128 == 0
    tm = _row_tile(S, 512)

    def body(n_ref, w_ref, o_ref):
        o_ref[...] = jnp.dot(n_ref[...], w_ref[...], preferred_element_type=F32)

    return pl.pallas_call(
        body, name=name, grid=(nk, S // tm),
        in_specs=[pl.BlockSpec((tm, D), lambda k, m: (m, 0)), pl.BlockSpec((None, D, wd), lambda k, m: (k, 0, 0))],
        out_specs=pl.BlockSpec((tm, wd), lambda k, m: (m, k)),
        out_shape=jax.ShapeDtypeStruct((S, nk * wd), F32),
        compiler_params=_params(("parallel", "parallel")),
    )(n, w)


def _ffn_dact(df, wd, ga, up, name):
    S, D = df.shape
    nk, w, _ = wd.shape
    tm = _row_tile(S, 512)

    def body(df_ref, wd_ref, ga_ref, up_ref, dga_ref, dup_ref):
        dact = lax.dot_general(df_ref[...], wd_ref[...], NT, preferred_element_type=F32)
        ga_ = ga_ref[...].astype(F32)
        up_ = up_ref[...].astype(F32)
        sig = _sigmoid(ga_)
        dga_ref[...] = (dact * up_ * (sig * (1.0 + ga_ * (1.0 - sig)))).astype(BF16)
        dup_ref[...] = (dact * (ga_ * sig)).astype(BF16)

    cspec = pl.BlockSpec((None, tm, w), lambda k, m: (k, m, 0))
    out = jax.ShapeDtypeStruct((nk, S, w), BF16)
    return pl.pallas_call(
        body, name=name, grid=(nk, S // tm),
        in_specs=[pl.BlockSpec((tm, D), lambda k, m: (m, 0)), pl.BlockSpec((None, w, D), lambda k, m: (k, 0, 0)),
                  cspec, cspec],
        out_specs=[cspec, cspec], out_shape=[out, out],
        compiler_params=_params(("parallel", "parallel")),
    )(df, wd, ga, up)


def _mm_nt(d, w, name):
    S, K = d.shape
    N = w.shape[0]
    tm = _row_tile(S, 512)

    def body(d_ref, w_ref, o_ref):
        o_ref[...] = lax.dot_general(d_ref[...], w_ref[...], NT, preferred_element_type=F32)

    return pl.pallas_call(
        body, name=name, grid=(S // tm,),
        in_specs=[pl.BlockSpec((tm, K), lambda m: (m, 0)), pl.BlockSpec((N, K), lambda m: (0, 0))],
        out_specs=pl.BlockSpec((tm, N), lambda m: (m, 0)),
        out_shape=jax.ShapeDtypeStruct((S, N), F32),
        compiler_params=_params(("parallel",)),
    )(d, w)


def _mm_nt_reduce(lhs_list, w_list, chunked3d, name):
    nk, D, kc = w_list[0].shape
    S = lhs_list[0].shape[1] if chunked3d else lhs_list[0].shape[0]
    tm = _row_tile(S, 512)
    npair = len(lhs_list)

    def body(*refs):
        l_refs, w_refs = refs[:npair], refs[npair:2 * npair]
        o_ref, acc_ref = refs[2 * npair], refs[2 * npair + 1]
        k = pl.program_id(1)

        @pl.when(k == 0)
        def _():
            acc_ref[...] = jnp.zeros_like(acc_ref)

        for l_ref, w_ref in zip(l_refs, w_refs):
            acc_ref[...] += lax.dot_general(l_ref[...], w_ref[...], NT, preferred_element_type=F32)

        @pl.when(k == nk - 1)
        def _():
            o_ref[...] = acc_ref[...]

    if chunked3d:
        lspec = pl.BlockSpec((None, tm, kc), lambda m, k: (k, m, 0))
    else:
        lspec = pl.BlockSpec((tm, kc), lambda m, k: (m, k))
    wspec = pl.BlockSpec((None, D, kc), lambda m, k: (k, 0, 0))
    return pl.pallas_call(
        body, name=name, grid=(S // tm, nk),
        in_specs=[lspec] * npair + [wspec] * npair,
        out_specs=pl.BlockSpec((tm, D), lambda m, k: (m, 0)),
        out_shape=jax.ShapeDtypeStruct((S, D), F32),
        scratch_shapes=[pltpu.VMEM((tm, D), F32)],
        compiler_params=_params(("parallel", "arbitrary")),
    )(*lhs_list, *w_list)


def _wgrad_chunk_lhs(lhs, rhs, name):
    nk, S, w = lhs.shape
    D = rhs.shape[1]
    ts = _row_tile(S, 512)
    ns = S // ts

    def body(l_ref, r_ref, o_ref, acc_ref):
        s = pl.program_id(1)

        @pl.when(s == 0)
        def _():
            acc_ref[...] = jnp.zeros_like(acc_ref)

        acc_ref[...] += lax.dot_general(l_ref[...], r_ref[...], TN, preferred_element_type=F32)

        @pl.when(s == ns - 1)
        def _():
            o_ref[...] = acc_ref[...]

    return pl.pallas_call(
        body, name=name, grid=(nk, ns),
        in_specs=[pl.BlockSpec((None, ts, w), lambda k, s: (k, s, 0)), pl.BlockSpec((ts, D), lambda k, s: (s, 0))],
        out_specs=pl.BlockSpec((None, w, D), lambda k, s: (k, 0, 0)),
        out_shape=jax.ShapeDtypeStruct((nk, w, D), F32),
        scratch_shapes=[pltpu.VMEM((w, D), F32)],
        compiler_params=_params(("parallel", "arbitrary")),
    )(lhs, rhs)


def _wgrad_chunk_rhs(lhs, rhs_list, nk, chunked3d, name):
    S, D = lhs.shape
    w = rhs_list[0].shape[2] if chunked3d else rhs_list[0].shape[1] // nk
    ts = _row_tile(S, 512)
    ns = S // ts
    nr = len(rhs_list)

    def body(*refs):
        l_ref, r_refs = refs[0], refs[1:1 + nr]
        o_refs, acc_refs = refs[1 + nr:1 + 2 * nr], refs[1 + 2 * nr:]
        s = pl.program_id(1)

        @pl.when(s == 0)
        def _():
            for acc_ref in acc_refs:
                acc_ref[...] = jnp.zeros_like(acc_ref)

        x = l_ref[...]
        for r_ref, acc_ref in zip(r_refs, acc_refs):
            acc_ref[...] += lax.dot_general(x, r_ref[...], TN, preferred_element_type=F32)

        @pl.when(s == ns - 1)
        def _():
            for o_ref, acc_ref in zip(o_refs, acc_refs):
                o_ref[...] = acc_ref[...]

    if chunked3d:
        rspec = pl.BlockSpec((None, ts, w), lambda k, s: (k, s, 0))
    else:
        rspec = pl.BlockSpec((ts, w), lambda k, s: (s, k))
    ospec = pl.BlockSpec((None, D, w), lambda k, s: (k, 0, 0))
    return pl.pallas_call(
        body, name=name, grid=(nk, ns),
        in_specs=[pl.BlockSpec((ts, D), lambda k, s: (s, 0))] + [rspec] * nr,
        out_specs=[ospec] * nr,
        out_shape=[jax.ShapeDtypeStruct((nk, D, w), F32)] * nr,
        scratch_shapes=[pltpu.VMEM((D, w), F32)] * nr,
        compiler_params=_params(("parallel", "arbitrary")),
    )(lhs, *rhs_list)


def _rope_tables(S):
    pos = jnp.arange(S, dtype=F32)
    inv_freq = ROPE_THETA ** (-jnp.arange(0, HEAD_DIM, 2, dtype=F32) / HEAD_DIM)
    ang = pos[:, None] * inv_freq[None, :]
    cos, sin = jnp.cos(ang), jnp.sin(ang)
    cos2 = jnp.concatenate([cos, cos, cos, cos], axis=1)
    sin2 = jnp.concatenate([-sin, sin, -sin, sin], axis=1)
    return cos2, sin2


def _rotate(t, cos, sin_signed):
    half = HEAD_DIM // 2
    lane = lax.broadcasted_iota(jnp.int32, t.shape, 1)
    first = (lane % HEAD_DIM) < half
    partner = jnp.where(first, pltpu.roll(t, 128 - half, 1), pltpu.roll(t, half, 1))
    return t * cos + partner * sin_signed


def _qkv_rope(proj, cos, sin, name):
    S = proj.shape[0]
    A = 512
    tr = _row_tile(S, 512)
    nb = A // 128
    scale = HEAD_DIM ** -0.5

    def body(q_ref, k_ref, v_ref, c_ref, s_ref, qo_ref, ko_ref, vo_ref):
        c, s = c_ref[...], s_ref[...]
        qo_ref[...] = (_rotate(q_ref[...], c, s) * scale).astype(BF16)
        ko_ref[...] = _rotate(k_ref[...], c, s).astype(BF16)
        vo_ref[...] = v_ref[...].astype(BF16)

    def col(off):
        return pl.BlockSpec((tr, 128), lambda i, j: (i, off + j))

    tab = pl.BlockSpec((tr, 128), lambda i, j: (i, 0))
    out = jax.ShapeDtypeStruct((S, A), BF16)
    return pl.pallas_call(
        body, name=name, grid=(S // tr, nb),
        in_specs=[col(0), col(nb), col(2 * nb), tab, tab],
        out_specs=[col(0), col(0), col(0)], out_shape=[out, out, out],
        compiler_params=_params(("parallel", "parallel")),
    )(proj, proj, proj, cos, sin)


def _rope_bwd(parts, cos, sin, scale, rotate, name):
    S, A = parts[0].shape
    tr = _row_tile(S, 512)

    def body(a_ref, b_ref, c_ref, cos_ref, sin_ref, o_ref):
        t = (a_ref[...] + b_ref[...]) + c_ref[...]
        if rotate:
            t = _rotate(t, cos_ref[...], -sin_ref[...])
        o_ref[...] = (t * scale).astype(BF16)

    col = pl.BlockSpec((tr, 128), lambda i, j: (i, j))
    tab = pl.BlockSpec((tr, 128), lambda i, j: (i, 0))
    return pl.pallas_call(
        body, name=name, grid=(S // tr, A // 128),
        in_specs=[col, col, col, tab, tab], out_specs=col,
        out_shape=jax.ShapeDtypeStruct((S, A), BF16),
        compiler_params=_params(("parallel", "parallel")),
    )(*parts, cos, sin)


def _band_masks(T, has_prev):
    qi = lax.broadcasted_iota(jnp.int32, (T, T), 0)
    kj = lax.broadcasted_iota(jnp.int32, (T, T), 1)
    return qi >= kj, (kj >= qi) & has_prev


def _attn_fwd(q, k, v, dilation, name):
    S, A = q.shape
    L = S // dilation
    cols = dilation * A
    T = min(ATTN_BLOCK, L)
    nb = L // T
    assert L % T == 0 and (T == ATTN_BLOCK or nb == 1)
    q, k, v = (t.reshape(L, cols) for t in (q, k, v))

    def body(q_ref, k_ref, v_ref, o_ref, lse_ref):
        lane = lax.broadcasted_iota(jnp.int32, (1, 128), 1)
        head0 = lane < HEAD_DIM

        def step(n, carry):
            rows = pl.ds(pl.multiple_of(n * T, T), T)
            prev = pl.ds(pl.multiple_of(jnp.maximum(n - 1, 0) * T, T), T)
            qb = q_ref[rows, :]
            kc, vc = k_ref[rows, :], v_ref[rows, :]
            kp, vp = k_ref[prev, :], v_ref[prev, :]
            valid_d, valid_o = _band_masks(T, n > 0)
            outs, lses = [], []
            for hmask in (head0, jnp.logical_not(head0)):
                qh = jnp.where(hmask, qb, jnp.zeros_like(qb))
                sd = jnp.where(valid_d, lax.dot_general(qh, kc, NT, preferred_element_type=F32), NEG)
                so = jnp.where(valid_o, lax.dot_general(qh, kp, NT, preferred_element_type=F32), NEG)
                m = jnp.maximum(jnp.max(sd, axis=-1, keepdims=True), jnp.max(so, axis=-1, keepdims=True))
                pd = jnp.exp(sd - m)
                po = jnp.exp(so - m)
                den = jnp.sum(pd, axis=-1, keepdims=True) + jnp.sum(po, axis=-1, keepdims=True)
                acc = jnp.dot(pd.astype(BF16), vc, preferred_element_type=F32)
                acc += jnp.dot(po.astype(BF16), vp, preferred_element_type=F32)
                outs.append(acc / den)
                lses.append(m + jnp.log(den))
            o_ref[rows, :] = jnp.where(head0, outs[0], outs[1])
            lse_ref[rows, :] = jnp.where(head0, lses[0], lses[1])
            return carry

        lax.fori_loop(0, nb, step, 0)

    blk = pl.BlockSpec((L, 128), lambda j: (0, j))
    out = jax.ShapeDtypeStruct((L, cols), F32)
    o, lse = pl.pallas_call(
        body, name=name, grid=(cols // 128,),
        in_specs=[blk, blk, blk], out_specs=[blk, blk], out_shape=[out, out],
        compiler_params=_params(("parallel",)),
    )(q, k, v)
    return o.reshape(S, A), lse.reshape(S, A)


def _attn_merge(outs, lses, gain, name):
    S, A = outs[0].shape
    tr = _row_tile(S, 256)

    def body(o1, o2, o3, l1, l2, l3, g_ref, attn_ref, lse_ref, y_ref):
        a, b, c = l1[...], l2[...], l3[...]
        m = jnp.maximum(jnp.maximum(a, b), c)
        ea, eb, ec = jnp.exp(a - m), jnp.exp(b - m), jnp.exp(c - m)
        z = (ea + eb) + ec
        attn = ((ea / z) * o1[...] + (eb / z) * o2[...]) + (ec / z) * o3[...]
        attn_ref[...] = attn
        lse_ref[...] = m + jnp.log(z)
        r = lax.rsqrt(jnp.mean(attn * attn, axis=-1, keepdims=True) + RMS_EPS)
        y_ref[...] = ((attn * r) * g_ref[...]).astype(BF16)

    row = pl.BlockSpec((tr, A), lambda i: (i, 0))
    return pl.pallas_call(
        body, name=name, grid=(S // tr,),
        in_specs=[row] * 6 + [_vec_spec(A, 1)], out_specs=[row, row, row],
        out_shape=[jax.ShapeDtypeStruct((S, A), F32), jax.ShapeDtypeStruct((S, A), F32),
                   jax.ShapeDtypeStruct((S, A), BF16)],
        compiler_params=_params(("parallel",)),
    )(*outs, *lses, gain)


def _attn_merge_bwd(dy, attn, gain, name):
    S, A = attn.shape
    tr = _row_tile(S, 256)

    def body(dy_ref, a_ref, g_ref, da_ref, dl_ref, dg_ref):
        @pl.when(pl.program_id(0) == 0)
        def _():
            dg_ref[...] = jnp.zeros_like(dg_ref)

        x = a_ref[...]
        dy_ = dy_ref[...]
        r = lax.rsqrt(jnp.mean(x * x, axis=-1, keepdims=True) + RMS_EPS)
        xh = x * r
        dg_ref[...] += jnp.sum(dy_ * xh, axis=0, keepdims=True)
        dxh = dy_ * g_ref[...]
        dx = r * (dxh - xh * jnp.mean(dxh * xh, axis=-1, keepdims=True))
        da_ref[...] = dx.astype(BF16)
        prod = dx * x
        hi = lax.broadcasted_iota(jnp.int32, (A, A), 0) // HEAD_DIM
        hj = lax.broadcasted_iota(jnp.int32, (A, A), 1) // HEAD_DIM
        same_head = (hi == hj).astype(F32)
        dl_ref[...] = jnp.dot(prod, same_head, preferred_element_type=F32, precision=lax.Precision.HIGHEST)

    row = pl.BlockSpec((tr, A), lambda i: (i, 0))
    vec = _vec_spec(A, 1)
    return pl.pallas_call(
        body, name=name, grid=(S // tr,),
        in_specs=[row, row, vec], out_specs=[row, row, vec],
        out_shape=[jax.ShapeDtypeStruct((S, A), BF16), jax.ShapeDtypeStruct((S, A), F32),
                   jax.ShapeDtypeStruct((1, A), F32)],
        compiler_params=_params(("arbitrary",)),
    )(dy, attn, gain)


def _attn_bwd(q, k, v, da, lse, delta, dilation, name):
    S, A = q.shape
    L = S // dilation
    cols = dilation * A
    T = min(ATTN_BLOCK, L)
    nb = L // T
    assert L % T == 0 and (T == ATTN_BLOCK or nb == 1)
    q, k, v, da, lse, delta = (t.reshape(L, cols) for t in (q, k, v, da, lse, delta))

    def body(q_ref, k_ref, v_ref, da_ref, lse_ref, dl_ref, dq_ref, dk_ref, dv_ref):
        lane = lax.broadcasted_iota(jnp.int32, (1, 128), 1)
        head0 = lane < HEAD_DIM
        dk_ref[...] = jnp.zeros_like(dk_ref)
        dv_ref[...] = jnp.zeros_like(dv_ref)

        def step(n, carry):
            rows = pl.ds(pl.multiple_of(n * T, T), T)
            prev = pl.ds(pl.multiple_of(jnp.maximum(n - 1, 0) * T, T), T)
            qb, dab = q_ref[rows, :], da_ref[rows, :]
            lse_b, dl_b = lse_ref[rows, :], dl_ref[rows, :]
            kc, vc = k_ref[rows, :], v_ref[rows, :]
            kp, vp = k_ref[prev, :], v_ref[prev, :]
            valid_d, valid_o = _band_masks(T, n > 0)
            dqs = []
            dkc = dkp = dvc = dvp = None
            for h, hmask in enumerate((head0, jnp.logical_not(head0))):
                qh = jnp.where(hmask, qb, jnp.zeros_like(qb))
                dah = jnp.where(hmask, dab, jnp.zeros_like(dab))
                c0 = h * HEAD_DIM
                lse_h = lse_b[:, c0:c0 + 1]
                dl_h = dl_b[:, c0:c0 + 1]
                sd = lax.dot_general(qh, kc, NT, preferred_element_type=F32)
                so = lax.dot_general(qh, kp, NT, preferred_element_type=F32)
                pd = jnp.where(valid_d, jnp.exp(sd - lse_h), 0.0)
                po = jnp.where(valid_o, jnp.exp(so - lse_h), 0.0)
                dpd = lax.dot_general(dah, vc, NT, preferred_element_type=F32)
                dpo = lax.dot_general(dah, vp, NT, preferred_element_type=F32)
                dsd = (pd * (dpd - dl_h)).astype(BF16)
                dso = (po * (dpo - dl_h)).astype(BF16)
                pdb, pob = pd.astype(BF16), po.astype(BF16)
                dqs.append(jnp.dot(dsd, kc, preferred_element_type=F32) + jnp.dot(dso, kp, preferred_element_type=F32))
                t_kc = lax.dot_general(dsd, qh, TN, preferred_element_type=F32)
                t_kp = lax.dot_general(dso, qh, TN, preferred_element_type=F32)
                t_vc = lax.dot_general(pdb, dah, TN, preferred_element_type=F32)
                t_vp = lax.dot_general(pob, dah, TN, preferred_element_type=F32)
                if h == 0:
                    dkc, dkp, dvc, dvp = t_kc, t_kp, t_vc, t_vp
                else:
                    dkc, dkp, dvc, dvp = dkc + t_kc, dkp + t_kp, dvc + t_vc, dvp + t_vp
            dq_ref[rows, :] = jnp.where(head0, dqs[0], dqs[1])
            dk_ref[rows, :] += dkc
            dv_ref[rows, :] += dvc
            dk_ref[prev, :] += dkp
            dv_ref[prev, :] += dvp
            return carry

        lax.fori_loop(0, nb, step, 0)

    blk = pl.BlockSpec((L, 128), lambda j: (0, j))
    out = jax.ShapeDtypeStruct((L, cols), F32)
    dq, dk, dv = pl.pallas_call(
        body, name=name, grid=(cols // 128,),
        in_specs=[blk] * 6, out_specs=[blk] * 3, out_shape=[out] * 3,
        compiler_params=_params(("parallel",)),
    )(q, k, v, da, lse, delta)
    return dq.reshape(S, A), dk.reshape(S, A), dv.reshape(S, A)


def _glu_window(a_ref, b_ref, ah_ref, bh_ref, first):
    u0 = a_ref[...] * _sigmoid(b_ref[...])
    u0h = ah_ref[...] * _sigmoid(bh_ref[...])
    u0h = jnp.where(first, jnp.zeros_like(u0h), u0h)
    return jnp.concatenate([u0h, u0], axis=0)


def _conv_norms(u1, lng, lnb):
    mu = jnp.mean(u1, axis=-1, keepdims=True)
    xc = u1 - mu
    rstd = lax.rsqrt(jnp.mean(xc * xc, axis=-1, keepdims=True) + LN_EPS)
    u1h = xc * rstd
    u2 = u1h * lng + lnb
    sig = _sigmoid(u2)
    u3 = u2 * sig
    r = lax.rsqrt(jnp.mean(u3 * u3, axis=-1, keepdims=True) + RMS_EPS)
    return rstd, u1h, u2, sig, u3, r


def _conv_specs(tr, C, col_a, col_b):
    per = tr // HALO

    def tile(col):
        return pl.BlockSpec((tr, C), lambda i: (i, col))

    def halo(col):
        return pl.BlockSpec((HALO, C), lambda i: (jnp.maximum(i * per - 1, 0), col))

    return tile(col_a), tile(col_b), halo(col_a), halo(col_b)


def _conv_fwd(proj, cw, cb, lng, lnb, gco, name):
    S = proj.shape[0]
    C = 512
    tr = _row_tile(S, 256)

    def body(a_ref, b_ref, ah_ref, bh_ref, w_ref, cb_ref, lng_ref, lnb_ref, gco_ref, u1_ref, y_ref):
        win = _glu_window(a_ref, b_ref, ah_ref, bh_ref, pl.program_id(0) == 0)
        acc = jnp.broadcast_to(cb_ref[...], (tr, C))
        for j in range(CONV_KERNEL):
            off = HALO - (CONV_KERNEL - 1) + j
            acc = acc + w_ref[j:j + 1, :] * win[off:off + tr, :]
        u1_ref[...] = acc
        _, _, _, _, u3, r = _conv_norms(acc, lng_ref[...], lnb_ref[...])
        y_ref[...] = ((u3 * r) * gco_ref[...]).astype(BF16)

    ta, tb, ha, hb = _conv_specs(tr, C, 3, 4)
    row = pl.BlockSpec((tr, C), lambda i: (i, 0))
    vec = _vec_spec(C, 1)
    return pl.pallas_call(
        body, name=name, grid=(S // tr,),
        in_specs=[ta, tb, ha, hb, pl.BlockSpec((HALO, C), lambda i: (0, 0)), vec, vec, vec, vec],
        out_specs=[row, row],
        out_shape=[jax.ShapeDtypeStruct((S, C), F32), jax.ShapeDtypeStruct((S, C), BF16)],
        compiler_params=_params(("parallel",)),
    )(proj, proj, proj, proj, cw, cb, lng, lnb, gco)


def _conv_bwd_norms(dy, u1, lng, lnb, gco, name):
    S, C = u1.shape
    tr = _row_tile(S, 256)

    def body(dy_ref, u1_ref, lng_ref, lnb_ref, gco_ref, du1_ref, dgco_ref, dlng_ref, dlnb_ref, dcb_ref):
        @pl.when(pl.program_id(0) == 0)
        def _():
            for ref in (dgco_ref, dlng_ref, dlnb_ref, dcb_ref):
                ref[...] = jnp.zeros_like(ref)

        lng = lng_ref[...]
        rstd, u1h, u2, sig, u3, r = _conv_norms(u1_ref[...], lng, lnb_ref[...])
        dy_ = dy_ref[...]
        u3h = u3 * r
        dgco_ref[...] += jnp.sum(dy_ * u3h, axis=0, keepdims=True)
        du3h = dy_ * gco_ref[...]
        du3 = r * (du3h - u3h * jnp.mean(du3h * u3h, axis=-1, keepdims=True))
        du2 = du3 * (sig * (1.0 + u2 * (1.0 - sig)))
        dlng_ref[...] += jnp.sum(du2 * u1h, axis=0, keepdims=True)
        dlnb_ref[...] += jnp.sum(du2, axis=0, keepdims=True)
        du1h = du2 * lng
        du1 = rstd * (du1h - jnp.mean(du1h, axis=-1, keepdims=True)
                      - u1h * jnp.mean(du1h * u1h, axis=-1, keepdims=True))
        du1_ref[...] = du1
        dcb_ref[...] += jnp.sum(du1, axis=0, keepdims=True)

    row = pl.BlockSpec((tr, C), lambda i: (i, 0))
    vec = _vec_spec(C, 1)
    return pl.pallas_call(
        body, name=name, grid=(S // tr,),
        in_specs=[pl.BlockSpec((tr, C), lambda i: (i, 1)), row, vec, vec, vec],
        out_specs=[row, vec, vec, vec, vec],
        out_shape=[jax.ShapeDtypeStruct((S, C), F32)] + [jax.ShapeDtypeStruct((1, C), F32)] * 4,
        compiler_params=_params(("arbitrary",)),
    )(dy, u1, lng, lnb, gco)


def _conv_bwd_taps(du1, proj, cw, name):
    S, C = du1.shape
    tr = _row_tile(S, 256)
    nt = S // tr
    per = tr // HALO

    def body(du_ref, dun_ref, a_ref, b_ref, ah_ref, bh_ref, w_ref, da_ref, db_ref, dw_ref):
        i = pl.program_id(0)

        @pl.when(i == 0)
        def _():
            dw_ref[...] = jnp.zeros_like(dw_ref)

        win = _glu_window(a_ref, b_ref, ah_ref, bh_ref, i == 0)
        du = du_ref[...]
        nxt = jnp.where(i == nt - 1, jnp.zeros_like(dun_ref[...]), dun_ref[...])
        dwin = jnp.concatenate([du, nxt], axis=0)
        du0 = jnp.zeros((tr, C), F32)
        taps = []
        for j in range(CONV_KERNEL):
            back = CONV_KERNEL - 1 - j
            du0 = du0 + w_ref[j:j + 1, :] * dwin[back:back + tr, :]
            off = HALO - (CONV_KERNEL - 1) + j
            taps.append(jnp.sum(du * win[off:off + tr, :], axis=0, keepdims=True))
        taps.append(jnp.zeros((HALO - CONV_KERNEL, C), F32))
        dw_ref[...] += jnp.concatenate(taps, axis=0)
        a, sig = a_ref[...], _sigmoid(b_ref[...])
        da_ref[...] = (du0 * sig).astype(BF16)
        db_ref[...] = (du0 * a * sig * (1.0 - sig)).astype(BF16)

    ta, tb, ha, hb = _conv_specs(tr, C, 3, 4)
    row = pl.BlockSpec((tr, C), lambda i: (i, 0))
    nxt = pl.BlockSpec((HALO, C), lambda i: (jnp.minimum((i + 1) * per, S // HALO - 1), 0))
    wspec = pl.BlockSpec((HALO, C), lambda i: (0, 0))
    return pl.pallas_call(
        body, name=name, grid=(nt,),
        in_specs=[row, nxt, ta, tb, ha, hb, wspec],
        out_specs=[row, row, wspec],
        out_shape=[jax.ShapeDtypeStruct((S, C), BF16), jax.ShapeDtypeStruct((S, C), BF16),
                   jax.ShapeDtypeStruct((HALO, C), F32)],
        compiler_params=_params(("arbitrary",)),
    )(du1, du1, proj, proj, proj, proj, cw)


def _ada_fwd(c_all, w, b, name):
    B, D = c_all.shape
    N = w.shape[1]
    tn = 768 if N % 768 == 0 else N

    def body(c_ref, w_ref, b_ref, o_ref):
        c = c_ref[...]
        a = (c * _sigmoid(c)).astype(BF16)
        o_ref[...] = jnp.dot(a, w_ref[...].astype(BF16), preferred_element_type=F32) + b_ref[...]

    return pl.pallas_call(
        body, name=name, grid=(N // tn,),
        in_specs=[pl.BlockSpec((B, D), lambda j: (0, 0)), pl.BlockSpec((D, tn), lambda j: (0, j)),
                  pl.BlockSpec((1, tn), lambda j: (0, j))],
        out_specs=pl.BlockSpec((B, tn), lambda j: (0, j)),
        out_shape=jax.ShapeDtypeStruct((B, N), F32),
        compiler_params=_params(("parallel",)),
    )(c_all, w, b)


def _ada_wgrad(c_t, dmod, name):
    D, B = c_t.shape
    N = dmod.shape[1]
    tn = 768 if N % 768 == 0 else N

    def body(c_ref, d_ref, o_ref):
        c = c_ref[...]
        a = (c * _sigmoid(c)).astype(BF16)
        o_ref[...] = jnp.dot(a, d_ref[...].astype(BF16), preferred_element_type=F32)

    return pl.pallas_call(
        body, name=name, grid=(N // tn,),
        in_specs=[pl.BlockSpec((D, B), lambda j: (0, 0)), pl.BlockSpec((B, tn), lambda j: (0, j))],
        out_specs=pl.BlockSpec((D, tn), lambda j: (0, j)),
        out_shape=jax.ShapeDtypeStruct((D, N), F32),
        compiler_params=_params(("parallel",)),
    )(c_t, dmod)


def _sum_rows(x, name):
    R, N = x.shape

    def body(x_ref, o_ref):
        acc = x_ref[0:1, :]
        for r in range(1, R):
            acc = acc + x_ref[r:r + 1, :]
        o_ref[...] = acc

    return pl.pallas_call(
        body, name=name, out_shape=jax.ShapeDtypeStruct((1, N), F32),
        compiler_params=_params(),
    )(x)


def _sum_slots(x, name):
    n, R, C = x.shape
    tr = _row_tile(R, 128) if R % 128 == 0 else R

    def body(x_ref, o_ref):
        acc = x_ref[0]
        for s in range(1, n):
            acc = acc + x_ref[s]
        o_ref[...] = acc

    return pl.pallas_call(
        body, name=name, grid=(R // tr,),
        in_specs=[pl.BlockSpec((n, tr, C), lambda i: (0, i, 0))],
        out_specs=pl.BlockSpec((tr, C), lambda i: (i, 0)),
        out_shape=jax.ShapeDtypeStruct((R, C), F32),
        compiler_params=_params(("parallel",)),
    )(x)


def _add_halves(g, recv, c_idx, name):
    nk, _, R, C = g.shape
    tr = _row_tile(R, 128) if R % 128 == 0 else R

    def body(c_ref, g_ref, r_ref, o_ref):
        o_ref[...] = g_ref[...] + r_ref[...]

    grid_spec = pltpu.PrefetchScalarGridSpec(
        num_scalar_prefetch=1, grid=(nk, R // tr),
        in_specs=[pl.BlockSpec((None, None, tr, C), lambda k, i, c: (k, c[0], i, 0)),
                  pl.BlockSpec((None, tr, C), lambda k, i, c: (k, i, 0))],
        out_specs=pl.BlockSpec((None, tr, C), lambda k, i, c: (k, i, 0)),
    )
    return pl.pallas_call(
        body, name=name, grid_spec=grid_spec,
        out_shape=jax.ShapeDtypeStruct((nk, R, C), F32),
        compiler_params=_params(("parallel", "parallel")),
    )(c_idx, g, recv)


def _adamw(w, g, m, v, name):
    R, C = w.shape
    tr = _row_tile(R, 256) if R % 256 == 0 else R
    bc1 = 1.0 - ADAM_B1 ** ADAM_STEP
    bc2 = 1.0 - ADAM_B2 ** ADAM_STEP

    def body(w_ref, g_ref, m_ref, v_ref, d_ref, mo_ref, vo_ref):
        g_ = g_ref[...]
        m_ = ADAM_B1 * m_ref[...] + (1.0 - ADAM_B1) * g_
        v_ = ADAM_B2 * v_ref[...] + (1.0 - ADAM_B2) * (g_ * g_)
        mo_ref[...] = m_
        vo_ref[...] = v_
        d_ref[...] = -ADAM_LR * ((m_ / bc1) / (jnp.sqrt(v_ / bc2) + ADAM_EPS) + ADAM_WD * w_ref[...])

    row = pl.BlockSpec((tr, C), lambda i: (i, 0))
    out = jax.ShapeDtypeStruct((R, C), F32)
    return pl.pallas_call(
        body, name=name, grid=(R // tr,),
        in_specs=[row] * 4, out_specs=[row] * 3, out_shape=[out] * 3,
        compiler_params=_params(("parallel",)),
    )(w, g, m, v)


def _coords():
    return lax.axis_index("x"), lax.axis_index("y"), lax.axis_index("c")


def _all_gather8(x, name):
    R, N = x.shape
    assert R == 8
    flips = [(fx, fy, fc) for fx in (0, 1) for fy in (0, 1) for fc in (0, 1)][1:]

    def body(x_ref, o_ref, send_sems, recv_sems):
        mx, my, mc = _coords()
        me = 4 * mx + 2 * my + mc

        def rows(dev):
            return o_ref.at[pl.ds(pl.multiple_of(dev * R, R), R), :]

        o_ref[pl.ds(pl.multiple_of(me * R, R), R), :] = x_ref[...]
        copies = []
        for t, (fx, fy, fc) in enumerate(flips):
            peer = (mx ^ fx, my ^ fy, mc ^ fc)
            copies.append(pltpu.make_async_remote_copy(
                src_ref=x_ref, dst_ref=rows(me), send_sem=send_sems.at[t], recv_sem=recv_sems.at[t],
                device_id=peer, device_id_type=MESH))
        for cp in copies:
            cp.start()
        for t, (fx, fy, fc) in enumerate(flips):
            peer_id = 4 * (mx ^ fx) + 2 * (my ^ fy) + (mc ^ fc)
            pltpu.make_async_remote_copy(
                src_ref=x_ref, dst_ref=rows(peer_id), send_sem=send_sems.at[t], recv_sem=recv_sems.at[t],
                device_id=(mx ^ fx, my ^ fy, mc ^ fc), device_id_type=MESH).wait_recv()
        for cp in copies:
            cp.wait_send()

    return pl.pallas_call(
        body, name=name,
        in_specs=[pl.BlockSpec(memory_space=pltpu.VMEM)],
        out_specs=pl.BlockSpec(memory_space=pltpu.VMEM),
        out_shape=jax.ShapeDtypeStruct((N_DEV * R, N), F32),
        scratch_shapes=[pltpu.SemaphoreType.DMA((7,)), pltpu.SemaphoreType.DMA((7,))],
        compiler_params=pltpu.CompilerParams(has_side_effects=True, vmem_limit_bytes=VMEM_LIMIT_BYTES),
    )(x)


CHIP_FLIPS = ((1, 0), (0, 1), (1, 1))


def _gather_chips(shards, name):
    n = len(shards)

    def body(*refs):
        in_refs, out_refs = refs[:n], refs[n:2 * n]
        send_sems, recv_sems, local_sems = refs[2 * n:]
        mx, my, mc = _coords()
        me = 2 * mx + my
        started = []
        for i in range(n):
            loc = pltpu.make_async_copy(in_refs[i], out_refs[i].at[me], local_sems.at[i])
            loc.start()
            started.append(loc)
        sends = []
        for i in range(n):
            for t, (fx, fy) in enumerate(CHIP_FLIPS):
                cp = pltpu.make_async_remote_copy(
                    src_ref=in_refs[i], dst_ref=out_refs[i].at[me],
                    send_sem=send_sems.at[i, t], recv_sem=recv_sems.at[i, t],
                    device_id=(mx ^ fx, my ^ fy, mc), device_id_type=MESH)
                cp.start()
                sends.append(cp)
        for i in range(n):
            for t, (fx, fy) in enumerate(CHIP_FLIPS):
                src_chip = 2 * (mx ^ fx) + (my ^ fy)
                pltpu.make_async_remote_copy(
                    src_ref=in_refs[i], dst_ref=out_refs[i].at[src_chip],
                    send_sem=send_sems.at[i, t], recv_sem=recv_sems.at[i, t],
                    device_id=(mx ^ fx, my ^ fy, mc), device_id_type=MESH).wait_recv()
        for cp in sends:
            cp.wait_send()
        for loc in started:
            loc.wait()

    any_spec = pl.BlockSpec(memory_space=pl.ANY)
    return pl.pallas_call(
        body, name=name,
        in_specs=[any_spec] * n, out_specs=[any_spec] * n,
        out_shape=[jax.ShapeDtypeStruct((N_CHIPS,) + s.shape, s.dtype) for s in shards],
        scratch_shapes=[pltpu.SemaphoreType.DMA((n, 3)), pltpu.SemaphoreType.DMA((n, 3)),
                        pltpu.SemaphoreType.DMA((n,))],
        compiler_params=pltpu.CompilerParams(has_side_effects=True),
    )(*shards)


def _send_other_halves(grads, name):
    n = len(grads)

    def body(*refs):
        in_refs, out_refs = refs[:n], refs[n:2 * n]
        send_sems, recv_sems = refs[2 * n:]
        mx, my, mc = _coords()
        sends = []
        for i in range(n):
            cp = pltpu.make_async_remote_copy(
                src_ref=in_refs[i].at[:, 1 - mc], dst_ref=out_refs[i],
                send_sem=send_sems.at[i], recv_sem=recv_sems.at[i],
                device_id=(mx, my, 1 - mc), device_id_type=MESH)
            cp.start()
            sends.append(cp)
        for cp in sends:
            cp.wait_recv()
        for cp in sends:
            cp.wait_send()

    any_spec = pl.BlockSpec(memory_space=pl.ANY)
    return pl.pallas_call(
        body, name=name,
        in_specs=[any_spec] * n, out_specs=[any_spec] * n,
        out_shape=[jax.ShapeDtypeStruct((g.shape[0],) + g.shape[2:], g.dtype) for g in grads],
        scratch_shapes=[pltpu.SemaphoreType.DMA((n,)), pltpu.SemaphoreType.DMA((n,))],
        compiler_params=pltpu.CompilerParams(has_side_effects=True),
    )(*grads)


def _exchange_chunks(sums, name):
    n = len(sums)

    def body(*refs):
        in_refs, out_refs = refs[:n], refs[n:2 * n]
        send_sems, recv_sems, local_sems = refs[2 * n:]
        mx, my, mc = _coords()
        me = 2 * mx + my
        started = []
        for i in range(n):
            loc = pltpu.make_async_copy(in_refs[i].at[me], out_refs[i].at[me], local_sems.at[i])
            loc.start()
            started.append(loc)
        sends = []
        for i in range(n):
            for t, (fx, fy) in enumerate(CHIP_FLIPS):
                peer_chip = 2 * (mx ^ fx) + (my ^ fy)
                cp = pltpu.make_async_remote_copy(
                    src_ref=in_refs[i].at[peer_chip], dst_ref=out_refs[i].at[me],
                    send_sem=send_sems.at[i, t], recv_sem=recv_sems.at[i, t],
                    device_id=(mx ^ fx, my ^ fy, mc), device_id_type=MESH)
                cp.start()
                sends.append(cp)
        for i in range(n):
            for t, (fx, fy) in enumerate(CHIP_FLIPS):
                peer_chip = 2 * (mx ^ fx) + (my ^ fy)
                pltpu.make_async_remote_copy(
                    src_ref=in_refs[i].at[me], dst_ref=out_refs[i].at[peer_chip],
                    send_sem=send_sems.at[i, t], recv_sem=recv_sems.at[i, t],
                    device_id=(mx ^ fx, my ^ fy, mc), device_id_type=MESH).wait_recv()
        for cp in sends:
            cp.wait_send()
        for loc in started:
            loc.wait()

    any_spec = pl.BlockSpec(memory_space=pl.ANY)
    return pl.pallas_call(
        body, name=name,
        in_specs=[any_spec] * n, out_specs=[any_spec] * n,
        out_shape=[jax.ShapeDtypeStruct(s.shape, s.dtype) for s in sums],
        scratch_shapes=[pltpu.SemaphoreType.DMA((n, 3)), pltpu.SemaphoreType.DMA((n, 3)),
                        pltpu.SemaphoreType.DMA((n,))],
        compiler_params=pltpu.CompilerParams(has_side_effects=True),
    )(*sums)


def _share_halves(halves, name):
    n = len(halves)

    def body(*refs):
        in_refs, out_refs = refs[:n], refs[n:2 * n]
        send_sems, recv_sems, local_sems = refs[2 * n:]
        mx, my, mc = _coords()
        started, sends = [], []
        for i in range(n):
            loc = pltpu.make_async_copy(in_refs[i], out_refs[i].at[mc], local_sems.at[i])
            loc.start()
            started.append(loc)
            cp = pltpu.make_async_remote_copy(
                src_ref=in_refs[i], dst_ref=out_refs[i].at[mc],
                send_sem=send_sems.at[i], recv_sem=recv_sems.at[i],
                device_id=(mx, my, 1 - mc), device_id_type=MESH)
            cp.start()
            sends.append(cp)
        for i in range(n):
            pltpu.make_async_remote_copy(
                src_ref=in_refs[i], dst_ref=out_refs[i].at[1 - mc],
                send_sem=send_sems.at[i], recv_sem=recv_sems.at[i],
                device_id=(mx, my, 1 - mc), device_id_type=MESH).wait_recv()
        for cp in sends:
            cp.wait_send()
        for loc in started:
            loc.wait()

    any_spec = pl.BlockSpec(memory_space=pl.ANY)
    return pl.pallas_call(
        body, name=name,
        in_specs=[any_spec] * n, out_specs=[any_spec] * n,
        out_shape=[jax.ShapeDtypeStruct((2,) + h.shape, h.dtype) for h in halves],
        scratch_shapes=[pltpu.SemaphoreType.DMA((n,)), pltpu.SemaphoreType.DMA((n,)),
                        pltpu.SemaphoreType.DMA((n,))],
        compiler_params=pltpu.CompilerParams(has_side_effects=True),
    )(*halves)


def _reduce_scatter_grads(grads, tag):
    c_idx = lax.axis_index("c").astype(jnp.int32).reshape(1)
    split = [g.reshape(g.shape[0], 2, g.shape[1] // 2, g.shape[2]) for g in grads]
    recv = _send_other_halves(split, f"rs_halves_{tag}")
    chip_sums = [_add_halves(g, r, c_idx, f"rs_add_{tag}_{i}") for i, (g, r) in enumerate(zip(split, recv))]
    slots = _exchange_chunks(chip_sums, f"rs_chunks_{tag}")
    halves = [_sum_slots(s, f"rs_sum_{tag}_{i}") for i, s in enumerate(slots)]
    both = _share_halves(halves, f"rs_share_{tag}")
    return [b.reshape(2 * b.shape[1], b.shape[2]) for b in both]


def _ffn_forward(h, gain, sc, sh, gate, wg, wu, wd, tag):
    n = _norm_mod(h, gain, sc, sh, f"{tag}_norm")
    ga, up, act = _ffn_gate_up(n, wg, wu, f"{tag}_gate_up")
    h_out, f = _mm_residual(act, wd, h, gate, 0.5, f"{tag}_down")
    return h_out, (h, n, ga, up, act, f)


def _ffn_backward(dh_out, saved, gain, sc, gate, wg, wu, wd, tag):
    h, n, ga, up, act, f = saved
    df, d_gate = _gate_bwd(dh_out, f, gate, 0.5, f"{tag}_gate_bwd")
    dga, dup = _ffn_dact(df, wd, ga, up, f"{tag}_dact")
    dwd = _wgrad_chunk_lhs(act, df, f"{tag}_dwd")
    dwg, dwu = _wgrad_chunk_rhs(n, [dga, dup], wg.shape[0], True, f"{tag}_dwgu")
    dn = _mm_nt_reduce([dga, dup], [wg, wu], True, f"{tag}_dn")
    dh_in, d_sh, d_sc, d_gain = _norm_mod_bwd(dn, h, gain, sc, dh_out, f"{tag}_norm_bwd")
    return dh_in, (dwg, dwu, dwd), (d_sh, d_sc, d_gate, d_gain)


def _pad_cols(v, n):
    return jnp.pad(v, ((0, 0), (0, n - v.shape[1])))


def _mixer_forward(h1, mix_norm_g, sc2, sh2, gt2, win, wout, conv_w, conv_dw_b, conv_ln_g, conv_ln_b, attn_out_g,
                   conv_out_g):
    S, D = h1.shape
    n2 = _norm_mod(h1, mix_norm_g, sc2, sh2, "mix_norm")
    proj = _mm_cols(n2, win, "mix_in")
    cos, sin = _rope_tables(S)
    q, k, v = _qkv_rope(proj, cos, sin, "qkv_rope")
    outs, lses = [], []
    for dil in DILATIONS:
        o, lse = _attn_fwd(q, k, v, dil, f"attn_fwd_d{dil}")
        outs.append(o)
        lses.append(lse)
    attn, lse_tot, ya = _attn_merge(outs, lses, attn_out_g, "attn_merge")
    u1, yc = _conv_fwd(proj, conv_w, conv_dw_b, conv_ln_g, conv_ln_b, conv_out_g, "conv_fwd")
    y = jnp.concatenate([ya, yc], axis=1)
    h2, mo = _mm_residual(y[None], wout.reshape(1, D, D), h1, gt2, 1.0, "mix_out")
    return h2, (h1, n2, proj, cos, sin, q, k, v, attn, lse_tot, u1, y, mo)


def _mixer_backward(dh2, saved, mix_norm_g, sc2, gt2, win, wout, conv_w, conv_ln_g, conv_ln_b, attn_out_g,
                    conv_out_g):
    h1, n2, proj, cos, sin, q, k, v, attn, lse_tot, u1, y, mo = saved
    S, D = h1.shape
    dmo, d_gt2 = _gate_bwd(dh2, mo, gt2, 1.0, "mix_gate_bwd")
    dwout = _wgrad_chunk_lhs(y[None], dmo, "mix_dwout")
    dy = _mm_nt(dmo, wout.reshape(D, D), "mix_dy")
    dattn, delta, d_attn_g = _attn_merge_bwd(dy, attn, attn_out_g, "attn_merge_bwd")
    dqs, dks, dvs = [], [], []
    for dil in DILATIONS:
        dq_, dk_, dv_ = _attn_bwd(q, k, v, dattn, lse_tot, delta, dil, f"attn_bwd_d{dil}")
        dqs.append(dq_)
        dks.append(dk_)
        dvs.append(dv_)
    dq = _rope_bwd(dqs, cos, sin, HEAD_DIM ** -0.5, True, "dq_rope")
    dk = _rope_bwd(dks, cos, sin, 1.0, True, "dk_rope")
    dv = _rope_bwd(dvs, cos, sin, 1.0, False, "dv_sum")
    du1, d_gco, d_lng, d_lnb, d_cb = _conv_bwd_norms(dy, u1, conv_ln_g, conv_ln_b, conv_out_g, "conv_bwd_norms")
    dga_, dgb_, d_cw = _conv_bwd_taps(du1, proj, conv_w, "conv_bwd_taps")
    dproj = jnp.concatenate([dq, dk, dv, dga_, dgb_], axis=1)
    (dwin,) = _wgrad_chunk_rhs(n2, [dproj], N_CHIPS, False, "mix_dwin")
    dn2 = _mm_nt_reduce([dproj], [win], False, "mix_dn")
    dh1, d_sh2, d_sc2, d_gain2 = _norm_mod_bwd(dn2, h1, mix_norm_g, sc2, dh2, "mix_norm_bwd")
    small = (d_sh2, d_sc2, d_gt2, d_gain2, d_cb, d_lng, d_lnb, d_attn_g, d_gco, d_cw)
    return dh1, (dwin, dwout), small


def _old_kernel(x, c, w_ada, b_ada, ffn1_norm_g, ffn1_w_gate, ffn1_w_up, ffn1_w_down, mix_norm_g, w_in, conv_dw_w, conv_dw_b, conv_ln_g, conv_ln_b, attn_out_g, conv_out_g, w_out, ffn2_norm_g, ffn2_w_gate, ffn2_w_up, ffn2_w_down, final_norm_g, loss_target, m_w_ada, m_b_ada, m_ffn1_norm_g, m_ffn1_w_gate, m_ffn1_w_up, m_ffn1_w_down, m_mix_norm_g, m_w_in, m_conv_dw_w, m_conv_dw_b, m_conv_ln_g, m_conv_ln_b, m_attn_out_g, m_conv_out_g, m_w_out, m_ffn2_norm_g, m_ffn2_w_gate, m_ffn2_w_up, m_ffn2_w_down, m_final_norm_g, v_w_ada, v_b_ada, v_ffn1_norm_g, v_ffn1_w_gate, v_ffn1_w_up, v_ffn1_w_down, v_mix_norm_g, v_w_in, v_conv_dw_w, v_conv_dw_b, v_conv_ln_g, v_conv_ln_b, v_attn_out_g, v_conv_out_g, v_w_out, v_ffn2_norm_g, v_ffn2_w_gate, v_ffn2_w_up, v_ffn2_w_down, v_final_norm_g):
    S, D = x.shape[1], x.shape[2]
    mx, my, mc = _coords()
    chip = 2 * mx + my
    dev = 4 * mx + 2 * my + mc
    h0 = x[0]
    target = loss_target[0]

    big = [ffn1_w_gate[0], ffn1_w_up[0], ffn1_w_down[0], w_in[0], w_out[0], ffn2_w_gate[0], ffn2_w_up[0],
           ffn2_w_down[0]]
    wg1, wu1, wd1, win, wout, wg2, wu2, wd2 = _gather_chips([w.astype(BF16) for w in big], "gather_weights")

    ncw = CONV_KERNEL * 128
    n0 = -(-(D + ncw) // 1024) * 1024
    pk0 = _pad_cols(jnp.concatenate([c.reshape(1, D), conv_dw_w.reshape(1, ncw)], axis=1), n0)
    g0 = _all_gather8(pk0.reshape(8, n0 // 8), "gather_c").reshape(N_DEV, n0)
    c_all = g0[:, :D]
    conv_w = jnp.concatenate([g0[2 * kc, D:D + ncw].reshape(CONV_KERNEL, 128) for kc in range(N_CHIPS)], axis=1)
    conv_w = jnp.pad(conv_w, ((0, HALO - CONV_KERNEL), (0, 0)))
    nmod = w_ada.shape[2]
    b_shard = lax.dynamic_slice(b_ada, (0, chip * nmod), (1, nmod))
    mod_part = _ada_fwd(c_all, w_ada[0], b_shard, "ada_fwd")
    g1 = _all_gather8(mod_part, "gather_mod")
    mod_all = jnp.concatenate([g1[16 * kc:16 * kc + 8] for kc in range(N_CHIPS)], axis=1)
    mod = lax.dynamic_slice(mod_all, (dev, 0), (1, 9 * D))
    sh1, sc1, gt1, sh2, sc2, gt2, sh3, sc3, gt3 = [mod[:, i * D:(i + 1) * D] for i in range(9)]

    h1, saved1 = _ffn_forward(h0, ffn1_norm_g, sc1, sh1, gt1, wg1, wu1, wd1, "ffn1")
    h2, saved2 = _mixer_forward(h1, mix_norm_g, sc2, sh2, gt2, win, wout, conv_w, conv_dw_b, conv_ln_g, conv_ln_b,
                                attn_out_g, conv_out_g)
    h3, saved3 = _ffn_forward(h2, ffn2_norm_g, sc3, sh3, gt3, wg2, wu2, wd2, "ffn2")
    loss_part, dh3, d_final_g = _loss_head(h3, final_norm_g.reshape(1, D), target, "loss_head")

    dh2, dw_ffn2, (d_sh3, d_sc3, d_gt3, d_gain3) = _ffn_backward(
        dh3, saved3, ffn2_norm_g, sc3, gt3, wg2, wu2, wd2, "ffn2")
    g_wg2, g_wu2, g_wd2 = _reduce_scatter_grads(list(dw_ffn2), "ffn2")
    dh1, (dwin, dwout), small_mix = _mixer_backward(
        dh2, saved2, mix_norm_g, sc2, gt2, win, wout, conv_w, conv_ln_g, conv_ln_b, attn_out_g, conv_out_g)
    d_sh2, d_sc2, d_gt2, d_gain2, d_cb, d_lng, d_lnb, d_attn_g, d_gco, d_cw = small_mix
    g_win, g_wout = _reduce_scatter_grads([dwin, dwout.reshape(N_CHIPS, D // N_CHIPS, D)], "mix")
    dh0, dw_ffn1, (d_sh1, d_sc1, d_gt1, d_gain1) = _ffn_backward(
        dh1, saved1, ffn1_norm_g, sc1, gt1, wg1, wu1, wd1, "ffn1")
    g_wg1, g_wu1, g_wd1 = _reduce_scatter_grads(list(dw_ffn1), "ffn1")

    dmod = jnp.concatenate([d_sh1, d_sc1, d_gt1, d_sh2, d_sc2, d_gt2, d_sh3, d_sc3, d_gt3], axis=1)
    small = [d_gain1, d_gain2, d_gain3, d_final_g, d_cb, d_lng, d_lnb, d_attn_g, d_gco,
             d_cw[:CONV_KERNEL].reshape(1, CONV_KERNEL * 512), loss_part]
    pk1 = jnp.concatenate([dmod] + small, axis=1)
    n1 = -(-pk1.shape[1] // 1024) * 1024
    gathered = _all_gather8(_pad_cols(pk1, n1).reshape(8, n1 // 8), "gather_small").reshape(N_DEV, n1)
    tot = _sum_rows(gathered, "sum_small")
    off = [0]

    def take(nel):
        out = tot[:, off[0]:off[0] + nel]
        off[0] += nel
        return out

    g_b_ada = take(9 * D)
    g_ffn1_norm, g_mix_norm, g_ffn2_norm, g_final = take(D), take(D), take(D), take(D)
    g_cb, g_lng, g_lnb, g_attn_g, g_gco = take(512), take(512), take(512), take(512), take(512)
    g_cw_full = take(CONV_KERNEL * 512).reshape(CONV_KERNEL, 512)
    loss = take(128)[0, 0]
    g_cw = lax.dynamic_slice(g_cw_full, (0, chip * 128), (CONV_KERNEL, 128))

    dmod_shard = lax.dynamic_slice(gathered[:, :9 * D], (0, chip * nmod), (N_DEV, nmod))
    dmod16 = jnp.pad(dmod_shard, ((0, N_DEV), (0, 0)))
    c_t16 = jnp.pad(c_all.T, ((0, 0), (0, N_DEV)))
    g_w_ada = _ada_wgrad(c_t16, dmod16, "ada_wgrad")

    names = ["w_ada", "b_ada", "ffn1_norm_g", "ffn1_w_gate", "ffn1_w_up", "ffn1_w_down", "mix_norm_g", "w_in",
             "conv_dw_w", "conv_dw_b", "conv_ln_g", "conv_ln_b", "attn_out_g", "conv_out_g", "w_out", "ffn2_norm_g",
             "ffn2_w_gate", "ffn2_w_up", "ffn2_w_down", "final_norm_g"]
    weights = [w_ada, b_ada, ffn1_norm_g, ffn1_w_gate, ffn1_w_up, ffn1_w_down, mix_norm_g, w_in, conv_dw_w,
               conv_dw_b, conv_ln_g, conv_ln_b, attn_out_g, conv_out_g, w_out, ffn2_norm_g, ffn2_w_gate, ffn2_w_up,
               ffn2_w_down, final_norm_g]
    ms = [m_w_ada, m_b_ada, m_ffn1_norm_g, m_ffn1_w_gate, m_ffn1_w_up, m_ffn1_w_down, m_mix_norm_g, m_w_in,
          m_conv_dw_w, m_conv_dw_b, m_conv_ln_g, m_conv_ln_b, m_attn_out_g, m_conv_out_g, m_w_out, m_ffn2_norm_g,
          m_ffn2_w_gate, m_ffn2_w_up, m_ffn2_w_down, m_final_norm_g]
    vs = [v_w_ada, v_b_ada, v_ffn1_norm_g, v_ffn1_w_gate, v_ffn1_w_up, v_ffn1_w_down, v_mix_norm_g, v_w_in,
          v_conv_dw_w, v_conv_dw_b, v_conv_ln_g, v_conv_ln_b, v_attn_out_g, v_conv_out_g, v_w_out, v_ffn2_norm_g,
          v_ffn2_w_gate, v_ffn2_w_up, v_ffn2_w_down, v_final_norm_g]
    grads_flat = [g_w_ada, g_b_ada, g_ffn1_norm, g_wg1, g_wu1, g_wd1, g_mix_norm, g_win, g_cw, g_cb, g_lng, g_lnb,
                  g_attn_g, g_gco, g_wout, g_ffn2_norm, g_wg2, g_wu2, g_wd2, g_final]
    big_names = ("w_ada", "ffn1_w_gate", "ffn1_w_up", "ffn1_w_down", "w_in", "w_out", "ffn2_w_gate", "ffn2_w_up",
                 "ffn2_w_down")
    grads, deltas, new_ms, new_vs = {}, {}, {}, {}
    for i, nm in enumerate(names):
        if nm in big_names:
            shape = weights[i].shape
            two_d = (shape[-2], shape[-1])
            g2 = grads_flat[i].reshape(two_d)
            d_, m_, v_ = _adamw(weights[i].reshape(two_d), g2, ms[i].reshape(two_d), vs[i].reshape(two_d),
                                f"adamw_{nm}")
            grads[nm], deltas[nm], new_ms[nm], new_vs[nm] = (t.reshape(shape) for t in (g2, d_, m_, v_))
    small_ix = [i for i, nm in enumerate(names) if nm not in big_names]

    def pack_small(arrs):
        flat = jnp.concatenate([arrs[i].reshape(1, -1) for i in small_ix], axis=1)
        npad = -(-flat.shape[1] // 1024) * 1024
        return _pad_cols(flat, npad).reshape(8, npad // 8)

    d_s, m_s, v_s = _adamw(pack_small(weights), pack_small(grads_flat), pack_small(ms), pack_small(vs),
                           "adamw_small")
    pos = 0
    for i in small_ix:
        nm, shape, nel = names[i], weights[i].shape, weights[i].size
        grads[nm] = grads_flat[i].reshape(shape)
        deltas[nm], new_ms[nm], new_vs[nm] = (t.reshape(1, -1)[:, pos:pos + nel].reshape(shape)
                                              for t in (d_s, m_s, v_s))
        pos += nel

    return (loss, dh0[None], *[grads[nm] for nm in names], *[deltas[nm] for nm in names],
            *[new_ms[nm] for nm in names], *[new_vs[nm] for nm in names])


HBM_SPEC = pl.BlockSpec(memory_space=pltpu.HBM)
SEM_SPEC = pl.BlockSpec(memory_space=pltpu.SEMAPHORE)
ANY_SPEC = pl.BlockSpec(memory_space=pl.ANY)
DATAFLOW = pltpu.SideEffectType.DATAFLOW_SIDE_EFFECTING
XY_FLIPS = ((0, 1), (1, 0), (1, 1))


def _after(x, *deps):
    return lax.optimization_barrier((x, *deps))[0]


def _half_rows(rows, half):
    return pl.ds(pl.multiple_of(half * (rows // 2), 8), rows // 2)


def _split_start(bufs, plan, n, name):
    nb = len(bufs)

    def body(*refs):
        send_sems, recv_sems, token = refs[nb], refs[nb + 1], refs[-1]
        for t, (src, dst, dev) in enumerate(plan(refs[:nb])):
            pltpu.make_async_remote_copy(src_ref=src, dst_ref=dst, send_sem=send_sems.at[t],
                                         recv_sem=recv_sems.at[t], device_id=dev, device_id_type=MESH).start()
        token[...] = jnp.zeros_like(token)

    out = pl.pallas_call(
        body, name=name,
        out_shape=(pltpu.SemaphoreType.DMA((n,)), pltpu.SemaphoreType.DMA((n,)),
                   *[pltpu.HBM(b.shape, b.dtype) for b in bufs], jax.ShapeDtypeStruct((8, 128), F32)),
        in_specs=[HBM_SPEC] * nb,
        out_specs=(SEM_SPEC, SEM_SPEC, *[HBM_SPEC] * nb, pl.BlockSpec(memory_space=pltpu.VMEM)),
        input_output_aliases={i: 2 + i for i in range(nb)},
        compiler_params=pltpu.CompilerParams(has_side_effects=DATAFLOW),
    )(*[pltpu.with_memory_space_constraint(b, pltpu.HBM) for b in bufs])
    return out[0], out[1], list(out[2:2 + nb]), out[-1]


def _split_wait(bufs, send_sems, recv_sems, plan, after, name):
    nb = len(bufs)

    def body(*refs):
        ss, rs = refs[nb], refs[nb + 1]
        for t, (src, dst, dev) in enumerate(plan(refs[:nb])):
            cp = pltpu.make_async_remote_copy(src_ref=src, dst_ref=dst, send_sem=ss.at[t], recv_sem=rs.at[t],
                                              device_id=dev, device_id_type=MESH)
            cp.wait_send()
            cp.wait_recv()

    out = pl.pallas_call(
        body, name=name,
        out_shape=tuple(pltpu.HBM(b.shape, b.dtype) for b in bufs),
        in_specs=[HBM_SPEC] * nb + [SEM_SPEC, SEM_SPEC] + [ANY_SPEC] * len(after),
        out_specs=tuple([HBM_SPEC] * nb),
        input_output_aliases={i: i for i in range(nb)},
        compiler_params=pltpu.CompilerParams(has_side_effects=DATAFLOW),
    )(*bufs, send_sems, recv_sems, *after)
    return list(out)


def _cast_place(w, chip_idx, name):
    R, C = w.shape
    tr = _row_tile(R, 256) if R % 256 == 0 else R

    def body(k_ref, w_ref, o_ref):
        o_ref[...] = w_ref[...].astype(BF16)

    grid_spec = pltpu.PrefetchScalarGridSpec(
        num_scalar_prefetch=1, grid=(R // tr,),
        in_specs=[pl.BlockSpec((tr, C), lambda i, k: (i, 0))],
        out_specs=pl.BlockSpec((None, tr, C), lambda i, k: (k[0], i, 0)),
    )
    return pl.pallas_call(
        body, name=name, grid_spec=grid_spec, out_shape=jax.ShapeDtypeStruct((N_CHIPS, R, C), BF16),
        compiler_params=_params(("parallel",)),
    )(chip_idx, w)


def _plan_gather(refs):
    mx, my, mc = _coords()
    me = 2 * mx + my
    plan = []
    for g in refs:
        mine = g.at[me, _half_rows(g.shape[1], mc), :]
        for fx, fy in XY_FLIPS:
            plan.append((mine, mine, (mx ^ fx, my ^ fy, mc)))
    return plan


def _forward_halves(gathered, name):
    n = len(gathered)

    def body(*refs):
        outs = refs[n:2 * n]
        send_sems, recv_sems = refs[2 * n:]
        mx, my, mc = _coords()
        sibling = (mx, my, 1 - mc)

        def piece(g, t, half):
            fx, fy = XY_FLIPS[t]
            return g.at[2 * (mx ^ fx) + (my ^ fy), _half_rows(g.shape[1], half), :]

        def copy(i, t, half):
            p = piece(outs[i], t, half)
            return pltpu.make_async_remote_copy(src_ref=p, dst_ref=p, send_sem=send_sems.at[i, t],
                                                recv_sem=recv_sems.at[i, t], device_id=sibling, device_id_type=MESH)

        sends = [copy(i, t, mc) for i in range(n) for t in range(3)]
        for cp in sends:
            cp.start()
        for i in range(n):
            for t in range(3):
                copy(i, t, 1 - mc).wait_recv()
        for cp in sends:
            cp.wait_send()

    return pl.pallas_call(
        body, name=name,
        in_specs=[ANY_SPEC] * n, out_specs=[ANY_SPEC] * n,
        out_shape=[jax.ShapeDtypeStruct(g.shape, g.dtype) for g in gathered],
        input_output_aliases={i: i for i in range(n)},
        scratch_shapes=[pltpu.SemaphoreType.DMA((n, 3)), pltpu.SemaphoreType.DMA((n, 3))],
        compiler_params=pltpu.CompilerParams(has_side_effects=True),
    )(*gathered)


def _plan_other_halves(refs):
    n = len(refs) // 2
    mx, my, mc = _coords()
    return [(g.at[pl.ds(0, N_CHIPS), _half_rows(g.shape[1], 1 - mc), :], land, (mx, my, 1 - mc))
            for g, land in zip(refs[:n], refs[n:])]


def _plan_chunks(refs):
    n = len(refs) // 2
    mx, my, mc = _coords()
    plan = []
    for s, land in zip(refs[:n], refs[n:]):
        for t, (fx, fy) in enumerate(XY_FLIPS):
            plan.append((s.at[2 * (mx ^ fx) + (my ^ fy)], land.at[t], (mx ^ fx, my ^ fy, mc)))
    return plan


def _plan_share(refs):
    mx, my, mc = _coords()
    return [(full.at[mc], full.at[mc], (mx, my, 1 - mc)) for full in refs]


def _add_half(g, recv, core_idx, name):
    nk, R, C = g.shape
    rh = R // 2
    tr = _row_tile(rh, 128) if rh % 128 == 0 else rh
    nt = rh // tr

    def body(c_ref, g_ref, r_ref, o_ref):
        o_ref[...] = g_ref[...] + r_ref[...]

    grid_spec = pltpu.PrefetchScalarGridSpec(
        num_scalar_prefetch=1, grid=(nk, nt),
        in_specs=[pl.BlockSpec((None, tr, C), lambda k, i, c: (k, c[0] * nt + i, 0)),
                  pl.BlockSpec((None, tr, C), lambda k, i, c: (k, i, 0))],
        out_specs=pl.BlockSpec((None, tr, C), lambda k, i, c: (k, i, 0)),
    )
    return pl.pallas_call(
        body, name=name, grid_spec=grid_spec, out_shape=jax.ShapeDtypeStruct((nk, rh, C), F32),
        compiler_params=_params(("parallel", "parallel")),
    )(core_idx, g, recv)


def _sum_chips(s, land, chip_core, name):
    _, rh, C = s.shape
    tr = _row_tile(rh, 128) if rh % 128 == 0 else rh

    def body(p_ref, s_ref, l_ref, o_ref):
        me = p_ref[0]
        acc = None
        for j in range(N_CHIPS):
            t = jnp.maximum(jnp.bitwise_xor(me, j) - 1, 0)
            term = jnp.where(me == j, s_ref[...], l_ref[t])
            acc = term if acc is None else acc + term
        o_ref[...] = acc

    grid_spec = pltpu.PrefetchScalarGridSpec(
        num_scalar_prefetch=1, grid=(rh // tr,),
        in_specs=[pl.BlockSpec((None, tr, C), lambda i, p: (p[0], i, 0)),
                  pl.BlockSpec((3, tr, C), lambda i, p: (0, i, 0))],
        out_specs=pl.BlockSpec((None, tr, C), lambda i, p: (p[1], i, 0)),
    )
    return pl.pallas_call(
        body, name=name, grid_spec=grid_spec, out_shape=jax.ShapeDtypeStruct((2, rh, C), F32),
        compiler_params=_params(("parallel",)),
    )(chip_core, s, land)


class _Exchange:
    def __init__(self, bufs, plan, n, name):
        self.plan, self.name = plan, name
        self.send_sems, self.recv_sems, self.bufs, self.token = _split_start(bufs, plan, n, name + "_start")

    def wait(self, after):
        return _split_wait(self.bufs, self.send_sems, self.recv_sems, self.plan, after, self.name + "_wait")


def _empty_like_hbm(shape, dtype):
    return lax.empty(shape, dtype)


def _rs_send_halves(grads, tag):
    lands = [_empty_like_hbm((g.shape[0], g.shape[1] // 2, g.shape[2]), g.dtype) for g in grads]
    return _Exchange(list(grads) + lands, _plan_other_halves, len(grads), f"rs_halves_{tag}")


def _rs_send_chunks(ex, after, core_idx, tag):
    bufs = ex.wait(after)
    n = len(bufs) // 2
    sums = [_add_half(g, r, core_idx, f"rs_add_{tag}_{i}") for i, (g, r) in enumerate(zip(bufs[:n], bufs[n:]))]
    lands = [_empty_like_hbm((3,) + s.shape[1:], s.dtype) for s in sums]
    return _Exchange(sums + lands, _plan_chunks, 3 * n, f"rs_chunks_{tag}")


def _rs_send_share(ex, after, chip_core, tag):
    bufs = ex.wait(after)
    n = len(bufs) // 2
    fulls = [_sum_chips(s, l, chip_core, f"rs_sum_{tag}_{i}") for i, (s, l) in enumerate(zip(bufs[:n], bufs[n:]))]
    return _Exchange(fulls, _plan_share, n, f"rs_share_{tag}")


def _rs_finish(ex, after):
    return [b.reshape(2 * b.shape[1], b.shape[2]) for b in ex.wait(after)]


def _ffn_backward_rs(dh_out, saved, gain, sc, gate, wg, wu, wd, core_idx, tag):
    h, n, ga, up, act, f = saved
    df, d_gate = _gate_bwd(dh_out, f, gate, 0.5, f"{tag}_gate_bwd")
    dga, dup = _ffn_dact(df, wd, ga, up, f"{tag}_dact")
    dwd = _wgrad_chunk_lhs(act, df, f"{tag}_dwd")
    dwg, dwu = _wgrad_chunk_rhs(n, [dga, dup], wg.shape[0], True, f"{tag}_dwgu")
    ex = _rs_send_halves([dwg, dwu, dwd], tag)
    dn = _mm_nt_reduce([_after(dga, ex.token), dup], [wg, wu], True, f"{tag}_dn")
    ex = _rs_send_chunks(ex, [dn], core_idx, tag)
    dh_in, d_sh, d_sc, d_gain = _norm_mod_bwd(_after(dn, ex.token), h, gain, sc, dh_out, f"{tag}_norm_bwd")
    return dh_in, ex, (d_sh, d_sc, d_gate, d_gain)


def _mixer_backward_rs(dh2, saved, mix_norm_g, sc2, gt2, win, wout, conv_w, conv_ln_g, conv_ln_b, attn_out_g,
                       conv_out_g, core_idx):
    h1, n2, proj, cos, sin, q, k, v, attn, lse_tot, u1, y, mo = saved
    S, D = h1.shape
    dmo, d_gt2 = _gate_bwd(dh2, mo, gt2, 1.0, "mix_gate_bwd")
    dwout = _wgrad_chunk_lhs(y[None], dmo, "mix_dwout")
    dy = _mm_nt(dmo, wout.reshape(D, D), "mix_dy")
    dattn, delta, d_attn_g = _attn_merge_bwd(dy, attn, attn_out_g, "attn_merge_bwd")
    dqs, dks, dvs = [], [], []
    for dil in DILATIONS:
        dq_, dk_, dv_ = _attn_bwd(q, k, v, dattn, lse_tot, delta, dil, f"attn_bwd_d{dil}")
        dqs.append(dq_)
        dks.append(dk_)
        dvs.append(dv_)
    dq = _rope_bwd(dqs, cos, sin, HEAD_DIM ** -0.5, True, "dq_rope")
    dk = _rope_bwd(dks, cos, sin, 1.0, True, "dk_rope")
    dv = _rope_bwd(dvs, cos, sin, 1.0, False, "dv_sum")
    du1, d_gco, d_lng, d_lnb, d_cb = _conv_bwd_norms(dy, u1, conv_ln_g, conv_ln_b, conv_out_g, "conv_bwd_norms")
    dga_, dgb_, d_cw = _conv_bwd_taps(du1, proj, conv_w, "conv_bwd_taps")
    dproj = jnp.concatenate([dq, dk, dv, dga_, dgb_], axis=1)
    (dwin,) = _wgrad_chunk_rhs(n2, [dproj], N_CHIPS, False, "mix_dwin")
    ex = _rs_send_halves([dwin, dwout.reshape(N_CHIPS, D // N_CHIPS, D)], "mix")
    dn2 = _mm_nt_reduce([_after(dproj, ex.token)], [win], False, "mix_dn")
    ex = _rs_send_chunks(ex, [dn2], core_idx, "mix")
    dh1, d_sh2, d_sc2, d_gain2 = _norm_mod_bwd(_after(dn2, ex.token), h1, mix_norm_g, sc2, dh2, "mix_norm_bwd")
    small = (d_sh2, d_sc2, d_gt2, d_gain2, d_cb, d_lng, d_lnb, d_attn_g, d_gco, d_cw)
    return dh1, ex, small


def kernel(x, c, w_ada, b_ada, ffn1_norm_g, ffn1_w_gate, ffn1_w_up, ffn1_w_down, mix_norm_g, w_in, conv_dw_w, conv_dw_b, conv_ln_g, conv_ln_b, attn_out_g, conv_out_g, w_out, ffn2_norm_g, ffn2_w_gate, ffn2_w_up, ffn2_w_down, final_norm_g, loss_target, m_w_ada, m_b_ada, m_ffn1_norm_g, m_ffn1_w_gate, m_ffn1_w_up, m_ffn1_w_down, m_mix_norm_g, m_w_in, m_conv_dw_w, m_conv_dw_b, m_conv_ln_g, m_conv_ln_b, m_attn_out_g, m_conv_out_g, m_w_out, m_ffn2_norm_g, m_ffn2_w_gate, m_ffn2_w_up, m_ffn2_w_down, m_final_norm_g, v_w_ada, v_b_ada, v_ffn1_norm_g, v_ffn1_w_gate, v_ffn1_w_up, v_ffn1_w_down, v_mix_norm_g, v_w_in, v_conv_dw_w, v_conv_dw_b, v_conv_ln_g, v_conv_ln_b, v_attn_out_g, v_conv_out_g, v_w_out, v_ffn2_norm_g, v_ffn2_w_gate, v_ffn2_w_up, v_ffn2_w_down, v_final_norm_g):
    S, D = x.shape[1], x.shape[2]
    mx, my, mc = _coords()
    chip = 2 * mx + my
    dev = 4 * mx + 2 * my + mc
    chip_idx = chip.astype(jnp.int32).reshape(1)
    core_idx = mc.astype(jnp.int32).reshape(1)
    chip_core = jnp.stack([chip, mc]).astype(jnp.int32)
    h0 = x[0]
    target = loss_target[0]

    def gather_start(ws, tag, *deps):
        slots = [_cast_place(w[0], chip_idx, f"cast_{tag}_{i}") for i, w in enumerate(ws)]
        slots[0] = _after(slots[0], *deps)
        return _Exchange(slots, _plan_gather, 3 * len(ws), f"gather_{tag}")

    def gather_finish(ex, after, tag):
        return _forward_halves(ex.wait(after), f"forward_{tag}")

    ex_w1 = gather_start([ffn1_w_gate, ffn1_w_up, ffn1_w_down], "ffn1")
    ex_wm = gather_start([w_in, w_out], "mix", ex_w1.token)
    ex_w2 = gather_start([ffn2_w_gate, ffn2_w_up, ffn2_w_down], "ffn2", ex_wm.token)

    ncw = CONV_KERNEL * 128
    n0 = -(-(D + ncw) // 1024) * 1024
    pk0 = _pad_cols(jnp.concatenate([c.reshape(1, D), conv_dw_w.reshape(1, ncw)], axis=1), n0)
    g0 = _all_gather8(_after(pk0, ex_w2.token).reshape(8, n0 // 8), "gather_c").reshape(N_DEV, n0)
    c_all = g0[:, :D]
    conv_w = jnp.concatenate([g0[2 * kc, D:D + ncw].reshape(CONV_KERNEL, 128) for kc in range(N_CHIPS)], axis=1)
    conv_w = jnp.pad(conv_w, ((0, HALO - CONV_KERNEL), (0, 0)))
    nmod = w_ada.shape[2]
    b_shard = lax.dynamic_slice(b_ada, (0, chip * nmod), (1, nmod))
    mod_part = _ada_fwd(c_all, w_ada[0], b_shard, "ada_fwd")
    g1 = _all_gather8(mod_part, "gather_mod")
    mod_all = jnp.concatenate([g1[16 * kc:16 * kc + 8] for kc in range(N_CHIPS)], axis=1)
    mod = lax.dynamic_slice(mod_all, (dev, 0), (1, 9 * D))
    sh1, sc1, gt1, sh2, sc2, gt2, sh3, sc3, gt3 = [mod[:, i * D:(i + 1) * D] for i in range(9)]

    wg1, wu1, wd1 = gather_finish(ex_w1, [mod], "ffn1")
    h1, saved1 = _ffn_forward(h0, ffn1_norm_g, sc1, sh1, gt1, wg1, wu1, wd1, "ffn1")
    win, wout = gather_finish(ex_wm, [h1], "mix")
    h2, saved2 = _mixer_forward(h1, mix_norm_g, sc2, sh2, gt2, win, wout, conv_w, conv_dw_b, conv_ln_g, conv_ln_b,
                                attn_out_g, conv_out_g)
    wg2, wu2, wd2 = gather_finish(ex_w2, [h2], "ffn2")
    h3, saved3 = _ffn_forward(h2, ffn2_norm_g, sc3, sh3, gt3, wg2, wu2, wd2, "ffn2")
    loss_part, dh3, d_final_g = _loss_head(h3, final_norm_g.reshape(1, D), target, "loss_head")

    dh2, ex_g2, (d_sh3, d_sc3, d_gt3, d_gain3) = _ffn_backward_rs(
        dh3, saved3, ffn2_norm_g, sc3, gt3, wg2, wu2, wd2, core_idx, "ffn2")
    dh1, ex_gm, small_mix = _mixer_backward_rs(
        dh2, saved2, mix_norm_g, sc2, gt2, win, wout, conv_w, conv_ln_g, conv_ln_b, attn_out_g, conv_out_g, core_idx)
    d_sh2, d_sc2, d_gt2, d_gain2, d_cb, d_lng, d_lnb, d_attn_g, d_gco, d_cw = small_mix
    dh0, ex_g1, (d_sh1, d_sc1, d_gt1, d_gain1) = _ffn_backward_rs(
        dh1, saved1, ffn1_norm_g, sc1, gt1, wg1, wu1, wd1, core_idx, "ffn1")
    ex_g2 = _rs_send_share(ex_g2, [dh0], chip_core, "ffn2")
    ex_gm = _rs_send_share(ex_gm, [ex_g2.token], chip_core, "mix")

    dmod = jnp.concatenate([d_sh1, d_sc1, d_gt1, d_sh2, d_sc2, d_gt2, d_sh3, d_sc3, d_gt3], axis=1)
    small = [d_gain1, d_gain2, d_gain3, d_final_g, d_cb, d_lng, d_lnb, d_attn_g, d_gco,
             d_cw[:CONV_KERNEL].reshape(1, CONV_KERNEL * 512), loss_part]
    pk1 = jnp.concatenate([dmod] + small, axis=1)
    n1 = -(-pk1.shape[1] // 1024) * 1024
    pk1 = _after(_pad_cols(pk1, n1), ex_gm.token)
    gathered = _all_gather8(pk1.reshape(8, n1 // 8), "gather_small").reshape(N_DEV, n1)
    tot = _sum_rows(gathered, "sum_small")
    off = [0]

    def take(nel):
        out = tot[:, off[0]:off[0] + nel]
        off[0] += nel
        return out

    g_b_ada = take(9 * D)
    g_ffn1_norm, g_mix_norm, g_ffn2_norm, g_final = take(D), take(D), take(D), take(D)
    g_cb, g_lng, g_lnb, g_attn_g, g_gco = take(512), take(512), take(512), take(512), take(512)
    g_cw_full = take(CONV_KERNEL * 512).reshape(CONV_KERNEL, 512)
    loss = take(128)[0, 0]
    g_cw = lax.dynamic_slice(g_cw_full, (0, chip * 128), (CONV_KERNEL, 128))

    dmod_shard = lax.dynamic_slice(gathered[:, :9 * D], (0, chip * nmod), (N_DEV, nmod))
    dmod16 = jnp.pad(dmod_shard, ((0, N_DEV), (0, 0)))
    c_t16 = jnp.pad(c_all.T, ((0, 0), (0, N_DEV)))
    g_w_ada = _ada_wgrad(c_t16, dmod16, "ada_wgrad")

    names = ["w_ada", "b_ada", "ffn1_norm_g", "ffn1_w_gate", "ffn1_w_up", "ffn1_w_down", "mix_norm_g", "w_in",
             "conv_dw_w", "conv_dw_b", "conv_ln_g", "conv_ln_b", "attn_out_g", "conv_out_g", "w_out", "ffn2_norm_g",
             "ffn2_w_gate", "ffn2_w_up", "ffn2_w_down", "final_norm_g"]
    weights = dict(zip(names, [w_ada, b_ada, ffn1_norm_g, ffn1_w_gate, ffn1_w_up, ffn1_w_down, mix_norm_g, w_in,
                               conv_dw_w, conv_dw_b, conv_ln_g, conv_ln_b, attn_out_g, conv_out_g, w_out,
                               ffn2_norm_g, ffn2_w_gate, ffn2_w_up, ffn2_w_down, final_norm_g]))
    ms = dict(zip(names, [m_w_ada, m_b_ada, m_ffn1_norm_g, m_ffn1_w_gate, m_ffn1_w_up, m_ffn1_w_down, m_mix_norm_g,
                          m_w_in, m_conv_dw_w, m_conv_dw_b, m_conv_ln_g, m_conv_ln_b, m_attn_out_g, m_conv_out_g,
                          m_w_out, m_ffn2_norm_g, m_ffn2_w_gate, m_ffn2_w_up, m_ffn2_w_down, m_final_norm_g]))
    vs = dict(zip(names, [v_w_ada, v_b_ada, v_ffn1_norm_g, v_ffn1_w_gate, v_ffn1_w_up, v_ffn1_w_down, v_mix_norm_g,
                          v_w_in, v_conv_dw_w, v_conv_dw_b, v_conv_ln_g, v_conv_ln_b, v_attn_out_g, v_conv_out_g,
                          v_w_out, v_ffn2_norm_g, v_ffn2_w_gate, v_ffn2_w_up, v_ffn2_w_down, v_final_norm_g]))
    grads, deltas, new_ms, new_vs = {}, {}, {}, {}

    def adamw_big(nm, g2d):
        shape = weights[nm].shape
        two_d = (shape[-2], shape[-1])
        d_, m_, v_ = _adamw(weights[nm].reshape(two_d), g2d.reshape(two_d), ms[nm].reshape(two_d),
                            vs[nm].reshape(two_d), f"adamw_{nm}")
        grads[nm], deltas[nm], new_ms[nm], new_vs[nm] = (t.reshape(shape) for t in (g2d, d_, m_, v_))
        return d_

    d_ada = adamw_big("w_ada", g_w_ada)
    small_grads = {"b_ada": g_b_ada, "ffn1_norm_g": g_ffn1_norm, "mix_norm_g": g_mix_norm, "conv_dw_w": g_cw,
                   "conv_dw_b": g_cb, "conv_ln_g": g_lng, "conv_ln_b": g_lnb, "attn_out_g": g_attn_g,
                   "conv_out_g": g_gco, "ffn2_norm_g": g_ffn2_norm, "final_norm_g": g_final}
    small_names = [nm for nm in names if nm in small_grads]

    def pack_small(arrs):
        flat = jnp.concatenate([arrs[nm].reshape(1, -1) for nm in small_names], axis=1)
        npad = -(-flat.shape[1] // 1024) * 1024
        return _pad_cols(flat, npad).reshape(8, npad // 8)

    d_s, m_s, v_s = _adamw(pack_small(weights), pack_small(small_grads), pack_small(ms), pack_small(vs),
                           "adamw_small")
    pos = 0
    for nm in small_names:
        shape, nel = weights[nm].shape, weights[nm].size
        grads[nm] = small_grads[nm].reshape(shape)
        deltas[nm], new_ms[nm], new_vs[nm] = (t.reshape(1, -1)[:, pos:pos + nel].reshape(shape)
                                              for t in (d_s, m_s, v_s))
        pos += nel

    g_wg2, g_wu2, g_wd2 = _rs_finish(ex_g2, [d_ada, d_s])
    last = [adamw_big(nm, g) for nm, g in (("ffn2_w_gate", g_wg2), ("ffn2_w_up", g_wu2), ("ffn2_w_down", g_wd2))]
    g_win, g_wout = _rs_finish(ex_gm, last)
    last = [adamw_big(nm, g) for nm, g in (("w_in", g_win), ("w_out", g_wout))]
    ex_g1 = _rs_send_share(ex_g1, last, chip_core, "ffn1")
    g_wg1, g_wu1, g_wd1 = _rs_finish(ex_g1, [])
    for nm, g in (("ffn1_w_gate", g_wg1), ("ffn1_w_up", g_wu1), ("ffn1_w_down", g_wd1)):
        adamw_big(nm, g)

    return (loss, dh0[None], *[grads[nm] for nm in names], *[deltas[nm] for nm in names],
            *[new_ms[nm] for nm in names], *[new_vs[nm] for nm in names])
```

```python
import jax
import jax.numpy as jnp
from jax import lax
from jax.experimental import pallas as pl
from jax.experimental.pallas import tpu as pltpu

F32 = jnp.float32
BF16 = jnp.bfloat16
MESH = pl.DeviceIdType.MESH

RMS_EPS = 1e-6
LN_EPS = 1e-5
HEAD_DIM = 64
ATTN_BLOCK = 128
DILATIONS = (1, 4, 16)
ROPE_THETA = 10000.0
CONV_KERNEL = 31
HALO = 32
N_CHIPS = 4
N_DEV = 8
ADAM_LR, ADAM_B1, ADAM_B2, ADAM_EPS, ADAM_WD, ADAM_STEP = 0.001, 0.9, 0.999, 1e-08, 0.01, 10
VMEM_LIMIT_BYTES = 48 * 1024 * 1024
NEG = -1e30

NT = (((1,), (1,)), ((), ()))
TN = (((0,), (0,)), ((), ()))


def _params(sem=None):
    return pltpu.CompilerParams(dimension_semantics=sem, vmem_limit_bytes=VMEM_LIMIT_BYTES)


def _row_tile(rows, want):
    t = min(rows, want)
    assert rows % t == 0
    return t


def _sigmoid(x):
    return 1.0 / (1.0 + jnp.exp(-x))


def _blind_to(body, n_in, n_dep):
    def wrapped(*refs):
        return body(*refs[:n_in], *refs[n_in + n_dep:])
    return wrapped


def _vec_spec(d, ngrid):
    if ngrid == 1:
        return pl.BlockSpec((1, d), lambda i: (0, 0))
    return pl.BlockSpec((1, d), lambda i, j: (0, 0))


def _norm_mod(h, gain, sc, sh, name):
    S, D = h.shape
    tr = _row_tile(S, 512)

    def body(h_ref, g_ref, sc_ref, sh_ref, n_ref):
        x = h_ref[...]
        r = lax.rsqrt(jnp.mean(x * x, axis=-1, keepdims=True) + RMS_EPS)
        y = (x * r) * g_ref[...]
        n_ref[...] = (y * (1.0 + sc_ref[...]) + sh_ref[...]).astype(BF16)

    row = pl.BlockSpec((tr, D), lambda i: (i, 0))
    return pl.pallas_call(
        body, name=name, grid=(S // tr,),
        in_specs=[row, _vec_spec(D, 1), _vec_spec(D, 1), _vec_spec(D, 1)],
        out_specs=row, out_shape=jax.ShapeDtypeStruct((S, D), BF16),
        compiler_params=_params(("parallel",)),
    )(h, gain, sc, sh)


def _norm_mod_bwd(dn, h_in, gain, sc, dh_out, name, deps=()):
    S, D = h_in.shape
    tr = _row_tile(S, 512)

    def body(dn_ref, h_ref, g_ref, sc_ref, dho_ref, dh_ref, dsh_ref, dsc_ref, dg_ref):
        @pl.when(pl.program_id(0) == 0)
        def _():
            dsh_ref[...] = jnp.zeros_like(dsh_ref)
            dsc_ref[...] = jnp.zeros_like(dsc_ref)
            dg_ref[...] = jnp.zeros_like(dg_ref)

        x = h_ref[...]
        dn_ = dn_ref[...]
        g = g_ref[...]
        one_sc = 1.0 + sc_ref[...]
        r = lax.rsqrt(jnp.mean(x * x, axis=-1, keepdims=True) + RMS_EPS)
        xh = x * r
        dsh_ref[...] += jnp.sum(dn_, axis=0, keepdims=True)
        dsc_ref[...] += jnp.sum(dn_ * (xh * g), axis=0, keepdims=True)
        dg_ref[...] += jnp.sum(dn_ * one_sc * xh, axis=0, keepdims=True)
        dxh = dn_ * (g * one_sc)
        dh_ref[...] = dho_ref[...] + r * (dxh - xh * jnp.mean(dxh * xh, axis=-1, keepdims=True))

    row = pl.BlockSpec((tr, D), lambda i: (i, 0))
    vec = _vec_spec(D, 1)
    return pl.pallas_call(
        _blind_to(body, 5, len(deps)), name=name, grid=(S // tr,),
        in_specs=[row, row, vec, vec, row] + [pl.BlockSpec(memory_space=pl.ANY)] * len(deps),
        out_specs=[row, vec, vec, vec],
        out_shape=[jax.ShapeDtypeStruct((S, D), F32)] + [jax.ShapeDtypeStruct((1, D), F32)] * 3,
        compiler_params=_params(("arbitrary",)),
    )(dn, h_in, gain, sc, dh_out, *deps)


def _gate_bwd(dh, f, gvec, coef, name):
    S, D = dh.shape
    tr = _row_tile(S, 512)

    def body(dh_ref, f_ref, g_ref, df_ref, dg_ref):
        @pl.when(pl.program_id(0) == 0)
        def _():
            dg_ref[...] = jnp.zeros_like(dg_ref)

        dh_ = dh_ref[...]
        df_ref[...] = ((coef * g_ref[...]) * dh_).astype(BF16)
        dg_ref[...] += jnp.sum(coef * dh_ * f_ref[...].astype(F32), axis=0, keepdims=True)

    row = pl.BlockSpec((tr, D), lambda i: (i, 0))
    vec = _vec_spec(D, 1)
    return pl.pallas_call(
        body, name=name, grid=(S // tr,),
        in_specs=[row, row, vec], out_specs=[row, vec],
        out_shape=[jax.ShapeDtypeStruct((S, D), BF16), jax.ShapeDtypeStruct((1, D), F32)],
        compiler_params=_params(("arbitrary",)),
    )(dh, f, gvec)


def _loss_head(h, gain, target, name):
    S, D = h.shape
    tr = _row_tile(S, 512)

    def body(h_ref, g_ref, t_ref, loss_ref, dh_ref, dg_ref):
        @pl.when(pl.program_id(0) == 0)
        def _():
            loss_ref[...] = jnp.zeros_like(loss_ref)
            dg_ref[...] = jnp.zeros_like(dg_ref)

        x = h_ref[...]
        g = g_ref[...]
        r = lax.rsqrt(jnp.mean(x * x, axis=-1, keepdims=True) + RMS_EPS)
        xh = x * r
        err = xh * g - t_ref[...]
        part = 0.5 * jnp.sum(jnp.mean(err * err, axis=-1, keepdims=True), axis=0, keepdims=True)
        loss_ref[...] += jnp.broadcast_to(part, loss_ref.shape)
        dy = err * (1.0 / D)
        dg_ref[...] += jnp.sum(dy * xh, axis=0, keepdims=True)
        dxh = dy * g
        dh_ref[...] = r * (dxh - xh * jnp.mean(dxh * xh, axis=-1, keepdims=True))

    row = pl.BlockSpec((tr, D), lambda i: (i, 0))
    vec = _vec_spec(D, 1)
    return pl.pallas_call(
        body, name=name, grid=(S // tr,),
        in_specs=[row, vec, row],
        out_specs=[pl.BlockSpec((1, 128), lambda i: (0, 0)), row, vec],
        out_shape=[jax.ShapeDtypeStruct((1, 128), F32), jax.ShapeDtypeStruct((S, D), F32),
                   jax.ShapeDtypeStruct((1, D), F32)],
        compiler_params=_params(("arbitrary",)),
    )(h, gain, target)


def _ffn_gate_up(n, wg, wu, name):
    S, D = n.shape
    nk, _, w = wg.shape
    tm = _row_tile(S, 512)

    def body(n_ref, wg_ref, wu_ref, ga_ref, up_ref, act_ref):
        x = n_ref[...]
        ga = jnp.dot(x, wg_ref[...], preferred_element_type=F32)
        up = jnp.dot(x, wu_ref[...], preferred_element_type=F32)
        ga_ref[...] = ga.astype(BF16)
        up_ref[...] = up.astype(BF16)
        act_ref[...] = ((ga * _sigmoid(ga)) * up).astype(BF16)

    wspec = pl.BlockSpec((None, D, w), lambda k, m: (k, 0, 0))
    ospec = pl.BlockSpec((None, tm, w), lambda k, m: (k, m, 0))
    out = jax.ShapeDtypeStruct((nk, S, w), BF16)
    return pl.pallas_call(
        body, name=name, grid=(nk, S // tm),
        in_specs=[pl.BlockSpec((tm, D), lambda k, m: (m, 0)), wspec, wspec],
        out_specs=[ospec, ospec, ospec], out_shape=[out, out, out],
        compiler_params=_params(("parallel", "parallel")),
    )(n, wg, wu)


def _mm_residual(lhs, w, h_in, gvec, coef, name):
    nk, S, kc = lhs.shape
    D = w.shape[2]
    tm = _row_tile(S, 512)

    def body(l_ref, w_ref, h_ref, g_ref, ho_ref, f_ref, acc_ref):
        k = pl.program_id(1)

        @pl.when(k == 0)
        def _():
            acc_ref[...] = jnp.zeros_like(acc_ref)

        acc_ref[...] += jnp.dot(l_ref[...], w_ref[...], preferred_element_type=F32)

        @pl.when(k == nk - 1)
        def _():
            f = acc_ref[...]
            f_ref[...] = f.astype(BF16)
            ho_ref[...] = h_ref[...] + (coef * g_ref[...]) * f

    row = pl.BlockSpec((tm, D), lambda m, k: (m, 0))
    return pl.pallas_call(
        body, name=name, grid=(S // tm, nk),
        in_specs=[pl.BlockSpec((None, tm, kc), lambda m, k: (k, m, 0)),
                  pl.BlockSpec((None, kc, D), lambda m, k: (k, 0, 0)), row, _vec_spec(D, 2)],
        out_specs=[row, row],
        out_shape=[jax.ShapeDtypeStruct((S, D), F32), jax.ShapeDtypeStruct((S, D), BF16)],
        scratch_shapes=[pltpu.VMEM((tm, D), F32)],
        compiler_params=_params(("parallel", "arbitrary")),
    )(lhs, w, h_in, gvec)


def _mm_cols(n, w, name):
    S, D = n.shape
    nk, _, wd = w.shape
    assert wd % 128 == 0
    tm = _row_tile(S, 512)

    def body(n_ref, w_ref, o_ref):
        o_ref[...] = jnp.dot(n_ref[...], w_ref[...], preferred_element_type=F32)

    return pl.pallas_call(
        body, name=name, grid=(nk, S // tm),
        in_specs=[pl.BlockSpec((tm, D), lambda k, m: (m, 0)), pl.BlockSpec((None, D, wd), lambda k, m: (k, 0, 0))],
        out_specs=pl.BlockSpec((tm, wd), lambda k, m: (m, k)),
        out_shape=jax.ShapeDtypeStruct((S, nk * wd), F32),
        compiler_params=_params(("parallel", "parallel")),
    )(n, w)


def _ffn_dact(df, wd, ga, up, name):
    S, D = df.shape
    nk, w, _ = wd.shape
    tm = _row_tile(S, 512)

    def body(df_ref, wd_ref, ga_ref, up_ref, dga_ref, dup_ref):
        dact = lax.dot_general(df_ref[...], wd_ref[...], NT, preferred_element_type=F32)
        ga_ = ga_ref[...].astype(F32)
        up_ = up_ref[...].astype(F32)
        sig = _sigmoid(ga_)
        dga_ref[...] = (dact * up_ * (sig * (1.0 + ga_ * (1.0 - sig)))).astype(BF16)
        dup_ref[...] = (dact * (ga_ * sig)).astype(BF16)

    cspec = pl.BlockSpec((None, tm, w), lambda k, m: (k, m, 0))
    out = jax.ShapeDtypeStruct((nk, S, w), BF16)
    return pl.pallas_call(
        body, name=name, grid=(nk, S // tm),
        in_specs=[pl.BlockSpec((tm, D), lambda k, m: (m, 0)), pl.BlockSpec((None, w, D), lambda k, m: (k, 0, 0)),
                  cspec, cspec],
        out_specs=[cspec, cspec], out_shape=[out, out],
        compiler_params=_params(("parallel", "parallel")),
    )(df, wd, ga, up)


def _mm_nt(d, w, name):
    S, K = d.shape
    N = w.shape[0]
    tm = _row_tile(S, 512)

    def body(d_ref, w_ref, o_ref):
        o_ref[...] = lax.dot_general(d_ref[...], w_ref[...], NT, preferred_element_type=F32)

    return pl.pallas_call(
        body, name=name, grid=(S // tm,),
        in_specs=[pl.BlockSpec((tm, K), lambda m: (m, 0)), pl.BlockSpec((N, K), lambda m: (0, 0))],
        out_specs=pl.BlockSpec((tm, N), lambda m: (m, 0)),
        out_shape=jax.ShapeDtypeStruct((S, N), F32),
        compiler_params=_params(("parallel",)),
    )(d, w)


def _mm_nt_reduce(lhs_list, w_list, chunked3d, name, deps=()):
    nk, D, kc = w_list[0].shape
    S = lhs_list[0].shape[1] if chunked3d else lhs_list[0].shape[0]
    tm = _row_tile(S, 512)
    npair = len(lhs_list)

    def body(*refs):
        l_refs, w_refs = refs[:npair], refs[npair:2 * npair]
        o_ref, acc_ref = refs[2 * npair], refs[2 * npair + 1]
        k = pl.program_id(1)

        @pl.when(k == 0)
        def _():
            acc_ref[...] = jnp.zeros_like(acc_ref)

        for l_ref, w_ref in zip(l_refs, w_refs):
            acc_ref[...] += lax.dot_general(l_ref[...], w_ref[...], NT, preferred_element_type=F32)

        @pl.when(k == nk - 1)
        def _():
            o_ref[...] = acc_ref[...]

    if chunked3d:
        lspec = pl.BlockSpec((None, tm, kc), lambda m, k: (k, m, 0))
    else:
        lspec = pl.BlockSpec((tm, kc), lambda m, k: (m, k))
    wspec = pl.BlockSpec((None, D, kc), lambda m, k: (k, 0, 0))
    return pl.pallas_call(
        _blind_to(body, 2 * npair, len(deps)), name=name, grid=(S // tm, nk),
        in_specs=[lspec] * npair + [wspec] * npair + [pl.BlockSpec(memory_space=pl.ANY)] * len(deps),
        out_specs=pl.BlockSpec((tm, D), lambda m, k: (m, 0)),
        out_shape=jax.ShapeDtypeStruct((S, D), F32),
        scratch_shapes=[pltpu.VMEM((tm, D), F32)],
        compiler_params=_params(("parallel", "arbitrary")),
    )(*lhs_list, *w_list, *deps)


def _wgrad_chunk_lhs(lhs, rhs, name):
    nk, S, w = lhs.shape
    D = rhs.shape[1]
    ts = _row_tile(S, 512)
    ns = S // ts

    def body(l_ref, r_ref, o_ref, acc_ref):
        s = pl.program_id(1)

        @pl.when(s == 0)
        def _():
            acc_ref[...] = jnp.zeros_like(acc_ref)

        acc_ref[...] += lax.dot_general(l_ref[...], r_ref[...], TN, preferred_element_type=F32)

        @pl.when(s == ns - 1)
        def _():
            o_ref[...] = acc_ref[...]

    return pl.pallas_call(
        body, name=name, grid=(nk, ns),
        in_specs=[pl.BlockSpec((None, ts, w), lambda k, s: (k, s, 0)), pl.BlockSpec((ts, D), lambda k, s: (s, 0))],
        out_specs=pl.BlockSpec((None, w, D), lambda k, s: (k, 0, 0)),
        out_shape=jax.ShapeDtypeStruct((nk, w, D), F32),
        scratch_shapes=[pltpu.VMEM((w, D), F32)],
        compiler_params=_params(("parallel", "arbitrary")),
    )(lhs, rhs)


def _wgrad_chunk_rhs(lhs, rhs_list, nk, chunked3d, name):
    S, D = lhs.shape
    w = rhs_list[0].shape[2] if chunked3d else rhs_list[0].shape[1] // nk
    ts = _row_tile(S, 512)
    ns = S // ts
    nr = len(rhs_list)

    def body(*refs):
        l_ref, r_refs = refs[0], refs[1:1 + nr]
        o_refs, acc_refs = refs[1 + nr:1 + 2 * nr], refs[1 + 2 * nr:]
        s = pl.program_id(1)

        @pl.when(s == 0)
        def _():
            for acc_ref in acc_refs:
                acc_ref[...] = jnp.zeros_like(acc_ref)

        x = l_ref[...]
        for r_ref, acc_ref in zip(r_refs, acc_refs):
            acc_ref[...] += lax.dot_general(x, r_ref[...], TN, preferred_element_type=F32)

        @pl.when(s == ns - 1)
        def _():
            for o_ref, acc_ref in zip(o_refs, acc_refs):
                o_ref[...] = acc_ref[...]

    if chunked3d:
        rspec = pl.BlockSpec((None, ts, w), lambda k, s: (k, s, 0))
    else:
        rspec = pl.BlockSpec((ts, w), lambda k, s: (s, k))
    ospec = pl.BlockSpec((None, D, w), lambda k, s: (k, 0, 0))
    return pl.pallas_call(
        body, name=name, grid=(nk, ns),
        in_specs=[pl.BlockSpec((ts, D), lambda k, s: (s, 0))] + [rspec] * nr,
        out_specs=[ospec] * nr,
        out_shape=[jax.ShapeDtypeStruct((nk, D, w), F32)] * nr,
        scratch_shapes=[pltpu.VMEM((D, w), F32)] * nr,
        compiler_params=_params(("parallel", "arbitrary")),
    )(lhs, *rhs_list)


def _rope_tables(S):
    pos = jnp.arange(S, dtype=F32)
    inv_freq = ROPE_THETA ** (-jnp.arange(0, HEAD_DIM, 2, dtype=F32) / HEAD_DIM)
    ang = pos[:, None] * inv_freq[None, :]
    cos, sin = jnp.cos(ang), jnp.sin(ang)
    cos2 = jnp.concatenate([cos, cos, cos, cos], axis=1)
    sin2 = jnp.concatenate([-sin, sin, -sin, sin], axis=1)
    return cos2, sin2


def _rotate(t, cos, sin_signed):
    half = HEAD_DIM // 2
    lane = lax.broadcasted_iota(jnp.int32, t.shape, 1)
    first = (lane % HEAD_DIM) < half
    partner = jnp.where(first, pltpu.roll(t, 128 - half, 1), pltpu.roll(t, half, 1))
    return t * cos + partner * sin_signed


def _qkv_rope(proj, cos, sin, name):
    S = proj.shape[0]
    A = 512
    tr = _row_tile(S, 512)
    nb = A // 128
    scale = HEAD_DIM ** -0.5

    def body(q_ref, k_ref, v_ref, c_ref, s_ref, qo_ref, ko_ref, vo_ref):
        c, s = c_ref[...], s_ref[...]
        qo_ref[...] = (_rotate(q_ref[...], c, s) * scale).astype(BF16)
        ko_ref[...] = _rotate(k_ref[...], c, s).astype(BF16)
        vo_ref[...] = v_ref[...].astype(BF16)

    def col(off):
        return pl.BlockSpec((tr, 128), lambda i, j: (i, off + j))

    tab = pl.BlockSpec((tr, 128), lambda i, j: (i, 0))
    out = jax.ShapeDtypeStruct((S, A), BF16)
    return pl.pallas_call(
        body, name=name, grid=(S // tr, nb),
        in_specs=[col(0), col(nb), col(2 * nb), tab, tab],
        out_specs=[col(0), col(0), col(0)], out_shape=[out, out, out],
        compiler_params=_params(("parallel", "parallel")),
    )(proj, proj, proj, cos, sin)


def _rope_bwd(parts, cos, sin, scale, rotate, name):
    S, A = parts[0].shape
    tr = _row_tile(S, 512)

    def body(a_ref, b_ref, c_ref, cos_ref, sin_ref, o_ref):
        t = (a_ref[...] + b_ref[...]) + c_ref[...]
        if rotate:
            t = _rotate(t, cos_ref[...], -sin_ref[...])
        o_ref[...] = (t * scale).astype(BF16)

    col = pl.BlockSpec((tr, 128), lambda i, j: (i, j))
    tab = pl.BlockSpec((tr, 128), lambda i, j: (i, 0))
    return pl.pallas_call(
        body, name=name, grid=(S // tr, A // 128),
        in_specs=[col, col, col, tab, tab], out_specs=col,
        out_shape=jax.ShapeDtypeStruct((S, A), BF16),
        compiler_params=_params(("parallel", "parallel")),
    )(*parts, cos, sin)


def _band_masks(T, has_prev):
    qi = lax.broadcasted_iota(jnp.int32, (T, T), 0)
    kj = lax.broadcasted_iota(jnp.int32, (T, T), 1)
    return qi >= kj, (kj >= qi) & has_prev


def _attn_fwd(q, k, v, dilation, name):
    S, A = q.shape
    L = S // dilation
    cols = dilation * A
    T = min(ATTN_BLOCK, L)
    nb = L // T
    assert L % T == 0 and (T == ATTN_BLOCK or nb == 1)
    q, k, v = (t.reshape(L, cols) for t in (q, k, v))

    def body(q_ref, k_ref, v_ref, o_ref, lse_ref):
        lane = lax.broadcasted_iota(jnp.int32, (1, 128), 1)
        head0 = lane < HEAD_DIM

        def step(n, carry):
            rows = pl.ds(pl.multiple_of(n * T, T), T)
            prev = pl.ds(pl.multiple_of(jnp.maximum(n - 1, 0) * T, T), T)
            qb = q_ref[rows, :]
            kc, vc = k_ref[rows, :], v_ref[rows, :]
            kp, vp = k_ref[prev, :], v_ref[prev, :]
            valid_d, valid_o = _band_masks(T, n > 0)
            outs, lses = [], []
            for hmask in (head0, jnp.logical_not(head0)):
                qh = jnp.where(hmask, qb, jnp.zeros_like(qb))
                sd = jnp.where(valid_d, lax.dot_general(qh, kc, NT, preferred_element_type=F32), NEG)
                so = jnp.where(valid_o, lax.dot_general(qh, kp, NT, preferred_element_type=F32), NEG)
                m = jnp.maximum(jnp.max(sd, axis=-1, keepdims=True), jnp.max(so, axis=-1, keepdims=True))
                pd = jnp.exp(sd - m)
                po = jnp.exp(so - m)
                den = jnp.sum(pd, axis=-1, keepdims=True) + jnp.sum(po, axis=-1, keepdims=True)
                acc = jnp.dot(pd.astype(BF16), vc, preferred_element_type=F32)
                acc += jnp.dot(po.astype(BF16), vp, preferred_element_type=F32)
                outs.append(acc / den)
                lses.append(m + jnp.log(den))
            o_ref[rows, :] = jnp.where(head0, outs[0], outs[1])
            lse_ref[rows, :] = jnp.where(head0, lses[0], lses[1])
            return carry

        lax.fori_loop(0, nb, step, 0)

    blk = pl.BlockSpec((L, 128), lambda j: (0, j))
    out = jax.ShapeDtypeStruct((L, cols), F32)
    o, lse = pl.pallas_call(
        body, name=name, grid=(cols // 128,),
        in_specs=[blk, blk, blk], out_specs=[blk, blk], out_shape=[out, out],
        compiler_params=_params(("parallel",)),
    )(q, k, v)
    return o.reshape(S, A), lse.reshape(S, A)


def _attn_merge(outs, lses, gain, name):
    S, A = outs[0].shape
    tr = _row_tile(S, 256)

    def body(o1, o2, o3, l1, l2, l3, g_ref, attn_ref, lse_ref, y_ref):
        a, b, c = l1[...], l2[...], l3[...]
        m = jnp.maximum(jnp.maximum(a, b), c)
        ea, eb, ec = jnp.exp(a - m), jnp.exp(b - m), jnp.exp(c - m)
        z = (ea + eb) + ec
        attn = ((ea / z) * o1[...] + (eb / z) * o2[...]) + (ec / z) * o3[...]
        attn_ref[...] = attn
        lse_ref[...] = m + jnp.log(z)
        r = lax.rsqrt(jnp.mean(attn * attn, axis=-1, keepdims=True) + RMS_EPS)
        y_ref[...] = ((attn * r) * g_ref[...]).astype(BF16)

    row = pl.BlockSpec((tr, A), lambda i: (i, 0))
    return pl.pallas_call(
        body, name=name, grid=(S // tr,),
        in_specs=[row] * 6 + [_vec_spec(A, 1)], out_specs=[row, row, row],
        out_shape=[jax.ShapeDtypeStruct((S, A), F32), jax.ShapeDtypeStruct((S, A), F32),
                   jax.ShapeDtypeStruct((S, A), BF16)],
        compiler_params=_params(("parallel",)),
    )(*outs, *lses, gain)


def _attn_merge_bwd(dy, attn, gain, name):
    S, A = attn.shape
    tr = _row_tile(S, 256)

    def body(dy_ref, a_ref, g_ref, da_ref, dl_ref, dg_ref):
        @pl.when(pl.program_id(0) == 0)
        def _():
            dg_ref[...] = jnp.zeros_like(dg_ref)

        x = a_ref[...]
        dy_ = dy_ref[...]
        r = lax.rsqrt(jnp.mean(x * x, axis=-1, keepdims=True) + RMS_EPS)
        xh = x * r
        dg_ref[...] += jnp.sum(dy_ * xh, axis=0, keepdims=True)
        dxh = dy_ * g_ref[...]
        dx = r * (dxh - xh * jnp.mean(dxh * xh, axis=-1, keepdims=True))
        da_ref[...] = dx.astype(BF16)
        prod = dx * x
        hi = lax.broadcasted_iota(jnp.int32, (A, A), 0) // HEAD_DIM
        hj = lax.broadcasted_iota(jnp.int32, (A, A), 1) // HEAD_DIM
        same_head = (hi == hj).astype(F32)
        dl_ref[...] = jnp.dot(prod, same_head, preferred_element_type=F32, precision=lax.Precision.HIGHEST)

    row = pl.BlockSpec((tr, A), lambda i: (i, 0))
    vec = _vec_spec(A, 1)
    return pl.pallas_call(
        body, name=name, grid=(S // tr,),
        in_specs=[row, row, vec], out_specs=[row, row, vec],
        out_shape=[jax.ShapeDtypeStruct((S, A), BF16), jax.ShapeDtypeStruct((S, A), F32),
                   jax.ShapeDtypeStruct((1, A), F32)],
        compiler_params=_params(("arbitrary",)),
    )(dy, attn, gain)


def _attn_bwd(q, k, v, da, lse, delta, dilation, name):
    S, A = q.shape
    L = S // dilation
    cols = dilation * A
    T = min(ATTN_BLOCK, L)
    nb = L // T
    assert L % T == 0 and (T == ATTN_BLOCK or nb == 1)
    q, k, v, da, lse, delta = (t.reshape(L, cols) for t in (q, k, v, da, lse, delta))

    def body(q_ref, k_ref, v_ref, da_ref, lse_ref, dl_ref, dq_ref, dk_ref, dv_ref):
        lane = lax.broadcasted_iota(jnp.int32, (1, 128), 1)
        head0 = lane < HEAD_DIM
        dk_ref[...] = jnp.zeros_like(dk_ref)
        dv_ref[...] = jnp.zeros_like(dv_ref)

        def step(n, carry):
            rows = pl.ds(pl.multiple_of(n * T, T), T)
            prev = pl.ds(pl.multiple_of(jnp.maximum(n - 1, 0) * T, T), T)
            qb, dab = q_ref[rows, :], da_ref[rows, :]
            lse_b, dl_b = lse_ref[rows, :], dl_ref[rows, :]
            kc, vc = k_ref[rows, :], v_ref[rows, :]
            kp, vp = k_ref[prev, :], v_ref[prev, :]
            valid_d, valid_o = _band_masks(T, n > 0)
            dqs = []
            dkc = dkp = dvc = dvp = None
            for h, hmask in enumerate((head0, jnp.logical_not(head0))):
                qh = jnp.where(hmask, qb, jnp.zeros_like(qb))
                dah = jnp.where(hmask, dab, jnp.zeros_like(dab))
                c0 = h * HEAD_DIM
                lse_h = lse_b[:, c0:c0 + 1]
                dl_h = dl_b[:, c0:c0 + 1]
                sd = lax.dot_general(qh, kc, NT, preferred_element_type=F32)
                so = lax.dot_general(qh, kp, NT, preferred_element_type=F32)
                pd = jnp.where(valid_d, jnp.exp(sd - lse_h), 0.0)
                po = jnp.where(valid_o, jnp.exp(so - lse_h), 0.0)
                dpd = lax.dot_general(dah, vc, NT, preferred_element_type=F32)
                dpo = lax.dot_general(dah, vp, NT, preferred_element_type=F32)
                dsd = (pd * (dpd - dl_h)).astype(BF16)
                dso = (po * (dpo - dl_h)).astype(BF16)
                pdb, pob = pd.astype(BF16), po.astype(BF16)
                dqs.append(jnp.dot(dsd, kc, preferred_element_type=F32) + jnp.dot(dso, kp, preferred_element_type=F32))
                t_kc = lax.dot_general(dsd, qh, TN, preferred_element_type=F32)
                t_kp = lax.dot_general(dso, qh, TN, preferred_element_type=F32)
                t_vc = lax.dot_general(pdb, dah, TN, preferred_element_type=F32)
                t_vp = lax.dot_general(pob, dah, TN, preferred_element_type=F32)
                if h == 0:
                    dkc, dkp, dvc, dvp = t_kc, t_kp, t_vc, t_vp
                else:
                    dkc, dkp, dvc, dvp = dkc + t_kc, dkp + t_kp, dvc + t_vc, dvp + t_vp
            dq_ref[rows, :] = jnp.where(head0, dqs[0], dqs[1])
            dk_ref[rows, :] += dkc
            dv_ref[rows, :] += dvc
            dk_ref[prev, :] += dkp
            dv_ref[prev, :] += dvp
            return carry

        lax.fori_loop(0, nb, step, 0)

    blk = pl.BlockSpec((L, 128), lambda j: (0, j))
    out = jax.ShapeDtypeStruct((L, cols), F32)
    dq, dk, dv = pl.pallas_call(
        body, name=name, grid=(cols // 128,),
        in_specs=[blk] * 6, out_specs=[blk] * 3, out_shape=[out] * 3,
        compiler_params=_params(("parallel",)),
    )(q, k, v, da, lse, delta)
    return dq.reshape(S, A), dk.reshape(S, A), dv.reshape(S, A)


def _glu_window(a_ref, b_ref, ah_ref, bh_ref, first):
    u0 = a_ref[...] * _sigmoid(b_ref[...])
    u0h = ah_ref[...] * _sigmoid(bh_ref[...])
    u0h = jnp.where(first, jnp.zeros_like(u0h), u0h)
    return jnp.concatenate([u0h, u0], axis=0)


def _conv_norms(u1, lng, lnb):
    mu = jnp.mean(u1, axis=-1, keepdims=True)
    xc = u1 - mu
    rstd = lax.rsqrt(jnp.mean(xc * xc, axis=-1, keepdims=True) + LN_EPS)
    u1h = xc * rstd
    u2 = u1h * lng + lnb
    sig = _sigmoid(u2)
    u3 = u2 * sig
    r = lax.rsqrt(jnp.mean(u3 * u3, axis=-1, keepdims=True) + RMS_EPS)
    return rstd, u1h, u2, sig, u3, r


def _conv_specs(tr, C, col_a, col_b):
    per = tr // HALO

    def tile(col):
        return pl.BlockSpec((tr, C), lambda i: (i, col))

    def halo(col):
        return pl.BlockSpec((HALO, C), lambda i: (jnp.maximum(i * per - 1, 0), col))

    return tile(col_a), tile(col_b), halo(col_a), halo(col_b)


def _conv_fwd(proj, cw, cb, lng, lnb, gco, name):
    S = proj.shape[0]
    C = 512
    tr = _row_tile(S, 256)

    def body(a_ref, b_ref, ah_ref, bh_ref, w_ref, cb_ref, lng_ref, lnb_ref, gco_ref, u1_ref, y_ref):
        win = _glu_window(a_ref, b_ref, ah_ref, bh_ref, pl.program_id(0) == 0)
        acc = jnp.broadcast_to(cb_ref[...], (tr, C))
        for j in range(CONV_KERNEL):
            off = HALO - (CONV_KERNEL - 1) + j
            acc = acc + w_ref[j:j + 1, :] * win[off:off + tr, :]
        u1_ref[...] = acc
        _, _, _, _, u3, r = _conv_norms(acc, lng_ref[...], lnb_ref[...])
        y_ref[...] = ((u3 * r) * gco_ref[...]).astype(BF16)

    ta, tb, ha, hb = _conv_specs(tr, C, 3, 4)
    row = pl.BlockSpec((tr, C), lambda i: (i, 0))
    vec = _vec_spec(C, 1)
    return pl.pallas_call(
        body, name=name, grid=(S // tr,),
        in_specs=[ta, tb, ha, hb, pl.BlockSpec((HALO, C), lambda i: (0, 0)), vec, vec, vec, vec],
        out_specs=[row, row],
        out_shape=[jax.ShapeDtypeStruct((S, C), F32), jax.ShapeDtypeStruct((S, C), BF16)],
        compiler_params=_params(("parallel",)),
    )(proj, proj, proj, proj, cw, cb, lng, lnb, gco)


def _conv_bwd_norms(dy, u1, lng, lnb, gco, name):
    S, C = u1.shape
    tr = _row_tile(S, 256)

    def body(dy_ref, u1_ref, lng_ref, lnb_ref, gco_ref, du1_ref, dgco_ref, dlng_ref, dlnb_ref, dcb_ref):
        @pl.when(pl.program_id(0) == 0)
        def _():
            for ref in (dgco_ref, dlng_ref, dlnb_ref, dcb_ref):
                ref[...] = jnp.zeros_like(ref)

        lng = lng_ref[...]
        rstd, u1h, u2, sig, u3, r = _conv_norms(u1_ref[...], lng, lnb_ref[...])
        dy_ = dy_ref[...]
        u3h = u3 * r
        dgco_ref[...] += jnp.sum(dy_ * u3h, axis=0, keepdims=True)
        du3h = dy_ * gco_ref[...]
        du3 = r * (du3h - u3h * jnp.mean(du3h * u3h, axis=-1, keepdims=True))
        du2 = du3 * (sig * (1.0 + u2 * (1.0 - sig)))
        dlng_ref[...] += jnp.sum(du2 * u1h, axis=0, keepdims=True)
        dlnb_ref[...] += jnp.sum(du2, axis=0, keepdims=True)
        du1h = du2 * lng
        du1 = rstd * (du1h - jnp.mean(du1h, axis=-1, keepdims=True)
                      - u1h * jnp.mean(du1h * u1h, axis=-1, keepdims=True))
        du1_ref[...] = du1
        dcb_ref[...] += jnp.sum(du1, axis=0, keepdims=True)

    row = pl.BlockSpec((tr, C), lambda i: (i, 0))
    vec = _vec_spec(C, 1)
    return pl.pallas_call(
        body, name=name, grid=(S // tr,),
        in_specs=[pl.BlockSpec((tr, C), lambda i: (i, 1)), row, vec, vec, vec],
        out_specs=[row, vec, vec, vec, vec],
        out_shape=[jax.ShapeDtypeStruct((S, C), F32)] + [jax.ShapeDtypeStruct((1, C), F32)] * 4,
        compiler_params=_params(("arbitrary",)),
    )(dy, u1, lng, lnb, gco)


def _conv_bwd_taps(du1, proj, cw, name):
    S, C = du1.shape
    tr = _row_tile(S, 256)
    nt = S // tr
    per = tr // HALO

    def body(du_ref, dun_ref, a_ref, b_ref, ah_ref, bh_ref, w_ref, da_ref, db_ref, dw_ref):
        i = pl.program_id(0)

        @pl.when(i == 0)
        def _():
            dw_ref[...] = jnp.zeros_like(dw_ref)

        win = _glu_window(a_ref, b_ref, ah_ref, bh_ref, i == 0)
        du = du_ref[...]
        nxt = jnp.where(i == nt - 1, jnp.zeros_like(dun_ref[...]), dun_ref[...])
        dwin = jnp.concatenate([du, nxt], axis=0)
        du0 = jnp.zeros((tr, C), F32)
        taps = []
        for j in range(CONV_KERNEL):
            back = CONV_KERNEL - 1 - j
            du0 = du0 + w_ref[j:j + 1, :] * dwin[back:back + tr, :]
            off = HALO - (CONV_KERNEL - 1) + j
            taps.append(jnp.sum(du * win[off:off + tr, :], axis=0, keepdims=True))
        taps.append(jnp.zeros((HALO - CONV_KERNEL, C), F32))
        dw_ref[...] += jnp.concatenate(taps, axis=0)
        a, sig = a_ref[...], _sigmoid(b_ref[...])
        da_ref[...] = (du0 * sig).astype(BF16)
        db_ref[...] = (du0 * a * sig * (1.0 - sig)).astype(BF16)

    ta, tb, ha, hb = _conv_specs(tr, C, 3, 4)
    row = pl.BlockSpec((tr, C), lambda i: (i, 0))
    nxt = pl.BlockSpec((HALO, C), lambda i: (jnp.minimum((i + 1) * per, S // HALO - 1), 0))
    wspec = pl.BlockSpec((HALO, C), lambda i: (0, 0))
    return pl.pallas_call(
        body, name=name, grid=(nt,),
        in_specs=[row, nxt, ta, tb, ha, hb, wspec],
        out_specs=[row, row, wspec],
        out_shape=[jax.ShapeDtypeStruct((S, C), BF16), jax.ShapeDtypeStruct((S, C), BF16),
                   jax.ShapeDtypeStruct((HALO, C), F32)],
        compiler_params=_params(("arbitrary",)),
    )(du1, du1, proj, proj, proj, proj, cw)


def _ada_fwd(c_all, w, b, name):
    B, D = c_all.shape
    N = w.shape[1]
    tn = 768 if N % 768 == 0 else N

    def body(c_ref, w_ref, b_ref, o_ref):
        c = c_ref[...]
        a = (c * _sigmoid(c)).astype(BF16)
        o_ref[...] = jnp.dot(a, w_ref[...].astype(BF16), preferred_element_type=F32) + b_ref[...]

    return pl.pallas_call(
        body, name=name, grid=(N // tn,),
        in_specs=[pl.BlockSpec((B, D), lambda j: (0, 0)), pl.BlockSpec((D, tn), lambda j: (0, j)),
                  pl.BlockSpec((1, tn), lambda j: (0, j))],
        out_specs=pl.BlockSpec((B, tn), lambda j: (0, j)),
        out_shape=jax.ShapeDtypeStruct((B, N), F32),
        compiler_params=_params(("parallel",)),
    )(c_all, w, b)


def _ada_wgrad(c_t, dmod, name):
    D, B = c_t.shape
    N = dmod.shape[1]
    tn = 768 if N % 768 == 0 else N

    def body(c_ref, d_ref, o_ref):
        c = c_ref[...]
        a = (c * _sigmoid(c)).astype(BF16)
        o_ref[...] = jnp.dot(a, d_ref[...].astype(BF16), preferred_element_type=F32)

    return pl.pallas_call(
        body, name=name, grid=(N // tn,),
        in_specs=[pl.BlockSpec((D, B), lambda j: (0, 0)), pl.BlockSpec((B, tn), lambda j: (0, j))],
        out_specs=pl.BlockSpec((D, tn), lambda j: (0, j)),
        out_shape=jax.ShapeDtypeStruct((D, N), F32),
        compiler_params=_params(("parallel",)),
    )(c_t, dmod)


def _sum_rows(x, name):
    R, N = x.shape

    def body(x_ref, o_ref):
        acc = x_ref[0:1, :]
        for r in range(1, R):
            acc = acc + x_ref[r:r + 1, :]
        o_ref[...] = acc

    return pl.pallas_call(
        body, name=name, out_shape=jax.ShapeDtypeStruct((1, N), F32),
        compiler_params=_params(),
    )(x)


def _sum_slots(x, name):
    n, R, C = x.shape
    tr = _row_tile(R, 128) if R % 128 == 0 else R

    def body(x_ref, o_ref):
        acc = x_ref[0]
        for s in range(1, n):
            acc = acc + x_ref[s]
        o_ref[...] = acc

    return pl.pallas_call(
        body, name=name, grid=(R // tr,),
        in_specs=[pl.BlockSpec((n, tr, C), lambda i: (0, i, 0))],
        out_specs=pl.BlockSpec((tr, C), lambda i: (i, 0)),
        out_shape=jax.ShapeDtypeStruct((R, C), F32),
        compiler_params=_params(("parallel",)),
    )(x)


def _add_halves(g, recv, c_idx, name):
    nk, _, R, C = g.shape
    tr = _row_tile(R, 128) if R % 128 == 0 else R

    def body(c_ref, g_ref, r_ref, o_ref):
        o_ref[...] = g_ref[...] + r_ref[...]

    grid_spec = pltpu.PrefetchScalarGridSpec(
        num_scalar_prefetch=1, grid=(nk, R // tr),
        in_specs=[pl.BlockSpec((None, None, tr, C), lambda k, i, c: (k, c[0], i, 0)),
                  pl.BlockSpec((None, tr, C), lambda k, i, c: (k, i, 0))],
        out_specs=pl.BlockSpec((None, tr, C), lambda k, i, c: (k, i, 0)),
    )
    return pl.pallas_call(
        body, name=name, grid_spec=grid_spec,
        out_shape=jax.ShapeDtypeStruct((nk, R, C), F32),
        compiler_params=_params(("parallel", "parallel")),
    )(c_idx, g, recv)


def _adamw(w, g, m, v, name):
    R, C = w.shape
    tr = _row_tile(R, 256) if R % 256 == 0 else R
    bc1 = 1.0 - ADAM_B1 ** ADAM_STEP
    bc2 = 1.0 - ADAM_B2 ** ADAM_STEP

    def body(w_ref, g_ref, m_ref, v_ref, d_ref, mo_ref, vo_ref):
        g_ = g_ref[...]
        m_ = ADAM_B1 * m_ref[...] + (1.0 - ADAM_B1) * g_
        v_ = ADAM_B2 * v_ref[...] + (1.0 - ADAM_B2) * (g_ * g_)
        mo_ref[...] = m_
        vo_ref[...] = v_
        d_ref[...] = -ADAM_LR * ((m_ / bc1) / (jnp.sqrt(v_ / bc2) + ADAM_EPS) + ADAM_WD * w_ref[...])

    row = pl.BlockSpec((tr, C), lambda i: (i, 0))
    out = jax.ShapeDtypeStruct((R, C), F32)
    return pl.pallas_call(
        body, name=name, grid=(R // tr,),
        in_specs=[row] * 4, out_specs=[row] * 3, out_shape=[out] * 3,
        compiler_params=_params(("parallel",)),
    )(w, g, m, v)


def _coords():
    return lax.axis_index("x"), lax.axis_index("y"), lax.axis_index("c")


def _all_gather8(x, name, deps=()):
    R, N = x.shape
    assert R == 8
    flips = [(fx, fy, fc) for fx in (0, 1) for fy in (0, 1) for fc in (0, 1)][1:]

    def body(x_ref, o_ref, send_sems, recv_sems):
        mx, my, mc = _coords()
        me = 4 * mx + 2 * my + mc

        def rows(dev):
            return o_ref.at[pl.ds(pl.multiple_of(dev * R, R), R), :]

        o_ref[pl.ds(pl.multiple_of(me * R, R), R), :] = x_ref[...]
        copies = []
        for t, (fx, fy, fc) in enumerate(flips):
            peer = (mx ^ fx, my ^ fy, mc ^ fc)
            copies.append(pltpu.make_async_remote_copy(
                src_ref=x_ref, dst_ref=rows(me), send_sem=send_sems.at[t], recv_sem=recv_sems.at[t],
                device_id=peer, device_id_type=MESH))
        for cp in copies:
            cp.start()
        for t, (fx, fy, fc) in enumerate(flips):
            peer_id = 4 * (mx ^ fx) + 2 * (my ^ fy) + (mc ^ fc)
            pltpu.make_async_remote_copy(
                src_ref=x_ref, dst_ref=rows(peer_id), send_sem=send_sems.at[t], recv_sem=recv_sems.at[t],
                device_id=(mx ^ fx, my ^ fy, mc ^ fc), device_id_type=MESH).wait_recv()
        for cp in copies:
            cp.wait_send()

    return pl.pallas_call(
        _blind_to(body, 1, len(deps)), name=name,
        in_specs=[pl.BlockSpec(memory_space=pltpu.VMEM)] + [pl.BlockSpec(memory_space=pl.ANY)] * len(deps),
        out_specs=pl.BlockSpec(memory_space=pltpu.VMEM),
        out_shape=jax.ShapeDtypeStruct((N_DEV * R, N), F32),
        scratch_shapes=[pltpu.SemaphoreType.DMA((7,)), pltpu.SemaphoreType.DMA((7,))],
        compiler_params=pltpu.CompilerParams(has_side_effects=True, vmem_limit_bytes=VMEM_LIMIT_BYTES),
    )(x, *deps)


CHIP_FLIPS = ((1, 0), (0, 1), (1, 1))


def _gather_chips(shards, name):
    n = len(shards)

    def body(*refs):
        in_refs, out_refs = refs[:n], refs[n:2 * n]
        send_sems, recv_sems, local_sems = refs[2 * n:]
        mx, my, mc = _coords()
        me = 2 * mx + my
        started = []
        for i in range(n):
            loc = pltpu.make_async_copy(in_refs[i], out_refs[i].at[me], local_sems.at[i])
            loc.start()
            started.append(loc)
        sends = []
        for i in range(n):
            for t, (fx, fy) in enumerate(CHIP_FLIPS):
                cp = pltpu.make_async_remote_copy(
                    src_ref=in_refs[i], dst_ref=out_refs[i].at[me],
                    send_sem=send_sems.at[i, t], recv_sem=recv_sems.at[i, t],
                    device_id=(mx ^ fx, my ^ fy, mc), device_id_type=MESH)
                cp.start()
                sends.append(cp)
        for i in range(n):
            for t, (fx, fy) in enumerate(CHIP_FLIPS):
                src_chip = 2 * (mx ^ fx) + (my ^ fy)
                pltpu.make_async_remote_copy(
                    src_ref=in_refs[i], dst_ref=out_refs[i].at[src_chip],
                    send_sem=send_sems.at[i, t], recv_sem=recv_sems.at[i, t],
                    device_id=(mx ^ fx, my ^ fy, mc), device_id_type=MESH).wait_recv()
        for cp in sends:
            cp.wait_send()
        for loc in started:
            loc.wait()

    any_spec = pl.BlockSpec(memory_space=pl.ANY)
    return pl.pallas_call(
        body, name=name,
        in_specs=[any_spec] * n, out_specs=[any_spec] * n,
        out_shape=[jax.ShapeDtypeStruct((N_CHIPS,) + s.shape, s.dtype) for s in shards],
        scratch_shapes=[pltpu.SemaphoreType.DMA((n, 3)), pltpu.SemaphoreType.DMA((n, 3)),
                        pltpu.SemaphoreType.DMA((n,))],
        compiler_params=pltpu.CompilerParams(has_side_effects=True),
    )(*shards)


def _send_other_halves(grads, name):
    n = len(grads)

    def body(*refs):
        in_refs, out_refs = refs[:n], refs[n:2 * n]
        send_sems, recv_sems = refs[2 * n:]
        mx, my, mc = _coords()
        sends = []
        for i in range(n):
            cp = pltpu.make_async_remote_copy(
                src_ref=in_refs[i].at[:, 1 - mc], dst_ref=out_refs[i],
                send_sem=send_sems.at[i], recv_sem=recv_sems.at[i],
                device_id=(mx, my, 1 - mc), device_id_type=MESH)
            cp.start()
            sends.append(cp)
        for cp in sends:
            cp.wait_recv()
        for cp in sends:
            cp.wait_send()

    any_spec = pl.BlockSpec(memory_space=pl.ANY)
    return pl.pallas_call(
        body, name=name,
        in_specs=[any_spec] * n, out_specs=[any_spec] * n,
        out_shape=[jax.ShapeDtypeStruct((g.shape[0],) + g.shape[2:], g.dtype) for g in grads],
        scratch_shapes=[pltpu.SemaphoreType.DMA((n,)), pltpu.SemaphoreType.DMA((n,))],
        compiler_params=pltpu.CompilerParams(has_side_effects=True),
    )(*grads)


def _exchange_chunks(sums, name):
    n = len(sums)

    def body(*refs):
        in_refs, out_refs = refs[:n], refs[n:2 * n]
        send_sems, recv_sems, local_sems = refs[2 * n:]
        mx, my, mc = _coords()
        me = 2 * mx + my
        started = []
        for i in range(n):
            loc = pltpu.make_async_copy(in_refs[i].at[me], out_refs[i].at[me], local_sems.at[i])
            loc.start()
            started.append(loc)
        sends = []
        for i in range(n):
            for t, (fx, fy) in enumerate(CHIP_FLIPS):
                peer_chip = 2 * (mx ^ fx) + (my ^ fy)
                cp = pltpu.make_async_remote_copy(
                    src_ref=in_refs[i].at[peer_chip], dst_ref=out_refs[i].at[me],
                    send_sem=send_sems.at[i, t], recv_sem=recv_sems.at[i, t],
                    device_id=(mx ^ fx, my ^ fy, mc), device_id_type=MESH)
                cp.start()
                sends.append(cp)
        for i in range(n):
            for t, (fx, fy) in enumerate(CHIP_FLIPS):
                peer_chip = 2 * (mx ^ fx) + (my ^ fy)
                pltpu.make_async_remote_copy(
                    src_ref=in_refs[i].at[me], dst_ref=out_refs[i].at[peer_chip],
                    send_sem=send_sems.at[i, t], recv_sem=recv_sems.at[i, t],
                    device_id=(mx ^ fx, my ^ fy, mc), device_id_type=MESH).wait_recv()
        for cp in sends:
            cp.wait_send()
        for loc in started:
            loc.wait()

    any_spec = pl.BlockSpec(memory_space=pl.ANY)
    return pl.pallas_call(
        body, name=name,
        in_specs=[any_spec] * n, out_specs=[any_spec] * n,
        out_shape=[jax.ShapeDtypeStruct(s.shape, s.dtype) for s in sums],
        scratch_shapes=[pltpu.SemaphoreType.DMA((n, 3)), pltpu.SemaphoreType.DMA((n, 3)),
                        pltpu.SemaphoreType.DMA((n,))],
        compiler_params=pltpu.CompilerParams(has_side_effects=True),
    )(*sums)


def _share_halves(halves, name):
    n = len(halves)

    def body(*refs):
        in_refs, out_refs = refs[:n], refs[n:2 * n]
        send_sems, recv_sems, local_sems = refs[2 * n:]
        mx, my, mc = _coords()
        started, sends = [], []
        for i in range(n):
            loc = pltpu.make_async_copy(in_refs[i], out_refs[i].at[mc], local_sems.at[i])
            loc.start()
            started.append(loc)
            cp = pltpu.make_async_remote_copy(
                src_ref=in_refs[i], dst_ref=out_refs[i].at[mc],
                send_sem=send_sems.at[i], recv_sem=recv_sems.at[i],
                device_id=(mx, my, 1 - mc), device_id_type=MESH)
            cp.start()
            sends.append(cp)
        for i in range(n):
            pltpu.make_async_remote_copy(
                src_ref=in_refs[i], dst_ref=out_refs[i].at[1 - mc],
                send_sem=send_sems.at[i], recv_sem=recv_sems.at[i],
                device_id=(mx, my, 1 - mc), device_id_type=MESH).wait_recv()
        for cp in sends:
            cp.wait_send()
        for loc in started:
            loc.wait()

    any_spec = pl.BlockSpec(memory_space=pl.ANY)
    return pl.pallas_call(
        body, name=name,
        in_specs=[any_spec] * n, out_specs=[any_spec] * n,
        out_shape=[jax.ShapeDtypeStruct((2,) + h.shape, h.dtype) for h in halves],
        scratch_shapes=[pltpu.SemaphoreType.DMA((n,)), pltpu.SemaphoreType.DMA((n,)),
                        pltpu.SemaphoreType.DMA((n,))],
        compiler_params=pltpu.CompilerParams(has_side_effects=True),
    )(*halves)


def _reduce_scatter_grads(grads, tag):
    c_idx = lax.axis_index("c").astype(jnp.int32).reshape(1)
    split = [g.reshape(g.shape[0], 2, g.shape[1] // 2, g.shape[2]) for g in grads]
    recv = _send_other_halves(split, f"rs_halves_{tag}")
    chip_sums = [_add_halves(g, r, c_idx, f"rs_add_{tag}_{i}") for i, (g, r) in enumerate(zip(split, recv))]
    slots = _exchange_chunks(chip_sums, f"rs_chunks_{tag}")
    halves = [_sum_slots(s, f"rs_sum_{tag}_{i}") for i, s in enumerate(slots)]
    both = _share_halves(halves, f"rs_share_{tag}")
    return [b.reshape(2 * b.shape[1], b.shape[2]) for b in both]


def _ffn_forward(h, gain, sc, sh, gate, wg, wu, wd, tag):
    n = _norm_mod(h, gain, sc, sh, f"{tag}_norm")
    ga, up, act = _ffn_gate_up(n, wg, wu, f"{tag}_gate_up")
    h_out, f = _mm_residual(act, wd, h, gate, 0.5, f"{tag}_down")
    return h_out, (h, n, ga, up, act, f)


def _ffn_backward(dh_out, saved, gain, sc, gate, wg, wu, wd, tag):
    h, n, ga, up, act, f = saved
    df, d_gate = _gate_bwd(dh_out, f, gate, 0.5, f"{tag}_gate_bwd")
    dga, dup = _ffn_dact(df, wd, ga, up, f"{tag}_dact")
    dwd = _wgrad_chunk_lhs(act, df, f"{tag}_dwd")
    dwg, dwu = _wgrad_chunk_rhs(n, [dga, dup], wg.shape[0], True, f"{tag}_dwgu")
    dn = _mm_nt_reduce([dga, dup], [wg, wu], True, f"{tag}_dn")
    dh_in, d_sh, d_sc, d_gain = _norm_mod_bwd(dn, h, gain, sc, dh_out, f"{tag}_norm_bwd")
    return dh_in, (dwg, dwu, dwd), (d_sh, d_sc, d_gate, d_gain)


def _pad_cols(v, n):
    return jnp.pad(v, ((0, 0), (0, n - v.shape[1])))


def _mixer_forward(h1, mix_norm_g, sc2, sh2, gt2, win, wout, conv_w, conv_dw_b, conv_ln_g, conv_ln_b, attn_out_g,
                   conv_out_g):
    S, D = h1.shape
    n2 = _norm_mod(h1, mix_norm_g, sc2, sh2, "mix_norm")
    proj = _mm_cols(n2, win, "mix_in")
    cos, sin = _rope_tables(S)
    q, k, v = _qkv_rope(proj, cos, sin, "qkv_rope")
    outs, lses = [], []
    for dil in DILATIONS:
        o, lse = _attn_fwd(q, k, v, dil, f"attn_fwd_d{dil}")
        outs.append(o)
        lses.append(lse)
    attn, lse_tot, ya = _attn_merge(outs, lses, attn_out_g, "attn_merge")
    u1, yc = _conv_fwd(proj, conv_w, conv_dw_b, conv_ln_g, conv_ln_b, conv_out_g, "conv_fwd")
    y = jnp.concatenate([ya, yc], axis=1)
    h2, mo = _mm_residual(y[None], wout.reshape(1, D, D), h1, gt2, 1.0, "mix_out")
    return h2, (h1, n2, proj, cos, sin, q, k, v, attn, lse_tot, u1, y, mo)


def _mixer_backward(dh2, saved, mix_norm_g, sc2, gt2, win, wout, conv_w, conv_ln_g, conv_ln_b, attn_out_g,
                    conv_out_g):
    h1, n2, proj, cos, sin, q, k, v, attn, lse_tot, u1, y, mo = saved
    S, D = h1.shape
    dmo, d_gt2 = _gate_bwd(dh2, mo, gt2, 1.0, "mix_gate_bwd")
    dwout = _wgrad_chunk_lhs(y[None], dmo, "mix_dwout")
    dy = _mm_nt(dmo, wout.reshape(D, D), "mix_dy")
    dattn, delta, d_attn_g = _attn_merge_bwd(dy, attn, attn_out_g, "attn_merge_bwd")
    dqs, dks, dvs = [], [], []
    for dil in DILATIONS:
        dq_, dk_, dv_ = _attn_bwd(q, k, v, dattn, lse_tot, delta, dil, f"attn_bwd_d{dil}")
        dqs.append(dq_)
        dks.append(dk_)
        dvs.append(dv_)
    dq = _rope_bwd(dqs, cos, sin, HEAD_DIM ** -0.5, True, "dq_rope")
    dk = _rope_bwd(dks, cos, sin, 1.0, True, "dk_rope")
    dv = _rope_bwd(dvs, cos, sin, 1.0, False, "dv_sum")
    du1, d_gco, d_lng, d_lnb, d_cb = _conv_bwd_norms(dy, u1, conv_ln_g, conv_ln_b, conv_out_g, "conv_bwd_norms")
    dga_, dgb_, d_cw = _conv_bwd_taps(du1, proj, conv_w, "conv_bwd_taps")
    dproj = jnp.concatenate([dq, dk, dv, dga_, dgb_], axis=1)
    (dwin,) = _wgrad_chunk_rhs(n2, [dproj], N_CHIPS, False, "mix_dwin")
    dn2 = _mm_nt_reduce([dproj], [win], False, "mix_dn")
    dh1, d_sh2, d_sc2, d_gain2 = _norm_mod_bwd(dn2, h1, mix_norm_g, sc2, dh2, "mix_norm_bwd")
    small = (d_sh2, d_sc2, d_gt2, d_gain2, d_cb, d_lng, d_lnb, d_attn_g, d_gco, d_cw)
    return dh1, (dwin, dwout), small


def _old_kernel(x, c, w_ada, b_ada, ffn1_norm_g, ffn1_w_gate, ffn1_w_up, ffn1_w_down, mix_norm_g, w_in, conv_dw_w, conv_dw_b, conv_ln_g, conv_ln_b, attn_out_g, conv_out_g, w_out, ffn2_norm_g, ffn2_w_gate, ffn2_w_up, ffn2_w_down, final_norm_g, loss_target, m_w_ada, m_b_ada, m_ffn1_norm_g, m_ffn1_w_gate, m_ffn1_w_up, m_ffn1_w_down, m_mix_norm_g, m_w_in, m_conv_dw_w, m_conv_dw_b, m_conv_ln_g, m_conv_ln_b, m_attn_out_g, m_conv_out_g, m_w_out, m_ffn2_norm_g, m_ffn2_w_gate, m_ffn2_w_up, m_ffn2_w_down, m_final_norm_g, v_w_ada, v_b_ada, v_ffn1_norm_g, v_ffn1_w_gate, v_ffn1_w_up, v_ffn1_w_down, v_mix_norm_g, v_w_in, v_conv_dw_w, v_conv_dw_b, v_conv_ln_g, v_conv_ln_b, v_attn_out_g, v_conv_out_g, v_w_out, v_ffn2_norm_g, v_ffn2_w_gate, v_ffn2_w_up, v_ffn2_w_down, v_final_norm_g):
    S, D = x.shape[1], x.shape[2]
    mx, my, mc = _coords()
    chip = 2 * mx + my
    dev = 4 * mx + 2 * my + mc
    h0 = x[0]
    target = loss_target[0]

    big = [ffn1_w_gate[0], ffn1_w_up[0], ffn1_w_down[0], w_in[0], w_out[0], ffn2_w_gate[0], ffn2_w_up[0],
           ffn2_w_down[0]]
    wg1, wu1, wd1, win, wout, wg2, wu2, wd2 = _gather_chips([w.astype(BF16) for w in big], "gather_weights")

    ncw = CONV_KERNEL * 128
    n0 = -(-(D + ncw) // 1024) * 1024
    pk0 = _pad_cols(jnp.concatenate([c.reshape(1, D), conv_dw_w.reshape(1, ncw)], axis=1), n0)
    g0 = _all_gather8(pk0.reshape(8, n0 // 8), "gather_c").reshape(N_DEV, n0)
    c_all = g0[:, :D]
    conv_w = jnp.concatenate([g0[2 * kc, D:D + ncw].reshape(CONV_KERNEL, 128) for kc in range(N_CHIPS)], axis=1)
    conv_w = jnp.pad(conv_w, ((0, HALO - CONV_KERNEL), (0, 0)))
    nmod = w_ada.shape[2]
    b_shard = lax.dynamic_slice(b_ada, (0, chip * nmod), (1, nmod))
    mod_part = _ada_fwd(c_all, w_ada[0], b_shard, "ada_fwd")
    g1 = _all_gather8(mod_part, "gather_mod")
    mod_all = jnp.concatenate([g1[16 * kc:16 * kc + 8] for kc in range(N_CHIPS)], axis=1)
    mod = lax.dynamic_slice(mod_all, (dev, 0), (1, 9 * D))
    sh1, sc1, gt1, sh2, sc2, gt2, sh3, sc3, gt3 = [mod[:, i * D:(i + 1) * D] for i in range(9)]

    h1, saved1 = _ffn_forward(h0, ffn1_norm_g, sc1, sh1, gt1, wg1, wu1, wd1, "ffn1")
    h2, saved2 = _mixer_forward(h1, mix_norm_g, sc2, sh2, gt2, win, wout, conv_w, conv_dw_b, conv_ln_g, conv_ln_b,
                                attn_out_g, conv_out_g)
    h3, saved3 = _ffn_forward(h2, ffn2_norm_g, sc3, sh3, gt3, wg2, wu2, wd2, "ffn2")
    loss_part, dh3, d_final_g = _loss_head(h3, final_norm_g.reshape(1, D), target, "loss_head")

    dh2, dw_ffn2, (d_sh3, d_sc3, d_gt3, d_gain3) = _ffn_backward(
        dh3, saved3, ffn2_norm_g, sc3, gt3, wg2, wu2, wd2, "ffn2")
    g_wg2, g_wu2, g_wd2 = _reduce_scatter_grads(list(dw_ffn2), "ffn2")
    dh1, (dwin, dwout), small_mix = _mixer_backward(
        dh2, saved2, mix_norm_g, sc2, gt2, win, wout, conv_w, conv_ln_g, conv_ln_b, attn_out_g, conv_out_g)
    d_sh2, d_sc2, d_gt2, d_gain2, d_cb, d_lng, d_lnb, d_attn_g, d_gco, d_cw = small_mix
    g_win, g_wout = _reduce_scatter_grads([dwin, dwout.reshape(N_CHIPS, D // N_CHIPS, D)], "mix")
    dh0, dw_ffn1, (d_sh1, d_sc1, d_gt1, d_gain1) = _ffn_backward(
        dh1, saved1, ffn1_norm_g, sc1, gt1, wg1, wu1, wd1, "ffn1")
    g_wg1, g_wu1, g_wd1 = _reduce_scatter_grads(list(dw_ffn1), "ffn1")

    dmod = jnp.concatenate([d_sh1, d_sc1, d_gt1, d_sh2, d_sc2, d_gt2, d_sh3, d_sc3, d_gt3], axis=1)
    small = [d_gain1, d_gain2, d_gain3, d_final_g, d_cb, d_lng, d_lnb, d_attn_g, d_gco,
             d_cw[:CONV_KERNEL].reshape(1, CONV_KERNEL * 512), loss_part]
    pk1 = jnp.concatenate([dmod] + small, axis=1)
    n1 = -(-pk1.shape[1] // 1024) * 1024
    gathered = _all_gather8(_pad_cols(pk1, n1).reshape(8, n1 // 8), "gather_small").reshape(N_DEV, n1)
    tot = _sum_rows(gathered, "sum_small")
    off = [0]

    def take(nel):
        out = tot[:, off[0]:off[0] + nel]
        off[0] += nel
        return out

    g_b_ada = take(9 * D)
    g_ffn1_norm, g_mix_norm, g_ffn2_norm, g_final = take(D), take(D), take(D), take(D)
    g_cb, g_lng, g_lnb, g_attn_g, g_gco = take(512), take(512), take(512), take(512), take(512)
    g_cw_full = take(CONV_KERNEL * 512).reshape(CONV_KERNEL, 512)
    loss = take(128)[0, 0]
    g_cw = lax.dynamic_slice(g_cw_full, (0, chip * 128), (CONV_KERNEL, 128))

    dmod_shard = lax.dynamic_slice(gathered[:, :9 * D], (0, chip * nmod), (N_DEV, nmod))
    dmod16 = jnp.pad(dmod_shard, ((0, N_DEV), (0, 0)))
    c_t16 = jnp.pad(c_all.T, ((0, 0), (0, N_DEV)))
    g_w_ada = _ada_wgrad(c_t16, dmod16, "ada_wgrad")

    names = ["w_ada", "b_ada", "ffn1_norm_g", "ffn1_w_gate", "ffn1_w_up", "ffn1_w_down", "mix_norm_g", "w_in",
             "conv_dw_w", "conv_dw_b", "conv_ln_g", "conv_ln_b", "attn_out_g", "conv_out_g", "w_out", "ffn2_norm_g",
             "ffn2_w_gate", "ffn2_w_up", "ffn2_w_down", "final_norm_g"]
    weights = [w_ada, b_ada, ffn1_norm_g, ffn1_w_gate, ffn1_w_up, ffn1_w_down, mix_norm_g, w_in, conv_dw_w,
               conv_dw_b, conv_ln_g, conv_ln_b, attn_out_g, conv_out_g, w_out, ffn2_norm_g, ffn2_w_gate, ffn2_w_up,
               ffn2_w_down, final_norm_g]
    ms = [m_w_ada, m_b_ada, m_ffn1_norm_g, m_ffn1_w_gate, m_ffn1_w_up, m_ffn1_w_down, m_mix_norm_g, m_w_in,
          m_conv_dw_w, m_conv_dw_b, m_conv_ln_g, m_conv_ln_b, m_attn_out_g, m_conv_out_g, m_w_out, m_ffn2_norm_g,
          m_ffn2_w_gate, m_ffn2_w_up, m_ffn2_w_down, m_final_norm_g]
    vs = [v_w_ada, v_b_ada, v_ffn1_norm_g, v_ffn1_w_gate, v_ffn1_w_up, v_ffn1_w_down, v_mix_norm_g, v_w_in,
          v_conv_dw_w, v_conv_dw_b, v_conv_ln_g, v_conv_ln_b, v_attn_out_g, v_conv_out_g, v_w_out, v_ffn2_norm_g,
          v_ffn2_w_gate, v_ffn2_w_up, v_ffn2_w_down, v_final_norm_g]
    grads_flat = [g_w_ada, g_b_ada, g_ffn1_norm, g_wg1, g_wu1, g_wd1, g_mix_norm, g_win, g_cw, g_cb, g_lng, g_lnb,
                  g_attn_g, g_gco, g_wout, g_ffn2_norm, g_wg2, g_wu2, g_wd2, g_final]
    big_names = ("w_ada", "ffn1_w_gate", "ffn1_w_up", "ffn1_w_down", "w_in", "w_out", "ffn2_w_gate", "ffn2_w_up",
                 "ffn2_w_down")
    grads, deltas, new_ms, new_vs = {}, {}, {}, {}
    for i, nm in enumerate(names):
        if nm in big_names:
            shape = weights[i].shape
            two_d = (shape[-2], shape[-1])
            g2 = grads_flat[i].reshape(two_d)
            d_, m_, v_ = _adamw(weights[i].reshape(two_d), g2, ms[i].reshape(two_d), vs[i].reshape(two_d),
                                f"adamw_{nm}")
            grads[nm], deltas[nm], new_ms[nm], new_vs[nm] = (t.reshape(shape) for t in (g2, d_, m_, v_))
    small_ix = [i for i, nm in enumerate(names) if nm not in big_names]

    def pack_small(arrs):
        flat = jnp.concatenate([arrs[i].reshape(1, -1) for i in small_ix], axis=1)
        npad = -(-flat.shape[1] // 1024) * 1024
        return _pad_cols(flat, npad).reshape(8, npad // 8)

    d_s, m_s, v_s = _adamw(pack_small(weights), pack_small(grads_flat), pack_small(ms), pack_small(vs),
                           "adamw_small")
    pos = 0
    for i in small_ix:
        nm, shape, nel = names[i], weights[i].shape, weights[i].size
        grads[nm] = grads_flat[i].reshape(shape)
        deltas[nm], new_ms[nm], new_vs[nm] = (t.reshape(1, -1)[:, pos:pos + nel].reshape(shape)
                                              for t in (d_s, m_s, v_s))
        pos += nel

    return (loss, dh0[None], *[grads[nm] for nm in names], *[deltas[nm] for nm in names],
            *[new_ms[nm] for nm in names], *[new_vs[nm] for nm in names])


HBM_SPEC = pl.BlockSpec(memory_space=pltpu.HBM)
SEM_SPEC = pl.BlockSpec(memory_space=pltpu.SEMAPHORE)
ANY_SPEC = pl.BlockSpec(memory_space=pl.ANY)
DATAFLOW = pltpu.SideEffectType.DATAFLOW_SIDE_EFFECTING
XY_FLIPS = ((0, 1), (1, 0), (1, 1))


def _half_rows(rows, half):
    return pl.ds(pl.multiple_of(half * (rows // 2), 8), rows // 2)


def _split_start(bufs, plan, n, name):
    nb = len(bufs)

    def body(*refs):
        send_sems, recv_sems, token = refs[nb], refs[nb + 1], refs[-1]
        for t, (src, dst, dev) in enumerate(plan(refs[:nb])):
            pltpu.make_async_remote_copy(src_ref=src, dst_ref=dst, send_sem=send_sems.at[t],
                                         recv_sem=recv_sems.at[t], device_id=dev, device_id_type=MESH).start()
        token[...] = jnp.zeros_like(token)

    out = pl.pallas_call(
        body, name=name,
        out_shape=(pltpu.SemaphoreType.DMA((n,)), pltpu.SemaphoreType.DMA((n,)),
                   *[pltpu.HBM(b.shape, b.dtype) for b in bufs], jax.ShapeDtypeStruct((8, 128), F32)),
        in_specs=[HBM_SPEC] * nb,
        out_specs=(SEM_SPEC, SEM_SPEC, *[HBM_SPEC] * nb, pl.BlockSpec(memory_space=pltpu.VMEM)),
        input_output_aliases={i: 2 + i for i in range(nb)},
        compiler_params=pltpu.CompilerParams(has_side_effects=DATAFLOW),
    )(*[pltpu.with_memory_space_constraint(b, pltpu.HBM) for b in bufs])
    return out[0], out[1], list(out[2:2 + nb]), out[-1]


def _split_wait(bufs, send_sems, recv_sems, plan, after, name):
    nb = len(bufs)

    def body(*refs):
        ss, rs = refs[nb], refs[nb + 1]
        for t, (src, dst, dev) in enumerate(plan(refs[:nb])):
            cp = pltpu.make_async_remote_copy(src_ref=src, dst_ref=dst, send_sem=ss.at[t], recv_sem=rs.at[t],
                                              device_id=dev, device_id_type=MESH)
            cp.wait_send()
            cp.wait_recv()

    out = pl.pallas_call(
        body, name=name,
        out_shape=tuple(pltpu.HBM(b.shape, b.dtype) for b in bufs),
        in_specs=[HBM_SPEC] * nb + [SEM_SPEC, SEM_SPEC] + [ANY_SPEC] * len(after),
        out_specs=tuple([HBM_SPEC] * nb),
        input_output_aliases={i: i for i in range(nb)},
        compiler_params=pltpu.CompilerParams(has_side_effects=DATAFLOW),
    )(*bufs, send_sems, recv_sems, *after)
    return list(out)


def _cast_place(w, chip_idx, name, deps=()):
    R, C = w.shape
    tr = _row_tile(R, 256) if R % 256 == 0 else R

    def body(k_ref, w_ref, o_ref):
        o_ref[...] = w_ref[...].astype(BF16)

    grid_spec = pltpu.PrefetchScalarGridSpec(
        num_scalar_prefetch=1, grid=(R // tr,),
        in_specs=[pl.BlockSpec((tr, C), lambda i, k: (i, 0))] + [pl.BlockSpec(memory_space=pl.ANY)] * len(deps),
        out_specs=pl.BlockSpec((None, tr, C), lambda i, k: (k[0], i, 0)),
    )
    return pl.pallas_call(
        _blind_to(body, 2, len(deps)), name=name, grid_spec=grid_spec,
        out_shape=jax.ShapeDtypeStruct((N_CHIPS, R, C), BF16),
        compiler_params=_params(("parallel",)),
    )(chip_idx, w, *deps)


def _plan_gather(refs):
    mx, my, mc = _coords()
    me = 2 * mx + my
    plan = []
    for g in refs:
        mine = g.at[me, _half_rows(g.shape[1], mc), :]
        for fx, fy in XY_FLIPS:
            plan.append((mine, mine, (mx ^ fx, my ^ fy, mc)))
    return plan


def _forward_halves(gathered, name):
    n = len(gathered)

    def body(*refs):
        outs = refs[n:2 * n]
        send_sems, recv_sems = refs[2 * n:]
        mx, my, mc = _coords()
        sibling = (mx, my, 1 - mc)

        def piece(g, t, half):
            fx, fy = XY_FLIPS[t]
            return g.at[2 * (mx ^ fx) + (my ^ fy), _half_rows(g.shape[1], half), :]

        def copy(i, t, half):
            p = piece(outs[i], t, half)
            return pltpu.make_async_remote_copy(src_ref=p, dst_ref=p, send_sem=send_sems.at[i, t],
                                                recv_sem=recv_sems.at[i, t], device_id=sibling, device_id_type=MESH)

        sends = [copy(i, t, mc) for i in range(n) for t in range(3)]
        for cp in sends:
            cp.start()
        for i in range(n):
            for t in range(3):
                copy(i, t, 1 - mc).wait_recv()
        for cp in sends:
            cp.wait_send()

    return pl.pallas_call(
        body, name=name,
        in_specs=[ANY_SPEC] * n, out_specs=[ANY_SPEC] * n,
        out_shape=[jax.ShapeDtypeStruct(g.shape, g.dtype) for g in gathered],
        input_output_aliases={i: i for i in range(n)},
        scratch_shapes=[pltpu.SemaphoreType.DMA((n, 3)), pltpu.SemaphoreType.DMA((n, 3))],
        compiler_params=pltpu.CompilerParams(has_side_effects=True),
    )(*gathered)


def _plan_other_halves(refs):
    n = len(refs) // 2
    mx, my, mc = _coords()
    return [(g.at[pl.ds(0, N_CHIPS), _half_rows(g.shape[1], 1 - mc), :], land, (mx, my, 1 - mc))
            for g, land in zip(refs[:n], refs[n:])]


def _plan_chunks(refs):
    n = len(refs) // 2
    mx, my, mc = _coords()
    plan = []
    for s, land in zip(refs[:n], refs[n:]):
        for t, (fx, fy) in enumerate(XY_FLIPS):
            plan.append((s.at[2 * (mx ^ fx) + (my ^ fy)], land.at[t], (mx ^ fx, my ^ fy, mc)))
    return plan


def _plan_share(refs):
    mx, my, mc = _coords()
    return [(full.at[mc], full.at[mc], (mx, my, 1 - mc)) for full in refs]


def _add_half(g, recv, core_idx, name):
    nk, R, C = g.shape
    rh = R // 2
    tr = _row_tile(rh, 128) if rh % 128 == 0 else rh
    nt = rh // tr

    def body(c_ref, g_ref, r_ref, o_ref):
        o_ref[...] = g_ref[...] + r_ref[...]

    grid_spec = pltpu.PrefetchScalarGridSpec(
        num_scalar_prefetch=1, grid=(nk, nt),
        in_specs=[pl.BlockSpec((None, tr, C), lambda k, i, c: (k, c[0] * nt + i, 0)),
                  pl.BlockSpec((None, tr, C), lambda k, i, c: (k, i, 0))],
        out_specs=pl.BlockSpec((None, tr, C), lambda k, i, c: (k, i, 0)),
    )
    return pl.pallas_call(
        body, name=name, grid_spec=grid_spec, out_shape=jax.ShapeDtypeStruct((nk, rh, C), F32),
        compiler_params=_params(("parallel", "parallel")),
    )(core_idx, g, recv)


def _sum_chips(s, land, chip_core, name):
    _, rh, C = s.shape
    tr = _row_tile(rh, 128) if rh % 128 == 0 else rh

    def body(p_ref, s_ref, l_ref, o_ref):
        me = p_ref[0]
        acc = None
        for j in range(N_CHIPS):
            t = jnp.maximum(jnp.bitwise_xor(me, j) - 1, 0)
            term = jnp.where(me == j, s_ref[...], l_ref[t])
            acc = term if acc is None else acc + term
        o_ref[...] = acc

    grid_spec = pltpu.PrefetchScalarGridSpec(
        num_scalar_prefetch=1, grid=(rh // tr,),
        in_specs=[pl.BlockSpec((None, tr, C), lambda i, p: (p[0], i, 0)),
                  pl.BlockSpec((3, tr, C), lambda i, p: (0, i, 0))],
        out_specs=pl.BlockSpec((None, tr, C), lambda i, p: (p[1], i, 0)),
    )
    return pl.pallas_call(
        body, name=name, grid_spec=grid_spec, out_shape=jax.ShapeDtypeStruct((2, rh, C), F32),
        compiler_params=_params(("parallel",)),
    )(chip_core, s, land)


class _Exchange:
    def __init__(self, bufs, plan, n, name):
        self.plan, self.name = plan, name
        self.send_sems, self.recv_sems, self.bufs, self.token = _split_start(bufs, plan, n, name + "_start")

    def wait(self, after):
        return _split_wait(self.bufs, self.send_sems, self.recv_sems, self.plan, after, self.name + "_wait")


def _empty_like_hbm(shape, dtype):
    return lax.empty(shape, dtype)


def _rs_send_halves(grads, tag):
    lands = [_empty_like_hbm((g.shape[0], g.shape[1] // 2, g.shape[2]), g.dtype) for g in grads]
    return _Exchange(list(grads) + lands, _plan_other_halves, len(grads), f"rs_halves_{tag}")


def _rs_send_chunks(ex, after, core_idx, tag):
    bufs = ex.wait(after)
    n = len(bufs) // 2
    sums = [_add_half(g, r, core_idx, f"rs_add_{tag}_{i}") for i, (g, r) in enumerate(zip(bufs[:n], bufs[n:]))]
    lands = [_empty_like_hbm((3,) + s.shape[1:], s.dtype) for s in sums]
    return _Exchange(sums + lands, _plan_chunks, 3 * n, f"rs_chunks_{tag}")


def _rs_send_share(ex, after, chip_core, tag):
    bufs = ex.wait(after)
    n = len(bufs) // 2
    fulls = [_sum_chips(s, l, chip_core, f"rs_sum_{tag}_{i}") for i, (s, l) in enumerate(zip(bufs[:n], bufs[n:]))]
    return _Exchange(fulls, _plan_share, n, f"rs_share_{tag}")


def _rs_finish(ex, after):
    return [b.reshape(2 * b.shape[1], b.shape[2]) for b in ex.wait(after)]


def _ffn_backward_rs(dh_out, saved, gain, sc, gate, wg, wu, wd, core_idx, tag):
    h, n, ga, up, act, f = saved
    df, d_gate = _gate_bwd(dh_out, f, gate, 0.5, f"{tag}_gate_bwd")
    dga, dup = _ffn_dact(df, wd, ga, up, f"{tag}_dact")
    dwd = _wgrad_chunk_lhs(act, df, f"{tag}_dwd")
    dwg, dwu = _wgrad_chunk_rhs(n, [dga, dup], wg.shape[0], True, f"{tag}_dwgu")
    ex = _rs_send_halves([dwg, dwu, dwd], tag)
    dn = _mm_nt_reduce([dga, dup], [wg, wu], True, f"{tag}_dn", deps=[ex.token])
    ex = _rs_send_chunks(ex, [dn], core_idx, tag)
    dh_in, d_sh, d_sc, d_gain = _norm_mod_bwd(dn, h, gain, sc, dh_out, f"{tag}_norm_bwd", deps=[ex.token])
    return dh_in, ex, (d_sh, d_sc, d_gate, d_gain)


def _mixer_backward_rs(dh2, saved, mix_norm_g, sc2, gt2, win, wout, conv_w, conv_ln_g, conv_ln_b, attn_out_g,
                       conv_out_g, core_idx):
    h1, n2, proj, cos, sin, q, k, v, attn, lse_tot, u1, y, mo = saved
    S, D = h1.shape
    dmo, d_gt2 = _gate_bwd(dh2, mo, gt2, 1.0, "mix_gate_bwd")
    dwout = _wgrad_chunk_lhs(y[None], dmo, "mix_dwout")
    dy = _mm_nt(dmo, wout.reshape(D, D), "mix_dy")
    dattn, delta, d_attn_g = _attn_merge_bwd(dy, attn, attn_out_g, "attn_merge_bwd")
    dqs, dks, dvs = [], [], []
    for dil in DILATIONS:
        dq_, dk_, dv_ = _attn_bwd(q, k, v, dattn, lse_tot, delta, dil, f"attn_bwd_d{dil}")
        dqs.append(dq_)
        dks.append(dk_)
        dvs.append(dv_)
    dq = _rope_bwd(dqs, cos, sin, HEAD_DIM ** -0.5, True, "dq_rope")
    dk = _rope_bwd(dks, cos, sin, 1.0, True, "dk_rope")
    dv = _rope_bwd(dvs, cos, sin, 1.0, False, "dv_sum")
    du1, d_gco, d_lng, d_lnb, d_cb = _conv_bwd_norms(dy, u1, conv_ln_g, conv_ln_b, conv_out_g, "conv_bwd_norms")
    dga_, dgb_, d_cw = _conv_bwd_taps(du1, proj, conv_w, "conv_bwd_taps")
    dproj = jnp.concatenate([dq, dk, dv, dga_, dgb_], axis=1)
    (dwin,) = _wgrad_chunk_rhs(n2, [dproj], N_CHIPS, False, "mix_dwin")
    ex = _rs_send_halves([dwin, dwout.reshape(N_CHIPS, D // N_CHIPS, D)], "mix")
    dn2 = _mm_nt_reduce([dproj], [win], False, "mix_dn", deps=[ex.token])
    ex = _rs_send_chunks(ex, [dn2], core_idx, "mix")
    dh1, d_sh2, d_sc2, d_gain2 = _norm_mod_bwd(dn2, h1, mix_norm_g, sc2, dh2, "mix_norm_bwd", deps=[ex.token])
    small = (d_sh2, d_sc2, d_gt2, d_gain2, d_cb, d_lng, d_lnb, d_attn_g, d_gco, d_cw)
    return dh1, ex, small


def kernel(x, c, w_ada, b_ada, ffn1_norm_g, ffn1_w_gate, ffn1_w_up, ffn1_w_down, mix_norm_g, w_in, conv_dw_w, conv_dw_b, conv_ln_g, conv_ln_b, attn_out_g, conv_out_g, w_out, ffn2_norm_g, ffn2_w_gate, ffn2_w_up, ffn2_w_down, final_norm_g, loss_target, m_w_ada, m_b_ada, m_ffn1_norm_g, m_ffn1_w_gate, m_ffn1_w_up, m_ffn1_w_down, m_mix_norm_g, m_w_in, m_conv_dw_w, m_conv_dw_b, m_conv_ln_g, m_conv_ln_b, m_attn_out_g, m_conv_out_g, m_w_out, m_ffn2_norm_g, m_ffn2_w_gate, m_ffn2_w_up, m_ffn2_w_down, m_final_norm_g, v_w_ada, v_b_ada, v_ffn1_norm_g, v_ffn1_w_gate, v_ffn1_w_up, v_ffn1_w_down, v_mix_norm_g, v_w_in, v_conv_dw_w, v_conv_dw_b, v_conv_ln_g, v_conv_ln_b, v_attn_out_g, v_conv_out_g, v_w_out, v_ffn2_norm_g, v_ffn2_w_gate, v_ffn2_w_up, v_ffn2_w_down, v_final_norm_g):
    S, D = x.shape[1], x.shape[2]
    mx, my, mc = _coords()
    chip = 2 * mx + my
    dev = 4 * mx + 2 * my + mc
    chip_idx = chip.astype(jnp.int32).reshape(1)
    core_idx = mc.astype(jnp.int32).reshape(1)
    chip_core = jnp.stack([chip, mc]).astype(jnp.int32)
    h0 = x[0]
    target = loss_target[0]

    def gather_start(ws, tag, *deps):
        slots = [_cast_place(w[0], chip_idx, f"cast_{tag}_{i}", deps=deps) for i, w in enumerate(ws)]
        return _Exchange(slots, _plan_gather, 3 * len(ws), f"gather_{tag}")

    def gather_finish(ex, after, tag):
        return _forward_halves(ex.wait(after), f"forward_{tag}")

    ex_w1 = gather_start([ffn1_w_gate, ffn1_w_up, ffn1_w_down], "ffn1")
    ex_wm = gather_start([w_in, w_out], "mix", ex_w1.token)
    ex_w2 = gather_start([ffn2_w_gate, ffn2_w_up, ffn2_w_down], "ffn2", ex_wm.token)

    ncw = CONV_KERNEL * 128
    n0 = -(-(D + ncw) // 1024) * 1024
    pk0 = _pad_cols(jnp.concatenate([c.reshape(1, D), conv_dw_w.reshape(1, ncw)], axis=1), n0)
    g0 = _all_gather8(pk0.reshape(8, n0 // 8), "gather_c", deps=[ex_w2.token]).reshape(N_DEV, n0)
    c_all = g0[:, :D]
    conv_w = jnp.concatenate([g0[2 * kc, D:D + ncw].reshape(CONV_KERNEL, 128) for kc in range(N_CHIPS)], axis=1)
    conv_w = jnp.pad(conv_w, ((0, HALO - CONV_KERNEL), (0, 0)))
    nmod = w_ada.shape[2]
    b_shard = lax.dynamic_slice(b_ada, (0, chip * nmod), (1, nmod))
    mod_part = _ada_fwd(c_all, w_ada[0], b_shard, "ada_fwd")
    g1 = _all_gather8(mod_part, "gather_mod")
    mod_all = jnp.concatenate([g1[16 * kc:16 * kc + 8] for kc in range(N_CHIPS)], axis=1)
    mod = lax.dynamic_slice(mod_all, (dev, 0), (1, 9 * D))
    sh1, sc1, gt1, sh2, sc2, gt2, sh3, sc3, gt3 = [mod[:, i * D:(i + 1) * D] for i in range(9)]

    wg1, wu1, wd1 = gather_finish(ex_w1, [mod], "ffn1")
    h1, saved1 = _ffn_forward(h0, ffn1_norm_g, sc1, sh1, gt1, wg1, wu1, wd1, "ffn1")
    win, wout = gather_finish(ex_wm, [h1], "mix")
    h2, saved2 = _mixer_forward(h1, mix_norm_g, sc2, sh2, gt2, win, wout, conv_w, conv_dw_b, conv_ln_g, conv_ln_b,
                                attn_out_g, conv_out_g)
    wg2, wu2, wd2 = gather_finish(ex_w2, [h2], "ffn2")
    h3, saved3 = _ffn_forward(h2, ffn2_norm_g, sc3, sh3, gt3, wg2, wu2, wd2, "ffn2")
    loss_part, dh3, d_final_g = _loss_head(h3, final_norm_g.reshape(1, D), target, "loss_head")

    dh2, ex_g2, (d_sh3, d_sc3, d_gt3, d_gain3) = _ffn_backward_rs(
        dh3, saved3, ffn2_norm_g, sc3, gt3, wg2, wu2, wd2, core_idx, "ffn2")
    dh1, ex_gm, small_mix = _mixer_backward_rs(
        dh2, saved2, mix_norm_g, sc2, gt2, win, wout, conv_w, conv_ln_g, conv_ln_b, attn_out_g, conv_out_g, core_idx)
    d_sh2, d_sc2, d_gt2, d_gain2, d_cb, d_lng, d_lnb, d_attn_g, d_gco, d_cw = small_mix
    dh0, ex_g1, (d_sh1, d_sc1, d_gt1, d_gain1) = _ffn_backward_rs(
        dh1, saved1, ffn1_norm_g, sc1, gt1, wg1, wu1, wd1, core_idx, "ffn1")
    ex_g2 = _rs_send_share(ex_g2, [dh0], chip_core, "ffn2")
    ex_gm = _rs_send_share(ex_gm, [ex_g2.token], chip_core, "mix")

    dmod = jnp.concatenate([d_sh1, d_sc1, d_gt1, d_sh2, d_sc2, d_gt2, d_sh3, d_sc3, d_gt3], axis=1)
    small = [d_gain1, d_gain2, d_gain3, d_final_g, d_cb, d_lng, d_lnb, d_attn_g, d_gco,
             d_cw[:CONV_KERNEL].reshape(1, CONV_KERNEL * 512), loss_part]
    pk1 = jnp.concatenate([dmod] + small, axis=1)
    n1 = -(-pk1.shape[1] // 1024) * 1024
    gathered = _all_gather8(_pad_cols(pk1, n1).reshape(8, n1 // 8), "gather_small",
                            deps=[ex_gm.token]).reshape(N_DEV, n1)
    tot = _sum_rows(gathered, "sum_small")
    off = [0]

    def take(nel):
        out = tot[:, off[0]:off[0] + nel]
        off[0] += nel
        return out

    g_b_ada = take(9 * D)
    g_ffn1_norm, g_mix_norm, g_ffn2_norm, g_final = take(D), take(D), take(D), take(D)
    g_cb, g_lng, g_lnb, g_attn_g, g_gco = take(512), take(512), take(512), take(512), take(512)
    g_cw_full = take(CONV_KERNEL * 512).reshape(CONV_KERNEL, 512)
    loss = take(128)[0, 0]
    g_cw = lax.dynamic_slice(g_cw_full, (0, chip * 128), (CONV_KERNEL, 128))

    dmod_shard = lax.dynamic_slice(gathered[:, :9 * D], (0, chip * nmod), (N_DEV, nmod))
    dmod16 = jnp.pad(dmod_shard, ((0, N_DEV), (0, 0)))
    c_t16 = jnp.pad(c_all.T, ((0, 0), (0, N_DEV)))
    g_w_ada = _ada_wgrad(c_t16, dmod16, "ada_wgrad")

    names = ["w_ada", "b_ada", "ffn1_norm_g", "ffn1_w_gate", "ffn1_w_up", "ffn1_w_down", "mix_norm_g", "w_in",
             "conv_dw_w", "conv_dw_b", "conv_ln_g", "conv_ln_b", "attn_out_g", "conv_out_g", "w_out", "ffn2_norm_g",
             "ffn2_w_gate", "ffn2_w_up", "ffn2_w_down", "final_norm_g"]
    weights = dict(zip(names, [w_ada, b_ada, ffn1_norm_g, ffn1_w_gate, ffn1_w_up, ffn1_w_down, mix_norm_g, w_in,
                               conv_dw_w, conv_dw_b, conv_ln_g, conv_ln_b, attn_out_g, conv_out_g, w_out,
                               ffn2_norm_g, ffn2_w_gate, ffn2_w_up, ffn2_w_down, final_norm_g]))
    ms = dict(zip(names, [m_w_ada, m_b_ada, m_ffn1_norm_g, m_ffn1_w_gate, m_ffn1_w_up, m_ffn1_w_down, m_mix_norm_g,
                          m_w_in, m_conv_dw_w, m_conv_dw_b, m_conv_ln_g, m_conv_ln_b, m_attn_out_g, m_conv_out_g,
                          m_w_out, m_ffn2_norm_g, m_ffn2_w_gate, m_ffn2_w_up, m_ffn2_w_down, m_final_norm_g]))
    vs = dict(zip(names, [v_w_ada, v_b_ada, v_ffn1_norm_g, v_ffn1_w_gate, v_ffn1_w_up, v_ffn1_w_down, v_mix_norm_g,
                          v_w_in, v_conv_dw_w, v_conv_dw_b, v_conv_ln_g, v_conv_ln_b, v_attn_out_g, v_conv_out_g,
                          v_w_out, v_ffn2_norm_g, v_ffn2_w_gate, v_ffn2_w_up, v_ffn2_w_down, v_final_norm_g]))
    grads, deltas, new_ms, new_vs = {}, {}, {}, {}

    def adamw_big(nm, g2d):
        shape = weights[nm].shape
        two_d = (shape[-2], shape[-1])
        d_, m_, v_ = _adamw(weights[nm].reshape(two_d), g2d.reshape(two_d), ms[nm].reshape(two_d),
                            vs[nm].reshape(two_d), f"adamw_{nm}")
        grads[nm], deltas[nm], new_ms[nm], new_vs[nm] = (t.reshape(shape) for t in (g2d, d_, m_, v_))
        return d_

    d_ada = adamw_big("w_ada", g_w_ada)
    small_grads = {"b_ada": g_b_ada, "ffn1_norm_g": g_ffn1_norm, "mix_norm_g": g_mix_norm, "conv_dw_w": g_cw,
                   "conv_dw_b": g_cb, "conv_ln_g": g_lng, "conv_ln_b": g_lnb, "attn_out_g": g_attn_g,
                   "conv_out_g": g_gco, "ffn2_norm_g": g_ffn2_norm, "final_norm_g": g_final}
    small_names = [nm for nm in names if nm in small_grads]

    def pack_small(arrs):
        flat = jnp.concatenate([arrs[nm].reshape(1, -1) for nm in small_names], axis=1)
        npad = -(-flat.shape[1] // 1024) * 1024
        return _pad_cols(flat, npad).reshape(8, npad // 8)

    d_s, m_s, v_s = _adamw(pack_small(weights), pack_small(small_grads), pack_small(ms), pack_small(vs),
                           "adamw_small")
    pos = 0
    for nm in small_names:
        shape, nel = weights[nm].shape, weights[nm].size
        grads[nm] = small_grads[nm].reshape(shape)
        deltas[nm], new_ms[nm], new_vs[nm] = (t.reshape(1, -1)[:, pos:pos + nel].reshape(shape)
                                              for t in (d_s, m_s, v_s))
        pos += nel

    g_wg2, g_wu2, g_wd2 = _rs_finish(ex_g2, [d_ada, d_s])
    last = [adamw_big(nm, g) for nm, g in (("ffn2_w_gate", g_wg2), ("ffn2_w_up", g_wu2), ("ffn2_w_down", g_wd2))]
    g_win, g_wout = _rs_finish(ex_gm, last)
    last = [adamw_big(nm, g) for nm, g in (("w_in", g_win), ("w_out", g_wout))]
    ex_g1 = _rs_send_share(ex_g1, last, chip_core, "ffn1")
    g_wg1, g_wu1, g_wd1 = _rs_finish(ex_g1, [])
    for nm, g in (("ffn1_w_gate", g_wg1), ("ffn1_w_up", g_wu1), ("ffn1_w_down", g_wd1)):
        adamw_big(nm, g)

    return (loss, dh0[None], *[grads[nm] for nm in names], *[deltas[nm] for nm in names],
            *[new_ms[nm] for nm in names], *[new_vs[nm] for nm in names])
```

```python
import jax
import jax.numpy as jnp
from jax import lax
from jax.experimental import pallas as pl
from jax.experimental.pallas import tpu as pltpu

F32 = jnp.float32
BF16 = jnp.bfloat16
MESH = pl.DeviceIdType.MESH

RMS_EPS = 1e-6
LN_EPS = 1e-5
HEAD_DIM = 64
ATTN_WIDTH = 512
CONV_WIDTH = 512
ATTN_BLOCK = 128
DILATIONS = (1, 4, 16)
SUPER_ROWS = ATTN_BLOCK * 16
ROPE_THETA = 10000.0
CONV_KERNEL = 31
HALO = 32
N_CHIPS = 4
N_DEV = 8
ADAM_LR, ADAM_B1, ADAM_B2, ADAM_EPS, ADAM_WD, ADAM_STEP = 0.001, 0.9, 0.999, 1e-08, 0.01, 10
VMEM_LIMIT_BYTES = 48 * 1024 * 1024
NEG = -1e30

NT = (((1,), (1,)), ((), ()))
TN = (((0,), (0,)), ((), ()))

ANY_SPEC = pl.BlockSpec(memory_space=pl.ANY)
HBM_SPEC = pl.BlockSpec(memory_space=pltpu.HBM)
SEM_SPEC = pl.BlockSpec(memory_space=pltpu.SEMAPHORE)
DATAFLOW = pltpu.SideEffectType.DATAFLOW_SIDE_EFFECTING
XY_FLIPS = ((0, 1), (1, 0), (1, 1))


def _params(sem=None):
    return pltpu.CompilerParams(dimension_semantics=sem, vmem_limit_bytes=VMEM_LIMIT_BYTES)


def _row_tile(rows, want):
    t = min(rows, want)
    assert rows % t == 0
    return t


def _sigmoid(x):
    return 1.0 / (1.0 + jnp.exp(-x))


def _deps(deps):
    return [d for d in deps if d is not None]


def _blind_to(body, n_in, n_dep):
    def wrapped(*refs):
        return body(*refs[:n_in], *refs[n_in + n_dep:])
    return wrapped


def _vec_spec(d, ngrid):
    if ngrid == 1:
        return pl.BlockSpec((1, d), lambda i: (0, 0))
    return pl.BlockSpec((1, d), lambda i, j: (0, 0))


def _norm_mod(h, gain, sc, sh, name):
    S, D = h.shape
    tr = _row_tile(S, 512)

    def body(h_ref, g_ref, sc_ref, sh_ref, n_ref):
        x = h_ref[...]
        r = lax.rsqrt(jnp.mean(x * x, axis=-1, keepdims=True) + RMS_EPS)
        y = (x * r) * g_ref[...]
        n_ref[...] = (y * (1.0 + sc_ref[...]) + sh_ref[...]).astype(BF16)

    row = pl.BlockSpec((tr, D), lambda i: (i, 0))
    return pl.pallas_call(
        body, name=name, grid=(S // tr,),
        in_specs=[row, _vec_spec(D, 1), _vec_spec(D, 1), _vec_spec(D, 1)],
        out_specs=row, out_shape=jax.ShapeDtypeStruct((S, D), BF16),
        compiler_params=_params(("parallel",)),
    )(h, gain, sc, sh)


def _norm_mod_bwd(dn, h_in, gain, sc, dh_out, name, deps=()):
    S, D = h_in.shape
    tr = _row_tile(S, 512)
    deps = _deps(deps)

    def body(dn_ref, h_ref, g_ref, sc_ref, dho_ref, dh_ref, dsh_ref, dsc_ref, dg_ref):
        @pl.when(pl.program_id(0) == 0)
        def _():
            dsh_ref[...] = jnp.zeros_like(dsh_ref)
            dsc_ref[...] = jnp.zeros_like(dsc_ref)
            dg_ref[...] = jnp.zeros_like(dg_ref)

        x = h_ref[...]
        dn_ = dn_ref[...]
        g = g_ref[...]
        one_sc = 1.0 + sc_ref[...]
        r = lax.rsqrt(jnp.mean(x * x, axis=-1, keepdims=True) + RMS_EPS)
        xh = x * r
        dsh_ref[...] += jnp.sum(dn_, axis=0, keepdims=True)
        dsc_ref[...] += jnp.sum(dn_ * (xh * g), axis=0, keepdims=True)
        dg_ref[...] += jnp.sum(dn_ * one_sc * xh, axis=0, keepdims=True)
        dxh = dn_ * (g * one_sc)
        dh_ref[...] = dho_ref[...] + r * (dxh - xh * jnp.mean(dxh * xh, axis=-1, keepdims=True))

    row = pl.BlockSpec((tr, D), lambda i: (i, 0))
    vec = _vec_spec(D, 1)
    return pl.pallas_call(
        _blind_to(body, 5, len(deps)), name=name, grid=(S // tr,),
        in_specs=[row, row, vec, vec, row] + [ANY_SPEC] * len(deps),
        out_specs=[row, vec, vec, vec],
        out_shape=[jax.ShapeDtypeStruct((S, D), F32)] + [jax.ShapeDtypeStruct((1, D), F32)] * 3,
        compiler_params=_params(("arbitrary",)),
    )(dn, h_in, gain, sc, dh_out, *deps)


def _gate_bwd(dh, f, gvec, coef, name):
    S, D = dh.shape
    tr = _row_tile(S, 512)

    def body(dh_ref, f_ref, g_ref, df_ref, dg_ref):
        @pl.when(pl.program_id(0) == 0)
        def _():
            dg_ref[...] = jnp.zeros_like(dg_ref)

        dh_ = dh_ref[...]
        df_ref[...] = ((coef * g_ref[...]) * dh_).astype(BF16)
        dg_ref[...] += jnp.sum(coef * dh_ * f_ref[...].astype(F32), axis=0, keepdims=True)

    row = pl.BlockSpec((tr, D), lambda i: (i, 0))
    vec = _vec_spec(D, 1)
    return pl.pallas_call(
        body, name=name, grid=(S // tr,),
        in_specs=[row, row, vec], out_specs=[row, vec],
        out_shape=[jax.ShapeDtypeStruct((S, D), BF16), jax.ShapeDtypeStruct((1, D), F32)],
        compiler_params=_params(("arbitrary",)),
    )(dh, f, gvec)


def _loss_head(h, gain, target, name):
    S, D = h.shape
    tr = _row_tile(S, 512)

    def body(h_ref, g_ref, t_ref, loss_ref, dh_ref, dg_ref):
        @pl.when(pl.program_id(0) == 0)
        def _():
            loss_ref[...] = jnp.zeros_like(loss_ref)
            dg_ref[...] = jnp.zeros_like(dg_ref)

        x = h_ref[...]
        g = g_ref[...]
        r = lax.rsqrt(jnp.mean(x * x, axis=-1, keepdims=True) + RMS_EPS)
        xh = x * r
        err = xh * g - t_ref[...]
        part = 0.5 * jnp.sum(jnp.mean(err * err, axis=-1, keepdims=True), axis=0, keepdims=True)
        loss_ref[...] += jnp.broadcast_to(part, loss_ref.shape)
        dy = err * (1.0 / D)
        dg_ref[...] += jnp.sum(dy * xh, axis=0, keepdims=True)
        dxh = dy * g
        dh_ref[...] = r * (dxh - xh * jnp.mean(dxh * xh, axis=-1, keepdims=True))

    row = pl.BlockSpec((tr, D), lambda i: (i, 0))
    vec = _vec_spec(D, 1)
    return pl.pallas_call(
        body, name=name, grid=(S // tr,),
        in_specs=[row, vec, row],
        out_specs=[pl.BlockSpec((1, 128), lambda i: (0, 0)), row, vec],
        out_shape=[jax.ShapeDtypeStruct((1, 128), F32), jax.ShapeDtypeStruct((S, D), F32),
                   jax.ShapeDtypeStruct((1, D), F32)],
        compiler_params=_params(("arbitrary",)),
    )(h, gain, target)


def _ffn_gate_up(n, wg, wu, name):
    S, D = n.shape
    nk, _, w = wg.shape
    tm = _row_tile(S, 512)

    def body(n_ref, wg_ref, wu_ref, ga_ref, up_ref, act_ref):
        x = n_ref[...]
        ga = jnp.dot(x, wg_ref[...], preferred_element_type=F32)
        up = jnp.dot(x, wu_ref[...], preferred_element_type=F32)
        ga_ref[...] = ga.astype(BF16)
        up_ref[...] = up.astype(BF16)
        act_ref[...] = ((ga * _sigmoid(ga)) * up).astype(BF16)

    wspec = pl.BlockSpec((None, D, w), lambda k, m: (k, 0, 0))
    ospec = pl.BlockSpec((None, tm, w), lambda k, m: (k, m, 0))
    out = jax.ShapeDtypeStruct((nk, S, w), BF16)
    return pl.pallas_call(
        body, name=name, grid=(nk, S // tm),
        in_specs=[pl.BlockSpec((tm, D), lambda k, m: (m, 0)), wspec, wspec],
        out_specs=[ospec, ospec, ospec], out_shape=[out, out, out],
        compiler_params=_params(("parallel", "parallel")),
    )(n, wg, wu)


def _mm_residual(lhs, w, h_in, gvec, coef, name):
    nk, S, kc = lhs.shape
    D = w.shape[2]
    tm = _row_tile(S, 512)

    def body(l_ref, w_ref, h_ref, g_ref, ho_ref, f_ref, acc_ref):
        k = pl.program_id(1)

        @pl.when(k == 0)
        def _():
            acc_ref[...] = jnp.zeros_like(acc_ref)

        acc_ref[...] += jnp.dot(l_ref[...], w_ref[...], preferred_element_type=F32)

        @pl.when(k == nk - 1)
        def _():
            f = acc_ref[...]
            f_ref[...] = f.astype(BF16)
            ho_ref[...] = h_ref[...] + (coef * g_ref[...]) * f

    row = pl.BlockSpec((tm, D), lambda m, k: (m, 0))
    return pl.pallas_call(
        body, name=name, grid=(S // tm, nk),
        in_specs=[pl.BlockSpec((None, tm, kc), lambda m, k: (k, m, 0)),
                  pl.BlockSpec((None, kc, D), lambda m, k: (k, 0, 0)), row, _vec_spec(D, 2)],
        out_specs=[row, row],
        out_shape=[jax.ShapeDtypeStruct((S, D), F32), jax.ShapeDtypeStruct((S, D), BF16)],
        scratch_shapes=[pltpu.VMEM((tm, D), F32)],
        compiler_params=_params(("parallel", "arbitrary")),
    )(lhs, w, h_in, gvec)


def _mm_cols(n, w, name):
    S, D = n.shape
    nk, _, wd = w.shape
    assert wd % 128 == 0
    tm = _row_tile(S, 512)

    def body(n_ref, w_ref, o_ref):
        o_ref[...] = jnp.dot(n_ref[...], w_ref[...], preferred_element_type=F32)

    return pl.pallas_call(
        body, name=name, grid=(nk, S // tm),
        in_specs=[pl.BlockSpec((tm, D), lambda k, m: (m, 0)), pl.BlockSpec((None, D, wd), lambda k, m: (k, 0, 0))],
        out_specs=pl.BlockSpec((tm, wd), lambda k, m: (m, k)),
        out_shape=jax.ShapeDtypeStruct((S, nk * wd), F32),
        compiler_params=_params(("parallel", "parallel")),
    )(n, w)


def _ffn_dact(df, wd, ga, up, name, deps=()):
    S, D = df.shape
    nk, w, _ = wd.shape
    tm = _row_tile(S, 512)
    deps = _deps(deps)

    def body(df_ref, wd_ref, ga_ref, up_ref, dga_ref, dup_ref):
        dact = lax.dot_general(df_ref[...], wd_ref[...], NT, preferred_element_type=F32)
        ga_ = ga_ref[...].astype(F32)
        up_ = up_ref[...].astype(F32)
        sig = _sigmoid(ga_)
        dga_ref[...] = (dact * up_ * (sig * (1.0 + ga_ * (1.0 - sig)))).astype(BF16)
        dup_ref[...] = (dact * (ga_ * sig)).astype(BF16)

    cspec = pl.BlockSpec((None, tm, w), lambda k, m: (k, m, 0))
    out = jax.ShapeDtypeStruct((nk, S, w), BF16)
    return pl.pallas_call(
        _blind_to(body, 4, len(deps)), name=name, grid=(nk, S // tm),
        in_specs=[pl.BlockSpec((tm, D), lambda k, m: (m, 0)), pl.BlockSpec((None, w, D), lambda k, m: (k, 0, 0)),
                  cspec, cspec] + [ANY_SPEC] * len(deps),
        out_specs=[cspec, cspec], out_shape=[out, out],
        compiler_params=_params(("parallel", "parallel")),
    )(df, wd, ga, up, *deps)


def _mm_nt(d, w, name):
    S, K = d.shape
    N = w.shape[0]
    tm = _row_tile(S, 512)

    def body(d_ref, w_ref, o_ref):
        o_ref[...] = lax.dot_general(d_ref[...], w_ref[...], NT, preferred_element_type=F32)

    return pl.pallas_call(
        body, name=name, grid=(S // tm,),
        in_specs=[pl.BlockSpec((tm, K), lambda m: (m, 0)), pl.BlockSpec((N, K), lambda m: (0, 0))],
        out_specs=pl.BlockSpec((tm, N), lambda m: (m, 0)),
        out_shape=jax.ShapeDtypeStruct((S, N), F32),
        compiler_params=_params(("parallel",)),
    )(d, w)


def _mm_nt_reduce(lhs_list, w_list, chunked3d, name, deps=()):
    nk, D, kc = w_list[0].shape
    S = lhs_list[0].shape[1] if chunked3d else lhs_list[0].shape[0]
    tm = _row_tile(S, 512)
    npair = len(lhs_list)
    deps = _deps(deps)

    def body(*refs):
        l_refs, w_refs = refs[:npair], refs[npair:2 * npair]
        o_ref, acc_ref = refs[2 * npair], refs[2 * npair + 1]
        k = pl.program_id(1)

        @pl.when(k == 0)
        def _():
            acc_ref[...] = jnp.zeros_like(acc_ref)

        for l_ref, w_ref in zip(l_refs, w_refs):
            acc_ref[...] += lax.dot_general(l_ref[...], w_ref[...], NT, preferred_element_type=F32)

        @pl.when(k == nk - 1)
        def _():
            o_ref[...] = acc_ref[...]

    if chunked3d:
        lspec = pl.BlockSpec((None, tm, kc), lambda m, k: (k, m, 0))
    else:
        lspec = pl.BlockSpec((tm, kc), lambda m, k: (m, k))
    wspec = pl.BlockSpec((None, D, kc), lambda m, k: (k, 0, 0))
    return pl.pallas_call(
        _blind_to(body, 2 * npair, len(deps)), name=name, grid=(S // tm, nk),
        in_specs=[lspec] * npair + [wspec] * npair + [ANY_SPEC] * len(deps),
        out_specs=pl.BlockSpec((tm, D), lambda m, k: (m, 0)),
        out_shape=jax.ShapeDtypeStruct((S, D), F32),
        scratch_shapes=[pltpu.VMEM((tm, D), F32)],
        compiler_params=_params(("parallel", "arbitrary")),
    )(*lhs_list, *w_list, *deps)


def _wgrad_chunk_lhs(lhs, rhs, name):
    nk, S, w = lhs.shape
    D = rhs.shape[1]
    ts = _row_tile(S, 512)
    ns = S // ts

    def body(l_ref, r_ref, o_ref, acc_ref):
        s = pl.program_id(1)

        @pl.when(s == 0)
        def _():
            acc_ref[...] = jnp.zeros_like(acc_ref)

        acc_ref[...] += lax.dot_general(l_ref[...], r_ref[...], TN, preferred_element_type=F32)

        @pl.when(s == ns - 1)
        def _():
            o_ref[...] = acc_ref[...]

    return pl.pallas_call(
        body, name=name, grid=(nk, ns),
        in_specs=[pl.BlockSpec((None, ts, w), lambda k, s: (k, s, 0)), pl.BlockSpec((ts, D), lambda k, s: (s, 0))],
        out_specs=pl.BlockSpec((None, w, D), lambda k, s: (k, 0, 0)),
        out_shape=jax.ShapeDtypeStruct((nk, w, D), F32),
        scratch_shapes=[pltpu.VMEM((w, D), F32)],
        compiler_params=_params(("parallel", "arbitrary")),
    )(lhs, rhs)


def _wgrad_chunk_rhs(lhs, rhs_list, nk, chunked3d, name, deps=()):
    S, D = lhs.shape
    w = rhs_list[0].shape[2] if chunked3d else rhs_list[0].shape[1] // nk
    ts = _row_tile(S, 512)
    ns = S // ts
    nr = len(rhs_list)
    deps = _deps(deps)

    def body(*refs):
        l_ref, r_refs = refs[0], refs[1:1 + nr]
        o_refs, acc_refs = refs[1 + nr:1 + 2 * nr], refs[1 + 2 * nr:]
        s = pl.program_id(1)

        @pl.when(s == 0)
        def _():
            for acc_ref in acc_refs:
                acc_ref[...] = jnp.zeros_like(acc_ref)

        x = l_ref[...]
        for r_ref, acc_ref in zip(r_refs, acc_refs):
            acc_ref[...] += lax.dot_general(x, r_ref[...], TN, preferred_element_type=F32)

        @pl.when(s == ns - 1)
        def _():
            for o_ref, acc_ref in zip(o_refs, acc_refs):
                o_ref[...] = acc_ref[...]

    if chunked3d:
        rspec = pl.BlockSpec((None, ts, w), lambda k, s: (k, s, 0))
    else:
        rspec = pl.BlockSpec((ts, w), lambda k, s: (s, k))
    ospec = pl.BlockSpec((None, D, w), lambda k, s: (k, 0, 0))
    return pl.pallas_call(
        _blind_to(body, 1 + nr, len(deps)), name=name, grid=(nk, ns),
        in_specs=[pl.BlockSpec((ts, D), lambda k, s: (s, 0))] + [rspec] * nr + [ANY_SPEC] * len(deps),
        out_specs=[ospec] * nr,
        out_shape=[jax.ShapeDtypeStruct((nk, D, w), F32)] * nr,
        scratch_shapes=[pltpu.VMEM((D, w), F32)] * nr,
        compiler_params=_params(("parallel", "arbitrary")),
    )(lhs, *rhs_list, *deps)


def _rope_tables(S):
    pos = jnp.arange(S, dtype=F32)
    inv_freq = ROPE_THETA ** (-jnp.arange(0, HEAD_DIM, 2, dtype=F32) / HEAD_DIM)
    ang = pos[:, None] * inv_freq[None, :]
    cos, sin = jnp.cos(ang), jnp.sin(ang)
    cos2 = jnp.concatenate([cos, cos, cos, cos], axis=1)
    sin2 = jnp.concatenate([-sin, sin, -sin, sin], axis=1)
    return cos2, sin2


def _rotate(t, cos, sin_signed):
    half = HEAD_DIM // 2
    lane = lax.broadcasted_iota(jnp.int32, t.shape, 1)
    first = (lane % HEAD_DIM) < half
    partner = jnp.where(first, pltpu.roll(t, 128 - half, 1), pltpu.roll(t, half, 1))
    return t * cos + partner * sin_signed


def _qkv_rope(proj, cos, sin, name):
    S = proj.shape[0]
    A = ATTN_WIDTH
    tr = _row_tile(S, 512)
    nb = A // 128
    scale = HEAD_DIM ** -0.5

    def body(q_ref, k_ref, v_ref, c_ref, s_ref, qo_ref, ko_ref, vo_ref):
        c, s = c_ref[...], s_ref[...]
        qo_ref[...] = _rotate(q_ref[...], c, s) * scale
        ko_ref[...] = _rotate(k_ref[...], c, s)
        vo_ref[...] = v_ref[...]

    def col(off):
        return pl.BlockSpec((tr, 128), lambda i, j: (i, off + j))

    tab = pl.BlockSpec((tr, 128), lambda i, j: (i, 0))
    out = jax.ShapeDtypeStruct((S, A), F32)
    return pl.pallas_call(
        body, name=name, grid=(S // tr, nb),
        in_specs=[col(0), col(nb), col(2 * nb), tab, tab],
        out_specs=[col(0), col(0), col(0)], out_shape=[out, out, out],
        compiler_params=_params(("parallel", "parallel")),
    )(proj, proj, proj, cos, sin)


def _band_masks(T, has_prev):
    qi = lax.broadcasted_iota(jnp.int32, (T, T), 0)
    kj = lax.broadcasted_iota(jnp.int32, (T, T), 1)
    return qi >= kj, (kj >= qi) & has_prev


def _branch_blocks(rows, dilation):
    T = min(ATTN_BLOCK, rows // dilation)
    return T, rows // T


def _block_rows(base, T, dilation):
    if dilation == 1:
        return pl.ds(pl.multiple_of(base, T), T)
    return pl.ds(base, T, stride=dilation)


def _attn_fwd(q, k, v, name):
    S, A = q.shape
    sup = min(S, SUPER_ROWS)
    nd = len(DILATIONS)
    assert S % sup == 0

    def body(q_ref, k_ref, v_ref, attn_ref, lse_ref, acc_s, m_s, l_s):
        lane = lax.broadcasted_iota(jnp.int32, (1, 128), 1)
        head0 = lane < HEAD_DIM

        def supertile(st, carry):
            row0 = st * sup
            for di, dil in enumerate(DILATIONS):
                T, nblk = _branch_blocks(sup, dil)
                span = T * dil
                assert T == ATTN_BLOCK or sup == S

                def block(idx, c2, di=di, dil=dil, T=T, span=span):
                    r = idx % dil
                    loc = (idx // dil) * span + r
                    base = row0 + loc
                    rows = _block_rows(base, T, dil)
                    prev = _block_rows(jnp.maximum(base - span, r), T, dil)
                    qb = q_ref[rows, :].astype(BF16)
                    kc, vc = k_ref[rows, :].astype(BF16), v_ref[rows, :].astype(BF16)
                    kp, vp = k_ref[prev, :].astype(BF16), v_ref[prev, :].astype(BF16)
                    valid_d, valid_o = _band_masks(T, base >= span)
                    accs, ms, ls = [], [], []
                    for hmask in (head0, jnp.logical_not(head0)):
                        qh = jnp.where(hmask, qb, jnp.zeros_like(qb))
                        sd = jnp.where(valid_d, lax.dot_general(qh, kc, NT, preferred_element_type=F32), NEG)
                        so = jnp.where(valid_o, lax.dot_general(qh, kp, NT, preferred_element_type=F32), NEG)
                        m = jnp.maximum(jnp.max(sd, axis=-1, keepdims=True), jnp.max(so, axis=-1, keepdims=True))
                        pd = jnp.exp(sd - m)
                        po = jnp.exp(so - m)
                        ls.append(jnp.sum(pd, axis=-1, keepdims=True) + jnp.sum(po, axis=-1, keepdims=True))
                        acc = jnp.dot(pd.astype(BF16), vc, preferred_element_type=F32)
                        accs.append(acc + jnp.dot(po.astype(BF16), vp, preferred_element_type=F32))
                        ms.append(m)
                    lrows = _block_rows(di * sup + loc, T, dil)
                    acc_s[lrows, :] = jnp.where(head0, accs[0], accs[1])
                    m_s[lrows, :] = jnp.where(head0, ms[0], ms[1])
                    l_s[lrows, :] = jnp.where(head0, ls[0], ls[1])
                    return c2

                lax.fori_loop(0, nblk, block, 0)

            chunk = min(sup, 256)

            def merge(ci, c2):
                lr = [pl.ds(pl.multiple_of(di * sup + ci * chunk, chunk), chunk) for di in range(nd)]
                gr = pl.ds(pl.multiple_of(row0 + ci * chunk, chunk), chunk)
                m0, m1, m2 = m_s[lr[0], :], m_s[lr[1], :], m_s[lr[2], :]
                mm = jnp.maximum(jnp.maximum(m0, m1), m2)
                w0, w1, w2 = jnp.exp(m0 - mm), jnp.exp(m1 - mm), jnp.exp(m2 - mm)
                den = (w0 * l_s[lr[0], :] + w1 * l_s[lr[1], :]) + w2 * l_s[lr[2], :]
                num = (w0 * acc_s[lr[0], :] + w1 * acc_s[lr[1], :]) + w2 * acc_s[lr[2], :]
                attn_ref[gr, :] = num / den
                lse_ref[gr, :] = mm + jnp.log(den)
                return c2

            lax.fori_loop(0, sup // chunk, merge, 0)
            return carry

        lax.fori_loop(0, S // sup, supertile, 0)

    blk = pl.BlockSpec((S, 128), lambda j: (0, j))
    out = jax.ShapeDtypeStruct((S, A), F32)
    return pl.pallas_call(
        body, name=name, grid=(A // 128,),
        in_specs=[blk, blk, blk], out_specs=[blk, blk], out_shape=[out, out],
        scratch_shapes=[pltpu.VMEM((nd * sup, 128), F32)] * 3,
        compiler_params=_params(("parallel",)),
    )(q, k, v)


def _attn_out_bwd(dy, attn, gain, name):
    S, A = attn.shape
    tr = _row_tile(S, 256)

    def body(dy_ref, a_ref, g_ref, da_ref, dl_ref, dg_ref):
        @pl.when(pl.program_id(0) == 0)
        def _():
            dg_ref[...] = jnp.zeros_like(dg_ref)

        x = a_ref[...]
        dy_ = dy_ref[...]
        r = lax.rsqrt(jnp.mean(x * x, axis=-1, keepdims=True) + RMS_EPS)
        xh = x * r
        dg_ref[...] += jnp.sum(dy_ * xh, axis=0, keepdims=True)
        dxh = dy_ * g_ref[...]
        dx = r * (dxh - xh * jnp.mean(dxh * xh, axis=-1, keepdims=True))
        da_ref[...] = dx
        prod = dx * x
        hi = lax.broadcasted_iota(jnp.int32, (A, A), 0) // HEAD_DIM
        hj = lax.broadcasted_iota(jnp.int32, (A, A), 1) // HEAD_DIM
        same_head = (hi == hj).astype(F32)
        dl_ref[...] = jnp.dot(prod, same_head, preferred_element_type=F32, precision=lax.Precision.HIGHEST)

    row = pl.BlockSpec((tr, A), lambda i: (i, 0))
    vec = _vec_spec(A, 1)
    return pl.pallas_call(
        body, name=name, grid=(S // tr,),
        in_specs=[row, row, vec], out_specs=[row, row, vec],
        out_shape=[jax.ShapeDtypeStruct((S, A), F32), jax.ShapeDtypeStruct((S, A), F32),
                   jax.ShapeDtypeStruct((1, A), F32)],
        compiler_params=_params(("arbitrary",)),
    )(dy, attn, gain)


def _attn_bwd(q, k, v, da, lse, delta, name):
    S, A = q.shape

    def body(q_ref, k_ref, v_ref, da_ref, lse_ref, dl_ref, dq_ref, dk_ref, dv_ref):
        lane = lax.broadcasted_iota(jnp.int32, (1, 128), 1)
        head0 = lane < HEAD_DIM
        dq_ref[...] = jnp.zeros_like(dq_ref)
        dk_ref[...] = jnp.zeros_like(dk_ref)
        dv_ref[...] = jnp.zeros_like(dv_ref)
        for dil in DILATIONS:
            T, nblk = _branch_blocks(S, dil)
            span = T * dil

            def block(idx, carry, dil=dil, T=T, span=span):
                r = idx % dil
                base = (idx // dil) * span + r
                rows = _block_rows(base, T, dil)
                prev = _block_rows(jnp.maximum(base - span, r), T, dil)
                qb, dab = q_ref[rows, :].astype(BF16), da_ref[rows, :].astype(BF16)
                lse_b, dl_b = lse_ref[rows, :], dl_ref[rows, :]
                kc, vc = k_ref[rows, :].astype(BF16), v_ref[rows, :].astype(BF16)
                kp, vp = k_ref[prev, :].astype(BF16), v_ref[prev, :].astype(BF16)
                valid_d, valid_o = _band_masks(T, base >= span)
                dqs = []
                dkc = dkp = dvc = dvp = None
                for h, hmask in enumerate((head0, jnp.logical_not(head0))):
                    qh = jnp.where(hmask, qb, jnp.zeros_like(qb))
                    dah = jnp.where(hmask, dab, jnp.zeros_like(dab))
                    c0 = h * HEAD_DIM
                    lse_h = lse_b[:, c0:c0 + 1]
                    dl_h = dl_b[:, c0:c0 + 1]
                    sd = lax.dot_general(qh, kc, NT, preferred_element_type=F32)
                    so = lax.dot_general(qh, kp, NT, preferred_element_type=F32)
                    pd = jnp.where(valid_d, jnp.exp(sd - lse_h), 0.0)
                    po = jnp.where(valid_o, jnp.exp(so - lse_h), 0.0)
                    dpd = lax.dot_general(dah, vc, NT, preferred_element_type=F32)
                    dpo = lax.dot_general(dah, vp, NT, preferred_element_type=F32)
                    dsd = (pd * (dpd - dl_h)).astype(BF16)
                    dso = (po * (dpo - dl_h)).astype(BF16)
                    pdb, pob = pd.astype(BF16), po.astype(BF16)
                    dqs.append(jnp.dot(dsd, kc, preferred_element_type=F32)
                               + jnp.dot(dso, kp, preferred_element_type=F32))
                    t_kc = lax.dot_general(dsd, qh, TN, preferred_element_type=F32)
                    t_kp = lax.dot_general(dso, qh, TN, preferred_element_type=F32)
                    t_vc = lax.dot_general(pdb, dah, TN, preferred_element_type=F32)
                    t_vp = lax.dot_general(pob, dah, TN, preferred_element_type=F32)
                    if h == 0:
                        dkc, dkp, dvc, dvp = t_kc, t_kp, t_vc, t_vp
                    else:
                        dkc, dkp, dvc, dvp = dkc + t_kc, dkp + t_kp, dvc + t_vc, dvp + t_vp
                dq_ref[rows, :] += jnp.where(head0, dqs[0], dqs[1])
                dk_ref[rows, :] += dkc
                dv_ref[rows, :] += dvc
                dk_ref[prev, :] += dkp
                dv_ref[prev, :] += dvp
                return carry

            lax.fori_loop(0, nblk, block, 0)

    blk = pl.BlockSpec((S, 128), lambda j: (0, j))
    out = jax.ShapeDtypeStruct((S, A), F32)
    return pl.pallas_call(
        body, name=name, grid=(A // 128,),
        in_specs=[blk] * 6, out_specs=[blk] * 3, out_shape=[out] * 3,
        compiler_params=_params(("parallel",)),
    )(q, k, v, da, lse, delta)


def _glu_window(a_ref, b_ref, ah_ref, bh_ref, first):
    u0 = a_ref[...] * _sigmoid(b_ref[...])
    u0h = ah_ref[...] * _sigmoid(bh_ref[...])
    u0h = jnp.where(first, jnp.zeros_like(u0h), u0h)
    return jnp.concatenate([u0h, u0], axis=0)


def _conv_norms(u1, lng, lnb):
    mu = jnp.mean(u1, axis=-1, keepdims=True)
    xc = u1 - mu
    rstd = lax.rsqrt(jnp.mean(xc * xc, axis=-1, keepdims=True) + LN_EPS)
    u1h = xc * rstd
    u2 = u1h * lng + lnb
    sig = _sigmoid(u2)
    u3 = u2 * sig
    r = lax.rsqrt(jnp.mean(u3 * u3, axis=-1, keepdims=True) + RMS_EPS)
    return rstd, u1h, u2, sig, u3, r


def _conv_specs(tr, C, col_a, col_b):
    per = tr // HALO

    def tile(col):
        return pl.BlockSpec((tr, C), lambda i: (i, col))

    def halo(col):
        return pl.BlockSpec((HALO, C), lambda i: (jnp.maximum(i * per - 1, 0), col))

    return tile(col_a), tile(col_b), halo(col_a), halo(col_b)


def _mixer_merge(proj, attn, cw, cb, lng, lnb, gat, gco, name):
    S = proj.shape[0]
    C = CONV_WIDTH
    A = attn.shape[1]
    tr = _row_tile(S, 256)

    def body(a_ref, b_ref, ah_ref, bh_ref, at_ref, w_ref, cb_ref, lng_ref, lnb_ref, gat_ref, gco_ref, u1_ref, y_ref):
        win = _glu_window(a_ref, b_ref, ah_ref, bh_ref, pl.program_id(0) == 0)
        acc = jnp.broadcast_to(cb_ref[...], (tr, C))
        for j in range(CONV_KERNEL):
            off = HALO - (CONV_KERNEL - 1) + j
            acc = acc + w_ref[j:j + 1, :] * win[off:off + tr, :]
        u1_ref[...] = acc
        _, _, _, _, u3, r = _conv_norms(acc, lng_ref[...], lnb_ref[...])
        y_ref[:, A:] = ((u3 * r) * gco_ref[...]).astype(BF16)
        x = at_ref[...]
        ra = lax.rsqrt(jnp.mean(x * x, axis=-1, keepdims=True) + RMS_EPS)
        y_ref[:, :A] = ((x * ra) * gat_ref[...]).astype(BF16)

    ta, tb, ha, hb = _conv_specs(tr, C, 3, 4)
    row = pl.BlockSpec((tr, C), lambda i: (i, 0))
    vec = _vec_spec(C, 1)
    return pl.pallas_call(
        body, name=name, grid=(S // tr,),
        in_specs=[ta, tb, ha, hb, pl.BlockSpec((tr, A), lambda i: (i, 0)), pl.BlockSpec((HALO, C), lambda i: (0, 0)),
                  vec, vec, vec, _vec_spec(A, 1), vec],
        out_specs=[row, pl.BlockSpec((tr, A + C), lambda i: (i, 0))],
        out_shape=[jax.ShapeDtypeStruct((S, C), F32), jax.ShapeDtypeStruct((S, A + C), BF16)],
        compiler_params=_params(("parallel",)),
    )(proj, proj, proj, proj, attn, cw, cb, lng, lnb, gat, gco)


def _conv_bwd_norms(dy, u1, lng, lnb, gco, name):
    S, C = u1.shape
    tr = _row_tile(S, 256)

    def body(dy_ref, u1_ref, lng_ref, lnb_ref, gco_ref, du1_ref, dgco_ref, dlng_ref, dlnb_ref, dcb_ref):
        @pl.when(pl.program_id(0) == 0)
        def _():
            for ref in (dgco_ref, dlng_ref, dlnb_ref, dcb_ref):
                ref[...] = jnp.zeros_like(ref)

        lng = lng_ref[...]
        rstd, u1h, u2, sig, u3, r = _conv_norms(u1_ref[...], lng, lnb_ref[...])
        dy_ = dy_ref[...]
        u3h = u3 * r
        dgco_ref[...] += jnp.sum(dy_ * u3h, axis=0, keepdims=True)
        du3h = dy_ * gco_ref[...]
        du3 = r * (du3h - u3h * jnp.mean(du3h * u3h, axis=-1, keepdims=True))
        du2 = du3 * (sig * (1.0 + u2 * (1.0 - sig)))
        dlng_ref[...] += jnp.sum(du2 * u1h, axis=0, keepdims=True)
        dlnb_ref[...] += jnp.sum(du2, axis=0, keepdims=True)
        du1h = du2 * lng
        du1 = rstd * (du1h - jnp.mean(du1h, axis=-1, keepdims=True)
                      - u1h * jnp.mean(du1h * u1h, axis=-1, keepdims=True))
        du1_ref[...] = du1
        dcb_ref[...] += jnp.sum(du1, axis=0, keepdims=True)

    row = pl.BlockSpec((tr, C), lambda i: (i, 0))
    vec = _vec_spec(C, 1)
    return pl.pallas_call(
        body, name=name, grid=(S // tr,),
        in_specs=[pl.BlockSpec((tr, C), lambda i: (i, 1)), row, vec, vec, vec],
        out_specs=[row, vec, vec, vec, vec],
        out_shape=[jax.ShapeDtypeStruct((S, C), F32)] + [jax.ShapeDtypeStruct((1, C), F32)] * 4,
        compiler_params=_params(("arbitrary",)),
    )(dy, u1, lng, lnb, gco)


def _dproj(du1, proj, cw, dq, dk, dv, cos, sin, name):
    S, C = du1.shape
    A = dq.shape[1]
    tr = _row_tile(S, 256)
    nt = S // tr
    per = tr // HALO
    scale = HEAD_DIM ** -0.5

    def body(du_ref, dun_ref, a_ref, b_ref, ah_ref, bh_ref, w_ref, dq_ref, dk_ref, dv_ref, cos_ref, sin_ref,
             dp_ref, dw_ref):
        i = pl.program_id(0)

        @pl.when(i == 0)
        def _():
            dw_ref[...] = jnp.zeros_like(dw_ref)

        win = _glu_window(a_ref, b_ref, ah_ref, bh_ref, i == 0)
        du = du_ref[...]
        nxt = jnp.where(i == nt - 1, jnp.zeros_like(dun_ref[...]), dun_ref[...])
        dwin = jnp.concatenate([du, nxt], axis=0)
        du0 = jnp.zeros((tr, C), F32)
        taps = []
        for j in range(CONV_KERNEL):
            back = CONV_KERNEL - 1 - j
            du0 = du0 + w_ref[j:j + 1, :] * dwin[back:back + tr, :]
            off = HALO - (CONV_KERNEL - 1) + j
            taps.append(jnp.sum(du * win[off:off + tr, :], axis=0, keepdims=True))
        taps.append(jnp.zeros((HALO - CONV_KERNEL, C), F32))
        dw_ref[...] += jnp.concatenate(taps, axis=0)
        a, sig = a_ref[...], _sigmoid(b_ref[...])
        dp_ref[:, 3 * A:3 * A + C] = (du0 * sig).astype(BF16)
        dp_ref[:, 3 * A + C:] = (du0 * a * sig * (1.0 - sig)).astype(BF16)
        cos_, nsin = cos_ref[...], -sin_ref[...]
        for j in range(A // 128):
            lanes = slice(j * 128, (j + 1) * 128)
            dp_ref[:, j * 128:(j + 1) * 128] = (_rotate(dq_ref[:, lanes], cos_, nsin) * scale).astype(BF16)
            dp_ref[:, A + j * 128:A + (j + 1) * 128] = _rotate(dk_ref[:, lanes], cos_, nsin).astype(BF16)
        dp_ref[:, 2 * A:3 * A] = dv_ref[...].astype(BF16)

    ta, tb, ha, hb = _conv_specs(tr, C, 3, 4)
    row = pl.BlockSpec((tr, C), lambda i: (i, 0))
    arow = pl.BlockSpec((tr, A), lambda i: (i, 0))
    tab = pl.BlockSpec((tr, 128), lambda i: (i, 0))
    nxt = pl.BlockSpec((HALO, C), lambda i: (jnp.minimum((i + 1) * per, S // HALO - 1), 0))
    wspec = pl.BlockSpec((HALO, C), lambda i: (0, 0))
    return pl.pallas_call(
        body, name=name, grid=(nt,),
        in_specs=[row, nxt, ta, tb, ha, hb, wspec, arow, arow, arow, tab, tab],
        out_specs=[pl.BlockSpec((tr, 3 * A + 2 * C), lambda i: (i, 0)), wspec],
        out_shape=[jax.ShapeDtypeStruct((S, 3 * A + 2 * C), BF16), jax.ShapeDtypeStruct((HALO, C), F32)],
        compiler_params=_params(("arbitrary",)),
    )(du1, du1, proj, proj, proj, proj, cw, dq, dk, dv, cos, sin)


def _ada_fwd(c_all, w, b, name):
    B, D = c_all.shape
    N = w.shape[1]
    tn = 768 if N % 768 == 0 else N

    def body(c_ref, w_ref, b_ref, o_ref):
        c = c_ref[...]
        a = (c * _sigmoid(c)).astype(BF16)
        o_ref[...] = jnp.dot(a, w_ref[...].astype(BF16), preferred_element_type=F32) + b_ref[...]

    return pl.pallas_call(
        body, name=name, grid=(N // tn,),
        in_specs=[pl.BlockSpec((B, D), lambda j: (0, 0)), pl.BlockSpec((D, tn), lambda j: (0, j)),
                  pl.BlockSpec((1, tn), lambda j: (0, j))],
        out_specs=pl.BlockSpec((B, tn), lambda j: (0, j)),
        out_shape=jax.ShapeDtypeStruct((B, N), F32),
        compiler_params=_params(("parallel",)),
    )(c_all, w, b)


def _ada_wgrad(c_t, dmod, name):
    D, B = c_t.shape
    N = dmod.shape[1]
    tn = 768 if N % 768 == 0 else N

    def body(c_ref, d_ref, o_ref):
        c = c_ref[...]
        a = (c * _sigmoid(c)).astype(BF16)
        o_ref[...] = jnp.dot(a, d_ref[...].astype(BF16), preferred_element_type=F32)

    return pl.pallas_call(
        body, name=name, grid=(N // tn,),
        in_specs=[pl.BlockSpec((D, B), lambda j: (0, 0)), pl.BlockSpec((B, tn), lambda j: (0, j))],
        out_specs=pl.BlockSpec((D, tn), lambda j: (0, j)),
        out_shape=jax.ShapeDtypeStruct((D, N), F32),
        compiler_params=_params(("parallel",)),
    )(c_t, dmod)


def _sum_rows(x, name):
    R, N = x.shape

    def body(x_ref, o_ref):
        acc = x_ref[0:1, :]
        for r in range(1, R):
            acc = acc + x_ref[r:r + 1, :]
        o_ref[...] = acc

    return pl.pallas_call(
        body, name=name, out_shape=jax.ShapeDtypeStruct((1, N), F32),
        compiler_params=_params(),
    )(x)


def _adamw(w, g, m, v, name, deps=()):
    R, C = w.shape
    tr = _row_tile(R, 256) if R % 256 == 0 else R
    bc1 = 1.0 - ADAM_B1 ** ADAM_STEP
    bc2 = 1.0 - ADAM_B2 ** ADAM_STEP
    deps = _deps(deps)

    def body(w_ref, g_ref, m_ref, v_ref, d_ref, mo_ref, vo_ref):
        g_ = g_ref[...]
        m_ = ADAM_B1 * m_ref[...] + (1.0 - ADAM_B1) * g_
        v_ = ADAM_B2 * v_ref[...] + (1.0 - ADAM_B2) * (g_ * g_)
        mo_ref[...] = m_
        vo_ref[...] = v_
        d_ref[...] = -ADAM_LR * ((m_ / bc1) / (jnp.sqrt(v_ / bc2) + ADAM_EPS) + ADAM_WD * w_ref[...])

    row = pl.BlockSpec((tr, C), lambda i: (i, 0))
    out = jax.ShapeDtypeStruct((R, C), F32)
    return pl.pallas_call(
        _blind_to(body, 4, len(deps)), name=name, grid=(R // tr,),
        in_specs=[row] * 4 + [ANY_SPEC] * len(deps), out_specs=[row] * 3, out_shape=[out] * 3,
        compiler_params=_params(("parallel",)),
    )(w, g, m, v, *deps)


def _coords():
    return lax.axis_index("x"), lax.axis_index("y"), lax.axis_index("c")


def _all_gather8(x, name, deps=()):
    R, N = x.shape
    assert R == 8
    flips = [(fx, fy, fc) for fx in (0, 1) for fy in (0, 1) for fc in (0, 1)][1:]
    deps = _deps(deps)

    def body(x_ref, o_ref, send_sems, recv_sems):
        mx, my, mc = _coords()
        me = 4 * mx + 2 * my + mc

        def rows(dev):
            return o_ref.at[pl.ds(pl.multiple_of(dev * R, R), R), :]

        o_ref[pl.ds(pl.multiple_of(me * R, R), R), :] = x_ref[...]
        copies = []
        for t, (fx, fy, fc) in enumerate(flips):
            peer = (mx ^ fx, my ^ fy, mc ^ fc)
            copies.append(pltpu.make_async_remote_copy(
                src_ref=x_ref, dst_ref=rows(me), send_sem=send_sems.at[t], recv_sem=recv_sems.at[t],
                device_id=peer, device_id_type=MESH))
        for cp in copies:
            cp.start()
        for t, (fx, fy, fc) in enumerate(flips):
            peer_id = 4 * (mx ^ fx) + 2 * (my ^ fy) + (mc ^ fc)
            pltpu.make_async_remote_copy(
                src_ref=x_ref, dst_ref=rows(peer_id), send_sem=send_sems.at[t], recv_sem=recv_sems.at[t],
                device_id=(mx ^ fx, my ^ fy, mc ^ fc), device_id_type=MESH).wait_recv()
        for cp in copies:
            cp.wait_send()

    return pl.pallas_call(
        _blind_to(body, 1, len(deps)), name=name,
        in_specs=[pl.BlockSpec(memory_space=pltpu.VMEM)] + [ANY_SPEC] * len(deps),
        out_specs=pl.BlockSpec(memory_space=pltpu.VMEM),
        out_shape=jax.ShapeDtypeStruct((N_DEV * R, N), F32),
        scratch_shapes=[pltpu.SemaphoreType.DMA((7,)), pltpu.SemaphoreType.DMA((7,))],
        compiler_params=pltpu.CompilerParams(has_side_effects=True, vmem_limit_bytes=VMEM_LIMIT_BYTES),
    )(x, *deps)


def _half_rows(rows, half):
    return pl.ds(pl.multiple_of(half * (rows // 2), 8), rows // 2)


def _split_start(bufs, plan, n, name):
    nb = len(bufs)

    def body(*refs):
        send_sems, recv_sems, token = refs[nb], refs[nb + 1], refs[-1]
        for t, (src, dst, dev) in enumerate(plan(refs[:nb])):
            pltpu.make_async_remote_copy(src_ref=src, dst_ref=dst, send_sem=send_sems.at[t],
                                         recv_sem=recv_sems.at[t], device_id=dev, device_id_type=MESH).start()
        token[...] = jnp.zeros_like(token)

    out = pl.pallas_call(
        body, name=name,
        out_shape=(pltpu.SemaphoreType.DMA((n,)), pltpu.SemaphoreType.DMA((n,)),
                   *[pltpu.HBM(b.shape, b.dtype) for b in bufs], jax.ShapeDtypeStruct((8, 128), F32)),
        in_specs=[HBM_SPEC] * nb,
        out_specs=(SEM_SPEC, SEM_SPEC, *[HBM_SPEC] * nb, pl.BlockSpec(memory_space=pltpu.VMEM)),
        input_output_aliases={i: 2 + i for i in range(nb)},
        compiler_params=pltpu.CompilerParams(has_side_effects=DATAFLOW),
    )(*[pltpu.with_memory_space_constraint(b, pltpu.HBM) for b in bufs])
    return out[0], out[1], list(out[2:2 + nb]), out[-1]


def _split_wait(bufs, send_sems, recv_sems, plan, after, name):
    nb = len(bufs)
    after = _deps(after)

    def body(*refs):
        ss, rs = refs[nb], refs[nb + 1]
        for t, (src, dst, dev) in enumerate(plan(refs[:nb])):
            cp = pltpu.make_async_remote_copy(src_ref=src, dst_ref=dst, send_sem=ss.at[t], recv_sem=rs.at[t],
                                              device_id=dev, device_id_type=MESH)
            cp.wait_send()
            cp.wait_recv()

    out = pl.pallas_call(
        body, name=name,
        out_shape=tuple(pltpu.HBM(b.shape, b.dtype) for b in bufs),
        in_specs=[HBM_SPEC] * nb + [SEM_SPEC, SEM_SPEC] + [ANY_SPEC] * len(after),
        out_specs=tuple([HBM_SPEC] * nb),
        input_output_aliases={i: i for i in range(nb)},
        compiler_params=pltpu.CompilerParams(has_side_effects=DATAFLOW),
    )(*bufs, send_sems, recv_sems, *after)
    return list(out)


class _Exchange:
    def __init__(self, bufs, plan, n, name):
        self.plan, self.name = plan, name
        self.send_sems, self.recv_sems, self.bufs, self.token = _split_start(bufs, plan, n, name + "_start")

    def wait(self, after):
        return _split_wait(self.bufs, self.send_sems, self.recv_sems, self.plan, after, self.name + "_wait")


def _cast_place(w, chip_idx, name, deps=()):
    R, C = w.shape
    tr = _row_tile(R, 256) if R % 256 == 0 else R
    deps = _deps(deps)

    def body(k_ref, w_ref, o_ref):
        o_ref[...] = w_ref[...].astype(BF16)

    grid_spec = pltpu.PrefetchScalarGridSpec(
        num_scalar_prefetch=1, grid=(R // tr,),
        in_specs=[pl.BlockSpec((tr, C), lambda i, k: (i, 0))] + [ANY_SPEC] * len(deps),
        out_specs=pl.BlockSpec((None, tr, C), lambda i, k: (k[0], i, 0)),
    )
    return pl.pallas_call(
        _blind_to(body, 2, len(deps)), name=name, grid_spec=grid_spec,
        out_shape=jax.ShapeDtypeStruct((N_CHIPS, R, C), BF16),
        compiler_params=_params(("parallel",)),
    )(chip_idx, w, *deps)


def _plan_gather(refs):
    mx, my, mc = _coords()
    me = 2 * mx + my
    plan = []
    for g in refs:
        mine = g.at[me, _half_rows(g.shape[1], mc), :]
        for fx, fy in XY_FLIPS:
            plan.append((mine, mine, (mx ^ fx, my ^ fy, mc)))
    return plan


def _plan_swap(refs):
    mx, my, mc = _coords()
    plan = []
    for g in refs:
        for fx, fy in XY_FLIPS:
            have = g.at[2 * (mx ^ fx) + (my ^ fy), _half_rows(g.shape[1], mc), :]
            plan.append((have, have, (mx, my, 1 - mc)))
    return plan


def _plan_other_halves(refs):
    n = len(refs) // 2
    mx, my, mc = _coords()
    return [(g.at[pl.ds(0, N_CHIPS), _half_rows(g.shape[1], 1 - mc), :], land, (mx, my, 1 - mc))
            for g, land in zip(refs[:n], refs[n:])]


def _plan_chunks(refs):
    n = len(refs) // 2
    mx, my, mc = _coords()
    plan = []
    for s, land in zip(refs[:n], refs[n:]):
        for t, (fx, fy) in enumerate(XY_FLIPS):
            plan.append((s.at[2 * (mx ^ fx) + (my ^ fy)], land.at[t], (mx ^ fx, my ^ fy, mc)))
    return plan


def _plan_share(refs):
    mx, my, mc = _coords()
    return [(full.at[mc], full.at[mc], (mx, my, 1 - mc)) for full in refs]


def _add_half(g, recv, core_idx, name):
    nk, R, C = g.shape
    rh = R // 2
    tr = _row_tile(rh, 128) if rh % 128 == 0 else rh
    nt = rh // tr

    def body(c_ref, g_ref, r_ref, o_ref):
        o_ref[...] = g_ref[...] + r_ref[...]

    grid_spec = pltpu.PrefetchScalarGridSpec(
        num_scalar_prefetch=1, grid=(nk, nt),
        in_specs=[pl.BlockSpec((None, tr, C), lambda k, i, c: (k, c[0] * nt + i, 0)),
                  pl.BlockSpec((None, tr, C), lambda k, i, c: (k, i, 0))],
        out_specs=pl.BlockSpec((None, tr, C), lambda k, i, c: (k, i, 0)),
    )
    return pl.pallas_call(
        body, name=name, grid_spec=grid_spec, out_shape=jax.ShapeDtypeStruct((nk, rh, C), F32),
        compiler_params=_params(("parallel", "parallel")),
    )(core_idx, g, recv)


def _sum_chips(s, land, chip_core, name):
    _, rh, C = s.shape
    tr = _row_tile(rh, 128) if rh % 128 == 0 else rh

    def body(p_ref, s_ref, l_ref, o_ref):
        me = p_ref[0]
        acc = None
        for j in range(N_CHIPS):
            t = jnp.maximum(jnp.bitwise_xor(me, j) - 1, 0)
            term = jnp.where(me == j, s_ref[...], l_ref[t])
            acc = term if acc is None else acc + term
        o_ref[...] = acc

    grid_spec = pltpu.PrefetchScalarGridSpec(
        num_scalar_prefetch=1, grid=(rh // tr,),
        in_specs=[pl.BlockSpec((None, tr, C), lambda i, p: (p[0], i, 0)),
                  pl.BlockSpec((3, tr, C), lambda i, p: (0, i, 0))],
        out_specs=pl.BlockSpec((None, tr, C), lambda i, p: (p[1], i, 0)),
    )
    return pl.pallas_call(
        body, name=name, grid_spec=grid_spec, out_shape=jax.ShapeDtypeStruct((2, rh, C), F32),
        compiler_params=_params(("parallel",)),
    )(chip_core, s, land)


def _rs_send_halves(grads, tag):
    lands = [lax.empty((g.shape[0], g.shape[1] // 2, g.shape[2]), g.dtype) for g in grads]
    return _Exchange(list(grads) + lands, _plan_other_halves, len(grads), f"rs_halves_{tag}")


def _rs_send_chunks(ex, after, core_idx, tag):
    bufs = ex.wait(after)
    n = len(bufs) // 2
    sums = [_add_half(g, r, core_idx, f"rs_add_{tag}_{i}") for i, (g, r) in enumerate(zip(bufs[:n], bufs[n:]))]
    lands = [lax.empty((3,) + s.shape[1:], s.dtype) for s in sums]
    return _Exchange(sums + lands, _plan_chunks, 3 * n, f"rs_chunks_{tag}")


def _rs_send_share(ex, after, chip_core, tag):
    bufs = ex.wait(after)
    n = len(bufs) // 2
    fulls = [_sum_chips(s, l, chip_core, f"rs_sum_{tag}_{i}") for i, (s, l) in enumerate(zip(bufs[:n], bufs[n:]))]
    return _Exchange(fulls, _plan_share, n, f"rs_share_{tag}")


def _rs_finish(ex, after):
    return [b.reshape(2 * b.shape[1], b.shape[2]) for b in ex.wait(after)]


def _ffn_forward(h, gain, sc, sh, gate, wg, wu, wd, tag):
    n = _norm_mod(h, gain, sc, sh, f"{tag}_norm")
    ga, up, act = _ffn_gate_up(n, wg, wu, f"{tag}_gate_up")
    h_out, f = _mm_residual(act, wd, h, gate, 0.5, f"{tag}_down")
    return h_out, (h, n, ga, up, act, f)


def _ffn_backward(dh_out, saved, gain, sc, gate, wg, wu, wd, core_idx, tag, last=False):
    h, n, ga, up, act, f = saved
    df, d_gate = _gate_bwd(dh_out, f, gate, 0.5, f"{tag}_gate_bwd")
    dwd = _wgrad_chunk_lhs(act, df, f"{tag}_dwd")
    ex_d = _rs_send_halves([dwd], f"{tag}_d")
    dga, dup = _ffn_dact(df, wd, ga, up, f"{tag}_dact", deps=[ex_d.token])
    ex_d = _rs_send_chunks(ex_d, [dga], core_idx, f"{tag}_d")
    dwg, dwu = _wgrad_chunk_rhs(n, [dga, dup], wg.shape[0], True, f"{tag}_dwgu", deps=[ex_d.token])
    ex_gu = _rs_send_halves([dwg, dwu], f"{tag}_gu")
    dn = _mm_nt_reduce([dga, dup], [wg, wu], True, f"{tag}_dn", deps=[ex_gu.token])
    if not last:
        ex_gu = _rs_send_chunks(ex_gu, [dn], core_idx, f"{tag}_gu")
    dh_in, d_sh, d_sc, d_gain = _norm_mod_bwd(dn, h, gain, sc, dh_out, f"{tag}_norm_bwd", deps=[ex_gu.token])
    return dh_in, (ex_d, ex_gu), (d_sh, d_sc, d_gate, d_gain)


def _pad_cols(v, n):
    return jnp.pad(v, ((0, 0), (0, n - v.shape[1])))


def _mixer_forward(h1, mix_norm_g, sc2, sh2, gt2, win, wout, conv_w, conv_dw_b, conv_ln_g, conv_ln_b, attn_out_g,
                   conv_out_g):
    S, D = h1.shape
    n2 = _norm_mod(h1, mix_norm_g, sc2, sh2, "mix_norm")
    proj = _mm_cols(n2, win, "mix_in")
    cos, sin = _rope_tables(S)
    q, k, v = _qkv_rope(proj, cos, sin, "qkv_rope")
    attn, lse = _attn_fwd(q, k, v, "attn_fwd")
    u1, y = _mixer_merge(proj, attn, conv_w, conv_dw_b, conv_ln_g, conv_ln_b, attn_out_g, conv_out_g, "mix_merge")
    h2, mo = _mm_residual(y[None], wout.reshape(1, D, D), h1, gt2, 1.0, "mix_out")
    return h2, (h1, n2, proj, cos, sin, q, k, v, attn, lse, u1, y, mo)


def _mixer_backward(dh2, saved, mix_norm_g, sc2, gt2, win, wout, conv_w, conv_ln_g, conv_ln_b, attn_out_g,
                    conv_out_g, core_idx):
    h1, n2, proj, cos, sin, q, k, v, attn, lse, u1, y, mo = saved
    S, D = h1.shape
    dmo, d_gt2 = _gate_bwd(dh2, mo, gt2, 1.0, "mix_gate_bwd")
    dwout = _wgrad_chunk_lhs(y[None], dmo, "mix_dwout")
    dy = _mm_nt(dmo, wout.reshape(D, D), "mix_dy")
    dattn, delta, d_attn_g = _attn_out_bwd(dy, attn, attn_out_g, "attn_out_bwd")
    dq, dk, dv = _attn_bwd(q, k, v, dattn, lse, delta, "attn_bwd")
    du1, d_gco, d_lng, d_lnb, d_cb = _conv_bwd_norms(dy, u1, conv_ln_g, conv_ln_b, conv_out_g, "conv_bwd_norms")
    dproj, d_cw = _dproj(du1, proj, conv_w, dq, dk, dv, cos, sin, "mix_dproj")
    (dwin,) = _wgrad_chunk_rhs(n2, [dproj], N_CHIPS, False, "mix_dwin")
    ex = _rs_send_halves([dwin, dwout.reshape(N_CHIPS, D // N_CHIPS, D)], "mix")
    dn2 = _mm_nt_reduce([dproj], [win], False, "mix_dn", deps=[ex.token])
    ex = _rs_send_chunks(ex, [dn2], core_idx, "mix")
    dh1, d_sh2, d_sc2, d_gain2 = _norm_mod_bwd(dn2, h1, mix_norm_g, sc2, dh2, "mix_norm_bwd", deps=[ex.token])
    small = (d_sh2, d_sc2, d_gt2, d_gain2, d_cb, d_lng, d_lnb, d_attn_g, d_gco, d_cw)
    return dh1, ex, small


def kernel(x, c, w_ada, b_ada, ffn1_norm_g, ffn1_w_gate, ffn1_w_up, ffn1_w_down, mix_norm_g, w_in, conv_dw_w, conv_dw_b, conv_ln_g, conv_ln_b, attn_out_g, conv_out_g, w_out, ffn2_norm_g, ffn2_w_gate, ffn2_w_up, ffn2_w_down, final_norm_g, loss_target, m_w_ada, m_b_ada, m_ffn1_norm_g, m_ffn1_w_gate, m_ffn1_w_up, m_ffn1_w_down, m_mix_norm_g, m_w_in, m_conv_dw_w, m_conv_dw_b, m_conv_ln_g, m_conv_ln_b, m_attn_out_g, m_conv_out_g, m_w_out, m_ffn2_norm_g, m_ffn2_w_gate, m_ffn2_w_up, m_ffn2_w_down, m_final_norm_g, v_w_ada, v_b_ada, v_ffn1_norm_g, v_ffn1_w_gate, v_ffn1_w_up, v_ffn1_w_down, v_mix_norm_g, v_w_in, v_conv_dw_w, v_conv_dw_b, v_conv_ln_g, v_conv_ln_b, v_attn_out_g, v_conv_out_g, v_w_out, v_ffn2_norm_g, v_ffn2_w_gate, v_ffn2_w_up, v_ffn2_w_down, v_final_norm_g):
    S, D = x.shape[1], x.shape[2]
    mx, my, mc = _coords()
    chip = 2 * mx + my
    dev = 4 * mx + 2 * my + mc
    chip_idx = chip.astype(jnp.int32).reshape(1)
    core_idx = mc.astype(jnp.int32).reshape(1)
    chip_core = jnp.stack([chip, mc]).astype(jnp.int32)
    h0 = x[0]
    target = loss_target[0]

    ncw = CONV_KERNEL * 128
    n0 = -(-(D + ncw) // 1024) * 1024
    pk0 = _pad_cols(jnp.concatenate([c.reshape(1, D), conv_dw_w.reshape(1, ncw)], axis=1), n0)
    g0 = _all_gather8(pk0.reshape(8, n0 // 8), "gather_c").reshape(N_DEV, n0)
    c_all = g0[:, :D]
    conv_w = jnp.concatenate([g0[2 * kc, D:D + ncw].reshape(CONV_KERNEL, 128) for kc in range(N_CHIPS)], axis=1)
    conv_w = jnp.pad(conv_w, ((0, HALO - CONV_KERNEL), (0, 0)))
    nmod = w_ada.shape[2]
    b_shard = lax.dynamic_slice(b_ada, (0, chip * nmod), (1, nmod))
    mod_part = _ada_fwd(c_all, w_ada[0], b_shard, "ada_fwd")
    g1 = _all_gather8(mod_part, "gather_mod")
    mod_all = jnp.concatenate([g1[16 * kc:16 * kc + 8] for kc in range(N_CHIPS)], axis=1)
    mod = lax.dynamic_slice(mod_all, (dev, 0), (1, 9 * D))
    sh1, sc1, gt1, sh2, sc2, gt2, sh3, sc3, gt3 = [mod[:, i * D:(i + 1) * D] for i in range(9)]

    def gather_start(ws, tag, dep):
        slots = [_cast_place(w[0], chip_idx, f"cast_{tag}_{i}", deps=[dep]) for i, w in enumerate(ws)]
        return _Exchange(slots, _plan_gather, 3 * len(ws), f"gather_{tag}")

    def swap_start(ex, after, tag):
        return _Exchange(ex.wait(after), _plan_swap, 3 * len(ex.bufs), f"swap_{tag}")

    ex_w1 = gather_start([ffn1_w_gate, ffn1_w_up, ffn1_w_down], "ffn1", g1)
    ex_wm = gather_start([w_in, w_out], "mix", ex_w1.token)
    ex_w2 = gather_start([ffn2_w_gate, ffn2_w_up, ffn2_w_down], "ffn2", ex_wm.token)

    n1 = _norm_mod(h0, ffn1_norm_g, sc1, sh1, "ffn1_norm")
    wg1, wu1, wd1 = swap_start(ex_w1, [n1, ex_w2.token], "ffn1").wait([])
    ga1, up1, act1 = _ffn_gate_up(n1, wg1, wu1, "ffn1_gate_up")
    ex_wm = swap_start(ex_wm, [act1], "mix")
    h1, f1 = _mm_residual(act1, wd1, h0, gt1, 0.5, "ffn1_down")
    saved1 = (h0, n1, ga1, up1, act1, f1)
    win, wout = ex_wm.wait([h1])
    ex_w2 = swap_start(ex_w2, [h1], "ffn2")
    h2, saved2 = _mixer_forward(h1, mix_norm_g, sc2, sh2, gt2, win, wout, conv_w, conv_dw_b, conv_ln_g, conv_ln_b,
                                attn_out_g, conv_out_g)
    wg2, wu2, wd2 = ex_w2.wait([h2])
    h3, saved3 = _ffn_forward(h2, ffn2_norm_g, sc3, sh3, gt3, wg2, wu2, wd2, "ffn2")
    loss_part, dh3, d_final_g = _loss_head(h3, final_norm_g.reshape(1, D), target, "loss_head")

    dh2, (ex_d2, ex_gu2), (d_sh3, d_sc3, d_gt3, d_gain3) = _ffn_backward(
        dh3, saved3, ffn2_norm_g, sc3, gt3, wg2, wu2, wd2, core_idx, "ffn2")
    dh1, ex_mix, small_mix = _mixer_backward(
        dh2, saved2, mix_norm_g, sc2, gt2, win, wout, conv_w, conv_ln_g, conv_ln_b, attn_out_g, conv_out_g, core_idx)
    d_sh2, d_sc2, d_gt2, d_gain2, d_cb, d_lng, d_lnb, d_attn_g, d_gco, d_cw = small_mix
    dh0, (ex_d1, ex_gu1), (d_sh1, d_sc1, d_gt1, d_gain1) = _ffn_backward(
        dh1, saved1, ffn1_norm_g, sc1, gt1, wg1, wu1, wd1, core_idx, "ffn1", last=True)

    dmod = jnp.concatenate([d_sh1, d_sc1, d_gt1, d_sh2, d_sc2, d_gt2, d_sh3, d_sc3, d_gt3], axis=1)
    small = [d_gain1, d_gain2, d_gain3, d_final_g, d_cb, d_lng, d_lnb, d_attn_g, d_gco,
             d_cw[:CONV_KERNEL].reshape(1, CONV_KERNEL * CONV_WIDTH), loss_part]
    pk1 = jnp.concatenate([dmod] + small, axis=1)
    n1_ = -(-pk1.shape[1] // 1024) * 1024
    gathered = _all_gather8(_pad_cols(pk1, n1_).reshape(8, n1_ // 8), "gather_small").reshape(N_DEV, n1_)
    ex_gu1 = _rs_send_chunks(ex_gu1, [gathered], core_idx, "ffn1_gu")
    tot = _sum_rows(gathered, "sum_small")
    off = [0]

    def take(nel):
        out = tot[:, off[0]:off[0] + nel]
        off[0] += nel
        return out

    g_b_ada = take(9 * D)
    g_ffn1_norm, g_mix_norm, g_ffn2_norm, g_final = take(D), take(D), take(D), take(D)
    g_cb, g_lng, g_lnb, g_attn_g, g_gco = take(512), take(512), take(512), take(512), take(512)
    g_cw_full = take(CONV_KERNEL * CONV_WIDTH).reshape(CONV_KERNEL, CONV_WIDTH)
    loss = take(128)[0, 0]
    g_cw = lax.dynamic_slice(g_cw_full, (0, chip * 128), (CONV_KERNEL, 128))

    dmod_shard = lax.dynamic_slice(gathered[:, :9 * D], (0, chip * nmod), (N_DEV, nmod))
    dmod16 = jnp.pad(dmod_shard, ((0, N_DEV), (0, 0)))
    c_t16 = jnp.pad(c_all.T, ((0, 0), (0, N_DEV)))
    g_w_ada = _ada_wgrad(c_t16, dmod16, "ada_wgrad")

    names = ["w_ada", "b_ada", "ffn1_norm_g", "ffn1_w_gate", "ffn1_w_up", "ffn1_w_down", "mix_norm_g", "w_in",
             "conv_dw_w", "conv_dw_b", "conv_ln_g", "conv_ln_b", "attn_out_g", "conv_out_g", "w_out", "ffn2_norm_g",
             "ffn2_w_gate", "ffn2_w_up", "ffn2_w_down", "final_norm_g"]
    weights = dict(zip(names, [w_ada, b_ada, ffn1_norm_g, ffn1_w_gate, ffn1_w_up, ffn1_w_down, mix_norm_g, w_in,
                               conv_dw_w, conv_dw_b, conv_ln_g, conv_ln_b, attn_out_g, conv_out_g, w_out,
                               ffn2_norm_g, ffn2_w_gate, ffn2_w_up, ffn2_w_down, final_norm_g]))
    ms = dict(zip(names, [m_w_ada, m_b_ada, m_ffn1_norm_g, m_ffn1_w_gate, m_ffn1_w_up, m_ffn1_w_down, m_mix_norm_g,
                          m_w_in, m_conv_dw_w, m_conv_dw_b, m_conv_ln_g, m_conv_ln_b, m_attn_out_g, m_conv_out_g,
                          m_w_out, m_ffn2_norm_g, m_ffn2_w_gate, m_ffn2_w_up, m_ffn2_w_down, m_final_norm_g]))
    vs = dict(zip(names, [v_w_ada, v_b_ada, v_ffn1_norm_g, v_ffn1_w_gate, v_ffn1_w_up, v_ffn1_w_down, v_mix_norm_g,
                          v_w_in, v_conv_dw_w, v_conv_dw_b, v_conv_ln_g, v_conv_ln_b, v_attn_out_g, v_conv_out_g,
                          v_w_out, v_ffn2_norm_g, v_ffn2_w_gate, v_ffn2_w_up, v_ffn2_w_down, v_final_norm_g]))
    grads, deltas, new_ms, new_vs = {}, {}, {}, {}

    def adamw_big(nm, g2d, deps=()):
        shape = weights[nm].shape
        two_d = (shape[-2], shape[-1])
        d_, m_, v_ = _adamw(weights[nm].reshape(two_d), g2d.reshape(two_d), ms[nm].reshape(two_d),
                            vs[nm].reshape(two_d), f"adamw_{nm}", deps=deps)
        grads[nm], deltas[nm], new_ms[nm], new_vs[nm] = (t.reshape(shape) for t in (g2d, d_, m_, v_))
        return d_

    d_ada = adamw_big("w_ada", g_w_ada, deps=[ex_gu1.token])
    small_grads = {"b_ada": g_b_ada, "ffn1_norm_g": g_ffn1_norm, "mix_norm_g": g_mix_norm, "conv_dw_w": g_cw,
                   "conv_dw_b": g_cb, "conv_ln_g": g_lng, "conv_ln_b": g_lnb, "attn_out_g": g_attn_g,
                   "conv_out_g": g_gco, "ffn2_norm_g": g_ffn2_norm, "final_norm_g": g_final}
    small_names = [nm for nm in names if nm in small_grads]

    def pack_small(arrs):
        flat = jnp.concatenate([arrs[nm].reshape(1, -1) for nm in small_names], axis=1)
        npad = -(-flat.shape[1] // 1024) * 1024
        return _pad_cols(flat, npad).reshape(8, npad // 8)

    d_s, m_s, v_s = _adamw(pack_small(weights), pack_small(small_grads), pack_small(ms), pack_small(vs),
                           "adamw_small")
    pos = 0
    for nm in small_names:
        shape, nel = weights[nm].shape, weights[nm].size
        grads[nm] = small_grads[nm].reshape(shape)
        deltas[nm], new_ms[nm], new_vs[nm] = (t.reshape(1, -1)[:, pos:pos + nel].reshape(shape)
                                              for t in (d_s, m_s, v_s))
        pos += nel

    ex_d2 = _rs_send_share(ex_d2, [d_ada, d_s], chip_core, "ffn2_d")
    ex_gu2 = _rs_send_share(ex_gu2, [ex_d2.token], chip_core, "ffn2_gu")
    ex_mix = _rs_send_share(ex_mix, [ex_gu2.token], chip_core, "mix")
    ex_d1 = _rs_send_share(ex_d1, [ex_mix.token], chip_core, "ffn1_d")
    (g_wd2,) = _rs_finish(ex_d2, [ex_d1.token])
    last = [adamw_big("ffn2_w_down", g_wd2)]
    g_wg2, g_wu2 = _rs_finish(ex_gu2, last)
    last = [adamw_big("ffn2_w_gate", g_wg2), adamw_big("ffn2_w_up", g_wu2)]
    g_win, g_wout = _rs_finish(ex_mix, last)
    last = [adamw_big("w_in", g_win), adamw_big("w_out", g_wout)]
    (g_wd1,) = _rs_finish(ex_d1, last)
    last = [adamw_big("ffn1_w_down", g_wd1)]
    ex_gu1 = _rs_send_share(ex_gu1, last, chip_core, "ffn1_gu")
    g_wg1, g_wu1 = _rs_finish(ex_gu1, [])
    adamw_big("ffn1_w_gate", g_wg1)
    adamw_big("ffn1_w_up", g_wu1)

    return (loss, dh0[None], *[grads[nm] for nm in names], *[deltas[nm] for nm in names],
            *[new_ms[nm] for nm in names], *[new_vs[nm] for nm in names])
```

```python
import jax
import jax.numpy as jnp
import numpy as np
from jax import lax
from jax.experimental import pallas as pl
from jax.experimental.pallas import tpu as pltpu

F32 = jnp.float32
BF16 = jnp.bfloat16
MESH = pl.DeviceIdType.MESH

RMS_EPS = 1e-6
LN_EPS = 1e-5
HEAD_DIM = 64
ATTN_WIDTH = 512
CONV_WIDTH = 512
ATTN_BLOCK = 128
DILATIONS = (1, 4, 16)
SUPER_ROWS = ATTN_BLOCK * 16
ROPE_THETA = 10000.0
CONV_KERNEL = 31
HALO = 32
N_CHIPS = 4
N_DEV = 8
ADAM_LR, ADAM_B1, ADAM_B2, ADAM_EPS, ADAM_WD, ADAM_STEP = 0.001, 0.9, 0.999, 1e-08, 0.01, 10
VMEM_LIMIT_BYTES = 48 * 1024 * 1024
NEG = -1e30

NT = (((1,), (1,)), ((), ()))
TN = (((0,), (0,)), ((), ()))

ANY_SPEC = pl.BlockSpec(memory_space=pl.ANY)
HBM_SPEC = pl.BlockSpec(memory_space=pltpu.HBM)
SEM_SPEC = pl.BlockSpec(memory_space=pltpu.SEMAPHORE)
DATAFLOW = pltpu.SideEffectType.DATAFLOW_SIDE_EFFECTING
XY_FLIPS = ((0, 1), (1, 0), (1, 1))


def _params(sem=None):
    return pltpu.CompilerParams(dimension_semantics=sem, vmem_limit_bytes=VMEM_LIMIT_BYTES)


def _row_tile(rows, want):
    t = min(rows, want)
    assert rows % t == 0
    return t


def _sigmoid(x):
    return 1.0 / (1.0 + jnp.exp(-x))


def _deps(deps):
    return [d for d in deps if d is not None]


def _blind_to(body, n_in, n_dep):
    def wrapped(*refs):
        return body(*refs[:n_in], *refs[n_in + n_dep:])
    return wrapped


def _vec_spec(d, ngrid):
    if ngrid == 1:
        return pl.BlockSpec((1, d), lambda i: (0, 0))
    return pl.BlockSpec((1, d), lambda i, j: (0, 0))


def _norm_mod(h, gain, sc, sh, name):
    S, D = h.shape
    tr = _row_tile(S, 512)

    def body(h_ref, g_ref, sc_ref, sh_ref, n_ref):
        x = h_ref[...]
        r = lax.rsqrt(jnp.mean(x * x, axis=-1, keepdims=True) + RMS_EPS)
        y = (x * r) * g_ref[...]
        n_ref[...] = (y * (1.0 + sc_ref[...]) + sh_ref[...]).astype(BF16)

    row = pl.BlockSpec((tr, D), lambda i: (i, 0))
    return pl.pallas_call(
        body, name=name, grid=(S // tr,),
        in_specs=[row, _vec_spec(D, 1), _vec_spec(D, 1), _vec_spec(D, 1)],
        out_specs=row, out_shape=jax.ShapeDtypeStruct((S, D), BF16),
        compiler_params=_params(("parallel",)),
    )(h, gain, sc, sh)


def _norm_mod_bwd(dn, h_in, gain, sc, dh_out, name, deps=()):
    S, D = h_in.shape
    tr = _row_tile(S, 512)
    deps = _deps(deps)

    def body(dn_ref, h_ref, g_ref, sc_ref, dho_ref, dh_ref, dsh_ref, dsc_ref, dg_ref):
        @pl.when(pl.program_id(0) == 0)
        def _():
            dsh_ref[...] = jnp.zeros_like(dsh_ref)
            dsc_ref[...] = jnp.zeros_like(dsc_ref)
            dg_ref[...] = jnp.zeros_like(dg_ref)

        x = h_ref[...]
        dn_ = dn_ref[...]
        g = g_ref[...]
        one_sc = 1.0 + sc_ref[...]
        r = lax.rsqrt(jnp.mean(x * x, axis=-1, keepdims=True) + RMS_EPS)
        xh = x * r
        dsh_ref[...] += jnp.sum(dn_, axis=0, keepdims=True)
        dsc_ref[...] += jnp.sum(dn_ * (xh * g), axis=0, keepdims=True)
        dg_ref[...] += jnp.sum(dn_ * one_sc * xh, axis=0, keepdims=True)
        dxh = dn_ * (g * one_sc)
        dh_ref[...] = dho_ref[...] + r * (dxh - xh * jnp.mean(dxh * xh, axis=-1, keepdims=True))

    row = pl.BlockSpec((tr, D), lambda i: (i, 0))
    vec = _vec_spec(D, 1)
    return pl.pallas_call(
        _blind_to(body, 5, len(deps)), name=name, grid=(S // tr,),
        in_specs=[row, row, vec, vec, row] + [ANY_SPEC] * len(deps),
        out_specs=[row, vec, vec, vec],
        out_shape=[jax.ShapeDtypeStruct((S, D), F32)] + [jax.ShapeDtypeStruct((1, D), F32)] * 3,
        compiler_params=_params(("arbitrary",)),
    )(dn, h_in, gain, sc, dh_out, *deps)


def _gate_bwd(dh, f, gvec, coef, name):
    S, D = dh.shape
    tr = _row_tile(S, 512)

    def body(dh_ref, f_ref, g_ref, df_ref, dg_ref):
        @pl.when(pl.program_id(0) == 0)
        def _():
            dg_ref[...] = jnp.zeros_like(dg_ref)

        dh_ = dh_ref[...]
        df_ref[...] = ((coef * g_ref[...]) * dh_).astype(BF16)
        dg_ref[...] += jnp.sum(coef * dh_ * f_ref[...].astype(F32), axis=0, keepdims=True)

    row = pl.BlockSpec((tr, D), lambda i: (i, 0))
    vec = _vec_spec(D, 1)
    return pl.pallas_call(
        body, name=name, grid=(S // tr,),
        in_specs=[row, row, vec], out_specs=[row, vec],
        out_shape=[jax.ShapeDtypeStruct((S, D), BF16), jax.ShapeDtypeStruct((1, D), F32)],
        compiler_params=_params(("arbitrary",)),
    )(dh, f, gvec)


def _loss_head(h, gain, target, name):
    S, D = h.shape
    tr = _row_tile(S, 512)

    def body(h_ref, g_ref, t_ref, loss_ref, dh_ref, dg_ref):
        @pl.when(pl.program_id(0) == 0)
        def _():
            loss_ref[...] = jnp.zeros_like(loss_ref)
            dg_ref[...] = jnp.zeros_like(dg_ref)

        x = h_ref[...]
        g = g_ref[...]
        r = lax.rsqrt(jnp.mean(x * x, axis=-1, keepdims=True) + RMS_EPS)
        xh = x * r
        err = xh * g - t_ref[...]
        part = 0.5 * jnp.sum(jnp.mean(err * err, axis=-1, keepdims=True), axis=0, keepdims=True)
        loss_ref[...] += jnp.broadcast_to(part, loss_ref.shape)
        dy = err * (1.0 / D)
        dg_ref[...] += jnp.sum(dy * xh, axis=0, keepdims=True)
        dxh = dy * g
        dh_ref[...] = r * (dxh - xh * jnp.mean(dxh * xh, axis=-1, keepdims=True))

    row = pl.BlockSpec((tr, D), lambda i: (i, 0))
    vec = _vec_spec(D, 1)
    return pl.pallas_call(
        body, name=name, grid=(S // tr,),
        in_specs=[row, vec, row],
        out_specs=[pl.BlockSpec((1, 128), lambda i: (0, 0)), row, vec],
        out_shape=[jax.ShapeDtypeStruct((1, 128), F32), jax.ShapeDtypeStruct((S, D), F32),
                   jax.ShapeDtypeStruct((1, D), F32)],
        compiler_params=_params(("arbitrary",)),
    )(h, gain, target)


def _ffn_gate_up(n, wg_t, wu_t, name):
    S, D = n.shape
    nk, w, _ = wg_t.shape
    tm = _row_tile(S, 512)

    def body(n_ref, wg_ref, wu_ref, ga_ref, up_ref, act_ref):
        x = n_ref[...]
        ga = lax.dot_general(x, wg_ref[...], NT, preferred_element_type=F32)
        up = lax.dot_general(x, wu_ref[...], NT, preferred_element_type=F32)
        ga_ref[...] = ga.astype(BF16)
        up_ref[...] = up.astype(BF16)
        act_ref[...] = ((ga * _sigmoid(ga)) * up).astype(BF16)

    wspec = pl.BlockSpec((None, w, D), lambda k, m: (k, 0, 0))
    ospec = pl.BlockSpec((None, tm, w), lambda k, m: (k, m, 0))
    out = jax.ShapeDtypeStruct((nk, S, w), BF16)
    return pl.pallas_call(
        body, name=name, grid=(nk, S // tm),
        in_specs=[pl.BlockSpec((tm, D), lambda k, m: (m, 0)), wspec, wspec],
        out_specs=[ospec, ospec, ospec], out_shape=[out, out, out],
        compiler_params=_params(("parallel", "parallel")),
    )(n, wg_t, wu_t)


def _mm_residual(lhs, w, h_in, gvec, coef, name):
    nk, S, kc = lhs.shape
    D = w.shape[2]
    tm = _row_tile(S, 512)

    def body(l_ref, w_ref, h_ref, g_ref, ho_ref, f_ref, acc_ref):
        k = pl.program_id(1)

        @pl.when(k == 0)
        def _():
            acc_ref[...] = jnp.zeros_like(acc_ref)

        acc_ref[...] += jnp.dot(l_ref[...], w_ref[...], preferred_element_type=F32)

        @pl.when(k == nk - 1)
        def _():
            f = acc_ref[...]
            f_ref[...] = f.astype(BF16)
            ho_ref[...] = h_ref[...] + (coef * g_ref[...]) * f

    row = pl.BlockSpec((tm, D), lambda m, k: (m, 0))
    return pl.pallas_call(
        body, name=name, grid=(S // tm, nk),
        in_specs=[pl.BlockSpec((None, tm, kc), lambda m, k: (k, m, 0)),
                  pl.BlockSpec((None, kc, D), lambda m, k: (k, 0, 0)), row, _vec_spec(D, 2)],
        out_specs=[row, row],
        out_shape=[jax.ShapeDtypeStruct((S, D), F32), jax.ShapeDtypeStruct((S, D), BF16)],
        scratch_shapes=[pltpu.VMEM((tm, D), F32)],
        compiler_params=_params(("parallel", "arbitrary")),
    )(lhs, w, h_in, gvec)


def _mm_cols(n, w, name):
    S, D = n.shape
    nk, _, wd = w.shape
    assert wd % 128 == 0
    tm = _row_tile(S, 512)

    def body(n_ref, w_ref, o_ref):
        o_ref[...] = jnp.dot(n_ref[...], w_ref[...], preferred_element_type=F32)

    return pl.pallas_call(
        body, name=name, grid=(nk, S // tm),
        in_specs=[pl.BlockSpec((tm, D), lambda k, m: (m, 0)), pl.BlockSpec((None, D, wd), lambda k, m: (k, 0, 0))],
        out_specs=pl.BlockSpec((tm, wd), lambda k, m: (m, k)),
        out_shape=jax.ShapeDtypeStruct((S, nk * wd), F32),
        compiler_params=_params(("parallel", "parallel")),
    )(n, w)


def _ffn_dact(df, wd, ga, up, name, deps=()):
    S, D = df.shape
    nk, w, _ = wd.shape
    tm = _row_tile(S, 512)
    deps = _deps(deps)

    def body(df_ref, wd_ref, ga_ref, up_ref, dga_ref, dup_ref):
        dact = lax.dot_general(df_ref[...], wd_ref[...], NT, preferred_element_type=F32)
        ga_ = ga_ref[...].astype(F32)
        up_ = up_ref[...].astype(F32)
        sig = _sigmoid(ga_)
        dga_ref[...] = (dact * up_ * (sig * (1.0 + ga_ * (1.0 - sig)))).astype(BF16)
        dup_ref[...] = (dact * (ga_ * sig)).astype(BF16)

    cspec = pl.BlockSpec((None, tm, w), lambda k, m: (k, m, 0))
    out = jax.ShapeDtypeStruct((nk, S, w), BF16)
    return pl.pallas_call(
        _blind_to(body, 4, len(deps)), name=name, grid=(nk, S // tm),
        in_specs=[pl.BlockSpec((tm, D), lambda k, m: (m, 0)), pl.BlockSpec((None, w, D), lambda k, m: (k, 0, 0)),
                  cspec, cspec] + [ANY_SPEC] * len(deps),
        out_specs=[cspec, cspec], out_shape=[out, out],
        compiler_params=_params(("parallel", "parallel")),
    )(df, wd, ga, up, *deps)


def _mm_nt(d, w, name):
    S, K = d.shape
    N = w.shape[0]
    tm = _row_tile(S, 512)

    def body(d_ref, w_ref, o_ref):
        o_ref[...] = lax.dot_general(d_ref[...], w_ref[...], NT, preferred_element_type=F32)

    return pl.pallas_call(
        body, name=name, grid=(S // tm,),
        in_specs=[pl.BlockSpec((tm, K), lambda m: (m, 0)), pl.BlockSpec((N, K), lambda m: (0, 0))],
        out_specs=pl.BlockSpec((tm, N), lambda m: (m, 0)),
        out_shape=jax.ShapeDtypeStruct((S, N), F32),
        compiler_params=_params(("parallel",)),
    )(d, w)


def _mm_reduce(lhs_list, w_list, chunked3d, w_is_kd, name, deps=()):
    nk = w_list[0].shape[0]
    kc, D = w_list[0].shape[1:] if w_is_kd else w_list[0].shape[:0:-1]
    S = lhs_list[0].shape[1] if chunked3d else lhs_list[0].shape[0]
    tm = _row_tile(S, 512)
    npair = len(lhs_list)
    deps = _deps(deps)

    def body(*refs):
        l_refs, w_refs = refs[:npair], refs[npair:2 * npair]
        o_ref, acc_ref = refs[2 * npair], refs[2 * npair + 1]
        k = pl.program_id(1)

        @pl.when(k == 0)
        def _():
            acc_ref[...] = jnp.zeros_like(acc_ref)

        for l_ref, w_ref in zip(l_refs, w_refs):
            if w_is_kd:
                acc_ref[...] += jnp.dot(l_ref[...], w_ref[...], preferred_element_type=F32)
            else:
                acc_ref[...] += lax.dot_general(l_ref[...], w_ref[...], NT, preferred_element_type=F32)

        @pl.when(k == nk - 1)
        def _():
            o_ref[...] = acc_ref[...]

    if chunked3d:
        lspec = pl.BlockSpec((None, tm, kc), lambda m, k: (k, m, 0))
    else:
        lspec = pl.BlockSpec((tm, kc), lambda m, k: (m, k))
    wspec = pl.BlockSpec((None,) + tuple(w_list[0].shape[1:]), lambda m, k: (k, 0, 0))
    return pl.pallas_call(
        _blind_to(body, 2 * npair, len(deps)), name=name, grid=(S // tm, nk),
        in_specs=[lspec] * npair + [wspec] * npair + [ANY_SPEC] * len(deps),
        out_specs=pl.BlockSpec((tm, D), lambda m, k: (m, 0)),
        out_shape=jax.ShapeDtypeStruct((S, D), F32),
        scratch_shapes=[pltpu.VMEM((tm, D), F32)],
        compiler_params=_params(("parallel", "arbitrary")),
    )(*lhs_list, *w_list, *deps)


def _wgrad_chunk_lhs(lhs_list, rhs, name, deps=()):
    nk, S, w = lhs_list[0].shape
    D = rhs.shape[1]
    ts = _row_tile(S, 512)
    ns = S // ts
    nl = len(lhs_list)
    deps = _deps(deps)

    def body(*refs):
        l_refs, r_ref = refs[:nl], refs[nl]
        o_refs, acc_refs = refs[nl + 1:2 * nl + 1], refs[2 * nl + 1:]
        s = pl.program_id(1)

        @pl.when(s == 0)
        def _():
            for acc_ref in acc_refs:
                acc_ref[...] = jnp.zeros_like(acc_ref)

        x = r_ref[...]
        for l_ref, acc_ref in zip(l_refs, acc_refs):
            acc_ref[...] += lax.dot_general(l_ref[...], x, TN, preferred_element_type=F32)

        @pl.when(s == ns - 1)
        def _():
            for o_ref, acc_ref in zip(o_refs, acc_refs):
                o_ref[...] = acc_ref[...]

    return pl.pallas_call(
        _blind_to(body, nl + 1, len(deps)), name=name, grid=(nk, ns),
        in_specs=[pl.BlockSpec((None, ts, w), lambda k, s: (k, s, 0))] * nl
        + [pl.BlockSpec((ts, D), lambda k, s: (s, 0))] + [ANY_SPEC] * len(deps),
        out_specs=[pl.BlockSpec((None, w, D), lambda k, s: (k, 0, 0))] * nl,
        out_shape=[jax.ShapeDtypeStruct((nk, w, D), F32)] * nl,
        scratch_shapes=[pltpu.VMEM((w, D), F32)] * nl,
        compiler_params=_params(("parallel", "arbitrary")),
    )(*lhs_list, rhs, *deps)


def _wgrad_chunk_rhs(lhs, rhs_list, nk, chunked3d, name, deps=()):
    S, D = lhs.shape
    w = rhs_list[0].shape[2] if chunked3d else rhs_list[0].shape[1] // nk
    ts = _row_tile(S, 512)
    ns = S // ts
    nr = len(rhs_list)
    deps = _deps(deps)

    def body(*refs):
        l_ref, r_refs = refs[0], refs[1:1 + nr]
        o_refs, acc_refs = refs[1 + nr:1 + 2 * nr], refs[1 + 2 * nr:]
        s = pl.program_id(1)

        @pl.when(s == 0)
        def _():
            for acc_ref in acc_refs:
                acc_ref[...] = jnp.zeros_like(acc_ref)

        x = l_ref[...]
        for r_ref, acc_ref in zip(r_refs, acc_refs):
            acc_ref[...] += lax.dot_general(x, r_ref[...], TN, preferred_element_type=F32)

        @pl.when(s == ns - 1)
        def _():
            for o_ref, acc_ref in zip(o_refs, acc_refs):
                o_ref[...] = acc_ref[...]

    if chunked3d:
        rspec = pl.BlockSpec((None, ts, w), lambda k, s: (k, s, 0))
    else:
        rspec = pl.BlockSpec((ts, w), lambda k, s: (s, k))
    ospec = pl.BlockSpec((None, D, w), lambda k, s: (k, 0, 0))
    return pl.pallas_call(
        _blind_to(body, 1 + nr, len(deps)), name=name, grid=(nk, ns),
        in_specs=[pl.BlockSpec((ts, D), lambda k, s: (s, 0))] + [rspec] * nr + [ANY_SPEC] * len(deps),
        out_specs=[ospec] * nr,
        out_shape=[jax.ShapeDtypeStruct((nk, D, w), F32)] * nr,
        scratch_shapes=[pltpu.VMEM((D, w), F32)] * nr,
        compiler_params=_params(("parallel", "arbitrary")),
    )(lhs, *rhs_list, *deps)


def _rope_tables(S):
    pos = np.arange(S, dtype=np.float32)
    inv_freq = (ROPE_THETA ** (-np.arange(0, HEAD_DIM, 2, dtype=np.float32) / HEAD_DIM)).astype(np.float32)
    ang = (pos[:, None] * inv_freq[None, :]).astype(np.float64)
    cos, sin = np.cos(ang).astype(np.float32), np.sin(ang).astype(np.float32)
    cos2 = np.concatenate([cos, cos, cos, cos], axis=1)
    sin2 = np.concatenate([-sin, sin, -sin, sin], axis=1)
    return jnp.asarray(cos2), jnp.asarray(sin2)


def _rotate(t, cos, sin_signed):
    half = HEAD_DIM // 2
    lane = lax.broadcasted_iota(jnp.int32, t.shape, 1)
    first = (lane % HEAD_DIM) < half
    partner = jnp.where(first, pltpu.roll(t, 128 - half, 1), pltpu.roll(t, half, 1))
    return t * cos + partner * sin_signed


def _qkv_rope(proj, cos, sin, name):
    S = proj.shape[0]
    A = ATTN_WIDTH
    tr = _row_tile(S, 512)
    nb = A // 128
    scale = HEAD_DIM ** -0.5

    def body(q_ref, k_ref, v_ref, c_ref, s_ref, qo_ref, ko_ref, vo_ref):
        c, s = c_ref[...], s_ref[...]
        qo_ref[...] = _rotate(q_ref[...], c, s) * scale
        ko_ref[...] = _rotate(k_ref[...], c, s)
        vo_ref[...] = v_ref[...]

    def col(off):
        return pl.BlockSpec((tr, 128), lambda i, j: (i, off + j))

    tab = pl.BlockSpec((tr, 128), lambda i, j: (i, 0))
    out = jax.ShapeDtypeStruct((S, A), F32)
    return pl.pallas_call(
        body, name=name, grid=(S // tr, nb),
        in_specs=[col(0), col(nb), col(2 * nb), tab, tab],
        out_specs=[col(0), col(0), col(0)], out_shape=[out, out, out],
        compiler_params=_params(("parallel", "parallel")),
    )(proj, proj, proj, cos, sin)


def _band_masks(T, has_prev):
    qi = lax.broadcasted_iota(jnp.int32, (T, T), 0)
    kj = lax.broadcasted_iota(jnp.int32, (T, T), 1)
    return qi >= kj, (kj >= qi) & has_prev


def _branch_blocks(rows, dilation):
    T = min(ATTN_BLOCK, rows // dilation)
    return T, rows // T


def _block_rows(base, T, dilation):
    if dilation == 1:
        return pl.ds(pl.multiple_of(base, T), T)
    return pl.ds(base, T, stride=dilation)


def _attn_fwd(q, k, v, name):
    S, A = q.shape
    sup = min(S, SUPER_ROWS)
    nd = len(DILATIONS)
    assert S % sup == 0

    def body(q_ref, k_ref, v_ref, attn_ref, lse_ref, acc_s, m_s, l_s):
        lane = lax.broadcasted_iota(jnp.int32, (1, 128), 1)
        head0 = lane < HEAD_DIM

        def supertile(st, carry):
            row0 = st * sup
            for di, dil in enumerate(DILATIONS):
                T, nblk = _branch_blocks(sup, dil)
                span = T * dil
                assert T == ATTN_BLOCK or sup == S

                def block(idx, c2, di=di, dil=dil, T=T, span=span):
                    r = idx % dil
                    loc = (idx // dil) * span + r
                    base = row0 + loc
                    rows = _block_rows(base, T, dil)
                    prev = _block_rows(jnp.maximum(base - span, r), T, dil)
                    qb = q_ref[rows, :].astype(BF16)
                    kc, vc = k_ref[rows, :].astype(BF16), v_ref[rows, :].astype(BF16)
                    kp, vp = k_ref[prev, :].astype(BF16), v_ref[prev, :].astype(BF16)
                    valid_d, valid_o = _band_masks(T, base >= span)
                    accs, ms, ls = [], [], []
                    for hmask in (head0, jnp.logical_not(head0)):
                        qh = jnp.where(hmask, qb, jnp.zeros_like(qb))
                        sd = jnp.where(valid_d, lax.dot_general(qh, kc, NT, preferred_element_type=F32), NEG)
                        so = jnp.where(valid_o, lax.dot_general(qh, kp, NT, preferred_element_type=F32), NEG)
                        m = jnp.maximum(jnp.max(sd, axis=-1, keepdims=True), jnp.max(so, axis=-1, keepdims=True))
                        pd = jnp.exp(sd - m)
                        po = jnp.exp(so - m)
                        ls.append(jnp.sum(pd, axis=-1, keepdims=True) + jnp.sum(po, axis=-1, keepdims=True))
                        acc = jnp.dot(pd.astype(BF16), vc, preferred_element_type=F32)
                        accs.append(acc + jnp.dot(po.astype(BF16), vp, preferred_element_type=F32))
                        ms.append(m)
                    lrows = _block_rows(di * sup + loc, T, dil)
                    acc_s[lrows, :] = jnp.where(head0, accs[0], accs[1])
                    m_s[lrows, :] = jnp.where(head0, ms[0], ms[1])
                    l_s[lrows, :] = jnp.where(head0, ls[0], ls[1])
                    return c2

                lax.fori_loop(0, nblk, block, 0, unroll=2)

            chunk = min(sup, 256)

            def merge(ci, c2):
                lr = [pl.ds(pl.multiple_of(di * sup + ci * chunk, chunk), chunk) for di in range(nd)]
                gr = pl.ds(pl.multiple_of(row0 + ci * chunk, chunk), chunk)
                m0, m1, m2 = m_s[lr[0], :], m_s[lr[1], :], m_s[lr[2], :]
                mm = jnp.maximum(jnp.maximum(m0, m1), m2)
                w0, w1, w2 = jnp.exp(m0 - mm), jnp.exp(m1 - mm), jnp.exp(m2 - mm)
                den = (w0 * l_s[lr[0], :] + w1 * l_s[lr[1], :]) + w2 * l_s[lr[2], :]
                num = (w0 * acc_s[lr[0], :] + w1 * acc_s[lr[1], :]) + w2 * acc_s[lr[2], :]
                attn_ref[gr, :] = num / den
                lse_ref[gr, :] = mm + jnp.log(den)
                return c2

            lax.fori_loop(0, sup // chunk, merge, 0)
            return carry

        lax.fori_loop(0, S // sup, supertile, 0)

    blk = pl.BlockSpec((S, 128), lambda j: (0, j))
    out = jax.ShapeDtypeStruct((S, A), F32)
    return pl.pallas_call(
        body, name=name, grid=(A // 128,),
        in_specs=[blk, blk, blk], out_specs=[blk, blk], out_shape=[out, out],
        scratch_shapes=[pltpu.VMEM((nd * sup, 128), F32)] * 3,
        compiler_params=_params(("parallel",)),
    )(q, k, v)


def _attn_out_bwd(dy, attn, gain, name):
    S, A = attn.shape
    tr = _row_tile(S, 256)

    def body(dy_ref, a_ref, g_ref, da_ref, dl_ref, dg_ref):
        @pl.when(pl.program_id(0) == 0)
        def _():
            dg_ref[...] = jnp.zeros_like(dg_ref)

        x = a_ref[...]
        dy_ = dy_ref[...]
        r = lax.rsqrt(jnp.mean(x * x, axis=-1, keepdims=True) + RMS_EPS)
        xh = x * r
        dg_ref[...] += jnp.sum(dy_ * xh, axis=0, keepdims=True)
        dxh = dy_ * g_ref[...]
        dx = r * (dxh - xh * jnp.mean(dxh * xh, axis=-1, keepdims=True))
        da_ref[...] = dx
        prod = dx * x
        hi = lax.broadcasted_iota(jnp.int32, (A, A), 0) // HEAD_DIM
        hj = lax.broadcasted_iota(jnp.int32, (A, A), 1) // HEAD_DIM
        same_head = (hi == hj).astype(F32)
        dl_ref[...] = jnp.dot(prod, same_head, preferred_element_type=F32, precision=lax.Precision.HIGHEST)

    row = pl.BlockSpec((tr, A), lambda i: (i, 0))
    vec = _vec_spec(A, 1)
    return pl.pallas_call(
        body, name=name, grid=(S // tr,),
        in_specs=[row, row, vec], out_specs=[row, row, vec],
        out_shape=[jax.ShapeDtypeStruct((S, A), F32), jax.ShapeDtypeStruct((S, A), F32),
                   jax.ShapeDtypeStruct((1, A), F32)],
        compiler_params=_params(("arbitrary",)),
    )(dy, attn, gain)


def _attn_bwd(q, k, v, da, lse, delta, name):
    S, A = q.shape

    def body(q_ref, k_ref, v_ref, da_ref, lse_ref, dl_ref, dq_ref, dk_ref, dv_ref):
        lane = lax.broadcasted_iota(jnp.int32, (1, 128), 1)
        head0 = lane < HEAD_DIM
        dq_ref[...] = jnp.zeros_like(dq_ref)
        dk_ref[...] = jnp.zeros_like(dk_ref)
        dv_ref[...] = jnp.zeros_like(dv_ref)
        for dil in DILATIONS:
            T, nblk = _branch_blocks(S, dil)
            span = T * dil

            def block(idx, carry, dil=dil, T=T, span=span):
                r = idx % dil
                base = (idx // dil) * span + r
                rows = _block_rows(base, T, dil)
                prev = _block_rows(jnp.maximum(base - span, r), T, dil)
                qb, dab = q_ref[rows, :].astype(BF16), da_ref[rows, :].astype(BF16)
                lse_b, dl_b = lse_ref[rows, :], dl_ref[rows, :]
                kc, vc = k_ref[rows, :].astype(BF16), v_ref[rows, :].astype(BF16)
                kp, vp = k_ref[prev, :].astype(BF16), v_ref[prev, :].astype(BF16)
                valid_d, valid_o = _band_masks(T, base >= span)
                dqs = []
                dkc = dkp = dvc = dvp = None
                for h, hmask in enumerate((head0, jnp.logical_not(head0))):
                    qh = jnp.where(hmask, qb, jnp.zeros_like(qb))
                    dah = jnp.where(hmask, dab, jnp.zeros_like(dab))
                    c0 = h * HEAD_DIM
                    lse_h = lse_b[:, c0:c0 + 1]
                    dl_h = dl_b[:, c0:c0 + 1]
                    sd = lax.dot_general(qh, kc, NT, preferred_element_type=F32)
                    so = lax.dot_general(qh, kp, NT, preferred_element_type=F32)
                    pd = jnp.where(valid_d, jnp.exp(sd - lse_h), 0.0)
                    po = jnp.where(valid_o, jnp.exp(so - lse_h), 0.0)
                    dpd = lax.dot_general(dah, vc, NT, preferred_element_type=F32)
                    dpo = lax.dot_general(dah, vp, NT, preferred_element_type=F32)
                    dsd = (pd * (dpd - dl_h)).astype(BF16)
                    dso = (po * (dpo - dl_h)).astype(BF16)
                    pdb, pob = pd.astype(BF16), po.astype(BF16)
                    dqs.append(jnp.dot(dsd, kc, preferred_element_type=F32)
                               + jnp.dot(dso, kp, preferred_element_type=F32))
                    t_kc = lax.dot_general(dsd, qh, TN, preferred_element_type=F32)
                    t_kp = lax.dot_general(dso, qh, TN, preferred_element_type=F32)
                    t_vc = lax.dot_general(pdb, dah, TN, preferred_element_type=F32)
                    t_vp = lax.dot_general(pob, dah, TN, preferred_element_type=F32)
                    if h == 0:
                        dkc, dkp, dvc, dvp = t_kc, t_kp, t_vc, t_vp
                    else:
                        dkc, dkp, dvc, dvp = dkc + t_kc, dkp + t_kp, dvc + t_vc, dvp + t_vp
                dq_ref[rows, :] += jnp.where(head0, dqs[0], dqs[1])
                dk_ref[rows, :] += dkc
                dv_ref[rows, :] += dvc
                dk_ref[prev, :] += dkp
                dv_ref[prev, :] += dvp
                return carry

            lax.fori_loop(0, nblk, block, 0, unroll=2)

    blk = pl.BlockSpec((S, 128), lambda j: (0, j))
    out = jax.ShapeDtypeStruct((S, A), F32)
    return pl.pallas_call(
        body, name=name, grid=(A // 128,),
        in_specs=[blk] * 6, out_specs=[blk] * 3, out_shape=[out] * 3,
        compiler_params=_params(("parallel",)),
    )(q, k, v, da, lse, delta)


def _glu_window(a_ref, b_ref, ah_ref, bh_ref, first):
    u0 = a_ref[...] * _sigmoid(b_ref[...])
    u0h = ah_ref[...] * _sigmoid(bh_ref[...])
    u0h = jnp.where(first, jnp.zeros_like(u0h), u0h)
    return jnp.concatenate([u0h, u0], axis=0)


def _conv_norms(u1, lng, lnb):
    mu = jnp.mean(u1, axis=-1, keepdims=True)
    xc = u1 - mu
    rstd = lax.rsqrt(jnp.mean(xc * xc, axis=-1, keepdims=True) + LN_EPS)
    u1h = xc * rstd
    u2 = u1h * lng + lnb
    sig = _sigmoid(u2)
    u3 = u2 * sig
    r = lax.rsqrt(jnp.mean(u3 * u3, axis=-1, keepdims=True) + RMS_EPS)
    return rstd, u1h, u2, sig, u3, r


def _conv_specs(tr, C, col_a, col_b):
    per = tr // HALO

    def tile(col):
        return pl.BlockSpec((tr, C), lambda i: (i, col))

    def halo(col):
        return pl.BlockSpec((HALO, C), lambda i: (jnp.maximum(i * per - 1, 0), col))

    return tile(col_a), tile(col_b), halo(col_a), halo(col_b)


def _mixer_merge(proj, attn, cw, cb, lng, lnb, gat, gco, name):
    S = proj.shape[0]
    C = CONV_WIDTH
    A = attn.shape[1]
    tr = _row_tile(S, 256)

    def body(a_ref, b_ref, ah_ref, bh_ref, at_ref, w_ref, cb_ref, lng_ref, lnb_ref, gat_ref, gco_ref, u1_ref, y_ref):
        win = _glu_window(a_ref, b_ref, ah_ref, bh_ref, pl.program_id(0) == 0)
        acc = jnp.broadcast_to(cb_ref[...], (tr, C))
        for j in range(CONV_KERNEL):
            off = HALO - (CONV_KERNEL - 1) + j
            acc = acc + w_ref[j:j + 1, :] * win[off:off + tr, :]
        u1_ref[...] = acc
        _, _, _, _, u3, r = _conv_norms(acc, lng_ref[...], lnb_ref[...])
        y_ref[:, A:] = ((u3 * r) * gco_ref[...]).astype(BF16)
        x = at_ref[...]
        ra = lax.rsqrt(jnp.mean(x * x, axis=-1, keepdims=True) + RMS_EPS)
        y_ref[:, :A] = ((x * ra) * gat_ref[...]).astype(BF16)

    ta, tb, ha, hb = _conv_specs(tr, C, 3, 4)
    row = pl.BlockSpec((tr, C), lambda i: (i, 0))
    vec = _vec_spec(C, 1)
    return pl.pallas_call(
        body, name=name, grid=(S // tr,),
        in_specs=[ta, tb, ha, hb, pl.BlockSpec((tr, A), lambda i: (i, 0)), pl.BlockSpec((HALO, C), lambda i: (0, 0)),
                  vec, vec, vec, _vec_spec(A, 1), vec],
        out_specs=[row, pl.BlockSpec((tr, A + C), lambda i: (i, 0))],
        out_shape=[jax.ShapeDtypeStruct((S, C), F32), jax.ShapeDtypeStruct((S, A + C), BF16)],
        compiler_params=_params(("parallel",)),
    )(proj, proj, proj, proj, attn, cw, cb, lng, lnb, gat, gco)


def _conv_bwd_norms(dy, u1, lng, lnb, gco, name):
    S, C = u1.shape
    tr = _row_tile(S, 256)

    def body(dy_ref, u1_ref, lng_ref, lnb_ref, gco_ref, du1_ref, dgco_ref, dlng_ref, dlnb_ref, dcb_ref):
        @pl.when(pl.program_id(0) == 0)
        def _():
            for ref in (dgco_ref, dlng_ref, dlnb_ref, dcb_ref):
                ref[...] = jnp.zeros_like(ref)

        lng = lng_ref[...]
        rstd, u1h, u2, sig, u3, r = _conv_norms(u1_ref[...], lng, lnb_ref[...])
        dy_ = dy_ref[...]
        u3h = u3 * r
        dgco_ref[...] += jnp.sum(dy_ * u3h, axis=0, keepdims=True)
        du3h = dy_ * gco_ref[...]
        du3 = r * (du3h - u3h * jnp.mean(du3h * u3h, axis=-1, keepdims=True))
        du2 = du3 * (sig * (1.0 + u2 * (1.0 - sig)))
        dlng_ref[...] += jnp.sum(du2 * u1h, axis=0, keepdims=True)
        dlnb_ref[...] += jnp.sum(du2, axis=0, keepdims=True)
        du1h = du2 * lng
        du1 = rstd * (du1h - jnp.mean(du1h, axis=-1, keepdims=True)
                      - u1h * jnp.mean(du1h * u1h, axis=-1, keepdims=True))
        du1_ref[...] = du1
        dcb_ref[...] += jnp.sum(du1, axis=0, keepdims=True)

    row = pl.BlockSpec((tr, C), lambda i: (i, 0))
    vec = _vec_spec(C, 1)
    return pl.pallas_call(
        body, name=name, grid=(S // tr,),
        in_specs=[pl.BlockSpec((tr, C), lambda i: (i, 1)), row, vec, vec, vec],
        out_specs=[row, vec, vec, vec, vec],
        out_shape=[jax.ShapeDtypeStruct((S, C), F32)] + [jax.ShapeDtypeStruct((1, C), F32)] * 4,
        compiler_params=_params(("arbitrary",)),
    )(dy, u1, lng, lnb, gco)


def _dproj(du1, proj, cw, dq, dk, dv, cos, sin, name):
    S, C = du1.shape
    A = dq.shape[1]
    tr = _row_tile(S, 256)
    nt = S // tr
    per = tr // HALO
    scale = HEAD_DIM ** -0.5

    def body(du_ref, dun_ref, a_ref, b_ref, ah_ref, bh_ref, w_ref, dq_ref, dk_ref, dv_ref, cos_ref, sin_ref,
             dp_ref, dw_ref):
        i = pl.program_id(0)

        @pl.when(i == 0)
        def _():
            dw_ref[...] = jnp.zeros_like(dw_ref)

        win = _glu_window(a_ref, b_ref, ah_ref, bh_ref, i == 0)
        du = du_ref[...]
        nxt = jnp.where(i == nt - 1, jnp.zeros_like(dun_ref[...]), dun_ref[...])
        dwin = jnp.concatenate([du, nxt], axis=0)
        du0 = jnp.zeros((tr, C), F32)
        taps = []
        for j in range(CONV_KERNEL):
            back = CONV_KERNEL - 1 - j
            du0 = du0 + w_ref[j:j + 1, :] * dwin[back:back + tr, :]
            off = HALO - (CONV_KERNEL - 1) + j
            taps.append(jnp.sum(du * win[off:off + tr, :], axis=0, keepdims=True))
        taps.append(jnp.zeros((HALO - CONV_KERNEL, C), F32))
        dw_ref[...] += jnp.concatenate(taps, axis=0)
        a, sig = a_ref[...], _sigmoid(b_ref[...])
        dp_ref[:, 3 * A:3 * A + C] = (du0 * sig).astype(BF16)
        dp_ref[:, 3 * A + C:] = (du0 * a * sig * (1.0 - sig)).astype(BF16)
        cos_, nsin = cos_ref[...], -sin_ref[...]
        for j in range(A // 128):
            lanes = slice(j * 128, (j + 1) * 128)
            dp_ref[:, j * 128:(j + 1) * 128] = (_rotate(dq_ref[:, lanes], cos_, nsin) * scale).astype(BF16)
            dp_ref[:, A + j * 128:A + (j + 1) * 128] = _rotate(dk_ref[:, lanes], cos_, nsin).astype(BF16)
        dp_ref[:, 2 * A:3 * A] = dv_ref[...].astype(BF16)

    ta, tb, ha, hb = _conv_specs(tr, C, 3, 4)
    row = pl.BlockSpec((tr, C), lambda i: (i, 0))
    arow = pl.BlockSpec((tr, A), lambda i: (i, 0))
    tab = pl.BlockSpec((tr, 128), lambda i: (i, 0))
    nxt = pl.BlockSpec((HALO, C), lambda i: (jnp.minimum((i + 1) * per, S // HALO - 1), 0))
    wspec = pl.BlockSpec((HALO, C), lambda i: (0, 0))
    return pl.pallas_call(
        body, name=name, grid=(nt,),
        in_specs=[row, nxt, ta, tb, ha, hb, wspec, arow, arow, arow, tab, tab],
        out_specs=[pl.BlockSpec((tr, 3 * A + 2 * C), lambda i: (i, 0)), wspec],
        out_shape=[jax.ShapeDtypeStruct((S, 3 * A + 2 * C), BF16), jax.ShapeDtypeStruct((HALO, C), F32)],
        compiler_params=_params(("arbitrary",)),
    )(du1, du1, proj, proj, proj, proj, cw, dq, dk, dv, cos, sin)


def _ada_fwd(c_all, w, b, name):
    B, D = c_all.shape
    N = w.shape[1]
    tn = 768 if N % 768 == 0 else N

    def body(c_ref, w_ref, b_ref, o_ref):
        c = c_ref[...]
        a = (c * _sigmoid(c)).astype(BF16)
        o_ref[...] = jnp.dot(a, w_ref[...].astype(BF16), preferred_element_type=F32) + b_ref[...]

    return pl.pallas_call(
        body, name=name, grid=(N // tn,),
        in_specs=[pl.BlockSpec((B, D), lambda j: (0, 0)), pl.BlockSpec((D, tn), lambda j: (0, j)),
                  pl.BlockSpec((1, tn), lambda j: (0, j))],
        out_specs=pl.BlockSpec((B, tn), lambda j: (0, j)),
        out_shape=jax.ShapeDtypeStruct((B, N), F32),
        compiler_params=_params(("parallel",)),
    )(c_all, w, b)


def _ada_wgrad(c_t, dmod, name):
    D, B = c_t.shape
    N = dmod.shape[1]
    tn = 768 if N % 768 == 0 else N

    def body(c_ref, d_ref, o_ref):
        c = c_ref[...]
        a = (c * _sigmoid(c)).astype(BF16)
        o_ref[...] = jnp.dot(a, d_ref[...].astype(BF16), preferred_element_type=F32)

    return pl.pallas_call(
        body, name=name, grid=(N // tn,),
        in_specs=[pl.BlockSpec((D, B), lambda j: (0, 0)), pl.BlockSpec((B, tn), lambda j: (0, j))],
        out_specs=pl.BlockSpec((D, tn), lambda j: (0, j)),
        out_shape=jax.ShapeDtypeStruct((D, N), F32),
        compiler_params=_params(("parallel",)),
    )(c_t, dmod)


def _sum_rows(x, name):
    R, N = x.shape

    def body(x_ref, o_ref):
        acc = x_ref[0:1, :]
        for r in range(1, R):
            acc = acc + x_ref[r:r + 1, :]
        o_ref[...] = acc

    return pl.pallas_call(
        body, name=name, out_shape=jax.ShapeDtypeStruct((1, N), F32),
        compiler_params=_params(),
    )(x)


def _adamw(w, g, m, v, name, deps=()):
    R, C = w.shape
    tr = _row_tile(R, 256) if R % 256 == 0 else R
    bc1 = 1.0 - ADAM_B1 ** ADAM_STEP
    bc2 = 1.0 - ADAM_B2 ** ADAM_STEP
    deps = _deps(deps)

    def body(w_ref, g_ref, m_ref, v_ref, d_ref, mo_ref, vo_ref):
        g_ = g_ref[...]
        m_ = ADAM_B1 * m_ref[...] + (1.0 - ADAM_B1) * g_
        v_ = ADAM_B2 * v_ref[...] + (1.0 - ADAM_B2) * (g_ * g_)
        mo_ref[...] = m_
        vo_ref[...] = v_
        d_ref[...] = -ADAM_LR * ((m_ / bc1) / (jnp.sqrt(v_ / bc2) + ADAM_EPS) + ADAM_WD * w_ref[...])

    row = pl.BlockSpec((tr, C), lambda i: (i, 0))
    out = jax.ShapeDtypeStruct((R, C), F32)
    return pl.pallas_call(
        _blind_to(body, 4, len(deps)), name=name, grid=(R // tr,),
        in_specs=[row] * 4 + [ANY_SPEC] * len(deps), out_specs=[row] * 3, out_shape=[out] * 3,
        compiler_params=_params(("parallel",)),
    )(w, g, m, v, *deps)


def _coords():
    return lax.axis_index("x"), lax.axis_index("y"), lax.axis_index("c")


def _all_gather8(x, name, deps=()):
    R, N = x.shape
    assert R == 8
    flips = [(fx, fy, fc) for fx in (0, 1) for fy in (0, 1) for fc in (0, 1)][1:]
    deps = _deps(deps)

    def body(x_ref, o_ref, send_sems, recv_sems):
        mx, my, mc = _coords()
        me = 4 * mx + 2 * my + mc

        def rows(dev):
            return o_ref.at[pl.ds(pl.multiple_of(dev * R, R), R), :]

        o_ref[pl.ds(pl.multiple_of(me * R, R), R), :] = x_ref[...]
        copies = []
        for t, (fx, fy, fc) in enumerate(flips):
            peer = (mx ^ fx, my ^ fy, mc ^ fc)
            copies.append(pltpu.make_async_remote_copy(
                src_ref=x_ref, dst_ref=rows(me), send_sem=send_sems.at[t], recv_sem=recv_sems.at[t],
                device_id=peer, device_id_type=MESH))
        for cp in copies:
            cp.start()
        for t, (fx, fy, fc) in enumerate(flips):
            peer_id = 4 * (mx ^ fx) + 2 * (my ^ fy) + (mc ^ fc)
            pltpu.make_async_remote_copy(
                src_ref=x_ref, dst_ref=rows(peer_id), send_sem=send_sems.at[t], recv_sem=recv_sems.at[t],
                device_id=(mx ^ fx, my ^ fy, mc ^ fc), device_id_type=MESH).wait_recv()
        for cp in copies:
            cp.wait_send()

    return pl.pallas_call(
        _blind_to(body, 1, len(deps)), name=name,
        in_specs=[pl.BlockSpec(memory_space=pltpu.VMEM)] + [ANY_SPEC] * len(deps),
        out_specs=pl.BlockSpec(memory_space=pltpu.VMEM),
        out_shape=jax.ShapeDtypeStruct((N_DEV * R, N), F32),
        scratch_shapes=[pltpu.SemaphoreType.DMA((7,)), pltpu.SemaphoreType.DMA((7,))],
        compiler_params=pltpu.CompilerParams(has_side_effects=True, vmem_limit_bytes=VMEM_LIMIT_BYTES),
    )(x, *deps)


def _half_rows(rows, half):
    return pl.ds(pl.multiple_of(half * (rows // 2), 8), rows // 2)


def _split_start(bufs, plan, n, name):
    nb = len(bufs)

    def body(*refs):
        send_sems, recv_sems, token = refs[nb], refs[nb + 1], refs[-1]
        for t, (src, dst, dev) in enumerate(plan(refs[:nb])):
            pltpu.make_async_remote_copy(src_ref=src, dst_ref=dst, send_sem=send_sems.at[t],
                                         recv_sem=recv_sems.at[t], device_id=dev, device_id_type=MESH).start()
        token[...] = jnp.zeros_like(token)

    out = pl.pallas_call(
        body, name=name,
        out_shape=(pltpu.SemaphoreType.DMA((n,)), pltpu.SemaphoreType.DMA((n,)),
                   *[pltpu.HBM(b.shape, b.dtype) for b in bufs], jax.ShapeDtypeStruct((8, 128), F32)),
        in_specs=[HBM_SPEC] * nb,
        out_specs=(SEM_SPEC, SEM_SPEC, *[HBM_SPEC] * nb, pl.BlockSpec(memory_space=pltpu.VMEM)),
        input_output_aliases={i: 2 + i for i in range(nb)},
        compiler_params=pltpu.CompilerParams(has_side_effects=DATAFLOW),
    )(*[pltpu.with_memory_space_constraint(b, pltpu.HBM) for b in bufs])
    return out[0], out[1], list(out[2:2 + nb]), out[-1]


def _split_wait(bufs, send_sems, recv_sems, plan, after, name):
    nb = len(bufs)
    after = _deps(after)

    def body(*refs):
        ss, rs = refs[nb], refs[nb + 1]
        for t, (src, dst, dev) in enumerate(plan(refs[:nb])):
            cp = pltpu.make_async_remote_copy(src_ref=src, dst_ref=dst, send_sem=ss.at[t], recv_sem=rs.at[t],
                                              device_id=dev, device_id_type=MESH)
            cp.wait_send()
            cp.wait_recv()

    out = pl.pallas_call(
        body, name=name,
        out_shape=tuple(pltpu.HBM(b.shape, b.dtype) for b in bufs),
        in_specs=[HBM_SPEC] * nb + [SEM_SPEC, SEM_SPEC] + [ANY_SPEC] * len(after),
        out_specs=tuple([HBM_SPEC] * nb),
        input_output_aliases={i: i for i in range(nb)},
        compiler_params=pltpu.CompilerParams(has_side_effects=DATAFLOW),
    )(*bufs, send_sems, recv_sems, *after)
    return list(out)


class _Exchange:
    def __init__(self, bufs, plan, n, name):
        self.plan, self.name = plan, name
        self.send_sems, self.recv_sems, self.bufs, self.token = _split_start(bufs, plan, n, name + "_start")

    def wait(self, after):
        return _split_wait(self.bufs, self.send_sems, self.recv_sems, self.plan, after, self.name + "_wait")


def _cast_place(w, chip_idx, name, deps=()):
    R, C = w.shape
    tr = _row_tile(R, 256) if R % 256 == 0 else R
    deps = _deps(deps)

    def body(k_ref, w_ref, o_ref):
        o_ref[...] = w_ref[...].astype(BF16)

    grid_spec = pltpu.PrefetchScalarGridSpec(
        num_scalar_prefetch=1, grid=(R // tr,),
        in_specs=[pl.BlockSpec((tr, C), lambda i, k: (i, 0))] + [ANY_SPEC] * len(deps),
        out_specs=pl.BlockSpec((None, tr, C), lambda i, k: (k[0], i, 0)),
    )
    return pl.pallas_call(
        _blind_to(body, 2, len(deps)), name=name, grid_spec=grid_spec,
        out_shape=jax.ShapeDtypeStruct((N_CHIPS, R, C), BF16),
        compiler_params=_params(("parallel",)),
    )(chip_idx, w, *deps)


def _plan_gather(refs):
    mx, my, mc = _coords()
    me = 2 * mx + my
    plan = []
    for g in refs:
        mine = g.at[me, _half_rows(g.shape[1], mc), :]
        for fx, fy in XY_FLIPS:
            plan.append((mine, mine, (mx ^ fx, my ^ fy, mc)))
    return plan


def _plan_swap(refs):
    mx, my, mc = _coords()
    plan = []
    for g in refs:
        for fx, fy in XY_FLIPS:
            have = g.at[2 * (mx ^ fx) + (my ^ fy), _half_rows(g.shape[1], mc), :]
            plan.append((have, have, (mx, my, 1 - mc)))
    return plan


def _plan_other_halves(refs):
    n = len(refs) // 2
    mx, my, mc = _coords()
    return [(g.at[pl.ds(0, N_CHIPS), _half_rows(g.shape[1], 1 - mc), :], land, (mx, my, 1 - mc))
            for g, land in zip(refs[:n], refs[n:])]


def _plan_chunks(refs):
    n = len(refs) // 2
    mx, my, mc = _coords()
    plan = []
    for s, land in zip(refs[:n], refs[n:]):
        for t, (fx, fy) in enumerate(XY_FLIPS):
            plan.append((s.at[2 * (mx ^ fx) + (my ^ fy)], land.at[t], (mx ^ fx, my ^ fy, mc)))
    return plan


def _plan_share(refs):
    mx, my, mc = _coords()
    return [(full.at[mc], full.at[mc], (mx, my, 1 - mc)) for full in refs]


def _add_half(g, recv, core_idx, name):
    nk, R, C = g.shape
    rh = R // 2
    tr = _row_tile(rh, 128) if rh % 128 == 0 else rh
    nt = rh // tr

    def body(c_ref, g_ref, r_ref, o_ref):
        o_ref[...] = (g_ref[...] + r_ref[...]).astype(BF16)

    grid_spec = pltpu.PrefetchScalarGridSpec(
        num_scalar_prefetch=1, grid=(nk, nt),
        in_specs=[pl.BlockSpec((None, tr, C), lambda k, i, c: (k, c[0] * nt + i, 0)),
                  pl.BlockSpec((None, tr, C), lambda k, i, c: (k, i, 0))],
        out_specs=pl.BlockSpec((None, tr, C), lambda k, i, c: (k, i, 0)),
    )
    return pl.pallas_call(
        body, name=name, grid_spec=grid_spec, out_shape=jax.ShapeDtypeStruct((nk, rh, C), BF16),
        compiler_params=_params(("parallel", "parallel")),
    )(core_idx, g, recv)


def _sum_chips(s, land, chip_core, name):
    _, rh, C = s.shape
    tr = _row_tile(rh, 128) if rh % 128 == 0 else rh

    def body(p_ref, s_ref, l_ref, o_ref):
        me = p_ref[0]
        acc = None
        for j in range(N_CHIPS):
            t = jnp.maximum(jnp.bitwise_xor(me, j) - 1, 0)
            term = jnp.where(me == j, s_ref[...], l_ref[t]).astype(F32)
            acc = term if acc is None else acc + term
        o_ref[...] = acc

    grid_spec = pltpu.PrefetchScalarGridSpec(
        num_scalar_prefetch=1, grid=(rh // tr,),
        in_specs=[pl.BlockSpec((None, tr, C), lambda i, p: (p[0], i, 0)),
                  pl.BlockSpec((3, tr, C), lambda i, p: (0, i, 0))],
        out_specs=pl.BlockSpec((None, tr, C), lambda i, p: (p[1], i, 0)),
    )
    return pl.pallas_call(
        body, name=name, grid_spec=grid_spec, out_shape=jax.ShapeDtypeStruct((2, rh, C), F32),
        compiler_params=_params(("parallel",)),
    )(chip_core, s, land)


def _rs_send_halves(grads, tag):
    lands = [lax.empty((g.shape[0], g.shape[1] // 2, g.shape[2]), g.dtype) for g in grads]
    return _Exchange(list(grads) + lands, _plan_other_halves, len(grads), f"rs_halves_{tag}")


def _rs_send_chunks(ex, after, core_idx, tag):
    bufs = ex.wait(after)
    n = len(bufs) // 2
    sums = [_add_half(g, r, core_idx, f"rs_add_{tag}_{i}") for i, (g, r) in enumerate(zip(bufs[:n], bufs[n:]))]
    lands = [lax.empty((3,) + s.shape[1:], s.dtype) for s in sums]
    return _Exchange(sums + lands, _plan_chunks, 3 * n, f"rs_chunks_{tag}")


def _rs_send_share(ex, after, chip_core, tag):
    bufs = ex.wait(after)
    n = len(bufs) // 2
    fulls = [_sum_chips(s, l, chip_core, f"rs_sum_{tag}_{i}") for i, (s, l) in enumerate(zip(bufs[:n], bufs[n:]))]
    return _Exchange(fulls, _plan_share, n, f"rs_share_{tag}")


def _rs_finish(ex, after):
    return [b.reshape(2 * b.shape[1], b.shape[2]) for b in ex.wait(after)]


def _ffn_forward(h, gain, sc, sh, gate, wg, wu, wd, tag):
    n = _norm_mod(h, gain, sc, sh, f"{tag}_norm")
    ga, up, act = _ffn_gate_up(n, wg, wu, f"{tag}_gate_up")
    h_out, f = _mm_residual(act, wd, h, gate, 0.5, f"{tag}_down")
    return h_out, (h, n, ga, up, act, f)


def _ffn_backward(dh_out, saved, gain, sc, gate, wg, wu, wd, core_idx, tag, last=False):
    h, n, ga, up, act, f = saved
    df, d_gate = _gate_bwd(dh_out, f, gate, 0.5, f"{tag}_gate_bwd")
    (dwd,) = _wgrad_chunk_lhs([act], df, f"{tag}_dwd")
    ex_d = _rs_send_halves([dwd], f"{tag}_d")
    dga, dup = _ffn_dact(df, wd, ga, up, f"{tag}_dact", deps=[ex_d.token])
    ex_d = _rs_send_chunks(ex_d, [dga], core_idx, f"{tag}_d")
    dwg, dwu = _wgrad_chunk_lhs([dga, dup], n, f"{tag}_dwgu", deps=[ex_d.token])
    ex_gu = _rs_send_halves([dwg, dwu], f"{tag}_gu")
    dn = _mm_reduce([dga, dup], [wg, wu], True, True, f"{tag}_dn", deps=[ex_gu.token])
    if not last:
        ex_gu = _rs_send_chunks(ex_gu, [dn], core_idx, f"{tag}_gu")
    dh_in, d_sh, d_sc, d_gain = _norm_mod_bwd(dn, h, gain, sc, dh_out, f"{tag}_norm_bwd", deps=[ex_gu.token])
    return dh_in, (ex_d, ex_gu), (d_sh, d_sc, d_gate, d_gain)


def _pad_cols(v, n):
    return jnp.pad(v, ((0, 0), (0, n - v.shape[1])))


def _mixer_forward(h1, mix_norm_g, sc2, sh2, gt2, win, wout, conv_w, conv_dw_b, conv_ln_g, conv_ln_b, attn_out_g,
                   conv_out_g):
    S, D = h1.shape
    n2 = _norm_mod(h1, mix_norm_g, sc2, sh2, "mix_norm")
    proj = _mm_cols(n2, win, "mix_in")
    cos, sin = _rope_tables(S)
    q, k, v = _qkv_rope(proj, cos, sin, "qkv_rope")
    attn, lse = _attn_fwd(q, k, v, "attn_fwd")
    u1, y = _mixer_merge(proj, attn, conv_w, conv_dw_b, conv_ln_g, conv_ln_b, attn_out_g, conv_out_g, "mix_merge")
    h2, mo = _mm_residual(y[None], wout.reshape(1, D, D), h1, gt2, 1.0, "mix_out")
    return h2, (h1, n2, proj, cos, sin, q, k, v, attn, lse, u1, y, mo)


def _mixer_backward(dh2, saved, mix_norm_g, sc2, gt2, win, wout, conv_w, conv_ln_g, conv_ln_b, attn_out_g,
                    conv_out_g, core_idx):
    h1, n2, proj, cos, sin, q, k, v, attn, lse, u1, y, mo = saved
    S, D = h1.shape
    dmo, d_gt2 = _gate_bwd(dh2, mo, gt2, 1.0, "mix_gate_bwd")
    (dwout,) = _wgrad_chunk_lhs([y[None]], dmo, "mix_dwout")
    dy = _mm_nt(dmo, wout.reshape(D, D), "mix_dy")
    dattn, delta, d_attn_g = _attn_out_bwd(dy, attn, attn_out_g, "attn_out_bwd")
    dq, dk, dv = _attn_bwd(q, k, v, dattn, lse, delta, "attn_bwd")
    du1, d_gco, d_lng, d_lnb, d_cb = _conv_bwd_norms(dy, u1, conv_ln_g, conv_ln_b, conv_out_g, "conv_bwd_norms")
    dproj, d_cw = _dproj(du1, proj, conv_w, dq, dk, dv, cos, sin, "mix_dproj")
    (dwin,) = _wgrad_chunk_rhs(n2, [dproj], N_CHIPS, False, "mix_dwin")
    ex = _rs_send_halves([dwin, dwout.reshape(N_CHIPS, D // N_CHIPS, D)], "mix")
    dn2 = _mm_reduce([dproj], [win], False, False, "mix_dn", deps=[ex.token])
    ex = _rs_send_chunks(ex, [dn2], core_idx, "mix")
    dh1, d_sh2, d_sc2, d_gain2 = _norm_mod_bwd(dn2, h1, mix_norm_g, sc2, dh2, "mix_norm_bwd", deps=[ex.token])
    small = (d_sh2, d_sc2, d_gt2, d_gain2, d_cb, d_lng, d_lnb, d_attn_g, d_gco, d_cw)
    return dh1, ex, small


def kernel(x, c, w_ada, b_ada, ffn1_norm_g, ffn1_w_gate, ffn1_w_up, ffn1_w_down, mix_norm_g, w_in, conv_dw_w, conv_dw_b, conv_ln_g, conv_ln_b, attn_out_g, conv_out_g, w_out, ffn2_norm_g, ffn2_w_gate, ffn2_w_up, ffn2_w_down, final_norm_g, loss_target, m_w_ada, m_b_ada, m_ffn1_norm_g, m_ffn1_w_gate, m_ffn1_w_up, m_ffn1_w_down, m_mix_norm_g, m_w_in, m_conv_dw_w, m_conv_dw_b, m_conv_ln_g, m_conv_ln_b, m_attn_out_g, m_conv_out_g, m_w_out, m_ffn2_norm_g, m_ffn2_w_gate, m_ffn2_w_up, m_ffn2_w_down, m_final_norm_g, v_w_ada, v_b_ada, v_ffn1_norm_g, v_ffn1_w_gate, v_ffn1_w_up, v_ffn1_w_down, v_mix_norm_g, v_w_in, v_conv_dw_w, v_conv_dw_b, v_conv_ln_g, v_conv_ln_b, v_attn_out_g, v_conv_out_g, v_w_out, v_ffn2_norm_g, v_ffn2_w_gate, v_ffn2_w_up, v_ffn2_w_down, v_final_norm_g):
    S, D = x.shape[1], x.shape[2]
    mx, my, mc = _coords()
    chip = 2 * mx + my
    dev = 4 * mx + 2 * my + mc
    chip_idx = chip.astype(jnp.int32).reshape(1)
    core_idx = mc.astype(jnp.int32).reshape(1)
    chip_core = jnp.stack([chip, mc]).astype(jnp.int32)
    h0 = x[0]
    target = loss_target[0]

    ncw = CONV_KERNEL * 128
    n0 = -(-(D + ncw) // 1024) * 1024
    pk0 = _pad_cols(jnp.concatenate([c.reshape(1, D), conv_dw_w.reshape(1, ncw)], axis=1), n0)
    g0 = _all_gather8(pk0.reshape(8, n0 // 8), "gather_c").reshape(N_DEV, n0)
    c_all = g0[:, :D]
    conv_w = jnp.concatenate([g0[2 * kc, D:D + ncw].reshape(CONV_KERNEL, 128) for kc in range(N_CHIPS)], axis=1)
    conv_w = jnp.pad(conv_w, ((0, HALO - CONV_KERNEL), (0, 0)))
    nmod = w_ada.shape[2]
    b_shard = lax.dynamic_slice(b_ada, (0, chip * nmod), (1, nmod))
    mod_part = _ada_fwd(c_all, w_ada[0], b_shard, "ada_fwd")
    g1 = _all_gather8(mod_part, "gather_mod")
    mod_all = jnp.concatenate([g1[16 * kc:16 * kc + 8] for kc in range(N_CHIPS)], axis=1)
    mod = lax.dynamic_slice(mod_all, (dev, 0), (1, 9 * D))
    sh1, sc1, gt1, sh2, sc2, gt2, sh3, sc3, gt3 = [mod[:, i * D:(i + 1) * D] for i in range(9)]

    def gather_start(ws, tag, dep):
        slots = [_cast_place(w, chip_idx, f"cast_{tag}_{i}", deps=[dep]) for i, w in enumerate(ws)]
        return _Exchange(slots, _plan_gather, 3 * len(ws), f"gather_{tag}")

    def swap_start(ex, after, tag):
        return _Exchange(ex.wait(after), _plan_swap, 3 * len(ex.bufs), f"swap_{tag}")

    ex_w1 = gather_start([ffn1_w_gate[0].T, ffn1_w_up[0].T, ffn1_w_down[0]], "ffn1", g1)
    ex_wm = gather_start([w_in[0], w_out[0]], "mix", ex_w1.token)
    ex_w2 = gather_start([ffn2_w_gate[0].T, ffn2_w_up[0].T, ffn2_w_down[0]], "ffn2", ex_wm.token)

    n1 = _norm_mod(h0, ffn1_norm_g, sc1, sh1, "ffn1_norm")
    wg1, wu1, wd1 = swap_start(ex_w1, [n1, ex_w2.token], "ffn1").wait([])
    ga1, up1, act1 = _ffn_gate_up(n1, wg1, wu1, "ffn1_gate_up")
    ex_wm = swap_start(ex_wm, [act1], "mix")
    h1, f1 = _mm_residual(act1, wd1, h0, gt1, 0.5, "ffn1_down")
    saved1 = (h0, n1, ga1, up1, act1, f1)
    win, wout = ex_wm.wait([h1])
    ex_w2 = swap_start(ex_w2, [h1], "ffn2")
    h2, saved2 = _mixer_forward(h1, mix_norm_g, sc2, sh2, gt2, win, wout, conv_w, conv_dw_b, conv_ln_g, conv_ln_b,
                                attn_out_g, conv_out_g)
    wg2, wu2, wd2 = ex_w2.wait([h2])
    h3, saved3 = _ffn_forward(h2, ffn2_norm_g, sc3, sh3, gt3, wg2, wu2, wd2, "ffn2")
    loss_part, dh3, d_final_g = _loss_head(h3, final_norm_g.reshape(1, D), target, "loss_head")

    dh2, (ex_d2, ex_gu2), (d_sh3, d_sc3, d_gt3, d_gain3) = _ffn_backward(
        dh3, saved3, ffn2_norm_g, sc3, gt3, wg2, wu2, wd2, core_idx, "ffn2")
    dh1, ex_mix, small_mix = _mixer_backward(
        dh2, saved2, mix_norm_g, sc2, gt2, win, wout, conv_w, conv_ln_g, conv_ln_b, attn_out_g, conv_out_g, core_idx)
    d_sh2, d_sc2, d_gt2, d_gain2, d_cb, d_lng, d_lnb, d_attn_g, d_gco, d_cw = small_mix
    dh0, (ex_d1, ex_gu1), (d_sh1, d_sc1, d_gt1, d_gain1) = _ffn_backward(
        dh1, saved1, ffn1_norm_g, sc1, gt1, wg1, wu1, wd1, core_idx, "ffn1", last=True)

    dmod = jnp.concatenate([d_sh1, d_sc1, d_gt1, d_sh2, d_sc2, d_gt2, d_sh3, d_sc3, d_gt3], axis=1)
    small = [d_gain1, d_gain2, d_gain3, d_final_g, d_cb, d_lng, d_lnb, d_attn_g, d_gco,
             d_cw[:CONV_KERNEL].reshape(1, CONV_KERNEL * CONV_WIDTH), loss_part]
    pk1 = jnp.concatenate([dmod] + small, axis=1)
    n1_ = -(-pk1.shape[1] // 1024) * 1024
    gathered = _all_gather8(_pad_cols(pk1, n1_).reshape(8, n1_ // 8), "gather_small").reshape(N_DEV, n1_)
    ex_gu1 = _rs_send_chunks(ex_gu1, [gathered], core_idx, "ffn1_gu")
    tot = _sum_rows(gathered, "sum_small")
    off = [0]

    def take(nel):
        out = tot[:, off[0]:off[0] + nel]
        off[0] += nel
        return out

    g_b_ada = take(9 * D)
    g_ffn1_norm, g_mix_norm, g_ffn2_norm, g_final = take(D), take(D), take(D), take(D)
    g_cb, g_lng, g_lnb, g_attn_g, g_gco = take(512), take(512), take(512), take(512), take(512)
    g_cw_full = take(CONV_KERNEL * CONV_WIDTH).reshape(CONV_KERNEL, CONV_WIDTH)
    loss = take(128)[0, 0]
    g_cw = lax.dynamic_slice(g_cw_full, (0, chip * 128), (CONV_KERNEL, 128))

    dmod_shard = lax.dynamic_slice(gathered[:, :9 * D], (0, chip * nmod), (N_DEV, nmod))
    dmod16 = jnp.pad(dmod_shard, ((0, N_DEV), (0, 0)))
    c_t16 = jnp.pad(c_all.T, ((0, 0), (0, N_DEV)))
    g_w_ada = _ada_wgrad(c_t16, dmod16, "ada_wgrad")

    names = ["w_ada", "b_ada", "ffn1_norm_g", "ffn1_w_gate", "ffn1_w_up", "ffn1_w_down", "mix_norm_g", "w_in",
             "conv_dw_w", "conv_dw_b", "conv_ln_g", "conv_ln_b", "attn_out_g", "conv_out_g", "w_out", "ffn2_norm_g",
             "ffn2_w_gate", "ffn2_w_up", "ffn2_w_down", "final_norm_g"]
    weights = dict(zip(names, [w_ada, b_ada, ffn1_norm_g, ffn1_w_gate, ffn1_w_up, ffn1_w_down, mix_norm_g, w_in,
                               conv_dw_w, conv_dw_b, conv_ln_g, conv_ln_b, attn_out_g, conv_out_g, w_out,
                               ffn2_norm_g, ffn2_w_gate, ffn2_w_up, ffn2_w_down, final_norm_g]))
    ms = dict(zip(names, [m_w_ada, m_b_ada, m_ffn1_norm_g, m_ffn1_w_gate, m_ffn1_w_up, m_ffn1_w_down, m_mix_norm_g,
                          m_w_in, m_conv_dw_w, m_conv_dw_b, m_conv_ln_g, m_conv_ln_b, m_attn_out_g, m_conv_out_g,
                          m_w_out, m_ffn2_norm_g, m_ffn2_w_gate, m_ffn2_w_up, m_ffn2_w_down, m_final_norm_g]))
    vs = dict(zip(names, [v_w_ada, v_b_ada, v_ffn1_norm_g, v_ffn1_w_gate, v_ffn1_w_up, v_ffn1_w_down, v_mix_norm_g,
                          v_w_in, v_conv_dw_w, v_conv_dw_b, v_conv_ln_g, v_conv_ln_b, v_attn_out_g, v_conv_out_g,
                          v_w_out, v_ffn2_norm_g, v_ffn2_w_gate, v_ffn2_w_up, v_ffn2_w_down, v_final_norm_g]))
    grads, deltas, new_ms, new_vs = {}, {}, {}, {}

    def adamw_big(nm, g2d, deps=(), transposed=False):
        shape = weights[nm].shape
        two_d = (shape[-2], shape[-1])

        def view(t):
            return t.reshape(two_d).T if transposed else t.reshape(two_d)

        d_, m_, v_ = _adamw(view(weights[nm]), g2d, view(ms[nm]), view(vs[nm]), f"adamw_{nm}", deps=deps)
        grads[nm], deltas[nm], new_ms[nm], new_vs[nm] = (
            (t.T if transposed else t).reshape(shape) for t in (g2d, d_, m_, v_))
        return d_

    d_ada = adamw_big("w_ada", g_w_ada, deps=[ex_gu1.token])
    small_grads = {"b_ada": g_b_ada, "ffn1_norm_g": g_ffn1_norm, "mix_norm_g": g_mix_norm, "conv_dw_w": g_cw,
                   "conv_dw_b": g_cb, "conv_ln_g": g_lng, "conv_ln_b": g_lnb, "attn_out_g": g_attn_g,
                   "conv_out_g": g_gco, "ffn2_norm_g": g_ffn2_norm, "final_norm_g": g_final}
    small_names = [nm for nm in names if nm in small_grads]

    def pack_small(arrs):
        flat = jnp.concatenate([arrs[nm].reshape(1, -1) for nm in small_names], axis=1)
        npad = -(-flat.shape[1] // 1024) * 1024
        return _pad_cols(flat, npad).reshape(8, npad // 8)

    d_s, m_s, v_s = _adamw(pack_small(weights), pack_small(small_grads), pack_small(ms), pack_small(vs),
                           "adamw_small")
    pos = 0
    for nm in small_names:
        shape, nel = weights[nm].shape, weights[nm].size
        grads[nm] = small_grads[nm].reshape(shape)
        deltas[nm], new_ms[nm], new_vs[nm] = (t.reshape(1, -1)[:, pos:pos + nel].reshape(shape)
                                              for t in (d_s, m_s, v_s))
        pos += nel

    ex_d2 = _rs_send_share(ex_d2, [d_ada, d_s], chip_core, "ffn2_d")
    ex_gu2 = _rs_send_share(ex_gu2, [ex_d2.token], chip_core, "ffn2_gu")
    ex_mix = _rs_send_share(ex_mix, [ex_gu2.token], chip_core, "mix")
    ex_d1 = _rs_send_share(ex_d1, [ex_mix.token], chip_core, "ffn1_d")
    (g_wd2,) = _rs_finish(ex_d2, [ex_d1.token])
    last = [adamw_big("ffn2_w_down", g_wd2)]
    g_wg2, g_wu2 = _rs_finish(ex_gu2, last)
    last = [adamw_big("ffn2_w_gate", g_wg2, transposed=True), adamw_big("ffn2_w_up", g_wu2, transposed=True)]
    g_win, g_wout = _rs_finish(ex_mix, last)
    last = [adamw_big("w_in", g_win), adamw_big("w_out", g_wout)]
    (g_wd1,) = _rs_finish(ex_d1, last)
    last = [adamw_big("ffn1_w_down", g_wd1)]
    ex_gu1 = _rs_send_share(ex_gu1, last, chip_core, "ffn1_gu")
    g_wg1, g_wu1 = _rs_finish(ex_gu1, [])
    adamw_big("ffn1_w_gate", g_wg1, transposed=True)
    adamw_big("ffn1_w_up", g_wu1, transposed=True)

    return (loss, dh0[None], *[grads[nm] for nm in names], *[deltas[nm] for nm in names],
            *[new_ms[nm] for nm in names], *[new_vs[nm] for nm in names])
```

```python
import jax
import jax.numpy as jnp
import numpy as np
from jax import lax
from jax.experimental import pallas as pl
from jax.experimental.pallas import tpu as pltpu

F32 = jnp.float32
BF16 = jnp.bfloat16
MESH = pl.DeviceIdType.MESH

RMS_EPS = 1e-6
LN_EPS = 1e-5
HEAD_DIM = 64
ATTN_WIDTH = 512
CONV_WIDTH = 512
ATTN_BLOCK = 128
DILATIONS = (1, 4, 16)
SUPER_ROWS = ATTN_BLOCK * 16
ROPE_THETA = 10000.0
CONV_KERNEL = 31
HALO = 32
N_CHIPS = 4
N_DEV = 8
ADAM_LR, ADAM_B1, ADAM_B2, ADAM_EPS, ADAM_WD, ADAM_STEP = 0.001, 0.9, 0.999, 1e-08, 0.01, 10
VMEM_LIMIT_BYTES = 48 * 1024 * 1024
NEG = -1e30

NT = (((1,), (1,)), ((), ()))
TN = (((0,), (0,)), ((), ()))

ANY_SPEC = pl.BlockSpec(memory_space=pl.ANY)
HBM_SPEC = pl.BlockSpec(memory_space=pltpu.HBM)
SEM_SPEC = pl.BlockSpec(memory_space=pltpu.SEMAPHORE)
DATAFLOW = pltpu.SideEffectType.DATAFLOW_SIDE_EFFECTING
XY_FLIPS = ((0, 1), (1, 0), (1, 1))


def _params(sem=None):
    return pltpu.CompilerParams(dimension_semantics=sem, vmem_limit_bytes=VMEM_LIMIT_BYTES)


def _row_tile(rows, want):
    t = min(rows, want)
    assert rows % t == 0
    return t


def _sigmoid(x):
    return 1.0 / (1.0 + jnp.exp(-x))


def _deps(deps):
    return [d for d in deps if d is not None]


def _blind_to(body, n_in, n_dep):
    def wrapped(*refs):
        return body(*refs[:n_in], *refs[n_in + n_dep:])
    return wrapped


def _vec_spec(d, ngrid):
    if ngrid == 1:
        return pl.BlockSpec((1, d), lambda i: (0, 0))
    return pl.BlockSpec((1, d), lambda i, j: (0, 0))


def _norm_mod(h, gain, sc, sh, name):
    S, D = h.shape
    tr = _row_tile(S, 512)

    def body(h_ref, g_ref, sc_ref, sh_ref, n_ref):
        x = h_ref[...]
        r = lax.rsqrt(jnp.mean(x * x, axis=-1, keepdims=True) + RMS_EPS)
        y = (x * r) * g_ref[...]
        n_ref[...] = (y * (1.0 + sc_ref[...]) + sh_ref[...]).astype(BF16)

    row = pl.BlockSpec((tr, D), lambda i: (i, 0))
    return pl.pallas_call(
        body, name=name, grid=(S // tr,),
        in_specs=[row, _vec_spec(D, 1), _vec_spec(D, 1), _vec_spec(D, 1)],
        out_specs=row, out_shape=jax.ShapeDtypeStruct((S, D), BF16),
        compiler_params=_params(("parallel",)),
    )(h, gain, sc, sh)


def _norm_mod_bwd(dn, h_in, gain, sc, dh_out, name, deps=()):
    S, D = h_in.shape
    tr = _row_tile(S, 512)
    deps = _deps(deps)

    def body(dn_ref, h_ref, g_ref, sc_ref, dho_ref, dh_ref, dsh_ref, dsc_ref, dg_ref):
        @pl.when(pl.program_id(0) == 0)
        def _():
            dsh_ref[...] = jnp.zeros_like(dsh_ref)
            dsc_ref[...] = jnp.zeros_like(dsc_ref)
            dg_ref[...] = jnp.zeros_like(dg_ref)

        x = h_ref[...]
        dn_ = dn_ref[...]
        g = g_ref[...]
        one_sc = 1.0 + sc_ref[...]
        r = lax.rsqrt(jnp.mean(x * x, axis=-1, keepdims=True) + RMS_EPS)
        xh = x * r
        dsh_ref[...] += jnp.sum(dn_, axis=0, keepdims=True)
        dsc_ref[...] += jnp.sum(dn_ * (xh * g), axis=0, keepdims=True)
        dg_ref[...] += jnp.sum(dn_ * one_sc * xh, axis=0, keepdims=True)
        dxh = dn_ * (g * one_sc)
        dh_ref[...] = dho_ref[...] + r * (dxh - xh * jnp.mean(dxh * xh, axis=-1, keepdims=True))

    row = pl.BlockSpec((tr, D), lambda i: (i, 0))
    vec = _vec_spec(D, 1)
    return pl.pallas_call(
        _blind_to(body, 5, len(deps)), name=name, grid=(S // tr,),
        in_specs=[row, row, vec, vec, row] + [ANY_SPEC] * len(deps),
        out_specs=[row, vec, vec, vec],
        out_shape=[jax.ShapeDtypeStruct((S, D), F32)] + [jax.ShapeDtypeStruct((1, D), F32)] * 3,
        compiler_params=_params(("arbitrary",)),
    )(dn, h_in, gain, sc, dh_out, *deps)


def _gate_bwd(dh, f, gvec, coef, name):
    S, D = dh.shape
    tr = _row_tile(S, 512)

    def body(dh_ref, f_ref, g_ref, df_ref, dg_ref):
        @pl.when(pl.program_id(0) == 0)
        def _():
            dg_ref[...] = jnp.zeros_like(dg_ref)

        dh_ = dh_ref[...]
        df_ref[...] = ((coef * g_ref[...]) * dh_).astype(BF16)
        dg_ref[...] += jnp.sum(coef * dh_ * f_ref[...].astype(F32), axis=0, keepdims=True)

    row = pl.BlockSpec((tr, D), lambda i: (i, 0))
    vec = _vec_spec(D, 1)
    return pl.pallas_call(
        body, name=name, grid=(S // tr,),
        in_specs=[row, row, vec], out_specs=[row, vec],
        out_shape=[jax.ShapeDtypeStruct((S, D), BF16), jax.ShapeDtypeStruct((1, D), F32)],
        compiler_params=_params(("arbitrary",)),
    )(dh, f, gvec)


def _loss_head(h, gain, target, name):
    S, D = h.shape
    tr = _row_tile(S, 512)

    def body(h_ref, g_ref, t_ref, loss_ref, dh_ref, dg_ref):
        @pl.when(pl.program_id(0) == 0)
        def _():
            loss_ref[...] = jnp.zeros_like(loss_ref)
            dg_ref[...] = jnp.zeros_like(dg_ref)

        x = h_ref[...]
        g = g_ref[...]
        r = lax.rsqrt(jnp.mean(x * x, axis=-1, keepdims=True) + RMS_EPS)
        xh = x * r
        err = xh * g - t_ref[...]
        part = 0.5 * jnp.sum(jnp.mean(err * err, axis=-1, keepdims=True), axis=0, keepdims=True)
        loss_ref[...] += jnp.broadcast_to(part, loss_ref.shape)
        dy = err * (1.0 / D)
        dg_ref[...] += jnp.sum(dy * xh, axis=0, keepdims=True)
        dxh = dy * g
        dh_ref[...] = r * (dxh - xh * jnp.mean(dxh * xh, axis=-1, keepdims=True))

    row = pl.BlockSpec((tr, D), lambda i: (i, 0))
    vec = _vec_spec(D, 1)
    return pl.pallas_call(
        body, name=name, grid=(S // tr,),
        in_specs=[row, vec, row],
        out_specs=[pl.BlockSpec((1, 128), lambda i: (0, 0)), row, vec],
        out_shape=[jax.ShapeDtypeStruct((1, 128), F32), jax.ShapeDtypeStruct((S, D), F32),
                   jax.ShapeDtypeStruct((1, D), F32)],
        compiler_params=_params(("arbitrary",)),
    )(h, gain, target)


def _ffn_gate_up(n, wg_t, wu_t, name):
    S, D = n.shape
    nk, w, _ = wg_t.shape
    tm = _row_tile(S, 512)

    def body(n_ref, wg_ref, wu_ref, ga_ref, up_ref, act_ref):
        x = n_ref[...]
        ga = lax.dot_general(x, wg_ref[...], NT, preferred_element_type=F32)
        up = lax.dot_general(x, wu_ref[...], NT, preferred_element_type=F32)
        ga_ref[...] = ga.astype(BF16)
        up_ref[...] = up.astype(BF16)
        act_ref[...] = ((ga * _sigmoid(ga)) * up).astype(BF16)

    wspec = pl.BlockSpec((None, w, D), lambda k, m: (k, 0, 0))
    ospec = pl.BlockSpec((None, tm, w), lambda k, m: (k, m, 0))
    out = jax.ShapeDtypeStruct((nk, S, w), BF16)
    return pl.pallas_call(
        body, name=name, grid=(nk, S // tm),
        in_specs=[pl.BlockSpec((tm, D), lambda k, m: (m, 0)), wspec, wspec],
        out_specs=[ospec, ospec, ospec], out_shape=[out, out, out],
        compiler_params=_params(("parallel", "parallel")),
    )(n, wg_t, wu_t)


def _mm_residual(lhs, w, h_in, gvec, coef, name):
    nk, S, kc = lhs.shape
    D = w.shape[2]
    tm = _row_tile(S, 512)

    def body(l_ref, w_ref, h_ref, g_ref, ho_ref, f_ref, acc_ref):
        k = pl.program_id(1)

        @pl.when(k == 0)
        def _():
            acc_ref[...] = jnp.zeros_like(acc_ref)

        acc_ref[...] += jnp.dot(l_ref[...], w_ref[...], preferred_element_type=F32)

        @pl.when(k == nk - 1)
        def _():
            f = acc_ref[...]
            f_ref[...] = f.astype(BF16)
            ho_ref[...] = h_ref[...] + (coef * g_ref[...]) * f

    row = pl.BlockSpec((tm, D), lambda m, k: (m, 0))
    return pl.pallas_call(
        body, name=name, grid=(S // tm, nk),
        in_specs=[pl.BlockSpec((None, tm, kc), lambda m, k: (k, m, 0)),
                  pl.BlockSpec((None, kc, D), lambda m, k: (k, 0, 0)), row, _vec_spec(D, 2)],
        out_specs=[row, row],
        out_shape=[jax.ShapeDtypeStruct((S, D), F32), jax.ShapeDtypeStruct((S, D), BF16)],
        scratch_shapes=[pltpu.VMEM((tm, D), F32)],
        compiler_params=_params(("parallel", "arbitrary")),
    )(lhs, w, h_in, gvec)


def _mm_cols(n, w, name):
    S, D = n.shape
    nk, _, wd = w.shape
    assert wd % 128 == 0
    tm = _row_tile(S, 512)

    def body(n_ref, w_ref, o_ref):
        o_ref[...] = jnp.dot(n_ref[...], w_ref[...], preferred_element_type=F32)

    return pl.pallas_call(
        body, name=name, grid=(nk, S // tm),
        in_specs=[pl.BlockSpec((tm, D), lambda k, m: (m, 0)), pl.BlockSpec((None, D, wd), lambda k, m: (k, 0, 0))],
        out_specs=pl.BlockSpec((tm, wd), lambda k, m: (m, k)),
        out_shape=jax.ShapeDtypeStruct((S, nk * wd), F32),
        compiler_params=_params(("parallel", "parallel")),
    )(n, w)


def _ffn_dact(df, wd, ga, up, name, deps=()):
    S, D = df.shape
    nk, w, _ = wd.shape
    tm = _row_tile(S, 512)
    deps = _deps(deps)

    def body(df_ref, wd_ref, ga_ref, up_ref, dga_ref, dup_ref):
        dact = lax.dot_general(df_ref[...], wd_ref[...], NT, preferred_element_type=F32)
        ga_ = ga_ref[...].astype(F32)
        up_ = up_ref[...].astype(F32)
        sig = _sigmoid(ga_)
        dga_ref[...] = (dact * up_ * (sig * (1.0 + ga_ * (1.0 - sig)))).astype(BF16)
        dup_ref[...] = (dact * (ga_ * sig)).astype(BF16)

    cspec = pl.BlockSpec((None, tm, w), lambda k, m: (k, m, 0))
    out = jax.ShapeDtypeStruct((nk, S, w), BF16)
    return pl.pallas_call(
        _blind_to(body, 4, len(deps)), name=name, grid=(nk, S // tm),
        in_specs=[pl.BlockSpec((tm, D), lambda k, m: (m, 0)), pl.BlockSpec((None, w, D), lambda k, m: (k, 0, 0)),
                  cspec, cspec] + [ANY_SPEC] * len(deps),
        out_specs=[cspec, cspec], out_shape=[out, out],
        compiler_params=_params(("parallel", "parallel")),
    )(df, wd, ga, up, *deps)


def _mm_nt(d, w, name):
    S, K = d.shape
    N = w.shape[0]
    tm = _row_tile(S, 512)

    def body(d_ref, w_ref, o_ref):
        o_ref[...] = lax.dot_general(d_ref[...], w_ref[...], NT, preferred_element_type=F32)

    return pl.pallas_call(
        body, name=name, grid=(S // tm,),
        in_specs=[pl.BlockSpec((tm, K), lambda m: (m, 0)), pl.BlockSpec((N, K), lambda m: (0, 0))],
        out_specs=pl.BlockSpec((tm, N), lambda m: (m, 0)),
        out_shape=jax.ShapeDtypeStruct((S, N), F32),
        compiler_params=_params(("parallel",)),
    )(d, w)


def _mm_reduce(lhs_list, w_list, chunked3d, w_is_kd, name, deps=()):
    nk = w_list[0].shape[0]
    kc, D = w_list[0].shape[1:] if w_is_kd else w_list[0].shape[:0:-1]
    S = lhs_list[0].shape[1] if chunked3d else lhs_list[0].shape[0]
    tm = _row_tile(S, 512)
    npair = len(lhs_list)
    deps = _deps(deps)

    def body(*refs):
        l_refs, w_refs = refs[:npair], refs[npair:2 * npair]
        o_ref, acc_ref = refs[2 * npair], refs[2 * npair + 1]
        k = pl.program_id(1)

        @pl.when(k == 0)
        def _():
            acc_ref[...] = jnp.zeros_like(acc_ref)

        for l_ref, w_ref in zip(l_refs, w_refs):
            if w_is_kd:
                acc_ref[...] += jnp.dot(l_ref[...], w_ref[...], preferred_element_type=F32)
            else:
                acc_ref[...] += lax.dot_general(l_ref[...], w_ref[...], NT, preferred_element_type=F32)

        @pl.when(k == nk - 1)
        def _():
            o_ref[...] = acc_ref[...]

    if chunked3d:
        lspec = pl.BlockSpec((None, tm, kc), lambda m, k: (k, m, 0))
    else:
        lspec = pl.BlockSpec((tm, kc), lambda m, k: (m, k))
    wspec = pl.BlockSpec((None,) + tuple(w_list[0].shape[1:]), lambda m, k: (k, 0, 0))
    return pl.pallas_call(
        _blind_to(body, 2 * npair, len(deps)), name=name, grid=(S // tm, nk),
        in_specs=[lspec] * npair + [wspec] * npair + [ANY_SPEC] * len(deps),
        out_specs=pl.BlockSpec((tm, D), lambda m, k: (m, 0)),
        out_shape=jax.ShapeDtypeStruct((S, D), F32),
        scratch_shapes=[pltpu.VMEM((tm, D), F32)],
        compiler_params=_params(("parallel", "arbitrary")),
    )(*lhs_list, *w_list, *deps)


def _wgrad_chunk_lhs(lhs_list, rhs, name, deps=()):
    nk, S, w = lhs_list[0].shape
    D = rhs.shape[1]
    ts = _row_tile(S, 512)
    ns = S // ts
    nl = len(lhs_list)
    deps = _deps(deps)

    def body(*refs):
        l_refs, r_ref = refs[:nl], refs[nl]
        o_refs, acc_refs = refs[nl + 1:2 * nl + 1], refs[2 * nl + 1:]
        s = pl.program_id(1)

        @pl.when(s == 0)
        def _():
            for acc_ref in acc_refs:
                acc_ref[...] = jnp.zeros_like(acc_ref)

        x = r_ref[...]
        for l_ref, acc_ref in zip(l_refs, acc_refs):
            acc_ref[...] += lax.dot_general(l_ref[...], x, TN, preferred_element_type=F32)

        @pl.when(s == ns - 1)
        def _():
            for o_ref, acc_ref in zip(o_refs, acc_refs):
                o_ref[...] = acc_ref[...]

    return pl.pallas_call(
        _blind_to(body, nl + 1, len(deps)), name=name, grid=(nk, ns),
        in_specs=[pl.BlockSpec((None, ts, w), lambda k, s: (k, s, 0))] * nl
        + [pl.BlockSpec((ts, D), lambda k, s: (s, 0))] + [ANY_SPEC] * len(deps),
        out_specs=[pl.BlockSpec((None, w, D), lambda k, s: (k, 0, 0))] * nl,
        out_shape=[jax.ShapeDtypeStruct((nk, w, D), F32)] * nl,
        scratch_shapes=[pltpu.VMEM((w, D), F32)] * nl,
        compiler_params=_params(("parallel", "arbitrary")),
    )(*lhs_list, rhs, *deps)


def _wgrad_chunk_rhs(lhs, rhs_list, nk, chunked3d, name, deps=()):
    S, D = lhs.shape
    w = rhs_list[0].shape[2] if chunked3d else rhs_list[0].shape[1] // nk
    ts = _row_tile(S, 512)
    ns = S // ts
    nr = len(rhs_list)
    deps = _deps(deps)

    def body(*refs):
        l_ref, r_refs = refs[0], refs[1:1 + nr]
        o_refs, acc_refs = refs[1 + nr:1 + 2 * nr], refs[1 + 2 * nr:]
        s = pl.program_id(1)

        @pl.when(s == 0)
        def _():
            for acc_ref in acc_refs:
                acc_ref[...] = jnp.zeros_like(acc_ref)

        x = l_ref[...]
        for r_ref, acc_ref in zip(r_refs, acc_refs):
            acc_ref[...] += lax.dot_general(x, r_ref[...], TN, preferred_element_type=F32)

        @pl.when(s == ns - 1)
        def _():
            for o_ref, acc_ref in zip(o_refs, acc_refs):
                o_ref[...] = acc_ref[...]

    if chunked3d:
        rspec = pl.BlockSpec((None, ts, w), lambda k, s: (k, s, 0))
    else:
        rspec = pl.BlockSpec((ts, w), lambda k, s: (s, k))
    ospec = pl.BlockSpec((None, D, w), lambda k, s: (k, 0, 0))
    return pl.pallas_call(
        _blind_to(body, 1 + nr, len(deps)), name=name, grid=(nk, ns),
        in_specs=[pl.BlockSpec((ts, D), lambda k, s: (s, 0))] + [rspec] * nr + [ANY_SPEC] * len(deps),
        out_specs=[ospec] * nr,
        out_shape=[jax.ShapeDtypeStruct((nk, D, w), F32)] * nr,
        scratch_shapes=[pltpu.VMEM((D, w), F32)] * nr,
        compiler_params=_params(("parallel", "arbitrary")),
    )(lhs, *rhs_list, *deps)


def _rope_tables(S):
    pos = np.arange(S, dtype=np.float32)
    inv_freq = (ROPE_THETA ** (-np.arange(0, HEAD_DIM, 2, dtype=np.float32) / HEAD_DIM)).astype(np.float32)
    ang = (pos[:, None] * inv_freq[None, :]).astype(np.float64)
    cos, sin = np.cos(ang).astype(np.float32), np.sin(ang).astype(np.float32)
    cos2 = np.concatenate([cos, cos, cos, cos], axis=1)
    sin2 = np.concatenate([-sin, sin, -sin, sin], axis=1)
    return jnp.asarray(cos2), jnp.asarray(sin2)


def _rotate(t, cos, sin_signed):
    half = HEAD_DIM // 2
    lane = lax.broadcasted_iota(jnp.int32, t.shape, 1)
    first = (lane % HEAD_DIM) < half
    partner = jnp.where(first, pltpu.roll(t, 128 - half, 1), pltpu.roll(t, half, 1))
    return t * cos + partner * sin_signed


def _qkv_rope(proj, cos, sin, name):
    S = proj.shape[0]
    A = ATTN_WIDTH
    tr = _row_tile(S, 512)
    nb = A // 128
    scale = HEAD_DIM ** -0.5

    def body(q_ref, k_ref, v_ref, c_ref, s_ref, qo_ref, ko_ref, vo_ref):
        c, s = c_ref[...], s_ref[...]
        qo_ref[...] = _rotate(q_ref[...], c, s) * scale
        ko_ref[...] = _rotate(k_ref[...], c, s)
        vo_ref[...] = v_ref[...]

    def col(off):
        return pl.BlockSpec((tr, 128), lambda i, j: (i, off + j))

    tab = pl.BlockSpec((tr, 128), lambda i, j: (i, 0))
    out = jax.ShapeDtypeStruct((S, A), F32)
    return pl.pallas_call(
        body, name=name, grid=(S // tr, nb),
        in_specs=[col(0), col(nb), col(2 * nb), tab, tab],
        out_specs=[col(0), col(0), col(0)], out_shape=[out, out, out],
        compiler_params=_params(("parallel", "parallel")),
    )(proj, proj, proj, cos, sin)


def _band_mask(T, has_prev):
    qi = lax.broadcasted_iota(jnp.int32, (T, 2 * T), 0)
    kj = lax.broadcasted_iota(jnp.int32, (T, 2 * T), 1)
    return ((kj < T) & (kj >= qi) & has_prev) | ((kj >= T) & (kj - T <= qi))


def _branch_blocks(rows, dilation):
    T = min(ATTN_BLOCK, rows // dilation)
    return T, rows // T


def _block_rows(base, T, dilation):
    if dilation == 1:
        return pl.ds(pl.multiple_of(base, T), T)
    return pl.ds(base, T, stride=dilation)


def _attn_fwd(q, k, v, name):
    S, A = q.shape
    sup = min(S, SUPER_ROWS)
    nd = len(DILATIONS)
    assert S % sup == 0

    def body(q_ref, k_ref, v_ref, attn_ref, lse_ref, acc_s, m_s, l_s):
        lane = lax.broadcasted_iota(jnp.int32, (1, 128), 1)
        head0 = lane < HEAD_DIM

        def supertile(st, carry):
            row0 = st * sup
            for di, dil in enumerate(DILATIONS):
                T, nblk = _branch_blocks(sup, dil)
                span = T * dil
                assert T == ATTN_BLOCK or sup == S

                def block(idx, c2, di=di, dil=dil, T=T, span=span):
                    r = idx % dil
                    loc = (idx // dil) * span + r
                    base = row0 + loc
                    rows = _block_rows(base, T, dil)
                    prev = _block_rows(jnp.maximum(base - span, r), T, dil)
                    qb = q_ref[rows, :].astype(BF16)
                    k2 = jnp.concatenate([k_ref[prev, :], k_ref[rows, :]], axis=0).astype(BF16)
                    v2 = jnp.concatenate([v_ref[prev, :], v_ref[rows, :]], axis=0).astype(BF16)
                    valid = _band_mask(T, base >= span)
                    accs, ms, ls = [], [], []
                    for hmask in (head0, jnp.logical_not(head0)):
                        qh = jnp.where(hmask, qb, jnp.zeros_like(qb))
                        s = jnp.where(valid, lax.dot_general(qh, k2, NT, preferred_element_type=F32), NEG)
                        m = jnp.max(s, axis=-1, keepdims=True)
                        p = jnp.exp(s - m)
                        ls.append(jnp.sum(p, axis=-1, keepdims=True))
                        ms.append(m)
                        accs.append(jnp.dot(p.astype(BF16), v2, preferred_element_type=F32))
                    lrows = _block_rows(di * sup + loc, T, dil)
                    acc_s[lrows, :] = jnp.where(head0, accs[0], accs[1])
                    m_s[lrows, :] = jnp.where(head0, ms[0], ms[1])
                    l_s[lrows, :] = jnp.where(head0, ls[0], ls[1])
                    return c2

                lax.fori_loop(0, nblk, block, 0, unroll=2)

            chunk = min(sup, 256)

            def merge(ci, c2):
                lr = [pl.ds(pl.multiple_of(di * sup + ci * chunk, chunk), chunk) for di in range(nd)]
                gr = pl.ds(pl.multiple_of(row0 + ci * chunk, chunk), chunk)
                m0, m1, m2 = m_s[lr[0], :], m_s[lr[1], :], m_s[lr[2], :]
                mm = jnp.maximum(jnp.maximum(m0, m1), m2)
                w0, w1, w2 = jnp.exp(m0 - mm), jnp.exp(m1 - mm), jnp.exp(m2 - mm)
                den = (w0 * l_s[lr[0], :] + w1 * l_s[lr[1], :]) + w2 * l_s[lr[2], :]
                num = (w0 * acc_s[lr[0], :] + w1 * acc_s[lr[1], :]) + w2 * acc_s[lr[2], :]
                attn_ref[gr, :] = num / den
                lse_ref[gr, :] = mm + jnp.log(den)
                return c2

            lax.fori_loop(0, sup // chunk, merge, 0)
            return carry

        lax.fori_loop(0, S // sup, supertile, 0)

    blk = pl.BlockSpec((S, 128), lambda j: (0, j))
    out = jax.ShapeDtypeStruct((S, A), F32)
    return pl.pallas_call(
        body, name=name, grid=(A // 128,),
        in_specs=[blk, blk, blk], out_specs=[blk, blk], out_shape=[out, out],
        scratch_shapes=[pltpu.VMEM((nd * sup, 128), F32)] * 3,
        compiler_params=_params(("parallel",)),
    )(q, k, v)


def _attn_out_bwd(dy, attn, gain, name):
    S, A = attn.shape
    tr = _row_tile(S, 256)

    def body(dy_ref, a_ref, g_ref, da_ref, dl_ref, dg_ref):
        @pl.when(pl.program_id(0) == 0)
        def _():
            dg_ref[...] = jnp.zeros_like(dg_ref)

        x = a_ref[...]
        dy_ = dy_ref[...]
        r = lax.rsqrt(jnp.mean(x * x, axis=-1, keepdims=True) + RMS_EPS)
        xh = x * r
        dg_ref[...] += jnp.sum(dy_ * xh, axis=0, keepdims=True)
        dxh = dy_ * g_ref[...]
        dx = r * (dxh - xh * jnp.mean(dxh * xh, axis=-1, keepdims=True))
        da_ref[...] = dx
        prod = dx * x
        hi = lax.broadcasted_iota(jnp.int32, (A, A), 0) // HEAD_DIM
        hj = lax.broadcasted_iota(jnp.int32, (A, A), 1) // HEAD_DIM
        same_head = (hi == hj).astype(F32)
        dl_ref[...] = jnp.dot(prod, same_head, preferred_element_type=F32, precision=lax.Precision.HIGHEST)

    row = pl.BlockSpec((tr, A), lambda i: (i, 0))
    vec = _vec_spec(A, 1)
    return pl.pallas_call(
        body, name=name, grid=(S // tr,),
        in_specs=[row, row, vec], out_specs=[row, row, vec],
        out_shape=[jax.ShapeDtypeStruct((S, A), F32), jax.ShapeDtypeStruct((S, A), F32),
                   jax.ShapeDtypeStruct((1, A), F32)],
        compiler_params=_params(("arbitrary",)),
    )(dy, attn, gain)


def _attn_bwd(q, k, v, da, lse, delta, name):
    S, A = q.shape

    def body(q_ref, k_ref, v_ref, da_ref, lse_ref, dl_ref, dq_ref, dk_ref, dv_ref):
        lane = lax.broadcasted_iota(jnp.int32, (1, 128), 1)
        head0 = lane < HEAD_DIM
        dq_ref[...] = jnp.zeros_like(dq_ref)
        dk_ref[...] = jnp.zeros_like(dk_ref)
        dv_ref[...] = jnp.zeros_like(dv_ref)
        for dil in DILATIONS:
            T, nblk = _branch_blocks(S, dil)
            span = T * dil

            def block(idx, carry, dil=dil, T=T, span=span):
                r = idx % dil
                base = (idx // dil) * span + r
                rows = _block_rows(base, T, dil)
                prev = _block_rows(jnp.maximum(base - span, r), T, dil)
                qb, dab = q_ref[rows, :].astype(BF16), da_ref[rows, :].astype(BF16)
                k2 = jnp.concatenate([k_ref[prev, :], k_ref[rows, :]], axis=0).astype(BF16)
                v2 = jnp.concatenate([v_ref[prev, :], v_ref[rows, :]], axis=0).astype(BF16)
                lse_b, dl_b = lse_ref[rows, :], dl_ref[rows, :]
                valid = _band_mask(T, base >= span)
                dqs = []
                dk2 = dv2 = None
                for h, hmask in enumerate((head0, jnp.logical_not(head0))):
                    qh = jnp.where(hmask, qb, jnp.zeros_like(qb))
                    dah = jnp.where(hmask, dab, jnp.zeros_like(dab))
                    c0 = h * HEAD_DIM
                    lse_h = lse_b[:, c0:c0 + 1]
                    dl_h = dl_b[:, c0:c0 + 1]
                    s = lax.dot_general(qh, k2, NT, preferred_element_type=F32)
                    p = jnp.where(valid, jnp.exp(s - lse_h), 0.0)
                    dp = lax.dot_general(dah, v2, NT, preferred_element_type=F32)
                    ds = (p * (dp - dl_h)).astype(BF16)
                    dqs.append(jnp.dot(ds, k2, preferred_element_type=F32))
                    t_k = lax.dot_general(ds, qh, TN, preferred_element_type=F32)
                    t_v = lax.dot_general(p.astype(BF16), dah, TN, preferred_element_type=F32)
                    dk2 = t_k if dk2 is None else dk2 + t_k
                    dv2 = t_v if dv2 is None else dv2 + t_v
                dq_ref[rows, :] += jnp.where(head0, dqs[0], dqs[1])
                dk_ref[rows, :] += dk2[T:]
                dv_ref[rows, :] += dv2[T:]
                dk_ref[prev, :] += dk2[:T]
                dv_ref[prev, :] += dv2[:T]
                return carry

            lax.fori_loop(0, nblk, block, 0, unroll=2)

    blk = pl.BlockSpec((S, 128), lambda j: (0, j))
    out = jax.ShapeDtypeStruct((S, A), F32)
    return pl.pallas_call(
        body, name=name, grid=(A // 128,),
        in_specs=[blk] * 6, out_specs=[blk] * 3, out_shape=[out] * 3,
        compiler_params=_params(("parallel",)),
    )(q, k, v, da, lse, delta)


def _glu_window(a_ref, b_ref, ah_ref, bh_ref, first):
    u0 = a_ref[...] * _sigmoid(b_ref[...])
    u0h = ah_ref[...] * _sigmoid(bh_ref[...])
    u0h = jnp.where(first, jnp.zeros_like(u0h), u0h)
    return jnp.concatenate([u0h, u0], axis=0)


def _conv_norms(u1, lng, lnb):
    mu = jnp.mean(u1, axis=-1, keepdims=True)
    xc = u1 - mu
    rstd = lax.rsqrt(jnp.mean(xc * xc, axis=-1, keepdims=True) + LN_EPS)
    u1h = xc * rstd
    u2 = u1h * lng + lnb
    sig = _sigmoid(u2)
    u3 = u2 * sig
    r = lax.rsqrt(jnp.mean(u3 * u3, axis=-1, keepdims=True) + RMS_EPS)
    return rstd, u1h, u2, sig, u3, r


def _conv_specs(tr, C, col_a, col_b):
    per = tr // HALO

    def tile(col):
        return pl.BlockSpec((tr, C), lambda i: (i, col))

    def halo(col):
        return pl.BlockSpec((HALO, C), lambda i: (jnp.maximum(i * per - 1, 0), col))

    return tile(col_a), tile(col_b), halo(col_a), halo(col_b)


def _mixer_merge(proj, attn, cw, cb, lng, lnb, gat, gco, name):
    S = proj.shape[0]
    C = CONV_WIDTH
    A = attn.shape[1]
    tr = _row_tile(S, 256)

    def body(a_ref, b_ref, ah_ref, bh_ref, at_ref, w_ref, cb_ref, lng_ref, lnb_ref, gat_ref, gco_ref, u1_ref, y_ref):
        win = _glu_window(a_ref, b_ref, ah_ref, bh_ref, pl.program_id(0) == 0)
        acc = jnp.broadcast_to(cb_ref[...], (tr, C))
        for j in range(CONV_KERNEL):
            off = HALO - (CONV_KERNEL - 1) + j
            acc = acc + w_ref[j:j + 1, :] * win[off:off + tr, :]
        u1_ref[...] = acc
        _, _, _, _, u3, r = _conv_norms(acc, lng_ref[...], lnb_ref[...])
        y_ref[:, A:] = ((u3 * r) * gco_ref[...]).astype(BF16)
        x = at_ref[...]
        ra = lax.rsqrt(jnp.mean(x * x, axis=-1, keepdims=True) + RMS_EPS)
        y_ref[:, :A] = ((x * ra) * gat_ref[...]).astype(BF16)

    ta, tb, ha, hb = _conv_specs(tr, C, 3, 4)
    row = pl.BlockSpec((tr, C), lambda i: (i, 0))
    vec = _vec_spec(C, 1)
    return pl.pallas_call(
        body, name=name, grid=(S // tr,),
        in_specs=[ta, tb, ha, hb, pl.BlockSpec((tr, A), lambda i: (i, 0)), pl.BlockSpec((HALO, C), lambda i: (0, 0)),
                  vec, vec, vec, _vec_spec(A, 1), vec],
        out_specs=[row, pl.BlockSpec((tr, A + C), lambda i: (i, 0))],
        out_shape=[jax.ShapeDtypeStruct((S, C), F32), jax.ShapeDtypeStruct((S, A + C), BF16)],
        compiler_params=_params(("parallel",)),
    )(proj, proj, proj, proj, attn, cw, cb, lng, lnb, gat, gco)


def _conv_bwd_norms(dy, u1, lng, lnb, gco, name):
    S, C = u1.shape
    tr = _row_tile(S, 256)

    def body(dy_ref, u1_ref, lng_ref, lnb_ref, gco_ref, du1_ref, dgco_ref, dlng_ref, dlnb_ref, dcb_ref):
        @pl.when(pl.program_id(0) == 0)
        def _():
            for ref in (dgco_ref, dlng_ref, dlnb_ref, dcb_ref):
                ref[...] = jnp.zeros_like(ref)

        lng = lng_ref[...]
        rstd, u1h, u2, sig, u3, r = _conv_norms(u1_ref[...], lng, lnb_ref[...])
        dy_ = dy_ref[...]
        u3h = u3 * r
        dgco_ref[...] += jnp.sum(dy_ * u3h, axis=0, keepdims=True)
        du3h = dy_ * gco_ref[...]
        du3 = r * (du3h - u3h * jnp.mean(du3h * u3h, axis=-1, keepdims=True))
        du2 = du3 * (sig * (1.0 + u2 * (1.0 - sig)))
        dlng_ref[...] += jnp.sum(du2 * u1h, axis=0, keepdims=True)
        dlnb_ref[...] += jnp.sum(du2, axis=0, keepdims=True)
        du1h = du2 * lng
        du1 = rstd * (du1h - jnp.mean(du1h, axis=-1, keepdims=True)
                      - u1h * jnp.mean(du1h * u1h, axis=-1, keepdims=True))
        du1_ref[...] = du1
        dcb_ref[...] += jnp.sum(du1, axis=0, keepdims=True)

    row = pl.BlockSpec((tr, C), lambda i: (i, 0))
    vec = _vec_spec(C, 1)
    return pl.pallas_call(
        body, name=name, grid=(S // tr,),
        in_specs=[pl.BlockSpec((tr, C), lambda i: (i, 1)), row, vec, vec, vec],
        out_specs=[row, vec, vec, vec, vec],
        out_shape=[jax.ShapeDtypeStruct((S, C), F32)] + [jax.ShapeDtypeStruct((1, C), F32)] * 4,
        compiler_params=_params(("arbitrary",)),
    )(dy, u1, lng, lnb, gco)


def _dproj(du1, proj, cw, dq, dk, dv, cos, sin, name):
    S, C = du1.shape
    A = dq.shape[1]
    tr = _row_tile(S, 256)
    nt = S // tr
    per = tr // HALO
    scale = HEAD_DIM ** -0.5

    def body(du_ref, dun_ref, a_ref, b_ref, ah_ref, bh_ref, w_ref, dq_ref, dk_ref, dv_ref, cos_ref, sin_ref,
             dp_ref, dw_ref):
        i = pl.program_id(0)

        @pl.when(i == 0)
        def _():
            dw_ref[...] = jnp.zeros_like(dw_ref)

        win = _glu_window(a_ref, b_ref, ah_ref, bh_ref, i == 0)
        du = du_ref[...]
        nxt = jnp.where(i == nt - 1, jnp.zeros_like(dun_ref[...]), dun_ref[...])
        dwin = jnp.concatenate([du, nxt], axis=0)
        du0 = jnp.zeros((tr, C), F32)
        taps = []
        for j in range(CONV_KERNEL):
            back = CONV_KERNEL - 1 - j
            du0 = du0 + w_ref[j:j + 1, :] * dwin[back:back + tr, :]
            off = HALO - (CONV_KERNEL - 1) + j
            taps.append(jnp.sum(du * win[off:off + tr, :], axis=0, keepdims=True))
        taps.append(jnp.zeros((HALO - CONV_KERNEL, C), F32))
        dw_ref[...] += jnp.concatenate(taps, axis=0)
        a, sig = a_ref[...], _sigmoid(b_ref[...])
        dp_ref[:, 3 * A:3 * A + C] = (du0 * sig).astype(BF16)
        dp_ref[:, 3 * A + C:] = (du0 * a * sig * (1.0 - sig)).astype(BF16)
        cos_, nsin = cos_ref[...], -sin_ref[...]
        for j in range(A // 128):
            lanes = slice(j * 128, (j + 1) * 128)
            dp_ref[:, j * 128:(j + 1) * 128] = (_rotate(dq_ref[:, lanes], cos_, nsin) * scale).astype(BF16)
            dp_ref[:, A + j * 128:A + (j + 1) * 128] = _rotate(dk_ref[:, lanes], cos_, nsin).astype(BF16)
        dp_ref[:, 2 * A:3 * A] = dv_ref[...].astype(BF16)

    ta, tb, ha, hb = _conv_specs(tr, C, 3, 4)
    row = pl.BlockSpec((tr, C), lambda i: (i, 0))
    arow = pl.BlockSpec((tr, A), lambda i: (i, 0))
    tab = pl.BlockSpec((tr, 128), lambda i: (i, 0))
    nxt = pl.BlockSpec((HALO, C), lambda i: (jnp.minimum((i + 1) * per, S // HALO - 1), 0))
    wspec = pl.BlockSpec((HALO, C), lambda i: (0, 0))
    return pl.pallas_call(
        body, name=name, grid=(nt,),
        in_specs=[row, nxt, ta, tb, ha, hb, wspec, arow, arow, arow, tab, tab],
        out_specs=[pl.BlockSpec((tr, 3 * A + 2 * C), lambda i: (i, 0)), wspec],
        out_shape=[jax.ShapeDtypeStruct((S, 3 * A + 2 * C), BF16), jax.ShapeDtypeStruct((HALO, C), F32)],
        compiler_params=_params(("arbitrary",)),
    )(du1, du1, proj, proj, proj, proj, cw, dq, dk, dv, cos, sin)


def _ada_fwd(c_all, w, b, name):
    B, D = c_all.shape
    N = w.shape[1]
    tn = 768 if N % 768 == 0 else N

    def body(c_ref, w_ref, b_ref, o_ref):
        c = c_ref[...]
        a = (c * _sigmoid(c)).astype(BF16)
        o_ref[...] = jnp.dot(a, w_ref[...].astype(BF16), preferred_element_type=F32) + b_ref[...]

    return pl.pallas_call(
        body, name=name, grid=(N // tn,),
        in_specs=[pl.BlockSpec((B, D), lambda j: (0, 0)), pl.BlockSpec((D, tn), lambda j: (0, j)),
                  pl.BlockSpec((1, tn), lambda j: (0, j))],
        out_specs=pl.BlockSpec((B, tn), lambda j: (0, j)),
        out_shape=jax.ShapeDtypeStruct((B, N), F32),
        compiler_params=_params(("parallel",)),
    )(c_all, w, b)


def _ada_wgrad(c_t, dmod, name):
    D, B = c_t.shape
    N = dmod.shape[1]
    tn = 768 if N % 768 == 0 else N

    def body(c_ref, d_ref, o_ref):
        c = c_ref[...]
        a = (c * _sigmoid(c)).astype(BF16)
        o_ref[...] = jnp.dot(a, d_ref[...].astype(BF16), preferred_element_type=F32)

    return pl.pallas_call(
        body, name=name, grid=(N // tn,),
        in_specs=[pl.BlockSpec((D, B), lambda j: (0, 0)), pl.BlockSpec((B, tn), lambda j: (0, j))],
        out_specs=pl.BlockSpec((D, tn), lambda j: (0, j)),
        out_shape=jax.ShapeDtypeStruct((D, N), F32),
        compiler_params=_params(("parallel",)),
    )(c_t, dmod)


def _sum_rows(x, name):
    R, N = x.shape

    def body(x_ref, o_ref):
        acc = x_ref[0:1, :]
        for r in range(1, R):
            acc = acc + x_ref[r:r + 1, :]
        o_ref[...] = acc

    return pl.pallas_call(
        body, name=name, out_shape=jax.ShapeDtypeStruct((1, N), F32),
        compiler_params=_params(),
    )(x)


def _adamw(w, g, m, v, name, deps=()):
    R, C = w.shape
    tr = _row_tile(R, 256) if R % 256 == 0 else R
    bc1 = 1.0 - ADAM_B1 ** ADAM_STEP
    bc2 = 1.0 - ADAM_B2 ** ADAM_STEP
    deps = _deps(deps)

    def body(w_ref, g_ref, m_ref, v_ref, d_ref, mo_ref, vo_ref):
        g_ = g_ref[...]
        m_ = ADAM_B1 * m_ref[...] + (1.0 - ADAM_B1) * g_
        v_ = ADAM_B2 * v_ref[...] + (1.0 - ADAM_B2) * (g_ * g_)
        mo_ref[...] = m_
        vo_ref[...] = v_
        d_ref[...] = -ADAM_LR * ((m_ / bc1) / (jnp.sqrt(v_ / bc2) + ADAM_EPS) + ADAM_WD * w_ref[...])

    row = pl.BlockSpec((tr, C), lambda i: (i, 0))
    out = jax.ShapeDtypeStruct((R, C), F32)
    return pl.pallas_call(
        _blind_to(body, 4, len(deps)), name=name, grid=(R // tr,),
        in_specs=[row] * 4 + [ANY_SPEC] * len(deps), out_specs=[row] * 3, out_shape=[out] * 3,
        compiler_params=_params(("parallel",)),
    )(w, g, m, v, *deps)


def _coords():
    return lax.axis_index("x"), lax.axis_index("y"), lax.axis_index("c")


def _all_gather8(x, name, deps=()):
    R, N = x.shape
    assert R == 8
    flips = [(fx, fy, fc) for fx in (0, 1) for fy in (0, 1) for fc in (0, 1)][1:]
    deps = _deps(deps)

    def body(x_ref, o_ref, send_sems, recv_sems):
        mx, my, mc = _coords()
        me = 4 * mx + 2 * my + mc

        def rows(dev):
            return o_ref.at[pl.ds(pl.multiple_of(dev * R, R), R), :]

        o_ref[pl.ds(pl.multiple_of(me * R, R), R), :] = x_ref[...]
        copies = []
        for t, (fx, fy, fc) in enumerate(flips):
            peer = (mx ^ fx, my ^ fy, mc ^ fc)
            copies.append(pltpu.make_async_remote_copy(
                src_ref=x_ref, dst_ref=rows(me), send_sem=send_sems.at[t], recv_sem=recv_sems.at[t],
                device_id=peer, device_id_type=MESH))
        for cp in copies:
            cp.start()
        for t, (fx, fy, fc) in enumerate(flips):
            peer_id = 4 * (mx ^ fx) + 2 * (my ^ fy) + (mc ^ fc)
            pltpu.make_async_remote_copy(
                src_ref=x_ref, dst_ref=rows(peer_id), send_sem=send_sems.at[t], recv_sem=recv_sems.at[t],
                device_id=(mx ^ fx, my ^ fy, mc ^ fc), device_id_type=MESH).wait_recv()
        for cp in copies:
            cp.wait_send()

    return pl.pallas_call(
        _blind_to(body, 1, len(deps)), name=name,
        in_specs=[pl.BlockSpec(memory_space=pltpu.VMEM)] + [ANY_SPEC] * len(deps),
        out_specs=pl.BlockSpec(memory_space=pltpu.VMEM),
        out_shape=jax.ShapeDtypeStruct((N_DEV * R, N), F32),
        scratch_shapes=[pltpu.SemaphoreType.DMA((7,)), pltpu.SemaphoreType.DMA((7,))],
        compiler_params=pltpu.CompilerParams(has_side_effects=True, vmem_limit_bytes=VMEM_LIMIT_BYTES),
    )(x, *deps)


def _half_rows(rows, half):
    return pl.ds(pl.multiple_of(half * (rows // 2), 8), rows // 2)


def _split_start(bufs, plan, n, name):
    nb = len(bufs)

    def body(*refs):
        send_sems, recv_sems, token = refs[nb], refs[nb + 1], refs[-1]
        for t, (src, dst, dev) in enumerate(plan(refs[:nb])):
            pltpu.make_async_remote_copy(src_ref=src, dst_ref=dst, send_sem=send_sems.at[t],
                                         recv_sem=recv_sems.at[t], device_id=dev, device_id_type=MESH).start()
        token[...] = jnp.zeros_like(token)

    out = pl.pallas_call(
        body, name=name,
        out_shape=(pltpu.SemaphoreType.DMA((n,)), pltpu.SemaphoreType.DMA((n,)),
                   *[pltpu.HBM(b.shape, b.dtype) for b in bufs], jax.ShapeDtypeStruct((8, 128), F32)),
        in_specs=[HBM_SPEC] * nb,
        out_specs=(SEM_SPEC, SEM_SPEC, *[HBM_SPEC] * nb, pl.BlockSpec(memory_space=pltpu.VMEM)),
        input_output_aliases={i: 2 + i for i in range(nb)},
        compiler_params=pltpu.CompilerParams(has_side_effects=DATAFLOW),
    )(*[pltpu.with_memory_space_constraint(b, pltpu.HBM) for b in bufs])
    return out[0], out[1], list(out[2:2 + nb]), out[-1]


def _split_wait(bufs, send_sems, recv_sems, plan, after, name):
    nb = len(bufs)
    after = _deps(after)

    def body(*refs):
        ss, rs = refs[nb], refs[nb + 1]
        for t, (src, dst, dev) in enumerate(plan(refs[:nb])):
            cp = pltpu.make_async_remote_copy(src_ref=src, dst_ref=dst, send_sem=ss.at[t], recv_sem=rs.at[t],
                                              device_id=dev, device_id_type=MESH)
            cp.wait_send()
            cp.wait_recv()

    out = pl.pallas_call(
        body, name=name,
        out_shape=tuple(pltpu.HBM(b.shape, b.dtype) for b in bufs),
        in_specs=[HBM_SPEC] * nb + [SEM_SPEC, SEM_SPEC] + [ANY_SPEC] * len(after),
        out_specs=tuple([HBM_SPEC] * nb),
        input_output_aliases={i: i for i in range(nb)},
        compiler_params=pltpu.CompilerParams(has_side_effects=DATAFLOW),
    )(*bufs, send_sems, recv_sems, *after)
    return list(out)


class _Exchange:
    def __init__(self, bufs, plan, n, name):
        self.plan, self.name = plan, name
        self.send_sems, self.recv_sems, self.bufs, self.token = _split_start(bufs, plan, n, name + "_start")

    def wait(self, after):
        return _split_wait(self.bufs, self.send_sems, self.recv_sems, self.plan, after, self.name + "_wait")


def _cast_place(w, chip_idx, name, deps=()):
    R, C = w.shape
    tr = _row_tile(R, 256) if R % 256 == 0 else R
    deps = _deps(deps)

    def body(k_ref, w_ref, o_ref):
        o_ref[...] = w_ref[...].astype(BF16)

    grid_spec = pltpu.PrefetchScalarGridSpec(
        num_scalar_prefetch=1, grid=(R // tr,),
        in_specs=[pl.BlockSpec((tr, C), lambda i, k: (i, 0))] + [ANY_SPEC] * len(deps),
        out_specs=pl.BlockSpec((None, tr, C), lambda i, k: (k[0], i, 0)),
    )
    return pl.pallas_call(
        _blind_to(body, 2, len(deps)), name=name, grid_spec=grid_spec,
        out_shape=jax.ShapeDtypeStruct((N_CHIPS, R, C), BF16),
        compiler_params=_params(("parallel",)),
    )(chip_idx, w, *deps)


def _plan_gather(refs):
    mx, my, mc = _coords()
    me = 2 * mx + my
    plan = []
    for g in refs:
        mine = g.at[me, _half_rows(g.shape[1], mc), :]
        for fx, fy in XY_FLIPS:
            plan.append((mine, mine, (mx ^ fx, my ^ fy, mc)))
    return plan


def _plan_swap(refs):
    mx, my, mc = _coords()
    plan = []
    for g in refs:
        for fx, fy in XY_FLIPS:
            have = g.at[2 * (mx ^ fx) + (my ^ fy), _half_rows(g.shape[1], mc), :]
            plan.append((have, have, (mx, my, 1 - mc)))
    return plan


def _plan_other_halves(refs):
    n = len(refs) // 2
    mx, my, mc = _coords()
    return [(g.at[pl.ds(0, N_CHIPS), _half_rows(g.shape[1], 1 - mc), :], land, (mx, my, 1 - mc))
            for g, land in zip(refs[:n], refs[n:])]


def _plan_chunks(refs):
    n = len(refs) // 2
    mx, my, mc = _coords()
    plan = []
    for s, land in zip(refs[:n], refs[n:]):
        for t, (fx, fy) in enumerate(XY_FLIPS):
            plan.append((s.at[2 * (mx ^ fx) + (my ^ fy)], land.at[t], (mx ^ fx, my ^ fy, mc)))
    return plan


def _plan_share(refs):
    mx, my, mc = _coords()
    return [(full.at[mc], full.at[mc], (mx, my, 1 - mc)) for full in refs]


def _add_half(g, recv, core_idx, name):
    nk, R, C = g.shape
    rh = R // 2
    tr = _row_tile(rh, 128) if rh % 128 == 0 else rh
    nt = rh // tr

    def body(c_ref, g_ref, r_ref, o_ref):
        o_ref[...] = (g_ref[...] + r_ref[...]).astype(BF16)

    grid_spec = pltpu.PrefetchScalarGridSpec(
        num_scalar_prefetch=1, grid=(nk, nt),
        in_specs=[pl.BlockSpec((None, tr, C), lambda k, i, c: (k, c[0] * nt + i, 0)),
                  pl.BlockSpec((None, tr, C), lambda k, i, c: (k, i, 0))],
        out_specs=pl.BlockSpec((None, tr, C), lambda k, i, c: (k, i, 0)),
    )
    return pl.pallas_call(
        body, name=name, grid_spec=grid_spec, out_shape=jax.ShapeDtypeStruct((nk, rh, C), BF16),
        compiler_params=_params(("parallel", "parallel")),
    )(core_idx, g, recv)


def _sum_chips(s, land, chip_core, name):
    _, rh, C = s.shape
    tr = _row_tile(rh, 128) if rh % 128 == 0 else rh

    def body(p_ref, s_ref, l_ref, o_ref):
        me = p_ref[0]
        acc = None
        for j in range(N_CHIPS):
            t = jnp.maximum(jnp.bitwise_xor(me, j) - 1, 0)
            term = jnp.where(me == j, s_ref[...], l_ref[t]).astype(F32)
            acc = term if acc is None else acc + term
        o_ref[...] = acc

    grid_spec = pltpu.PrefetchScalarGridSpec(
        num_scalar_prefetch=1, grid=(rh // tr,),
        in_specs=[pl.BlockSpec((None, tr, C), lambda i, p: (p[0], i, 0)),
                  pl.BlockSpec((3, tr, C), lambda i, p: (0, i, 0))],
        out_specs=pl.BlockSpec((None, tr, C), lambda i, p: (p[1], i, 0)),
    )
    return pl.pallas_call(
        body, name=name, grid_spec=grid_spec, out_shape=jax.ShapeDtypeStruct((2, rh, C), F32),
        compiler_params=_params(("parallel",)),
    )(chip_core, s, land)


def _rs_send_halves(grads, tag):
    lands = [lax.empty((g.shape[0], g.shape[1] // 2, g.shape[2]), g.dtype) for g in grads]
    return _Exchange(list(grads) + lands, _plan_other_halves, len(grads), f"rs_halves_{tag}")


def _rs_send_chunks(ex, after, core_idx, tag):
    bufs = ex.wait(after)
    n = len(bufs) // 2
    sums = [_add_half(g, r, core_idx, f"rs_add_{tag}_{i}") for i, (g, r) in enumerate(zip(bufs[:n], bufs[n:]))]
    lands = [lax.empty((3,) + s.shape[1:], s.dtype) for s in sums]
    return _Exchange(sums + lands, _plan_chunks, 3 * n, f"rs_chunks_{tag}")


def _rs_send_share(ex, after, chip_core, tag):
    bufs = ex.wait(after)
    n = len(bufs) // 2
    fulls = [_sum_chips(s, l, chip_core, f"rs_sum_{tag}_{i}") for i, (s, l) in enumerate(zip(bufs[:n], bufs[n:]))]
    return _Exchange(fulls, _plan_share, n, f"rs_share_{tag}")


def _rs_finish(ex, after):
    return [b.reshape(2 * b.shape[1], b.shape[2]) for b in ex.wait(after)]


def _ffn_forward(h, gain, sc, sh, gate, wg, wu, wd, tag):
    n = _norm_mod(h, gain, sc, sh, f"{tag}_norm")
    ga, up, act = _ffn_gate_up(n, wg, wu, f"{tag}_gate_up")
    h_out, f = _mm_residual(act, wd, h, gate, 0.5, f"{tag}_down")
    return h_out, (h, n, ga, up, act, f)


def _ffn_backward(dh_out, saved, gain, sc, gate, wg, wu, wd, core_idx, tag, last=False):
    h, n, ga, up, act, f = saved
    df, d_gate = _gate_bwd(dh_out, f, gate, 0.5, f"{tag}_gate_bwd")
    (dwd,) = _wgrad_chunk_lhs([act], df, f"{tag}_dwd")
    ex_d = _rs_send_halves([dwd], f"{tag}_d")
    dga, dup = _ffn_dact(df, wd, ga, up, f"{tag}_dact", deps=[ex_d.token])
    ex_d = _rs_send_chunks(ex_d, [dga], core_idx, f"{tag}_d")
    dwg, dwu = _wgrad_chunk_lhs([dga, dup], n, f"{tag}_dwgu", deps=[ex_d.token])
    ex_gu = _rs_send_halves([dwg, dwu], f"{tag}_gu")
    dn = _mm_reduce([dga, dup], [wg, wu], True, True, f"{tag}_dn", deps=[ex_gu.token])
    if not last:
        ex_gu = _rs_send_chunks(ex_gu, [dn], core_idx, f"{tag}_gu")
    dh_in, d_sh, d_sc, d_gain = _norm_mod_bwd(dn, h, gain, sc, dh_out, f"{tag}_norm_bwd", deps=[ex_gu.token])
    return dh_in, (ex_d, ex_gu), (d_sh, d_sc, d_gate, d_gain)


def _pad_cols(v, n):
    return jnp.pad(v, ((0, 0), (0, n - v.shape[1])))


def _mixer_forward(h1, mix_norm_g, sc2, sh2, gt2, win, wout, conv_w, conv_dw_b, conv_ln_g, conv_ln_b, attn_out_g,
                   conv_out_g):
    S, D = h1.shape
    n2 = _norm_mod(h1, mix_norm_g, sc2, sh2, "mix_norm")
    proj = _mm_cols(n2, win, "mix_in")
    cos, sin = _rope_tables(S)
    q, k, v = _qkv_rope(proj, cos, sin, "qkv_rope")
    attn, lse = _attn_fwd(q, k, v, "attn_fwd")
    u1, y = _mixer_merge(proj, attn, conv_w, conv_dw_b, conv_ln_g, conv_ln_b, attn_out_g, conv_out_g, "mix_merge")
    h2, mo = _mm_residual(y[None], wout.reshape(1, D, D), h1, gt2, 1.0, "mix_out")
    return h2, (h1, n2, proj, cos, sin, q, k, v, attn, lse, u1, y, mo)


def _mixer_backward(dh2, saved, mix_norm_g, sc2, gt2, win, wout, conv_w, conv_ln_g, conv_ln_b, attn_out_g,
                    conv_out_g, core_idx):
    h1, n2, proj, cos, sin, q, k, v, attn, lse, u1, y, mo = saved
    S, D = h1.shape
    dmo, d_gt2 = _gate_bwd(dh2, mo, gt2, 1.0, "mix_gate_bwd")
    (dwout,) = _wgrad_chunk_lhs([y[None]], dmo, "mix_dwout")
    dy = _mm_nt(dmo, wout.reshape(D, D), "mix_dy")
    dattn, delta, d_attn_g = _attn_out_bwd(dy, attn, attn_out_g, "attn_out_bwd")
    dq, dk, dv = _attn_bwd(q, k, v, dattn, lse, delta, "attn_bwd")
    du1, d_gco, d_lng, d_lnb, d_cb = _conv_bwd_norms(dy, u1, conv_ln_g, conv_ln_b, conv_out_g, "conv_bwd_norms")
    dproj, d_cw = _dproj(du1, proj, conv_w, dq, dk, dv, cos, sin, "mix_dproj")
    (dwin,) = _wgrad_chunk_rhs(n2, [dproj], N_CHIPS, False, "mix_dwin")
    ex = _rs_send_halves([dwin, dwout.reshape(N_CHIPS, D // N_CHIPS, D)], "mix")
    dn2 = _mm_reduce([dproj], [win], False, False, "mix_dn", deps=[ex.token])
    ex = _rs_send_chunks(ex, [dn2], core_idx, "mix")
    dh1, d_sh2, d_sc2, d_gain2 = _norm_mod_bwd(dn2, h1, mix_norm_g, sc2, dh2, "mix_norm_bwd", deps=[ex.token])
    small = (d_sh2, d_sc2, d_gt2, d_gain2, d_cb, d_lng, d_lnb, d_attn_g, d_gco, d_cw)
    return dh1, ex, small


def kernel(x, c, w_ada, b_ada, ffn1_norm_g, ffn1_w_gate, ffn1_w_up, ffn1_w_down, mix_norm_g, w_in, conv_dw_w, conv_dw_b, conv_ln_g, conv_ln_b, attn_out_g, conv_out_g, w_out, ffn2_norm_g, ffn2_w_gate, ffn2_w_up, ffn2_w_down, final_norm_g, loss_target, m_w_ada, m_b_ada, m_ffn1_norm_g, m_ffn1_w_gate, m_ffn1_w_up, m_ffn1_w_down, m_mix_norm_g, m_w_in, m_conv_dw_w, m_conv_dw_b, m_conv_ln_g, m_conv_ln_b, m_attn_out_g, m_conv_out_g, m_w_out, m_ffn2_norm_g, m_ffn2_w_gate, m_ffn2_w_up, m_ffn2_w_down, m_final_norm_g, v_w_ada, v_b_ada, v_ffn1_norm_g, v_ffn1_w_gate, v_ffn1_w_up, v_ffn1_w_down, v_mix_norm_g, v_w_in, v_conv_dw_w, v_conv_dw_b, v_conv_ln_g, v_conv_ln_b, v_attn_out_g, v_conv_out_g, v_w_out, v_ffn2_norm_g, v_ffn2_w_gate, v_ffn2_w_up, v_ffn2_w_down, v_final_norm_g):
    S, D = x.shape[1], x.shape[2]
    mx, my, mc = _coords()
    chip = 2 * mx + my
    dev = 4 * mx + 2 * my + mc
    chip_idx = chip.astype(jnp.int32).reshape(1)
    core_idx = mc.astype(jnp.int32).reshape(1)
    chip_core = jnp.stack([chip, mc]).astype(jnp.int32)
    h0 = x[0]
    target = loss_target[0]

    ncw = CONV_KERNEL * 128
    n0 = -(-(D + ncw) // 1024) * 1024
    pk0 = _pad_cols(jnp.concatenate([c.reshape(1, D), conv_dw_w.reshape(1, ncw)], axis=1), n0)
    g0 = _all_gather8(pk0.reshape(8, n0 // 8), "gather_c").reshape(N_DEV, n0)
    c_all = g0[:, :D]
    conv_w = jnp.concatenate([g0[2 * kc, D:D + ncw].reshape(CONV_KERNEL, 128) for kc in range(N_CHIPS)], axis=1)
    conv_w = jnp.pad(conv_w, ((0, HALO - CONV_KERNEL), (0, 0)))
    nmod = w_ada.shape[2]
    b_shard = lax.dynamic_slice(b_ada, (0, chip * nmod), (1, nmod))
    mod_part = _ada_fwd(c_all, w_ada[0], b_shard, "ada_fwd")
    g1 = _all_gather8(mod_part, "gather_mod")
    mod_all = jnp.concatenate([g1[16 * kc:16 * kc + 8] for kc in range(N_CHIPS)], axis=1)
    mod = lax.dynamic_slice(mod_all, (dev, 0), (1, 9 * D))
    sh1, sc1, gt1, sh2, sc2, gt2, sh3, sc3, gt3 = [mod[:, i * D:(i + 1) * D] for i in range(9)]

    def gather_start(ws, tag, dep):
        slots = [_cast_place(w, chip_idx, f"cast_{tag}_{i}", deps=[dep]) for i, w in enumerate(ws)]
        return _Exchange(slots, _plan_gather, 3 * len(ws), f"gather_{tag}")

    def swap_start(ex, after, tag):
        return _Exchange(ex.wait(after), _plan_swap, 3 * len(ex.bufs), f"swap_{tag}")

    ex_w1 = gather_start([ffn1_w_gate[0].T, ffn1_w_up[0].T, ffn1_w_down[0]], "ffn1", g1)
    ex_wm = gather_start([w_in[0], w_out[0]], "mix", ex_w1.token)
    ex_w2 = gather_start([ffn2_w_gate[0].T, ffn2_w_up[0].T, ffn2_w_down[0]], "ffn2", ex_wm.token)

    n1 = _norm_mod(h0, ffn1_norm_g, sc1, sh1, "ffn1_norm")
    wg1, wu1, wd1 = swap_start(ex_w1, [n1, ex_w2.token], "ffn1").wait([])
    ga1, up1, act1 = _ffn_gate_up(n1, wg1, wu1, "ffn1_gate_up")
    ex_wm = swap_start(ex_wm, [act1], "mix")
    h1, f1 = _mm_residual(act1, wd1, h0, gt1, 0.5, "ffn1_down")
    saved1 = (h0, n1, ga1, up1, act1, f1)
    win, wout = ex_wm.wait([h1])
    ex_w2 = swap_start(ex_w2, [h1], "ffn2")
    h2, saved2 = _mixer_forward(h1, mix_norm_g, sc2, sh2, gt2, win, wout, conv_w, conv_dw_b, conv_ln_g, conv_ln_b,
                                attn_out_g, conv_out_g)
    wg2, wu2, wd2 = ex_w2.wait([h2])
    h3, saved3 = _ffn_forward(h2, ffn2_norm_g, sc3, sh3, gt3, wg2, wu2, wd2, "ffn2")
    loss_part, dh3, d_final_g = _loss_head(h3, final_norm_g.reshape(1, D), target, "loss_head")

    dh2, (ex_d2, ex_gu2), (d_sh3, d_sc3, d_gt3, d_gain3) = _ffn_backward(
        dh3, saved3, ffn2_norm_g, sc3, gt3, wg2, wu2, wd2, core_idx, "ffn2")
    dh1, ex_mix, small_mix = _mixer_backward(
        dh2, saved2, mix_norm_g, sc2, gt2, win, wout, conv_w, conv_ln_g, conv_ln_b, attn_out_g, conv_out_g, core_idx)
    d_sh2, d_sc2, d_gt2, d_gain2, d_cb, d_lng, d_lnb, d_attn_g, d_gco, d_cw = small_mix
    dh0, (ex_d1, ex_gu1), (d_sh1, d_sc1, d_gt1, d_gain1) = _ffn_backward(
        dh1, saved1, ffn1_norm_g, sc1, gt1, wg1, wu1, wd1, core_idx, "ffn1", last=True)

    dmod = jnp.concatenate([d_sh1, d_sc1, d_gt1, d_sh2, d_sc2, d_gt2, d_sh3, d_sc3, d_gt3], axis=1)
    small = [d_gain1, d_gain2, d_gain3, d_final_g, d_cb, d_lng, d_lnb, d_attn_g, d_gco,
             d_cw[:CONV_KERNEL].reshape(1, CONV_KERNEL * CONV_WIDTH), loss_part]
    pk1 = jnp.concatenate([dmod] + small, axis=1)
    n1_ = -(-pk1.shape[1] // 1024) * 1024
    gathered = _all_gather8(_pad_cols(pk1, n1_).reshape(8, n1_ // 8), "gather_small").reshape(N_DEV, n1_)
    ex_gu1 = _rs_send_chunks(ex_gu1, [gathered], core_idx, "ffn1_gu")
    tot = _sum_rows(gathered, "sum_small")
    off = [0]

    def take(nel):
        out = tot[:, off[0]:off[0] + nel]
        off[0] += nel
        return out

    g_b_ada = take(9 * D)
    g_ffn1_norm, g_mix_norm, g_ffn2_norm, g_final = take(D), take(D), take(D), take(D)
    g_cb, g_lng, g_lnb, g_attn_g, g_gco = take(512), take(512), take(512), take(512), take(512)
    g_cw_full = take(CONV_KERNEL * CONV_WIDTH).reshape(CONV_KERNEL, CONV_WIDTH)
    loss = take(128)[0, 0]
    g_cw = lax.dynamic_slice(g_cw_full, (0, chip * 128), (CONV_KERNEL, 128))

    dmod_shard = lax.dynamic_slice(gathered[:, :9 * D], (0, chip * nmod), (N_DEV, nmod))
    dmod16 = jnp.pad(dmod_shard, ((0, N_DEV), (0, 0)))
    c_t16 = jnp.pad(c_all.T, ((0, 0), (0, N_DEV)))
    g_w_ada = _ada_wgrad(c_t16, dmod16, "ada_wgrad")

    names = ["w_ada", "b_ada", "ffn1_norm_g", "ffn1_w_gate", "ffn1_w_up", "ffn1_w_down", "mix_norm_g", "w_in",
             "conv_dw_w", "conv_dw_b", "conv_ln_g", "conv_ln_b", "attn_out_g", "conv_out_g", "w_out", "ffn2_norm_g",
             "ffn2_w_gate", "ffn2_w_up", "ffn2_w_down", "final_norm_g"]
    weights = dict(zip(names, [w_ada, b_ada, ffn1_norm_g, ffn1_w_gate, ffn1_w_up, ffn1_w_down, mix_norm_g, w_in,
                               conv_dw_w, conv_dw_b, conv_ln_g, conv_ln_b, attn_out_g, conv_out_g, w_out,
                               ffn2_norm_g, ffn2_w_gate, ffn2_w_up, ffn2_w_down, final_norm_g]))
    ms = dict(zip(names, [m_w_ada, m_b_ada, m_ffn1_norm_g, m_ffn1_w_gate, m_ffn1_w_up, m_ffn1_w_down, m_mix_norm_g,
                          m_w_in, m_conv_dw_w, m_conv_dw_b, m_conv_ln_g, m_conv_ln_b, m_attn_out_g, m_conv_out_g,
                          m_w_out, m_ffn2_norm_g, m_ffn2_w_gate, m_ffn2_w_up, m_ffn2_w_down, m_final_norm_g]))
    vs = dict(zip(names, [v_w_ada, v_b_ada, v_ffn1_norm_g, v_ffn1_w_gate, v_ffn1_w_up, v_ffn1_w_down, v_mix_norm_g,
                          v_w_in, v_conv_dw_w, v_conv_dw_b, v_conv_ln_g, v_conv_ln_b, v_attn_out_g, v_conv_out_g,
                          v_w_out, v_ffn2_norm_g, v_ffn2_w_gate, v_ffn2_w_up, v_ffn2_w_down, v_final_norm_g]))
    grads, deltas, new_ms, new_vs = {}, {}, {}, {}

    def adamw_big(nm, g2d, deps=(), transposed=False):
        shape = weights[nm].shape
        two_d = (shape[-2], shape[-1])

        def view(t):
            return t.reshape(two_d).T if transposed else t.reshape(two_d)

        d_, m_, v_ = _adamw(view(weights[nm]), g2d, view(ms[nm]), view(vs[nm]), f"adamw_{nm}", deps=deps)
        grads[nm], deltas[nm], new_ms[nm], new_vs[nm] = (
            (t.T if transposed else t).reshape(shape) for t in (g2d, d_, m_, v_))
        return d_

    d_ada = adamw_big("w_ada", g_w_ada, deps=[ex_gu1.token])
    small_grads = {"b_ada": g_b_ada, "ffn1_norm_g": g_ffn1_norm, "mix_norm_g": g_mix_norm, "conv_dw_w": g_cw,
                   "conv_dw_b": g_cb, "conv_ln_g": g_lng, "conv_ln_b": g_lnb, "attn_out_g": g_attn_g,
                   "conv_out_g": g_gco, "ffn2_norm_g": g_ffn2_norm, "final_norm_g": g_final}
    small_names = [nm for nm in names if nm in small_grads]

    def pack_small(arrs):
        flat = jnp.concatenate([arrs[nm].reshape(1, -1) for nm in small_names], axis=1)
        npad = -(-flat.shape[1] // 1024) * 1024
        return _pad_cols(flat, npad).reshape(8, npad // 8)

    d_s, m_s, v_s = _adamw(pack_small(weights), pack_small(small_grads), pack_small(ms), pack_small(vs),
                           "adamw_small")
    pos = 0
    for nm in small_names:
        shape, nel = weights[nm].shape, weights[nm].size
        grads[nm] = small_grads[nm].reshape(shape)
        deltas[nm], new_ms[nm], new_vs[nm] = (t.reshape(1, -1)[:, pos:pos + nel].reshape(shape)
                                              for t in (d_s, m_s, v_s))
        pos += nel

    ex_d2 = _rs_send_share(ex_d2, [d_ada, d_s], chip_core, "ffn2_d")
    ex_gu2 = _rs_send_share(ex_gu2, [ex_d2.token], chip_core, "ffn2_gu")
    ex_mix = _rs_send_share(ex_mix, [ex_gu2.token], chip_core, "mix")
    ex_d1 = _rs_send_share(ex_d1, [ex_mix.token], chip_core, "ffn1_d")
    (g_wd2,) = _rs_finish(ex_d2, [ex_d1.token])
    last = [adamw_big("ffn2_w_down", g_wd2)]
    g_wg2, g_wu2 = _rs_finish(ex_gu2, last)
    last = [adamw_big("ffn2_w_gate", g_wg2, transposed=True), adamw_big("ffn2_w_up", g_wu2, transposed=True)]
    g_win, g_wout = _rs_finish(ex_mix, last)
    last = [adamw_big("w_in", g_win), adamw_big("w_out", g_wout)]
    (g_wd1,) = _rs_finish(ex_d1, last)
    last = [adamw_big("ffn1_w_down", g_wd1)]
    ex_gu1 = _rs_send_share(ex_gu1, last, chip_core, "ffn1_gu")
    g_wg1, g_wu1 = _rs_finish(ex_gu1, [])
    adamw_big("ffn1_w_gate", g_wg1, transposed=True)
    adamw_big("ffn1_w_up", g_wu1, transposed=True)

    return (loss, dh0[None], *[grads[nm] for nm in names], *[deltas[nm] for nm in names],
            *[new_ms[nm] for nm in names], *[new_vs[nm] for nm in names])
```

```python
import jax
import jax.numpy as jnp
import numpy as np
from jax import lax
from jax.experimental import pallas as pl
from jax.experimental.pallas import tpu as pltpu

F32 = jnp.float32
BF16 = jnp.bfloat16
MESH = pl.DeviceIdType.MESH

RMS_EPS = 1e-6
LN_EPS = 1e-5
HEAD_DIM = 64
ATTN_WIDTH = 512
CONV_WIDTH = 512
ATTN_BLOCK = 128
DILATIONS = (1, 4, 16)
SUPER_ROWS = ATTN_BLOCK * 16
ROPE_THETA = 10000.0
CONV_KERNEL = 31
HALO = 32
N_CHIPS = 4
N_DEV = 8
ADAM_LR, ADAM_B1, ADAM_B2, ADAM_EPS, ADAM_WD, ADAM_STEP = 0.001, 0.9, 0.999, 1e-08, 0.01, 10
VMEM_LIMIT_BYTES = 62 * 1024 * 1024
NEG = -1e30

NT = (((1,), (1,)), ((), ()))
TN = (((0,), (0,)), ((), ()))

ANY_SPEC = pl.BlockSpec(memory_space=pl.ANY)
HBM_SPEC = pl.BlockSpec(memory_space=pltpu.HBM)
SEM_SPEC = pl.BlockSpec(memory_space=pltpu.SEMAPHORE)
DATAFLOW = pltpu.SideEffectType.DATAFLOW_SIDE_EFFECTING
XY_FLIPS = ((0, 1), (1, 0), (1, 1))


def _params(sem=None):
    return pltpu.CompilerParams(dimension_semantics=sem, vmem_limit_bytes=VMEM_LIMIT_BYTES)


def _row_tile(rows, want):
    t = min(rows, want)
    assert rows % t == 0
    return t


def _sigmoid(x):
    return 1.0 / (1.0 + jnp.exp(-x))


def _deps(deps):
    return [d for d in deps if d is not None]


def _blind_to(body, n_in, n_dep):
    def wrapped(*refs):
        return body(*refs[:n_in], *refs[n_in + n_dep:])
    return wrapped


def _vec_spec(d, ngrid):
    if ngrid == 1:
        return pl.BlockSpec((1, d), lambda i: (0, 0))
    return pl.BlockSpec((1, d), lambda i, j: (0, 0))


def _norm_mod(h, gain, sc, sh, name):
    S, D = h.shape
    tr = _row_tile(S, 512)

    def body(h_ref, g_ref, sc_ref, sh_ref, n_ref):
        x = h_ref[...]
        r = lax.rsqrt(jnp.mean(x * x, axis=-1, keepdims=True) + RMS_EPS)
        y = (x * r) * g_ref[...]
        n_ref[...] = (y * (1.0 + sc_ref[...]) + sh_ref[...]).astype(BF16)

    row = pl.BlockSpec((tr, D), lambda i: (i, 0))
    return pl.pallas_call(
        body, name=name, grid=(S // tr,),
        in_specs=[row, _vec_spec(D, 1), _vec_spec(D, 1), _vec_spec(D, 1)],
        out_specs=row, out_shape=jax.ShapeDtypeStruct((S, D), BF16),
        compiler_params=_params(("parallel",)),
    )(h, gain, sc, sh)


def _norm_mod_bwd(dn, h_in, gain, sc, dh_out, name, prev=None, deps=()):
    S, D = h_in.shape
    tr = _row_tile(S, 512)
    deps = _deps(deps)
    n_in = 5 if prev is None else 7

    def body(*refs):
        dn_ref, h_ref, g_ref, sc_ref, dho_ref = refs[:5]
        dh_ref, dsh_ref, dsc_ref, dg_ref = refs[n_in:n_in + 4]

        @pl.when(pl.program_id(0) == 0)
        def _():
            for ref in refs[n_in + 1:n_in + 4] + refs[n_in + 5:]:
                ref[...] = jnp.zeros_like(ref)

        x = h_ref[...]
        dn_ = dn_ref[...]
        g = g_ref[...]
        one_sc = 1.0 + sc_ref[...]
        r = lax.rsqrt(jnp.mean(x * x, axis=-1, keepdims=True) + RMS_EPS)
        xh = x * r
        dsh_ref[...] += jnp.sum(dn_, axis=0, keepdims=True)
        dsc_ref[...] += jnp.sum(dn_ * (xh * g), axis=0, keepdims=True)
        dg_ref[...] += jnp.sum(dn_ * one_sc * xh, axis=0, keepdims=True)
        dxh = dn_ * (g * one_sc)
        dh = dho_ref[...] + r * (dxh - xh * jnp.mean(dxh * xh, axis=-1, keepdims=True))
        dh_ref[...] = dh
        if prev is not None:
            _gate_back(dh, refs[5], refs[6], prev[2], refs[n_in + 4], refs[n_in + 5])

    row = pl.BlockSpec((tr, D), lambda i: (i, 0))
    vec = _vec_spec(D, 1)
    extra_in = [] if prev is None else [row, vec]
    extra_out = [] if prev is None else [row, vec]
    extra_shape = [] if prev is None else [jax.ShapeDtypeStruct((S, D), BF16), jax.ShapeDtypeStruct((1, D), F32)]
    return pl.pallas_call(
        _blind_to(body, n_in, len(deps)), name=name, grid=(S // tr,),
        in_specs=[row, row, vec, vec, row] + extra_in + [ANY_SPEC] * len(deps),
        out_specs=[row, vec, vec, vec] + extra_out,
        out_shape=[jax.ShapeDtypeStruct((S, D), F32)] + [jax.ShapeDtypeStruct((1, D), F32)] * 3 + extra_shape,
        compiler_params=_params(("arbitrary",)),
    )(dn, h_in, gain, sc, dh_out, *([] if prev is None else prev[:2]), *deps)


def _gate_back(dh, f_ref, gate_ref, coef, df_ref, dgate_ref):
    df_ref[...] = ((coef * gate_ref[...]) * dh).astype(BF16)
    dgate_ref[...] += jnp.sum(coef * dh * f_ref[...].astype(F32), axis=0, keepdims=True)


def _loss_head(h, gain, target, f, gate, coef, name):
    S, D = h.shape
    tr = _row_tile(S, 512)

    def body(h_ref, g_ref, t_ref, f_ref, gate_ref, loss_ref, dh_ref, dg_ref, df_ref, dgate_ref):
        @pl.when(pl.program_id(0) == 0)
        def _():
            loss_ref[...] = jnp.zeros_like(loss_ref)
            dg_ref[...] = jnp.zeros_like(dg_ref)
            dgate_ref[...] = jnp.zeros_like(dgate_ref)

        x = h_ref[...]
        g = g_ref[...]
        r = lax.rsqrt(jnp.mean(x * x, axis=-1, keepdims=True) + RMS_EPS)
        xh = x * r
        err = xh * g - t_ref[...]
        part = 0.5 * jnp.sum(jnp.mean(err * err, axis=-1, keepdims=True), axis=0, keepdims=True)
        loss_ref[...] += jnp.broadcast_to(part, loss_ref.shape)
        dy = err * (1.0 / D)
        dg_ref[...] += jnp.sum(dy * xh, axis=0, keepdims=True)
        dxh = dy * g
        dh = r * (dxh - xh * jnp.mean(dxh * xh, axis=-1, keepdims=True))
        dh_ref[...] = dh
        _gate_back(dh, f_ref, gate_ref, coef, df_ref, dgate_ref)

    row = pl.BlockSpec((tr, D), lambda i: (i, 0))
    vec = _vec_spec(D, 1)
    return pl.pallas_call(
        body, name=name, grid=(S // tr,),
        in_specs=[row, vec, row, row, vec],
        out_specs=[pl.BlockSpec((1, 128), lambda i: (0, 0)), row, vec, row, vec],
        out_shape=[jax.ShapeDtypeStruct((1, 128), F32), jax.ShapeDtypeStruct((S, D), F32),
                   jax.ShapeDtypeStruct((1, D), F32), jax.ShapeDtypeStruct((S, D), BF16),
                   jax.ShapeDtypeStruct((1, D), F32)],
        compiler_params=_params(("arbitrary",)),
    )(h, gain, target, f, gate)


def _ffn_gate_up(n, wg_t, wu_t, name):
    S, D = n.shape
    nk, w, _ = wg_t.shape
    tm = _row_tile(S, 512)

    def body(n_ref, wg_ref, wu_ref, ga_ref, up_ref, act_ref):
        x = n_ref[...]
        ga = lax.dot_general(x, wg_ref[...], NT, preferred_element_type=F32)
        up = lax.dot_general(x, wu_ref[...], NT, preferred_element_type=F32)
        ga_ref[...] = ga.astype(BF16)
        up_ref[...] = up.astype(BF16)
        act_ref[...] = ((ga * _sigmoid(ga)) * up).astype(BF16)

    wspec = pl.BlockSpec((None, w, D), lambda k, m: (k, 0, 0))
    ospec = pl.BlockSpec((None, tm, w), lambda k, m: (k, m, 0))
    out = jax.ShapeDtypeStruct((nk, S, w), BF16)
    return pl.pallas_call(
        body, name=name, grid=(nk, S // tm),
        in_specs=[pl.BlockSpec((tm, D), lambda k, m: (m, 0)), wspec, wspec],
        out_specs=[ospec, ospec, ospec], out_shape=[out, out, out],
        compiler_params=_params(("parallel", "parallel")),
    )(n, wg_t, wu_t)


def _mm_residual(lhs, w, h_in, gvec, coef, name, next_norm=None):
    nk, S, kc = lhs.shape
    D = w.shape[2]
    tm = _row_tile(S, 512)
    n_in = 4 if next_norm is None else 7

    def body(*refs):
        l_ref, w_ref, h_ref, g_ref = refs[:4]
        ho_ref, f_ref = refs[n_in:n_in + 2]
        acc_ref = refs[-1]
        k = pl.program_id(1)

        @pl.when(k == 0)
        def _():
            acc_ref[...] = jnp.zeros_like(acc_ref)

        acc_ref[...] += jnp.dot(l_ref[...], w_ref[...], preferred_element_type=F32)

        @pl.when(k == nk - 1)
        def _():
            f = acc_ref[...]
            f_ref[...] = f.astype(BF16)
            x = h_ref[...] + (coef * g_ref[...]) * f
            ho_ref[...] = x
            if next_norm is not None:
                ng_ref, sc_ref, sh_ref = refs[4:7]
                r = lax.rsqrt(jnp.mean(x * x, axis=-1, keepdims=True) + RMS_EPS)
                y = (x * r) * ng_ref[...]
                refs[n_in + 2][...] = (y * (1.0 + sc_ref[...]) + sh_ref[...]).astype(BF16)

    row = pl.BlockSpec((tm, D), lambda m, k: (m, 0))
    vec = _vec_spec(D, 2)
    with_n = next_norm is not None
    return pl.pallas_call(
        body, name=name, grid=(S // tm, nk),
        in_specs=[pl.BlockSpec((None, tm, kc), lambda m, k: (k, m, 0)),
                  pl.BlockSpec((None, kc, D), lambda m, k: (k, 0, 0)), row, vec] + [vec] * (3 * with_n),
        out_specs=[row, row] + [row] * with_n,
        out_shape=[jax.ShapeDtypeStruct((S, D), F32), jax.ShapeDtypeStruct((S, D), BF16)]
        + [jax.ShapeDtypeStruct((S, D), BF16)] * with_n,
        scratch_shapes=[pltpu.VMEM((tm, D), F32)],
        compiler_params=_params(("parallel", "arbitrary")),
    )(lhs, w, h_in, gvec, *(next_norm or ()))


def _mm_cols(n, w, name):
    S, D = n.shape
    nk, _, wd = w.shape
    assert wd % 128 == 0
    tm = _row_tile(S, 512)

    def body(n_ref, w_ref, o_ref):
        o_ref[...] = jnp.dot(n_ref[...], w_ref[...], preferred_element_type=F32)

    return pl.pallas_call(
        body, name=name, grid=(nk, S // tm),
        in_specs=[pl.BlockSpec((tm, D), lambda k, m: (m, 0)), pl.BlockSpec((None, D, wd), lambda k, m: (k, 0, 0))],
        out_specs=pl.BlockSpec((tm, wd), lambda k, m: (m, k)),
        out_shape=jax.ShapeDtypeStruct((S, nk * wd), F32),
        compiler_params=_params(("parallel", "parallel")),
    )(n, w)


def _ffn_dact(df, wd, ga, up, name, deps=()):
    S, D = df.shape
    nk, w, _ = wd.shape
    tm = _row_tile(S, 512)
    deps = _deps(deps)

    def body(df_ref, wd_ref, ga_ref, up_ref, dga_ref, dup_ref):
        dact = lax.dot_general(df_ref[...], wd_ref[...], NT, preferred_element_type=F32)
        ga_ = ga_ref[...].astype(F32)
        up_ = up_ref[...].astype(F32)
        sig = _sigmoid(ga_)
        dga_ref[...] = (dact * up_ * (sig * (1.0 + ga_ * (1.0 - sig)))).astype(BF16)
        dup_ref[...] = (dact * (ga_ * sig)).astype(BF16)

    cspec = pl.BlockSpec((None, tm, w), lambda k, m: (k, m, 0))
    out = jax.ShapeDtypeStruct((nk, S, w), BF16)
    return pl.pallas_call(
        _blind_to(body, 4, len(deps)), name=name, grid=(nk, S // tm),
        in_specs=[pl.BlockSpec((tm, D), lambda k, m: (m, 0)), pl.BlockSpec((None, w, D), lambda k, m: (k, 0, 0)),
                  cspec, cspec] + [ANY_SPEC] * len(deps),
        out_specs=[cspec, cspec], out_shape=[out, out],
        compiler_params=_params(("parallel", "parallel")),
    )(df, wd, ga, up, *deps)


def _mm_nt(d, w, name):
    S, K = d.shape
    N = w.shape[0]
    tm = _row_tile(S, 512)

    def body(d_ref, w_ref, o_ref):
        o_ref[...] = lax.dot_general(d_ref[...], w_ref[...], NT, preferred_element_type=F32)

    return pl.pallas_call(
        body, name=name, grid=(S // tm,),
        in_specs=[pl.BlockSpec((tm, K), lambda m: (m, 0)), pl.BlockSpec((N, K), lambda m: (0, 0))],
        out_specs=pl.BlockSpec((tm, N), lambda m: (m, 0)),
        out_shape=jax.ShapeDtypeStruct((S, N), F32),
        compiler_params=_params(("parallel",)),
    )(d, w)


def _mm_reduce(lhs_list, w_list, chunked3d, w_is_kd, name, deps=()):
    nk = w_list[0].shape[0]
    kc, D = w_list[0].shape[1:] if w_is_kd else w_list[0].shape[:0:-1]
    S = lhs_list[0].shape[1] if chunked3d else lhs_list[0].shape[0]
    tm = _row_tile(S, 512)
    npair = len(lhs_list)
    deps = _deps(deps)

    def body(*refs):
        l_refs, w_refs = refs[:npair], refs[npair:2 * npair]
        o_ref, acc_ref = refs[2 * npair], refs[2 * npair + 1]
        k = pl.program_id(1)

        @pl.when(k == 0)
        def _():
            acc_ref[...] = jnp.zeros_like(acc_ref)

        for l_ref, w_ref in zip(l_refs, w_refs):
            if w_is_kd:
                acc_ref[...] += jnp.dot(l_ref[...], w_ref[...], preferred_element_type=F32)
            else:
                acc_ref[...] += lax.dot_general(l_ref[...], w_ref[...], NT, preferred_element_type=F32)

        @pl.when(k == nk - 1)
        def _():
            o_ref[...] = acc_ref[...]

    if chunked3d:
        lspec = pl.BlockSpec((None, tm, kc), lambda m, k: (k, m, 0))
    else:
        lspec = pl.BlockSpec((tm, kc), lambda m, k: (m, k))
    wspec = pl.BlockSpec((None,) + tuple(w_list[0].shape[1:]), lambda m, k: (k, 0, 0))
    return pl.pallas_call(
        _blind_to(body, 2 * npair, len(deps)), name=name, grid=(S // tm, nk),
        in_specs=[lspec] * npair + [wspec] * npair + [ANY_SPEC] * len(deps),
        out_specs=pl.BlockSpec((tm, D), lambda m, k: (m, 0)),
        out_shape=jax.ShapeDtypeStruct((S, D), F32),
        scratch_shapes=[pltpu.VMEM((tm, D), F32)],
        compiler_params=_params(("parallel", "arbitrary")),
    )(*lhs_list, *w_list, *deps)


def _wgrad_chunk_lhs(lhs_list, rhs, name, deps=()):
    nk, S, w = lhs_list[0].shape
    D = rhs.shape[1]
    ts = _row_tile(S, 512)
    ns = S // ts
    nl = len(lhs_list)
    deps = _deps(deps)

    def body(*refs):
        l_refs, r_ref = refs[:nl], refs[nl]
        o_refs, acc_refs = refs[nl + 1:2 * nl + 1], refs[2 * nl + 1:]
        s = pl.program_id(1)

        @pl.when(s == 0)
        def _():
            for acc_ref in acc_refs:
                acc_ref[...] = jnp.zeros_like(acc_ref)

        x = r_ref[...]
        for l_ref, acc_ref in zip(l_refs, acc_refs):
            acc_ref[...] += lax.dot_general(l_ref[...], x, TN, preferred_element_type=F32)

        @pl.when(s == ns - 1)
        def _():
            for o_ref, acc_ref in zip(o_refs, acc_refs):
                o_ref[...] = acc_ref[...]

    return pl.pallas_call(
        _blind_to(body, nl + 1, len(deps)), name=name, grid=(nk, ns),
        in_specs=[pl.BlockSpec((None, ts, w), lambda k, s: (k, s, 0))] * nl
        + [pl.BlockSpec((ts, D), lambda k, s: (s, 0))] + [ANY_SPEC] * len(deps),
        out_specs=[pl.BlockSpec((None, w, D), lambda k, s: (k, 0, 0))] * nl,
        out_shape=[jax.ShapeDtypeStruct((nk, w, D), F32)] * nl,
        scratch_shapes=[pltpu.VMEM((w, D), F32)] * nl,
        compiler_params=_params(("parallel", "arbitrary")),
    )(*lhs_list, rhs, *deps)


def _wgrad_chunk_rhs(lhs, rhs_list, nk, chunked3d, name, deps=()):
    S, D = lhs.shape
    w = rhs_list[0].shape[2] if chunked3d else rhs_list[0].shape[1] // nk
    ts = _row_tile(S, 512)
    ns = S // ts
    nr = len(rhs_list)
    deps = _deps(deps)

    def body(*refs):
        l_ref, r_refs = refs[0], refs[1:1 + nr]
        o_refs, acc_refs = refs[1 + nr:1 + 2 * nr], refs[1 + 2 * nr:]
        s = pl.program_id(1)

        @pl.when(s == 0)
        def _():
            for acc_ref in acc_refs:
                acc_ref[...] = jnp.zeros_like(acc_ref)

        x = l_ref[...]
        for r_ref, acc_ref in zip(r_refs, acc_refs):
            acc_ref[...] += lax.dot_general(x, r_ref[...], TN, preferred_element_type=F32)

        @pl.when(s == ns - 1)
        def _():
            for o_ref, acc_ref in zip(o_refs, acc_refs):
                o_ref[...] = acc_ref[...]

    if chunked3d:
        rspec = pl.BlockSpec((None, ts, w), lambda k, s: (k, s, 0))
    else:
        rspec = pl.BlockSpec((ts, w), lambda k, s: (s, k))
    ospec = pl.BlockSpec((None, D, w), lambda k, s: (k, 0, 0))
    return pl.pallas_call(
        _blind_to(body, 1 + nr, len(deps)), name=name, grid=(nk, ns),
        in_specs=[pl.BlockSpec((ts, D), lambda k, s: (s, 0))] + [rspec] * nr + [ANY_SPEC] * len(deps),
        out_specs=[ospec] * nr,
        out_shape=[jax.ShapeDtypeStruct((nk, D, w), F32)] * nr,
        scratch_shapes=[pltpu.VMEM((D, w), F32)] * nr,
        compiler_params=_params(("parallel", "arbitrary")),
    )(lhs, *rhs_list, *deps)


def _rope_tables(S):
    pos = np.arange(S, dtype=np.float32)
    inv_freq = (ROPE_THETA ** (-np.arange(0, HEAD_DIM, 2, dtype=np.float32) / HEAD_DIM)).astype(np.float32)
    ang = (pos[:, None] * inv_freq[None, :]).astype(np.float64)
    cos, sin = np.cos(ang).astype(np.float32), np.sin(ang).astype(np.float32)
    cos2 = np.concatenate([cos, cos, cos, cos], axis=1)
    sin2 = np.concatenate([-sin, sin, -sin, sin], axis=1)
    return jnp.asarray(cos2), jnp.asarray(sin2)


def _rotate(t, cos, sin_signed):
    half = HEAD_DIM // 2
    lane = lax.broadcasted_iota(jnp.int32, t.shape, 1)
    first = (lane % HEAD_DIM) < half
    partner = jnp.where(first, pltpu.roll(t, 128 - half, 1), pltpu.roll(t, half, 1))
    return t * cos + partner * sin_signed


def _qkv_rope(proj, cos, sin, name):
    S = proj.shape[0]
    A = ATTN_WIDTH
    tr = _row_tile(S, 512)
    nb = A // 128
    scale = HEAD_DIM ** -0.5

    def body(q_ref, k_ref, v_ref, c_ref, s_ref, qo_ref, ko_ref, vo_ref):
        c, s = c_ref[...], s_ref[...]
        qo_ref[...] = _rotate(q_ref[...], c, s) * scale
        ko_ref[...] = _rotate(k_ref[...], c, s)
        vo_ref[...] = v_ref[...]

    def col(off):
        return pl.BlockSpec((tr, 128), lambda i, j: (i, off + j))

    tab = pl.BlockSpec((tr, 128), lambda i, j: (i, 0))
    out = jax.ShapeDtypeStruct((S, A), F32)
    return pl.pallas_call(
        body, name=name, grid=(S // tr, nb),
        in_specs=[col(0), col(nb), col(2 * nb), tab, tab],
        out_specs=[col(0), col(0), col(0)], out_shape=[out, out, out],
        compiler_params=_params(("parallel", "parallel")),
    )(proj, proj, proj, cos, sin)


def _band_mask(T, has_prev):
    qi = lax.broadcasted_iota(jnp.int32, (T, 2 * T), 0)
    kj = lax.broadcasted_iota(jnp.int32, (T, 2 * T), 1)
    return ((kj < T) & (kj >= qi) & has_prev) | ((kj >= T) & (kj - T <= qi))


def _branch_blocks(rows, dilation):
    T = min(ATTN_BLOCK, rows // dilation)
    return T, rows // T


def _block_rows(base, T, dilation):
    if dilation == 1:
        return pl.ds(pl.multiple_of(base, T), T)
    return pl.ds(base, T, stride=dilation)


def _attn_fwd(q, k, v, name):
    S, A = q.shape
    sup = min(S, SUPER_ROWS)
    nd = len(DILATIONS)
    assert S % sup == 0

    def body(q_ref, k_ref, v_ref, attn_ref, lse_ref, acc_s, m_s, l_s):
        lane = lax.broadcasted_iota(jnp.int32, (1, 128), 1)
        head0 = lane < HEAD_DIM

        def supertile(st, carry):
            row0 = st * sup
            for di, dil in enumerate(DILATIONS):
                T, nblk = _branch_blocks(sup, dil)
                span = T * dil
                assert T == ATTN_BLOCK or sup == S

                def block(idx, c2, di=di, dil=dil, T=T, span=span):
                    r = idx % dil
                    loc = (idx // dil) * span + r
                    base = row0 + loc
                    rows = _block_rows(base, T, dil)
                    prev = _block_rows(jnp.maximum(base - span, r), T, dil)
                    qb = q_ref[rows, :].astype(BF16)
                    k2 = jnp.concatenate([k_ref[prev, :], k_ref[rows, :]], axis=0).astype(BF16)
                    v2 = jnp.concatenate([v_ref[prev, :], v_ref[rows, :]], axis=0).astype(BF16)
                    valid = _band_mask(T, base >= span)
                    accs, ms, ls = [], [], []
                    for hmask in (head0, jnp.logical_not(head0)):
                        qh = jnp.where(hmask, qb, jnp.zeros_like(qb))
                        s = jnp.where(valid, lax.dot_general(qh, k2, NT, preferred_element_type=F32), NEG)
                        m = jnp.max(s, axis=-1, keepdims=True)
                        p = jnp.exp(s - m)
                        ls.append(jnp.sum(p, axis=-1, keepdims=True))
                        ms.append(m)
                        accs.append(jnp.dot(p.astype(BF16), v2, preferred_element_type=F32))
                    lrows = _block_rows(di * sup + loc, T, dil)
                    acc_s[lrows, :] = jnp.where(head0, accs[0], accs[1])
                    m_s[lrows, :] = jnp.where(head0, ms[0], ms[1])
                    l_s[lrows, :] = jnp.where(head0, ls[0], ls[1])
                    return c2

                lax.fori_loop(0, nblk, block, 0, unroll=2)

            chunk = min(sup, 256)

            def merge(ci, c2):
                lr = [pl.ds(pl.multiple_of(di * sup + ci * chunk, chunk), chunk) for di in range(nd)]
                gr = pl.ds(pl.multiple_of(row0 + ci * chunk, chunk), chunk)
                m0, m1, m2 = m_s[lr[0], :], m_s[lr[1], :], m_s[lr[2], :]
                mm = jnp.maximum(jnp.maximum(m0, m1), m2)
                w0, w1, w2 = jnp.exp(m0 - mm), jnp.exp(m1 - mm), jnp.exp(m2 - mm)
                den = (w0 * l_s[lr[0], :] + w1 * l_s[lr[1], :]) + w2 * l_s[lr[2], :]
                num = (w0 * acc_s[lr[0], :] + w1 * acc_s[lr[1], :]) + w2 * acc_s[lr[2], :]
                attn_ref[gr, :] = num / den
                lse_ref[gr, :] = mm + jnp.log(den)
                return c2

            lax.fori_loop(0, sup // chunk, merge, 0)
            return carry

        lax.fori_loop(0, S // sup, supertile, 0)

    blk = pl.BlockSpec((S, 128), lambda j: (0, j))
    out = jax.ShapeDtypeStruct((S, A), F32)
    return pl.pallas_call(
        body, name=name, grid=(A // 128,),
        in_specs=[blk, blk, blk], out_specs=[blk, blk], out_shape=[out, out],
        scratch_shapes=[pltpu.VMEM((nd * sup, 128), F32)] * 3,
        compiler_params=_params(("parallel",)),
    )(q, k, v)


def _attn_out_bwd(dy, attn, gain, name):
    S, A = attn.shape
    tr = _row_tile(S, 256)

    def body(dy_ref, a_ref, g_ref, da_ref, dl_ref, dg_ref):
        @pl.when(pl.program_id(0) == 0)
        def _():
            dg_ref[...] = jnp.zeros_like(dg_ref)

        x = a_ref[...]
        dy_ = dy_ref[...]
        r = lax.rsqrt(jnp.mean(x * x, axis=-1, keepdims=True) + RMS_EPS)
        xh = x * r
        dg_ref[...] += jnp.sum(dy_ * xh, axis=0, keepdims=True)
        dxh = dy_ * g_ref[...]
        dx = r * (dxh - xh * jnp.mean(dxh * xh, axis=-1, keepdims=True))
        da_ref[...] = dx
        prod = dx * x
        hi = lax.broadcasted_iota(jnp.int32, (A, A), 0) // HEAD_DIM
        hj = lax.broadcasted_iota(jnp.int32, (A, A), 1) // HEAD_DIM
        same_head = (hi == hj).astype(F32)
        dl_ref[...] = jnp.dot(prod, same_head, preferred_element_type=F32, precision=lax.Precision.HIGHEST)

    row = pl.BlockSpec((tr, A), lambda i: (i, 0))
    vec = _vec_spec(A, 1)
    return pl.pallas_call(
        body, name=name, grid=(S // tr,),
        in_specs=[row, row, vec], out_specs=[row, row, vec],
        out_shape=[jax.ShapeDtypeStruct((S, A), F32), jax.ShapeDtypeStruct((S, A), F32),
                   jax.ShapeDtypeStruct((1, A), F32)],
        compiler_params=_params(("arbitrary",)),
    )(dy, attn, gain)


def _attn_bwd(q, k, v, da, lse, delta, name):
    S, A = q.shape

    def body(q_ref, k_ref, v_ref, da_ref, lse_ref, dl_ref, dq_ref, dk_ref, dv_ref):
        lane = lax.broadcasted_iota(jnp.int32, (1, 128), 1)
        head0 = lane < HEAD_DIM
        dq_ref[...] = jnp.zeros_like(dq_ref)
        dk_ref[...] = jnp.zeros_like(dk_ref)
        dv_ref[...] = jnp.zeros_like(dv_ref)
        for dil in DILATIONS:
            T, nblk = _branch_blocks(S, dil)
            span = T * dil

            def block(idx, carry, dil=dil, T=T, span=span):
                r = idx % dil
                base = (idx // dil) * span + r
                rows = _block_rows(base, T, dil)
                prev = _block_rows(jnp.maximum(base - span, r), T, dil)
                qb, dab = q_ref[rows, :].astype(BF16), da_ref[rows, :].astype(BF16)
                k2 = jnp.concatenate([k_ref[prev, :], k_ref[rows, :]], axis=0).astype(BF16)
                v2 = jnp.concatenate([v_ref[prev, :], v_ref[rows, :]], axis=0).astype(BF16)
                lse_b, dl_b = lse_ref[rows, :], dl_ref[rows, :]
                valid = _band_mask(T, base >= span)
                dqs = []
                dk2 = dv2 = None
                for h, hmask in enumerate((head0, jnp.logical_not(head0))):
                    qh = jnp.where(hmask, qb, jnp.zeros_like(qb))
                    dah = jnp.where(hmask, dab, jnp.zeros_like(dab))
                    c0 = h * HEAD_DIM
                    lse_h = lse_b[:, c0:c0 + 1]
                    dl_h = dl_b[:, c0:c0 + 1]
                    s = lax.dot_general(qh, k2, NT, preferred_element_type=F32)
                    p = jnp.where(valid, jnp.exp(s - lse_h), 0.0)
                    dp = lax.dot_general(dah, v2, NT, preferred_element_type=F32)
                    ds = (p * (dp - dl_h)).astype(BF16)
                    dqs.append(jnp.dot(ds, k2, preferred_element_type=F32))
                    t_k = lax.dot_general(ds, qh, TN, preferred_element_type=F32)
                    t_v = lax.dot_general(p.astype(BF16), dah, TN, preferred_element_type=F32)
                    dk2 = t_k if dk2 is None else dk2 + t_k
                    dv2 = t_v if dv2 is None else dv2 + t_v
                dq_ref[rows, :] += jnp.where(head0, dqs[0], dqs[1])
                dk_ref[rows, :] += dk2[T:]
                dv_ref[rows, :] += dv2[T:]
                dk_ref[prev, :] += dk2[:T]
                dv_ref[prev, :] += dv2[:T]
                return carry

            lax.fori_loop(0, nblk, block, 0, unroll=2)

    blk = pl.BlockSpec((S, 128), lambda j: (0, j))
    out = jax.ShapeDtypeStruct((S, A), F32)
    return pl.pallas_call(
        body, name=name, grid=(A // 128,),
        in_specs=[blk] * 6, out_specs=[blk] * 3, out_shape=[out] * 3,
        compiler_params=_params(("parallel",)),
    )(q, k, v, da, lse, delta)


def _glu_window(a_ref, b_ref, ah_ref, bh_ref, first):
    u0 = a_ref[...] * _sigmoid(b_ref[...])
    u0h = ah_ref[...] * _sigmoid(bh_ref[...])
    u0h = jnp.where(first, jnp.zeros_like(u0h), u0h)
    return jnp.concatenate([u0h, u0], axis=0)


def _conv_norms(u1, lng, lnb):
    mu = jnp.mean(u1, axis=-1, keepdims=True)
    xc = u1 - mu
    rstd = lax.rsqrt(jnp.mean(xc * xc, axis=-1, keepdims=True) + LN_EPS)
    u1h = xc * rstd
    u2 = u1h * lng + lnb
    sig = _sigmoid(u2)
    u3 = u2 * sig
    r = lax.rsqrt(jnp.mean(u3 * u3, axis=-1, keepdims=True) + RMS_EPS)
    return rstd, u1h, u2, sig, u3, r


def _conv_specs(tr, C, col_a, col_b):
    per = tr // HALO

    def tile(col):
        return pl.BlockSpec((tr, C), lambda i: (i, col))

    def halo(col):
        return pl.BlockSpec((HALO, C), lambda i: (jnp.maximum(i * per - 1, 0), col))

    return tile(col_a), tile(col_b), halo(col_a), halo(col_b)


def _mixer_merge(proj, attn, cw, cb, lng, lnb, gat, gco, name):
    S = proj.shape[0]
    C = CONV_WIDTH
    A = attn.shape[1]
    tr = _row_tile(S, 256)

    def body(a_ref, b_ref, ah_ref, bh_ref, at_ref, w_ref, cb_ref, lng_ref, lnb_ref, gat_ref, gco_ref, u1_ref, y_ref):
        win = _glu_window(a_ref, b_ref, ah_ref, bh_ref, pl.program_id(0) == 0)
        acc = jnp.broadcast_to(cb_ref[...], (tr, C))
        for j in range(CONV_KERNEL):
            off = HALO - (CONV_KERNEL - 1) + j
            acc = acc + w_ref[j:j + 1, :] * win[off:off + tr, :]
        u1_ref[...] = acc
        _, _, _, _, u3, r = _conv_norms(acc, lng_ref[...], lnb_ref[...])
        y_ref[:, A:] = ((u3 * r) * gco_ref[...]).astype(BF16)
        x = at_ref[...]
        ra = lax.rsqrt(jnp.mean(x * x, axis=-1, keepdims=True) + RMS_EPS)
        y_ref[:, :A] = ((x * ra) * gat_ref[...]).astype(BF16)

    ta, tb, ha, hb = _conv_specs(tr, C, 3, 4)
    row = pl.BlockSpec((tr, C), lambda i: (i, 0))
    vec = _vec_spec(C, 1)
    return pl.pallas_call(
        body, name=name, grid=(S // tr,),
        in_specs=[ta, tb, ha, hb, pl.BlockSpec((tr, A), lambda i: (i, 0)), pl.BlockSpec((HALO, C), lambda i: (0, 0)),
                  vec, vec, vec, _vec_spec(A, 1), vec],
        out_specs=[row, pl.BlockSpec((tr, A + C), lambda i: (i, 0))],
        out_shape=[jax.ShapeDtypeStruct((S, C), F32), jax.ShapeDtypeStruct((S, A + C), BF16)],
        compiler_params=_params(("parallel",)),
    )(proj, proj, proj, proj, attn, cw, cb, lng, lnb, gat, gco)


def _conv_bwd_norms(dy, u1, lng, lnb, gco, name):
    S, C = u1.shape
    tr = _row_tile(S, 256)

    def body(dy_ref, u1_ref, lng_ref, lnb_ref, gco_ref, du1_ref, dgco_ref, dlng_ref, dlnb_ref, dcb_ref):
        @pl.when(pl.program_id(0) == 0)
        def _():
            for ref in (dgco_ref, dlng_ref, dlnb_ref, dcb_ref):
                ref[...] = jnp.zeros_like(ref)

        lng = lng_ref[...]
        rstd, u1h, u2, sig, u3, r = _conv_norms(u1_ref[...], lng, lnb_ref[...])
        dy_ = dy_ref[...]
        u3h = u3 * r
        dgco_ref[...] += jnp.sum(dy_ * u3h, axis=0, keepdims=True)
        du3h = dy_ * gco_ref[...]
        du3 = r * (du3h - u3h * jnp.mean(du3h * u3h, axis=-1, keepdims=True))
        du2 = du3 * (sig * (1.0 + u2 * (1.0 - sig)))
        dlng_ref[...] += jnp.sum(du2 * u1h, axis=0, keepdims=True)
        dlnb_ref[...] += jnp.sum(du2, axis=0, keepdims=True)
        du1h = du2 * lng
        du1 = rstd * (du1h - jnp.mean(du1h, axis=-1, keepdims=True)
                      - u1h * jnp.mean(du1h * u1h, axis=-1, keepdims=True))
        du1_ref[...] = du1
        dcb_ref[...] += jnp.sum(du1, axis=0, keepdims=True)

    row = pl.BlockSpec((tr, C), lambda i: (i, 0))
    vec = _vec_spec(C, 1)
    return pl.pallas_call(
        body, name=name, grid=(S // tr,),
        in_specs=[pl.BlockSpec((tr, C), lambda i: (i, 1)), row, vec, vec, vec],
        out_specs=[row, vec, vec, vec, vec],
        out_shape=[jax.ShapeDtypeStruct((S, C), F32)] + [jax.ShapeDtypeStruct((1, C), F32)] * 4,
        compiler_params=_params(("arbitrary",)),
    )(dy, u1, lng, lnb, gco)


def _dproj(du1, proj, cw, dq, dk, dv, cos, sin, name):
    S, C = du1.shape
    A = dq.shape[1]
    tr = _row_tile(S, 256)
    nt = S // tr
    per = tr // HALO
    scale = HEAD_DIM ** -0.5

    def body(du_ref, dun_ref, a_ref, b_ref, ah_ref, bh_ref, w_ref, dq_ref, dk_ref, dv_ref, cos_ref, sin_ref,
             dp_ref, dw_ref):
        i = pl.program_id(0)

        @pl.when(i == 0)
        def _():
            dw_ref[...] = jnp.zeros_like(dw_ref)

        win = _glu_window(a_ref, b_ref, ah_ref, bh_ref, i == 0)
        du = du_ref[...]
        nxt = jnp.where(i == nt - 1, jnp.zeros_like(dun_ref[...]), dun_ref[...])
        dwin = jnp.concatenate([du, nxt], axis=0)
        du0 = jnp.zeros((tr, C), F32)
        taps = []
        for j in range(CONV_KERNEL):
            back = CONV_KERNEL - 1 - j
            du0 = du0 + w_ref[j:j + 1, :] * dwin[back:back + tr, :]
            off = HALO - (CONV_KERNEL - 1) + j
            taps.append(jnp.sum(du * win[off:off + tr, :], axis=0, keepdims=True))
        taps.append(jnp.zeros((HALO - CONV_KERNEL, C), F32))
        dw_ref[...] += jnp.concatenate(taps, axis=0)
        a, sig = a_ref[...], _sigmoid(b_ref[...])
        dp_ref[:, 3 * A:3 * A + C] = (du0 * sig).astype(BF16)
        dp_ref[:, 3 * A + C:] = (du0 * a * sig * (1.0 - sig)).astype(BF16)
        cos_, nsin = cos_ref[...], -sin_ref[...]
        for j in range(A // 128):
            lanes = slice(j * 128, (j + 1) * 128)
            dp_ref[:, j * 128:(j + 1) * 128] = (_rotate(dq_ref[:, lanes], cos_, nsin) * scale).astype(BF16)
            dp_ref[:, A + j * 128:A + (j + 1) * 128] = _rotate(dk_ref[:, lanes], cos_, nsin).astype(BF16)
        dp_ref[:, 2 * A:3 * A] = dv_ref[...].astype(BF16)

    ta, tb, ha, hb = _conv_specs(tr, C, 3, 4)
    row = pl.BlockSpec((tr, C), lambda i: (i, 0))
    arow = pl.BlockSpec((tr, A), lambda i: (i, 0))
    tab = pl.BlockSpec((tr, 128), lambda i: (i, 0))
    nxt = pl.BlockSpec((HALO, C), lambda i: (jnp.minimum((i + 1) * per, S // HALO - 1), 0))
    wspec = pl.BlockSpec((HALO, C), lambda i: (0, 0))
    return pl.pallas_call(
        body, name=name, grid=(nt,),
        in_specs=[row, nxt, ta, tb, ha, hb, wspec, arow, arow, arow, tab, tab],
        out_specs=[pl.BlockSpec((tr, 3 * A + 2 * C), lambda i: (i, 0)), wspec],
        out_shape=[jax.ShapeDtypeStruct((S, 3 * A + 2 * C), BF16), jax.ShapeDtypeStruct((HALO, C), F32)],
        compiler_params=_params(("arbitrary",)),
    )(du1, du1, proj, proj, proj, proj, cw, dq, dk, dv, cos, sin)


def _ada_fwd(c_all, w, b, name):
    B, D = c_all.shape
    N = w.shape[1]
    tn = 768 if N % 768 == 0 else N

    def body(c_ref, w_ref, b_ref, o_ref):
        c = c_ref[...]
        a = (c * _sigmoid(c)).astype(BF16)
        o_ref[...] = jnp.dot(a, w_ref[...].astype(BF16), preferred_element_type=F32) + b_ref[...]

    return pl.pallas_call(
        body, name=name, grid=(N // tn,),
        in_specs=[pl.BlockSpec((B, D), lambda j: (0, 0)), pl.BlockSpec((D, tn), lambda j: (0, j)),
                  pl.BlockSpec((1, tn), lambda j: (0, j))],
        out_specs=pl.BlockSpec((B, tn), lambda j: (0, j)),
        out_shape=jax.ShapeDtypeStruct((B, N), F32),
        compiler_params=_params(("parallel",)),
    )(c_all, w, b)


def _ada_wgrad(c_t, dmod, name):
    D, B = c_t.shape
    N = dmod.shape[1]
    tn = 768 if N % 768 == 0 else N

    def body(c_ref, d_ref, o_ref):
        c = c_ref[...]
        a = (c * _sigmoid(c)).astype(BF16)
        o_ref[...] = jnp.dot(a, d_ref[...].astype(BF16), preferred_element_type=F32)

    return pl.pallas_call(
        body, name=name, grid=(N // tn,),
        in_specs=[pl.BlockSpec((D, B), lambda j: (0, 0)), pl.BlockSpec((B, tn), lambda j: (0, j))],
        out_specs=pl.BlockSpec((D, tn), lambda j: (0, j)),
        out_shape=jax.ShapeDtypeStruct((D, N), F32),
        compiler_params=_params(("parallel",)),
    )(c_t, dmod)


def _sum_rows(x, name):
    R, N = x.shape

    def body(x_ref, o_ref):
        acc = x_ref[0:1, :]
        for r in range(1, R):
            acc = acc + x_ref[r:r + 1, :]
        o_ref[...] = acc

    return pl.pallas_call(
        body, name=name, out_shape=jax.ShapeDtypeStruct((1, N), F32),
        compiler_params=_params(),
    )(x)


def _adamw(w, g, m, v, name, deps=()):
    R, C = w.shape
    tr = _row_tile(R, 256) if R % 256 == 0 else R
    bc1 = 1.0 - ADAM_B1 ** ADAM_STEP
    bc2 = 1.0 - ADAM_B2 ** ADAM_STEP
    deps = _deps(deps)

    def body(w_ref, g_ref, m_ref, v_ref, d_ref, mo_ref, vo_ref):
        g_ = g_ref[...]
        m_ = ADAM_B1 * m_ref[...] + (1.0 - ADAM_B1) * g_
        v_ = ADAM_B2 * v_ref[...] + (1.0 - ADAM_B2) * (g_ * g_)
        mo_ref[...] = m_
        vo_ref[...] = v_
        d_ref[...] = -ADAM_LR * ((m_ / bc1) / (jnp.sqrt(v_ / bc2) + ADAM_EPS) + ADAM_WD * w_ref[...])

    row = pl.BlockSpec((tr, C), lambda i: (i, 0))
    out = jax.ShapeDtypeStruct((R, C), F32)
    return pl.pallas_call(
        _blind_to(body, 4, len(deps)), name=name, grid=(R // tr,),
        in_specs=[row] * 4 + [ANY_SPEC] * len(deps), out_specs=[row] * 3, out_shape=[out] * 3,
        compiler_params=_params(("parallel",)),
    )(w, g, m, v, *deps)


def _coords():
    return lax.axis_index("x"), lax.axis_index("y"), lax.axis_index("c")


def _all_gather8(x, name, deps=()):
    R, N = x.shape
    assert R == 8
    flips = [(fx, fy, fc) for fx in (0, 1) for fy in (0, 1) for fc in (0, 1)][1:]
    deps = _deps(deps)

    def body(x_ref, o_ref, send_sems, recv_sems):
        mx, my, mc = _coords()
        me = 4 * mx + 2 * my + mc

        def rows(dev):
            return o_ref.at[pl.ds(pl.multiple_of(dev * R, R), R), :]

        o_ref[pl.ds(pl.multiple_of(me * R, R), R), :] = x_ref[...]
        copies = []
        for t, (fx, fy, fc) in enumerate(flips):
            peer = (mx ^ fx, my ^ fy, mc ^ fc)
            copies.append(pltpu.make_async_remote_copy(
                src_ref=x_ref, dst_ref=rows(me), send_sem=send_sems.at[t], recv_sem=recv_sems.at[t],
                device_id=peer, device_id_type=MESH))
        for cp in copies:
            cp.start()
        for t, (fx, fy, fc) in enumerate(flips):
            peer_id = 4 * (mx ^ fx) + 2 * (my ^ fy) + (mc ^ fc)
            pltpu.make_async_remote_copy(
                src_ref=x_ref, dst_ref=rows(peer_id), send_sem=send_sems.at[t], recv_sem=recv_sems.at[t],
                device_id=(mx ^ fx, my ^ fy, mc ^ fc), device_id_type=MESH).wait_recv()
        for cp in copies:
            cp.wait_send()

    return pl.pallas_call(
        _blind_to(body, 1, len(deps)), name=name,
        in_specs=[pl.BlockSpec(memory_space=pltpu.VMEM)] + [ANY_SPEC] * len(deps),
        out_specs=pl.BlockSpec(memory_space=pltpu.VMEM),
        out_shape=jax.ShapeDtypeStruct((N_DEV * R, N), F32),
        scratch_shapes=[pltpu.SemaphoreType.DMA((7,)), pltpu.SemaphoreType.DMA((7,))],
        compiler_params=pltpu.CompilerParams(has_side_effects=True, vmem_limit_bytes=VMEM_LIMIT_BYTES),
    )(x, *deps)


def _half_rows(rows, half):
    return pl.ds(pl.multiple_of(half * (rows // 2), 8), rows // 2)


def _split_start(bufs, plan, n, name):
    nb = len(bufs)

    def body(*refs):
        send_sems, recv_sems, token = refs[nb], refs[nb + 1], refs[-1]
        for t, (src, dst, dev) in enumerate(plan(refs[:nb])):
            pltpu.make_async_remote_copy(src_ref=src, dst_ref=dst, send_sem=send_sems.at[t],
                                         recv_sem=recv_sems.at[t], device_id=dev, device_id_type=MESH).start()
        token[...] = jnp.zeros_like(token)

    out = pl.pallas_call(
        body, name=name,
        out_shape=(pltpu.SemaphoreType.DMA((n,)), pltpu.SemaphoreType.DMA((n,)),
                   *[pltpu.HBM(b.shape, b.dtype) for b in bufs], jax.ShapeDtypeStruct((8, 128), F32)),
        in_specs=[HBM_SPEC] * nb,
        out_specs=(SEM_SPEC, SEM_SPEC, *[HBM_SPEC] * nb, pl.BlockSpec(memory_space=pltpu.VMEM)),
        input_output_aliases={i: 2 + i for i in range(nb)},
        compiler_params=pltpu.CompilerParams(has_side_effects=DATAFLOW),
    )(*[pltpu.with_memory_space_constraint(b, pltpu.HBM) for b in bufs])
    return out[0], out[1], list(out[2:2 + nb]), out[-1]


def _split_wait(bufs, send_sems, recv_sems, plan, after, name):
    nb = len(bufs)
    after = _deps(after)

    def body(*refs):
        ss, rs = refs[nb], refs[nb + 1]
        for t, (src, dst, dev) in enumerate(plan(refs[:nb])):
            cp = pltpu.make_async_remote_copy(src_ref=src, dst_ref=dst, send_sem=ss.at[t], recv_sem=rs.at[t],
                                              device_id=dev, device_id_type=MESH)
            cp.wait_send()
            cp.wait_recv()

    out = pl.pallas_call(
        body, name=name,
        out_shape=tuple(pltpu.HBM(b.shape, b.dtype) for b in bufs),
        in_specs=[HBM_SPEC] * nb + [SEM_SPEC, SEM_SPEC] + [ANY_SPEC] * len(after),
        out_specs=tuple([HBM_SPEC] * nb),
        input_output_aliases={i: i for i in range(nb)},
        compiler_params=pltpu.CompilerParams(has_side_effects=DATAFLOW),
    )(*bufs, send_sems, recv_sems, *after)
    return list(out)


class _Exchange:
    def __init__(self, bufs, plan, n, name):
        self.plan, self.name = plan, name
        self.send_sems, self.recv_sems, self.bufs, self.token = _split_start(bufs, plan, n, name + "_start")

    def wait(self, after):
        return _split_wait(self.bufs, self.send_sems, self.recv_sems, self.plan, after, self.name + "_wait")


def _cast_place(w, chip_idx, name, deps=()):
    R, C = w.shape
    tr = _row_tile(R, 256) if R % 256 == 0 else R
    deps = _deps(deps)

    def body(k_ref, w_ref, o_ref):
        o_ref[...] = w_ref[...].astype(BF16)

    grid_spec = pltpu.PrefetchScalarGridSpec(
        num_scalar_prefetch=1, grid=(R // tr,),
        in_specs=[pl.BlockSpec((tr, C), lambda i, k: (i, 0))] + [ANY_SPEC] * len(deps),
        out_specs=pl.BlockSpec((None, tr, C), lambda i, k: (k[0], i, 0)),
    )
    return pl.pallas_call(
        _blind_to(body, 2, len(deps)), name=name, grid_spec=grid_spec,
        out_shape=jax.ShapeDtypeStruct((N_CHIPS, R, C), BF16),
        compiler_params=_params(("parallel",)),
    )(chip_idx, w, *deps)


def _plan_gather(refs):
    mx, my, mc = _coords()
    me = 2 * mx + my
    plan = []
    for g in refs:
        mine = g.at[me, _half_rows(g.shape[1], mc), :]
        for fx, fy in XY_FLIPS:
            plan.append((mine, mine, (mx ^ fx, my ^ fy, mc)))
    return plan


def _plan_swap(refs):
    mx, my, mc = _coords()
    plan = []
    for g in refs:
        for fx, fy in XY_FLIPS:
            have = g.at[2 * (mx ^ fx) + (my ^ fy), _half_rows(g.shape[1], mc), :]
            plan.append((have, have, (mx, my, 1 - mc)))
    return plan


def _plan_other_halves(refs):
    n = len(refs) // 2
    mx, my, mc = _coords()
    return [(g.at[pl.ds(0, N_CHIPS), _half_rows(g.shape[1], 1 - mc), :], land, (mx, my, 1 - mc))
            for g, land in zip(refs[:n], refs[n:])]


def _plan_chunks(refs):
    n = len(refs) // 2
    mx, my, mc = _coords()
    plan = []
    for s, land in zip(refs[:n], refs[n:]):
        for t, (fx, fy) in enumerate(XY_FLIPS):
            plan.append((s.at[2 * (mx ^ fx) + (my ^ fy)], land.at[t], (mx ^ fx, my ^ fy, mc)))
    return plan


def _plan_share(refs):
    mx, my, mc = _coords()
    return [(full.at[mc], full.at[mc], (mx, my, 1 - mc)) for full in refs]


def _add_half(g, recv, core_idx, name):
    nk, R, C = g.shape
    rh = R // 2
    tr = _row_tile(rh, 128) if rh % 128 == 0 else rh
    nt = rh // tr

    def body(c_ref, g_ref, r_ref, o_ref):
        o_ref[...] = (g_ref[...] + r_ref[...]).astype(BF16)

    grid_spec = pltpu.PrefetchScalarGridSpec(
        num_scalar_prefetch=1, grid=(nk, nt),
        in_specs=[pl.BlockSpec((None, tr, C), lambda k, i, c: (k, c[0] * nt + i, 0)),
                  pl.BlockSpec((None, tr, C), lambda k, i, c: (k, i, 0))],
        out_specs=pl.BlockSpec((None, tr, C), lambda k, i, c: (k, i, 0)),
    )
    return pl.pallas_call(
        body, name=name, grid_spec=grid_spec, out_shape=jax.ShapeDtypeStruct((nk, rh, C), BF16),
        compiler_params=_params(("parallel", "parallel")),
    )(core_idx, g, recv)


def _sum_chips(s, land, chip_core, name):
    _, rh, C = s.shape
    tr = _row_tile(rh, 128) if rh % 128 == 0 else rh

    def body(p_ref, s_ref, l_ref, o_ref):
        me = p_ref[0]
        acc = None
        for j in range(N_CHIPS):
            t = jnp.maximum(jnp.bitwise_xor(me, j) - 1, 0)
            term = jnp.where(me == j, s_ref[...], l_ref[t]).astype(F32)
            acc = term if acc is None else acc + term
        o_ref[...] = acc

    grid_spec = pltpu.PrefetchScalarGridSpec(
        num_scalar_prefetch=1, grid=(rh // tr,),
        in_specs=[pl.BlockSpec((None, tr, C), lambda i, p: (p[0], i, 0)),
                  pl.BlockSpec((3, tr, C), lambda i, p: (0, i, 0))],
        out_specs=pl.BlockSpec((None, tr, C), lambda i, p: (p[1], i, 0)),
    )
    return pl.pallas_call(
        body, name=name, grid_spec=grid_spec, out_shape=jax.ShapeDtypeStruct((2, rh, C), F32),
        compiler_params=_params(("parallel",)),
    )(chip_core, s, land)


def _rs_send_halves(grads, tag):
    lands = [lax.empty((g.shape[0], g.shape[1] // 2, g.shape[2]), g.dtype) for g in grads]
    return _Exchange(list(grads) + lands, _plan_other_halves, len(grads), f"rs_halves_{tag}")


def _rs_send_chunks(ex, after, core_idx, tag):
    bufs = ex.wait(after)
    n = len(bufs) // 2
    sums = [_add_half(g, r, core_idx, f"rs_add_{tag}_{i}") for i, (g, r) in enumerate(zip(bufs[:n], bufs[n:]))]
    lands = [lax.empty((3,) + s.shape[1:], s.dtype) for s in sums]
    return _Exchange(sums + lands, _plan_chunks, 3 * n, f"rs_chunks_{tag}")


def _rs_send_share(ex, after, chip_core, tag):
    bufs = ex.wait(after)
    n = len(bufs) // 2
    fulls = [_sum_chips(s, l, chip_core, f"rs_sum_{tag}_{i}") for i, (s, l) in enumerate(zip(bufs[:n], bufs[n:]))]
    return _Exchange(fulls, _plan_share, n, f"rs_share_{tag}")


def _rs_finish(ex, after):
    return [b.reshape(2 * b.shape[1], b.shape[2]) for b in ex.wait(after)]


def _ffn_forward(h, n, gate, wg, wu, wd, tag, next_norm=None):
    ga, up, act = _ffn_gate_up(n, wg, wu, f"{tag}_gate_up")
    h_out, f, *n_next = _mm_residual(act, wd, h, gate, 0.5, f"{tag}_down", next_norm=next_norm)
    return (h_out, *n_next), (h, n, ga, up, act, f)


def _ffn_backward(dh_out, df, saved, gain, sc, wg, wu, wd, core_idx, tag, prev=None, last=False):
    h, n, ga, up, act, _ = saved
    (dwd,) = _wgrad_chunk_lhs([act], df, f"{tag}_dwd")
    ex_d = _rs_send_halves([dwd], f"{tag}_d")
    dga, dup = _ffn_dact(df, wd, ga, up, f"{tag}_dact", deps=[ex_d.token])
    ex_d = _rs_send_chunks(ex_d, [dga], core_idx, f"{tag}_d")
    dwg, dwu = _wgrad_chunk_lhs([dga, dup], n, f"{tag}_dwgu", deps=[ex_d.token])
    ex_gu = _rs_send_halves([dwg, dwu], f"{tag}_gu")
    dn = _mm_reduce([dga, dup], [wg, wu], True, True, f"{tag}_dn", deps=[ex_gu.token])
    if not last:
        ex_gu = _rs_send_chunks(ex_gu, [dn], core_idx, f"{tag}_gu")
    outs = _norm_mod_bwd(dn, h, gain, sc, dh_out, f"{tag}_norm_bwd", prev=prev, deps=[ex_gu.token])
    return outs, (ex_d, ex_gu)


def _pad_cols(v, n):
    return jnp.pad(v, ((0, 0), (0, n - v.shape[1])))


def _mixer_forward(h1, n2, gt2, win, wout, conv_w, conv_dw_b, conv_ln_g, conv_ln_b, attn_out_g, conv_out_g,
                   next_norm=None):
    S, D = h1.shape
    proj = _mm_cols(n2, win, "mix_in")
    cos, sin = _rope_tables(S)
    q, k, v = _qkv_rope(proj, cos, sin, "qkv_rope")
    attn, lse = _attn_fwd(q, k, v, "attn_fwd")
    u1, y = _mixer_merge(proj, attn, conv_w, conv_dw_b, conv_ln_g, conv_ln_b, attn_out_g, conv_out_g, "mix_merge")
    h2, mo, *n_next = _mm_residual(y[None], wout.reshape(1, D, D), h1, gt2, 1.0, "mix_out", next_norm=next_norm)
    return (h2, *n_next), (h1, n2, proj, cos, sin, q, k, v, attn, lse, u1, y, mo)


def _mixer_backward(dh2, dmo, saved, mix_norm_g, sc2, win, wout, conv_w, conv_ln_g, conv_ln_b, attn_out_g,
                    conv_out_g, core_idx, prev=None):
    h1, n2, proj, cos, sin, q, k, v, attn, lse, u1, y, _ = saved
    S, D = h1.shape
    (dwout,) = _wgrad_chunk_lhs([y[None]], dmo, "mix_dwout")
    dy = _mm_nt(dmo, wout.reshape(D, D), "mix_dy")
    dattn, delta, d_attn_g = _attn_out_bwd(dy, attn, attn_out_g, "attn_out_bwd")
    dq, dk, dv = _attn_bwd(q, k, v, dattn, lse, delta, "attn_bwd")
    du1, d_gco, d_lng, d_lnb, d_cb = _conv_bwd_norms(dy, u1, conv_ln_g, conv_ln_b, conv_out_g, "conv_bwd_norms")
    dproj, d_cw = _dproj(du1, proj, conv_w, dq, dk, dv, cos, sin, "mix_dproj")
    (dwin,) = _wgrad_chunk_rhs(n2, [dproj], N_CHIPS, False, "mix_dwin")
    ex = _rs_send_halves([dwin, dwout.reshape(N_CHIPS, D // N_CHIPS, D)], "mix")
    dn2 = _mm_reduce([dproj], [win], False, False, "mix_dn", deps=[ex.token])
    ex = _rs_send_chunks(ex, [dn2], core_idx, "mix")
    outs = _norm_mod_bwd(dn2, h1, mix_norm_g, sc2, dh2, "mix_norm_bwd", prev=prev, deps=[ex.token])
    return outs, ex, (d_cb, d_lng, d_lnb, d_attn_g, d_gco, d_cw)


def kernel(x, c, w_ada, b_ada, ffn1_norm_g, ffn1_w_gate, ffn1_w_up, ffn1_w_down, mix_norm_g, w_in, conv_dw_w, conv_dw_b, conv_ln_g, conv_ln_b, attn_out_g, conv_out_g, w_out, ffn2_norm_g, ffn2_w_gate, ffn2_w_up, ffn2_w_down, final_norm_g, loss_target, m_w_ada, m_b_ada, m_ffn1_norm_g, m_ffn1_w_gate, m_ffn1_w_up, m_ffn1_w_down, m_mix_norm_g, m_w_in, m_conv_dw_w, m_conv_dw_b, m_conv_ln_g, m_conv_ln_b, m_attn_out_g, m_conv_out_g, m_w_out, m_ffn2_norm_g, m_ffn2_w_gate, m_ffn2_w_up, m_ffn2_w_down, m_final_norm_g, v_w_ada, v_b_ada, v_ffn1_norm_g, v_ffn1_w_gate, v_ffn1_w_up, v_ffn1_w_down, v_mix_norm_g, v_w_in, v_conv_dw_w, v_conv_dw_b, v_conv_ln_g, v_conv_ln_b, v_attn_out_g, v_conv_out_g, v_w_out, v_ffn2_norm_g, v_ffn2_w_gate, v_ffn2_w_up, v_ffn2_w_down, v_final_norm_g):
    S, D = x.shape[1], x.shape[2]
    mx, my, mc = _coords()
    chip = 2 * mx + my
    dev = 4 * mx + 2 * my + mc
    chip_idx = chip.astype(jnp.int32).reshape(1)
    core_idx = mc.astype(jnp.int32).reshape(1)
    chip_core = jnp.stack([chip, mc]).astype(jnp.int32)
    h0 = x[0]
    target = loss_target[0]

    ncw = CONV_KERNEL * 128
    n0 = -(-(D + ncw) // 1024) * 1024
    pk0 = _pad_cols(jnp.concatenate([c.reshape(1, D), conv_dw_w.reshape(1, ncw)], axis=1), n0)
    g0 = _all_gather8(pk0.reshape(8, n0 // 8), "gather_c").reshape(N_DEV, n0)
    c_all = g0[:, :D]
    conv_w = jnp.concatenate([g0[2 * kc, D:D + ncw].reshape(CONV_KERNEL, 128) for kc in range(N_CHIPS)], axis=1)
    conv_w = jnp.pad(conv_w, ((0, HALO - CONV_KERNEL), (0, 0)))
    nmod = w_ada.shape[2]
    b_shard = lax.dynamic_slice(b_ada, (0, chip * nmod), (1, nmod))
    def gather_start(ws, tag, dep):
        slots = [_cast_place(w, chip_idx, f"cast_{tag}_{i}", deps=[dep]) for i, w in enumerate(ws)]
        return _Exchange(slots, _plan_gather, 3 * len(ws), f"gather_{tag}")

    def swap_start(ex, after, tag):
        return _Exchange(ex.wait(after), _plan_swap, 3 * len(ex.bufs), f"swap_{tag}")

    ex_gu1 = gather_start([ffn1_w_gate[0].T, ffn1_w_up[0].T], "ffn1_gu", g0)
    mod_part = _ada_fwd(c_all, w_ada[0], b_shard, "ada_fwd")
    g1 = _all_gather8(mod_part, "gather_mod", deps=[ex_gu1.token])
    mod_all = jnp.concatenate([g1[16 * kc:16 * kc + 8] for kc in range(N_CHIPS)], axis=1)
    mod = lax.dynamic_slice(mod_all, (dev, 0), (1, 9 * D))
    sh1, sc1, gt1, sh2, sc2, gt2, sh3, sc3, gt3 = [mod[:, i * D:(i + 1) * D] for i in range(9)]
    ex_d1 = gather_start([ffn1_w_down[0]], "ffn1_d", g1)
    ex_wm = gather_start([w_in[0], w_out[0]], "mix", ex_d1.token)
    ex_w2 = gather_start([ffn2_w_gate[0].T, ffn2_w_up[0].T, ffn2_w_down[0]], "ffn2", ex_wm.token)

    n1 = _norm_mod(h0, ffn1_norm_g, sc1, sh1, "ffn1_norm")
    wg1, wu1 = swap_start(ex_gu1, [n1, ex_w2.token], "ffn1_gu").wait([])
    ga1, up1, act1 = _ffn_gate_up(n1, wg1, wu1, "ffn1_gate_up")
    (wd1,) = swap_start(ex_d1, [act1], "ffn1_d").wait([])
    ex_wm = swap_start(ex_wm, [wd1], "mix")
    h1, f1, n2 = _mm_residual(act1, wd1, h0, gt1, 0.5, "ffn1_down", next_norm=(mix_norm_g, sc2, sh2))
    saved1 = (h0, n1, ga1, up1, act1, f1)
    win, wout = ex_wm.wait([h1])
    ex_w2 = swap_start(ex_w2, [h1], "ffn2")
    (h2, n3), saved2 = _mixer_forward(h1, n2, gt2, win, wout, conv_w, conv_dw_b, conv_ln_g, conv_ln_b,
                                      attn_out_g, conv_out_g, next_norm=(ffn2_norm_g, sc3, sh3))
    wg2, wu2, wd2 = ex_w2.wait([h2])
    (h3,), saved3 = _ffn_forward(h2, n3, gt3, wg2, wu2, wd2, "ffn2")
    loss_part, dh3, d_final_g, df3, d_gt3 = _loss_head(h3, final_norm_g.reshape(1, D), target, saved3[5], gt3, 0.5,
                                                       "loss_head")

    (dh2, d_sh3, d_sc3, d_gain3, dmo, d_gt2), (ex_d2, ex_gu2) = _ffn_backward(
        dh3, df3, saved3, ffn2_norm_g, sc3, wg2, wu2, wd2, core_idx, "ffn2", prev=(saved2[12], gt2, 1.0))
    (dh1, d_sh2, d_sc2, d_gain2, df1, d_gt1), ex_mix, small_mix = _mixer_backward(
        dh2, dmo, saved2, mix_norm_g, sc2, win, wout, conv_w, conv_ln_g, conv_ln_b, attn_out_g, conv_out_g, core_idx,
        prev=(f1, gt1, 0.5))
    d_cb, d_lng, d_lnb, d_attn_g, d_gco, d_cw = small_mix
    (dh0, d_sh1, d_sc1, d_gain1), (ex_d1, ex_gu1) = _ffn_backward(
        dh1, df1, saved1, ffn1_norm_g, sc1, wg1, wu1, wd1, core_idx, "ffn1", last=True)

    dmod = jnp.concatenate([d_sh1, d_sc1, d_gt1, d_sh2, d_sc2, d_gt2, d_sh3, d_sc3, d_gt3], axis=1)
    small = [d_gain1, d_gain2, d_gain3, d_final_g, d_cb, d_lng, d_lnb, d_attn_g, d_gco,
             d_cw[:CONV_KERNEL].reshape(1, CONV_KERNEL * CONV_WIDTH), loss_part]
    pk1 = jnp.concatenate([dmod] + small, axis=1)
    n1_ = -(-pk1.shape[1] // 1024) * 1024
    gathered = _all_gather8(_pad_cols(pk1, n1_).reshape(8, n1_ // 8), "gather_small").reshape(N_DEV, n1_)
    ex_gu1 = _rs_send_chunks(ex_gu1, [gathered], core_idx, "ffn1_gu")
    tot = _sum_rows(gathered, "sum_small")
    off = [0]

    def take(nel):
        out = tot[:, off[0]:off[0] + nel]
        off[0] += nel
        return out

    g_b_ada = take(9 * D)
    g_ffn1_norm, g_mix_norm, g_ffn2_norm, g_final = take(D), take(D), take(D), take(D)
    g_cb, g_lng, g_lnb, g_attn_g, g_gco = take(512), take(512), take(512), take(512), take(512)
    g_cw_full = take(CONV_KERNEL * CONV_WIDTH).reshape(CONV_KERNEL, CONV_WIDTH)
    loss = take(128)[0, 0]
    g_cw = lax.dynamic_slice(g_cw_full, (0, chip * 128), (CONV_KERNEL, 128))

    dmod_shard = lax.dynamic_slice(gathered[:, :9 * D], (0, chip * nmod), (N_DEV, nmod))
    dmod16 = jnp.pad(dmod_shard, ((0, N_DEV), (0, 0)))
    c_t16 = jnp.pad(c_all.T, ((0, 0), (0, N_DEV)))
    g_w_ada = _ada_wgrad(c_t16, dmod16, "ada_wgrad")

    names = ["w_ada", "b_ada", "ffn1_norm_g", "ffn1_w_gate", "ffn1_w_up", "ffn1_w_down", "mix_norm_g", "w_in",
             "conv_dw_w", "conv_dw_b", "conv_ln_g", "conv_ln_b", "attn_out_g", "conv_out_g", "w_out", "ffn2_norm_g",
             "ffn2_w_gate", "ffn2_w_up", "ffn2_w_down", "final_norm_g"]
    weights = dict(zip(names, [w_ada, b_ada, ffn1_norm_g, ffn1_w_gate, ffn1_w_up, ffn1_w_down, mix_norm_g, w_in,
                               conv_dw_w, conv_dw_b, conv_ln_g, conv_ln_b, attn_out_g, conv_out_g, w_out,
                               ffn2_norm_g, ffn2_w_gate, ffn2_w_up, ffn2_w_down, final_norm_g]))
    ms = dict(zip(names, [m_w_ada, m_b_ada, m_ffn1_norm_g, m_ffn1_w_gate, m_ffn1_w_up, m_ffn1_w_down, m_mix_norm_g,
                          m_w_in, m_conv_dw_w, m_conv_dw_b, m_conv_ln_g, m_conv_ln_b, m_attn_out_g, m_conv_out_g,
                          m_w_out, m_ffn2_norm_g, m_ffn2_w_gate, m_ffn2_w_up, m_ffn2_w_down, m_final_norm_g]))
    vs = dict(zip(names, [v_w_ada, v_b_ada, v_ffn1_norm_g, v_ffn1_w_gate, v_ffn1_w_up, v_ffn1_w_down, v_mix_norm_g,
                          v_w_in, v_conv_dw_w, v_conv_dw_b, v_conv_ln_g, v_conv_ln_b, v_attn_out_g, v_conv_out_g,
                          v_w_out, v_ffn2_norm_g, v_ffn2_w_gate, v_ffn2_w_up, v_ffn2_w_down, v_final_norm_g]))
    grads, deltas, new_ms, new_vs = {}, {}, {}, {}

    def adamw_big(nm, g2d, deps=(), transposed=False):
        shape = weights[nm].shape
        two_d = (shape[-2], shape[-1])

        def view(t):
            return t.reshape(two_d).T if transposed else t.reshape(two_d)

        d_, m_, v_ = _adamw(view(weights[nm]), g2d, view(ms[nm]), view(vs[nm]), f"adamw_{nm}", deps=deps)
        grads[nm], deltas[nm], new_ms[nm], new_vs[nm] = (
            (t.T if transposed else t).reshape(shape) for t in (g2d, d_, m_, v_))
        return d_

    d_ada = adamw_big("w_ada", g_w_ada, deps=[ex_gu1.token])
    small_grads = {"b_ada": g_b_ada, "ffn1_norm_g": g_ffn1_norm, "mix_norm_g": g_mix_norm, "conv_dw_w": g_cw,
                   "conv_dw_b": g_cb, "conv_ln_g": g_lng, "conv_ln_b": g_lnb, "attn_out_g": g_attn_g,
                   "conv_out_g": g_gco, "ffn2_norm_g": g_ffn2_norm, "final_norm_g": g_final}
    small_names = [nm for nm in names if nm in small_grads]

    def pack_small(arrs):
        flat = jnp.concatenate([arrs[nm].reshape(1, -1) for nm in small_names], axis=1)
        npad = -(-flat.shape[1] // 1024) * 1024
        return _pad_cols(flat, npad).reshape(8, npad // 8)

    d_s, m_s, v_s = _adamw(pack_small(weights), pack_small(small_grads), pack_small(ms), pack_small(vs),
                           "adamw_small")
    pos = 0
    for nm in small_names:
        shape, nel = weights[nm].shape, weights[nm].size
        grads[nm] = small_grads[nm].reshape(shape)
        deltas[nm], new_ms[nm], new_vs[nm] = (t.reshape(1, -1)[:, pos:pos + nel].reshape(shape)
                                              for t in (d_s, m_s, v_s))
        pos += nel

    ex_d2 = _rs_send_share(ex_d2, [d_ada, d_s], chip_core, "ffn2_d")
    ex_gu2 = _rs_send_share(ex_gu2, [ex_d2.token], chip_core, "ffn2_gu")
    ex_mix = _rs_send_share(ex_mix, [ex_gu2.token], chip_core, "mix")
    ex_d1 = _rs_send_share(ex_d1, [ex_mix.token], chip_core, "ffn1_d")
    (g_wd2,) = _rs_finish(ex_d2, [ex_d1.token])
    last = [adamw_big("ffn2_w_down", g_wd2)]
    g_wg2, g_wu2 = _rs_finish(ex_gu2, last)
    last = [adamw_big("ffn2_w_gate", g_wg2, transposed=True), adamw_big("ffn2_w_up", g_wu2, transposed=True)]
    g_win, g_wout = _rs_finish(ex_mix, last)
    last = [adamw_big("w_in", g_win), adamw_big("w_out", g_wout)]
    (g_wd1,) = _rs_finish(ex_d1, last)
    last = [adamw_big("ffn1_w_down", g_wd1)]
    ex_gu1 = _rs_send_share(ex_gu1, last, chip_core, "ffn1_gu")
    g_wg1, g_wu1 = _rs_finish(ex_gu1, [])
    adamw_big("ffn1_w_gate", g_wg1, transposed=True)
    adamw_big("ffn1_w_up", g_wu1, transposed=True)

    return (loss, dh0[None], *[grads[nm] for nm in names], *[deltas[nm] for nm in names],
            *[new_ms[nm] for nm in names], *[new_vs[nm] for nm in names])
```

```python
import jax
import jax.numpy as jnp
import numpy as np
from jax import lax
from jax.experimental import pallas as pl
from jax.experimental.pallas import tpu as pltpu

F32 = jnp.float32
BF16 = jnp.bfloat16
MESH = pl.DeviceIdType.MESH

RMS_EPS = 1e-6
LN_EPS = 1e-5
HEAD_DIM = 64
ATTN_WIDTH = 512
CONV_WIDTH = 512
ATTN_BLOCK = 128
DILATIONS = (1, 4, 16)
SUPER_ROWS = ATTN_BLOCK * 16
ROPE_THETA = 10000.0
CONV_KERNEL = 31
HALO = 32
N_CHIPS = 4
N_DEV = 8
ADAM_LR, ADAM_B1, ADAM_B2, ADAM_EPS, ADAM_WD, ADAM_STEP = 0.001, 0.9, 0.999, 1e-08, 0.01, 10
VMEM_LIMIT_BYTES = 48 * 1024 * 1024
VMEM_LIMIT_STREAMING = 62 * 1024 * 1024
NEG = -1e30

NT = (((1,), (1,)), ((), ()))
TN = (((0,), (0,)), ((), ()))

ANY_SPEC = pl.BlockSpec(memory_space=pl.ANY)
HBM_SPEC = pl.BlockSpec(memory_space=pltpu.HBM)
SEM_SPEC = pl.BlockSpec(memory_space=pltpu.SEMAPHORE)
DATAFLOW = pltpu.SideEffectType.DATAFLOW_SIDE_EFFECTING
XY_FLIPS = ((0, 1), (1, 0), (1, 1))


def _params(sem=None, streaming=False):
    return pltpu.CompilerParams(dimension_semantics=sem,
                                vmem_limit_bytes=VMEM_LIMIT_STREAMING if streaming else VMEM_LIMIT_BYTES)


def _row_tile(rows, want):
    t = min(rows, want)
    assert rows % t == 0
    return t


def _sigmoid(x):
    return 1.0 / (1.0 + jnp.exp(-x))


def _deps(deps):
    return [d for d in deps if d is not None]


def _blind_to(body, n_in, n_dep):
    def wrapped(*refs):
        return body(*refs[:n_in], *refs[n_in + n_dep:])
    return wrapped


def _vec_spec(d, ngrid):
    if ngrid == 1:
        return pl.BlockSpec((1, d), lambda i: (0, 0))
    return pl.BlockSpec((1, d), lambda i, j: (0, 0))


def _norm_mod(h, gain, sc, sh, name, deps=()):
    S, D = h.shape
    tr = _row_tile(S, 512)
    deps = _deps(deps)

    def body(h_ref, g_ref, sc_ref, sh_ref, n_ref):
        x = h_ref[...]
        r = lax.rsqrt(jnp.mean(x * x, axis=-1, keepdims=True) + RMS_EPS)
        y = (x * r) * g_ref[...]
        n_ref[...] = (y * (1.0 + sc_ref[...]) + sh_ref[...]).astype(BF16)

    row = pl.BlockSpec((tr, D), lambda i: (i, 0))
    return pl.pallas_call(
        _blind_to(body, 4, len(deps)), name=name, grid=(S // tr,),
        in_specs=[row, _vec_spec(D, 1), _vec_spec(D, 1), _vec_spec(D, 1)] + [ANY_SPEC] * len(deps),
        out_specs=row, out_shape=jax.ShapeDtypeStruct((S, D), BF16),
        compiler_params=_params(("parallel",)),
    )(h, gain, sc, sh, *deps)


def _norm_mod_bwd(dn, h_in, gain, sc, dh_out, name, prev=None, deps=()):
    S, D = h_in.shape
    tr = _row_tile(S, 512)
    deps = _deps(deps)
    n_in = 5 if prev is None else 7

    def body(*refs):
        dn_ref, h_ref, g_ref, sc_ref, dho_ref = refs[:5]
        dh_ref, dsh_ref, dsc_ref, dg_ref = refs[n_in:n_in + 4]

        @pl.when(pl.program_id(0) == 0)
        def _():
            for ref in refs[n_in + 1:n_in + 4] + refs[n_in + 5:]:
                ref[...] = jnp.zeros_like(ref)

        x = h_ref[...]
        dn_ = dn_ref[...]
        g = g_ref[...]
        one_sc = 1.0 + sc_ref[...]
        r = lax.rsqrt(jnp.mean(x * x, axis=-1, keepdims=True) + RMS_EPS)
        xh = x * r
        dsh_ref[...] += jnp.sum(dn_, axis=0, keepdims=True)
        dsc_ref[...] += jnp.sum(dn_ * (xh * g), axis=0, keepdims=True)
        dg_ref[...] += jnp.sum(dn_ * one_sc * xh, axis=0, keepdims=True)
        dxh = dn_ * (g * one_sc)
        dh = dho_ref[...] + r * (dxh - xh * jnp.mean(dxh * xh, axis=-1, keepdims=True))
        dh_ref[...] = dh
        if prev is not None:
            _gate_back(dh, refs[5], refs[6], prev[2], refs[n_in + 4], refs[n_in + 5])

    row = pl.BlockSpec((tr, D), lambda i: (i, 0))
    vec = _vec_spec(D, 1)
    extra_in = [] if prev is None else [row, vec]
    extra_out = [] if prev is None else [row, vec]
    extra_shape = [] if prev is None else [jax.ShapeDtypeStruct((S, D), BF16), jax.ShapeDtypeStruct((1, D), F32)]
    return pl.pallas_call(
        _blind_to(body, n_in, len(deps)), name=name, grid=(S // tr,),
        in_specs=[row, row, vec, vec, row] + extra_in + [ANY_SPEC] * len(deps),
        out_specs=[row, vec, vec, vec] + extra_out,
        out_shape=[jax.ShapeDtypeStruct((S, D), F32)] + [jax.ShapeDtypeStruct((1, D), F32)] * 3 + extra_shape,
        compiler_params=_params(("arbitrary",)),
    )(dn, h_in, gain, sc, dh_out, *([] if prev is None else prev[:2]), *deps)


def _gate_back(dh, f_ref, gate_ref, coef, df_ref, dgate_ref):
    df_ref[...] = ((coef * gate_ref[...]) * dh).astype(BF16)
    dgate_ref[...] += jnp.sum(coef * dh * f_ref[...].astype(F32), axis=0, keepdims=True)


def _loss_head(h, gain, target, f, gate, coef, name):
    S, D = h.shape
    tr = _row_tile(S, 512)

    def body(h_ref, g_ref, t_ref, f_ref, gate_ref, loss_ref, dh_ref, dg_ref, df_ref, dgate_ref):
        @pl.when(pl.program_id(0) == 0)
        def _():
            loss_ref[...] = jnp.zeros_like(loss_ref)
            dg_ref[...] = jnp.zeros_like(dg_ref)
            dgate_ref[...] = jnp.zeros_like(dgate_ref)

        x = h_ref[...]
        g = g_ref[...]
        r = lax.rsqrt(jnp.mean(x * x, axis=-1, keepdims=True) + RMS_EPS)
        xh = x * r
        err = xh * g - t_ref[...]
        part = 0.5 * jnp.sum(jnp.mean(err * err, axis=-1, keepdims=True), axis=0, keepdims=True)
        loss_ref[...] += jnp.broadcast_to(part, loss_ref.shape)
        dy = err * (1.0 / D)
        dg_ref[...] += jnp.sum(dy * xh, axis=0, keepdims=True)
        dxh = dy * g
        dh = r * (dxh - xh * jnp.mean(dxh * xh, axis=-1, keepdims=True))
        dh_ref[...] = dh
        _gate_back(dh, f_ref, gate_ref, coef, df_ref, dgate_ref)

    row = pl.BlockSpec((tr, D), lambda i: (i, 0))
    vec = _vec_spec(D, 1)
    return pl.pallas_call(
        body, name=name, grid=(S // tr,),
        in_specs=[row, vec, row, row, vec],
        out_specs=[pl.BlockSpec((1, 128), lambda i: (0, 0)), row, vec, row, vec],
        out_shape=[jax.ShapeDtypeStruct((1, 128), F32), jax.ShapeDtypeStruct((S, D), F32),
                   jax.ShapeDtypeStruct((1, D), F32), jax.ShapeDtypeStruct((S, D), BF16),
                   jax.ShapeDtypeStruct((1, D), F32)],
        compiler_params=_params(("arbitrary",)),
    )(h, gain, target, f, gate)


def _ffn_gate_up(n, wg_t, wu_t, name):
    S, D = n.shape
    nk, w, _ = wg_t.shape
    tm = _row_tile(S, 512)

    def body(n_ref, wg_ref, wu_ref, ga_ref, up_ref, act_ref):
        x = n_ref[...]
        ga = lax.dot_general(x, wg_ref[...], NT, preferred_element_type=F32)
        up = lax.dot_general(x, wu_ref[...], NT, preferred_element_type=F32)
        ga_ref[...] = ga.astype(BF16)
        up_ref[...] = up.astype(BF16)
        act_ref[...] = ((ga * _sigmoid(ga)) * up).astype(BF16)

    wspec = pl.BlockSpec((None, w, D), lambda k, m: (k, 0, 0))
    ospec = pl.BlockSpec((None, tm, w), lambda k, m: (k, m, 0))
    out = jax.ShapeDtypeStruct((nk, S, w), BF16)
    return pl.pallas_call(
        body, name=name, grid=(nk, S // tm),
        in_specs=[pl.BlockSpec((tm, D), lambda k, m: (m, 0)), wspec, wspec],
        out_specs=[ospec, ospec, ospec], out_shape=[out, out, out],
        compiler_params=_params(("parallel", "parallel")),
    )(n, wg_t, wu_t)


def _mm_residual(lhs, w, h_in, gvec, coef, name, next_norm=None):
    nk, S, kc = lhs.shape
    D = w.shape[2]
    tm = _row_tile(S, 512)
    n_in = 4 if next_norm is None else 7

    def body(*refs):
        l_ref, w_ref, h_ref, g_ref = refs[:4]
        ho_ref, f_ref = refs[n_in:n_in + 2]
        acc_ref = refs[-1]
        k = pl.program_id(1)

        @pl.when(k == 0)
        def _():
            acc_ref[...] = jnp.zeros_like(acc_ref)

        acc_ref[...] += jnp.dot(l_ref[...], w_ref[...], preferred_element_type=F32)

        @pl.when(k == nk - 1)
        def _():
            f = acc_ref[...]
            f_ref[...] = f.astype(BF16)
            x = h_ref[...] + (coef * g_ref[...]) * f
            ho_ref[...] = x
            if next_norm is not None:
                ng_ref, sc_ref, sh_ref = refs[4:7]
                r = lax.rsqrt(jnp.mean(x * x, axis=-1, keepdims=True) + RMS_EPS)
                y = (x * r) * ng_ref[...]
                refs[n_in + 2][...] = (y * (1.0 + sc_ref[...]) + sh_ref[...]).astype(BF16)

    row = pl.BlockSpec((tm, D), lambda m, k: (m, 0))
    vec = _vec_spec(D, 2)
    with_n = next_norm is not None
    return pl.pallas_call(
        body, name=name, grid=(S // tm, nk),
        in_specs=[pl.BlockSpec((None, tm, kc), lambda m, k: (k, m, 0)),
                  pl.BlockSpec((None, kc, D), lambda m, k: (k, 0, 0)), row, vec] + [vec] * (3 * with_n),
        out_specs=[row, row] + [row] * with_n,
        out_shape=[jax.ShapeDtypeStruct((S, D), F32), jax.ShapeDtypeStruct((S, D), BF16)]
        + [jax.ShapeDtypeStruct((S, D), BF16)] * with_n,
        scratch_shapes=[pltpu.VMEM((tm, D), F32)],
        compiler_params=_params(("parallel", "arbitrary")),
    )(lhs, w, h_in, gvec, *(next_norm or ()))


def _mm_cols(n, w, name):
    S, D = n.shape
    nk, _, wd = w.shape
    assert wd % 128 == 0
    tm = _row_tile(S, 512)

    def body(n_ref, w_ref, o_ref):
        o_ref[...] = jnp.dot(n_ref[...], w_ref[...], preferred_element_type=F32)

    return pl.pallas_call(
        body, name=name, grid=(nk, S // tm),
        in_specs=[pl.BlockSpec((tm, D), lambda k, m: (m, 0)), pl.BlockSpec((None, D, wd), lambda k, m: (k, 0, 0))],
        out_specs=pl.BlockSpec((tm, wd), lambda k, m: (m, k)),
        out_shape=jax.ShapeDtypeStruct((S, nk * wd), F32),
        compiler_params=_params(("parallel", "parallel")),
    )(n, w)


def _ffn_dact(df, wd, ga, up, name, deps=()):
    S, D = df.shape
    nk, w, _ = wd.shape
    tm = _row_tile(S, 512)
    deps = _deps(deps)

    def body(df_ref, wd_ref, ga_ref, up_ref, dga_ref, dup_ref):
        dact = lax.dot_general(df_ref[...], wd_ref[...], NT, preferred_element_type=F32)
        ga_ = ga_ref[...].astype(F32)
        up_ = up_ref[...].astype(F32)
        sig = _sigmoid(ga_)
        dga_ref[...] = (dact * up_ * (sig * (1.0 + ga_ * (1.0 - sig)))).astype(BF16)
        dup_ref[...] = (dact * (ga_ * sig)).astype(BF16)

    cspec = pl.BlockSpec((None, tm, w), lambda k, m: (k, m, 0))
    out = jax.ShapeDtypeStruct((nk, S, w), BF16)
    return pl.pallas_call(
        _blind_to(body, 4, len(deps)), name=name, grid=(nk, S // tm),
        in_specs=[pl.BlockSpec((tm, D), lambda k, m: (m, 0)), pl.BlockSpec((None, w, D), lambda k, m: (k, 0, 0)),
                  cspec, cspec] + [ANY_SPEC] * len(deps),
        out_specs=[cspec, cspec], out_shape=[out, out],
        compiler_params=_params(("parallel", "parallel")),
    )(df, wd, ga, up, *deps)


def _mm_nt(d, w, name):
    S, K = d.shape
    N = w.shape[0]
    tm = _row_tile(S, 512)

    def body(d_ref, w_ref, o_ref):
        o_ref[...] = lax.dot_general(d_ref[...], w_ref[...], NT, preferred_element_type=F32)

    return pl.pallas_call(
        body, name=name, grid=(S // tm,),
        in_specs=[pl.BlockSpec((tm, K), lambda m: (m, 0)), pl.BlockSpec((N, K), lambda m: (0, 0))],
        out_specs=pl.BlockSpec((tm, N), lambda m: (m, 0)),
        out_shape=jax.ShapeDtypeStruct((S, N), F32),
        compiler_params=_params(("parallel",)),
    )(d, w)


def _mm_reduce(lhs_list, w_list, chunked3d, w_is_kd, name, deps=()):
    nk = w_list[0].shape[0]
    kc, D = w_list[0].shape[1:] if w_is_kd else w_list[0].shape[:0:-1]
    S = lhs_list[0].shape[1] if chunked3d else lhs_list[0].shape[0]
    tm = _row_tile(S, 512)
    npair = len(lhs_list)
    deps = _deps(deps)

    def body(*refs):
        l_refs, w_refs = refs[:npair], refs[npair:2 * npair]
        o_ref, acc_ref = refs[2 * npair], refs[2 * npair + 1]
        k = pl.program_id(1)

        @pl.when(k == 0)
        def _():
            acc_ref[...] = jnp.zeros_like(acc_ref)

        for l_ref, w_ref in zip(l_refs, w_refs):
            if w_is_kd:
                acc_ref[...] += jnp.dot(l_ref[...], w_ref[...], preferred_element_type=F32)
            else:
                acc_ref[...] += lax.dot_general(l_ref[...], w_ref[...], NT, preferred_element_type=F32)

        @pl.when(k == nk - 1)
        def _():
            o_ref[...] = acc_ref[...]

    if chunked3d:
        lspec = pl.BlockSpec((None, tm, kc), lambda m, k: (k, m, 0))
    else:
        lspec = pl.BlockSpec((tm, kc), lambda m, k: (m, k))
    wspec = pl.BlockSpec((None,) + tuple(w_list[0].shape[1:]), lambda m, k: (k, 0, 0))
    return pl.pallas_call(
        _blind_to(body, 2 * npair, len(deps)), name=name, grid=(S // tm, nk),
        in_specs=[lspec] * npair + [wspec] * npair + [ANY_SPEC] * len(deps),
        out_specs=pl.BlockSpec((tm, D), lambda m, k: (m, 0)),
        out_shape=jax.ShapeDtypeStruct((S, D), F32),
        scratch_shapes=[pltpu.VMEM((tm, D), F32)],
        compiler_params=_params(("parallel", "arbitrary")),
    )(*lhs_list, *w_list, *deps)


def _wgrad_chunk_lhs(lhs_list, rhs, name, deps=()):
    nk, S, w = lhs_list[0].shape
    D = rhs.shape[1]
    ts = _row_tile(S, 512)
    ns = S // ts
    nl = len(lhs_list)
    deps = _deps(deps)

    def body(*refs):
        l_refs, r_ref = refs[:nl], refs[nl]
        o_refs, acc_refs = refs[nl + 1:2 * nl + 1], refs[2 * nl + 1:]
        s = pl.program_id(1)

        @pl.when(s == 0)
        def _():
            for acc_ref in acc_refs:
                acc_ref[...] = jnp.zeros_like(acc_ref)

        x = r_ref[...]
        for l_ref, acc_ref in zip(l_refs, acc_refs):
            acc_ref[...] += lax.dot_general(l_ref[...], x, TN, preferred_element_type=F32)

        @pl.when(s == ns - 1)
        def _():
            for o_ref, acc_ref in zip(o_refs, acc_refs):
                o_ref[...] = acc_ref[...]

    return pl.pallas_call(
        _blind_to(body, nl + 1, len(deps)), name=name, grid=(nk, ns),
        in_specs=[pl.BlockSpec((None, ts, w), lambda k, s: (k, s, 0))] * nl
        + [pl.BlockSpec((ts, D), lambda k, s: (s, 0))] + [ANY_SPEC] * len(deps),
        out_specs=[pl.BlockSpec((None, w, D), lambda k, s: (k, 0, 0))] * nl,
        out_shape=[jax.ShapeDtypeStruct((nk, w, D), F32)] * nl,
        scratch_shapes=[pltpu.VMEM((w, D), F32)] * nl,
        compiler_params=_params(("parallel", "arbitrary")),
    )(*lhs_list, rhs, *deps)


def _wgrad_chunk_rhs(lhs, rhs_list, nk, chunked3d, name, deps=()):
    S, D = lhs.shape
    w = rhs_list[0].shape[2] if chunked3d else rhs_list[0].shape[1] // nk
    ts = _row_tile(S, 512)
    ns = S // ts
    nr = len(rhs_list)
    deps = _deps(deps)

    def body(*refs):
        l_ref, r_refs = refs[0], refs[1:1 + nr]
        o_refs, acc_refs = refs[1 + nr:1 + 2 * nr], refs[1 + 2 * nr:]
        s = pl.program_id(1)

        @pl.when(s == 0)
        def _():
            for acc_ref in acc_refs:
                acc_ref[...] = jnp.zeros_like(acc_ref)

        x = l_ref[...]
        for r_ref, acc_ref in zip(r_refs, acc_refs):
            acc_ref[...] += lax.dot_general(x, r_ref[...], TN, preferred_element_type=F32)

        @pl.when(s == ns - 1)
        def _():
            for o_ref, acc_ref in zip(o_refs, acc_refs):
                o_ref[...] = acc_ref[...]

    if chunked3d:
        rspec = pl.BlockSpec((None, ts, w), lambda k, s: (k, s, 0))
    else:
        rspec = pl.BlockSpec((ts, w), lambda k, s: (s, k))
    ospec = pl.BlockSpec((None, D, w), lambda k, s: (k, 0, 0))
    return pl.pallas_call(
        _blind_to(body, 1 + nr, len(deps)), name=name, grid=(nk, ns),
        in_specs=[pl.BlockSpec((ts, D), lambda k, s: (s, 0))] + [rspec] * nr + [ANY_SPEC] * len(deps),
        out_specs=[ospec] * nr,
        out_shape=[jax.ShapeDtypeStruct((nk, D, w), F32)] * nr,
        scratch_shapes=[pltpu.VMEM((D, w), F32)] * nr,
        compiler_params=_params(("parallel", "arbitrary")),
    )(lhs, *rhs_list, *deps)


def _rope_tables(S):
    pos = np.arange(S, dtype=np.float32)
    inv_freq = (ROPE_THETA ** (-np.arange(0, HEAD_DIM, 2, dtype=np.float32) / HEAD_DIM)).astype(np.float32)
    ang = (pos[:, None] * inv_freq[None, :]).astype(np.float64)
    cos, sin = np.cos(ang).astype(np.float32), np.sin(ang).astype(np.float32)
    cos2 = np.concatenate([cos, cos, cos, cos], axis=1)
    sin2 = np.concatenate([-sin, sin, -sin, sin], axis=1)
    return jnp.asarray(cos2), jnp.asarray(sin2)


def _rotate(t, cos, sin_signed):
    half = HEAD_DIM // 2
    lane = lax.broadcasted_iota(jnp.int32, t.shape, 1)
    first = (lane % HEAD_DIM) < half
    partner = jnp.where(first, pltpu.roll(t, 128 - half, 1), pltpu.roll(t, half, 1))
    return t * cos + partner * sin_signed


def _qkv_rope(proj, cos, sin, name):
    S = proj.shape[0]
    A = ATTN_WIDTH
    tr = _row_tile(S, 512)
    nb = A // 128
    scale = HEAD_DIM ** -0.5

    def body(q_ref, k_ref, v_ref, c_ref, s_ref, qo_ref, ko_ref, vo_ref):
        c, s = c_ref[...], s_ref[...]
        qo_ref[...] = _rotate(q_ref[...], c, s) * scale
        ko_ref[...] = _rotate(k_ref[...], c, s)
        vo_ref[...] = v_ref[...]

    def col(off):
        return pl.BlockSpec((tr, 128), lambda i, j: (i, off + j))

    tab = pl.BlockSpec((tr, 128), lambda i, j: (i, 0))
    out = jax.ShapeDtypeStruct((S, A), F32)
    return pl.pallas_call(
        body, name=name, grid=(S // tr, nb),
        in_specs=[col(0), col(nb), col(2 * nb), tab, tab],
        out_specs=[col(0), col(0), col(0)], out_shape=[out, out, out],
        compiler_params=_params(("parallel", "parallel")),
    )(proj, proj, proj, cos, sin)


def _band_mask(T, has_prev):
    qi = lax.broadcasted_iota(jnp.int32, (T, 2 * T), 0)
    kj = lax.broadcasted_iota(jnp.int32, (T, 2 * T), 1)
    return ((kj < T) & (kj >= qi) & has_prev) | ((kj >= T) & (kj - T <= qi))


def _branch_blocks(rows, dilation):
    T = min(ATTN_BLOCK, rows // dilation)
    return T, rows // T


def _block_rows(base, T, dilation):
    if dilation == 1:
        return pl.ds(pl.multiple_of(base, T), T)
    return pl.ds(base, T, stride=dilation)


def _attn_fwd(q, k, v, name):
    S, A = q.shape
    sup = min(S, SUPER_ROWS)
    nd = len(DILATIONS)
    assert S % sup == 0

    def body(q_ref, k_ref, v_ref, attn_ref, lse_ref, acc_s, m_s, l_s):
        lane = lax.broadcasted_iota(jnp.int32, (1, 128), 1)
        head0 = lane < HEAD_DIM

        def supertile(st, carry):
            row0 = st * sup
            for di, dil in enumerate(DILATIONS):
                T, nblk = _branch_blocks(sup, dil)
                span = T * dil
                assert T == ATTN_BLOCK or sup == S

                def block(idx, c2, di=di, dil=dil, T=T, span=span):
                    r = idx % dil
                    loc = (idx // dil) * span + r
                    base = row0 + loc
                    rows = _block_rows(base, T, dil)
                    prev = _block_rows(jnp.maximum(base - span, r), T, dil)
                    qb = q_ref[rows, :].astype(BF16)
                    k2 = jnp.concatenate([k_ref[prev, :], k_ref[rows, :]], axis=0).astype(BF16)
                    v2 = jnp.concatenate([v_ref[prev, :], v_ref[rows, :]], axis=0).astype(BF16)
                    valid = _band_mask(T, base >= span)
                    accs, ms, ls = [], [], []
                    for hmask in (head0, jnp.logical_not(head0)):
                        qh = jnp.where(hmask, qb, jnp.zeros_like(qb))
                        s = jnp.where(valid, lax.dot_general(qh, k2, NT, preferred_element_type=F32), NEG)
                        m = jnp.max(s, axis=-1, keepdims=True)
                        p = jnp.exp(s - m)
                        ls.append(jnp.sum(p, axis=-1, keepdims=True))
                        ms.append(m)
                        accs.append(jnp.dot(p.astype(BF16), v2, preferred_element_type=F32))
                    lrows = _block_rows(di * sup + loc, T, dil)
                    acc_s[lrows, :] = jnp.where(head0, accs[0], accs[1])
                    m_s[lrows, :] = jnp.where(head0, ms[0], ms[1])
                    l_s[lrows, :] = jnp.where(head0, ls[0], ls[1])
                    return c2

                lax.fori_loop(0, nblk, block, 0, unroll=2)

            chunk = min(sup, 256)

            def merge(ci, c2):
                lr = [pl.ds(pl.multiple_of(di * sup + ci * chunk, chunk), chunk) for di in range(nd)]
                gr = pl.ds(pl.multiple_of(row0 + ci * chunk, chunk), chunk)
                m0, m1, m2 = m_s[lr[0], :], m_s[lr[1], :], m_s[lr[2], :]
                mm = jnp.maximum(jnp.maximum(m0, m1), m2)
                w0, w1, w2 = jnp.exp(m0 - mm), jnp.exp(m1 - mm), jnp.exp(m2 - mm)
                den = (w0 * l_s[lr[0], :] + w1 * l_s[lr[1], :]) + w2 * l_s[lr[2], :]
                num = (w0 * acc_s[lr[0], :] + w1 * acc_s[lr[1], :]) + w2 * acc_s[lr[2], :]
                attn_ref[gr, :] = num / den
                lse_ref[gr, :] = mm + jnp.log(den)
                return c2

            lax.fori_loop(0, sup // chunk, merge, 0)
            return carry

        lax.fori_loop(0, S // sup, supertile, 0)

    blk = pl.BlockSpec((S, 128), lambda j: (0, j))
    out = jax.ShapeDtypeStruct((S, A), F32)
    return pl.pallas_call(
        body, name=name, grid=(A // 128,),
        in_specs=[blk, blk, blk], out_specs=[blk, blk], out_shape=[out, out],
        scratch_shapes=[pltpu.VMEM((nd * sup, 128), F32)] * 3,
        compiler_params=_params(("parallel",)),
    )(q, k, v)


def _attn_out_bwd(dy, attn, gain, name):
    S, A = attn.shape
    tr = _row_tile(S, 256)

    def body(dy_ref, a_ref, g_ref, da_ref, dl_ref, dg_ref):
        @pl.when(pl.program_id(0) == 0)
        def _():
            dg_ref[...] = jnp.zeros_like(dg_ref)

        x = a_ref[...]
        dy_ = dy_ref[...]
        r = lax.rsqrt(jnp.mean(x * x, axis=-1, keepdims=True) + RMS_EPS)
        xh = x * r
        dg_ref[...] += jnp.sum(dy_ * xh, axis=0, keepdims=True)
        dxh = dy_ * g_ref[...]
        dx = r * (dxh - xh * jnp.mean(dxh * xh, axis=-1, keepdims=True))
        da_ref[...] = dx
        prod = dx * x
        hi = lax.broadcasted_iota(jnp.int32, (A, A), 0) // HEAD_DIM
        hj = lax.broadcasted_iota(jnp.int32, (A, A), 1) // HEAD_DIM
        same_head = (hi == hj).astype(F32)
        dl_ref[...] = jnp.dot(prod, same_head, preferred_element_type=F32, precision=lax.Precision.HIGHEST)

    row = pl.BlockSpec((tr, A), lambda i: (i, 0))
    vec = _vec_spec(A, 1)
    return pl.pallas_call(
        body, name=name, grid=(S // tr,),
        in_specs=[row, row, vec], out_specs=[row, row, vec],
        out_shape=[jax.ShapeDtypeStruct((S, A), F32), jax.ShapeDtypeStruct((S, A), F32),
                   jax.ShapeDtypeStruct((1, A), F32)],
        compiler_params=_params(("arbitrary",)),
    )(dy, attn, gain)


def _attn_bwd(q, k, v, da, lse, delta, name):
    S, A = q.shape

    def body(q_ref, k_ref, v_ref, da_ref, lse_ref, dl_ref, dq_ref, dk_ref, dv_ref):
        lane = lax.broadcasted_iota(jnp.int32, (1, 128), 1)
        head0 = lane < HEAD_DIM
        dq_ref[...] = jnp.zeros_like(dq_ref)
        dk_ref[...] = jnp.zeros_like(dk_ref)
        dv_ref[...] = jnp.zeros_like(dv_ref)
        for dil in DILATIONS:
            T, nblk = _branch_blocks(S, dil)
            span = T * dil

            def block(idx, carry, dil=dil, T=T, span=span):
                r = idx % dil
                base = (idx // dil) * span + r
                rows = _block_rows(base, T, dil)
                prev = _block_rows(jnp.maximum(base - span, r), T, dil)
                qb, dab = q_ref[rows, :].astype(BF16), da_ref[rows, :].astype(BF16)
                k2 = jnp.concatenate([k_ref[prev, :], k_ref[rows, :]], axis=0).astype(BF16)
                v2 = jnp.concatenate([v_ref[prev, :], v_ref[rows, :]], axis=0).astype(BF16)
                lse_b, dl_b = lse_ref[rows, :], dl_ref[rows, :]
                valid = _band_mask(T, base >= span)
                dqs = []
                dk2 = dv2 = None
                for h, hmask in enumerate((head0, jnp.logical_not(head0))):
                    qh = jnp.where(hmask, qb, jnp.zeros_like(qb))
                    dah = jnp.where(hmask, dab, jnp.zeros_like(dab))
                    c0 = h * HEAD_DIM
                    lse_h = lse_b[:, c0:c0 + 1]
                    dl_h = dl_b[:, c0:c0 + 1]
                    s = lax.dot_general(qh, k2, NT, preferred_element_type=F32)
                    p = jnp.where(valid, jnp.exp(s - lse_h), 0.0)
                    dp = lax.dot_general(dah, v2, NT, preferred_element_type=F32)
                    ds = (p * (dp - dl_h)).astype(BF16)
                    dqs.append(jnp.dot(ds, k2, preferred_element_type=F32))
                    t_k = lax.dot_general(ds, qh, TN, preferred_element_type=F32)
                    t_v = lax.dot_general(p.astype(BF16), dah, TN, preferred_element_type=F32)
                    dk2 = t_k if dk2 is None else dk2 + t_k
                    dv2 = t_v if dv2 is None else dv2 + t_v
                dq_ref[rows, :] += jnp.where(head0, dqs[0], dqs[1])
                dk_ref[rows, :] += dk2[T:]
                dv_ref[rows, :] += dv2[T:]
                dk_ref[prev, :] += dk2[:T]
                dv_ref[prev, :] += dv2[:T]
                return carry

            lax.fori_loop(0, nblk, block, 0, unroll=2)

    blk = pl.BlockSpec((S, 128), lambda j: (0, j))
    out = jax.ShapeDtypeStruct((S, A), F32)
    return pl.pallas_call(
        body, name=name, grid=(A // 128,),
        in_specs=[blk] * 6, out_specs=[blk] * 3, out_shape=[out] * 3,
        compiler_params=_params(("parallel",)),
    )(q, k, v, da, lse, delta)


def _glu_window(a_ref, b_ref, ah_ref, bh_ref, first):
    u0 = a_ref[...] * _sigmoid(b_ref[...])
    u0h = ah_ref[...] * _sigmoid(bh_ref[...])
    u0h = jnp.where(first, jnp.zeros_like(u0h), u0h)
    return jnp.concatenate([u0h, u0], axis=0)


def _conv_norms(u1, lng, lnb):
    mu = jnp.mean(u1, axis=-1, keepdims=True)
    xc = u1 - mu
    rstd = lax.rsqrt(jnp.mean(xc * xc, axis=-1, keepdims=True) + LN_EPS)
    u1h = xc * rstd
    u2 = u1h * lng + lnb
    sig = _sigmoid(u2)
    u3 = u2 * sig
    r = lax.rsqrt(jnp.mean(u3 * u3, axis=-1, keepdims=True) + RMS_EPS)
    return rstd, u1h, u2, sig, u3, r


def _conv_specs(tr, C, col_a, col_b):
    per = tr // HALO

    def tile(col):
        return pl.BlockSpec((tr, C), lambda i: (i, col))

    def halo(col):
        return pl.BlockSpec((HALO, C), lambda i: (jnp.maximum(i * per - 1, 0), col))

    return tile(col_a), tile(col_b), halo(col_a), halo(col_b)


def _mixer_merge(proj, attn, cw, cb, lng, lnb, gat, gco, name):
    S = proj.shape[0]
    C = CONV_WIDTH
    A = attn.shape[1]
    tr = _row_tile(S, 256)

    def body(a_ref, b_ref, ah_ref, bh_ref, at_ref, w_ref, cb_ref, lng_ref, lnb_ref, gat_ref, gco_ref, u1_ref, y_ref):
        win = _glu_window(a_ref, b_ref, ah_ref, bh_ref, pl.program_id(0) == 0)
        acc = jnp.broadcast_to(cb_ref[...], (tr, C))
        for j in range(CONV_KERNEL):
            off = HALO - (CONV_KERNEL - 1) + j
            acc = acc + w_ref[j:j + 1, :] * win[off:off + tr, :]
        u1_ref[...] = acc
        _, _, _, _, u3, r = _conv_norms(acc, lng_ref[...], lnb_ref[...])
        y_ref[:, A:] = ((u3 * r) * gco_ref[...]).astype(BF16)
        x = at_ref[...]
        ra = lax.rsqrt(jnp.mean(x * x, axis=-1, keepdims=True) + RMS_EPS)
        y_ref[:, :A] = ((x * ra) * gat_ref[...]).astype(BF16)

    ta, tb, ha, hb = _conv_specs(tr, C, 3, 4)
    row = pl.BlockSpec((tr, C), lambda i: (i, 0))
    vec = _vec_spec(C, 1)
    return pl.pallas_call(
        body, name=name, grid=(S // tr,),
        in_specs=[ta, tb, ha, hb, pl.BlockSpec((tr, A), lambda i: (i, 0)), pl.BlockSpec((HALO, C), lambda i: (0, 0)),
                  vec, vec, vec, _vec_spec(A, 1), vec],
        out_specs=[row, pl.BlockSpec((tr, A + C), lambda i: (i, 0))],
        out_shape=[jax.ShapeDtypeStruct((S, C), F32), jax.ShapeDtypeStruct((S, A + C), BF16)],
        compiler_params=_params(("parallel",)),
    )(proj, proj, proj, proj, attn, cw, cb, lng, lnb, gat, gco)


def _conv_bwd_norms(dy, u1, lng, lnb, gco, name):
    S, C = u1.shape
    tr = _row_tile(S, 256)

    def body(dy_ref, u1_ref, lng_ref, lnb_ref, gco_ref, du1_ref, dgco_ref, dlng_ref, dlnb_ref, dcb_ref):
        @pl.when(pl.program_id(0) == 0)
        def _():
            for ref in (dgco_ref, dlng_ref, dlnb_ref, dcb_ref):
                ref[...] = jnp.zeros_like(ref)

        lng = lng_ref[...]
        rstd, u1h, u2, sig, u3, r = _conv_norms(u1_ref[...], lng, lnb_ref[...])
        dy_ = dy_ref[...]
        u3h = u3 * r
        dgco_ref[...] += jnp.sum(dy_ * u3h, axis=0, keepdims=True)
        du3h = dy_ * gco_ref[...]
        du3 = r * (du3h - u3h * jnp.mean(du3h * u3h, axis=-1, keepdims=True))
        du2 = du3 * (sig * (1.0 + u2 * (1.0 - sig)))
        dlng_ref[...] += jnp.sum(du2 * u1h, axis=0, keepdims=True)
        dlnb_ref[...] += jnp.sum(du2, axis=0, keepdims=True)
        du1h = du2 * lng
        du1 = rstd * (du1h - jnp.mean(du1h, axis=-1, keepdims=True)
                      - u1h * jnp.mean(du1h * u1h, axis=-1, keepdims=True))
        du1_ref[...] = du1
        dcb_ref[...] += jnp.sum(du1, axis=0, keepdims=True)

    row = pl.BlockSpec((tr, C), lambda i: (i, 0))
    vec = _vec_spec(C, 1)
    return pl.pallas_call(
        body, name=name, grid=(S // tr,),
        in_specs=[pl.BlockSpec((tr, C), lambda i: (i, 1)), row, vec, vec, vec],
        out_specs=[row, vec, vec, vec, vec],
        out_shape=[jax.ShapeDtypeStruct((S, C), F32)] + [jax.ShapeDtypeStruct((1, C), F32)] * 4,
        compiler_params=_params(("arbitrary",)),
    )(dy, u1, lng, lnb, gco)


def _dproj(du1, proj, cw, dq, dk, dv, cos, sin, name):
    S, C = du1.shape
    A = dq.shape[1]
    tr = _row_tile(S, 256)
    nt = S // tr
    per = tr // HALO
    scale = HEAD_DIM ** -0.5

    def body(du_ref, dun_ref, a_ref, b_ref, ah_ref, bh_ref, w_ref, dq_ref, dk_ref, dv_ref, cos_ref, sin_ref,
             dp_ref, dw_ref):
        i = pl.program_id(0)

        @pl.when(i == 0)
        def _():
            dw_ref[...] = jnp.zeros_like(dw_ref)

        win = _glu_window(a_ref, b_ref, ah_ref, bh_ref, i == 0)
        du = du_ref[...]
        nxt = jnp.where(i == nt - 1, jnp.zeros_like(dun_ref[...]), dun_ref[...])
        dwin = jnp.concatenate([du, nxt], axis=0)
        du0 = jnp.zeros((tr, C), F32)
        taps = []
        for j in range(CONV_KERNEL):
            back = CONV_KERNEL - 1 - j
            du0 = du0 + w_ref[j:j + 1, :] * dwin[back:back + tr, :]
            off = HALO - (CONV_KERNEL - 1) + j
            taps.append(jnp.sum(du * win[off:off + tr, :], axis=0, keepdims=True))
        taps.append(jnp.zeros((HALO - CONV_KERNEL, C), F32))
        dw_ref[...] += jnp.concatenate(taps, axis=0)
        a, sig = a_ref[...], _sigmoid(b_ref[...])
        dp_ref[:, 3 * A:3 * A + C] = (du0 * sig).astype(BF16)
        dp_ref[:, 3 * A + C:] = (du0 * a * sig * (1.0 - sig)).astype(BF16)
        cos_, nsin = cos_ref[...], -sin_ref[...]
        for j in range(A // 128):
            lanes = slice(j * 128, (j + 1) * 128)
            dp_ref[:, j * 128:(j + 1) * 128] = (_rotate(dq_ref[:, lanes], cos_, nsin) * scale).astype(BF16)
            dp_ref[:, A + j * 128:A + (j + 1) * 128] = _rotate(dk_ref[:, lanes], cos_, nsin).astype(BF16)
        dp_ref[:, 2 * A:3 * A] = dv_ref[...].astype(BF16)

    ta, tb, ha, hb = _conv_specs(tr, C, 3, 4)
    row = pl.BlockSpec((tr, C), lambda i: (i, 0))
    arow = pl.BlockSpec((tr, A), lambda i: (i, 0))
    tab = pl.BlockSpec((tr, 128), lambda i: (i, 0))
    nxt = pl.BlockSpec((HALO, C), lambda i: (jnp.minimum((i + 1) * per, S // HALO - 1), 0))
    wspec = pl.BlockSpec((HALO, C), lambda i: (0, 0))
    return pl.pallas_call(
        body, name=name, grid=(nt,),
        in_specs=[row, nxt, ta, tb, ha, hb, wspec, arow, arow, arow, tab, tab],
        out_specs=[pl.BlockSpec((tr, 3 * A + 2 * C), lambda i: (i, 0)), wspec],
        out_shape=[jax.ShapeDtypeStruct((S, 3 * A + 2 * C), BF16), jax.ShapeDtypeStruct((HALO, C), F32)],
        compiler_params=_params(("arbitrary",)),
    )(du1, du1, proj, proj, proj, proj, cw, dq, dk, dv, cos, sin)


def _ada_fwd(c_all, w, b, name):
    B, D = c_all.shape
    N = w.shape[1]
    tn = 768 if N % 768 == 0 else N

    def body(c_ref, w_ref, b_ref, o_ref):
        c = c_ref[...]
        a = (c * _sigmoid(c)).astype(BF16)
        o_ref[...] = jnp.dot(a, w_ref[...].astype(BF16), preferred_element_type=F32) + b_ref[...]

    return pl.pallas_call(
        body, name=name, grid=(N // tn,),
        in_specs=[pl.BlockSpec((B, D), lambda j: (0, 0)), pl.BlockSpec((D, tn), lambda j: (0, j)),
                  pl.BlockSpec((1, tn), lambda j: (0, j))],
        out_specs=pl.BlockSpec((B, tn), lambda j: (0, j)),
        out_shape=jax.ShapeDtypeStruct((B, N), F32),
        compiler_params=_params(("parallel",)),
    )(c_all, w, b)


def _ada_wgrad(c_t, dmod, name):
    D, B = c_t.shape
    N = dmod.shape[1]
    tn = 768 if N % 768 == 0 else N

    def body(c_ref, d_ref, o_ref):
        c = c_ref[...]
        a = (c * _sigmoid(c)).astype(BF16)
        o_ref[...] = jnp.dot(a, d_ref[...].astype(BF16), preferred_element_type=F32)

    return pl.pallas_call(
        body, name=name, grid=(N // tn,),
        in_specs=[pl.BlockSpec((D, B), lambda j: (0, 0)), pl.BlockSpec((B, tn), lambda j: (0, j))],
        out_specs=pl.BlockSpec((D, tn), lambda j: (0, j)),
        out_shape=jax.ShapeDtypeStruct((D, N), F32),
        compiler_params=_params(("parallel",)),
    )(c_t, dmod)


def _sum_rows(x, name):
    R, N = x.shape

    def body(x_ref, o_ref):
        acc = x_ref[0:1, :]
        for r in range(1, R):
            acc = acc + x_ref[r:r + 1, :]
        o_ref[...] = acc

    return pl.pallas_call(
        body, name=name, out_shape=jax.ShapeDtypeStruct((1, N), F32),
        compiler_params=_params(),
    )(x)


def _adamw(w, g, m, v, name, deps=()):
    R, C = w.shape
    tr = _row_tile(R, 256) if R % 256 == 0 else R
    bc1 = 1.0 - ADAM_B1 ** ADAM_STEP
    bc2 = 1.0 - ADAM_B2 ** ADAM_STEP
    deps = _deps(deps)

    def body(w_ref, g_ref, m_ref, v_ref, d_ref, mo_ref, vo_ref):
        g_ = g_ref[...]
        m_ = ADAM_B1 * m_ref[...] + (1.0 - ADAM_B1) * g_
        v_ = ADAM_B2 * v_ref[...] + (1.0 - ADAM_B2) * (g_ * g_)
        mo_ref[...] = m_
        vo_ref[...] = v_
        d_ref[...] = -ADAM_LR * ((m_ / bc1) / (jnp.sqrt(v_ / bc2) + ADAM_EPS) + ADAM_WD * w_ref[...])

    row = pl.BlockSpec((tr, C), lambda i: (i, 0))
    out = jax.ShapeDtypeStruct((R, C), F32)
    return pl.pallas_call(
        _blind_to(body, 4, len(deps)), name=name, grid=(R // tr,),
        in_specs=[row] * 4 + [ANY_SPEC] * len(deps), out_specs=[row] * 3, out_shape=[out] * 3,
        compiler_params=_params(("parallel",), streaming=True),
    )(w, g, m, v, *deps)


def _coords():
    return lax.axis_index("x"), lax.axis_index("y"), lax.axis_index("c")


def _all_gather8(x, name, deps=()):
    R, N = x.shape
    assert R == 8
    flips = [(fx, fy, fc) for fx in (0, 1) for fy in (0, 1) for fc in (0, 1)][1:]
    deps = _deps(deps)

    def body(x_ref, o_ref, send_sems, recv_sems):
        mx, my, mc = _coords()
        me = 4 * mx + 2 * my + mc

        def rows(dev):
            return o_ref.at[pl.ds(pl.multiple_of(dev * R, R), R), :]

        o_ref[pl.ds(pl.multiple_of(me * R, R), R), :] = x_ref[...]
        copies = []
        for t, (fx, fy, fc) in enumerate(flips):
            peer = (mx ^ fx, my ^ fy, mc ^ fc)
            copies.append(pltpu.make_async_remote_copy(
                src_ref=x_ref, dst_ref=rows(me), send_sem=send_sems.at[t], recv_sem=recv_sems.at[t],
                device_id=peer, device_id_type=MESH))
        for cp in copies:
            cp.start()
        for t, (fx, fy, fc) in enumerate(flips):
            peer_id = 4 * (mx ^ fx) + 2 * (my ^ fy) + (mc ^ fc)
            pltpu.make_async_remote_copy(
                src_ref=x_ref, dst_ref=rows(peer_id), send_sem=send_sems.at[t], recv_sem=recv_sems.at[t],
                device_id=(mx ^ fx, my ^ fy, mc ^ fc), device_id_type=MESH).wait_recv()
        for cp in copies:
            cp.wait_send()

    return pl.pallas_call(
        _blind_to(body, 1, len(deps)), name=name,
        in_specs=[pl.BlockSpec(memory_space=pltpu.VMEM)] + [ANY_SPEC] * len(deps),
        out_specs=pl.BlockSpec(memory_space=pltpu.VMEM),
        out_shape=jax.ShapeDtypeStruct((N_DEV * R, N), F32),
        scratch_shapes=[pltpu.SemaphoreType.DMA((7,)), pltpu.SemaphoreType.DMA((7,))],
        compiler_params=pltpu.CompilerParams(has_side_effects=True, vmem_limit_bytes=VMEM_LIMIT_BYTES),
    )(x, *deps)


def _half_rows(rows, half):
    return pl.ds(pl.multiple_of(half * (rows // 2), 8), rows // 2)


def _split_start(bufs, plan, n, name):
    nb = len(bufs)

    def body(*refs):
        send_sems, recv_sems, token = refs[nb], refs[nb + 1], refs[-1]
        for t, (src, dst, dev) in enumerate(plan(refs[:nb])):
            pltpu.make_async_remote_copy(src_ref=src, dst_ref=dst, send_sem=send_sems.at[t],
                                         recv_sem=recv_sems.at[t], device_id=dev, device_id_type=MESH).start()
        token[...] = jnp.zeros_like(token)

    out = pl.pallas_call(
        body, name=name,
        out_shape=(pltpu.SemaphoreType.DMA((n,)), pltpu.SemaphoreType.DMA((n,)),
                   *[pltpu.HBM(b.shape, b.dtype) for b in bufs], jax.ShapeDtypeStruct((8, 128), F32)),
        in_specs=[HBM_SPEC] * nb,
        out_specs=(SEM_SPEC, SEM_SPEC, *[HBM_SPEC] * nb, pl.BlockSpec(memory_space=pltpu.VMEM)),
        input_output_aliases={i: 2 + i for i in range(nb)},
        compiler_params=pltpu.CompilerParams(has_side_effects=DATAFLOW),
    )(*[pltpu.with_memory_space_constraint(b, pltpu.HBM) for b in bufs])
    return out[0], out[1], list(out[2:2 + nb]), out[-1]


def _split_wait(bufs, send_sems, recv_sems, plan, after, name):
    nb = len(bufs)
    after = _deps(after)

    def body(*refs):
        ss, rs = refs[nb], refs[nb + 1]
        for t, (src, dst, dev) in enumerate(plan(refs[:nb])):
            cp = pltpu.make_async_remote_copy(src_ref=src, dst_ref=dst, send_sem=ss.at[t], recv_sem=rs.at[t],
                                              device_id=dev, device_id_type=MESH)
            cp.wait_send()
            cp.wait_recv()

    out = pl.pallas_call(
        body, name=name,
        out_shape=tuple(pltpu.HBM(b.shape, b.dtype) for b in bufs),
        in_specs=[HBM_SPEC] * nb + [SEM_SPEC, SEM_SPEC] + [ANY_SPEC] * len(after),
        out_specs=tuple([HBM_SPEC] * nb),
        input_output_aliases={i: i for i in range(nb)},
        compiler_params=pltpu.CompilerParams(has_side_effects=DATAFLOW),
    )(*bufs, send_sems, recv_sems, *after)
    return list(out)


class _Exchange:
    def __init__(self, bufs, plan, n, name):
        self.plan, self.name = plan, name
        self.send_sems, self.recv_sems, self.bufs, self.token = _split_start(bufs, plan, n, name + "_start")

    def wait(self, after):
        return _split_wait(self.bufs, self.send_sems, self.recv_sems, self.plan, after, self.name + "_wait")


def _cast_place(w, chip_idx, name, deps=()):
    R, C = w.shape
    tr = _row_tile(R, 256) if R % 256 == 0 else R
    deps = _deps(deps)

    def body(k_ref, w_ref, o_ref):
        o_ref[...] = w_ref[...].astype(BF16)

    grid_spec = pltpu.PrefetchScalarGridSpec(
        num_scalar_prefetch=1, grid=(R // tr,),
        in_specs=[pl.BlockSpec((tr, C), lambda i, k: (i, 0))] + [ANY_SPEC] * len(deps),
        out_specs=pl.BlockSpec((None, tr, C), lambda i, k: (k[0], i, 0)),
    )
    return pl.pallas_call(
        _blind_to(body, 2, len(deps)), name=name, grid_spec=grid_spec,
        out_shape=jax.ShapeDtypeStruct((N_CHIPS, R, C), BF16),
        compiler_params=_params(("parallel",), streaming=True),
    )(chip_idx, w, *deps)


def _plan_gather(refs):
    mx, my, mc = _coords()
    me = 2 * mx + my
    plan = []
    for g in refs:
        mine = g.at[me, _half_rows(g.shape[1], mc), :]
        for fx, fy in XY_FLIPS:
            plan.append((mine, mine, (mx ^ fx, my ^ fy, mc)))
    return plan


def _plan_swap(refs):
    mx, my, mc = _coords()
    plan = []
    for g in refs:
        for fx, fy in XY_FLIPS:
            have = g.at[2 * (mx ^ fx) + (my ^ fy), _half_rows(g.shape[1], mc), :]
            plan.append((have, have, (mx, my, 1 - mc)))
    return plan


def _plan_other_halves(refs):
    n = len(refs) // 2
    mx, my, mc = _coords()
    return [(g.at[pl.ds(0, N_CHIPS), _half_rows(g.shape[1], 1 - mc), :], land, (mx, my, 1 - mc))
            for g, land in zip(refs[:n], refs[n:])]


def _plan_chunks(refs):
    n = len(refs) // 2
    mx, my, mc = _coords()
    plan = []
    for s, land in zip(refs[:n], refs[n:]):
        for t, (fx, fy) in enumerate(XY_FLIPS):
            plan.append((s.at[2 * (mx ^ fx) + (my ^ fy)], land.at[t], (mx ^ fx, my ^ fy, mc)))
    return plan


def _plan_share(refs):
    mx, my, mc = _coords()
    return [(full.at[mc], full.at[mc], (mx, my, 1 - mc)) for full in refs]


def _add_half(g, recv, core_idx, name):
    nk, R, C = g.shape
    rh = R // 2
    tr = _row_tile(rh, 128) if rh % 128 == 0 else rh
    nt = rh // tr

    def body(c_ref, g_ref, r_ref, o_ref):
        o_ref[...] = (g_ref[...] + r_ref[...]).astype(BF16)

    grid_spec = pltpu.PrefetchScalarGridSpec(
        num_scalar_prefetch=1, grid=(nk, nt),
        in_specs=[pl.BlockSpec((None, tr, C), lambda k, i, c: (k, c[0] * nt + i, 0)),
                  pl.BlockSpec((None, tr, C), lambda k, i, c: (k, i, 0))],
        out_specs=pl.BlockSpec((None, tr, C), lambda k, i, c: (k, i, 0)),
    )
    return pl.pallas_call(
        body, name=name, grid_spec=grid_spec, out_shape=jax.ShapeDtypeStruct((nk, rh, C), BF16),
        compiler_params=_params(("parallel", "parallel"), streaming=True),
    )(core_idx, g, recv)


def _sum_chips(s, land, chip_core, name):
    _, rh, C = s.shape
    tr = _row_tile(rh, 128) if rh % 128 == 0 else rh

    def body(p_ref, s_ref, l_ref, o_ref):
        me = p_ref[0]
        acc = None
        for j in range(N_CHIPS):
            t = jnp.maximum(jnp.bitwise_xor(me, j) - 1, 0)
            term = jnp.where(me == j, s_ref[...], l_ref[t]).astype(F32)
            acc = term if acc is None else acc + term
        o_ref[...] = acc

    grid_spec = pltpu.PrefetchScalarGridSpec(
        num_scalar_prefetch=1, grid=(rh // tr,),
        in_specs=[pl.BlockSpec((None, tr, C), lambda i, p: (p[0], i, 0)),
                  pl.BlockSpec((3, tr, C), lambda i, p: (0, i, 0))],
        out_specs=pl.BlockSpec((None, tr, C), lambda i, p: (p[1], i, 0)),
    )
    return pl.pallas_call(
        body, name=name, grid_spec=grid_spec, out_shape=jax.ShapeDtypeStruct((2, rh, C), F32),
        compiler_params=_params(("parallel",), streaming=True),
    )(chip_core, s, land)


def _rs_send_halves(grads, tag):
    lands = [lax.empty((g.shape[0], g.shape[1] // 2, g.shape[2]), g.dtype) for g in grads]
    return _Exchange(list(grads) + lands, _plan_other_halves, len(grads), f"rs_halves_{tag}")


def _rs_send_chunks(ex, after, core_idx, tag):
    bufs = ex.wait(after)
    n = len(bufs) // 2
    sums = [_add_half(g, r, core_idx, f"rs_add_{tag}_{i}") for i, (g, r) in enumerate(zip(bufs[:n], bufs[n:]))]
    lands = [lax.empty((3,) + s.shape[1:], s.dtype) for s in sums]
    return _Exchange(sums + lands, _plan_chunks, 3 * n, f"rs_chunks_{tag}")


def _rs_send_share(ex, after, chip_core, tag):
    bufs = ex.wait(after)
    n = len(bufs) // 2
    fulls = [_sum_chips(s, l, chip_core, f"rs_sum_{tag}_{i}") for i, (s, l) in enumerate(zip(bufs[:n], bufs[n:]))]
    return _Exchange(fulls, _plan_share, n, f"rs_share_{tag}")


def _rs_finish(ex, after):
    return [b.reshape(2 * b.shape[1], b.shape[2]) for b in ex.wait(after)]


def _ffn_forward(h, n, gate, wg, wu, wd, tag, next_norm=None):
    ga, up, act = _ffn_gate_up(n, wg, wu, f"{tag}_gate_up")
    h_out, f, *n_next = _mm_residual(act, wd, h, gate, 0.5, f"{tag}_down", next_norm=next_norm)
    return (h_out, *n_next), (h, n, ga, up, act, f)


def _ffn_backward(dh_out, df, saved, gain, sc, wg, wu, wd, core_idx, tag, prev=None, last=False):
    h, n, ga, up, act, _ = saved
    (dwd,) = _wgrad_chunk_lhs([act], df, f"{tag}_dwd")
    ex_d = _rs_send_halves([dwd], f"{tag}_d")
    dga, dup = _ffn_dact(df, wd, ga, up, f"{tag}_dact", deps=[ex_d.token])
    ex_d = _rs_send_chunks(ex_d, [dga], core_idx, f"{tag}_d")
    dwg, dwu = _wgrad_chunk_lhs([dga, dup], n, f"{tag}_dwgu", deps=[ex_d.token])
    ex_gu = _rs_send_halves([dwg, dwu], f"{tag}_gu")
    dn = _mm_reduce([dga, dup], [wg, wu], True, True, f"{tag}_dn", deps=[ex_gu.token])
    if not last:
        ex_gu = _rs_send_chunks(ex_gu, [dn], core_idx, f"{tag}_gu")
    outs = _norm_mod_bwd(dn, h, gain, sc, dh_out, f"{tag}_norm_bwd", prev=prev, deps=[ex_gu.token])
    return outs, (ex_d, ex_gu)


def _pad_cols(v, n):
    return jnp.pad(v, ((0, 0), (0, n - v.shape[1])))


def _mixer_forward(h1, n2, gt2, win, wout, conv_w, conv_dw_b, conv_ln_g, conv_ln_b, attn_out_g, conv_out_g,
                   next_norm=None):
    S, D = h1.shape
    proj = _mm_cols(n2, win, "mix_in")
    cos, sin = _rope_tables(S)
    q, k, v = _qkv_rope(proj, cos, sin, "qkv_rope")
    attn, lse = _attn_fwd(q, k, v, "attn_fwd")
    u1, y = _mixer_merge(proj, attn, conv_w, conv_dw_b, conv_ln_g, conv_ln_b, attn_out_g, conv_out_g, "mix_merge")
    h2, mo, *n_next = _mm_residual(y[None], wout.reshape(1, D, D), h1, gt2, 1.0, "mix_out", next_norm=next_norm)
    return (h2, *n_next), (h1, n2, proj, cos, sin, q, k, v, attn, lse, u1, y, mo)


def _mixer_backward(dh2, dmo, saved, mix_norm_g, sc2, win, wout, conv_w, conv_ln_g, conv_ln_b, attn_out_g,
                    conv_out_g, core_idx, prev=None):
    h1, n2, proj, cos, sin, q, k, v, attn, lse, u1, y, _ = saved
    S, D = h1.shape
    (dwout,) = _wgrad_chunk_lhs([y[None]], dmo, "mix_dwout")
    dy = _mm_nt(dmo, wout.reshape(D, D), "mix_dy")
    dattn, delta, d_attn_g = _attn_out_bwd(dy, attn, attn_out_g, "attn_out_bwd")
    dq, dk, dv = _attn_bwd(q, k, v, dattn, lse, delta, "attn_bwd")
    du1, d_gco, d_lng, d_lnb, d_cb = _conv_bwd_norms(dy, u1, conv_ln_g, conv_ln_b, conv_out_g, "conv_bwd_norms")
    dproj, d_cw = _dproj(du1, proj, conv_w, dq, dk, dv, cos, sin, "mix_dproj")
    (dwin,) = _wgrad_chunk_rhs(n2, [dproj], N_CHIPS, False, "mix_dwin")
    ex = _rs_send_halves([dwin, dwout.reshape(N_CHIPS, D // N_CHIPS, D)], "mix")
    dn2 = _mm_reduce([dproj], [win], False, False, "mix_dn", deps=[ex.token])
    ex = _rs_send_chunks(ex, [dn2], core_idx, "mix")
    outs = _norm_mod_bwd(dn2, h1, mix_norm_g, sc2, dh2, "mix_norm_bwd", prev=prev, deps=[ex.token])
    return outs, ex, (d_cb, d_lng, d_lnb, d_attn_g, d_gco, d_cw)


def kernel(x, c, w_ada, b_ada, ffn1_norm_g, ffn1_w_gate, ffn1_w_up, ffn1_w_down, mix_norm_g, w_in, conv_dw_w, conv_dw_b, conv_ln_g, conv_ln_b, attn_out_g, conv_out_g, w_out, ffn2_norm_g, ffn2_w_gate, ffn2_w_up, ffn2_w_down, final_norm_g, loss_target, m_w_ada, m_b_ada, m_ffn1_norm_g, m_ffn1_w_gate, m_ffn1_w_up, m_ffn1_w_down, m_mix_norm_g, m_w_in, m_conv_dw_w, m_conv_dw_b, m_conv_ln_g, m_conv_ln_b, m_attn_out_g, m_conv_out_g, m_w_out, m_ffn2_norm_g, m_ffn2_w_gate, m_ffn2_w_up, m_ffn2_w_down, m_final_norm_g, v_w_ada, v_b_ada, v_ffn1_norm_g, v_ffn1_w_gate, v_ffn1_w_up, v_ffn1_w_down, v_mix_norm_g, v_w_in, v_conv_dw_w, v_conv_dw_b, v_conv_ln_g, v_conv_ln_b, v_attn_out_g, v_conv_out_g, v_w_out, v_ffn2_norm_g, v_ffn2_w_gate, v_ffn2_w_up, v_ffn2_w_down, v_final_norm_g):
    S, D = x.shape[1], x.shape[2]
    mx, my, mc = _coords()
    chip = 2 * mx + my
    dev = 4 * mx + 2 * my + mc
    chip_idx = chip.astype(jnp.int32).reshape(1)
    core_idx = mc.astype(jnp.int32).reshape(1)
    chip_core = jnp.stack([chip, mc]).astype(jnp.int32)
    h0 = x[0]
    target = loss_target[0]

    ncw = CONV_KERNEL * 128
    n0 = -(-(D + ncw) // 1024) * 1024
    pk0 = _pad_cols(jnp.concatenate([c.reshape(1, D), conv_dw_w.reshape(1, ncw)], axis=1), n0)
    g0 = _all_gather8(pk0.reshape(8, n0 // 8), "gather_c").reshape(N_DEV, n0)
    c_all = g0[:, :D]
    conv_w = jnp.concatenate([g0[2 * kc, D:D + ncw].reshape(CONV_KERNEL, 128) for kc in range(N_CHIPS)], axis=1)
    conv_w = jnp.pad(conv_w, ((0, HALO - CONV_KERNEL), (0, 0)))
    nmod = w_ada.shape[2]
    b_shard = lax.dynamic_slice(b_ada, (0, chip * nmod), (1, nmod))
    mod_part = _ada_fwd(c_all, w_ada[0], b_shard, "ada_fwd")
    g1 = _all_gather8(mod_part, "gather_mod")
    mod_all = jnp.concatenate([g1[16 * kc:16 * kc + 8] for kc in range(N_CHIPS)], axis=1)
    mod = lax.dynamic_slice(mod_all, (dev, 0), (1, 9 * D))
    sh1, sc1, gt1, sh2, sc2, gt2, sh3, sc3, gt3 = [mod[:, i * D:(i + 1) * D] for i in range(9)]

    def gather_start(ws, tag, dep):
        slots = [_cast_place(w, chip_idx, f"cast_{tag}_{i}", deps=[dep]) for i, w in enumerate(ws)]
        return _Exchange(slots, _plan_gather, 3 * len(ws), f"gather_{tag}")

    def swap_start(ex, after, tag):
        return _Exchange(ex.wait(after), _plan_swap, 3 * len(ex.bufs), f"swap_{tag}")

    ex_gu1 = gather_start([ffn1_w_gate[0].T, ffn1_w_up[0].T], "ffn1_gu", g1)
    n1 = _norm_mod(h0, ffn1_norm_g, sc1, sh1, "ffn1_norm", deps=[ex_gu1.token])
    ex_d1 = gather_start([ffn1_w_down[0]], "ffn1_d", n1)
    ex_wm = gather_start([w_in[0], w_out[0]], "mix", ex_d1.token)
    ex_w2 = gather_start([ffn2_w_gate[0].T, ffn2_w_up[0].T, ffn2_w_down[0]], "ffn2", ex_wm.token)

    wg1, wu1 = swap_start(ex_gu1, [ex_w2.token], "ffn1_gu").wait([])
    ga1, up1, act1 = _ffn_gate_up(n1, wg1, wu1, "ffn1_gate_up")
    (wd1,) = swap_start(ex_d1, [act1], "ffn1_d").wait([])
    ex_wm = swap_start(ex_wm, [wd1], "mix")
    h1, f1, n2 = _mm_residual(act1, wd1, h0, gt1, 0.5, "ffn1_down", next_norm=(mix_norm_g, sc2, sh2))
    saved1 = (h0, n1, ga1, up1, act1, f1)
    win, wout = ex_wm.wait([h1])
    ex_w2 = swap_start(ex_w2, [h1], "ffn2")
    (h2, n3), saved2 = _mixer_forward(h1, n2, gt2, win, wout, conv_w, conv_dw_b, conv_ln_g, conv_ln_b,
                                      attn_out_g, conv_out_g, next_norm=(ffn2_norm_g, sc3, sh3))
    wg2, wu2, wd2 = ex_w2.wait([h2])
    (h3,), saved3 = _ffn_forward(h2, n3, gt3, wg2, wu2, wd2, "ffn2")
    loss_part, dh3, d_final_g, df3, d_gt3 = _loss_head(h3, final_norm_g.reshape(1, D), target, saved3[5], gt3, 0.5,
                                                       "loss_head")

    (dh2, d_sh3, d_sc3, d_gain3, dmo, d_gt2), (ex_d2, ex_gu2) = _ffn_backward(
        dh3, df3, saved3, ffn2_norm_g, sc3, wg2, wu2, wd2, core_idx, "ffn2", prev=(saved2[12], gt2, 1.0))
    (dh1, d_sh2, d_sc2, d_gain2, df1, d_gt1), ex_mix, small_mix = _mixer_backward(
        dh2, dmo, saved2, mix_norm_g, sc2, win, wout, conv_w, conv_ln_g, conv_ln_b, attn_out_g, conv_out_g, core_idx,
        prev=(f1, gt1, 0.5))
    d_cb, d_lng, d_lnb, d_attn_g, d_gco, d_cw = small_mix
    (dh0, d_sh1, d_sc1, d_gain1), (ex_d1, ex_gu1) = _ffn_backward(
        dh1, df1, saved1, ffn1_norm_g, sc1, wg1, wu1, wd1, core_idx, "ffn1", last=True)

    dmod = jnp.concatenate([d_sh1, d_sc1, d_gt1, d_sh2, d_sc2, d_gt2, d_sh3, d_sc3, d_gt3], axis=1)
    small = [d_gain1, d_gain2, d_gain3, d_final_g, d_cb, d_lng, d_lnb, d_attn_g, d_gco,
             d_cw[:CONV_KERNEL].reshape(1, CONV_KERNEL * CONV_WIDTH), loss_part]
    pk1 = jnp.concatenate([dmod] + small, axis=1)
    n1_ = -(-pk1.shape[1] // 1024) * 1024
    gathered = _all_gather8(_pad_cols(pk1, n1_).reshape(8, n1_ // 8), "gather_small").reshape(N_DEV, n1_)
    ex_gu1 = _rs_send_chunks(ex_gu1, [gathered], core_idx, "ffn1_gu")
    tot = _sum_rows(gathered, "sum_small")
    off = [0]

    def take(nel):
        out = tot[:, off[0]:off[0] + nel]
        off[0] += nel
        return out

    g_b_ada = take(9 * D)
    g_ffn1_norm, g_mix_norm, g_ffn2_norm, g_final = take(D), take(D), take(D), take(D)
    g_cb, g_lng, g_lnb, g_attn_g, g_gco = take(512), take(512), take(512), take(512), take(512)
    g_cw_full = take(CONV_KERNEL * CONV_WIDTH).reshape(CONV_KERNEL, CONV_WIDTH)
    loss = take(128)[0, 0]
    g_cw = lax.dynamic_slice(g_cw_full, (0, chip * 128), (CONV_KERNEL, 128))

    dmod_shard = lax.dynamic_slice(gathered[:, :9 * D], (0, chip * nmod), (N_DEV, nmod))
    dmod16 = jnp.pad(dmod_shard, ((0, N_DEV), (0, 0)))
    c_t16 = jnp.pad(c_all.T, ((0, 0), (0, N_DEV)))
    g_w_ada = _ada_wgrad(c_t16, dmod16, "ada_wgrad")

    names = ["w_ada", "b_ada", "ffn1_norm_g", "ffn1_w_gate", "ffn1_w_up", "ffn1_w_down", "mix_norm_g", "w_in",
             "conv_dw_w", "conv_dw_b", "conv_ln_g", "conv_ln_b", "attn_out_g", "conv_out_g", "w_out", "ffn2_norm_g",
             "ffn2_w_gate", "ffn2_w_up", "ffn2_w_down", "final_norm_g"]
    weights = dict(zip(names, [w_ada, b_ada, ffn1_norm_g, ffn1_w_gate, ffn1_w_up, ffn1_w_down, mix_norm_g, w_in,
                               conv_dw_w, conv_dw_b, conv_ln_g, conv_ln_b, attn_out_g, conv_out_g, w_out,
                               ffn2_norm_g, ffn2_w_gate, ffn2_w_up, ffn2_w_down, final_norm_g]))
    ms = dict(zip(names, [m_w_ada, m_b_ada, m_ffn1_norm_g, m_ffn1_w_gate, m_ffn1_w_up, m_ffn1_w_down, m_mix_norm_g,
                          m_w_in, m_conv_dw_w, m_conv_dw_b, m_conv_ln_g, m_conv_ln_b, m_attn_out_g, m_conv_out_g,
                          m_w_out, m_ffn2_norm_g, m_ffn2_w_gate, m_ffn2_w_up, m_ffn2_w_down, m_final_norm_g]))
    vs = dict(zip(names, [v_w_ada, v_b_ada, v_ffn1_norm_g, v_ffn1_w_gate, v_ffn1_w_up, v_ffn1_w_down, v_mix_norm_g,
                          v_w_in, v_conv_dw_w, v_conv_dw_b, v_conv_ln_g, v_conv_ln_b, v_attn_out_g, v_conv_out_g,
                          v_w_out, v_ffn2_norm_g, v_ffn2_w_gate, v_ffn2_w_up, v_ffn2_w_down, v_final_norm_g]))
    grads, deltas, new_ms, new_vs = {}, {}, {}, {}

    def adamw_big(nm, g2d, deps=(), transposed=False):
        shape = weights[nm].shape
        two_d = (shape[-2], shape[-1])

        def view(t):
            return t.reshape(two_d).T if transposed else t.reshape(two_d)

        d_, m_, v_ = _adamw(view(weights[nm]), g2d, view(ms[nm]), view(vs[nm]), f"adamw_{nm}", deps=deps)
        grads[nm], deltas[nm], new_ms[nm], new_vs[nm] = (
            (t.T if transposed else t).reshape(shape) for t in (g2d, d_, m_, v_))
        return d_

    d_ada = adamw_big("w_ada", g_w_ada, deps=[ex_gu1.token])
    small_grads = {"b_ada": g_b_ada, "ffn1_norm_g": g_ffn1_norm, "mix_norm_g": g_mix_norm, "conv_dw_w": g_cw,
                   "conv_dw_b": g_cb, "conv_ln_g": g_lng, "conv_ln_b": g_lnb, "attn_out_g": g_attn_g,
                   "conv_out_g": g_gco, "ffn2_norm_g": g_ffn2_norm, "final_norm_g": g_final}
    small_names = [nm for nm in names if nm in small_grads]

    def pack_small(arrs):
        flat = jnp.concatenate([arrs[nm].reshape(1, -1) for nm in small_names], axis=1)
        npad = -(-flat.shape[1] // 1024) * 1024
        return _pad_cols(flat, npad).reshape(8, npad // 8)

    d_s, m_s, v_s = _adamw(pack_small(weights), pack_small(small_grads), pack_small(ms), pack_small(vs),
                           "adamw_small")
    pos = 0
    for nm in small_names:
        shape, nel = weights[nm].shape, weights[nm].size
        grads[nm] = small_grads[nm].reshape(shape)
        deltas[nm], new_ms[nm], new_vs[nm] = (t.reshape(1, -1)[:, pos:pos + nel].reshape(shape)
                                              for t in (d_s, m_s, v_s))
        pos += nel

    ex_d2 = _rs_send_share(ex_d2, [d_ada, d_s], chip_core, "ffn2_d")
    ex_gu2 = _rs_send_share(ex_gu2, [ex_d2.token], chip_core, "ffn2_gu")
    ex_mix = _rs_send_share(ex_mix, [ex_gu2.token], chip_core, "mix")
    ex_d1 = _rs_send_share(ex_d1, [ex_mix.token], chip_core, "ffn1_d")
    (g_wd2,) = _rs_finish(ex_d2, [ex_d1.token])
    last = [adamw_big("ffn2_w_down", g_wd2)]
    g_wg2, g_wu2 = _rs_finish(ex_gu2, last)
    last = [adamw_big("ffn2_w_gate", g_wg2, transposed=True), adamw_big("ffn2_w_up", g_wu2, transposed=True)]
    g_win, g_wout = _rs_finish(ex_mix, last)
    last = [adamw_big("w_in", g_win), adamw_big("w_out", g_wout)]
    (g_wd1,) = _rs_finish(ex_d1, last)
    last = [adamw_big("ffn1_w_down", g_wd1)]
    ex_gu1 = _rs_send_share(ex_gu1, last, chip_core, "ffn1_gu")
    g_wg1, g_wu1 = _rs_finish(ex_gu1, [])
    adamw_big("ffn1_w_gate", g_wg1, transposed=True)
    adamw_big("ffn1_w_up", g_wu1, transposed=True)

    return (loss, dh0[None], *[grads[nm] for nm in names], *[deltas[nm] for nm in names],
            *[new_ms[nm] for nm in names], *[new_vs[nm] for nm in names])
```

```python
import jax
import jax.numpy as jnp
import numpy as np
from jax import lax
from jax.experimental import pallas as pl
from jax.experimental.pallas import tpu as pltpu

F32 = jnp.float32
BF16 = jnp.bfloat16
MESH = pl.DeviceIdType.MESH

RMS_EPS = 1e-6
LN_EPS = 1e-5
HEAD_DIM = 64
ATTN_WIDTH = 512
CONV_WIDTH = 512
ATTN_BLOCK = 128
DILATIONS = (1, 4, 16)
SUPER_ROWS = ATTN_BLOCK * 16
ROPE_THETA = 10000.0
CONV_KERNEL = 31
HALO = 32
N_CHIPS = 4
N_DEV = 8
ADAM_LR, ADAM_B1, ADAM_B2, ADAM_EPS, ADAM_WD, ADAM_STEP = 0.001, 0.9, 0.999, 1e-08, 0.01, 10
VMEM_LIMIT_BYTES = 48 * 1024 * 1024
VMEM_LIMIT_STREAMING = 62 * 1024 * 1024
NEG = -1e30

NT = (((1,), (1,)), ((), ()))
TN = (((0,), (0,)), ((), ()))

ANY_SPEC = pl.BlockSpec(memory_space=pl.ANY)
HBM_SPEC = pl.BlockSpec(memory_space=pltpu.HBM)
SEM_SPEC = pl.BlockSpec(memory_space=pltpu.SEMAPHORE)
DATAFLOW = pltpu.SideEffectType.DATAFLOW_SIDE_EFFECTING
XY_FLIPS = ((0, 1), (1, 0), (1, 1))


def _params(sem=None, streaming=False):
    return pltpu.CompilerParams(dimension_semantics=sem,
                                vmem_limit_bytes=VMEM_LIMIT_STREAMING if streaming else VMEM_LIMIT_BYTES)


def _row_tile(rows, want):
    t = min(rows, want)
    assert rows % t == 0
    return t


def _sigmoid(x):
    return 1.0 / (1.0 + jnp.exp(-x))


def _deps(deps):
    return [d for d in deps if d is not None]


def _blind_to(body, n_in, n_dep):
    def wrapped(*refs):
        return body(*refs[:n_in], *refs[n_in + n_dep:])
    return wrapped


def _vec_spec(d, ngrid):
    if ngrid == 1:
        return pl.BlockSpec((1, d), lambda i: (0, 0))
    return pl.BlockSpec((1, d), lambda i, j: (0, 0))


def _norm_mod(h, gain, sc, sh, name, deps=()):
    S, D = h.shape
    tr = _row_tile(S, 512)
    deps = _deps(deps)

    def body(h_ref, g_ref, sc_ref, sh_ref, n_ref):
        x = h_ref[...]
        r = lax.rsqrt(jnp.mean(x * x, axis=-1, keepdims=True) + RMS_EPS)
        y = (x * r) * g_ref[...]
        n_ref[...] = (y * (1.0 + sc_ref[...]) + sh_ref[...]).astype(BF16)

    row = pl.BlockSpec((tr, D), lambda i: (i, 0))
    return pl.pallas_call(
        _blind_to(body, 4, len(deps)), name=name, grid=(S // tr,),
        in_specs=[row, _vec_spec(D, 1), _vec_spec(D, 1), _vec_spec(D, 1)] + [ANY_SPEC] * len(deps),
        out_specs=row, out_shape=jax.ShapeDtypeStruct((S, D), BF16),
        compiler_params=_params(("parallel",)),
    )(h, gain, sc, sh, *deps)


def _norm_mod_bwd(dn, h_in, gain, sc, dh_out, name, prev=None, deps=()):
    S, D = h_in.shape
    tr = _row_tile(S, 512)
    deps = _deps(deps)
    n_in = 5 if prev is None else 7

    def body(*refs):
        dn_ref, h_ref, g_ref, sc_ref, dho_ref = refs[:5]
        dh_ref, dsh_ref, dsc_ref, dg_ref = refs[n_in:n_in + 4]

        @pl.when(pl.program_id(0) == 0)
        def _():
            for ref in refs[n_in + 1:n_in + 4] + refs[n_in + 5:]:
                ref[...] = jnp.zeros_like(ref)

        x = h_ref[...]
        dn_ = dn_ref[...]
        g = g_ref[...]
        one_sc = 1.0 + sc_ref[...]
        r = lax.rsqrt(jnp.mean(x * x, axis=-1, keepdims=True) + RMS_EPS)
        xh = x * r
        dsh_ref[...] += jnp.sum(dn_, axis=0, keepdims=True)
        dsc_ref[...] += jnp.sum(dn_ * (xh * g), axis=0, keepdims=True)
        dg_ref[...] += jnp.sum(dn_ * one_sc * xh, axis=0, keepdims=True)
        dxh = dn_ * (g * one_sc)
        dh = dho_ref[...] + r * (dxh - xh * jnp.mean(dxh * xh, axis=-1, keepdims=True))
        dh_ref[...] = dh
        if prev is not None:
            _gate_back(dh, refs[5], refs[6], prev[2], refs[n_in + 4], refs[n_in + 5])

    row = pl.BlockSpec((tr, D), lambda i: (i, 0))
    vec = _vec_spec(D, 1)
    extra_in = [] if prev is None else [row, vec]
    extra_out = [] if prev is None else [row, vec]
    extra_shape = [] if prev is None else [jax.ShapeDtypeStruct((S, D), BF16), jax.ShapeDtypeStruct((1, D), F32)]
    return pl.pallas_call(
        _blind_to(body, n_in, len(deps)), name=name, grid=(S // tr,),
        in_specs=[row, row, vec, vec, row] + extra_in + [ANY_SPEC] * len(deps),
        out_specs=[row, vec, vec, vec] + extra_out,
        out_shape=[jax.ShapeDtypeStruct((S, D), F32)] + [jax.ShapeDtypeStruct((1, D), F32)] * 3 + extra_shape,
        compiler_params=_params(("arbitrary",)),
    )(dn, h_in, gain, sc, dh_out, *([] if prev is None else prev[:2]), *deps)


def _gate_back(dh, f_ref, gate_ref, coef, df_ref, dgate_ref):
    df_ref[...] = ((coef * gate_ref[...]) * dh).astype(BF16)
    dgate_ref[...] += jnp.sum(coef * dh * f_ref[...].astype(F32), axis=0, keepdims=True)


def _loss_head(h, gain, target, f, gate, coef, name):
    S, D = h.shape
    tr = _row_tile(S, 512)

    def body(h_ref, g_ref, t_ref, f_ref, gate_ref, loss_ref, dh_ref, dg_ref, df_ref, dgate_ref):
        @pl.when(pl.program_id(0) == 0)
        def _():
            loss_ref[...] = jnp.zeros_like(loss_ref)
            dg_ref[...] = jnp.zeros_like(dg_ref)
            dgate_ref[...] = jnp.zeros_like(dgate_ref)

        x = h_ref[...]
        g = g_ref[...]
        r = lax.rsqrt(jnp.mean(x * x, axis=-1, keepdims=True) + RMS_EPS)
        xh = x * r
        err = xh * g - t_ref[...]
        part = 0.5 * jnp.sum(jnp.mean(err * err, axis=-1, keepdims=True), axis=0, keepdims=True)
        loss_ref[...] += jnp.broadcast_to(part, loss_ref.shape)
        dy = err * (1.0 / D)
        dg_ref[...] += jnp.sum(dy * xh, axis=0, keepdims=True)
        dxh = dy * g
        dh = r * (dxh - xh * jnp.mean(dxh * xh, axis=-1, keepdims=True))
        dh_ref[...] = dh
        _gate_back(dh, f_ref, gate_ref, coef, df_ref, dgate_ref)

    row = pl.BlockSpec((tr, D), lambda i: (i, 0))
    vec = _vec_spec(D, 1)
    return pl.pallas_call(
        body, name=name, grid=(S // tr,),
        in_specs=[row, vec, row, row, vec],
        out_specs=[pl.BlockSpec((1, 128), lambda i: (0, 0)), row, vec, row, vec],
        out_shape=[jax.ShapeDtypeStruct((1, 128), F32), jax.ShapeDtypeStruct((S, D), F32),
                   jax.ShapeDtypeStruct((1, D), F32), jax.ShapeDtypeStruct((S, D), BF16),
                   jax.ShapeDtypeStruct((1, D), F32)],
        compiler_params=_params(("arbitrary",)),
    )(h, gain, target, f, gate)


def _ffn_gate_up(n, wg_t, wu_t, name):
    S, D = n.shape
    nk, w, _ = wg_t.shape
    tm = _row_tile(S, 512)

    def body(n_ref, wg_ref, wu_ref, ga_ref, up_ref, act_ref):
        x = n_ref[...]
        ga = lax.dot_general(x, wg_ref[...], NT, preferred_element_type=F32)
        up = lax.dot_general(x, wu_ref[...], NT, preferred_element_type=F32)
        ga_ref[...] = ga.astype(BF16)
        up_ref[...] = up.astype(BF16)
        act_ref[...] = ((ga * _sigmoid(ga)) * up).astype(BF16)

    wspec = pl.BlockSpec((None, w, D), lambda k, m: (k, 0, 0))
    ospec = pl.BlockSpec((None, tm, w), lambda k, m: (k, m, 0))
    out = jax.ShapeDtypeStruct((nk, S, w), BF16)
    return pl.pallas_call(
        body, name=name, grid=(nk, S // tm),
        in_specs=[pl.BlockSpec((tm, D), lambda k, m: (m, 0)), wspec, wspec],
        out_specs=[ospec, ospec, ospec], out_shape=[out, out, out],
        compiler_params=_params(("parallel", "parallel")),
    )(n, wg_t, wu_t)


def _mm_residual(lhs, w, h_in, gvec, coef, name, next_norm=None):
    nk, S, kc = lhs.shape
    D = w.shape[2]
    tm = _row_tile(S, 512)
    n_in = 4 if next_norm is None else 7

    def body(*refs):
        l_ref, w_ref, h_ref, g_ref = refs[:4]
        ho_ref, f_ref = refs[n_in:n_in + 2]
        acc_ref = refs[-1]
        k = pl.program_id(1)

        @pl.when(k == 0)
        def _():
            acc_ref[...] = jnp.zeros_like(acc_ref)

        acc_ref[...] += jnp.dot(l_ref[...], w_ref[...], preferred_element_type=F32)

        @pl.when(k == nk - 1)
        def _():
            f = acc_ref[...]
            f_ref[...] = f.astype(BF16)
            x = h_ref[...] + (coef * g_ref[...]) * f
            ho_ref[...] = x
            if next_norm is not None:
                ng_ref, sc_ref, sh_ref = refs[4:7]
                r = lax.rsqrt(jnp.mean(x * x, axis=-1, keepdims=True) + RMS_EPS)
                y = (x * r) * ng_ref[...]
                refs[n_in + 2][...] = (y * (1.0 + sc_ref[...]) + sh_ref[...]).astype(BF16)

    row = pl.BlockSpec((tm, D), lambda m, k: (m, 0))
    vec = _vec_spec(D, 2)
    with_n = next_norm is not None
    return pl.pallas_call(
        body, name=name, grid=(S // tm, nk),
        in_specs=[pl.BlockSpec((None, tm, kc), lambda m, k: (k, m, 0)),
                  pl.BlockSpec((None, kc, D), lambda m, k: (k, 0, 0)), row, vec] + [vec] * (3 * with_n),
        out_specs=[row, row] + [row] * with_n,
        out_shape=[jax.ShapeDtypeStruct((S, D), F32), jax.ShapeDtypeStruct((S, D), BF16)]
        + [jax.ShapeDtypeStruct((S, D), BF16)] * with_n,
        scratch_shapes=[pltpu.VMEM((tm, D), F32)],
        compiler_params=_params(("parallel", "arbitrary")),
    )(lhs, w, h_in, gvec, *(next_norm or ()))


def _mm_cols(n, w, name):
    S, D = n.shape
    nk, _, wd = w.shape
    assert wd % 128 == 0
    tm = _row_tile(S, 512)

    def body(n_ref, w_ref, o_ref):
        o_ref[...] = jnp.dot(n_ref[...], w_ref[...], preferred_element_type=F32)

    return pl.pallas_call(
        body, name=name, grid=(nk, S // tm),
        in_specs=[pl.BlockSpec((tm, D), lambda k, m: (m, 0)), pl.BlockSpec((None, D, wd), lambda k, m: (k, 0, 0))],
        out_specs=pl.BlockSpec((tm, wd), lambda k, m: (m, k)),
        out_shape=jax.ShapeDtypeStruct((S, nk * wd), F32),
        compiler_params=_params(("parallel", "parallel")),
    )(n, w)


def _ffn_dact(df, wd, ga, up, name, deps=()):
    S, D = df.shape
    nk, w, _ = wd.shape
    tm = _row_tile(S, 512)
    deps = _deps(deps)

    def body(df_ref, wd_ref, ga_ref, up_ref, dga_ref, dup_ref):
        dact = lax.dot_general(df_ref[...], wd_ref[...], NT, preferred_element_type=F32)
        ga_ = ga_ref[...].astype(F32)
        up_ = up_ref[...].astype(F32)
        sig = _sigmoid(ga_)
        dga_ref[...] = (dact * up_ * (sig * (1.0 + ga_ * (1.0 - sig)))).astype(BF16)
        dup_ref[...] = (dact * (ga_ * sig)).astype(BF16)

    cspec = pl.BlockSpec((None, tm, w), lambda k, m: (k, m, 0))
    out = jax.ShapeDtypeStruct((nk, S, w), BF16)
    return pl.pallas_call(
        _blind_to(body, 4, len(deps)), name=name, grid=(nk, S // tm),
        in_specs=[pl.BlockSpec((tm, D), lambda k, m: (m, 0)), pl.BlockSpec((None, w, D), lambda k, m: (k, 0, 0)),
                  cspec, cspec] + [ANY_SPEC] * len(deps),
        out_specs=[cspec, cspec], out_shape=[out, out],
        compiler_params=_params(("parallel", "parallel")),
    )(df, wd, ga, up, *deps)


def _mm_nt(d, w, name):
    S, K = d.shape
    N = w.shape[0]
    tm = _row_tile(S, 512)

    def body(d_ref, w_ref, o_ref):
        o_ref[...] = lax.dot_general(d_ref[...], w_ref[...], NT, preferred_element_type=F32)

    return pl.pallas_call(
        body, name=name, grid=(S // tm,),
        in_specs=[pl.BlockSpec((tm, K), lambda m: (m, 0)), pl.BlockSpec((N, K), lambda m: (0, 0))],
        out_specs=pl.BlockSpec((tm, N), lambda m: (m, 0)),
        out_shape=jax.ShapeDtypeStruct((S, N), F32),
        compiler_params=_params(("parallel",)),
    )(d, w)


def _mm_reduce(lhs_list, w_list, chunked3d, w_is_kd, name, deps=()):
    nk = w_list[0].shape[0]
    kc, D = w_list[0].shape[1:] if w_is_kd else w_list[0].shape[:0:-1]
    S = lhs_list[0].shape[1] if chunked3d else lhs_list[0].shape[0]
    tm = _row_tile(S, 512)
    npair = len(lhs_list)
    deps = _deps(deps)

    def body(*refs):
        l_refs, w_refs = refs[:npair], refs[npair:2 * npair]
        o_ref, acc_ref = refs[2 * npair], refs[2 * npair + 1]
        k = pl.program_id(1)

        @pl.when(k == 0)
        def _():
            acc_ref[...] = jnp.zeros_like(acc_ref)

        for l_ref, w_ref in zip(l_refs, w_refs):
            if w_is_kd:
                acc_ref[...] += jnp.dot(l_ref[...], w_ref[...], preferred_element_type=F32)
            else:
                acc_ref[...] += lax.dot_general(l_ref[...], w_ref[...], NT, preferred_element_type=F32)

        @pl.when(k == nk - 1)
        def _():
            o_ref[...] = acc_ref[...]

    if chunked3d:
        lspec = pl.BlockSpec((None, tm, kc), lambda m, k: (k, m, 0))
    else:
        lspec = pl.BlockSpec((tm, kc), lambda m, k: (m, k))
    wspec = pl.BlockSpec((None,) + tuple(w_list[0].shape[1:]), lambda m, k: (k, 0, 0))
    return pl.pallas_call(
        _blind_to(body, 2 * npair, len(deps)), name=name, grid=(S // tm, nk),
        in_specs=[lspec] * npair + [wspec] * npair + [ANY_SPEC] * len(deps),
        out_specs=pl.BlockSpec((tm, D), lambda m, k: (m, 0)),
        out_shape=jax.ShapeDtypeStruct((S, D), F32),
        scratch_shapes=[pltpu.VMEM((tm, D), F32)],
        compiler_params=_params(("parallel", "arbitrary")),
    )(*lhs_list, *w_list, *deps)


def _wgrad_chunk_lhs(lhs_list, rhs, name, deps=()):
    nk, S, w = lhs_list[0].shape
    D = rhs.shape[1]
    ts = _row_tile(S, 512)
    ns = S // ts
    nl = len(lhs_list)
    deps = _deps(deps)

    def body(*refs):
        l_refs, r_ref = refs[:nl], refs[nl]
        o_refs, acc_refs = refs[nl + 1:2 * nl + 1], refs[2 * nl + 1:]
        s = pl.program_id(1)

        @pl.when(s == 0)
        def _():
            for acc_ref in acc_refs:
                acc_ref[...] = jnp.zeros_like(acc_ref)

        x = r_ref[...]
        for l_ref, acc_ref in zip(l_refs, acc_refs):
            acc_ref[...] += lax.dot_general(l_ref[...], x, TN, preferred_element_type=F32)

        @pl.when(s == ns - 1)
        def _():
            for o_ref, acc_ref in zip(o_refs, acc_refs):
                o_ref[...] = acc_ref[...]

    return pl.pallas_call(
        _blind_to(body, nl + 1, len(deps)), name=name, grid=(nk, ns),
        in_specs=[pl.BlockSpec((None, ts, w), lambda k, s: (k, s, 0))] * nl
        + [pl.BlockSpec((ts, D), lambda k, s: (s, 0))] + [ANY_SPEC] * len(deps),
        out_specs=[pl.BlockSpec((None, w, D), lambda k, s: (k, 0, 0))] * nl,
        out_shape=[jax.ShapeDtypeStruct((nk, w, D), F32)] * nl,
        scratch_shapes=[pltpu.VMEM((w, D), F32)] * nl,
        compiler_params=_params(("parallel", "arbitrary")),
    )(*lhs_list, rhs, *deps)


def _wgrad_chunk_rhs(lhs, rhs_list, nk, chunked3d, name, deps=()):
    S, D = lhs.shape
    w = rhs_list[0].shape[2] if chunked3d else rhs_list[0].shape[1] // nk
    ts = _row_tile(S, 512)
    ns = S // ts
    nr = len(rhs_list)
    deps = _deps(deps)

    def body(*refs):
        l_ref, r_refs = refs[0], refs[1:1 + nr]
        o_refs, acc_refs = refs[1 + nr:1 + 2 * nr], refs[1 + 2 * nr:]
        s = pl.program_id(1)

        @pl.when(s == 0)
        def _():
            for acc_ref in acc_refs:
                acc_ref[...] = jnp.zeros_like(acc_ref)

        x = l_ref[...]
        for r_ref, acc_ref in zip(r_refs, acc_refs):
            acc_ref[...] += lax.dot_general(x, r_ref[...], TN, preferred_element_type=F32)

        @pl.when(s == ns - 1)
        def _():
            for o_ref, acc_ref in zip(o_refs, acc_refs):
                o_ref[...] = acc_ref[...]

    if chunked3d:
        rspec = pl.BlockSpec((None, ts, w), lambda k, s: (k, s, 0))
    else:
        rspec = pl.BlockSpec((ts, w), lambda k, s: (s, k))
    ospec = pl.BlockSpec((None, D, w), lambda k, s: (k, 0, 0))
    return pl.pallas_call(
        _blind_to(body, 1 + nr, len(deps)), name=name, grid=(nk, ns),
        in_specs=[pl.BlockSpec((ts, D), lambda k, s: (s, 0))] + [rspec] * nr + [ANY_SPEC] * len(deps),
        out_specs=[ospec] * nr,
        out_shape=[jax.ShapeDtypeStruct((nk, D, w), F32)] * nr,
        scratch_shapes=[pltpu.VMEM((D, w), F32)] * nr,
        compiler_params=_params(("parallel", "arbitrary")),
    )(lhs, *rhs_list, *deps)


def _rope_tables(S):
    pos = np.arange(S, dtype=np.float32)
    inv_freq = (ROPE_THETA ** (-np.arange(0, HEAD_DIM, 2, dtype=np.float32) / HEAD_DIM)).astype(np.float32)
    ang = (pos[:, None] * inv_freq[None, :]).astype(np.float64)
    cos, sin = np.cos(ang).astype(np.float32), np.sin(ang).astype(np.float32)
    cos2 = np.concatenate([cos, cos, cos, cos], axis=1)
    sin2 = np.concatenate([-sin, sin, -sin, sin], axis=1)
    return jnp.asarray(cos2), jnp.asarray(sin2)


def _rotate(t, cos, sin_signed):
    half = HEAD_DIM // 2
    lane = lax.broadcasted_iota(jnp.int32, t.shape, 1)
    first = (lane % HEAD_DIM) < half
    partner = jnp.where(first, pltpu.roll(t, 128 - half, 1), pltpu.roll(t, half, 1))
    return t * cos + partner * sin_signed


def _qkv_rope(proj, cos, sin, name):
    S = proj.shape[0]
    A = ATTN_WIDTH
    tr = _row_tile(S, 512)
    nb = A // 128
    scale = HEAD_DIM ** -0.5

    def body(q_ref, k_ref, v_ref, c_ref, s_ref, qo_ref, ko_ref, vo_ref):
        c, s = c_ref[...], s_ref[...]
        qo_ref[...] = _rotate(q_ref[...], c, s) * scale
        ko_ref[...] = _rotate(k_ref[...], c, s)
        vo_ref[...] = v_ref[...]

    def col(off):
        return pl.BlockSpec((tr, 128), lambda i, j: (i, off + j))

    tab = pl.BlockSpec((tr, 128), lambda i, j: (i, 0))
    out = jax.ShapeDtypeStruct((S, A), F32)
    return pl.pallas_call(
        body, name=name, grid=(S // tr, nb),
        in_specs=[col(0), col(nb), col(2 * nb), tab, tab],
        out_specs=[col(0), col(0), col(0)], out_shape=[out, out, out],
        compiler_params=_params(("parallel", "parallel")),
    )(proj, proj, proj, cos, sin)


def _band_mask(T, has_prev):
    qi = lax.broadcasted_iota(jnp.int32, (T, 2 * T), 0)
    kj = lax.broadcasted_iota(jnp.int32, (T, 2 * T), 1)
    return ((kj < T) & (kj >= qi) & has_prev) | ((kj >= T) & (kj - T <= qi))


def _branch_blocks(rows, dilation):
    T = min(ATTN_BLOCK, rows // dilation)
    return T, rows // T


def _block_rows(base, T, dilation):
    if dilation == 1:
        return pl.ds(pl.multiple_of(base, T), T)
    return pl.ds(base, T, stride=dilation)


def _attn_fwd(q, k, v, name):
    S, A = q.shape
    sup = min(S, SUPER_ROWS)
    nd = len(DILATIONS)
    assert S % sup == 0

    def body(q_ref, k_ref, v_ref, attn_ref, lse_ref, acc_s, m_s, l_s):
        lane = lax.broadcasted_iota(jnp.int32, (1, 128), 1)
        head0 = lane < HEAD_DIM

        def supertile(st, carry):
            row0 = st * sup
            for di, dil in enumerate(DILATIONS):
                T, nblk = _branch_blocks(sup, dil)
                span = T * dil
                assert T == ATTN_BLOCK or sup == S

                def block(idx, c2, di=di, dil=dil, T=T, span=span):
                    r = idx % dil
                    loc = (idx // dil) * span + r
                    base = row0 + loc
                    rows = _block_rows(base, T, dil)
                    prev = _block_rows(jnp.maximum(base - span, r), T, dil)
                    qb = q_ref[rows, :].astype(BF16)
                    k2 = jnp.concatenate([k_ref[prev, :], k_ref[rows, :]], axis=0).astype(BF16)
                    v2 = jnp.concatenate([v_ref[prev, :], v_ref[rows, :]], axis=0).astype(BF16)
                    valid = _band_mask(T, base >= span)
                    accs, ms, ls = [], [], []
                    for hmask in (head0, jnp.logical_not(head0)):
                        qh = jnp.where(hmask, qb, jnp.zeros_like(qb))
                        s = jnp.where(valid, lax.dot_general(qh, k2, NT, preferred_element_type=F32), NEG)
                        m = jnp.max(s, axis=-1, keepdims=True)
                        p = jnp.exp(s - m)
                        ls.append(jnp.sum(p, axis=-1, keepdims=True))
                        ms.append(m)
                        accs.append(jnp.dot(p.astype(BF16), v2, preferred_element_type=F32))
                    lrows = _block_rows(di * sup + loc, T, dil)
                    acc_s[lrows, :] = jnp.where(head0, accs[0], accs[1])
                    m_s[lrows, :] = jnp.where(head0, ms[0], ms[1])
                    l_s[lrows, :] = jnp.where(head0, ls[0], ls[1])
                    return c2

                lax.fori_loop(0, nblk, block, 0, unroll=4)

            chunk = min(sup, 256)

            def merge(ci, c2):
                lr = [pl.ds(pl.multiple_of(di * sup + ci * chunk, chunk), chunk) for di in range(nd)]
                gr = pl.ds(pl.multiple_of(row0 + ci * chunk, chunk), chunk)
                m0, m1, m2 = m_s[lr[0], :], m_s[lr[1], :], m_s[lr[2], :]
                mm = jnp.maximum(jnp.maximum(m0, m1), m2)
                w0, w1, w2 = jnp.exp(m0 - mm), jnp.exp(m1 - mm), jnp.exp(m2 - mm)
                den = (w0 * l_s[lr[0], :] + w1 * l_s[lr[1], :]) + w2 * l_s[lr[2], :]
                num = (w0 * acc_s[lr[0], :] + w1 * acc_s[lr[1], :]) + w2 * acc_s[lr[2], :]
                attn_ref[gr, :] = num / den
                lse_ref[gr, :] = mm + jnp.log(den)
                return c2

            lax.fori_loop(0, sup // chunk, merge, 0)
            return carry

        lax.fori_loop(0, S // sup, supertile, 0)

    blk = pl.BlockSpec((S, 128), lambda j: (0, j))
    out = jax.ShapeDtypeStruct((S, A), F32)
    return pl.pallas_call(
        body, name=name, grid=(A // 128,),
        in_specs=[blk, blk, blk], out_specs=[blk, blk], out_shape=[out, out],
        scratch_shapes=[pltpu.VMEM((nd * sup, 128), F32)] * 3,
        compiler_params=_params(("parallel",)),
    )(q, k, v)


def _attn_out_bwd(dy, attn, gain, name):
    S, A = attn.shape
    tr = _row_tile(S, 256)

    def body(dy_ref, a_ref, g_ref, da_ref, dl_ref, dg_ref):
        @pl.when(pl.program_id(0) == 0)
        def _():
            dg_ref[...] = jnp.zeros_like(dg_ref)

        x = a_ref[...]
        dy_ = dy_ref[...]
        r = lax.rsqrt(jnp.mean(x * x, axis=-1, keepdims=True) + RMS_EPS)
        xh = x * r
        dg_ref[...] += jnp.sum(dy_ * xh, axis=0, keepdims=True)
        dxh = dy_ * g_ref[...]
        dx = r * (dxh - xh * jnp.mean(dxh * xh, axis=-1, keepdims=True))
        da_ref[...] = dx
        prod = dx * x
        hi = lax.broadcasted_iota(jnp.int32, (A, A), 0) // HEAD_DIM
        hj = lax.broadcasted_iota(jnp.int32, (A, A), 1) // HEAD_DIM
        same_head = (hi == hj).astype(F32)
        dl_ref[...] = jnp.dot(prod, same_head, preferred_element_type=F32, precision=lax.Precision.HIGHEST)

    row = pl.BlockSpec((tr, A), lambda i: (i, 0))
    vec = _vec_spec(A, 1)
    return pl.pallas_call(
        body, name=name, grid=(S // tr,),
        in_specs=[row, row, vec], out_specs=[row, row, vec],
        out_shape=[jax.ShapeDtypeStruct((S, A), F32), jax.ShapeDtypeStruct((S, A), F32),
                   jax.ShapeDtypeStruct((1, A), F32)],
        compiler_params=_params(("arbitrary",)),
    )(dy, attn, gain)


def _attn_bwd(q, k, v, da, lse, delta, name):
    S, A = q.shape

    def body(q_ref, k_ref, v_ref, da_ref, lse_ref, dl_ref, dq_ref, dk_ref, dv_ref):
        lane = lax.broadcasted_iota(jnp.int32, (1, 128), 1)
        head0 = lane < HEAD_DIM
        dq_ref[...] = jnp.zeros_like(dq_ref)
        dk_ref[...] = jnp.zeros_like(dk_ref)
        dv_ref[...] = jnp.zeros_like(dv_ref)
        for dil in DILATIONS:
            T, nblk = _branch_blocks(S, dil)
            span = T * dil

            def block(idx, carry, dil=dil, T=T, span=span):
                r = idx % dil
                base = (idx // dil) * span + r
                rows = _block_rows(base, T, dil)
                prev = _block_rows(jnp.maximum(base - span, r), T, dil)
                qb, dab = q_ref[rows, :].astype(BF16), da_ref[rows, :].astype(BF16)
                k2 = jnp.concatenate([k_ref[prev, :], k_ref[rows, :]], axis=0).astype(BF16)
                v2 = jnp.concatenate([v_ref[prev, :], v_ref[rows, :]], axis=0).astype(BF16)
                lse_b, dl_b = lse_ref[rows, :], dl_ref[rows, :]
                valid = _band_mask(T, base >= span)
                dqs = []
                dk2 = dv2 = None
                for h, hmask in enumerate((head0, jnp.logical_not(head0))):
                    qh = jnp.where(hmask, qb, jnp.zeros_like(qb))
                    dah = jnp.where(hmask, dab, jnp.zeros_like(dab))
                    c0 = h * HEAD_DIM
                    lse_h = lse_b[:, c0:c0 + 1]
                    dl_h = dl_b[:, c0:c0 + 1]
                    s = lax.dot_general(qh, k2, NT, preferred_element_type=F32)
                    p = jnp.where(valid, jnp.exp(s - lse_h), 0.0)
                    dp = lax.dot_general(dah, v2, NT, preferred_element_type=F32)
                    ds = (p * (dp - dl_h)).astype(BF16)
                    dqs.append(jnp.dot(ds, k2, preferred_element_type=F32))
                    t_k = lax.dot_general(ds, qh, TN, preferred_element_type=F32)
                    t_v = lax.dot_general(p.astype(BF16), dah, TN, preferred_element_type=F32)
                    dk2 = t_k if dk2 is None else dk2 + t_k
                    dv2 = t_v if dv2 is None else dv2 + t_v
                dq_ref[rows, :] += jnp.where(head0, dqs[0], dqs[1])
                dk_ref[rows, :] += dk2[T:]
                dv_ref[rows, :] += dv2[T:]
                dk_ref[prev, :] += dk2[:T]
                dv_ref[prev, :] += dv2[:T]
                return carry

            lax.fori_loop(0, nblk, block, 0, unroll=4)

    blk = pl.BlockSpec((S, 128), lambda j: (0, j))
    out = jax.ShapeDtypeStruct((S, A), F32)
    return pl.pallas_call(
        body, name=name, grid=(A // 128,),
        in_specs=[blk] * 6, out_specs=[blk] * 3, out_shape=[out] * 3,
        compiler_params=_params(("parallel",)),
    )(q, k, v, da, lse, delta)


SUBLANES = 8
CONV_CHUNK = 64
FIRST_TAP = HALO - (CONV_KERNEL - 1)


def _store_shifted(shift_s, win, rows):
    shift_s[0, pl.ds(0, rows), :] = win
    for b in range(1, SUBLANES):
        shift_s[b, pl.ds(0, rows - SUBLANES), :] = win[b:b + rows - SUBLANES, :]


def _glu_window(a_ref, b_ref, ah_ref, bh_ref, first):
    u0 = a_ref[...] * _sigmoid(b_ref[...])
    u0h = ah_ref[...] * _sigmoid(bh_ref[...])
    u0h = jnp.where(first, jnp.zeros_like(u0h), u0h)
    return jnp.concatenate([u0h, u0], axis=0)


def _conv_norms(u1, lng, lnb):
    mu = jnp.mean(u1, axis=-1, keepdims=True)
    xc = u1 - mu
    rstd = lax.rsqrt(jnp.mean(xc * xc, axis=-1, keepdims=True) + LN_EPS)
    u1h = xc * rstd
    u2 = u1h * lng + lnb
    sig = _sigmoid(u2)
    u3 = u2 * sig
    r = lax.rsqrt(jnp.mean(u3 * u3, axis=-1, keepdims=True) + RMS_EPS)
    return rstd, u1h, u2, sig, u3, r


def _conv_specs(tr, C, col_a, col_b):
    per = tr // HALO

    def tile(col):
        return pl.BlockSpec((tr, C), lambda i: (i, col))

    def halo(col):
        return pl.BlockSpec((HALO, C), lambda i: (jnp.maximum(i * per - 1, 0), col))

    return tile(col_a), tile(col_b), halo(col_a), halo(col_b)


def _mixer_merge(proj, attn, cw, cb, lng, lnb, gat, gco, name):
    S = proj.shape[0]
    C = CONV_WIDTH
    A = attn.shape[1]
    tr = _row_tile(S, 256)

    def body(a_ref, b_ref, ah_ref, bh_ref, at_ref, w_ref, cb_ref, lng_ref, lnb_ref, gat_ref, gco_ref, u1_ref, y_ref,
             shift_s):
        _store_shifted(shift_s, _glu_window(a_ref, b_ref, ah_ref, bh_ref, pl.program_id(0) == 0), tr + HALO)

        def chunk(rc, carry):
            r0 = pl.multiple_of(rc * CONV_CHUNK, CONV_CHUNK)
            for lb in range(C // 128):
                lanes = slice(lb * 128, (lb + 1) * 128)
                acc = jnp.broadcast_to(cb_ref[:, lanes], (CONV_CHUNK, 128))
                for j in range(CONV_KERNEL):
                    a8, b = divmod(FIRST_TAP + j, SUBLANES)
                    acc = acc + w_ref[j:j + 1, lanes] * shift_s[b, pl.ds(r0 + a8 * SUBLANES, CONV_CHUNK), lanes]
                u1_ref[pl.ds(r0, CONV_CHUNK), lanes] = acc
            return carry

        lax.fori_loop(0, tr // CONV_CHUNK, chunk, 0)
        _, _, _, _, u3, r = _conv_norms(u1_ref[...], lng_ref[...], lnb_ref[...])
        y_ref[:, A:] = ((u3 * r) * gco_ref[...]).astype(BF16)
        x = at_ref[...]
        ra = lax.rsqrt(jnp.mean(x * x, axis=-1, keepdims=True) + RMS_EPS)
        y_ref[:, :A] = ((x * ra) * gat_ref[...]).astype(BF16)

    ta, tb, ha, hb = _conv_specs(tr, C, 3, 4)
    row = pl.BlockSpec((tr, C), lambda i: (i, 0))
    vec = _vec_spec(C, 1)
    return pl.pallas_call(
        body, name=name, grid=(S // tr,),
        in_specs=[ta, tb, ha, hb, pl.BlockSpec((tr, A), lambda i: (i, 0)), pl.BlockSpec((HALO, C), lambda i: (0, 0)),
                  vec, vec, vec, _vec_spec(A, 1), vec],
        out_specs=[row, pl.BlockSpec((tr, A + C), lambda i: (i, 0))],
        out_shape=[jax.ShapeDtypeStruct((S, C), F32), jax.ShapeDtypeStruct((S, A + C), BF16)],
        scratch_shapes=[pltpu.VMEM((SUBLANES, tr + HALO, C), F32)],
        compiler_params=_params(("parallel",)),
    )(proj, proj, proj, proj, attn, cw, cb, lng, lnb, gat, gco)


def _conv_bwd_norms(dy, u1, lng, lnb, gco, name):
    S, C = u1.shape
    tr = _row_tile(S, 256)

    def body(dy_ref, u1_ref, lng_ref, lnb_ref, gco_ref, du1_ref, dgco_ref, dlng_ref, dlnb_ref, dcb_ref):
        @pl.when(pl.program_id(0) == 0)
        def _():
            for ref in (dgco_ref, dlng_ref, dlnb_ref, dcb_ref):
                ref[...] = jnp.zeros_like(ref)

        lng = lng_ref[...]
        rstd, u1h, u2, sig, u3, r = _conv_norms(u1_ref[...], lng, lnb_ref[...])
        dy_ = dy_ref[...]
        u3h = u3 * r
        dgco_ref[...] += jnp.sum(dy_ * u3h, axis=0, keepdims=True)
        du3h = dy_ * gco_ref[...]
        du3 = r * (du3h - u3h * jnp.mean(du3h * u3h, axis=-1, keepdims=True))
        du2 = du3 * (sig * (1.0 + u2 * (1.0 - sig)))
        dlng_ref[...] += jnp.sum(du2 * u1h, axis=0, keepdims=True)
        dlnb_ref[...] += jnp.sum(du2, axis=0, keepdims=True)
        du1h = du2 * lng
        du1 = rstd * (du1h - jnp.mean(du1h, axis=-1, keepdims=True)
                      - u1h * jnp.mean(du1h * u1h, axis=-1, keepdims=True))
        du1_ref[...] = du1
        dcb_ref[...] += jnp.sum(du1, axis=0, keepdims=True)

    row = pl.BlockSpec((tr, C), lambda i: (i, 0))
    vec = _vec_spec(C, 1)
    return pl.pallas_call(
        body, name=name, grid=(S // tr,),
        in_specs=[pl.BlockSpec((tr, C), lambda i: (i, 1)), row, vec, vec, vec],
        out_specs=[row, vec, vec, vec, vec],
        out_shape=[jax.ShapeDtypeStruct((S, C), F32)] + [jax.ShapeDtypeStruct((1, C), F32)] * 4,
        compiler_params=_params(("arbitrary",)),
    )(dy, u1, lng, lnb, gco)


def _dproj(du1, proj, cw, dq, dk, dv, cos, sin, name):
    S, C = du1.shape
    A = dq.shape[1]
    tr = _row_tile(S, 256)
    nt = S // tr
    per = tr // HALO
    scale = HEAD_DIM ** -0.5

    def body(du_ref, dun_ref, a_ref, b_ref, ah_ref, bh_ref, w_ref, dq_ref, dk_ref, dv_ref, cos_ref, sin_ref,
             dp_ref, dw_ref, win_s, dwin_s, du0_s, tap_s):
        i = pl.program_id(0)

        @pl.when(i == 0)
        def _():
            dw_ref[...] = jnp.zeros_like(dw_ref)

        _store_shifted(win_s, _glu_window(a_ref, b_ref, ah_ref, bh_ref, i == 0), tr + HALO)
        nxt = jnp.where(i == nt - 1, jnp.zeros_like(dun_ref[...]), dun_ref[...])
        _store_shifted(dwin_s, jnp.concatenate([du_ref[...], nxt], axis=0), tr + HALO)
        tap_s[...] = jnp.zeros_like(tap_s)

        def chunk(rc, carry):
            r0 = pl.multiple_of(rc * CONV_CHUNK, CONV_CHUNK)
            for lb in range(C // 128):
                lanes = slice(lb * 128, (lb + 1) * 128)
                du = du_ref[pl.ds(r0, CONV_CHUNK), lanes]
                acc = jnp.zeros((CONV_CHUNK, 128), F32)
                for j in range(CONV_KERNEL):
                    a8, b = divmod(CONV_KERNEL - 1 - j, SUBLANES)
                    acc = acc + w_ref[j:j + 1, lanes] * dwin_s[b, pl.ds(r0 + a8 * SUBLANES, CONV_CHUNK), lanes]
                    a8, b = divmod(FIRST_TAP + j, SUBLANES)
                    prod = du * win_s[b, pl.ds(r0 + a8 * SUBLANES, CONV_CHUNK), lanes]
                    part = prod[0:SUBLANES]
                    for g in range(1, CONV_CHUNK // SUBLANES):
                        part = part + prod[g * SUBLANES:(g + 1) * SUBLANES]
                    tap_s[j * SUBLANES:(j + 1) * SUBLANES, lanes] += part
                du0_s[pl.ds(r0, CONV_CHUNK), lanes] = acc
            return carry

        lax.fori_loop(0, tr // CONV_CHUNK, chunk, 0)
        taps = [jnp.sum(tap_s[j * SUBLANES:(j + 1) * SUBLANES, :], axis=0, keepdims=True)
                for j in range(CONV_KERNEL)]
        taps.append(jnp.zeros((HALO - CONV_KERNEL, C), F32))
        dw_ref[...] += jnp.concatenate(taps, axis=0)
        du0 = du0_s[...]
        a, sig = a_ref[...], _sigmoid(b_ref[...])
        dp_ref[:, 3 * A:3 * A + C] = (du0 * sig).astype(BF16)
        dp_ref[:, 3 * A + C:] = (du0 * a * sig * (1.0 - sig)).astype(BF16)
        cos_, nsin = cos_ref[...], -sin_ref[...]
        for j in range(A // 128):
            lanes = slice(j * 128, (j + 1) * 128)
            dp_ref[:, j * 128:(j + 1) * 128] = (_rotate(dq_ref[:, lanes], cos_, nsin) * scale).astype(BF16)
            dp_ref[:, A + j * 128:A + (j + 1) * 128] = _rotate(dk_ref[:, lanes], cos_, nsin).astype(BF16)
        dp_ref[:, 2 * A:3 * A] = dv_ref[...].astype(BF16)

    ta, tb, ha, hb = _conv_specs(tr, C, 3, 4)
    row = pl.BlockSpec((tr, C), lambda i: (i, 0))
    arow = pl.BlockSpec((tr, A), lambda i: (i, 0))
    tab = pl.BlockSpec((tr, 128), lambda i: (i, 0))
    nxt = pl.BlockSpec((HALO, C), lambda i: (jnp.minimum((i + 1) * per, S // HALO - 1), 0))
    wspec = pl.BlockSpec((HALO, C), lambda i: (0, 0))
    return pl.pallas_call(
        body, name=name, grid=(nt,),
        in_specs=[row, nxt, ta, tb, ha, hb, wspec, arow, arow, arow, tab, tab],
        out_specs=[pl.BlockSpec((tr, 3 * A + 2 * C), lambda i: (i, 0)), wspec],
        out_shape=[jax.ShapeDtypeStruct((S, 3 * A + 2 * C), BF16), jax.ShapeDtypeStruct((HALO, C), F32)],
        scratch_shapes=[pltpu.VMEM((SUBLANES, tr + HALO, C), F32), pltpu.VMEM((SUBLANES, tr + HALO, C), F32),
                        pltpu.VMEM((tr, C), F32), pltpu.VMEM((HALO * SUBLANES, C), F32)],
        compiler_params=_params(("arbitrary",)),
    )(du1, du1, proj, proj, proj, proj, cw, dq, dk, dv, cos, sin)


def _ada_fwd(c_all, w, b, name):
    B, D = c_all.shape
    N = w.shape[1]
    tn = 768 if N % 768 == 0 else N

    def body(c_ref, w_ref, b_ref, o_ref):
        c = c_ref[...]
        a = (c * _sigmoid(c)).astype(BF16)
        o_ref[...] = jnp.dot(a, w_ref[...].astype(BF16), preferred_element_type=F32) + b_ref[...]

    return pl.pallas_call(
        body, name=name, grid=(N // tn,),
        in_specs=[pl.BlockSpec((B, D), lambda j: (0, 0)), pl.BlockSpec((D, tn), lambda j: (0, j)),
                  pl.BlockSpec((1, tn), lambda j: (0, j))],
        out_specs=pl.BlockSpec((B, tn), lambda j: (0, j)),
        out_shape=jax.ShapeDtypeStruct((B, N), F32),
        compiler_params=_params(("parallel",)),
    )(c_all, w, b)


def _ada_wgrad(c_t, dmod, name):
    D, B = c_t.shape
    N = dmod.shape[1]
    tn = 768 if N % 768 == 0 else N

    def body(c_ref, d_ref, o_ref):
        c = c_ref[...]
        a = (c * _sigmoid(c)).astype(BF16)
        o_ref[...] = jnp.dot(a, d_ref[...].astype(BF16), preferred_element_type=F32)

    return pl.pallas_call(
        body, name=name, grid=(N // tn,),
        in_specs=[pl.BlockSpec((D, B), lambda j: (0, 0)), pl.BlockSpec((B, tn), lambda j: (0, j))],
        out_specs=pl.BlockSpec((D, tn), lambda j: (0, j)),
        out_shape=jax.ShapeDtypeStruct((D, N), F32),
        compiler_params=_params(("parallel",)),
    )(c_t, dmod)


def _sum_rows(x, name):
    R, N = x.shape

    def body(x_ref, o_ref):
        acc = x_ref[0:1, :]
        for r in range(1, R):
            acc = acc + x_ref[r:r + 1, :]
        o_ref[...] = acc

    return pl.pallas_call(
        body, name=name, out_shape=jax.ShapeDtypeStruct((1, N), F32),
        compiler_params=_params(),
    )(x)


def _adamw(w, g, m, v, name, deps=()):
    R, C = w.shape
    tr = _row_tile(R, 256) if R % 256 == 0 else R
    bc1 = 1.0 - ADAM_B1 ** ADAM_STEP
    bc2 = 1.0 - ADAM_B2 ** ADAM_STEP
    deps = _deps(deps)

    def body(w_ref, g_ref, m_ref, v_ref, d_ref, mo_ref, vo_ref):
        g_ = g_ref[...]
        m_ = ADAM_B1 * m_ref[...] + (1.0 - ADAM_B1) * g_
        v_ = ADAM_B2 * v_ref[...] + (1.0 - ADAM_B2) * (g_ * g_)
        mo_ref[...] = m_
        vo_ref[...] = v_
        d_ref[...] = -ADAM_LR * ((m_ / bc1) / (jnp.sqrt(v_ / bc2) + ADAM_EPS) + ADAM_WD * w_ref[...])

    row = pl.BlockSpec((tr, C), lambda i: (i, 0))
    out = jax.ShapeDtypeStruct((R, C), F32)
    return pl.pallas_call(
        _blind_to(body, 4, len(deps)), name=name, grid=(R // tr,),
        in_specs=[row] * 4 + [ANY_SPEC] * len(deps), out_specs=[row] * 3, out_shape=[out] * 3,
        compiler_params=_params(("parallel",), streaming=True),
    )(w, g, m, v, *deps)


def _coords():
    return lax.axis_index("x"), lax.axis_index("y"), lax.axis_index("c")


def _all_gather8(x, name, deps=()):
    R, N = x.shape
    assert R == 8
    flips = [(fx, fy, fc) for fx in (0, 1) for fy in (0, 1) for fc in (0, 1)][1:]
    deps = _deps(deps)

    def body(x_ref, o_ref, send_sems, recv_sems):
        mx, my, mc = _coords()
        me = 4 * mx + 2 * my + mc

        def rows(dev):
            return o_ref.at[pl.ds(pl.multiple_of(dev * R, R), R), :]

        o_ref[pl.ds(pl.multiple_of(me * R, R), R), :] = x_ref[...]
        copies = []
        for t, (fx, fy, fc) in enumerate(flips):
            peer = (mx ^ fx, my ^ fy, mc ^ fc)
            copies.append(pltpu.make_async_remote_copy(
                src_ref=x_ref, dst_ref=rows(me), send_sem=send_sems.at[t], recv_sem=recv_sems.at[t],
                device_id=peer, device_id_type=MESH))
        for cp in copies:
            cp.start()
        for t, (fx, fy, fc) in enumerate(flips):
            peer_id = 4 * (mx ^ fx) + 2 * (my ^ fy) + (mc ^ fc)
            pltpu.make_async_remote_copy(
                src_ref=x_ref, dst_ref=rows(peer_id), send_sem=send_sems.at[t], recv_sem=recv_sems.at[t],
                device_id=(mx ^ fx, my ^ fy, mc ^ fc), device_id_type=MESH).wait_recv()
        for cp in copies:
            cp.wait_send()

    return pl.pallas_call(
        _blind_to(body, 1, len(deps)), name=name,
        in_specs=[pl.BlockSpec(memory_space=pltpu.VMEM)] + [ANY_SPEC] * len(deps),
        out_specs=pl.BlockSpec(memory_space=pltpu.VMEM),
        out_shape=jax.ShapeDtypeStruct((N_DEV * R, N), F32),
        scratch_shapes=[pltpu.SemaphoreType.DMA((7,)), pltpu.SemaphoreType.DMA((7,))],
        compiler_params=pltpu.CompilerParams(has_side_effects=True, vmem_limit_bytes=VMEM_LIMIT_BYTES),
    )(x, *deps)


def _half_rows(rows, half):
    return pl.ds(pl.multiple_of(half * (rows // 2), 8), rows // 2)


def _split_start(bufs, plan, n, name):
    nb = len(bufs)

    def body(*refs):
        send_sems, recv_sems, token = refs[nb], refs[nb + 1], refs[-1]
        for t, (src, dst, dev) in enumerate(plan(refs[:nb])):
            pltpu.make_async_remote_copy(src_ref=src, dst_ref=dst, send_sem=send_sems.at[t],
                                         recv_sem=recv_sems.at[t], device_id=dev, device_id_type=MESH).start()
        token[...] = jnp.zeros_like(token)

    out = pl.pallas_call(
        body, name=name,
        out_shape=(pltpu.SemaphoreType.DMA((n,)), pltpu.SemaphoreType.DMA((n,)),
                   *[pltpu.HBM(b.shape, b.dtype) for b in bufs], jax.ShapeDtypeStruct((8, 128), F32)),
        in_specs=[HBM_SPEC] * nb,
        out_specs=(SEM_SPEC, SEM_SPEC, *[HBM_SPEC] * nb, pl.BlockSpec(memory_space=pltpu.VMEM)),
        input_output_aliases={i: 2 + i for i in range(nb)},
        compiler_params=pltpu.CompilerParams(has_side_effects=DATAFLOW),
    )(*[pltpu.with_memory_space_constraint(b, pltpu.HBM) for b in bufs])
    return out[0], out[1], list(out[2:2 + nb]), out[-1]


def _split_wait(bufs, send_sems, recv_sems, plan, after, name):
    nb = len(bufs)
    after = _deps(after)

    def body(*refs):
        ss, rs = refs[nb], refs[nb + 1]
        for t, (src, dst, dev) in enumerate(plan(refs[:nb])):
            cp = pltpu.make_async_remote_copy(src_ref=src, dst_ref=dst, send_sem=ss.at[t], recv_sem=rs.at[t],
                                              device_id=dev, device_id_type=MESH)
            cp.wait_send()
            cp.wait_recv()

    out = pl.pallas_call(
        body, name=name,
        out_shape=tuple(pltpu.HBM(b.shape, b.dtype) for b in bufs),
        in_specs=[HBM_SPEC] * nb + [SEM_SPEC, SEM_SPEC] + [ANY_SPEC] * len(after),
        out_specs=tuple([HBM_SPEC] * nb),
        input_output_aliases={i: i for i in range(nb)},
        compiler_params=pltpu.CompilerParams(has_side_effects=DATAFLOW),
    )(*bufs, send_sems, recv_sems, *after)
    return list(out)


class _Exchange:
    def __init__(self, bufs, plan, n, name):
        self.plan, self.name = plan, name
        self.send_sems, self.recv_sems, self.bufs, self.token = _split_start(bufs, plan, n, name + "_start")

    def wait(self, after):
        return _split_wait(self.bufs, self.send_sems, self.recv_sems, self.plan, after, self.name + "_wait")


def _cast_place(w, chip_idx, name, deps=()):
    R, C = w.shape
    tr = _row_tile(R, 256) if R % 256 == 0 else R
    deps = _deps(deps)

    def body(k_ref, w_ref, o_ref):
        o_ref[...] = w_ref[...].astype(BF16)

    grid_spec = pltpu.PrefetchScalarGridSpec(
        num_scalar_prefetch=1, grid=(R // tr,),
        in_specs=[pl.BlockSpec((tr, C), lambda i, k: (i, 0))] + [ANY_SPEC] * len(deps),
        out_specs=pl.BlockSpec((None, tr, C), lambda i, k: (k[0], i, 0)),
    )
    return pl.pallas_call(
        _blind_to(body, 2, len(deps)), name=name, grid_spec=grid_spec,
        out_shape=jax.ShapeDtypeStruct((N_CHIPS, R, C), BF16),
        compiler_params=_params(("parallel",), streaming=True),
    )(chip_idx, w, *deps)


def _plan_gather(refs):
    mx, my, mc = _coords()
    me = 2 * mx + my
    plan = []
    for g in refs:
        mine = g.at[me, _half_rows(g.shape[1], mc), :]
        for fx, fy in XY_FLIPS:
            plan.append((mine, mine, (mx ^ fx, my ^ fy, mc)))
    return plan


def _plan_swap(refs):
    mx, my, mc = _coords()
    plan = []
    for g in refs:
        for fx, fy in XY_FLIPS:
            have = g.at[2 * (mx ^ fx) + (my ^ fy), _half_rows(g.shape[1], mc), :]
            plan.append((have, have, (mx, my, 1 - mc)))
    return plan


def _plan_other_halves(refs):
    n = len(refs) // 2
    mx, my, mc = _coords()
    return [(g.at[pl.ds(0, N_CHIPS), _half_rows(g.shape[1], 1 - mc), :], land, (mx, my, 1 - mc))
            for g, land in zip(refs[:n], refs[n:])]


def _plan_chunks(refs):
    n = len(refs) // 2
    mx, my, mc = _coords()
    plan = []
    for s, land in zip(refs[:n], refs[n:]):
        for t, (fx, fy) in enumerate(XY_FLIPS):
            plan.append((s.at[2 * (mx ^ fx) + (my ^ fy)], land.at[t], (mx ^ fx, my ^ fy, mc)))
    return plan


def _plan_share(refs):
    mx, my, mc = _coords()
    return [(full.at[mc], full.at[mc], (mx, my, 1 - mc)) for full in refs]


def _add_half(g, recv, core_idx, name):
    nk, R, C = g.shape
    rh = R // 2
    tr = _row_tile(rh, 128) if rh % 128 == 0 else rh
    nt = rh // tr

    def body(c_ref, g_ref, r_ref, o_ref):
        o_ref[...] = (g_ref[...] + r_ref[...]).astype(BF16)

    grid_spec = pltpu.PrefetchScalarGridSpec(
        num_scalar_prefetch=1, grid=(nk, nt),
        in_specs=[pl.BlockSpec((None, tr, C), lambda k, i, c: (k, c[0] * nt + i, 0)),
                  pl.BlockSpec((None, tr, C), lambda k, i, c: (k, i, 0))],
        out_specs=pl.BlockSpec((None, tr, C), lambda k, i, c: (k, i, 0)),
    )
    return pl.pallas_call(
        body, name=name, grid_spec=grid_spec, out_shape=jax.ShapeDtypeStruct((nk, rh, C), BF16),
        compiler_params=_params(("parallel", "parallel"), streaming=True),
    )(core_idx, g, recv)


def _sum_chips(s, land, chip_core, name):
    _, rh, C = s.shape
    tr = _row_tile(rh, 128) if rh % 128 == 0 else rh

    def body(p_ref, s_ref, l_ref, o_ref):
        me = p_ref[0]
        acc = None
        for j in range(N_CHIPS):
            t = jnp.maximum(jnp.bitwise_xor(me, j) - 1, 0)
            term = jnp.where(me == j, s_ref[...], l_ref[t]).astype(F32)
            acc = term if acc is None else acc + term
        o_ref[...] = acc

    grid_spec = pltpu.PrefetchScalarGridSpec(
        num_scalar_prefetch=1, grid=(rh // tr,),
        in_specs=[pl.BlockSpec((None, tr, C), lambda i, p: (p[0], i, 0)),
                  pl.BlockSpec((3, tr, C), lambda i, p: (0, i, 0))],
        out_specs=pl.BlockSpec((None, tr, C), lambda i, p: (p[1], i, 0)),
    )
    return pl.pallas_call(
        body, name=name, grid_spec=grid_spec, out_shape=jax.ShapeDtypeStruct((2, rh, C), F32),
        compiler_params=_params(("parallel",), streaming=True),
    )(chip_core, s, land)


def _rs_send_halves(grads, tag):
    lands = [lax.empty((g.shape[0], g.shape[1] // 2, g.shape[2]), g.dtype) for g in grads]
    return _Exchange(list(grads) + lands, _plan_other_halves, len(grads), f"rs_halves_{tag}")


def _rs_send_chunks(ex, after, core_idx, tag):
    bufs = ex.wait(after)
    n = len(bufs) // 2
    sums = [_add_half(g, r, core_idx, f"rs_add_{tag}_{i}") for i, (g, r) in enumerate(zip(bufs[:n], bufs[n:]))]
    lands = [lax.empty((3,) + s.shape[1:], s.dtype) for s in sums]
    return _Exchange(sums + lands, _plan_chunks, 3 * n, f"rs_chunks_{tag}")


def _rs_send_share(ex, after, chip_core, tag):
    bufs = ex.wait(after)
    n = len(bufs) // 2
    fulls = [_sum_chips(s, l, chip_core, f"rs_sum_{tag}_{i}") for i, (s, l) in enumerate(zip(bufs[:n], bufs[n:]))]
    return _Exchange(fulls, _plan_share, n, f"rs_share_{tag}")


def _rs_finish(ex, after):
    return [b.reshape(2 * b.shape[1], b.shape[2]) for b in ex.wait(after)]


def _ffn_forward(h, n, gate, wg, wu, wd, tag, next_norm=None):
    ga, up, act = _ffn_gate_up(n, wg, wu, f"{tag}_gate_up")
    h_out, f, *n_next = _mm_residual(act, wd, h, gate, 0.5, f"{tag}_down", next_norm=next_norm)
    return (h_out, *n_next), (h, n, ga, up, act, f)


def _ffn_backward(dh_out, df, saved, gain, sc, wg, wu, wd, core_idx, tag, prev=None, last=False):
    h, n, ga, up, act, _ = saved
    (dwd,) = _wgrad_chunk_lhs([act], df, f"{tag}_dwd")
    ex_d = _rs_send_halves([dwd], f"{tag}_d")
    dga, dup = _ffn_dact(df, wd, ga, up, f"{tag}_dact", deps=[ex_d.token])
    ex_d = _rs_send_chunks(ex_d, [dga], core_idx, f"{tag}_d")
    dwg, dwu = _wgrad_chunk_lhs([dga, dup], n, f"{tag}_dwgu", deps=[ex_d.token])
    ex_gu = _rs_send_halves([dwg, dwu], f"{tag}_gu")
    dn = _mm_reduce([dga, dup], [wg, wu], True, True, f"{tag}_dn", deps=[ex_gu.token])
    if not last:
        ex_gu = _rs_send_chunks(ex_gu, [dn], core_idx, f"{tag}_gu")
    outs = _norm_mod_bwd(dn, h, gain, sc, dh_out, f"{tag}_norm_bwd", prev=prev, deps=[ex_gu.token])
    return outs, (ex_d, ex_gu)


def _pad_cols(v, n):
    return jnp.pad(v, ((0, 0), (0, n - v.shape[1])))


def _mixer_forward(h1, n2, gt2, win, wout, conv_w, conv_dw_b, conv_ln_g, conv_ln_b, attn_out_g, conv_out_g,
                   next_norm=None):
    S, D = h1.shape
    proj = _mm_cols(n2, win, "mix_in")
    cos, sin = _rope_tables(S)
    q, k, v = _qkv_rope(proj, cos, sin, "qkv_rope")
    attn, lse = _attn_fwd(q, k, v, "attn_fwd")
    u1, y = _mixer_merge(proj, attn, conv_w, conv_dw_b, conv_ln_g, conv_ln_b, attn_out_g, conv_out_g, "mix_merge")
    h2, mo, *n_next = _mm_residual(y[None], wout.reshape(1, D, D), h1, gt2, 1.0, "mix_out", next_norm=next_norm)
    return (h2, *n_next), (h1, n2, proj, cos, sin, q, k, v, attn, lse, u1, y, mo)


def _mixer_backward(dh2, dmo, saved, mix_norm_g, sc2, win, wout, conv_w, conv_ln_g, conv_ln_b, attn_out_g,
                    conv_out_g, core_idx, prev=None):
    h1, n2, proj, cos, sin, q, k, v, attn, lse, u1, y, _ = saved
    S, D = h1.shape
    (dwout,) = _wgrad_chunk_lhs([y[None]], dmo, "mix_dwout")
    dy = _mm_nt(dmo, wout.reshape(D, D), "mix_dy")
    dattn, delta, d_attn_g = _attn_out_bwd(dy, attn, attn_out_g, "attn_out_bwd")
    dq, dk, dv = _attn_bwd(q, k, v, dattn, lse, delta, "attn_bwd")
    du1, d_gco, d_lng, d_lnb, d_cb = _conv_bwd_norms(dy, u1, conv_ln_g, conv_ln_b, conv_out_g, "conv_bwd_norms")
    dproj, d_cw = _dproj(du1, proj, conv_w, dq, dk, dv, cos, sin, "mix_dproj")
    (dwin,) = _wgrad_chunk_rhs(n2, [dproj], N_CHIPS, False, "mix_dwin")
    ex = _rs_send_halves([dwin, dwout.reshape(N_CHIPS, D // N_CHIPS, D)], "mix")
    dn2 = _mm_reduce([dproj], [win], False, False, "mix_dn", deps=[ex.token])
    ex = _rs_send_chunks(ex, [dn2], core_idx, "mix")
    outs = _norm_mod_bwd(dn2, h1, mix_norm_g, sc2, dh2, "mix_norm_bwd", prev=prev, deps=[ex.token])
    return outs, ex, (d_cb, d_lng, d_lnb, d_attn_g, d_gco, d_cw)


def kernel(x, c, w_ada, b_ada, ffn1_norm_g, ffn1_w_gate, ffn1_w_up, ffn1_w_down, mix_norm_g, w_in, conv_dw_w, conv_dw_b, conv_ln_g, conv_ln_b, attn_out_g, conv_out_g, w_out, ffn2_norm_g, ffn2_w_gate, ffn2_w_up, ffn2_w_down, final_norm_g, loss_target, m_w_ada, m_b_ada, m_ffn1_norm_g, m_ffn1_w_gate, m_ffn1_w_up, m_ffn1_w_down, m_mix_norm_g, m_w_in, m_conv_dw_w, m_conv_dw_b, m_conv_ln_g, m_conv_ln_b, m_attn_out_g, m_conv_out_g, m_w_out, m_ffn2_norm_g, m_ffn2_w_gate, m_ffn2_w_up, m_ffn2_w_down, m_final_norm_g, v_w_ada, v_b_ada, v_ffn1_norm_g, v_ffn1_w_gate, v_ffn1_w_up, v_ffn1_w_down, v_mix_norm_g, v_w_in, v_conv_dw_w, v_conv_dw_b, v_conv_ln_g, v_conv_ln_b, v_attn_out_g, v_conv_out_g, v_w_out, v_ffn2_norm_g, v_ffn2_w_gate, v_ffn2_w_up, v_ffn2_w_down, v_final_norm_g):
    S, D = x.shape[1], x.shape[2]
    mx, my, mc = _coords()
    chip = 2 * mx + my
    dev = 4 * mx + 2 * my + mc
    chip_idx = chip.astype(jnp.int32).reshape(1)
    core_idx = mc.astype(jnp.int32).reshape(1)
    chip_core = jnp.stack([chip, mc]).astype(jnp.int32)
    h0 = x[0]
    target = loss_target[0]

    ncw = CONV_KERNEL * 128
    n0 = -(-(D + ncw) // 1024) * 1024
    pk0 = _pad_cols(jnp.concatenate([c.reshape(1, D), conv_dw_w.reshape(1, ncw)], axis=1), n0)
    g0 = _all_gather8(pk0.reshape(8, n0 // 8), "gather_c").reshape(N_DEV, n0)
    c_all = g0[:, :D]
    conv_w = jnp.concatenate([g0[2 * kc, D:D + ncw].reshape(CONV_KERNEL, 128) for kc in range(N_CHIPS)], axis=1)
    conv_w = jnp.pad(conv_w, ((0, HALO - CONV_KERNEL), (0, 0)))
    nmod = w_ada.shape[2]
    b_shard = lax.dynamic_slice(b_ada, (0, chip * nmod), (1, nmod))
    mod_part = _ada_fwd(c_all, w_ada[0], b_shard, "ada_fwd")
    g1 = _all_gather8(mod_part, "gather_mod")
    mod_all = jnp.concatenate([g1[16 * kc:16 * kc + 8] for kc in range(N_CHIPS)], axis=1)
    mod = lax.dynamic_slice(mod_all, (dev, 0), (1, 9 * D))
    sh1, sc1, gt1, sh2, sc2, gt2, sh3, sc3, gt3 = [mod[:, i * D:(i + 1) * D] for i in range(9)]

    def gather_start(ws, tag, dep):
        slots = [_cast_place(w, chip_idx, f"cast_{tag}_{i}", deps=[dep]) for i, w in enumerate(ws)]
        return _Exchange(slots, _plan_gather, 3 * len(ws), f"gather_{tag}")

    def swap_start(ex, after, tag):
        return _Exchange(ex.wait(after), _plan_swap, 3 * len(ex.bufs), f"swap_{tag}")

    ex_gu1 = gather_start([ffn1_w_gate[0].T, ffn1_w_up[0].T], "ffn1_gu", g1)
    n1 = _norm_mod(h0, ffn1_norm_g, sc1, sh1, "ffn1_norm", deps=[ex_gu1.token])
    ex_d1 = gather_start([ffn1_w_down[0]], "ffn1_d", n1)
    ex_wm = gather_start([w_in[0], w_out[0]], "mix", ex_d1.token)
    ex_w2 = gather_start([ffn2_w_gate[0].T, ffn2_w_up[0].T, ffn2_w_down[0]], "ffn2", ex_wm.token)

    wg1, wu1 = swap_start(ex_gu1, [ex_w2.token], "ffn1_gu").wait([])
    ga1, up1, act1 = _ffn_gate_up(n1, wg1, wu1, "ffn1_gate_up")
    (wd1,) = swap_start(ex_d1, [act1], "ffn1_d").wait([])
    ex_wm = swap_start(ex_wm, [wd1], "mix")
    h1, f1, n2 = _mm_residual(act1, wd1, h0, gt1, 0.5, "ffn1_down", next_norm=(mix_norm_g, sc2, sh2))
    saved1 = (h0, n1, ga1, up1, act1, f1)
    win, wout = ex_wm.wait([h1])
    ex_w2 = swap_start(ex_w2, [h1], "ffn2")
    (h2, n3), saved2 = _mixer_forward(h1, n2, gt2, win, wout, conv_w, conv_dw_b, conv_ln_g, conv_ln_b,
                                      attn_out_g, conv_out_g, next_norm=(ffn2_norm_g, sc3, sh3))
    wg2, wu2, wd2 = ex_w2.wait([h2])
    (h3,), saved3 = _ffn_forward(h2, n3, gt3, wg2, wu2, wd2, "ffn2")
    loss_part, dh3, d_final_g, df3, d_gt3 = _loss_head(h3, final_norm_g.reshape(1, D), target, saved3[5], gt3, 0.5,
                                                       "loss_head")

    (dh2, d_sh3, d_sc3, d_gain3, dmo, d_gt2), (ex_d2, ex_gu2) = _ffn_backward(
        dh3, df3, saved3, ffn2_norm_g, sc3, wg2, wu2, wd2, core_idx, "ffn2", prev=(saved2[12], gt2, 1.0))
    (dh1, d_sh2, d_sc2, d_gain2, df1, d_gt1), ex_mix, small_mix = _mixer_backward(
        dh2, dmo, saved2, mix_norm_g, sc2, win, wout, conv_w, conv_ln_g, conv_ln_b, attn_out_g, conv_out_g, core_idx,
        prev=(f1, gt1, 0.5))
    d_cb, d_lng, d_lnb, d_attn_g, d_gco, d_cw = small_mix
    (dh0, d_sh1, d_sc1, d_gain1), (ex_d1, ex_gu1) = _ffn_backward(
        dh1, df1, saved1, ffn1_norm_g, sc1, wg1, wu1, wd1, core_idx, "ffn1", last=True)

    dmod = jnp.concatenate([d_sh1, d_sc1, d_gt1, d_sh2, d_sc2, d_gt2, d_sh3, d_sc3, d_gt3], axis=1)
    small = [d_gain1, d_gain2, d_gain3, d_final_g, d_cb, d_lng, d_lnb, d_attn_g, d_gco,
             d_cw[:CONV_KERNEL].reshape(1, CONV_KERNEL * CONV_WIDTH), loss_part]
    pk1 = jnp.concatenate([dmod] + small, axis=1)
    n1_ = -(-pk1.shape[1] // 1024) * 1024
    gathered = _all_gather8(_pad_cols(pk1, n1_).reshape(8, n1_ // 8), "gather_small").reshape(N_DEV, n1_)
    ex_gu1 = _rs_send_chunks(ex_gu1, [gathered], core_idx, "ffn1_gu")
    tot = _sum_rows(gathered, "sum_small")
    off = [0]

    def take(nel):
        out = tot[:, off[0]:off[0] + nel]
        off[0] += nel
        return out

    g_b_ada = take(9 * D)
    g_ffn1_norm, g_mix_norm, g_ffn2_norm, g_final = take(D), take(D), take(D), take(D)
    g_cb, g_lng, g_lnb, g_attn_g, g_gco = take(512), take(512), take(512), take(512), take(512)
    g_cw_full = take(CONV_KERNEL * CONV_WIDTH).reshape(CONV_KERNEL, CONV_WIDTH)
    loss = take(128)[0, 0]
    g_cw = lax.dynamic_slice(g_cw_full, (0, chip * 128), (CONV_KERNEL, 128))

    dmod_shard = lax.dynamic_slice(gathered[:, :9 * D], (0, chip * nmod), (N_DEV, nmod))
    dmod16 = jnp.pad(dmod_shard, ((0, N_DEV), (0, 0)))
    c_t16 = jnp.pad(c_all.T, ((0, 0), (0, N_DEV)))
    g_w_ada = _ada_wgrad(c_t16, dmod16, "ada_wgrad")

    names = ["w_ada", "b_ada", "ffn1_norm_g", "ffn1_w_gate", "ffn1_w_up", "ffn1_w_down", "mix_norm_g", "w_in",
             "conv_dw_w", "conv_dw_b", "conv_ln_g", "conv_ln_b", "attn_out_g", "conv_out_g", "w_out", "ffn2_norm_g",
             "ffn2_w_gate", "ffn2_w_up", "ffn2_w_down", "final_norm_g"]
    weights = dict(zip(names, [w_ada, b_ada, ffn1_norm_g, ffn1_w_gate, ffn1_w_up, ffn1_w_down, mix_norm_g, w_in,
                               conv_dw_w, conv_dw_b, conv_ln_g, conv_ln_b, attn_out_g, conv_out_g, w_out,
                               ffn2_norm_g, ffn2_w_gate, ffn2_w_up, ffn2_w_down, final_norm_g]))
    ms = dict(zip(names, [m_w_ada, m_b_ada, m_ffn1_norm_g, m_ffn1_w_gate, m_ffn1_w_up, m_ffn1_w_down, m_mix_norm_g,
                          m_w_in, m_conv_dw_w, m_conv_dw_b, m_conv_ln_g, m_conv_ln_b, m_attn_out_g, m_conv_out_g,
                          m_w_out, m_ffn2_norm_g, m_ffn2_w_gate, m_ffn2_w_up, m_ffn2_w_down, m_final_norm_g]))
    vs = dict(zip(names, [v_w_ada, v_b_ada, v_ffn1_norm_g, v_ffn1_w_gate, v_ffn1_w_up, v_ffn1_w_down, v_mix_norm_g,
                          v_w_in, v_conv_dw_w, v_conv_dw_b, v_conv_ln_g, v_conv_ln_b, v_attn_out_g, v_conv_out_g,
                          v_w_out, v_ffn2_norm_g, v_ffn2_w_gate, v_ffn2_w_up, v_ffn2_w_down, v_final_norm_g]))
    grads, deltas, new_ms, new_vs = {}, {}, {}, {}

    def adamw_big(nm, g2d, deps=(), transposed=False):
        shape = weights[nm].shape
        two_d = (shape[-2], shape[-1])

        def view(t):
            return t.reshape(two_d).T if transposed else t.reshape(two_d)

        d_, m_, v_ = _adamw(view(weights[nm]), g2d, view(ms[nm]), view(vs[nm]), f"adamw_{nm}", deps=deps)
        grads[nm], deltas[nm], new_ms[nm], new_vs[nm] = (
            (t.T if transposed else t).reshape(shape) for t in (g2d, d_, m_, v_))
        return d_

    d_ada = adamw_big("w_ada", g_w_ada, deps=[ex_gu1.token])
    small_grads = {"b_ada": g_b_ada, "ffn1_norm_g": g_ffn1_norm, "mix_norm_g": g_mix_norm, "conv_dw_w": g_cw,
                   "conv_dw_b": g_cb, "conv_ln_g": g_lng, "conv_ln_b": g_lnb, "attn_out_g": g_attn_g,
                   "conv_out_g": g_gco, "ffn2_norm_g": g_ffn2_norm, "final_norm_g": g_final}
    small_names = [nm for nm in names if nm in small_grads]

    def pack_small(arrs):
        flat = jnp.concatenate([arrs[nm].reshape(1, -1) for nm in small_names], axis=1)
        npad = -(-flat.shape[1] // 1024) * 1024
        return _pad_cols(flat, npad).reshape(8, npad // 8)

    d_s, m_s, v_s = _adamw(pack_small(weights), pack_small(small_grads), pack_small(ms), pack_small(vs),
                           "adamw_small")
    pos = 0
    for nm in small_names:
        shape, nel = weights[nm].shape, weights[nm].size
        grads[nm] = small_grads[nm].reshape(shape)
        deltas[nm], new_ms[nm], new_vs[nm] = (t.reshape(1, -1)[:, pos:pos + nel].reshape(shape)
                                              for t in (d_s, m_s, v_s))
        pos += nel

    ex_d2 = _rs_send_share(ex_d2, [d_ada, d_s], chip_core, "ffn2_d")
    ex_gu2 = _rs_send_share(ex_gu2, [ex_d2.token], chip_core, "ffn2_gu")
    ex_mix = _rs_send_share(ex_mix, [ex_gu2.token], chip_core, "mix")
    ex_d1 = _rs_send_share(ex_d1, [ex_mix.token], chip_core, "ffn1_d")
    (g_wd2,) = _rs_finish(ex_d2, [ex_d1.token])
    last = [adamw_big("ffn2_w_down", g_wd2)]
    g_wg2, g_wu2 = _rs_finish(ex_gu2, last)
    last = [adamw_big("ffn2_w_gate", g_wg2, transposed=True), adamw_big("ffn2_w_up", g_wu2, transposed=True)]
    g_win, g_wout = _rs_finish(ex_mix, last)
    last = [adamw_big("w_in", g_win), adamw_big("w_out", g_wout)]
    (g_wd1,) = _rs_finish(ex_d1, last)
    last = [adamw_big("ffn1_w_down", g_wd1)]
    ex_gu1 = _rs_send_share(ex_gu1, last, chip_core, "ffn1_gu")
    g_wg1, g_wu1 = _rs_finish(ex_gu1, [])
    adamw_big("ffn1_w_gate", g_wg1, transposed=True)
    adamw_big("ffn1_w_up", g_wu1, transposed=True)

    return (loss, dh0[None], *[grads[nm] for nm in names], *[deltas[nm] for nm in names],
            *[new_ms[nm] for nm in names], *[new_vs[nm] for nm in names])
```

```python
import jax
import jax.numpy as jnp
import numpy as np
from jax import lax
from jax.experimental import pallas as pl
from jax.experimental.pallas import tpu as pltpu

F32 = jnp.float32
BF16 = jnp.bfloat16
MESH = pl.DeviceIdType.MESH

RMS_EPS = 1e-6
LN_EPS = 1e-5
HEAD_DIM = 64
ATTN_WIDTH = 512
CONV_WIDTH = 512
ATTN_BLOCK = 128
DILATIONS = (1, 4, 16)
SUPER_ROWS = ATTN_BLOCK * 16
ROPE_THETA = 10000.0
CONV_KERNEL = 31
HALO = 32
N_CHIPS = 4
N_DEV = 8
ADAM_LR, ADAM_B1, ADAM_B2, ADAM_EPS, ADAM_WD, ADAM_STEP = 0.001, 0.9, 0.999, 1e-08, 0.01, 10
VMEM_LIMIT_BYTES = 48 * 1024 * 1024
VMEM_LIMIT_STREAMING = 62 * 1024 * 1024
NEG = -1e30

NT = (((1,), (1,)), ((), ()))
TN = (((0,), (0,)), ((), ()))

ANY_SPEC = pl.BlockSpec(memory_space=pl.ANY)
HBM_SPEC = pl.BlockSpec(memory_space=pltpu.HBM)
SEM_SPEC = pl.BlockSpec(memory_space=pltpu.SEMAPHORE)
DATAFLOW = pltpu.SideEffectType.DATAFLOW_SIDE_EFFECTING
XY_FLIPS = ((0, 1), (1, 0), (1, 1))


def _params(sem=None, streaming=False):
    return pltpu.CompilerParams(dimension_semantics=sem,
                                vmem_limit_bytes=VMEM_LIMIT_STREAMING if streaming else VMEM_LIMIT_BYTES)


def _row_tile(rows, want):
    t = min(rows, want)
    assert rows % t == 0
    return t


def _sigmoid(x):
    return 1.0 / (1.0 + jnp.exp(-x))


def _deps(deps):
    return [d for d in deps if d is not None]


def _blind_to(body, n_in, n_dep):
    def wrapped(*refs):
        return body(*refs[:n_in], *refs[n_in + n_dep:])
    return wrapped


def _vec_spec(d, ngrid):
    if ngrid == 1:
        return pl.BlockSpec((1, d), lambda i: (0, 0))
    return pl.BlockSpec((1, d), lambda i, j: (0, 0))


def _norm_mod(h, gain, sc, sh, name, deps=()):
    S, D = h.shape
    tr = _row_tile(S, 512)
    deps = _deps(deps)

    def body(h_ref, g_ref, sc_ref, sh_ref, n_ref):
        x = h_ref[...]
        r = lax.rsqrt(jnp.mean(x * x, axis=-1, keepdims=True) + RMS_EPS)
        y = (x * r) * g_ref[...]
        n_ref[...] = (y * (1.0 + sc_ref[...]) + sh_ref[...]).astype(BF16)

    row = pl.BlockSpec((tr, D), lambda i: (i, 0))
    return pl.pallas_call(
        _blind_to(body, 4, len(deps)), name=name, grid=(S // tr,),
        in_specs=[row, _vec_spec(D, 1), _vec_spec(D, 1), _vec_spec(D, 1)] + [ANY_SPEC] * len(deps),
        out_specs=row, out_shape=jax.ShapeDtypeStruct((S, D), BF16),
        compiler_params=_params(("parallel",)),
    )(h, gain, sc, sh, *deps)


def _norm_mod_bwd(dn, h_in, gain, sc, dh_out, name, prev=None, deps=()):
    S, D = h_in.shape
    tr = _row_tile(S, 512)
    deps = _deps(deps)
    n_in = 5 if prev is None else 7

    def body(*refs):
        dn_ref, h_ref, g_ref, sc_ref, dho_ref = refs[:5]
        dh_ref, dsh_ref, dsc_ref, dg_ref = refs[n_in:n_in + 4]

        @pl.when(pl.program_id(0) == 0)
        def _():
            for ref in refs[n_in + 1:n_in + 4] + refs[n_in + 5:]:
                ref[...] = jnp.zeros_like(ref)

        x = h_ref[...]
        dn_ = dn_ref[...]
        g = g_ref[...]
        one_sc = 1.0 + sc_ref[...]
        r = lax.rsqrt(jnp.mean(x * x, axis=-1, keepdims=True) + RMS_EPS)
        xh = x * r
        dsh_ref[...] += jnp.sum(dn_, axis=0, keepdims=True)
        dsc_ref[...] += jnp.sum(dn_ * (xh * g), axis=0, keepdims=True)
        dg_ref[...] += jnp.sum(dn_ * one_sc * xh, axis=0, keepdims=True)
        dxh = dn_ * (g * one_sc)
        dh = dho_ref[...] + r * (dxh - xh * jnp.mean(dxh * xh, axis=-1, keepdims=True))
        dh_ref[...] = dh
        if prev is not None:
            _gate_back(dh, refs[5], refs[6], prev[2], refs[n_in + 4], refs[n_in + 5])

    row = pl.BlockSpec((tr, D), lambda i: (i, 0))
    vec = _vec_spec(D, 1)
    extra_in = [] if prev is None else [row, vec]
    extra_out = [] if prev is None else [row, vec]
    extra_shape = [] if prev is None else [jax.ShapeDtypeStruct((S, D), BF16), jax.ShapeDtypeStruct((1, D), F32)]
    return pl.pallas_call(
        _blind_to(body, n_in, len(deps)), name=name, grid=(S // tr,),
        in_specs=[row, row, vec, vec, row] + extra_in + [ANY_SPEC] * len(deps),
        out_specs=[row, vec, vec, vec] + extra_out,
        out_shape=[jax.ShapeDtypeStruct((S, D), F32)] + [jax.ShapeDtypeStruct((1, D), F32)] * 3 + extra_shape,
        compiler_params=_params(("arbitrary",)),
    )(dn, h_in, gain, sc, dh_out, *([] if prev is None else prev[:2]), *deps)


def _gate_back(dh, f_ref, gate_ref, coef, df_ref, dgate_ref):
    df_ref[...] = ((coef * gate_ref[...]) * dh).astype(BF16)
    dgate_ref[...] += jnp.sum(coef * dh * f_ref[...].astype(F32), axis=0, keepdims=True)


def _loss_head(h, gain, target, f, gate, coef, name):
    S, D = h.shape
    tr = _row_tile(S, 512)

    def body(h_ref, g_ref, t_ref, f_ref, gate_ref, loss_ref, dh_ref, dg_ref, df_ref, dgate_ref):
        @pl.when(pl.program_id(0) == 0)
        def _():
            loss_ref[...] = jnp.zeros_like(loss_ref)
            dg_ref[...] = jnp.zeros_like(dg_ref)
            dgate_ref[...] = jnp.zeros_like(dgate_ref)

        x = h_ref[...]
        g = g_ref[...]
        r = lax.rsqrt(jnp.mean(x * x, axis=-1, keepdims=True) + RMS_EPS)
        xh = x * r
        err = xh * g - t_ref[...]
        part = 0.5 * jnp.sum(jnp.mean(err * err, axis=-1, keepdims=True), axis=0, keepdims=True)
        loss_ref[...] += jnp.broadcast_to(part, loss_ref.shape)
        dy = err * (1.0 / D)
        dg_ref[...] += jnp.sum(dy * xh, axis=0, keepdims=True)
        dxh = dy * g
        dh = r * (dxh - xh * jnp.mean(dxh * xh, axis=-1, keepdims=True))
        dh_ref[...] = dh
        _gate_back(dh, f_ref, gate_ref, coef, df_ref, dgate_ref)

    row = pl.BlockSpec((tr, D), lambda i: (i, 0))
    vec = _vec_spec(D, 1)
    return pl.pallas_call(
        body, name=name, grid=(S // tr,),
        in_specs=[row, vec, row, row, vec],
        out_specs=[pl.BlockSpec((1, 128), lambda i: (0, 0)), row, vec, row, vec],
        out_shape=[jax.ShapeDtypeStruct((1, 128), F32), jax.ShapeDtypeStruct((S, D), F32),
                   jax.ShapeDtypeStruct((1, D), F32), jax.ShapeDtypeStruct((S, D), BF16),
                   jax.ShapeDtypeStruct((1, D), F32)],
        compiler_params=_params(("arbitrary",)),
    )(h, gain, target, f, gate)


def _ffn_gate_up(n, wg_t, wu_t, name):
    S, D = n.shape
    nk, w, _ = wg_t.shape
    tm = _row_tile(S, 512)

    def body(n_ref, wg_ref, wu_ref, ga_ref, up_ref, act_ref):
        x = n_ref[...]
        ga = lax.dot_general(x, wg_ref[...], NT, preferred_element_type=F32)
        up = lax.dot_general(x, wu_ref[...], NT, preferred_element_type=F32)
        ga_ref[...] = ga.astype(BF16)
        up_ref[...] = up.astype(BF16)
        act_ref[...] = ((ga * _sigmoid(ga)) * up).astype(BF16)

    wspec = pl.BlockSpec((None, w, D), lambda k, m: (k, 0, 0))
    ospec = pl.BlockSpec((None, tm, w), lambda k, m: (k, m, 0))
    out = jax.ShapeDtypeStruct((nk, S, w), BF16)
    return pl.pallas_call(
        body, name=name, grid=(nk, S // tm),
        in_specs=[pl.BlockSpec((tm, D), lambda k, m: (m, 0)), wspec, wspec],
        out_specs=[ospec, ospec, ospec], out_shape=[out, out, out],
        compiler_params=_params(("parallel", "parallel")),
    )(n, wg_t, wu_t)


def _mm_residual(lhs, w, h_in, gvec, coef, name, next_norm=None):
    nk, S, kc = lhs.shape
    D = w.shape[2]
    tm = _row_tile(S, 512)
    n_in = 4 if next_norm is None else 7

    def body(*refs):
        l_ref, w_ref, h_ref, g_ref = refs[:4]
        ho_ref, f_ref = refs[n_in:n_in + 2]
        acc_ref = refs[-1]
        k = pl.program_id(1)

        @pl.when(k == 0)
        def _():
            acc_ref[...] = jnp.zeros_like(acc_ref)

        acc_ref[...] += jnp.dot(l_ref[...], w_ref[...], preferred_element_type=F32)

        @pl.when(k == nk - 1)
        def _():
            f = acc_ref[...]
            f_ref[...] = f.astype(BF16)
            x = h_ref[...] + (coef * g_ref[...]) * f
            ho_ref[...] = x
            if next_norm is not None:
                ng_ref, sc_ref, sh_ref = refs[4:7]
                r = lax.rsqrt(jnp.mean(x * x, axis=-1, keepdims=True) + RMS_EPS)
                y = (x * r) * ng_ref[...]
                refs[n_in + 2][...] = (y * (1.0 + sc_ref[...]) + sh_ref[...]).astype(BF16)

    row = pl.BlockSpec((tm, D), lambda m, k: (m, 0))
    vec = _vec_spec(D, 2)
    with_n = next_norm is not None
    return pl.pallas_call(
        body, name=name, grid=(S // tm, nk),
        in_specs=[pl.BlockSpec((None, tm, kc), lambda m, k: (k, m, 0)),
                  pl.BlockSpec((None, kc, D), lambda m, k: (k, 0, 0)), row, vec] + [vec] * (3 * with_n),
        out_specs=[row, row] + [row] * with_n,
        out_shape=[jax.ShapeDtypeStruct((S, D), F32), jax.ShapeDtypeStruct((S, D), BF16)]
        + [jax.ShapeDtypeStruct((S, D), BF16)] * with_n,
        scratch_shapes=[pltpu.VMEM((tm, D), F32)],
        compiler_params=_params(("parallel", "arbitrary")),
    )(lhs, w, h_in, gvec, *(next_norm or ()))


def _mm_cols(n, w, name):
    S, D = n.shape
    nk, _, wd = w.shape
    assert wd % 128 == 0
    tm = _row_tile(S, 512)

    def body(n_ref, w_ref, o_ref):
        o_ref[...] = jnp.dot(n_ref[...], w_ref[...], preferred_element_type=F32)

    return pl.pallas_call(
        body, name=name, grid=(nk, S // tm),
        in_specs=[pl.BlockSpec((tm, D), lambda k, m: (m, 0)), pl.BlockSpec((None, D, wd), lambda k, m: (k, 0, 0))],
        out_specs=pl.BlockSpec((tm, wd), lambda k, m: (m, k)),
        out_shape=jax.ShapeDtypeStruct((S, nk * wd), F32),
        compiler_params=_params(("parallel", "parallel")),
    )(n, w)


def _ffn_dact(df, wd, ga, up, name, deps=()):
    S, D = df.shape
    nk, w, _ = wd.shape
    tm = _row_tile(S, 512)
    deps = _deps(deps)

    def body(df_ref, wd_ref, ga_ref, up_ref, dga_ref, dup_ref):
        dact = lax.dot_general(df_ref[...], wd_ref[...], NT, preferred_element_type=F32)
        ga_ = ga_ref[...].astype(F32)
        up_ = up_ref[...].astype(F32)
        sig = _sigmoid(ga_)
        dga_ref[...] = (dact * up_ * (sig * (1.0 + ga_ * (1.0 - sig)))).astype(BF16)
        dup_ref[...] = (dact * (ga_ * sig)).astype(BF16)

    cspec = pl.BlockSpec((None, tm, w), lambda k, m: (k, m, 0))
    out = jax.ShapeDtypeStruct((nk, S, w), BF16)
    return pl.pallas_call(
        _blind_to(body, 4, len(deps)), name=name, grid=(nk, S // tm),
        in_specs=[pl.BlockSpec((tm, D), lambda k, m: (m, 0)), pl.BlockSpec((None, w, D), lambda k, m: (k, 0, 0)),
                  cspec, cspec] + [ANY_SPEC] * len(deps),
        out_specs=[cspec, cspec], out_shape=[out, out],
        compiler_params=_params(("parallel", "parallel")),
    )(df, wd, ga, up, *deps)


def _mm_nt(d, w, name):
    S, K = d.shape
    N = w.shape[0]
    tm = _row_tile(S, 512)

    def body(d_ref, w_ref, o_ref):
        o_ref[...] = lax.dot_general(d_ref[...], w_ref[...], NT, preferred_element_type=F32)

    return pl.pallas_call(
        body, name=name, grid=(S // tm,),
        in_specs=[pl.BlockSpec((tm, K), lambda m: (m, 0)), pl.BlockSpec((N, K), lambda m: (0, 0))],
        out_specs=pl.BlockSpec((tm, N), lambda m: (m, 0)),
        out_shape=jax.ShapeDtypeStruct((S, N), F32),
        compiler_params=_params(("parallel",)),
    )(d, w)


def _mm_reduce(lhs_list, w_list, chunked3d, w_is_kd, name, deps=()):
    nk = w_list[0].shape[0]
    kc, D = w_list[0].shape[1:] if w_is_kd else w_list[0].shape[:0:-1]
    S = lhs_list[0].shape[1] if chunked3d else lhs_list[0].shape[0]
    tm = _row_tile(S, 512)
    npair = len(lhs_list)
    deps = _deps(deps)

    def body(*refs):
        l_refs, w_refs = refs[:npair], refs[npair:2 * npair]
        o_ref, acc_ref = refs[2 * npair], refs[2 * npair + 1]
        k = pl.program_id(1)

        @pl.when(k == 0)
        def _():
            acc_ref[...] = jnp.zeros_like(acc_ref)

        for l_ref, w_ref in zip(l_refs, w_refs):
            if w_is_kd:
                acc_ref[...] += jnp.dot(l_ref[...], w_ref[...], preferred_element_type=F32)
            else:
                acc_ref[...] += lax.dot_general(l_ref[...], w_ref[...], NT, preferred_element_type=F32)

        @pl.when(k == nk - 1)
        def _():
            o_ref[...] = acc_ref[...]

    if chunked3d:
        lspec = pl.BlockSpec((None, tm, kc), lambda m, k: (k, m, 0))
    else:
        lspec = pl.BlockSpec((tm, kc), lambda m, k: (m, k))
    wspec = pl.BlockSpec((None,) + tuple(w_list[0].shape[1:]), lambda m, k: (k, 0, 0))
    return pl.pallas_call(
        _blind_to(body, 2 * npair, len(deps)), name=name, grid=(S // tm, nk),
        in_specs=[lspec] * npair + [wspec] * npair + [ANY_SPEC] * len(deps),
        out_specs=pl.BlockSpec((tm, D), lambda m, k: (m, 0)),
        out_shape=jax.ShapeDtypeStruct((S, D), F32),
        scratch_shapes=[pltpu.VMEM((tm, D), F32)],
        compiler_params=_params(("parallel", "arbitrary")),
    )(*lhs_list, *w_list, *deps)


def _wgrad_chunk_lhs(lhs_list, rhs, name, deps=()):
    nk, S, w = lhs_list[0].shape
    D = rhs.shape[1]
    ts = _row_tile(S, 512)
    ns = S // ts
    nl = len(lhs_list)
    deps = _deps(deps)

    def body(*refs):
        l_refs, r_ref = refs[:nl], refs[nl]
        o_refs, acc_refs = refs[nl + 1:2 * nl + 1], refs[2 * nl + 1:]
        s = pl.program_id(1)

        @pl.when(s == 0)
        def _():
            for acc_ref in acc_refs:
                acc_ref[...] = jnp.zeros_like(acc_ref)

        x = r_ref[...]
        for l_ref, acc_ref in zip(l_refs, acc_refs):
            acc_ref[...] += lax.dot_general(l_ref[...], x, TN, preferred_element_type=F32)

        @pl.when(s == ns - 1)
        def _():
            for o_ref, acc_ref in zip(o_refs, acc_refs):
                o_ref[...] = acc_ref[...]

    return pl.pallas_call(
        _blind_to(body, nl + 1, len(deps)), name=name, grid=(nk, ns),
        in_specs=[pl.BlockSpec((None, ts, w), lambda k, s: (k, s, 0))] * nl
        + [pl.BlockSpec((ts, D), lambda k, s: (s, 0))] + [ANY_SPEC] * len(deps),
        out_specs=[pl.BlockSpec((None, w, D), lambda k, s: (k, 0, 0))] * nl,
        out_shape=[jax.ShapeDtypeStruct((nk, w, D), F32)] * nl,
        scratch_shapes=[pltpu.VMEM((w, D), F32)] * nl,
        compiler_params=_params(("parallel", "arbitrary")),
    )(*lhs_list, rhs, *deps)


def _wgrad_chunk_rhs(lhs, rhs_list, nk, chunked3d, name, deps=()):
    S, D = lhs.shape
    w = rhs_list[0].shape[2] if chunked3d else rhs_list[0].shape[1] // nk
    ts = _row_tile(S, 512)
    ns = S // ts
    nr = len(rhs_list)
    deps = _deps(deps)

    def body(*refs):
        l_ref, r_refs = refs[0], refs[1:1 + nr]
        o_refs, acc_refs = refs[1 + nr:1 + 2 * nr], refs[1 + 2 * nr:]
        s = pl.program_id(1)

        @pl.when(s == 0)
        def _():
            for acc_ref in acc_refs:
                acc_ref[...] = jnp.zeros_like(acc_ref)

        x = l_ref[...]
        for r_ref, acc_ref in zip(r_refs, acc_refs):
            acc_ref[...] += lax.dot_general(x, r_ref[...], TN, preferred_element_type=F32)

        @pl.when(s == ns - 1)
        def _():
            for o_ref, acc_ref in zip(o_refs, acc_refs):
                o_ref[...] = acc_ref[...]

    if chunked3d:
        rspec = pl.BlockSpec((None, ts, w), lambda k, s: (k, s, 0))
    else:
        rspec = pl.BlockSpec((ts, w), lambda k, s: (s, k))
    ospec = pl.BlockSpec((None, D, w), lambda k, s: (k, 0, 0))
    return pl.pallas_call(
        _blind_to(body, 1 + nr, len(deps)), name=name, grid=(nk, ns),
        in_specs=[pl.BlockSpec((ts, D), lambda k, s: (s, 0))] + [rspec] * nr + [ANY_SPEC] * len(deps),
        out_specs=[ospec] * nr,
        out_shape=[jax.ShapeDtypeStruct((nk, D, w), F32)] * nr,
        scratch_shapes=[pltpu.VMEM((D, w), F32)] * nr,
        compiler_params=_params(("parallel", "arbitrary")),
    )(lhs, *rhs_list, *deps)


def _rope_tables(S):
    pos = np.arange(S, dtype=np.float32)
    inv_freq = (ROPE_THETA ** (-np.arange(0, HEAD_DIM, 2, dtype=np.float32) / HEAD_DIM)).astype(np.float32)
    ang = (pos[:, None] * inv_freq[None, :]).astype(np.float64)
    cos, sin = np.cos(ang).astype(np.float32), np.sin(ang).astype(np.float32)
    cos2 = np.concatenate([cos, cos, cos, cos], axis=1)
    sin2 = np.concatenate([-sin, sin, -sin, sin], axis=1)
    return jnp.asarray(cos2), jnp.asarray(sin2)


def _rotate(t, cos, sin_signed):
    half = HEAD_DIM // 2
    lane = lax.broadcasted_iota(jnp.int32, t.shape, 1)
    first = (lane % HEAD_DIM) < half
    partner = jnp.where(first, pltpu.roll(t, 128 - half, 1), pltpu.roll(t, half, 1))
    return t * cos + partner * sin_signed


def _qkv_rope(proj, cos, sin, name):
    S = proj.shape[0]
    A = ATTN_WIDTH
    tr = _row_tile(S, 512)
    nb = A // 128
    scale = HEAD_DIM ** -0.5

    def body(q_ref, k_ref, v_ref, c_ref, s_ref, qo_ref, ko_ref, vo_ref):
        c, s = c_ref[...], s_ref[...]
        qo_ref[...] = _rotate(q_ref[...], c, s) * scale
        ko_ref[...] = _rotate(k_ref[...], c, s)
        vo_ref[...] = v_ref[...]

    def col(off):
        return pl.BlockSpec((tr, 128), lambda i, j: (i, off + j))

    tab = pl.BlockSpec((tr, 128), lambda i, j: (i, 0))
    out = jax.ShapeDtypeStruct((S, A), F32)
    return pl.pallas_call(
        body, name=name, grid=(S // tr, nb),
        in_specs=[col(0), col(nb), col(2 * nb), tab, tab],
        out_specs=[col(0), col(0), col(0)], out_shape=[out, out, out],
        compiler_params=_params(("parallel", "parallel")),
    )(proj, proj, proj, cos, sin)


def _band_mask(T, has_prev):
    qi = lax.broadcasted_iota(jnp.int32, (T, 2 * T), 0)
    kj = lax.broadcasted_iota(jnp.int32, (T, 2 * T), 1)
    return ((kj < T) & (kj >= qi) & has_prev) | ((kj >= T) & (kj - T <= qi))


def _stack_heads(x, head0):
    zero = jnp.zeros_like(x)
    return jnp.concatenate([jnp.where(head0, x, zero), jnp.where(head0, zero, x)], axis=0)


def _branch_blocks(rows, dilation):
    T = min(ATTN_BLOCK, rows // dilation)
    return T, rows // T


def _block_rows(base, T, dilation):
    if dilation == 1:
        return pl.ds(pl.multiple_of(base, T), T)
    return pl.ds(base, T, stride=dilation)


def _attn_fwd(q, k, v, name):
    S, A = q.shape
    sup = min(S, SUPER_ROWS)
    nd = len(DILATIONS)
    assert S % sup == 0

    def body(q_ref, k_ref, v_ref, attn_ref, lse_ref, acc_s, m_s, l_s):
        lane = lax.broadcasted_iota(jnp.int32, (1, 128), 1)
        head0 = lane < HEAD_DIM

        def supertile(st, carry):
            row0 = st * sup
            for di, dil in enumerate(DILATIONS):
                T, nblk = _branch_blocks(sup, dil)
                span = T * dil
                assert T == ATTN_BLOCK or sup == S

                def block(idx, c2, di=di, dil=dil, T=T, span=span):
                    r = idx % dil
                    loc = (idx // dil) * span + r
                    base = row0 + loc
                    rows = _block_rows(base, T, dil)
                    prev = _block_rows(jnp.maximum(base - span, r), T, dil)
                    qb = q_ref[rows, :].astype(BF16)
                    k2 = jnp.concatenate([k_ref[prev, :], k_ref[rows, :]], axis=0).astype(BF16)
                    v2 = jnp.concatenate([v_ref[prev, :], v_ref[rows, :]], axis=0).astype(BF16)
                    valid = _band_mask(T, base >= span)
                    q2 = _stack_heads(qb, head0)
                    s = lax.dot_general(q2, k2, NT, preferred_element_type=F32)
                    s = jnp.where(jnp.concatenate([valid, valid], axis=0), s, NEG)
                    m = jnp.max(s, axis=-1, keepdims=True)
                    p = jnp.exp(s - m)
                    l = jnp.sum(p, axis=-1, keepdims=True)
                    acc = jnp.dot(p.astype(BF16), v2, preferred_element_type=F32)
                    lrows = _block_rows(di * sup + loc, T, dil)
                    acc_s[lrows, :] = jnp.where(head0, acc[:T], acc[T:])
                    m_s[lrows, :] = jnp.where(head0, m[:T], m[T:])
                    l_s[lrows, :] = jnp.where(head0, l[:T], l[T:])
                    return c2

                lax.fori_loop(0, nblk, block, 0, unroll=4)

            chunk = min(sup, 256)

            def merge(ci, c2):
                lr = [pl.ds(pl.multiple_of(di * sup + ci * chunk, chunk), chunk) for di in range(nd)]
                gr = pl.ds(pl.multiple_of(row0 + ci * chunk, chunk), chunk)
                m0, m1, m2 = m_s[lr[0], :], m_s[lr[1], :], m_s[lr[2], :]
                mm = jnp.maximum(jnp.maximum(m0, m1), m2)
                w0, w1, w2 = jnp.exp(m0 - mm), jnp.exp(m1 - mm), jnp.exp(m2 - mm)
                den = (w0 * l_s[lr[0], :] + w1 * l_s[lr[1], :]) + w2 * l_s[lr[2], :]
                num = (w0 * acc_s[lr[0], :] + w1 * acc_s[lr[1], :]) + w2 * acc_s[lr[2], :]
                attn_ref[gr, :] = num / den
                lse_ref[gr, :] = mm + jnp.log(den)
                return c2

            lax.fori_loop(0, sup // chunk, merge, 0)
            return carry

        lax.fori_loop(0, S // sup, supertile, 0)

    blk = pl.BlockSpec((S, 128), lambda j: (0, j))
    out = jax.ShapeDtypeStruct((S, A), F32)
    return pl.pallas_call(
        body, name=name, grid=(A // 128,),
        in_specs=[blk, blk, blk], out_specs=[blk, blk], out_shape=[out, out],
        scratch_shapes=[pltpu.VMEM((nd * sup, 128), F32)] * 3,
        compiler_params=_params(("parallel",)),
    )(q, k, v)


def _attn_out_bwd(dy, attn, gain, name):
    S, A = attn.shape
    tr = _row_tile(S, 256)

    def body(dy_ref, a_ref, g_ref, da_ref, dl_ref, dg_ref):
        @pl.when(pl.program_id(0) == 0)
        def _():
            dg_ref[...] = jnp.zeros_like(dg_ref)

        x = a_ref[...]
        dy_ = dy_ref[...]
        r = lax.rsqrt(jnp.mean(x * x, axis=-1, keepdims=True) + RMS_EPS)
        xh = x * r
        dg_ref[...] += jnp.sum(dy_ * xh, axis=0, keepdims=True)
        dxh = dy_ * g_ref[...]
        dx = r * (dxh - xh * jnp.mean(dxh * xh, axis=-1, keepdims=True))
        da_ref[...] = dx
        prod = dx * x
        hi = lax.broadcasted_iota(jnp.int32, (A, A), 0) // HEAD_DIM
        hj = lax.broadcasted_iota(jnp.int32, (A, A), 1) // HEAD_DIM
        same_head = (hi == hj).astype(F32)
        dl_ref[...] = jnp.dot(prod, same_head, preferred_element_type=F32, precision=lax.Precision.HIGHEST)

    row = pl.BlockSpec((tr, A), lambda i: (i, 0))
    vec = _vec_spec(A, 1)
    return pl.pallas_call(
        body, name=name, grid=(S // tr,),
        in_specs=[row, row, vec], out_specs=[row, row, vec],
        out_shape=[jax.ShapeDtypeStruct((S, A), F32), jax.ShapeDtypeStruct((S, A), F32),
                   jax.ShapeDtypeStruct((1, A), F32)],
        compiler_params=_params(("arbitrary",)),
    )(dy, attn, gain)


def _attn_bwd(q, k, v, da, lse, delta, name):
    S, A = q.shape

    def body(q_ref, k_ref, v_ref, da_ref, lse_ref, dl_ref, dq_ref, dk_ref, dv_ref):
        lane = lax.broadcasted_iota(jnp.int32, (1, 128), 1)
        head0 = lane < HEAD_DIM
        dq_ref[...] = jnp.zeros_like(dq_ref)
        dk_ref[...] = jnp.zeros_like(dk_ref)
        dv_ref[...] = jnp.zeros_like(dv_ref)
        for dil in DILATIONS:
            T, nblk = _branch_blocks(S, dil)
            span = T * dil

            def block(idx, carry, dil=dil, T=T, span=span):
                r = idx % dil
                base = (idx // dil) * span + r
                rows = _block_rows(base, T, dil)
                prev = _block_rows(jnp.maximum(base - span, r), T, dil)
                qb, dab = q_ref[rows, :].astype(BF16), da_ref[rows, :].astype(BF16)
                k2 = jnp.concatenate([k_ref[prev, :], k_ref[rows, :]], axis=0).astype(BF16)
                v2 = jnp.concatenate([v_ref[prev, :], v_ref[rows, :]], axis=0).astype(BF16)
                lse_b, dl_b = lse_ref[rows, :], dl_ref[rows, :]
                valid = _band_mask(T, base >= span)
                valid2 = jnp.concatenate([valid, valid], axis=0)
                q2, da2 = _stack_heads(qb, head0), _stack_heads(dab, head0)
                lse2 = jnp.concatenate([lse_b[:, 0:1], lse_b[:, HEAD_DIM:HEAD_DIM + 1]], axis=0)
                dl2 = jnp.concatenate([dl_b[:, 0:1], dl_b[:, HEAD_DIM:HEAD_DIM + 1]], axis=0)
                s = lax.dot_general(q2, k2, NT, preferred_element_type=F32)
                p = jnp.where(valid2, jnp.exp(s - lse2), 0.0)
                dp = lax.dot_general(da2, v2, NT, preferred_element_type=F32)
                ds = (p * (dp - dl2)).astype(BF16)
                dq2 = jnp.dot(ds, k2, preferred_element_type=F32)
                dk2 = lax.dot_general(ds, q2, TN, preferred_element_type=F32)
                dv2 = lax.dot_general(p.astype(BF16), da2, TN, preferred_element_type=F32)
                dq_ref[rows, :] += jnp.where(head0, dq2[:T], dq2[T:])
                dk_ref[rows, :] += dk2[T:]
                dv_ref[rows, :] += dv2[T:]
                dk_ref[prev, :] += dk2[:T]
                dv_ref[prev, :] += dv2[:T]
                return carry

            lax.fori_loop(0, nblk, block, 0, unroll=4)

    blk = pl.BlockSpec((S, 128), lambda j: (0, j))
    out = jax.ShapeDtypeStruct((S, A), F32)
    return pl.pallas_call(
        body, name=name, grid=(A // 128,),
        in_specs=[blk] * 6, out_specs=[blk] * 3, out_shape=[out] * 3,
        compiler_params=_params(("parallel",)),
    )(q, k, v, da, lse, delta)


SUBLANES = 8
CONV_CHUNK = 64
FIRST_TAP = HALO - (CONV_KERNEL - 1)


def _store_shifted(shift_s, win, rows):
    shift_s[0, pl.ds(0, rows), :] = win
    for b in range(1, SUBLANES):
        shift_s[b, pl.ds(0, rows - SUBLANES), :] = win[b:b + rows - SUBLANES, :]


def _glu_window(a_ref, b_ref, ah_ref, bh_ref, first):
    u0 = a_ref[...] * _sigmoid(b_ref[...])
    u0h = ah_ref[...] * _sigmoid(bh_ref[...])
    u0h = jnp.where(first, jnp.zeros_like(u0h), u0h)
    return jnp.concatenate([u0h, u0], axis=0)


def _conv_norms(u1, lng, lnb):
    mu = jnp.mean(u1, axis=-1, keepdims=True)
    xc = u1 - mu
    rstd = lax.rsqrt(jnp.mean(xc * xc, axis=-1, keepdims=True) + LN_EPS)
    u1h = xc * rstd
    u2 = u1h * lng + lnb
    sig = _sigmoid(u2)
    u3 = u2 * sig
    r = lax.rsqrt(jnp.mean(u3 * u3, axis=-1, keepdims=True) + RMS_EPS)
    return rstd, u1h, u2, sig, u3, r


def _conv_specs(tr, C, col_a, col_b):
    per = tr // HALO

    def tile(col):
        return pl.BlockSpec((tr, C), lambda i: (i, col))

    def halo(col):
        return pl.BlockSpec((HALO, C), lambda i: (jnp.maximum(i * per - 1, 0), col))

    return tile(col_a), tile(col_b), halo(col_a), halo(col_b)


def _mixer_merge(proj, attn, cw, cb, lng, lnb, gat, gco, name):
    S = proj.shape[0]
    C = CONV_WIDTH
    A = attn.shape[1]
    tr = _row_tile(S, 256)

    def body(a_ref, b_ref, ah_ref, bh_ref, at_ref, w_ref, cb_ref, lng_ref, lnb_ref, gat_ref, gco_ref, u1_ref, y_ref,
             shift_s):
        _store_shifted(shift_s, _glu_window(a_ref, b_ref, ah_ref, bh_ref, pl.program_id(0) == 0), tr + HALO)

        def chunk(rc, carry):
            r0 = pl.multiple_of(rc * CONV_CHUNK, CONV_CHUNK)
            for lb in range(C // 128):
                lanes = slice(lb * 128, (lb + 1) * 128)
                acc = jnp.broadcast_to(cb_ref[:, lanes], (CONV_CHUNK, 128))
                for j in range(CONV_KERNEL):
                    a8, b = divmod(FIRST_TAP + j, SUBLANES)
                    acc = acc + w_ref[j:j + 1, lanes] * shift_s[b, pl.ds(r0 + a8 * SUBLANES, CONV_CHUNK), lanes]
                u1_ref[pl.ds(r0, CONV_CHUNK), lanes] = acc
            return carry

        lax.fori_loop(0, tr // CONV_CHUNK, chunk, 0)
        _, _, _, _, u3, r = _conv_norms(u1_ref[...], lng_ref[...], lnb_ref[...])
        y_ref[:, A:] = ((u3 * r) * gco_ref[...]).astype(BF16)
        x = at_ref[...]
        ra = lax.rsqrt(jnp.mean(x * x, axis=-1, keepdims=True) + RMS_EPS)
        y_ref[:, :A] = ((x * ra) * gat_ref[...]).astype(BF16)

    ta, tb, ha, hb = _conv_specs(tr, C, 3, 4)
    row = pl.BlockSpec((tr, C), lambda i: (i, 0))
    vec = _vec_spec(C, 1)
    return pl.pallas_call(
        body, name=name, grid=(S // tr,),
        in_specs=[ta, tb, ha, hb, pl.BlockSpec((tr, A), lambda i: (i, 0)), pl.BlockSpec((HALO, C), lambda i: (0, 0)),
                  vec, vec, vec, _vec_spec(A, 1), vec],
        out_specs=[row, pl.BlockSpec((tr, A + C), lambda i: (i, 0))],
        out_shape=[jax.ShapeDtypeStruct((S, C), F32), jax.ShapeDtypeStruct((S, A + C), BF16)],
        scratch_shapes=[pltpu.VMEM((SUBLANES, tr + HALO, C), F32)],
        compiler_params=_params(("parallel",)),
    )(proj, proj, proj, proj, attn, cw, cb, lng, lnb, gat, gco)


def _conv_bwd_norms(dy, u1, lng, lnb, gco, name):
    S, C = u1.shape
    tr = _row_tile(S, 256)

    def body(dy_ref, u1_ref, lng_ref, lnb_ref, gco_ref, du1_ref, dgco_ref, dlng_ref, dlnb_ref, dcb_ref):
        @pl.when(pl.program_id(0) == 0)
        def _():
            for ref in (dgco_ref, dlng_ref, dlnb_ref, dcb_ref):
                ref[...] = jnp.zeros_like(ref)

        lng = lng_ref[...]
        rstd, u1h, u2, sig, u3, r = _conv_norms(u1_ref[...], lng, lnb_ref[...])
        dy_ = dy_ref[...]
        u3h = u3 * r
        dgco_ref[...] += jnp.sum(dy_ * u3h, axis=0, keepdims=True)
        du3h = dy_ * gco_ref[...]
        du3 = r * (du3h - u3h * jnp.mean(du3h * u3h, axis=-1, keepdims=True))
        du2 = du3 * (sig * (1.0 + u2 * (1.0 - sig)))
        dlng_ref[...] += jnp.sum(du2 * u1h, axis=0, keepdims=True)
        dlnb_ref[...] += jnp.sum(du2, axis=0, keepdims=True)
        du1h = du2 * lng
        du1 = rstd * (du1h - jnp.mean(du1h, axis=-1, keepdims=True)
                      - u1h * jnp.mean(du1h * u1h, axis=-1, keepdims=True))
        du1_ref[...] = du1
        dcb_ref[...] += jnp.sum(du1, axis=0, keepdims=True)

    row = pl.BlockSpec((tr, C), lambda i: (i, 0))
    vec = _vec_spec(C, 1)
    return pl.pallas_call(
        body, name=name, grid=(S // tr,),
        in_specs=[pl.BlockSpec((tr, C), lambda i: (i, 1)), row, vec, vec, vec],
        out_specs=[row, vec, vec, vec, vec],
        out_shape=[jax.ShapeDtypeStruct((S, C), F32)] + [jax.ShapeDtypeStruct((1, C), F32)] * 4,
        compiler_params=_params(("arbitrary",)),
    )(dy, u1, lng, lnb, gco)


def _dproj(du1, proj, cw, dq, dk, dv, cos, sin, name):
    S, C = du1.shape
    A = dq.shape[1]
    tr = _row_tile(S, 256)
    nt = S // tr
    per = tr // HALO
    scale = HEAD_DIM ** -0.5

    def body(du_ref, dun_ref, a_ref, b_ref, ah_ref, bh_ref, w_ref, dq_ref, dk_ref, dv_ref, cos_ref, sin_ref,
             dp_ref, dw_ref, win_s, dwin_s, du0_s, tap_s):
        i = pl.program_id(0)

        @pl.when(i == 0)
        def _():
            dw_ref[...] = jnp.zeros_like(dw_ref)

        _store_shifted(win_s, _glu_window(a_ref, b_ref, ah_ref, bh_ref, i == 0), tr + HALO)
        nxt = jnp.where(i == nt - 1, jnp.zeros_like(dun_ref[...]), dun_ref[...])
        _store_shifted(dwin_s, jnp.concatenate([du_ref[...], nxt], axis=0), tr + HALO)
        tap_s[...] = jnp.zeros_like(tap_s)

        def chunk(rc, carry):
            r0 = pl.multiple_of(rc * CONV_CHUNK, CONV_CHUNK)
            for lb in range(C // 128):
                lanes = slice(lb * 128, (lb + 1) * 128)
                du = du_ref[pl.ds(r0, CONV_CHUNK), lanes]
                acc = jnp.zeros((CONV_CHUNK, 128), F32)
                for j in range(CONV_KERNEL):
                    a8, b = divmod(CONV_KERNEL - 1 - j, SUBLANES)
                    acc = acc + w_ref[j:j + 1, lanes] * dwin_s[b, pl.ds(r0 + a8 * SUBLANES, CONV_CHUNK), lanes]
                    a8, b = divmod(FIRST_TAP + j, SUBLANES)
                    prod = du * win_s[b, pl.ds(r0 + a8 * SUBLANES, CONV_CHUNK), lanes]
                    part = prod[0:SUBLANES]
                    for g in range(1, CONV_CHUNK // SUBLANES):
                        part = part + prod[g * SUBLANES:(g + 1) * SUBLANES]
                    tap_s[j * SUBLANES:(j + 1) * SUBLANES, lanes] += part
                du0_s[pl.ds(r0, CONV_CHUNK), lanes] = acc
            return carry

        lax.fori_loop(0, tr // CONV_CHUNK, chunk, 0)
        taps = [jnp.sum(tap_s[j * SUBLANES:(j + 1) * SUBLANES, :], axis=0, keepdims=True)
                for j in range(CONV_KERNEL)]
        taps.append(jnp.zeros((HALO - CONV_KERNEL, C), F32))
        dw_ref[...] += jnp.concatenate(taps, axis=0)
        du0 = du0_s[...]
        a, sig = a_ref[...], _sigmoid(b_ref[...])
        dp_ref[:, 3 * A:3 * A + C] = (du0 * sig).astype(BF16)
        dp_ref[:, 3 * A + C:] = (du0 * a * sig * (1.0 - sig)).astype(BF16)
        cos_, nsin = cos_ref[...], -sin_ref[...]
        for j in range(A // 128):
            lanes = slice(j * 128, (j + 1) * 128)
            dp_ref[:, j * 128:(j + 1) * 128] = (_rotate(dq_ref[:, lanes], cos_, nsin) * scale).astype(BF16)
            dp_ref[:, A + j * 128:A + (j + 1) * 128] = _rotate(dk_ref[:, lanes], cos_, nsin).astype(BF16)
        dp_ref[:, 2 * A:3 * A] = dv_ref[...].astype(BF16)

    ta, tb, ha, hb = _conv_specs(tr, C, 3, 4)
    row = pl.BlockSpec((tr, C), lambda i: (i, 0))
    arow = pl.BlockSpec((tr, A), lambda i: (i, 0))
    tab = pl.BlockSpec((tr, 128), lambda i: (i, 0))
    nxt = pl.BlockSpec((HALO, C), lambda i: (jnp.minimum((i + 1) * per, S // HALO - 1), 0))
    wspec = pl.BlockSpec((HALO, C), lambda i: (0, 0))
    return pl.pallas_call(
        body, name=name, grid=(nt,),
        in_specs=[row, nxt, ta, tb, ha, hb, wspec, arow, arow, arow, tab, tab],
        out_specs=[pl.BlockSpec((tr, 3 * A + 2 * C), lambda i: (i, 0)), wspec],
        out_shape=[jax.ShapeDtypeStruct((S, 3 * A + 2 * C), BF16), jax.ShapeDtypeStruct((HALO, C), F32)],
        scratch_shapes=[pltpu.VMEM((SUBLANES, tr + HALO, C), F32), pltpu.VMEM((SUBLANES, tr + HALO, C), F32),
                        pltpu.VMEM((tr, C), F32), pltpu.VMEM((HALO * SUBLANES, C), F32)],
        compiler_params=_params(("arbitrary",)),
    )(du1, du1, proj, proj, proj, proj, cw, dq, dk, dv, cos, sin)


def _ada_fwd(c_all, w, b, name):
    B, D = c_all.shape
    N = w.shape[1]
    tn = 768 if N % 768 == 0 else N

    def body(c_ref, w_ref, b_ref, o_ref):
        c = c_ref[...]
        a = (c * _sigmoid(c)).astype(BF16)
        o_ref[...] = jnp.dot(a, w_ref[...].astype(BF16), preferred_element_type=F32) + b_ref[...]

    return pl.pallas_call(
        body, name=name, grid=(N // tn,),
        in_specs=[pl.BlockSpec((B, D), lambda j: (0, 0)), pl.BlockSpec((D, tn), lambda j: (0, j)),
                  pl.BlockSpec((1, tn), lambda j: (0, j))],
        out_specs=pl.BlockSpec((B, tn), lambda j: (0, j)),
        out_shape=jax.ShapeDtypeStruct((B, N), F32),
        compiler_params=_params(("parallel",)),
    )(c_all, w, b)


def _ada_wgrad(c_t, dmod, name):
    D, B = c_t.shape
    N = dmod.shape[1]
    tn = 768 if N % 768 == 0 else N

    def body(c_ref, d_ref, o_ref):
        c = c_ref[...]
        a = (c * _sigmoid(c)).astype(BF16)
        o_ref[...] = jnp.dot(a, d_ref[...].astype(BF16), preferred_element_type=F32)

    return pl.pallas_call(
        body, name=name, grid=(N // tn,),
        in_specs=[pl.BlockSpec((D, B), lambda j: (0, 0)), pl.BlockSpec((B, tn), lambda j: (0, j))],
        out_specs=pl.BlockSpec((D, tn), lambda j: (0, j)),
        out_shape=jax.ShapeDtypeStruct((D, N), F32),
        compiler_params=_params(("parallel",)),
    )(c_t, dmod)


def _sum_rows(x, name):
    R, N = x.shape

    def body(x_ref, o_ref):
        acc = x_ref[0:1, :]
        for r in range(1, R):
            acc = acc + x_ref[r:r + 1, :]
        o_ref[...] = acc

    return pl.pallas_call(
        body, name=name, out_shape=jax.ShapeDtypeStruct((1, N), F32),
        compiler_params=_params(),
    )(x)


def _adamw(w, g, m, v, name, deps=()):
    R, C = w.shape
    tr = _row_tile(R, 256) if R % 256 == 0 else R
    bc1 = 1.0 - ADAM_B1 ** ADAM_STEP
    bc2 = 1.0 - ADAM_B2 ** ADAM_STEP
    deps = _deps(deps)

    def body(w_ref, g_ref, m_ref, v_ref, d_ref, mo_ref, vo_ref):
        g_ = g_ref[...]
        m_ = ADAM_B1 * m_ref[...] + (1.0 - ADAM_B1) * g_
        v_ = ADAM_B2 * v_ref[...] + (1.0 - ADAM_B2) * (g_ * g_)
        mo_ref[...] = m_
        vo_ref[...] = v_
        d_ref[...] = -ADAM_LR * ((m_ / bc1) / (jnp.sqrt(v_ / bc2) + ADAM_EPS) + ADAM_WD * w_ref[...])

    row = pl.BlockSpec((tr, C), lambda i: (i, 0))
    out = jax.ShapeDtypeStruct((R, C), F32)
    return pl.pallas_call(
        _blind_to(body, 4, len(deps)), name=name, grid=(R // tr,),
        in_specs=[row] * 4 + [ANY_SPEC] * len(deps), out_specs=[row] * 3, out_shape=[out] * 3,
        compiler_params=_params(("parallel",), streaming=True),
    )(w, g, m, v, *deps)


def _coords():
    return lax.axis_index("x"), lax.axis_index("y"), lax.axis_index("c")


def _all_gather8(x, name, deps=()):
    R, N = x.shape
    assert R == 8
    flips = [(fx, fy, fc) for fx in (0, 1) for fy in (0, 1) for fc in (0, 1)][1:]
    deps = _deps(deps)

    def body(x_ref, o_ref, send_sems, recv_sems):
        mx, my, mc = _coords()
        me = 4 * mx + 2 * my + mc

        def rows(dev):
            return o_ref.at[pl.ds(pl.multiple_of(dev * R, R), R), :]

        o_ref[pl.ds(pl.multiple_of(me * R, R), R), :] = x_ref[...]
        copies = []
        for t, (fx, fy, fc) in enumerate(flips):
            peer = (mx ^ fx, my ^ fy, mc ^ fc)
            copies.append(pltpu.make_async_remote_copy(
                src_ref=x_ref, dst_ref=rows(me), send_sem=send_sems.at[t], recv_sem=recv_sems.at[t],
                device_id=peer, device_id_type=MESH))
        for cp in copies:
            cp.start()
        for t, (fx, fy, fc) in enumerate(flips):
            peer_id = 4 * (mx ^ fx) + 2 * (my ^ fy) + (mc ^ fc)
            pltpu.make_async_remote_copy(
                src_ref=x_ref, dst_ref=rows(peer_id), send_sem=send_sems.at[t], recv_sem=recv_sems.at[t],
                device_id=(mx ^ fx, my ^ fy, mc ^ fc), device_id_type=MESH).wait_recv()
        for cp in copies:
            cp.wait_send()

    return pl.pallas_call(
        _blind_to(body, 1, len(deps)), name=name,
        in_specs=[pl.BlockSpec(memory_space=pltpu.VMEM)] + [ANY_SPEC] * len(deps),
        out_specs=pl.BlockSpec(memory_space=pltpu.VMEM),
        out_shape=jax.ShapeDtypeStruct((N_DEV * R, N), F32),
        scratch_shapes=[pltpu.SemaphoreType.DMA((7,)), pltpu.SemaphoreType.DMA((7,))],
        compiler_params=pltpu.CompilerParams(has_side_effects=True, vmem_limit_bytes=VMEM_LIMIT_BYTES),
    )(x, *deps)


def _half_rows(rows, half):
    return pl.ds(pl.multiple_of(half * (rows // 2), 8), rows // 2)


def _split_start(bufs, plan, n, name):
    nb = len(bufs)

    def body(*refs):
        send_sems, recv_sems, token = refs[nb], refs[nb + 1], refs[-1]
        for t, (src, dst, dev) in enumerate(plan(refs[:nb])):
            pltpu.make_async_remote_copy(src_ref=src, dst_ref=dst, send_sem=send_sems.at[t],
                                         recv_sem=recv_sems.at[t], device_id=dev, device_id_type=MESH).start()
        token[...] = jnp.zeros_like(token)

    out = pl.pallas_call(
        body, name=name,
        out_shape=(pltpu.SemaphoreType.DMA((n,)), pltpu.SemaphoreType.DMA((n,)),
                   *[pltpu.HBM(b.shape, b.dtype) for b in bufs], jax.ShapeDtypeStruct((8, 128), F32)),
        in_specs=[HBM_SPEC] * nb,
        out_specs=(SEM_SPEC, SEM_SPEC, *[HBM_SPEC] * nb, pl.BlockSpec(memory_space=pltpu.VMEM)),
        input_output_aliases={i: 2 + i for i in range(nb)},
        compiler_params=pltpu.CompilerParams(has_side_effects=DATAFLOW),
    )(*[pltpu.with_memory_space_constraint(b, pltpu.HBM) for b in bufs])
    return out[0], out[1], list(out[2:2 + nb]), out[-1]


def _split_wait(bufs, send_sems, recv_sems, plan, after, name):
    nb = len(bufs)
    after = _deps(after)

    def body(*refs):
        ss, rs = refs[nb], refs[nb + 1]
        for t, (src, dst, dev) in enumerate(plan(refs[:nb])):
            cp = pltpu.make_async_remote_copy(src_ref=src, dst_ref=dst, send_sem=ss.at[t], recv_sem=rs.at[t],
                                              device_id=dev, device_id_type=MESH)
            cp.wait_send()
            cp.wait_recv()

    out = pl.pallas_call(
        body, name=name,
        out_shape=tuple(pltpu.HBM(b.shape, b.dtype) for b in bufs),
        in_specs=[HBM_SPEC] * nb + [SEM_SPEC, SEM_SPEC] + [ANY_SPEC] * len(after),
        out_specs=tuple([HBM_SPEC] * nb),
        input_output_aliases={i: i for i in range(nb)},
        compiler_params=pltpu.CompilerParams(has_side_effects=DATAFLOW),
    )(*bufs, send_sems, recv_sems, *after)
    return list(out)


class _Exchange:
    def __init__(self, bufs, plan, n, name):
        self.plan, self.name = plan, name
        self.send_sems, self.recv_sems, self.bufs, self.token = _split_start(bufs, plan, n, name + "_start")

    def wait(self, after):
        return _split_wait(self.bufs, self.send_sems, self.recv_sems, self.plan, after, self.name + "_wait")


def _cast_place(w, chip_idx, name, deps=()):
    R, C = w.shape
    tr = _row_tile(R, 256) if R % 256 == 0 else R
    deps = _deps(deps)

    def body(k_ref, w_ref, o_ref):
        o_ref[...] = w_ref[...].astype(BF16)

    grid_spec = pltpu.PrefetchScalarGridSpec(
        num_scalar_prefetch=1, grid=(R // tr,),
        in_specs=[pl.BlockSpec((tr, C), lambda i, k: (i, 0))] + [ANY_SPEC] * len(deps),
        out_specs=pl.BlockSpec((None, tr, C), lambda i, k: (k[0], i, 0)),
    )
    return pl.pallas_call(
        _blind_to(body, 2, len(deps)), name=name, grid_spec=grid_spec,
        out_shape=jax.ShapeDtypeStruct((N_CHIPS, R, C), BF16),
        compiler_params=_params(("parallel",), streaming=True),
    )(chip_idx, w, *deps)


def _plan_gather(refs):
    mx, my, mc = _coords()
    me = 2 * mx + my
    plan = []
    for g in refs:
        mine = g.at[me, _half_rows(g.shape[1], mc), :]
        for fx, fy in XY_FLIPS:
            plan.append((mine, mine, (mx ^ fx, my ^ fy, mc)))
    return plan


def _plan_swap(refs):
    mx, my, mc = _coords()
    plan = []
    for g in refs:
        for fx, fy in XY_FLIPS:
            have = g.at[2 * (mx ^ fx) + (my ^ fy), _half_rows(g.shape[1], mc), :]
            plan.append((have, have, (mx, my, 1 - mc)))
    return plan


def _plan_other_halves(refs):
    n = len(refs) // 2
    mx, my, mc = _coords()
    return [(g.at[pl.ds(0, N_CHIPS), _half_rows(g.shape[1], 1 - mc), :], land, (mx, my, 1 - mc))
            for g, land in zip(refs[:n], refs[n:])]


def _plan_chunks(refs):
    n = len(refs) // 2
    mx, my, mc = _coords()
    plan = []
    for s, land in zip(refs[:n], refs[n:]):
        for t, (fx, fy) in enumerate(XY_FLIPS):
            plan.append((s.at[2 * (mx ^ fx) + (my ^ fy)], land.at[t], (mx ^ fx, my ^ fy, mc)))
    return plan


def _plan_share(refs):
    mx, my, mc = _coords()
    return [(full.at[mc], full.at[mc], (mx, my, 1 - mc)) for full in refs]


def _add_half(g, recv, core_idx, name):
    nk, R, C = g.shape
    rh = R // 2
    tr = _row_tile(rh, 128) if rh % 128 == 0 else rh
    nt = rh // tr

    def body(c_ref, g_ref, r_ref, o_ref):
        o_ref[...] = (g_ref[...] + r_ref[...]).astype(BF16)

    grid_spec = pltpu.PrefetchScalarGridSpec(
        num_scalar_prefetch=1, grid=(nk, nt),
        in_specs=[pl.BlockSpec((None, tr, C), lambda k, i, c: (k, c[0] * nt + i, 0)),
                  pl.BlockSpec((None, tr, C), lambda k, i, c: (k, i, 0))],
        out_specs=pl.BlockSpec((None, tr, C), lambda k, i, c: (k, i, 0)),
    )
    return pl.pallas_call(
        body, name=name, grid_spec=grid_spec, out_shape=jax.ShapeDtypeStruct((nk, rh, C), BF16),
        compiler_params=_params(("parallel", "parallel"), streaming=True),
    )(core_idx, g, recv)


def _sum_chips(s, land, chip_core, name):
    _, rh, C = s.shape
    tr = _row_tile(rh, 128) if rh % 128 == 0 else rh

    def body(p_ref, s_ref, l_ref, o_ref):
        me = p_ref[0]
        acc = None
        for j in range(N_CHIPS):
            t = jnp.maximum(jnp.bitwise_xor(me, j) - 1, 0)
            term = jnp.where(me == j, s_ref[...], l_ref[t]).astype(F32)
            acc = term if acc is None else acc + term
        o_ref[...] = acc

    grid_spec = pltpu.PrefetchScalarGridSpec(
        num_scalar_prefetch=1, grid=(rh // tr,),
        in_specs=[pl.BlockSpec((None, tr, C), lambda i, p: (p[0], i, 0)),
                  pl.BlockSpec((3, tr, C), lambda i, p: (0, i, 0))],
        out_specs=pl.BlockSpec((None, tr, C), lambda i, p: (p[1], i, 0)),
    )
    return pl.pallas_call(
        body, name=name, grid_spec=grid_spec, out_shape=jax.ShapeDtypeStruct((2, rh, C), F32),
        compiler_params=_params(("parallel",), streaming=True),
    )(chip_core, s, land)


def _rs_send_halves(grads, tag):
    lands = [lax.empty((g.shape[0], g.shape[1] // 2, g.shape[2]), g.dtype) for g in grads]
    return _Exchange(list(grads) + lands, _plan_other_halves, len(grads), f"rs_halves_{tag}")


def _rs_send_chunks(ex, after, core_idx, tag):
    bufs = ex.wait(after)
    n = len(bufs) // 2
    sums = [_add_half(g, r, core_idx, f"rs_add_{tag}_{i}") for i, (g, r) in enumerate(zip(bufs[:n], bufs[n:]))]
    lands = [lax.empty((3,) + s.shape[1:], s.dtype) for s in sums]
    return _Exchange(sums + lands, _plan_chunks, 3 * n, f"rs_chunks_{tag}")


def _rs_send_share(ex, after, chip_core, tag):
    bufs = ex.wait(after)
    n = len(bufs) // 2
    fulls = [_sum_chips(s, l, chip_core, f"rs_sum_{tag}_{i}") for i, (s, l) in enumerate(zip(bufs[:n], bufs[n:]))]
    return _Exchange(fulls, _plan_share, n, f"rs_share_{tag}")


def _rs_finish(ex, after):
    return [b.reshape(2 * b.shape[1], b.shape[2]) for b in ex.wait(after)]


def _ffn_forward(h, n, gate, wg, wu, wd, tag, next_norm=None):
    ga, up, act = _ffn_gate_up(n, wg, wu, f"{tag}_gate_up")
    h_out, f, *n_next = _mm_residual(act, wd, h, gate, 0.5, f"{tag}_down", next_norm=next_norm)
    return (h_out, *n_next), (h, n, ga, up, act, f)


def _ffn_backward(dh_out, df, saved, gain, sc, wg, wu, wd, core_idx, tag, prev=None, last=False):
    h, n, ga, up, act, _ = saved
    (dwd,) = _wgrad_chunk_lhs([act], df, f"{tag}_dwd")
    ex_d = _rs_send_halves([dwd], f"{tag}_d")
    dga, dup = _ffn_dact(df, wd, ga, up, f"{tag}_dact", deps=[ex_d.token])
    ex_d = _rs_send_chunks(ex_d, [dga], core_idx, f"{tag}_d")
    dwg, dwu = _wgrad_chunk_lhs([dga, dup], n, f"{tag}_dwgu", deps=[ex_d.token])
    ex_gu = _rs_send_halves([dwg, dwu], f"{tag}_gu")
    dn = _mm_reduce([dga, dup], [wg, wu], True, True, f"{tag}_dn", deps=[ex_gu.token])
    if not last:
        ex_gu = _rs_send_chunks(ex_gu, [dn], core_idx, f"{tag}_gu")
    outs = _norm_mod_bwd(dn, h, gain, sc, dh_out, f"{tag}_norm_bwd", prev=prev, deps=[ex_gu.token])
    return outs, (ex_d, ex_gu)


def _pad_cols(v, n):
    return jnp.pad(v, ((0, 0), (0, n - v.shape[1])))


def _mixer_forward(h1, n2, gt2, win, wout, conv_w, conv_dw_b, conv_ln_g, conv_ln_b, attn_out_g, conv_out_g,
                   next_norm=None):
    S, D = h1.shape
    proj = _mm_cols(n2, win, "mix_in")
    cos, sin = _rope_tables(S)
    q, k, v = _qkv_rope(proj, cos, sin, "qkv_rope")
    attn, lse = _attn_fwd(q, k, v, "attn_fwd")
    u1, y = _mixer_merge(proj, attn, conv_w, conv_dw_b, conv_ln_g, conv_ln_b, attn_out_g, conv_out_g, "mix_merge")
    h2, mo, *n_next = _mm_residual(y[None], wout.reshape(1, D, D), h1, gt2, 1.0, "mix_out", next_norm=next_norm)
    return (h2, *n_next), (h1, n2, proj, cos, sin, q, k, v, attn, lse, u1, y, mo)


def _mixer_backward(dh2, dmo, saved, mix_norm_g, sc2, win, wout, conv_w, conv_ln_g, conv_ln_b, attn_out_g,
                    conv_out_g, core_idx, prev=None):
    h1, n2, proj, cos, sin, q, k, v, attn, lse, u1, y, _ = saved
    S, D = h1.shape
    (dwout,) = _wgrad_chunk_lhs([y[None]], dmo, "mix_dwout")
    dy = _mm_nt(dmo, wout.reshape(D, D), "mix_dy")
    dattn, delta, d_attn_g = _attn_out_bwd(dy, attn, attn_out_g, "attn_out_bwd")
    dq, dk, dv = _attn_bwd(q, k, v, dattn, lse, delta, "attn_bwd")
    du1, d_gco, d_lng, d_lnb, d_cb = _conv_bwd_norms(dy, u1, conv_ln_g, conv_ln_b, conv_out_g, "conv_bwd_norms")
    dproj, d_cw = _dproj(du1, proj, conv_w, dq, dk, dv, cos, sin, "mix_dproj")
    (dwin,) = _wgrad_chunk_rhs(n2, [dproj], N_CHIPS, False, "mix_dwin")
    ex = _rs_send_halves([dwin, dwout.reshape(N_CHIPS, D // N_CHIPS, D)], "mix")
    dn2 = _mm_reduce([dproj], [win], False, False, "mix_dn", deps=[ex.token])
    ex = _rs_send_chunks(ex, [dn2], core_idx, "mix")
    outs = _norm_mod_bwd(dn2, h1, mix_norm_g, sc2, dh2, "mix_norm_bwd", prev=prev, deps=[ex.token])
    return outs, ex, (d_cb, d_lng, d_lnb, d_attn_g, d_gco, d_cw)


def kernel(x, c, w_ada, b_ada, ffn1_norm_g, ffn1_w_gate, ffn1_w_up, ffn1_w_down, mix_norm_g, w_in, conv_dw_w, conv_dw_b, conv_ln_g, conv_ln_b, attn_out_g, conv_out_g, w_out, ffn2_norm_g, ffn2_w_gate, ffn2_w_up, ffn2_w_down, final_norm_g, loss_target, m_w_ada, m_b_ada, m_ffn1_norm_g, m_ffn1_w_gate, m_ffn1_w_up, m_ffn1_w_down, m_mix_norm_g, m_w_in, m_conv_dw_w, m_conv_dw_b, m_conv_ln_g, m_conv_ln_b, m_attn_out_g, m_conv_out_g, m_w_out, m_ffn2_norm_g, m_ffn2_w_gate, m_ffn2_w_up, m_ffn2_w_down, m_final_norm_g, v_w_ada, v_b_ada, v_ffn1_norm_g, v_ffn1_w_gate, v_ffn1_w_up, v_ffn1_w_down, v_mix_norm_g, v_w_in, v_conv_dw_w, v_conv_dw_b, v_conv_ln_g, v_conv_ln_b, v_attn_out_g, v_conv_out_g, v_w_out, v_ffn2_norm_g, v_ffn2_w_gate, v_ffn2_w_up, v_ffn2_w_down, v_final_norm_g):
    S, D = x.shape[1], x.shape[2]
    mx, my, mc = _coords()
    chip = 2 * mx + my
    dev = 4 * mx + 2 * my + mc
    chip_idx = chip.astype(jnp.int32).reshape(1)
    core_idx = mc.astype(jnp.int32).reshape(1)
    chip_core = jnp.stack([chip, mc]).astype(jnp.int32)
    h0 = x[0]
    target = loss_target[0]

    ncw = CONV_KERNEL * 128
    n0 = -(-(D + ncw) // 1024) * 1024
    pk0 = _pad_cols(jnp.concatenate([c.reshape(1, D), conv_dw_w.reshape(1, ncw)], axis=1), n0)
    g0 = _all_gather8(pk0.reshape(8, n0 // 8), "gather_c").reshape(N_DEV, n0)
    c_all = g0[:, :D]
    conv_w = jnp.concatenate([g0[2 * kc, D:D + ncw].reshape(CONV_KERNEL, 128) for kc in range(N_CHIPS)], axis=1)
    conv_w = jnp.pad(conv_w, ((0, HALO - CONV_KERNEL), (0, 0)))
    nmod = w_ada.shape[2]
    b_shard = lax.dynamic_slice(b_ada, (0, chip * nmod), (1, nmod))
    mod_part = _ada_fwd(c_all, w_ada[0], b_shard, "ada_fwd")
    g1 = _all_gather8(mod_part, "gather_mod")
    mod_all = jnp.concatenate([g1[16 * kc:16 * kc + 8] for kc in range(N_CHIPS)], axis=1)
    mod = lax.dynamic_slice(mod_all, (dev, 0), (1, 9 * D))
    sh1, sc1, gt1, sh2, sc2, gt2, sh3, sc3, gt3 = [mod[:, i * D:(i + 1) * D] for i in range(9)]

    def gather_start(ws, tag, dep):
        slots = [_cast_place(w, chip_idx, f"cast_{tag}_{i}", deps=[dep]) for i, w in enumerate(ws)]
        return _Exchange(slots, _plan_gather, 3 * len(ws), f"gather_{tag}")

    def swap_start(ex, after, tag):
        return _Exchange(ex.wait(after), _plan_swap, 3 * len(ex.bufs), f"swap_{tag}")

    ex_gu1 = gather_start([ffn1_w_gate[0].T, ffn1_w_up[0].T], "ffn1_gu", g1)
    n1 = _norm_mod(h0, ffn1_norm_g, sc1, sh1, "ffn1_norm", deps=[ex_gu1.token])
    ex_d1 = gather_start([ffn1_w_down[0]], "ffn1_d", n1)
    ex_wm = gather_start([w_in[0], w_out[0]], "mix", ex_d1.token)
    ex_w2 = gather_start([ffn2_w_gate[0].T, ffn2_w_up[0].T, ffn2_w_down[0]], "ffn2", ex_wm.token)

    wg1, wu1 = swap_start(ex_gu1, [ex_w2.token], "ffn1_gu").wait([])
    ga1, up1, act1 = _ffn_gate_up(n1, wg1, wu1, "ffn1_gate_up")
    (wd1,) = swap_start(ex_d1, [act1], "ffn1_d").wait([])
    ex_wm = swap_start(ex_wm, [wd1], "mix")
    h1, f1, n2 = _mm_residual(act1, wd1, h0, gt1, 0.5, "ffn1_down", next_norm=(mix_norm_g, sc2, sh2))
    saved1 = (h0, n1, ga1, up1, act1, f1)
    win, wout = ex_wm.wait([h1])
    ex_w2 = swap_start(ex_w2, [h1], "ffn2")
    (h2, n3), saved2 = _mixer_forward(h1, n2, gt2, win, wout, conv_w, conv_dw_b, conv_ln_g, conv_ln_b,
                                      attn_out_g, conv_out_g, next_norm=(ffn2_norm_g, sc3, sh3))
    wg2, wu2, wd2 = ex_w2.wait([h2])
    (h3,), saved3 = _ffn_forward(h2, n3, gt3, wg2, wu2, wd2, "ffn2")
    loss_part, dh3, d_final_g, df3, d_gt3 = _loss_head(h3, final_norm_g.reshape(1, D), target, saved3[5], gt3, 0.5,
                                                       "loss_head")

    (dh2, d_sh3, d_sc3, d_gain3, dmo, d_gt2), (ex_d2, ex_gu2) = _ffn_backward(
        dh3, df3, saved3, ffn2_norm_g, sc3, wg2, wu2, wd2, core_idx, "ffn2", prev=(saved2[12], gt2, 1.0))
    (dh1, d_sh2, d_sc2, d_gain2, df1, d_gt1), ex_mix, small_mix = _mixer_backward(
        dh2, dmo, saved2, mix_norm_g, sc2, win, wout, conv_w, conv_ln_g, conv_ln_b, attn_out_g, conv_out_g, core_idx,
        prev=(f1, gt1, 0.5))
    d_cb, d_lng, d_lnb, d_attn_g, d_gco, d_cw = small_mix
    (dh0, d_sh1, d_sc1, d_gain1), (ex_d1, ex_gu1) = _ffn_backward(
        dh1, df1, saved1, ffn1_norm_g, sc1, wg1, wu1, wd1, core_idx, "ffn1", last=True)

    dmod = jnp.concatenate([d_sh1, d_sc1, d_gt1, d_sh2, d_sc2, d_gt2, d_sh3, d_sc3, d_gt3], axis=1)
    small = [d_gain1, d_gain2, d_gain3, d_final_g, d_cb, d_lng, d_lnb, d_attn_g, d_gco,
             d_cw[:CONV_KERNEL].reshape(1, CONV_KERNEL * CONV_WIDTH), loss_part]
    pk1 = jnp.concatenate([dmod] + small, axis=1)
    n1_ = -(-pk1.shape[1] // 1024) * 1024
    gathered = _all_gather8(_pad_cols(pk1, n1_).reshape(8, n1_ // 8), "gather_small").reshape(N_DEV, n1_)
    ex_gu1 = _rs_send_chunks(ex_gu1, [gathered], core_idx, "ffn1_gu")
    tot = _sum_rows(gathered, "sum_small")
    off = [0]

    def take(nel):
        out = tot[:, off[0]:off[0] + nel]
        off[0] += nel
        return out

    g_b_ada = take(9 * D)
    g_ffn1_norm, g_mix_norm, g_ffn2_norm, g_final = take(D), take(D), take(D), take(D)
    g_cb, g_lng, g_lnb, g_attn_g, g_gco = take(512), take(512), take(512), take(512), take(512)
    g_cw_full = take(CONV_KERNEL * CONV_WIDTH).reshape(CONV_KERNEL, CONV_WIDTH)
    loss = take(128)[0, 0]
    g_cw = lax.dynamic_slice(g_cw_full, (0, chip * 128), (CONV_KERNEL, 128))

    dmod_shard = lax.dynamic_slice(gathered[:, :9 * D], (0, chip * nmod), (N_DEV, nmod))
    dmod16 = jnp.pad(dmod_shard, ((0, N_DEV), (0, 0)))
    c_t16 = jnp.pad(c_all.T, ((0, 0), (0, N_DEV)))
    g_w_ada = _ada_wgrad(c_t16, dmod16, "ada_wgrad")

    names = ["w_ada", "b_ada", "ffn1_norm_g", "ffn1_w_gate", "ffn1_w_up", "ffn1_w_down", "mix_norm_g", "w_in",
             "conv_dw_w", "conv_dw_b", "conv_ln_g", "conv_ln_b", "attn_out_g", "conv_out_g", "w_out", "ffn2_norm_g",
             "ffn2_w_gate", "ffn2_w_up", "ffn2_w_down", "final_norm_g"]
    weights = dict(zip(names, [w_ada, b_ada, ffn1_norm_g, ffn1_w_gate, ffn1_w_up, ffn1_w_down, mix_norm_g, w_in,
                               conv_dw_w, conv_dw_b, conv_ln_g, conv_ln_b, attn_out_g, conv_out_g, w_out,
                               ffn2_norm_g, ffn2_w_gate, ffn2_w_up, ffn2_w_down, final_norm_g]))
    ms = dict(zip(names, [m_w_ada, m_b_ada, m_ffn1_norm_g, m_ffn1_w_gate, m_ffn1_w_up, m_ffn1_w_down, m_mix_norm_g,
                          m_w_in, m_conv_dw_w, m_conv_dw_b, m_conv_ln_g, m_conv_ln_b, m_attn_out_g, m_conv_out_g,
                          m_w_out, m_ffn2_norm_g, m_ffn2_w_gate, m_ffn2_w_up, m_ffn2_w_down, m_final_norm_g]))
    vs = dict(zip(names, [v_w_ada, v_b_ada, v_ffn1_norm_g, v_ffn1_w_gate, v_ffn1_w_up, v_ffn1_w_down, v_mix_norm_g,
                          v_w_in, v_conv_dw_w, v_conv_dw_b, v_conv_ln_g, v_conv_ln_b, v_attn_out_g, v_conv_out_g,
                          v_w_out, v_ffn2_norm_g, v_ffn2_w_gate, v_ffn2_w_up, v_ffn2_w_down, v_final_norm_g]))
    grads, deltas, new_ms, new_vs = {}, {}, {}, {}

    def adamw_big(nm, g2d, deps=(), transposed=False):
        shape = weights[nm].shape
        two_d = (shape[-2], shape[-1])

        def view(t):
            return t.reshape(two_d).T if transposed else t.reshape(two_d)

        d_, m_, v_ = _adamw(view(weights[nm]), g2d, view(ms[nm]), view(vs[nm]), f"adamw_{nm}", deps=deps)
        grads[nm], deltas[nm], new_ms[nm], new_vs[nm] = (
            (t.T if transposed else t).reshape(shape) for t in (g2d, d_, m_, v_))
        return d_

    d_ada = adamw_big("w_ada", g_w_ada, deps=[ex_gu1.token])
    small_grads = {"b_ada": g_b_ada, "ffn1_norm_g": g_ffn1_norm, "mix_norm_g": g_mix_norm, "conv_dw_w": g_cw,
                   "conv_dw_b": g_cb, "conv_ln_g": g_lng, "conv_ln_b": g_lnb, "attn_out_g": g_attn_g,
                   "conv_out_g": g_gco, "ffn2_norm_g": g_ffn2_norm, "final_norm_g": g_final}
    small_names = [nm for nm in names if nm in small_grads]

    def pack_small(arrs):
        flat = jnp.concatenate([arrs[nm].reshape(1, -1) for nm in small_names], axis=1)
        npad = -(-flat.shape[1] // 1024) * 1024
        return _pad_cols(flat, npad).reshape(8, npad // 8)

    d_s, m_s, v_s = _adamw(pack_small(weights), pack_small(small_grads), pack_small(ms), pack_small(vs),
                           "adamw_small")
    pos = 0
    for nm in small_names:
        shape, nel = weights[nm].shape, weights[nm].size
        grads[nm] = small_grads[nm].reshape(shape)
        deltas[nm], new_ms[nm], new_vs[nm] = (t.reshape(1, -1)[:, pos:pos + nel].reshape(shape)
                                              for t in (d_s, m_s, v_s))
        pos += nel

    ex_d2 = _rs_send_share(ex_d2, [d_ada, d_s], chip_core, "ffn2_d")
    ex_gu2 = _rs_send_share(ex_gu2, [ex_d2.token], chip_core, "ffn2_gu")
    ex_mix = _rs_send_share(ex_mix, [ex_gu2.token], chip_core, "mix")
    ex_d1 = _rs_send_share(ex_d1, [ex_mix.token], chip_core, "ffn1_d")
    (g_wd2,) = _rs_finish(ex_d2, [ex_d1.token])
    last = [adamw_big("ffn2_w_down", g_wd2)]
    g_wg2, g_wu2 = _rs_finish(ex_gu2, last)
    last = [adamw_big("ffn2_w_gate", g_wg2, transposed=True), adamw_big("ffn2_w_up", g_wu2, transposed=True)]
    g_win, g_wout = _rs_finish(ex_mix, last)
    last = [adamw_big("w_in", g_win), adamw_big("w_out", g_wout)]
    (g_wd1,) = _rs_finish(ex_d1, last)
    last = [adamw_big("ffn1_w_down", g_wd1)]
    ex_gu1 = _rs_send_share(ex_gu1, last, chip_core, "ffn1_gu")
    g_wg1, g_wu1 = _rs_finish(ex_gu1, [])
    adamw_big("ffn1_w_gate", g_wg1, transposed=True)
    adamw_big("ffn1_w_up", g_wu1, transposed=True)

    return (loss, dh0[None], *[grads[nm] for nm in names], *[deltas[nm] for nm in names],
            *[new_ms[nm] for nm in names], *[new_vs[nm] for nm in names])
```

```python
import jax
import jax.numpy as jnp
import numpy as np
from jax import lax
from jax.experimental import pallas as pl
from jax.experimental.pallas import tpu as pltpu

F32 = jnp.float32
BF16 = jnp.bfloat16
MESH = pl.DeviceIdType.MESH

RMS_EPS = 1e-6
LN_EPS = 1e-5
HEAD_DIM = 64
ATTN_WIDTH = 512
CONV_WIDTH = 512
ATTN_BLOCK = 128
DILATIONS = (1, 4, 16)
SUPER_ROWS = ATTN_BLOCK * 16
ROPE_THETA = 10000.0
CONV_KERNEL = 31
HALO = 32
N_CHIPS = 4
N_DEV = 8
ADAM_LR, ADAM_B1, ADAM_B2, ADAM_EPS, ADAM_WD, ADAM_STEP = 0.001, 0.9, 0.999, 1e-08, 0.01, 10
VMEM_LIMIT_BYTES = 48 * 1024 * 1024
VMEM_LIMIT_STREAMING = 62 * 1024 * 1024
NEG = -1e30

NT = (((1,), (1,)), ((), ()))
TN = (((0,), (0,)), ((), ()))

ANY_SPEC = pl.BlockSpec(memory_space=pl.ANY)
HBM_SPEC = pl.BlockSpec(memory_space=pltpu.HBM)
SEM_SPEC = pl.BlockSpec(memory_space=pltpu.SEMAPHORE)
DATAFLOW = pltpu.SideEffectType.DATAFLOW_SIDE_EFFECTING
XY_FLIPS = ((0, 1), (1, 0), (1, 1))


def _params(sem=None, streaming=False):
    return pltpu.CompilerParams(dimension_semantics=sem,
                                vmem_limit_bytes=VMEM_LIMIT_STREAMING if streaming else VMEM_LIMIT_BYTES)


def _row_tile(rows, want):
    t = min(rows, want)
    assert rows % t == 0
    return t


def _sigmoid(x):
    return 1.0 / (1.0 + jnp.exp(-x))


def _deps(deps):
    return [d for d in deps if d is not None]


def _blind_to(body, n_in, n_dep):
    def wrapped(*refs):
        return body(*refs[:n_in], *refs[n_in + n_dep:])
    return wrapped


def _vec_spec(d, ngrid):
    if ngrid == 1:
        return pl.BlockSpec((1, d), lambda i: (0, 0))
    return pl.BlockSpec((1, d), lambda i, j: (0, 0))


def _norm_mod(h, gain, sc, sh, name, deps=()):
    S, D = h.shape
    tr = _row_tile(S, 512)
    deps = _deps(deps)

    def body(h_ref, g_ref, sc_ref, sh_ref, n_ref):
        x = h_ref[...]
        r = lax.rsqrt(jnp.mean(x * x, axis=-1, keepdims=True) + RMS_EPS)
        y = (x * r) * g_ref[...]
        n_ref[...] = (y * (1.0 + sc_ref[...]) + sh_ref[...]).astype(BF16)

    row = pl.BlockSpec((tr, D), lambda i: (i, 0))
    return pl.pallas_call(
        _blind_to(body, 4, len(deps)), name=name, grid=(S // tr,),
        in_specs=[row, _vec_spec(D, 1), _vec_spec(D, 1), _vec_spec(D, 1)] + [ANY_SPEC] * len(deps),
        out_specs=row, out_shape=jax.ShapeDtypeStruct((S, D), BF16),
        compiler_params=_params(("parallel",)),
    )(h, gain, sc, sh, *deps)


def _norm_mod_bwd(dn, h_in, gain, sc, dh_out, name, prev=None, deps=()):
    S, D = h_in.shape
    tr = _row_tile(S, 512)
    deps = _deps(deps)
    n_in = 5 if prev is None else 7

    def body(*refs):
        dn_ref, h_ref, g_ref, sc_ref, dho_ref = refs[:5]
        dh_ref, dsh_ref, dsc_ref, dg_ref = refs[n_in:n_in + 4]

        @pl.when(pl.program_id(0) == 0)
        def _():
            for ref in refs[n_in + 1:n_in + 4] + refs[n_in + 5:]:
                ref[...] = jnp.zeros_like(ref)

        x = h_ref[...]
        dn_ = dn_ref[...]
        g = g_ref[...]
        one_sc = 1.0 + sc_ref[...]
        r = lax.rsqrt(jnp.mean(x * x, axis=-1, keepdims=True) + RMS_EPS)
        xh = x * r
        dsh_ref[...] += jnp.sum(dn_, axis=0, keepdims=True)
        dsc_ref[...] += jnp.sum(dn_ * (xh * g), axis=0, keepdims=True)
        dg_ref[...] += jnp.sum(dn_ * one_sc * xh, axis=0, keepdims=True)
        dxh = dn_ * (g * one_sc)
        dh = dho_ref[...] + r * (dxh - xh * jnp.mean(dxh * xh, axis=-1, keepdims=True))
        dh_ref[...] = dh
        if prev is not None:
            _gate_back(dh, refs[5], refs[6], prev[2], refs[n_in + 4], refs[n_in + 5])

    row = pl.BlockSpec((tr, D), lambda i: (i, 0))
    vec = _vec_spec(D, 1)
    extra_in = [] if prev is None else [row, vec]
    extra_out = [] if prev is None else [row, vec]
    extra_shape = [] if prev is None else [jax.ShapeDtypeStruct((S, D), BF16), jax.ShapeDtypeStruct((1, D), F32)]
    return pl.pallas_call(
        _blind_to(body, n_in, len(deps)), name=name, grid=(S // tr,),
        in_specs=[row, row, vec, vec, row] + extra_in + [ANY_SPEC] * len(deps),
        out_specs=[row, vec, vec, vec] + extra_out,
        out_shape=[jax.ShapeDtypeStruct((S, D), F32)] + [jax.ShapeDtypeStruct((1, D), F32)] * 3 + extra_shape,
        compiler_params=_params(("arbitrary",)),
    )(dn, h_in, gain, sc, dh_out, *([] if prev is None else prev[:2]), *deps)


def _gate_back(dh, f_ref, gate_ref, coef, df_ref, dgate_ref):
    df_ref[...] = ((coef * gate_ref[...]) * dh).astype(BF16)
    dgate_ref[...] += jnp.sum(coef * dh * f_ref[...].astype(F32), axis=0, keepdims=True)


def _loss_head(h, gain, target, f, gate, coef, name):
    S, D = h.shape
    tr = _row_tile(S, 512)

    def body(h_ref, g_ref, t_ref, f_ref, gate_ref, loss_ref, dh_ref, dg_ref, df_ref, dgate_ref):
        @pl.when(pl.program_id(0) == 0)
        def _():
            loss_ref[...] = jnp.zeros_like(loss_ref)
            dg_ref[...] = jnp.zeros_like(dg_ref)
            dgate_ref[...] = jnp.zeros_like(dgate_ref)

        x = h_ref[...]
        g = g_ref[...]
        r = lax.rsqrt(jnp.mean(x * x, axis=-1, keepdims=True) + RMS_EPS)
        xh = x * r
        err = xh * g - t_ref[...]
        part = 0.5 * jnp.sum(jnp.mean(err * err, axis=-1, keepdims=True), axis=0, keepdims=True)
        loss_ref[...] += jnp.broadcast_to(part, loss_ref.shape)
        dy = err * (1.0 / D)
        dg_ref[...] += jnp.sum(dy * xh, axis=0, keepdims=True)
        dxh = dy * g
        dh = r * (dxh - xh * jnp.mean(dxh * xh, axis=-1, keepdims=True))
        dh_ref[...] = dh
        _gate_back(dh, f_ref, gate_ref, coef, df_ref, dgate_ref)

    row = pl.BlockSpec((tr, D), lambda i: (i, 0))
    vec = _vec_spec(D, 1)
    return pl.pallas_call(
        body, name=name, grid=(S // tr,),
        in_specs=[row, vec, row, row, vec],
        out_specs=[pl.BlockSpec((1, 128), lambda i: (0, 0)), row, vec, row, vec],
        out_shape=[jax.ShapeDtypeStruct((1, 128), F32), jax.ShapeDtypeStruct((S, D), F32),
                   jax.ShapeDtypeStruct((1, D), F32), jax.ShapeDtypeStruct((S, D), BF16),
                   jax.ShapeDtypeStruct((1, D), F32)],
        compiler_params=_params(("arbitrary",)),
    )(h, gain, target, f, gate)


def _ffn_gate_up(n, wg_t, wu_t, name):
    S, D = n.shape
    nk, w, _ = wg_t.shape
    tm = _row_tile(S, 512)

    def body(n_ref, wg_ref, wu_ref, dga_ref, dup_ref, act_ref):
        x = n_ref[...]
        ga = lax.dot_general(x, wg_ref[...], NT, preferred_element_type=F32)
        up = lax.dot_general(x, wu_ref[...], NT, preferred_element_type=F32)
        sig = _sigmoid(ga)
        silu = ga * sig
        dga_ref[...] = (up * (sig * (1.0 + ga * (1.0 - sig)))).astype(BF16)
        dup_ref[...] = silu.astype(BF16)
        act_ref[...] = (silu * up).astype(BF16)

    wspec = pl.BlockSpec((None, w, D), lambda k, m: (k, 0, 0))
    ospec = pl.BlockSpec((None, tm, w), lambda k, m: (k, m, 0))
    out = jax.ShapeDtypeStruct((nk, S, w), BF16)
    return pl.pallas_call(
        body, name=name, grid=(nk, S // tm),
        in_specs=[pl.BlockSpec((tm, D), lambda k, m: (m, 0)), wspec, wspec],
        out_specs=[ospec, ospec, ospec], out_shape=[out, out, out],
        compiler_params=_params(("parallel", "parallel")),
    )(n, wg_t, wu_t)


def _mm_residual(lhs, w, h_in, gvec, coef, name, next_norm=None):
    nk, S, kc = lhs.shape
    D = w.shape[2]
    tm = _row_tile(S, 512)
    n_in = 4 if next_norm is None else 7

    def body(*refs):
        l_ref, w_ref, h_ref, g_ref = refs[:4]
        ho_ref, f_ref = refs[n_in:n_in + 2]
        acc_ref = refs[-1]
        k = pl.program_id(1)

        @pl.when(k == 0)
        def _():
            acc_ref[...] = jnp.zeros_like(acc_ref)

        acc_ref[...] += jnp.dot(l_ref[...], w_ref[...], preferred_element_type=F32)

        @pl.when(k == nk - 1)
        def _():
            f = acc_ref[...]
            f_ref[...] = f.astype(BF16)
            x = h_ref[...] + (coef * g_ref[...]) * f
            ho_ref[...] = x
            if next_norm is not None:
                ng_ref, sc_ref, sh_ref = refs[4:7]
                r = lax.rsqrt(jnp.mean(x * x, axis=-1, keepdims=True) + RMS_EPS)
                y = (x * r) * ng_ref[...]
                refs[n_in + 2][...] = (y * (1.0 + sc_ref[...]) + sh_ref[...]).astype(BF16)

    row = pl.BlockSpec((tm, D), lambda m, k: (m, 0))
    vec = _vec_spec(D, 2)
    with_n = next_norm is not None
    return pl.pallas_call(
        body, name=name, grid=(S // tm, nk),
        in_specs=[pl.BlockSpec((None, tm, kc), lambda m, k: (k, m, 0)),
                  pl.BlockSpec((None, kc, D), lambda m, k: (k, 0, 0)), row, vec] + [vec] * (3 * with_n),
        out_specs=[row, row] + [row] * with_n,
        out_shape=[jax.ShapeDtypeStruct((S, D), F32), jax.ShapeDtypeStruct((S, D), BF16)]
        + [jax.ShapeDtypeStruct((S, D), BF16)] * with_n,
        scratch_shapes=[pltpu.VMEM((tm, D), F32)],
        compiler_params=_params(("parallel", "arbitrary")),
    )(lhs, w, h_in, gvec, *(next_norm or ()))


def _mm_cols(n, w, name):
    S, D = n.shape
    nk, _, wd = w.shape
    assert wd % 128 == 0
    tm = _row_tile(S, 512)

    def body(n_ref, w_ref, o_ref):
        o_ref[...] = jnp.dot(n_ref[...], w_ref[...], preferred_element_type=F32)

    return pl.pallas_call(
        body, name=name, grid=(nk, S // tm),
        in_specs=[pl.BlockSpec((tm, D), lambda k, m: (m, 0)), pl.BlockSpec((None, D, wd), lambda k, m: (k, 0, 0))],
        out_specs=pl.BlockSpec((tm, wd), lambda k, m: (m, k)),
        out_shape=jax.ShapeDtypeStruct((S, nk * wd), F32),
        compiler_params=_params(("parallel", "parallel")),
    )(n, w)


def _ffn_dact(df, wd, fga, fup, name, deps=()):
    S, D = df.shape
    nk, w, _ = wd.shape
    tm = _row_tile(S, 512)
    deps = _deps(deps)

    def body(df_ref, wd_ref, fga_ref, fup_ref, dga_ref, dup_ref):
        dact = lax.dot_general(df_ref[...], wd_ref[...], NT, preferred_element_type=F32)
        dga_ref[...] = (dact * fga_ref[...].astype(F32)).astype(BF16)
        dup_ref[...] = (dact * fup_ref[...].astype(F32)).astype(BF16)

    cspec = pl.BlockSpec((None, tm, w), lambda k, m: (k, m, 0))
    out = jax.ShapeDtypeStruct((nk, S, w), BF16)
    return pl.pallas_call(
        _blind_to(body, 4, len(deps)), name=name, grid=(nk, S // tm),
        in_specs=[pl.BlockSpec((tm, D), lambda k, m: (m, 0)), pl.BlockSpec((None, w, D), lambda k, m: (k, 0, 0)),
                  cspec, cspec] + [ANY_SPEC] * len(deps),
        out_specs=[cspec, cspec], out_shape=[out, out],
        compiler_params=_params(("parallel", "parallel")),
    )(df, wd, fga, fup, *deps)


def _mm_nt(d, w, name):
    S, K = d.shape
    N = w.shape[0]
    tm = _row_tile(S, 512)

    def body(d_ref, w_ref, o_ref):
        o_ref[...] = lax.dot_general(d_ref[...], w_ref[...], NT, preferred_element_type=F32)

    return pl.pallas_call(
        body, name=name, grid=(S // tm,),
        in_specs=[pl.BlockSpec((tm, K), lambda m: (m, 0)), pl.BlockSpec((N, K), lambda m: (0, 0))],
        out_specs=pl.BlockSpec((tm, N), lambda m: (m, 0)),
        out_shape=jax.ShapeDtypeStruct((S, N), F32),
        compiler_params=_params(("parallel",)),
    )(d, w)


def _mm_reduce(lhs_list, w_list, chunked3d, w_is_kd, name, deps=()):
    nk = w_list[0].shape[0]
    kc, D = w_list[0].shape[1:] if w_is_kd else w_list[0].shape[:0:-1]
    S = lhs_list[0].shape[1] if chunked3d else lhs_list[0].shape[0]
    tm = _row_tile(S, 512)
    npair = len(lhs_list)
    deps = _deps(deps)

    def body(*refs):
        l_refs, w_refs = refs[:npair], refs[npair:2 * npair]
        o_ref, acc_ref = refs[2 * npair], refs[2 * npair + 1]
        k = pl.program_id(1)

        @pl.when(k == 0)
        def _():
            acc_ref[...] = jnp.zeros_like(acc_ref)

        for l_ref, w_ref in zip(l_refs, w_refs):
            if w_is_kd:
                acc_ref[...] += jnp.dot(l_ref[...], w_ref[...], preferred_element_type=F32)
            else:
                acc_ref[...] += lax.dot_general(l_ref[...], w_ref[...], NT, preferred_element_type=F32)

        @pl.when(k == nk - 1)
        def _():
            o_ref[...] = acc_ref[...]

    if chunked3d:
        lspec = pl.BlockSpec((None, tm, kc), lambda m, k: (k, m, 0))
    else:
        lspec = pl.BlockSpec((tm, kc), lambda m, k: (m, k))
    wspec = pl.BlockSpec((None,) + tuple(w_list[0].shape[1:]), lambda m, k: (k, 0, 0))
    return pl.pallas_call(
        _blind_to(body, 2 * npair, len(deps)), name=name, grid=(S // tm, nk),
        in_specs=[lspec] * npair + [wspec] * npair + [ANY_SPEC] * len(deps),
        out_specs=pl.BlockSpec((tm, D), lambda m, k: (m, 0)),
        out_shape=jax.ShapeDtypeStruct((S, D), F32),
        scratch_shapes=[pltpu.VMEM((tm, D), F32)],
        compiler_params=_params(("parallel", "arbitrary")),
    )(*lhs_list, *w_list, *deps)


def _wgrad_chunk_lhs(lhs_list, rhs, name, deps=()):
    nk, S, w = lhs_list[0].shape
    D = rhs.shape[1]
    ts = _row_tile(S, 512)
    ns = S // ts
    nl = len(lhs_list)
    deps = _deps(deps)

    def body(*refs):
        l_refs, r_ref = refs[:nl], refs[nl]
        o_refs, acc_refs = refs[nl + 1:2 * nl + 1], refs[2 * nl + 1:]
        s = pl.program_id(1)

        @pl.when(s == 0)
        def _():
            for acc_ref in acc_refs:
                acc_ref[...] = jnp.zeros_like(acc_ref)

        x = r_ref[...]
        for l_ref, acc_ref in zip(l_refs, acc_refs):
            acc_ref[...] += lax.dot_general(l_ref[...], x, TN, preferred_element_type=F32)

        @pl.when(s == ns - 1)
        def _():
            for o_ref, acc_ref in zip(o_refs, acc_refs):
                o_ref[...] = acc_ref[...]

    return pl.pallas_call(
        _blind_to(body, nl + 1, len(deps)), name=name, grid=(nk, ns),
        in_specs=[pl.BlockSpec((None, ts, w), lambda k, s: (k, s, 0))] * nl
        + [pl.BlockSpec((ts, D), lambda k, s: (s, 0))] + [ANY_SPEC] * len(deps),
        out_specs=[pl.BlockSpec((None, w, D), lambda k, s: (k, 0, 0))] * nl,
        out_shape=[jax.ShapeDtypeStruct((nk, w, D), F32)] * nl,
        scratch_shapes=[pltpu.VMEM((w, D), F32)] * nl,
        compiler_params=_params(("parallel", "arbitrary")),
    )(*lhs_list, rhs, *deps)


def _wgrad_chunk_rhs(lhs, rhs_list, nk, chunked3d, name, deps=()):
    S, D = lhs.shape
    w = rhs_list[0].shape[2] if chunked3d else rhs_list[0].shape[1] // nk
    ts = _row_tile(S, 512)
    ns = S // ts
    nr = len(rhs_list)
    deps = _deps(deps)

    def body(*refs):
        l_ref, r_refs = refs[0], refs[1:1 + nr]
        o_refs, acc_refs = refs[1 + nr:1 + 2 * nr], refs[1 + 2 * nr:]
        s = pl.program_id(1)

        @pl.when(s == 0)
        def _():
            for acc_ref in acc_refs:
                acc_ref[...] = jnp.zeros_like(acc_ref)

        x = l_ref[...]
        for r_ref, acc_ref in zip(r_refs, acc_refs):
            acc_ref[...] += lax.dot_general(x, r_ref[...], TN, preferred_element_type=F32)

        @pl.when(s == ns - 1)
        def _():
            for o_ref, acc_ref in zip(o_refs, acc_refs):
                o_ref[...] = acc_ref[...]

    if chunked3d:
        rspec = pl.BlockSpec((None, ts, w), lambda k, s: (k, s, 0))
    else:
        rspec = pl.BlockSpec((ts, w), lambda k, s: (s, k))
    ospec = pl.BlockSpec((None, D, w), lambda k, s: (k, 0, 0))
    return pl.pallas_call(
        _blind_to(body, 1 + nr, len(deps)), name=name, grid=(nk, ns),
        in_specs=[pl.BlockSpec((ts, D), lambda k, s: (s, 0))] + [rspec] * nr + [ANY_SPEC] * len(deps),
        out_specs=[ospec] * nr,
        out_shape=[jax.ShapeDtypeStruct((nk, D, w), F32)] * nr,
        scratch_shapes=[pltpu.VMEM((D, w), F32)] * nr,
        compiler_params=_params(("parallel", "arbitrary")),
    )(lhs, *rhs_list, *deps)


def _rope_tables(S):
    pos = np.arange(S, dtype=np.float32)
    inv_freq = (ROPE_THETA ** (-np.arange(0, HEAD_DIM, 2, dtype=np.float32) / HEAD_DIM)).astype(np.float32)
    ang = (pos[:, None] * inv_freq[None, :]).astype(np.float64)
    cos, sin = np.cos(ang).astype(np.float32), np.sin(ang).astype(np.float32)
    cos2 = np.concatenate([cos, cos, cos, cos], axis=1)
    sin2 = np.concatenate([-sin, sin, -sin, sin], axis=1)
    return jnp.asarray(cos2), jnp.asarray(sin2)


def _rotate(t, cos, sin_signed):
    half = HEAD_DIM // 2
    lane = lax.broadcasted_iota(jnp.int32, t.shape, 1)
    first = (lane % HEAD_DIM) < half
    partner = jnp.where(first, pltpu.roll(t, 128 - half, 1), pltpu.roll(t, half, 1))
    return t * cos + partner * sin_signed


def _qkv_rope(proj, cos, sin, name):
    S = proj.shape[0]
    A = ATTN_WIDTH
    tr = _row_tile(S, 512)
    nb = A // 128
    scale = HEAD_DIM ** -0.5

    def body(q_ref, k_ref, v_ref, c_ref, s_ref, qo_ref, ko_ref, vo_ref):
        c, s = c_ref[...], s_ref[...]
        qo_ref[...] = _rotate(q_ref[...], c, s) * scale
        ko_ref[...] = _rotate(k_ref[...], c, s)
        vo_ref[...] = v_ref[...]

    def col(off):
        return pl.BlockSpec((tr, 128), lambda i, j: (i, off + j))

    tab = pl.BlockSpec((tr, 128), lambda i, j: (i, 0))
    out = jax.ShapeDtypeStruct((S, A), F32)
    return pl.pallas_call(
        body, name=name, grid=(S // tr, nb),
        in_specs=[col(0), col(nb), col(2 * nb), tab, tab],
        out_specs=[col(0), col(0), col(0)], out_shape=[out, out, out],
        compiler_params=_params(("parallel", "parallel")),
    )(proj, proj, proj, cos, sin)


def _band_mask(T, has_prev):
    qi = lax.broadcasted_iota(jnp.int32, (T, 2 * T), 0)
    kj = lax.broadcasted_iota(jnp.int32, (T, 2 * T), 1)
    return ((kj < T) & (kj >= qi) & has_prev) | ((kj >= T) & (kj - T <= qi))


def _stack_heads(x, head0):
    zero = jnp.zeros_like(x)
    return jnp.concatenate([jnp.where(head0, x, zero), jnp.where(head0, zero, x)], axis=0)


def _branch_blocks(rows, dilation):
    T = min(ATTN_BLOCK, rows // dilation)
    return T, rows // T


def _block_rows(base, T, dilation):
    if dilation == 1:
        return pl.ds(pl.multiple_of(base, T), T)
    return pl.ds(base, T, stride=dilation)


def _attn_fwd(q, k, v, name):
    S, A = q.shape
    sup = min(S, SUPER_ROWS)
    nd = len(DILATIONS)
    assert S % sup == 0

    def body(q_ref, k_ref, v_ref, attn_ref, lse_ref, acc_s, m_s, l_s):
        lane = lax.broadcasted_iota(jnp.int32, (1, 128), 1)
        head0 = lane < HEAD_DIM

        def supertile(st, carry):
            row0 = st * sup
            for di, dil in enumerate(DILATIONS):
                T, nblk = _branch_blocks(sup, dil)
                span = T * dil
                assert T == ATTN_BLOCK or sup == S

                def block(idx, c2, di=di, dil=dil, T=T, span=span):
                    r = idx % dil
                    loc = (idx // dil) * span + r
                    base = row0 + loc
                    rows = _block_rows(base, T, dil)
                    prev = _block_rows(jnp.maximum(base - span, r), T, dil)
                    qb = q_ref[rows, :].astype(BF16)
                    k2 = jnp.concatenate([k_ref[prev, :], k_ref[rows, :]], axis=0).astype(BF16)
                    v2 = jnp.concatenate([v_ref[prev, :], v_ref[rows, :]], axis=0).astype(BF16)
                    valid = _band_mask(T, base >= span)
                    q2 = _stack_heads(qb, head0)
                    s = lax.dot_general(q2, k2, NT, preferred_element_type=F32)
                    s = jnp.where(jnp.concatenate([valid, valid], axis=0), s, NEG)
                    m = jnp.max(s, axis=-1, keepdims=True)
                    p = jnp.exp(s - m)
                    l = jnp.sum(p, axis=-1, keepdims=True)
                    acc = jnp.dot(p.astype(BF16), v2, preferred_element_type=F32)
                    lrows = _block_rows(di * sup + loc, T, dil)
                    acc_s[lrows, :] = jnp.where(head0, acc[:T], acc[T:])
                    m_s[lrows, :] = jnp.where(head0, m[:T], m[T:])
                    l_s[lrows, :] = jnp.where(head0, l[:T], l[T:])
                    return c2

                lax.fori_loop(0, nblk, block, 0, unroll=8)

            chunk = min(sup, 256)

            def merge(ci, c2):
                lr = [pl.ds(pl.multiple_of(di * sup + ci * chunk, chunk), chunk) for di in range(nd)]
                gr = pl.ds(pl.multiple_of(row0 + ci * chunk, chunk), chunk)
                m0, m1, m2 = m_s[lr[0], :], m_s[lr[1], :], m_s[lr[2], :]
                mm = jnp.maximum(jnp.maximum(m0, m1), m2)
                w0, w1, w2 = jnp.exp(m0 - mm), jnp.exp(m1 - mm), jnp.exp(m2 - mm)
                den = (w0 * l_s[lr[0], :] + w1 * l_s[lr[1], :]) + w2 * l_s[lr[2], :]
                num = (w0 * acc_s[lr[0], :] + w1 * acc_s[lr[1], :]) + w2 * acc_s[lr[2], :]
                attn_ref[gr, :] = num / den
                lse_ref[gr, :] = mm + jnp.log(den)
                return c2

            lax.fori_loop(0, sup // chunk, merge, 0)
            return carry

        lax.fori_loop(0, S // sup, supertile, 0)

    blk = pl.BlockSpec((S, 128), lambda j: (0, j))
    out = jax.ShapeDtypeStruct((S, A), F32)
    return pl.pallas_call(
        body, name=name, grid=(A // 128,),
        in_specs=[blk, blk, blk], out_specs=[blk, blk], out_shape=[out, out],
        scratch_shapes=[pltpu.VMEM((nd * sup, 128), F32)] * 3,
        compiler_params=_params(("parallel",)),
    )(q, k, v)


def _attn_out_bwd(dy, attn, gain, name):
    S, A = attn.shape
    tr = _row_tile(S, 256)

    def body(dy_ref, a_ref, g_ref, da_ref, dl_ref, dg_ref):
        @pl.when(pl.program_id(0) == 0)
        def _():
            dg_ref[...] = jnp.zeros_like(dg_ref)

        x = a_ref[...]
        dy_ = dy_ref[...]
        r = lax.rsqrt(jnp.mean(x * x, axis=-1, keepdims=True) + RMS_EPS)
        xh = x * r
        dg_ref[...] += jnp.sum(dy_ * xh, axis=0, keepdims=True)
        dxh = dy_ * g_ref[...]
        dx = r * (dxh - xh * jnp.mean(dxh * xh, axis=-1, keepdims=True))
        da_ref[...] = dx
        prod = dx * x
        hi = lax.broadcasted_iota(jnp.int32, (A, A), 0) // HEAD_DIM
        hj = lax.broadcasted_iota(jnp.int32, (A, A), 1) // HEAD_DIM
        same_head = (hi == hj).astype(F32)
        dl_ref[...] = jnp.dot(prod, same_head, preferred_element_type=F32, precision=lax.Precision.HIGHEST)

    row = pl.BlockSpec((tr, A), lambda i: (i, 0))
    vec = _vec_spec(A, 1)
    return pl.pallas_call(
        body, name=name, grid=(S // tr,),
        in_specs=[row, row, vec], out_specs=[row, row, vec],
        out_shape=[jax.ShapeDtypeStruct((S, A), F32), jax.ShapeDtypeStruct((S, A), F32),
                   jax.ShapeDtypeStruct((1, A), F32)],
        compiler_params=_params(("arbitrary",)),
    )(dy, attn, gain)


def _attn_bwd(q, k, v, da, lse, delta, name):
    S, A = q.shape

    def body(q_ref, k_ref, v_ref, da_ref, lse_ref, dl_ref, dq_ref, dk_ref, dv_ref):
        lane = lax.broadcasted_iota(jnp.int32, (1, 128), 1)
        head0 = lane < HEAD_DIM
        dq_ref[...] = jnp.zeros_like(dq_ref)
        dk_ref[...] = jnp.zeros_like(dk_ref)
        dv_ref[...] = jnp.zeros_like(dv_ref)
        for dil in DILATIONS:
            T, nblk = _branch_blocks(S, dil)
            span = T * dil

            def block(idx, carry, dil=dil, T=T, span=span):
                r = idx % dil
                base = (idx // dil) * span + r
                rows = _block_rows(base, T, dil)
                prev = _block_rows(jnp.maximum(base - span, r), T, dil)
                qb, dab = q_ref[rows, :].astype(BF16), da_ref[rows, :].astype(BF16)
                k2 = jnp.concatenate([k_ref[prev, :], k_ref[rows, :]], axis=0).astype(BF16)
                v2 = jnp.concatenate([v_ref[prev, :], v_ref[rows, :]], axis=0).astype(BF16)
                lse_b, dl_b = lse_ref[rows, :], dl_ref[rows, :]
                valid = _band_mask(T, base >= span)
                valid2 = jnp.concatenate([valid, valid], axis=0)
                q2, da2 = _stack_heads(qb, head0), _stack_heads(dab, head0)
                lse2 = jnp.concatenate([lse_b[:, 0:1], lse_b[:, HEAD_DIM:HEAD_DIM + 1]], axis=0)
                dl2 = jnp.concatenate([dl_b[:, 0:1], dl_b[:, HEAD_DIM:HEAD_DIM + 1]], axis=0)
                s = lax.dot_general(q2, k2, NT, preferred_element_type=F32)
                p = jnp.where(valid2, jnp.exp(s - lse2), 0.0)
                dp = lax.dot_general(da2, v2, NT, preferred_element_type=F32)
                ds = (p * (dp - dl2)).astype(BF16)
                dq2 = jnp.dot(ds, k2, preferred_element_type=F32)
                dk2 = lax.dot_general(ds, q2, TN, preferred_element_type=F32)
                dv2 = lax.dot_general(p.astype(BF16), da2, TN, preferred_element_type=F32)
                dq_ref[rows, :] += jnp.where(head0, dq2[:T], dq2[T:])
                dk_ref[rows, :] += dk2[T:]
                dv_ref[rows, :] += dv2[T:]
                dk_ref[prev, :] += dk2[:T]
                dv_ref[prev, :] += dv2[:T]
                return carry

            lax.fori_loop(0, nblk, block, 0, unroll=8)

    blk = pl.BlockSpec((S, 128), lambda j: (0, j))
    out = jax.ShapeDtypeStruct((S, A), F32)
    return pl.pallas_call(
        body, name=name, grid=(A // 128,),
        in_specs=[blk] * 6, out_specs=[blk] * 3, out_shape=[out] * 3,
        compiler_params=_params(("parallel",)),
    )(q, k, v, da, lse, delta)


SUBLANES = 8
CONV_CHUNK = 64
FIRST_TAP = HALO - (CONV_KERNEL - 1)


def _store_shifted(shift_s, win, rows):
    shift_s[0, pl.ds(0, rows), :] = win
    for b in range(1, SUBLANES):
        shift_s[b, pl.ds(0, rows - SUBLANES), :] = win[b:b + rows - SUBLANES, :]


def _glu_window(a_ref, b_ref, ah_ref, bh_ref, first):
    u0 = a_ref[...] * _sigmoid(b_ref[...])
    u0h = ah_ref[...] * _sigmoid(bh_ref[...])
    u0h = jnp.where(first, jnp.zeros_like(u0h), u0h)
    return jnp.concatenate([u0h, u0], axis=0)


def _conv_norms(u1, lng, lnb):
    mu = jnp.mean(u1, axis=-1, keepdims=True)
    xc = u1 - mu
    rstd = lax.rsqrt(jnp.mean(xc * xc, axis=-1, keepdims=True) + LN_EPS)
    u1h = xc * rstd
    u2 = u1h * lng + lnb
    sig = _sigmoid(u2)
    u3 = u2 * sig
    r = lax.rsqrt(jnp.mean(u3 * u3, axis=-1, keepdims=True) + RMS_EPS)
    return rstd, u1h, u2, sig, u3, r


def _conv_specs(tr, C, col_a, col_b):
    per = tr // HALO

    def tile(col):
        return pl.BlockSpec((tr, C), lambda i: (i, col))

    def halo(col):
        return pl.BlockSpec((HALO, C), lambda i: (jnp.maximum(i * per - 1, 0), col))

    return tile(col_a), tile(col_b), halo(col_a), halo(col_b)


def _mixer_merge(proj, attn, cw, cb, lng, lnb, gat, gco, name):
    S = proj.shape[0]
    C = CONV_WIDTH
    A = attn.shape[1]
    tr = _row_tile(S, 256)

    def body(a_ref, b_ref, ah_ref, bh_ref, at_ref, w_ref, cb_ref, lng_ref, lnb_ref, gat_ref, gco_ref, u1_ref, y_ref,
             shift_s):
        _store_shifted(shift_s, _glu_window(a_ref, b_ref, ah_ref, bh_ref, pl.program_id(0) == 0), tr + HALO)

        def chunk(rc, carry):
            r0 = pl.multiple_of(rc * CONV_CHUNK, CONV_CHUNK)
            for lb in range(C // 128):
                lanes = slice(lb * 128, (lb + 1) * 128)
                acc = jnp.broadcast_to(cb_ref[:, lanes], (CONV_CHUNK, 128))
                for j in range(CONV_KERNEL):
                    a8, b = divmod(FIRST_TAP + j, SUBLANES)
                    acc = acc + w_ref[j:j + 1, lanes] * shift_s[b, pl.ds(r0 + a8 * SUBLANES, CONV_CHUNK), lanes]
                u1_ref[pl.ds(r0, CONV_CHUNK), lanes] = acc
            return carry

        lax.fori_loop(0, tr // CONV_CHUNK, chunk, 0)
        _, _, _, _, u3, r = _conv_norms(u1_ref[...], lng_ref[...], lnb_ref[...])
        y_ref[:, A:] = ((u3 * r) * gco_ref[...]).astype(BF16)
        x = at_ref[...]
        ra = lax.rsqrt(jnp.mean(x * x, axis=-1, keepdims=True) + RMS_EPS)
        y_ref[:, :A] = ((x * ra) * gat_ref[...]).astype(BF16)

    ta, tb, ha, hb = _conv_specs(tr, C, 3, 4)
    row = pl.BlockSpec((tr, C), lambda i: (i, 0))
    vec = _vec_spec(C, 1)
    return pl.pallas_call(
        body, name=name, grid=(S // tr,),
        in_specs=[ta, tb, ha, hb, pl.BlockSpec((tr, A), lambda i: (i, 0)), pl.BlockSpec((HALO, C), lambda i: (0, 0)),
                  vec, vec, vec, _vec_spec(A, 1), vec],
        out_specs=[row, pl.BlockSpec((tr, A + C), lambda i: (i, 0))],
        out_shape=[jax.ShapeDtypeStruct((S, C), F32), jax.ShapeDtypeStruct((S, A + C), BF16)],
        scratch_shapes=[pltpu.VMEM((SUBLANES, tr + HALO, C), F32)],
        compiler_params=_params(("parallel",)),
    )(proj, proj, proj, proj, attn, cw, cb, lng, lnb, gat, gco)


def _conv_bwd_norms(dy, u1, lng, lnb, gco, name):
    S, C = u1.shape
    tr = _row_tile(S, 256)

    def body(dy_ref, u1_ref, lng_ref, lnb_ref, gco_ref, du1_ref, dgco_ref, dlng_ref, dlnb_ref, dcb_ref):
        @pl.when(pl.program_id(0) == 0)
        def _():
            for ref in (dgco_ref, dlng_ref, dlnb_ref, dcb_ref):
                ref[...] = jnp.zeros_like(ref)

        lng = lng_ref[...]
        rstd, u1h, u2, sig, u3, r = _conv_norms(u1_ref[...], lng, lnb_ref[...])
        dy_ = dy_ref[...]
        u3h = u3 * r
        dgco_ref[...] += jnp.sum(dy_ * u3h, axis=0, keepdims=True)
        du3h = dy_ * gco_ref[...]
        du3 = r * (du3h - u3h * jnp.mean(du3h * u3h, axis=-1, keepdims=True))
        du2 = du3 * (sig * (1.0 + u2 * (1.0 - sig)))
        dlng_ref[...] += jnp.sum(du2 * u1h, axis=0, keepdims=True)
        dlnb_ref[...] += jnp.sum(du2, axis=0, keepdims=True)
        du1h = du2 * lng
        du1 = rstd * (du1h - jnp.mean(du1h, axis=-1, keepdims=True)
                      - u1h * jnp.mean(du1h * u1h, axis=-1, keepdims=True))
        du1_ref[...] = du1
        dcb_ref[...] += jnp.sum(du1, axis=0, keepdims=True)

    row = pl.BlockSpec((tr, C), lambda i: (i, 0))
    vec = _vec_spec(C, 1)
    return pl.pallas_call(
        body, name=name, grid=(S // tr,),
        in_specs=[pl.BlockSpec((tr, C), lambda i: (i, 1)), row, vec, vec, vec],
        out_specs=[row, vec, vec, vec, vec],
        out_shape=[jax.ShapeDtypeStruct((S, C), F32)] + [jax.ShapeDtypeStruct((1, C), F32)] * 4,
        compiler_params=_params(("arbitrary",)),
    )(dy, u1, lng, lnb, gco)


def _dproj(du1, proj, cw, dq, dk, dv, cos, sin, name):
    S, C = du1.shape
    A = dq.shape[1]
    tr = _row_tile(S, 256)
    nt = S // tr
    per = tr // HALO
    scale = HEAD_DIM ** -0.5

    def body(du_ref, dun_ref, a_ref, b_ref, ah_ref, bh_ref, w_ref, dq_ref, dk_ref, dv_ref, cos_ref, sin_ref,
             dp_ref, dw_ref, win_s, dwin_s, du0_s, tap_s):
        i = pl.program_id(0)

        @pl.when(i == 0)
        def _():
            dw_ref[...] = jnp.zeros_like(dw_ref)

        _store_shifted(win_s, _glu_window(a_ref, b_ref, ah_ref, bh_ref, i == 0), tr + HALO)
        nxt = jnp.where(i == nt - 1, jnp.zeros_like(dun_ref[...]), dun_ref[...])
        _store_shifted(dwin_s, jnp.concatenate([du_ref[...], nxt], axis=0), tr + HALO)
        tap_s[...] = jnp.zeros_like(tap_s)

        def chunk(rc, carry):
            r0 = pl.multiple_of(rc * CONV_CHUNK, CONV_CHUNK)
            for lb in range(C // 128):
                lanes = slice(lb * 128, (lb + 1) * 128)
                du = du_ref[pl.ds(r0, CONV_CHUNK), lanes]
                acc = jnp.zeros((CONV_CHUNK, 128), F32)
                for j in range(CONV_KERNEL):
                    a8, b = divmod(CONV_KERNEL - 1 - j, SUBLANES)
                    acc = acc + w_ref[j:j + 1, lanes] * dwin_s[b, pl.ds(r0 + a8 * SUBLANES, CONV_CHUNK), lanes]
                    a8, b = divmod(FIRST_TAP + j, SUBLANES)
                    prod = du * win_s[b, pl.ds(r0 + a8 * SUBLANES, CONV_CHUNK), lanes]
                    part = prod[0:SUBLANES]
                    for g in range(1, CONV_CHUNK // SUBLANES):
                        part = part + prod[g * SUBLANES:(g + 1) * SUBLANES]
                    tap_s[j * SUBLANES:(j + 1) * SUBLANES, lanes] += part
                du0_s[pl.ds(r0, CONV_CHUNK), lanes] = acc
            return carry

        lax.fori_loop(0, tr // CONV_CHUNK, chunk, 0)
        taps = [jnp.sum(tap_s[j * SUBLANES:(j + 1) * SUBLANES, :], axis=0, keepdims=True)
                for j in range(CONV_KERNEL)]
        taps.append(jnp.zeros((HALO - CONV_KERNEL, C), F32))
        dw_ref[...] += jnp.concatenate(taps, axis=0)
        du0 = du0_s[...]
        a, sig = a_ref[...], _sigmoid(b_ref[...])
        dp_ref[:, 3 * A:3 * A + C] = (du0 * sig).astype(BF16)
        dp_ref[:, 3 * A + C:] = (du0 * a * sig * (1.0 - sig)).astype(BF16)
        cos_, nsin = cos_ref[...], -sin_ref[...]
        for j in range(A // 128):
            lanes = slice(j * 128, (j + 1) * 128)
            dp_ref[:, j * 128:(j + 1) * 128] = (_rotate(dq_ref[:, lanes], cos_, nsin) * scale).astype(BF16)
            dp_ref[:, A + j * 128:A + (j + 1) * 128] = _rotate(dk_ref[:, lanes], cos_, nsin).astype(BF16)
        dp_ref[:, 2 * A:3 * A] = dv_ref[...].astype(BF16)

    ta, tb, ha, hb = _conv_specs(tr, C, 3, 4)
    row = pl.BlockSpec((tr, C), lambda i: (i, 0))
    arow = pl.BlockSpec((tr, A), lambda i: (i, 0))
    tab = pl.BlockSpec((tr, 128), lambda i: (i, 0))
    nxt = pl.BlockSpec((HALO, C), lambda i: (jnp.minimum((i + 1) * per, S // HALO - 1), 0))
    wspec = pl.BlockSpec((HALO, C), lambda i: (0, 0))
    return pl.pallas_call(
        body, name=name, grid=(nt,),
        in_specs=[row, nxt, ta, tb, ha, hb, wspec, arow, arow, arow, tab, tab],
        out_specs=[pl.BlockSpec((tr, 3 * A + 2 * C), lambda i: (i, 0)), wspec],
        out_shape=[jax.ShapeDtypeStruct((S, 3 * A + 2 * C), BF16), jax.ShapeDtypeStruct((HALO, C), F32)],
        scratch_shapes=[pltpu.VMEM((SUBLANES, tr + HALO, C), F32), pltpu.VMEM((SUBLANES, tr + HALO, C), F32),
                        pltpu.VMEM((tr, C), F32), pltpu.VMEM((HALO * SUBLANES, C), F32)],
        compiler_params=_params(("arbitrary",)),
    )(du1, du1, proj, proj, proj, proj, cw, dq, dk, dv, cos, sin)


def _ada_fwd(c_all, w, b, name):
    B, D = c_all.shape
    N = w.shape[1]
    tn = 768 if N % 768 == 0 else N

    def body(c_ref, w_ref, b_ref, o_ref):
        c = c_ref[...]
        a = (c * _sigmoid(c)).astype(BF16)
        o_ref[...] = jnp.dot(a, w_ref[...].astype(BF16), preferred_element_type=F32) + b_ref[...]

    return pl.pallas_call(
        body, name=name, grid=(N // tn,),
        in_specs=[pl.BlockSpec((B, D), lambda j: (0, 0)), pl.BlockSpec((D, tn), lambda j: (0, j)),
                  pl.BlockSpec((1, tn), lambda j: (0, j))],
        out_specs=pl.BlockSpec((B, tn), lambda j: (0, j)),
        out_shape=jax.ShapeDtypeStruct((B, N), F32),
        compiler_params=_params(("parallel",)),
    )(c_all, w, b)


def _ada_wgrad(c_t, dmod, name):
    D, B = c_t.shape
    N = dmod.shape[1]
    tn = 768 if N % 768 == 0 else N

    def body(c_ref, d_ref, o_ref):
        c = c_ref[...]
        a = (c * _sigmoid(c)).astype(BF16)
        o_ref[...] = jnp.dot(a, d_ref[...].astype(BF16), preferred_element_type=F32)

    return pl.pallas_call(
        body, name=name, grid=(N // tn,),
        in_specs=[pl.BlockSpec((D, B), lambda j: (0, 0)), pl.BlockSpec((B, tn), lambda j: (0, j))],
        out_specs=pl.BlockSpec((D, tn), lambda j: (0, j)),
        out_shape=jax.ShapeDtypeStruct((D, N), F32),
        compiler_params=_params(("parallel",)),
    )(c_t, dmod)


def _sum_rows(x, name):
    R, N = x.shape

    def body(x_ref, o_ref):
        acc = x_ref[0:1, :]
        for r in range(1, R):
            acc = acc + x_ref[r:r + 1, :]
        o_ref[...] = acc

    return pl.pallas_call(
        body, name=name, out_shape=jax.ShapeDtypeStruct((1, N), F32),
        compiler_params=_params(),
    )(x)


def _adamw(w, g, m, v, name, deps=()):
    R, C = w.shape
    tr = _row_tile(R, 256) if R % 256 == 0 else R
    bc1 = 1.0 - ADAM_B1 ** ADAM_STEP
    bc2 = 1.0 - ADAM_B2 ** ADAM_STEP
    deps = _deps(deps)

    def body(w_ref, g_ref, m_ref, v_ref, d_ref, mo_ref, vo_ref, go_ref):
        g_ = g_ref[...]
        m_ = ADAM_B1 * m_ref[...] + (1.0 - ADAM_B1) * g_
        v_ = ADAM_B2 * v_ref[...] + (1.0 - ADAM_B2) * (g_ * g_)
        mo_ref[...] = m_
        vo_ref[...] = v_
        go_ref[...] = g_
        d_ref[...] = -ADAM_LR * ((m_ / bc1) / (jnp.sqrt(v_ / bc2) + ADAM_EPS) + ADAM_WD * w_ref[...])

    row = pl.BlockSpec((tr, C), lambda i: (i, 0))
    out = jax.ShapeDtypeStruct((R, C), F32)
    return pl.pallas_call(
        _blind_to(body, 4, len(deps)), name=name, grid=(R // tr,),
        in_specs=[row] * 4 + [ANY_SPEC] * len(deps), out_specs=[row] * 4, out_shape=[out] * 4,
        compiler_params=_params(("parallel",), streaming=True),
    )(w, g, m, v, *deps)


def _coords():
    return lax.axis_index("x"), lax.axis_index("y"), lax.axis_index("c")


def _all_gather8(x, name, deps=()):
    R, N = x.shape
    assert R == 8
    flips = [(fx, fy, fc) for fx in (0, 1) for fy in (0, 1) for fc in (0, 1)][1:]
    deps = _deps(deps)

    def body(x_ref, o_ref, send_sems, recv_sems):
        mx, my, mc = _coords()
        me = 4 * mx + 2 * my + mc

        def rows(dev):
            return o_ref.at[pl.ds(pl.multiple_of(dev * R, R), R), :]

        o_ref[pl.ds(pl.multiple_of(me * R, R), R), :] = x_ref[...]
        copies = []
        for t, (fx, fy, fc) in enumerate(flips):
            peer = (mx ^ fx, my ^ fy, mc ^ fc)
            copies.append(pltpu.make_async_remote_copy(
                src_ref=x_ref, dst_ref=rows(me), send_sem=send_sems.at[t], recv_sem=recv_sems.at[t],
                device_id=peer, device_id_type=MESH))
        for cp in copies:
            cp.start()
        for t, (fx, fy, fc) in enumerate(flips):
            peer_id = 4 * (mx ^ fx) + 2 * (my ^ fy) + (mc ^ fc)
            pltpu.make_async_remote_copy(
                src_ref=x_ref, dst_ref=rows(peer_id), send_sem=send_sems.at[t], recv_sem=recv_sems.at[t],
                device_id=(mx ^ fx, my ^ fy, mc ^ fc), device_id_type=MESH).wait_recv()
        for cp in copies:
            cp.wait_send()

    return pl.pallas_call(
        _blind_to(body, 1, len(deps)), name=name,
        in_specs=[pl.BlockSpec(memory_space=pltpu.VMEM)] + [ANY_SPEC] * len(deps),
        out_specs=pl.BlockSpec(memory_space=pltpu.VMEM),
        out_shape=jax.ShapeDtypeStruct((N_DEV * R, N), F32),
        scratch_shapes=[pltpu.SemaphoreType.DMA((7,)), pltpu.SemaphoreType.DMA((7,))],
        compiler_params=pltpu.CompilerParams(has_side_effects=True, vmem_limit_bytes=VMEM_LIMIT_BYTES),
    )(x, *deps)


def _half_rows(rows, half):
    return pl.ds(pl.multiple_of(half * (rows // 2), 8), rows // 2)


def _split_start(bufs, plan, n, name):
    nb = len(bufs)

    def body(*refs):
        send_sems, recv_sems, token = refs[nb], refs[nb + 1], refs[-1]
        for t, (src, dst, dev) in enumerate(plan(refs[:nb])):
            pltpu.make_async_remote_copy(src_ref=src, dst_ref=dst, send_sem=send_sems.at[t],
                                         recv_sem=recv_sems.at[t], device_id=dev, device_id_type=MESH).start()
        token[...] = jnp.zeros_like(token)

    out = pl.pallas_call(
        body, name=name,
        out_shape=(pltpu.SemaphoreType.DMA((n,)), pltpu.SemaphoreType.DMA((n,)),
                   *[pltpu.HBM(b.shape, b.dtype) for b in bufs], jax.ShapeDtypeStruct((8, 128), F32)),
        in_specs=[HBM_SPEC] * nb,
        out_specs=(SEM_SPEC, SEM_SPEC, *[HBM_SPEC] * nb, pl.BlockSpec(memory_space=pltpu.VMEM)),
        input_output_aliases={i: 2 + i for i in range(nb)},
        compiler_params=pltpu.CompilerParams(has_side_effects=DATAFLOW),
    )(*[pltpu.with_memory_space_constraint(b, pltpu.HBM) for b in bufs])
    return out[0], out[1], list(out[2:2 + nb]), out[-1]


def _split_wait(bufs, send_sems, recv_sems, plan, after, name):
    nb = len(bufs)
    after = _deps(after)

    def body(*refs):
        ss, rs = refs[nb], refs[nb + 1]
        for t, (src, dst, dev) in enumerate(plan(refs[:nb])):
            cp = pltpu.make_async_remote_copy(src_ref=src, dst_ref=dst, send_sem=ss.at[t], recv_sem=rs.at[t],
                                              device_id=dev, device_id_type=MESH)
            cp.wait_send()
            cp.wait_recv()

    out = pl.pallas_call(
        body, name=name,
        out_shape=tuple(pltpu.HBM(b.shape, b.dtype) for b in bufs),
        in_specs=[HBM_SPEC] * nb + [SEM_SPEC, SEM_SPEC] + [ANY_SPEC] * len(after),
        out_specs=tuple([HBM_SPEC] * nb),
        input_output_aliases={i: i for i in range(nb)},
        compiler_params=pltpu.CompilerParams(has_side_effects=DATAFLOW),
    )(*bufs, send_sems, recv_sems, *after)
    return list(out)


class _Exchange:
    def __init__(self, bufs, plan, n, name):
        self.plan, self.name = plan, name
        self.send_sems, self.recv_sems, self.bufs, self.token = _split_start(bufs, plan, n, name + "_start")

    def wait(self, after):
        return _split_wait(self.bufs, self.send_sems, self.recv_sems, self.plan, after, self.name + "_wait")


def _cast_place(w, chip_idx, name, deps=()):
    R, C = w.shape
    tr = _row_tile(R, 256) if R % 256 == 0 else R
    deps = _deps(deps)

    def body(k_ref, w_ref, o_ref):
        o_ref[...] = w_ref[...].astype(BF16)

    grid_spec = pltpu.PrefetchScalarGridSpec(
        num_scalar_prefetch=1, grid=(R // tr,),
        in_specs=[pl.BlockSpec((tr, C), lambda i, k: (i, 0))] + [ANY_SPEC] * len(deps),
        out_specs=pl.BlockSpec((None, tr, C), lambda i, k: (k[0], i, 0)),
    )
    return pl.pallas_call(
        _blind_to(body, 2, len(deps)), name=name, grid_spec=grid_spec,
        out_shape=jax.ShapeDtypeStruct((N_CHIPS, R, C), BF16),
        compiler_params=_params(("parallel",), streaming=True),
    )(chip_idx, w, *deps)


def _plan_gather(refs):
    mx, my, mc = _coords()
    me = 2 * mx + my
    plan = []
    for g in refs:
        mine = g.at[me, _half_rows(g.shape[1], mc), :]
        for fx, fy in XY_FLIPS:
            plan.append((mine, mine, (mx ^ fx, my ^ fy, mc)))
    return plan


def _plan_swap(refs):
    mx, my, mc = _coords()
    plan = []
    for g in refs:
        for fx, fy in XY_FLIPS:
            have = g.at[2 * (mx ^ fx) + (my ^ fy), _half_rows(g.shape[1], mc), :]
            plan.append((have, have, (mx, my, 1 - mc)))
    return plan


def _plan_other_halves(refs):
    n = len(refs) // 2
    mx, my, mc = _coords()
    return [(g.at[pl.ds(0, N_CHIPS), _half_rows(g.shape[1], 1 - mc), :], land, (mx, my, 1 - mc))
            for g, land in zip(refs[:n], refs[n:])]


def _plan_chunks(refs):
    n = len(refs) // 2
    mx, my, mc = _coords()
    plan = []
    for s, land in zip(refs[:n], refs[n:]):
        for t, (fx, fy) in enumerate(XY_FLIPS):
            plan.append((s.at[2 * (mx ^ fx) + (my ^ fy)], land.at[t], (mx ^ fx, my ^ fy, mc)))
    return plan


def _plan_share(refs):
    mx, my, mc = _coords()
    return [(full.at[mc], full.at[mc], (mx, my, 1 - mc)) for full in refs]


def _add_half(g, recv, core_idx, name):
    nk, R, C = g.shape
    rh = R // 2
    tr = _row_tile(rh, 128) if rh % 128 == 0 else rh
    nt = rh // tr

    def body(c_ref, g_ref, r_ref, o_ref):
        o_ref[...] = (g_ref[...] + r_ref[...]).astype(BF16)

    grid_spec = pltpu.PrefetchScalarGridSpec(
        num_scalar_prefetch=1, grid=(nk, nt),
        in_specs=[pl.BlockSpec((None, tr, C), lambda k, i, c: (k, c[0] * nt + i, 0)),
                  pl.BlockSpec((None, tr, C), lambda k, i, c: (k, i, 0))],
        out_specs=pl.BlockSpec((None, tr, C), lambda k, i, c: (k, i, 0)),
    )
    return pl.pallas_call(
        body, name=name, grid_spec=grid_spec, out_shape=jax.ShapeDtypeStruct((nk, rh, C), BF16),
        compiler_params=_params(("parallel", "parallel"), streaming=True),
    )(core_idx, g, recv)


def _sum_chips(s, land, chip_core, name):
    _, rh, C = s.shape
    tr = _row_tile(rh, 128) if rh % 128 == 0 else rh

    def body(p_ref, s_ref, l_ref, o_ref):
        me = p_ref[0]
        acc = None
        for j in range(N_CHIPS):
            t = jnp.maximum(jnp.bitwise_xor(me, j) - 1, 0)
            term = jnp.where(me == j, s_ref[...], l_ref[t]).astype(F32)
            acc = term if acc is None else acc + term
        o_ref[...] = acc

    grid_spec = pltpu.PrefetchScalarGridSpec(
        num_scalar_prefetch=1, grid=(rh // tr,),
        in_specs=[pl.BlockSpec((None, tr, C), lambda i, p: (p[0], i, 0)),
                  pl.BlockSpec((3, tr, C), lambda i, p: (0, i, 0))],
        out_specs=pl.BlockSpec((None, tr, C), lambda i, p: (p[1], i, 0)),
    )
    return pl.pallas_call(
        body, name=name, grid_spec=grid_spec, out_shape=jax.ShapeDtypeStruct((2, rh, C), F32),
        compiler_params=_params(("parallel",), streaming=True),
    )(chip_core, s, land)


def _rs_send_halves(grads, tag):
    lands = [lax.empty((g.shape[0], g.shape[1] // 2, g.shape[2]), g.dtype) for g in grads]
    return _Exchange(list(grads) + lands, _plan_other_halves, len(grads), f"rs_halves_{tag}")


def _rs_send_chunks(ex, after, core_idx, tag):
    bufs = ex.wait(after)
    n = len(bufs) // 2
    sums = [_add_half(g, r, core_idx, f"rs_add_{tag}_{i}") for i, (g, r) in enumerate(zip(bufs[:n], bufs[n:]))]
    lands = [lax.empty((3,) + s.shape[1:], s.dtype) for s in sums]
    return _Exchange(sums + lands, _plan_chunks, 3 * n, f"rs_chunks_{tag}")


def _rs_send_share(ex, after, chip_core, tag):
    bufs = ex.wait(after)
    n = len(bufs) // 2
    fulls = [_sum_chips(s, l, chip_core, f"rs_sum_{tag}_{i}") for i, (s, l) in enumerate(zip(bufs[:n], bufs[n:]))]
    return _Exchange(fulls, _plan_share, n, f"rs_share_{tag}")


def _rs_finish(ex, after):
    return [b.reshape(2 * b.shape[1], b.shape[2]) for b in ex.wait(after)]


def _ffn_forward(h, n, gate, wg, wu, wd, tag, next_norm=None):
    ga, up, act = _ffn_gate_up(n, wg, wu, f"{tag}_gate_up")
    h_out, f, *n_next = _mm_residual(act, wd, h, gate, 0.5, f"{tag}_down", next_norm=next_norm)
    return (h_out, *n_next), (h, n, ga, up, act, f)


def _ffn_backward(dh_out, df, saved, gain, sc, wg, wu, wd, core_idx, tag, prev=None, last=False):
    h, n, ga, up, act, _ = saved
    (dwd,) = _wgrad_chunk_lhs([act], df, f"{tag}_dwd")
    ex_d = _rs_send_halves([dwd], f"{tag}_d")
    dga, dup = _ffn_dact(df, wd, ga, up, f"{tag}_dact", deps=[ex_d.token])
    ex_d = _rs_send_chunks(ex_d, [dga], core_idx, f"{tag}_d")
    dwg, dwu = _wgrad_chunk_lhs([dga, dup], n, f"{tag}_dwgu", deps=[ex_d.token])
    ex_gu = _rs_send_halves([dwg, dwu], f"{tag}_gu")
    dn = _mm_reduce([dga, dup], [wg, wu], True, True, f"{tag}_dn", deps=[ex_gu.token])
    if not last:
        ex_gu = _rs_send_chunks(ex_gu, [dn], core_idx, f"{tag}_gu")
    outs = _norm_mod_bwd(dn, h, gain, sc, dh_out, f"{tag}_norm_bwd", prev=prev, deps=[ex_gu.token])
    return outs, (ex_d, ex_gu)


def _pad_cols(v, n):
    return jnp.pad(v, ((0, 0), (0, n - v.shape[1])))


def _mixer_forward(h1, n2, gt2, win, wout, conv_w, conv_dw_b, conv_ln_g, conv_ln_b, attn_out_g, conv_out_g,
                   next_norm=None):
    S, D = h1.shape
    proj = _mm_cols(n2, win, "mix_in")
    cos, sin = _rope_tables(S)
    q, k, v = _qkv_rope(proj, cos, sin, "qkv_rope")
    attn, lse = _attn_fwd(q, k, v, "attn_fwd")
    u1, y = _mixer_merge(proj, attn, conv_w, conv_dw_b, conv_ln_g, conv_ln_b, attn_out_g, conv_out_g, "mix_merge")
    h2, mo, *n_next = _mm_residual(y[None], wout.reshape(1, D, D), h1, gt2, 1.0, "mix_out", next_norm=next_norm)
    return (h2, *n_next), (h1, n2, proj, cos, sin, q, k, v, attn, lse, u1, y, mo)


def _mixer_backward(dh2, dmo, saved, mix_norm_g, sc2, win, wout, conv_w, conv_ln_g, conv_ln_b, attn_out_g,
                    conv_out_g, core_idx, prev=None):
    h1, n2, proj, cos, sin, q, k, v, attn, lse, u1, y, _ = saved
    S, D = h1.shape
    (dwout,) = _wgrad_chunk_lhs([y[None]], dmo, "mix_dwout")
    dy = _mm_nt(dmo, wout.reshape(D, D), "mix_dy")
    dattn, delta, d_attn_g = _attn_out_bwd(dy, attn, attn_out_g, "attn_out_bwd")
    dq, dk, dv = _attn_bwd(q, k, v, dattn, lse, delta, "attn_bwd")
    du1, d_gco, d_lng, d_lnb, d_cb = _conv_bwd_norms(dy, u1, conv_ln_g, conv_ln_b, conv_out_g, "conv_bwd_norms")
    dproj, d_cw = _dproj(du1, proj, conv_w, dq, dk, dv, cos, sin, "mix_dproj")
    (dwin,) = _wgrad_chunk_rhs(n2, [dproj], N_CHIPS, False, "mix_dwin")
    ex = _rs_send_halves([dwin, dwout.reshape(N_CHIPS, D // N_CHIPS, D)], "mix")
    dn2 = _mm_reduce([dproj], [win], False, False, "mix_dn", deps=[ex.token])
    ex = _rs_send_chunks(ex, [dn2], core_idx, "mix")
    outs = _norm_mod_bwd(dn2, h1, mix_norm_g, sc2, dh2, "mix_norm_bwd", prev=prev, deps=[ex.token])
    return outs, ex, (d_cb, d_lng, d_lnb, d_attn_g, d_gco, d_cw)


def kernel(x, c, w_ada, b_ada, ffn1_norm_g, ffn1_w_gate, ffn1_w_up, ffn1_w_down, mix_norm_g, w_in, conv_dw_w, conv_dw_b, conv_ln_g, conv_ln_b, attn_out_g, conv_out_g, w_out, ffn2_norm_g, ffn2_w_gate, ffn2_w_up, ffn2_w_down, final_norm_g, loss_target, m_w_ada, m_b_ada, m_ffn1_norm_g, m_ffn1_w_gate, m_ffn1_w_up, m_ffn1_w_down, m_mix_norm_g, m_w_in, m_conv_dw_w, m_conv_dw_b, m_conv_ln_g, m_conv_ln_b, m_attn_out_g, m_conv_out_g, m_w_out, m_ffn2_norm_g, m_ffn2_w_gate, m_ffn2_w_up, m_ffn2_w_down, m_final_norm_g, v_w_ada, v_b_ada, v_ffn1_norm_g, v_ffn1_w_gate, v_ffn1_w_up, v_ffn1_w_down, v_mix_norm_g, v_w_in, v_conv_dw_w, v_conv_dw_b, v_conv_ln_g, v_conv_ln_b, v_attn_out_g, v_conv_out_g, v_w_out, v_ffn2_norm_g, v_ffn2_w_gate, v_ffn2_w_up, v_ffn2_w_down, v_final_norm_g):
    S, D = x.shape[1], x.shape[2]
    mx, my, mc = _coords()
    chip = 2 * mx + my
    dev = 4 * mx + 2 * my + mc
    chip_idx = chip.astype(jnp.int32).reshape(1)
    core_idx = mc.astype(jnp.int32).reshape(1)
    chip_core = jnp.stack([chip, mc]).astype(jnp.int32)
    h0 = x[0]
    target = loss_target[0]

    ncw = CONV_KERNEL * 128
    n0 = -(-(D + ncw) // 1024) * 1024
    pk0 = _pad_cols(jnp.concatenate([c.reshape(1, D), conv_dw_w.reshape(1, ncw)], axis=1), n0)
    g0 = _all_gather8(pk0.reshape(8, n0 // 8), "gather_c").reshape(N_DEV, n0)
    c_all = g0[:, :D]
    conv_w = jnp.concatenate([g0[2 * kc, D:D + ncw].reshape(CONV_KERNEL, 128) for kc in range(N_CHIPS)], axis=1)
    conv_w = jnp.pad(conv_w, ((0, HALO - CONV_KERNEL), (0, 0)))
    nmod = w_ada.shape[2]
    b_shard = lax.dynamic_slice(b_ada, (0, chip * nmod), (1, nmod))
    mod_part = _ada_fwd(c_all, w_ada[0], b_shard, "ada_fwd")
    g1 = _all_gather8(mod_part, "gather_mod")
    mod_all = jnp.concatenate([g1[16 * kc:16 * kc + 8] for kc in range(N_CHIPS)], axis=1)
    mod = lax.dynamic_slice(mod_all, (dev, 0), (1, 9 * D))
    sh1, sc1, gt1, sh2, sc2, gt2, sh3, sc3, gt3 = [mod[:, i * D:(i + 1) * D] for i in range(9)]

    def gather_start(ws, tag, dep):
        slots = [_cast_place(w, chip_idx, f"cast_{tag}_{i}", deps=[dep]) for i, w in enumerate(ws)]
        return _Exchange(slots, _plan_gather, 3 * len(ws), f"gather_{tag}")

    def swap_start(ex, after, tag):
        return _Exchange(ex.wait(after), _plan_swap, 3 * len(ex.bufs), f"swap_{tag}")

    ex_gu1 = gather_start([ffn1_w_gate[0].T, ffn1_w_up[0].T], "ffn1_gu", g1)
    n1 = _norm_mod(h0, ffn1_norm_g, sc1, sh1, "ffn1_norm", deps=[ex_gu1.token])
    ex_d1 = gather_start([ffn1_w_down[0]], "ffn1_d", n1)
    ex_wm = gather_start([w_in[0], w_out[0]], "mix", ex_d1.token)
    ex_w2 = gather_start([ffn2_w_gate[0].T, ffn2_w_up[0].T, ffn2_w_down[0]], "ffn2", ex_wm.token)

    wg1, wu1 = swap_start(ex_gu1, [ex_w2.token], "ffn1_gu").wait([])
    ga1, up1, act1 = _ffn_gate_up(n1, wg1, wu1, "ffn1_gate_up")
    (wd1,) = swap_start(ex_d1, [act1], "ffn1_d").wait([])
    ex_wm = swap_start(ex_wm, [wd1], "mix")
    h1, f1, n2 = _mm_residual(act1, wd1, h0, gt1, 0.5, "ffn1_down", next_norm=(mix_norm_g, sc2, sh2))
    saved1 = (h0, n1, ga1, up1, act1, f1)
    win, wout = ex_wm.wait([h1])
    ex_w2 = swap_start(ex_w2, [h1], "ffn2")
    (h2, n3), saved2 = _mixer_forward(h1, n2, gt2, win, wout, conv_w, conv_dw_b, conv_ln_g, conv_ln_b,
                                      attn_out_g, conv_out_g, next_norm=(ffn2_norm_g, sc3, sh3))
    wg2, wu2, wd2 = ex_w2.wait([h2])
    (h3,), saved3 = _ffn_forward(h2, n3, gt3, wg2, wu2, wd2, "ffn2")
    loss_part, dh3, d_final_g, df3, d_gt3 = _loss_head(h3, final_norm_g.reshape(1, D), target, saved3[5], gt3, 0.5,
                                                       "loss_head")

    (dh2, d_sh3, d_sc3, d_gain3, dmo, d_gt2), (ex_d2, ex_gu2) = _ffn_backward(
        dh3, df3, saved3, ffn2_norm_g, sc3, wg2, wu2, wd2, core_idx, "ffn2", prev=(saved2[12], gt2, 1.0))
    (dh1, d_sh2, d_sc2, d_gain2, df1, d_gt1), ex_mix, small_mix = _mixer_backward(
        dh2, dmo, saved2, mix_norm_g, sc2, win, wout, conv_w, conv_ln_g, conv_ln_b, attn_out_g, conv_out_g, core_idx,
        prev=(f1, gt1, 0.5))
    d_cb, d_lng, d_lnb, d_attn_g, d_gco, d_cw = small_mix
    (dh0, d_sh1, d_sc1, d_gain1), (ex_d1, ex_gu1) = _ffn_backward(
        dh1, df1, saved1, ffn1_norm_g, sc1, wg1, wu1, wd1, core_idx, "ffn1", last=True)

    dmod = jnp.concatenate([d_sh1, d_sc1, d_gt1, d_sh2, d_sc2, d_gt2, d_sh3, d_sc3, d_gt3], axis=1)
    small = [d_gain1, d_gain2, d_gain3, d_final_g, d_cb, d_lng, d_lnb, d_attn_g, d_gco,
             d_cw[:CONV_KERNEL].reshape(1, CONV_KERNEL * CONV_WIDTH), loss_part]
    pk1 = jnp.concatenate([dmod] + small, axis=1)
    n1_ = -(-pk1.shape[1] // 1024) * 1024
    gathered = _all_gather8(_pad_cols(pk1, n1_).reshape(8, n1_ // 8), "gather_small").reshape(N_DEV, n1_)
    ex_gu1 = _rs_send_chunks(ex_gu1, [gathered], core_idx, "ffn1_gu")
    tot = _sum_rows(gathered, "sum_small")
    off = [0]

    def take(nel):
        out = tot[:, off[0]:off[0] + nel]
        off[0] += nel
        return out

    g_b_ada = take(9 * D)
    g_ffn1_norm, g_mix_norm, g_ffn2_norm, g_final = take(D), take(D), take(D), take(D)
    g_cb, g_lng, g_lnb, g_attn_g, g_gco = take(512), take(512), take(512), take(512), take(512)
    g_cw_full = take(CONV_KERNEL * CONV_WIDTH).reshape(CONV_KERNEL, CONV_WIDTH)
    loss = take(128)[0, 0]
    g_cw = lax.dynamic_slice(g_cw_full, (0, chip * 128), (CONV_KERNEL, 128))

    dmod_shard = lax.dynamic_slice(gathered[:, :9 * D], (0, chip * nmod), (N_DEV, nmod))
    dmod16 = jnp.pad(dmod_shard, ((0, N_DEV), (0, 0)))
    c_t16 = jnp.pad(c_all.T, ((0, 0), (0, N_DEV)))
    g_w_ada = _ada_wgrad(c_t16, dmod16, "ada_wgrad")

    names = ["w_ada", "b_ada", "ffn1_norm_g", "ffn1_w_gate", "ffn1_w_up", "ffn1_w_down", "mix_norm_g", "w_in",
             "conv_dw_w", "conv_dw_b", "conv_ln_g", "conv_ln_b", "attn_out_g", "conv_out_g", "w_out", "ffn2_norm_g",
             "ffn2_w_gate", "ffn2_w_up", "ffn2_w_down", "final_norm_g"]
    weights = dict(zip(names, [w_ada, b_ada, ffn1_norm_g, ffn1_w_gate, ffn1_w_up, ffn1_w_down, mix_norm_g, w_in,
                               conv_dw_w, conv_dw_b, conv_ln_g, conv_ln_b, attn_out_g, conv_out_g, w_out,
                               ffn2_norm_g, ffn2_w_gate, ffn2_w_up, ffn2_w_down, final_norm_g]))
    ms = dict(zip(names, [m_w_ada, m_b_ada, m_ffn1_norm_g, m_ffn1_w_gate, m_ffn1_w_up, m_ffn1_w_down, m_mix_norm_g,
                          m_w_in, m_conv_dw_w, m_conv_dw_b, m_conv_ln_g, m_conv_ln_b, m_attn_out_g, m_conv_out_g,
                          m_w_out, m_ffn2_norm_g, m_ffn2_w_gate, m_ffn2_w_up, m_ffn2_w_down, m_final_norm_g]))
    vs = dict(zip(names, [v_w_ada, v_b_ada, v_ffn1_norm_g, v_ffn1_w_gate, v_ffn1_w_up, v_ffn1_w_down, v_mix_norm_g,
                          v_w_in, v_conv_dw_w, v_conv_dw_b, v_conv_ln_g, v_conv_ln_b, v_attn_out_g, v_conv_out_g,
                          v_w_out, v_ffn2_norm_g, v_ffn2_w_gate, v_ffn2_w_up, v_ffn2_w_down, v_final_norm_g]))
    grads, deltas, new_ms, new_vs = {}, {}, {}, {}

    def adamw_big(nm, g2d, deps=(), transposed=False):
        shape = weights[nm].shape
        two_d = (shape[-2], shape[-1])

        def view(t):
            return t.reshape(two_d).T if transposed else t.reshape(two_d)

        d_, m_, v_, g_ = _adamw(view(weights[nm]), g2d, view(ms[nm]), view(vs[nm]), f"adamw_{nm}", deps=deps)
        grads[nm], deltas[nm], new_ms[nm], new_vs[nm] = (
            (t.T if transposed else t).reshape(shape) for t in (g_, d_, m_, v_))
        return d_

    d_ada = adamw_big("w_ada", g_w_ada, deps=[ex_gu1.token])
    small_grads = {"b_ada": g_b_ada, "ffn1_norm_g": g_ffn1_norm, "mix_norm_g": g_mix_norm, "conv_dw_w": g_cw,
                   "conv_dw_b": g_cb, "conv_ln_g": g_lng, "conv_ln_b": g_lnb, "attn_out_g": g_attn_g,
                   "conv_out_g": g_gco, "ffn2_norm_g": g_ffn2_norm, "final_norm_g": g_final}
    small_names = [nm for nm in names if nm in small_grads]

    def pack_small(arrs):
        flat = jnp.concatenate([arrs[nm].reshape(1, -1) for nm in small_names], axis=1)
        npad = -(-flat.shape[1] // 1024) * 1024
        return _pad_cols(flat, npad).reshape(8, npad // 8)

    d_s, m_s, v_s, _ = _adamw(pack_small(weights), pack_small(small_grads), pack_small(ms), pack_small(vs),
                           "adamw_small")
    pos = 0
    for nm in small_names:
        shape, nel = weights[nm].shape, weights[nm].size
        grads[nm] = small_grads[nm].reshape(shape)
        deltas[nm], new_ms[nm], new_vs[nm] = (t.reshape(1, -1)[:, pos:pos + nel].reshape(shape)
                                              for t in (d_s, m_s, v_s))
        pos += nel

    ex_d2 = _rs_send_share(ex_d2, [d_ada, d_s], chip_core, "ffn2_d")
    ex_gu2 = _rs_send_share(ex_gu2, [ex_d2.token], chip_core, "ffn2_gu")
    ex_mix = _rs_send_share(ex_mix, [ex_gu2.token], chip_core, "mix")
    ex_d1 = _rs_send_share(ex_d1, [ex_mix.token], chip_core, "ffn1_d")
    (g_wd2,) = _rs_finish(ex_d2, [ex_d1.token])
    last = [adamw_big("ffn2_w_down", g_wd2)]
    g_wg2, g_wu2 = _rs_finish(ex_gu2, last)
    last = [adamw_big("ffn2_w_gate", g_wg2, transposed=True), adamw_big("ffn2_w_up", g_wu2, transposed=True)]
    g_win, g_wout = _rs_finish(ex_mix, last)
    last = [adamw_big("w_in", g_win), adamw_big("w_out", g_wout)]
    (g_wd1,) = _rs_finish(ex_d1, last)
    last = [adamw_big("ffn1_w_down", g_wd1)]
    ex_gu1 = _rs_send_share(ex_gu1, last, chip_core, "ffn1_gu")
    g_wg1, g_wu1 = _rs_finish(ex_gu1, [])
    adamw_big("ffn1_w_gate", g_wg1, transposed=True)
    adamw_big("ffn1_w_up", g_wu1, transposed=True)

    return (loss, dh0[None], *[grads[nm] for nm in names], *[deltas[nm] for nm in names],
            *[new_ms[nm] for nm in names], *[new_vs[nm] for nm in names])
```

```python
import jax
import jax.numpy as jnp
import numpy as np
from jax import lax
from jax.experimental import pallas as pl
from jax.experimental.pallas import tpu as pltpu

F32 = jnp.float32
BF16 = jnp.bfloat16
MESH = pl.DeviceIdType.MESH

RMS_EPS = 1e-6
LN_EPS = 1e-5
HEAD_DIM = 64
ATTN_WIDTH = 512
CONV_WIDTH = 512
ATTN_BLOCK = 128
DILATIONS = (1, 4, 16)
SUPER_ROWS = ATTN_BLOCK * 16
ROPE_THETA = 10000.0
CONV_KERNEL = 31
HALO = 32
N_CHIPS = 4
N_DEV = 8
ADAM_LR, ADAM_B1, ADAM_B2, ADAM_EPS, ADAM_WD, ADAM_STEP = 0.001, 0.9, 0.999, 1e-08, 0.01, 10
VMEM_LIMIT_BYTES = 48 * 1024 * 1024
VMEM_LIMIT_STREAMING = 62 * 1024 * 1024
NEG = -1e30

NT = (((1,), (1,)), ((), ()))
TN = (((0,), (0,)), ((), ()))

ANY_SPEC = pl.BlockSpec(memory_space=pl.ANY)
HBM_SPEC = pl.BlockSpec(memory_space=pltpu.HBM)
SEM_SPEC = pl.BlockSpec(memory_space=pltpu.SEMAPHORE)
DATAFLOW = pltpu.SideEffectType.DATAFLOW_SIDE_EFFECTING
XY_FLIPS = ((0, 1), (1, 0), (1, 1))


def _params(sem=None, streaming=False):
    return pltpu.CompilerParams(dimension_semantics=sem,
                                vmem_limit_bytes=VMEM_LIMIT_STREAMING if streaming else VMEM_LIMIT_BYTES)


def _row_tile(rows, want):
    t = min(rows, want)
    assert rows % t == 0
    return t


def _sigmoid(x):
    return 1.0 / (1.0 + jnp.exp(-x))


def _deps(deps):
    return [d for d in deps if d is not None]


def _blind_to(body, n_in, n_dep):
    def wrapped(*refs):
        return body(*refs[:n_in], *refs[n_in + n_dep:])
    return wrapped


def _vec_spec(d, ngrid):
    if ngrid == 1:
        return pl.BlockSpec((1, d), lambda i: (0, 0))
    return pl.BlockSpec((1, d), lambda i, j: (0, 0))


def _norm_mod(h, gain, sc, sh, name, deps=()):
    S, D = h.shape
    tr = _row_tile(S, 512)
    deps = _deps(deps)

    def body(h_ref, g_ref, sc_ref, sh_ref, n_ref):
        x = h_ref[...]
        r = lax.rsqrt(jnp.mean(x * x, axis=-1, keepdims=True) + RMS_EPS)
        y = (x * r) * g_ref[...]
        n_ref[...] = (y * (1.0 + sc_ref[...]) + sh_ref[...]).astype(BF16)

    row = pl.BlockSpec((tr, D), lambda i: (i, 0))
    return pl.pallas_call(
        _blind_to(body, 4, len(deps)), name=name, grid=(S // tr,),
        in_specs=[row, _vec_spec(D, 1), _vec_spec(D, 1), _vec_spec(D, 1)] + [ANY_SPEC] * len(deps),
        out_specs=row, out_shape=jax.ShapeDtypeStruct((S, D), BF16),
        compiler_params=_params(("parallel",)),
    )(h, gain, sc, sh, *deps)


def _norm_mod_bwd(dn, h_in, gain, sc, dh_out, name, prev=None, deps=()):
    S, D = h_in.shape
    tr = _row_tile(S, 512)
    deps = _deps(deps)
    n_in = 5 if prev is None else 7

    def body(*refs):
        dn_ref, h_ref, g_ref, sc_ref, dho_ref = refs[:5]
        dh_ref, dsh_ref, dsc_ref, dg_ref = refs[n_in:n_in + 4]

        @pl.when(pl.program_id(0) == 0)
        def _():
            for ref in refs[n_in + 1:n_in + 4] + refs[n_in + 5:]:
                ref[...] = jnp.zeros_like(ref)

        x = h_ref[...]
        dn_ = dn_ref[...]
        g = g_ref[...]
        one_sc = 1.0 + sc_ref[...]
        r = lax.rsqrt(jnp.mean(x * x, axis=-1, keepdims=True) + RMS_EPS)
        xh = x * r
        dsh_ref[...] += jnp.sum(dn_, axis=0, keepdims=True)
        dsc_ref[...] += jnp.sum(dn_ * (xh * g), axis=0, keepdims=True)
        dg_ref[...] += jnp.sum(dn_ * one_sc * xh, axis=0, keepdims=True)
        dxh = dn_ * (g * one_sc)
        dh = dho_ref[...] + r * (dxh - xh * jnp.mean(dxh * xh, axis=-1, keepdims=True))
        dh_ref[...] = dh
        if prev is not None:
            _gate_back(dh, refs[5], refs[6], prev[2], refs[n_in + 4], refs[n_in + 5])

    row = pl.BlockSpec((tr, D), lambda i: (i, 0))
    vec = _vec_spec(D, 1)
    extra_in = [] if prev is None else [row, vec]
    extra_out = [] if prev is None else [row, vec]
    extra_shape = [] if prev is None else [jax.ShapeDtypeStruct((S, D), BF16), jax.ShapeDtypeStruct((1, D), F32)]
    return pl.pallas_call(
        _blind_to(body, n_in, len(deps)), name=name, grid=(S // tr,),
        in_specs=[row, row, vec, vec, row] + extra_in + [ANY_SPEC] * len(deps),
        out_specs=[row, vec, vec, vec] + extra_out,
        out_shape=[jax.ShapeDtypeStruct((S, D), F32)] + [jax.ShapeDtypeStruct((1, D), F32)] * 3 + extra_shape,
        compiler_params=_params(("arbitrary",)),
    )(dn, h_in, gain, sc, dh_out, *([] if prev is None else prev[:2]), *deps)


def _gate_back(dh, f_ref, gate_ref, coef, df_ref, dgate_ref):
    df_ref[...] = ((coef * gate_ref[...]) * dh).astype(BF16)
    dgate_ref[...] += jnp.sum(coef * dh * f_ref[...].astype(F32), axis=0, keepdims=True)


def _loss_head(h, gain, target, f, gate, coef, name):
    S, D = h.shape
    tr = _row_tile(S, 512)

    def body(h_ref, g_ref, t_ref, f_ref, gate_ref, loss_ref, dh_ref, dg_ref, df_ref, dgate_ref):
        @pl.when(pl.program_id(0) == 0)
        def _():
            loss_ref[...] = jnp.zeros_like(loss_ref)
            dg_ref[...] = jnp.zeros_like(dg_ref)
            dgate_ref[...] = jnp.zeros_like(dgate_ref)

        x = h_ref[...]
        g = g_ref[...]
        r = lax.rsqrt(jnp.mean(x * x, axis=-1, keepdims=True) + RMS_EPS)
        xh = x * r
        err = xh * g - t_ref[...]
        part = 0.5 * jnp.sum(jnp.mean(err * err, axis=-1, keepdims=True), axis=0, keepdims=True)
        loss_ref[...] += jnp.broadcast_to(part, loss_ref.shape)
        dy = err * (1.0 / D)
        dg_ref[...] += jnp.sum(dy * xh, axis=0, keepdims=True)
        dxh = dy * g
        dh = r * (dxh - xh * jnp.mean(dxh * xh, axis=-1, keepdims=True))
        dh_ref[...] = dh
        _gate_back(dh, f_ref, gate_ref, coef, df_ref, dgate_ref)

    row = pl.BlockSpec((tr, D), lambda i: (i, 0))
    vec = _vec_spec(D, 1)
    return pl.pallas_call(
        body, name=name, grid=(S // tr,),
        in_specs=[row, vec, row, row, vec],
        out_specs=[pl.BlockSpec((1, 128), lambda i: (0, 0)), row, vec, row, vec],
        out_shape=[jax.ShapeDtypeStruct((1, 128), F32), jax.ShapeDtypeStruct((S, D), F32),
                   jax.ShapeDtypeStruct((1, D), F32), jax.ShapeDtypeStruct((S, D), BF16),
                   jax.ShapeDtypeStruct((1, D), F32)],
        compiler_params=_params(("arbitrary",)),
    )(h, gain, target, f, gate)


def _ffn_gate_up(n, wg_t, wu_t, name):
    S, D = n.shape
    nk, w, _ = wg_t.shape
    tm = _row_tile(S, 512)

    def body(n_ref, wg_ref, wu_ref, dga_ref, dup_ref, act_ref):
        x = n_ref[...]
        ga = lax.dot_general(x, wg_ref[...], NT, preferred_element_type=F32)
        up = lax.dot_general(x, wu_ref[...], NT, preferred_element_type=F32)
        sig = _sigmoid(ga)
        silu = ga * sig
        dga_ref[...] = (up * (sig * (1.0 + ga * (1.0 - sig)))).astype(BF16)
        dup_ref[...] = silu.astype(BF16)
        act_ref[...] = (silu * up).astype(BF16)

    wspec = pl.BlockSpec((None, w, D), lambda k, m: (k, 0, 0))
    ospec = pl.BlockSpec((None, tm, w), lambda k, m: (k, m, 0))
    out = jax.ShapeDtypeStruct((nk, S, w), BF16)
    return pl.pallas_call(
        body, name=name, grid=(nk, S // tm),
        in_specs=[pl.BlockSpec((tm, D), lambda k, m: (m, 0)), wspec, wspec],
        out_specs=[ospec, ospec, ospec], out_shape=[out, out, out],
        compiler_params=_params(("parallel", "parallel")),
    )(n, wg_t, wu_t)


def _mm_residual(lhs, w, h_in, gvec, coef, name, next_norm=None):
    nk, S, kc = lhs.shape
    D = w.shape[2]
    tm = _row_tile(S, 512)
    n_in = 4 if next_norm is None else 7

    def body(*refs):
        l_ref, w_ref, h_ref, g_ref = refs[:4]
        ho_ref, f_ref = refs[n_in:n_in + 2]
        acc_ref = refs[-1]
        k = pl.program_id(1)

        @pl.when(k == 0)
        def _():
            acc_ref[...] = jnp.zeros_like(acc_ref)

        acc_ref[...] += jnp.dot(l_ref[...], w_ref[...], preferred_element_type=F32)

        @pl.when(k == nk - 1)
        def _():
            f = acc_ref[...]
            f_ref[...] = f.astype(BF16)
            x = h_ref[...] + (coef * g_ref[...]) * f
            ho_ref[...] = x
            if next_norm is not None:
                ng_ref, sc_ref, sh_ref = refs[4:7]
                r = lax.rsqrt(jnp.mean(x * x, axis=-1, keepdims=True) + RMS_EPS)
                y = (x * r) * ng_ref[...]
                refs[n_in + 2][...] = (y * (1.0 + sc_ref[...]) + sh_ref[...]).astype(BF16)

    row = pl.BlockSpec((tm, D), lambda m, k: (m, 0))
    vec = _vec_spec(D, 2)
    with_n = next_norm is not None
    return pl.pallas_call(
        body, name=name, grid=(S // tm, nk),
        in_specs=[pl.BlockSpec((None, tm, kc), lambda m, k: (k, m, 0)),
                  pl.BlockSpec((None, kc, D), lambda m, k: (k, 0, 0)), row, vec] + [vec] * (3 * with_n),
        out_specs=[row, row] + [row] * with_n,
        out_shape=[jax.ShapeDtypeStruct((S, D), F32), jax.ShapeDtypeStruct((S, D), BF16)]
        + [jax.ShapeDtypeStruct((S, D), BF16)] * with_n,
        scratch_shapes=[pltpu.VMEM((tm, D), F32)],
        compiler_params=_params(("parallel", "arbitrary")),
    )(lhs, w, h_in, gvec, *(next_norm or ()))


def _mm_cols_rope(n, w, cos, sin, name):
    S, D = n.shape
    nk, _, wd = w.shape
    assert wd % 128 == 0
    tm = _row_tile(S, 512)
    q_groups = ATTN_WIDTH // 128
    scale = HEAD_DIM ** -0.5

    def body(n_ref, w_ref, cos_ref, sin_ref, o_ref):
        x = jnp.dot(n_ref[...], w_ref[...], preferred_element_type=F32)
        first = pl.program_id(0) * (wd // 128)
        c, s = cos_ref[...], sin_ref[...]
        for j in range(wd // 128):
            lanes = slice(j * 128, (j + 1) * 128)
            group = first + j
            xj = x[:, lanes]
            rot = _rotate(xj, c, s) * jnp.where(group < q_groups, scale, 1.0)
            o_ref[:, lanes] = jnp.where(group < 2 * q_groups, rot, xj)

    tab = pl.BlockSpec((tm, 128), lambda k, m: (m, 0))
    return pl.pallas_call(
        body, name=name, grid=(nk, S // tm),
        in_specs=[pl.BlockSpec((tm, D), lambda k, m: (m, 0)), pl.BlockSpec((None, D, wd), lambda k, m: (k, 0, 0)),
                  tab, tab],
        out_specs=pl.BlockSpec((tm, wd), lambda k, m: (m, k)),
        out_shape=jax.ShapeDtypeStruct((S, nk * wd), F32),
        compiler_params=_params(("parallel", "parallel")),
    )(n, w, cos, sin)


def _ffn_dact(df, wd, fga, fup, name, deps=()):
    S, D = df.shape
    nk, w, _ = wd.shape
    tm = _row_tile(S, 512)
    deps = _deps(deps)

    def body(df_ref, wd_ref, fga_ref, fup_ref, dga_ref, dup_ref):
        dact = lax.dot_general(df_ref[...], wd_ref[...], NT, preferred_element_type=F32)
        dga_ref[...] = (dact * fga_ref[...].astype(F32)).astype(BF16)
        dup_ref[...] = (dact * fup_ref[...].astype(F32)).astype(BF16)

    cspec = pl.BlockSpec((None, tm, w), lambda k, m: (k, m, 0))
    out = jax.ShapeDtypeStruct((nk, S, w), BF16)
    return pl.pallas_call(
        _blind_to(body, 4, len(deps)), name=name, grid=(nk, S // tm),
        in_specs=[pl.BlockSpec((tm, D), lambda k, m: (m, 0)), pl.BlockSpec((None, w, D), lambda k, m: (k, 0, 0)),
                  cspec, cspec] + [ANY_SPEC] * len(deps),
        out_specs=[cspec, cspec], out_shape=[out, out],
        compiler_params=_params(("parallel", "parallel")),
    )(df, wd, fga, fup, *deps)


def _mm_nt(d, w, name):
    S, K = d.shape
    N = w.shape[0]
    tm = _row_tile(S, 512)

    def body(d_ref, w_ref, o_ref):
        o_ref[...] = lax.dot_general(d_ref[...], w_ref[...], NT, preferred_element_type=F32)

    return pl.pallas_call(
        body, name=name, grid=(S // tm,),
        in_specs=[pl.BlockSpec((tm, K), lambda m: (m, 0)), pl.BlockSpec((N, K), lambda m: (0, 0))],
        out_specs=pl.BlockSpec((tm, N), lambda m: (m, 0)),
        out_shape=jax.ShapeDtypeStruct((S, N), F32),
        compiler_params=_params(("parallel",)),
    )(d, w)


def _mm_reduce(lhs_list, w_list, chunked3d, w_is_kd, name, deps=()):
    nk = w_list[0].shape[0]
    kc, D = w_list[0].shape[1:] if w_is_kd else w_list[0].shape[:0:-1]
    S = lhs_list[0].shape[1] if chunked3d else lhs_list[0].shape[0]
    tm = _row_tile(S, 512)
    npair = len(lhs_list)
    deps = _deps(deps)

    def body(*refs):
        l_refs, w_refs = refs[:npair], refs[npair:2 * npair]
        o_ref, acc_ref = refs[2 * npair], refs[2 * npair + 1]
        k = pl.program_id(1)

        @pl.when(k == 0)
        def _():
            acc_ref[...] = jnp.zeros_like(acc_ref)

        for l_ref, w_ref in zip(l_refs, w_refs):
            if w_is_kd:
                acc_ref[...] += jnp.dot(l_ref[...], w_ref[...], preferred_element_type=F32)
            else:
                acc_ref[...] += lax.dot_general(l_ref[...], w_ref[...], NT, preferred_element_type=F32)

        @pl.when(k == nk - 1)
        def _():
            o_ref[...] = acc_ref[...]

    if chunked3d:
        lspec = pl.BlockSpec((None, tm, kc), lambda m, k: (k, m, 0))
    else:
        lspec = pl.BlockSpec((tm, kc), lambda m, k: (m, k))
    wspec = pl.BlockSpec((None,) + tuple(w_list[0].shape[1:]), lambda m, k: (k, 0, 0))
    return pl.pallas_call(
        _blind_to(body, 2 * npair, len(deps)), name=name, grid=(S // tm, nk),
        in_specs=[lspec] * npair + [wspec] * npair + [ANY_SPEC] * len(deps),
        out_specs=pl.BlockSpec((tm, D), lambda m, k: (m, 0)),
        out_shape=jax.ShapeDtypeStruct((S, D), F32),
        scratch_shapes=[pltpu.VMEM((tm, D), F32)],
        compiler_params=_params(("parallel", "arbitrary")),
    )(*lhs_list, *w_list, *deps)


def _wgrad_chunk_lhs(lhs_list, rhs, name, deps=()):
    nk, S, w = lhs_list[0].shape
    D = rhs.shape[1]
    ts = _row_tile(S, 512)
    ns = S // ts
    nl = len(lhs_list)
    deps = _deps(deps)

    def body(*refs):
        l_refs, r_ref = refs[:nl], refs[nl]
        o_refs, acc_refs = refs[nl + 1:2 * nl + 1], refs[2 * nl + 1:]
        s = pl.program_id(1)

        @pl.when(s == 0)
        def _():
            for acc_ref in acc_refs:
                acc_ref[...] = jnp.zeros_like(acc_ref)

        x = r_ref[...]
        for l_ref, acc_ref in zip(l_refs, acc_refs):
            acc_ref[...] += lax.dot_general(l_ref[...], x, TN, preferred_element_type=F32)

        @pl.when(s == ns - 1)
        def _():
            for o_ref, acc_ref in zip(o_refs, acc_refs):
                o_ref[...] = acc_ref[...]

    return pl.pallas_call(
        _blind_to(body, nl + 1, len(deps)), name=name, grid=(nk, ns),
        in_specs=[pl.BlockSpec((None, ts, w), lambda k, s: (k, s, 0))] * nl
        + [pl.BlockSpec((ts, D), lambda k, s: (s, 0))] + [ANY_SPEC] * len(deps),
        out_specs=[pl.BlockSpec((None, w, D), lambda k, s: (k, 0, 0))] * nl,
        out_shape=[jax.ShapeDtypeStruct((nk, w, D), F32)] * nl,
        scratch_shapes=[pltpu.VMEM((w, D), F32)] * nl,
        compiler_params=_params(("parallel", "arbitrary")),
    )(*lhs_list, rhs, *deps)


def _wgrad_chunk_rhs(lhs, rhs_list, nk, chunked3d, name, deps=()):
    S, D = lhs.shape
    w = rhs_list[0].shape[2] if chunked3d else rhs_list[0].shape[1] // nk
    ts = _row_tile(S, 512)
    ns = S // ts
    nr = len(rhs_list)
    deps = _deps(deps)

    def body(*refs):
        l_ref, r_refs = refs[0], refs[1:1 + nr]
        o_refs, acc_refs = refs[1 + nr:1 + 2 * nr], refs[1 + 2 * nr:]
        s = pl.program_id(1)

        @pl.when(s == 0)
        def _():
            for acc_ref in acc_refs:
                acc_ref[...] = jnp.zeros_like(acc_ref)

        x = l_ref[...]
        for r_ref, acc_ref in zip(r_refs, acc_refs):
            acc_ref[...] += lax.dot_general(x, r_ref[...], TN, preferred_element_type=F32)

        @pl.when(s == ns - 1)
        def _():
            for o_ref, acc_ref in zip(o_refs, acc_refs):
                o_ref[...] = acc_ref[...]

    if chunked3d:
        rspec = pl.BlockSpec((None, ts, w), lambda k, s: (k, s, 0))
    else:
        rspec = pl.BlockSpec((ts, w), lambda k, s: (s, k))
    ospec = pl.BlockSpec((None, D, w), lambda k, s: (k, 0, 0))
    return pl.pallas_call(
        _blind_to(body, 1 + nr, len(deps)), name=name, grid=(nk, ns),
        in_specs=[pl.BlockSpec((ts, D), lambda k, s: (s, 0))] + [rspec] * nr + [ANY_SPEC] * len(deps),
        out_specs=[ospec] * nr,
        out_shape=[jax.ShapeDtypeStruct((nk, D, w), F32)] * nr,
        scratch_shapes=[pltpu.VMEM((D, w), F32)] * nr,
        compiler_params=_params(("parallel", "arbitrary")),
    )(lhs, *rhs_list, *deps)


def _rope_tables(S):
    pos = np.arange(S, dtype=np.float32)
    inv_freq = (ROPE_THETA ** (-np.arange(0, HEAD_DIM, 2, dtype=np.float32) / HEAD_DIM)).astype(np.float32)
    ang = (pos[:, None] * inv_freq[None, :]).astype(np.float64)
    cos, sin = np.cos(ang).astype(np.float32), np.sin(ang).astype(np.float32)
    cos2 = np.concatenate([cos, cos, cos, cos], axis=1)
    sin2 = np.concatenate([-sin, sin, -sin, sin], axis=1)
    return jnp.asarray(cos2), jnp.asarray(sin2)


def _rotate(t, cos, sin_signed):
    half = HEAD_DIM // 2
    lane = lax.broadcasted_iota(jnp.int32, t.shape, 1)
    first = (lane % HEAD_DIM) < half
    partner = jnp.where(first, pltpu.roll(t, 128 - half, 1), pltpu.roll(t, half, 1))
    return t * cos + partner * sin_signed


def _qkv_rope(proj, cos, sin, name):
    S = proj.shape[0]
    A = ATTN_WIDTH
    tr = _row_tile(S, 512)
    nb = A // 128
    scale = HEAD_DIM ** -0.5

    def body(q_ref, k_ref, v_ref, c_ref, s_ref, qo_ref, ko_ref, vo_ref):
        c, s = c_ref[...], s_ref[...]
        qo_ref[...] = _rotate(q_ref[...], c, s) * scale
        ko_ref[...] = _rotate(k_ref[...], c, s)
        vo_ref[...] = v_ref[...]

    def col(off):
        return pl.BlockSpec((tr, 128), lambda i, j: (i, off + j))

    tab = pl.BlockSpec((tr, 128), lambda i, j: (i, 0))
    out = jax.ShapeDtypeStruct((S, A), F32)
    return pl.pallas_call(
        body, name=name, grid=(S // tr, nb),
        in_specs=[col(0), col(nb), col(2 * nb), tab, tab],
        out_specs=[col(0), col(0), col(0)], out_shape=[out, out, out],
        compiler_params=_params(("parallel", "parallel")),
    )(proj, proj, proj, cos, sin)


def _band_mask(T, has_prev):
    qi = lax.broadcasted_iota(jnp.int32, (T, 2 * T), 0)
    kj = lax.broadcasted_iota(jnp.int32, (T, 2 * T), 1)
    return ((kj < T) & (kj >= qi) & has_prev) | ((kj >= T) & (kj - T <= qi))


def _stack_heads(x, head0):
    zero = jnp.zeros_like(x)
    return jnp.concatenate([jnp.where(head0, x, zero), jnp.where(head0, zero, x)], axis=0)


def _branch_blocks(rows, dilation):
    T = min(ATTN_BLOCK, rows // dilation)
    return T, rows // T


def _block_rows(base, T, dilation):
    if dilation == 1:
        return pl.ds(pl.multiple_of(base, T), T)
    return pl.ds(base, T, stride=dilation)


def _qkv_specs(S):
    groups = ATTN_WIDTH // 128
    return [pl.BlockSpec((S, 128), lambda j, off=t * groups: (0, off + j)) for t in range(3)]


def _attn_fwd(proj, name):
    S, A = proj.shape[0], ATTN_WIDTH
    sup = min(S, SUPER_ROWS)
    nd = len(DILATIONS)
    assert S % sup == 0

    def body(q_ref, k_ref, v_ref, attn_ref, lse_ref, acc_s, m_s, l_s):
        lane = lax.broadcasted_iota(jnp.int32, (1, 128), 1)
        head0 = lane < HEAD_DIM

        def supertile(st, carry):
            row0 = st * sup
            for di, dil in enumerate(DILATIONS):
                T, nblk = _branch_blocks(sup, dil)
                span = T * dil
                assert T == ATTN_BLOCK or sup == S

                def block(idx, c2, di=di, dil=dil, T=T, span=span):
                    r = idx % dil
                    loc = (idx // dil) * span + r
                    base = row0 + loc
                    rows = _block_rows(base, T, dil)
                    prev = _block_rows(jnp.maximum(base - span, r), T, dil)
                    qb = q_ref[rows, :].astype(BF16)
                    k2 = jnp.concatenate([k_ref[prev, :], k_ref[rows, :]], axis=0).astype(BF16)
                    v2 = jnp.concatenate([v_ref[prev, :], v_ref[rows, :]], axis=0).astype(BF16)
                    valid = _band_mask(T, base >= span)
                    q2 = _stack_heads(qb, head0)
                    s = lax.dot_general(q2, k2, NT, preferred_element_type=F32)
                    s = jnp.where(jnp.concatenate([valid, valid], axis=0), s, NEG)
                    m = jnp.max(s, axis=-1, keepdims=True)
                    p = jnp.exp(s - m)
                    l = jnp.sum(p, axis=-1, keepdims=True)
                    acc = jnp.dot(p.astype(BF16), v2, preferred_element_type=F32)
                    lrows = _block_rows(di * sup + loc, T, dil)
                    acc_s[lrows, :] = jnp.where(head0, acc[:T], acc[T:])
                    m_s[lrows, :] = jnp.where(head0, m[:T], m[T:])
                    l_s[lrows, :] = jnp.where(head0, l[:T], l[T:])
                    return c2

                lax.fori_loop(0, nblk, block, 0, unroll=8)

            chunk = min(sup, 256)

            def merge(ci, c2):
                lr = [pl.ds(pl.multiple_of(di * sup + ci * chunk, chunk), chunk) for di in range(nd)]
                gr = pl.ds(pl.multiple_of(row0 + ci * chunk, chunk), chunk)
                m0, m1, m2 = m_s[lr[0], :], m_s[lr[1], :], m_s[lr[2], :]
                mm = jnp.maximum(jnp.maximum(m0, m1), m2)
                w0, w1, w2 = jnp.exp(m0 - mm), jnp.exp(m1 - mm), jnp.exp(m2 - mm)
                den = (w0 * l_s[lr[0], :] + w1 * l_s[lr[1], :]) + w2 * l_s[lr[2], :]
                num = (w0 * acc_s[lr[0], :] + w1 * acc_s[lr[1], :]) + w2 * acc_s[lr[2], :]
                attn_ref[gr, :] = num / den
                lse_ref[gr, :] = mm + jnp.log(den)
                return c2

            lax.fori_loop(0, sup // chunk, merge, 0)
            return carry

        lax.fori_loop(0, S // sup, supertile, 0)

    blk = pl.BlockSpec((S, 128), lambda j: (0, j))
    out = jax.ShapeDtypeStruct((S, A), F32)
    return pl.pallas_call(
        body, name=name, grid=(A // 128,),
        in_specs=_qkv_specs(S), out_specs=[blk, blk], out_shape=[out, out],
        scratch_shapes=[pltpu.VMEM((nd * sup, 128), F32)] * 3,
        compiler_params=_params(("parallel",)),
    )(proj, proj, proj)


def _attn_out_bwd(dy, attn, gain, name):
    S, A = attn.shape
    tr = _row_tile(S, 256)

    def body(dy_ref, a_ref, g_ref, da_ref, dl_ref, dg_ref):
        @pl.when(pl.program_id(0) == 0)
        def _():
            dg_ref[...] = jnp.zeros_like(dg_ref)

        x = a_ref[...]
        dy_ = dy_ref[...]
        r = lax.rsqrt(jnp.mean(x * x, axis=-1, keepdims=True) + RMS_EPS)
        xh = x * r
        dg_ref[...] += jnp.sum(dy_ * xh, axis=0, keepdims=True)
        dxh = dy_ * g_ref[...]
        dx = r * (dxh - xh * jnp.mean(dxh * xh, axis=-1, keepdims=True))
        da_ref[...] = dx
        prod = dx * x
        hi = lax.broadcasted_iota(jnp.int32, (A, A), 0) // HEAD_DIM
        hj = lax.broadcasted_iota(jnp.int32, (A, A), 1) // HEAD_DIM
        same_head = (hi == hj).astype(F32)
        dl_ref[...] = jnp.dot(prod, same_head, preferred_element_type=F32, precision=lax.Precision.HIGHEST)

    row = pl.BlockSpec((tr, A), lambda i: (i, 0))
    vec = _vec_spec(A, 1)
    return pl.pallas_call(
        body, name=name, grid=(S // tr,),
        in_specs=[row, row, vec], out_specs=[row, row, vec],
        out_shape=[jax.ShapeDtypeStruct((S, A), F32), jax.ShapeDtypeStruct((S, A), F32),
                   jax.ShapeDtypeStruct((1, A), F32)],
        compiler_params=_params(("arbitrary",)),
    )(dy, attn, gain)


def _attn_bwd(proj, da, lse, delta, name):
    S, A = da.shape

    def body(q_ref, k_ref, v_ref, da_ref, lse_ref, dl_ref, dq_ref, dk_ref, dv_ref):
        lane = lax.broadcasted_iota(jnp.int32, (1, 128), 1)
        head0 = lane < HEAD_DIM
        dq_ref[...] = jnp.zeros_like(dq_ref)
        dk_ref[...] = jnp.zeros_like(dk_ref)
        dv_ref[...] = jnp.zeros_like(dv_ref)
        for dil in DILATIONS:
            T, nblk = _branch_blocks(S, dil)
            span = T * dil

            def block(idx, carry, dil=dil, T=T, span=span):
                r = idx % dil
                base = (idx // dil) * span + r
                rows = _block_rows(base, T, dil)
                prev = _block_rows(jnp.maximum(base - span, r), T, dil)
                qb, dab = q_ref[rows, :].astype(BF16), da_ref[rows, :].astype(BF16)
                k2 = jnp.concatenate([k_ref[prev, :], k_ref[rows, :]], axis=0).astype(BF16)
                v2 = jnp.concatenate([v_ref[prev, :], v_ref[rows, :]], axis=0).astype(BF16)
                lse_b, dl_b = lse_ref[rows, :], dl_ref[rows, :]
                valid = _band_mask(T, base >= span)
                valid2 = jnp.concatenate([valid, valid], axis=0)
                q2, da2 = _stack_heads(qb, head0), _stack_heads(dab, head0)
                lse2 = jnp.concatenate([lse_b[:, 0:1], lse_b[:, HEAD_DIM:HEAD_DIM + 1]], axis=0)
                dl2 = jnp.concatenate([dl_b[:, 0:1], dl_b[:, HEAD_DIM:HEAD_DIM + 1]], axis=0)
                s = lax.dot_general(q2, k2, NT, preferred_element_type=F32)
                p = jnp.where(valid2, jnp.exp(s - lse2), 0.0)
                dp = lax.dot_general(da2, v2, NT, preferred_element_type=F32)
                ds = (p * (dp - dl2)).astype(BF16)
                dq2 = jnp.dot(ds, k2, preferred_element_type=F32)
                dk2 = lax.dot_general(ds, q2, TN, preferred_element_type=F32)
                dv2 = lax.dot_general(p.astype(BF16), da2, TN, preferred_element_type=F32)
                dq_ref[rows, :] += jnp.where(head0, dq2[:T], dq2[T:])
                dk_ref[rows, :] += dk2[T:]
                dv_ref[rows, :] += dv2[T:]
                dk_ref[prev, :] += dk2[:T]
                dv_ref[prev, :] += dv2[:T]
                return carry

            lax.fori_loop(0, nblk, block, 0, unroll=8)

    blk = pl.BlockSpec((S, 128), lambda j: (0, j))
    out = jax.ShapeDtypeStruct((S, A), F32)
    return pl.pallas_call(
        body, name=name, grid=(A // 128,),
        in_specs=_qkv_specs(S) + [blk] * 3, out_specs=[blk] * 3, out_shape=[out] * 3,
        compiler_params=_params(("parallel",)),
    )(proj, proj, proj, da, lse, delta)


SUBLANES = 8
CONV_CHUNK = 64
FIRST_TAP = HALO - (CONV_KERNEL - 1)


def _store_shifted(shift_s, win, rows):
    shift_s[0, pl.ds(0, rows), :] = win
    for b in range(1, SUBLANES):
        shift_s[b, pl.ds(0, rows - SUBLANES), :] = win[b:b + rows - SUBLANES, :]


def _glu_window(a_ref, b_ref, ah_ref, bh_ref, first):
    u0 = a_ref[...] * _sigmoid(b_ref[...])
    u0h = ah_ref[...] * _sigmoid(bh_ref[...])
    u0h = jnp.where(first, jnp.zeros_like(u0h), u0h)
    return jnp.concatenate([u0h, u0], axis=0)


def _conv_norms(u1, lng, lnb):
    mu = jnp.mean(u1, axis=-1, keepdims=True)
    xc = u1 - mu
    rstd = lax.rsqrt(jnp.mean(xc * xc, axis=-1, keepdims=True) + LN_EPS)
    u1h = xc * rstd
    u2 = u1h * lng + lnb
    sig = _sigmoid(u2)
    u3 = u2 * sig
    r = lax.rsqrt(jnp.mean(u3 * u3, axis=-1, keepdims=True) + RMS_EPS)
    return rstd, u1h, u2, sig, u3, r


def _conv_specs(tr, C, col_a, col_b):
    per = tr // HALO

    def tile(col):
        return pl.BlockSpec((tr, C), lambda i: (i, col))

    def halo(col):
        return pl.BlockSpec((HALO, C), lambda i: (jnp.maximum(i * per - 1, 0), col))

    return tile(col_a), tile(col_b), halo(col_a), halo(col_b)


def _mixer_merge(proj, attn, cw, cb, lng, lnb, gat, gco, name):
    S = proj.shape[0]
    C = CONV_WIDTH
    A = attn.shape[1]
    tr = _row_tile(S, 256)

    def body(a_ref, b_ref, ah_ref, bh_ref, at_ref, w_ref, cb_ref, lng_ref, lnb_ref, gat_ref, gco_ref, u1_ref, y_ref,
             shift_s):
        _store_shifted(shift_s, _glu_window(a_ref, b_ref, ah_ref, bh_ref, pl.program_id(0) == 0), tr + HALO)

        def chunk(rc, carry):
            r0 = pl.multiple_of(rc * CONV_CHUNK, CONV_CHUNK)
            for lb in range(C // 128):
                lanes = slice(lb * 128, (lb + 1) * 128)
                acc = jnp.broadcast_to(cb_ref[:, lanes], (CONV_CHUNK, 128))
                for j in range(CONV_KERNEL):
                    a8, b = divmod(FIRST_TAP + j, SUBLANES)
                    acc = acc + w_ref[j:j + 1, lanes] * shift_s[b, pl.ds(r0 + a8 * SUBLANES, CONV_CHUNK), lanes]
                u1_ref[pl.ds(r0, CONV_CHUNK), lanes] = acc
            return carry

        lax.fori_loop(0, tr // CONV_CHUNK, chunk, 0)
        _, _, _, _, u3, r = _conv_norms(u1_ref[...], lng_ref[...], lnb_ref[...])
        y_ref[:, A:] = ((u3 * r) * gco_ref[...]).astype(BF16)
        x = at_ref[...]
        ra = lax.rsqrt(jnp.mean(x * x, axis=-1, keepdims=True) + RMS_EPS)
        y_ref[:, :A] = ((x * ra) * gat_ref[...]).astype(BF16)

    ta, tb, ha, hb = _conv_specs(tr, C, 3, 4)
    row = pl.BlockSpec((tr, C), lambda i: (i, 0))
    vec = _vec_spec(C, 1)
    return pl.pallas_call(
        body, name=name, grid=(S // tr,),
        in_specs=[ta, tb, ha, hb, pl.BlockSpec((tr, A), lambda i: (i, 0)), pl.BlockSpec((HALO, C), lambda i: (0, 0)),
                  vec, vec, vec, _vec_spec(A, 1), vec],
        out_specs=[row, pl.BlockSpec((tr, A + C), lambda i: (i, 0))],
        out_shape=[jax.ShapeDtypeStruct((S, C), F32), jax.ShapeDtypeStruct((S, A + C), BF16)],
        scratch_shapes=[pltpu.VMEM((SUBLANES, tr + HALO, C), F32)],
        compiler_params=_params(("parallel",)),
    )(proj, proj, proj, proj, attn, cw, cb, lng, lnb, gat, gco)


def _conv_bwd_norms(dy, u1, lng, lnb, gco, name):
    S, C = u1.shape
    tr = _row_tile(S, 256)

    def body(dy_ref, u1_ref, lng_ref, lnb_ref, gco_ref, du1_ref, dgco_ref, dlng_ref, dlnb_ref, dcb_ref):
        @pl.when(pl.program_id(0) == 0)
        def _():
            for ref in (dgco_ref, dlng_ref, dlnb_ref, dcb_ref):
                ref[...] = jnp.zeros_like(ref)

        lng = lng_ref[...]
        rstd, u1h, u2, sig, u3, r = _conv_norms(u1_ref[...], lng, lnb_ref[...])
        dy_ = dy_ref[...]
        u3h = u3 * r
        dgco_ref[...] += jnp.sum(dy_ * u3h, axis=0, keepdims=True)
        du3h = dy_ * gco_ref[...]
        du3 = r * (du3h - u3h * jnp.mean(du3h * u3h, axis=-1, keepdims=True))
        du2 = du3 * (sig * (1.0 + u2 * (1.0 - sig)))
        dlng_ref[...] += jnp.sum(du2 * u1h, axis=0, keepdims=True)
        dlnb_ref[...] += jnp.sum(du2, axis=0, keepdims=True)
        du1h = du2 * lng
        du1 = rstd * (du1h - jnp.mean(du1h, axis=-1, keepdims=True)
                      - u1h * jnp.mean(du1h * u1h, axis=-1, keepdims=True))
        du1_ref[...] = du1
        dcb_ref[...] += jnp.sum(du1, axis=0, keepdims=True)

    row = pl.BlockSpec((tr, C), lambda i: (i, 0))
    vec = _vec_spec(C, 1)
    return pl.pallas_call(
        body, name=name, grid=(S // tr,),
        in_specs=[pl.BlockSpec((tr, C), lambda i: (i, 1)), row, vec, vec, vec],
        out_specs=[row, vec, vec, vec, vec],
        out_shape=[jax.ShapeDtypeStruct((S, C), F32)] + [jax.ShapeDtypeStruct((1, C), F32)] * 4,
        compiler_params=_params(("arbitrary",)),
    )(dy, u1, lng, lnb, gco)


def _dproj(du1, proj, cw, dq, dk, dv, cos, sin, name):
    S, C = du1.shape
    A = dq.shape[1]
    tr = _row_tile(S, 256)
    nt = S // tr
    per = tr // HALO
    scale = HEAD_DIM ** -0.5

    def body(du_ref, dun_ref, a_ref, b_ref, ah_ref, bh_ref, w_ref, dq_ref, dk_ref, dv_ref, cos_ref, sin_ref,
             dp_ref, dw_ref, win_s, dwin_s, du0_s, tap_s):
        i = pl.program_id(0)

        @pl.when(i == 0)
        def _():
            dw_ref[...] = jnp.zeros_like(dw_ref)

        _store_shifted(win_s, _glu_window(a_ref, b_ref, ah_ref, bh_ref, i == 0), tr + HALO)
        nxt = jnp.where(i == nt - 1, jnp.zeros_like(dun_ref[...]), dun_ref[...])
        _store_shifted(dwin_s, jnp.concatenate([du_ref[...], nxt], axis=0), tr + HALO)
        tap_s[...] = jnp.zeros_like(tap_s)

        def chunk(rc, carry):
            r0 = pl.multiple_of(rc * CONV_CHUNK, CONV_CHUNK)
            for lb in range(C // 128):
                lanes = slice(lb * 128, (lb + 1) * 128)
                du = du_ref[pl.ds(r0, CONV_CHUNK), lanes]
                acc = jnp.zeros((CONV_CHUNK, 128), F32)
                for j in range(CONV_KERNEL):
                    a8, b = divmod(CONV_KERNEL - 1 - j, SUBLANES)
                    acc = acc + w_ref[j:j + 1, lanes] * dwin_s[b, pl.ds(r0 + a8 * SUBLANES, CONV_CHUNK), lanes]
                    a8, b = divmod(FIRST_TAP + j, SUBLANES)
                    prod = du * win_s[b, pl.ds(r0 + a8 * SUBLANES, CONV_CHUNK), lanes]
                    part = prod[0:SUBLANES]
                    for g in range(1, CONV_CHUNK // SUBLANES):
                        part = part + prod[g * SUBLANES:(g + 1) * SUBLANES]
                    tap_s[j * SUBLANES:(j + 1) * SUBLANES, lanes] += part
                du0_s[pl.ds(r0, CONV_CHUNK), lanes] = acc
            return carry

        lax.fori_loop(0, tr // CONV_CHUNK, chunk, 0)
        taps = [jnp.sum(tap_s[j * SUBLANES:(j + 1) * SUBLANES, :], axis=0, keepdims=True)
                for j in range(CONV_KERNEL)]
        taps.append(jnp.zeros((HALO - CONV_KERNEL, C), F32))
        dw_ref[...] += jnp.concatenate(taps, axis=0)
        du0 = du0_s[...]
        a, sig = a_ref[...], _sigmoid(b_ref[...])
        dp_ref[:, 3 * A:3 * A + C] = (du0 * sig).astype(BF16)
        dp_ref[:, 3 * A + C:] = (du0 * a * sig * (1.0 - sig)).astype(BF16)
        cos_, nsin = cos_ref[...], -sin_ref[...]
        for j in range(A // 128):
            lanes = slice(j * 128, (j + 1) * 128)
            dp_ref[:, j * 128:(j + 1) * 128] = (_rotate(dq_ref[:, lanes], cos_, nsin) * scale).astype(BF16)
            dp_ref[:, A + j * 128:A + (j + 1) * 128] = _rotate(dk_ref[:, lanes], cos_, nsin).astype(BF16)
        dp_ref[:, 2 * A:3 * A] = dv_ref[...].astype(BF16)

    ta, tb, ha, hb = _conv_specs(tr, C, 3, 4)
    row = pl.BlockSpec((tr, C), lambda i: (i, 0))
    arow = pl.BlockSpec((tr, A), lambda i: (i, 0))
    tab = pl.BlockSpec((tr, 128), lambda i: (i, 0))
    nxt = pl.BlockSpec((HALO, C), lambda i: (jnp.minimum((i + 1) * per, S // HALO - 1), 0))
    wspec = pl.BlockSpec((HALO, C), lambda i: (0, 0))
    return pl.pallas_call(
        body, name=name, grid=(nt,),
        in_specs=[row, nxt, ta, tb, ha, hb, wspec, arow, arow, arow, tab, tab],
        out_specs=[pl.BlockSpec((tr, 3 * A + 2 * C), lambda i: (i, 0)), wspec],
        out_shape=[jax.ShapeDtypeStruct((S, 3 * A + 2 * C), BF16), jax.ShapeDtypeStruct((HALO, C), F32)],
        scratch_shapes=[pltpu.VMEM((SUBLANES, tr + HALO, C), F32), pltpu.VMEM((SUBLANES, tr + HALO, C), F32),
                        pltpu.VMEM((tr, C), F32), pltpu.VMEM((HALO * SUBLANES, C), F32)],
        compiler_params=_params(("arbitrary",)),
    )(du1, du1, proj, proj, proj, proj, cw, dq, dk, dv, cos, sin)


def _ada_fwd(c_all, w, b, name):
    B, D = c_all.shape
    N = w.shape[1]
    tn = 768 if N % 768 == 0 else N

    def body(c_ref, w_ref, b_ref, o_ref):
        c = c_ref[...]
        a = (c * _sigmoid(c)).astype(BF16)
        o_ref[...] = jnp.dot(a, w_ref[...].astype(BF16), preferred_element_type=F32) + b_ref[...]

    return pl.pallas_call(
        body, name=name, grid=(N // tn,),
        in_specs=[pl.BlockSpec((B, D), lambda j: (0, 0)), pl.BlockSpec((D, tn), lambda j: (0, j)),
                  pl.BlockSpec((1, tn), lambda j: (0, j))],
        out_specs=pl.BlockSpec((B, tn), lambda j: (0, j)),
        out_shape=jax.ShapeDtypeStruct((B, N), F32),
        compiler_params=_params(("parallel",)),
    )(c_all, w, b)


def _ada_wgrad(c_t, dmod, name):
    D, B = c_t.shape
    N = dmod.shape[1]
    tn = 768 if N % 768 == 0 else N

    def body(c_ref, d_ref, o_ref):
        c = c_ref[...]
        a = (c * _sigmoid(c)).astype(BF16)
        o_ref[...] = jnp.dot(a, d_ref[...].astype(BF16), preferred_element_type=F32)

    return pl.pallas_call(
        body, name=name, grid=(N // tn,),
        in_specs=[pl.BlockSpec((D, B), lambda j: (0, 0)), pl.BlockSpec((B, tn), lambda j: (0, j))],
        out_specs=pl.BlockSpec((D, tn), lambda j: (0, j)),
        out_shape=jax.ShapeDtypeStruct((D, N), F32),
        compiler_params=_params(("parallel",)),
    )(c_t, dmod)


def _sum_rows(x, name):
    R, N = x.shape

    def body(x_ref, o_ref):
        acc = x_ref[0:1, :]
        for r in range(1, R):
            acc = acc + x_ref[r:r + 1, :]
        o_ref[...] = acc

    return pl.pallas_call(
        body, name=name, out_shape=jax.ShapeDtypeStruct((1, N), F32),
        compiler_params=_params(),
    )(x)


def _adamw(w, g, m, v, name, deps=()):
    R, C = w.shape
    tr = _row_tile(R, 256) if R % 256 == 0 else R
    bc1 = 1.0 - ADAM_B1 ** ADAM_STEP
    bc2 = 1.0 - ADAM_B2 ** ADAM_STEP
    deps = _deps(deps)

    def body(w_ref, g_ref, m_ref, v_ref, d_ref, mo_ref, vo_ref, go_ref):
        g_ = g_ref[...]
        m_ = ADAM_B1 * m_ref[...] + (1.0 - ADAM_B1) * g_
        v_ = ADAM_B2 * v_ref[...] + (1.0 - ADAM_B2) * (g_ * g_)
        mo_ref[...] = m_
        vo_ref[...] = v_
        go_ref[...] = g_
        d_ref[...] = -ADAM_LR * ((m_ / bc1) / (jnp.sqrt(v_ / bc2) + ADAM_EPS) + ADAM_WD * w_ref[...])

    row = pl.BlockSpec((tr, C), lambda i: (i, 0))
    out = jax.ShapeDtypeStruct((R, C), F32)
    return pl.pallas_call(
        _blind_to(body, 4, len(deps)), name=name, grid=(R // tr,),
        in_specs=[row] * 4 + [ANY_SPEC] * len(deps), out_specs=[row] * 4, out_shape=[out] * 4,
        compiler_params=_params(("parallel",), streaming=True),
    )(w, g, m, v, *deps)


def _coords():
    return lax.axis_index("x"), lax.axis_index("y"), lax.axis_index("c")


def _all_gather8(x, name, deps=()):
    R, N = x.shape
    assert R == 8
    flips = [(fx, fy, fc) for fx in (0, 1) for fy in (0, 1) for fc in (0, 1)][1:]
    deps = _deps(deps)

    def body(x_ref, o_ref, send_sems, recv_sems):
        mx, my, mc = _coords()
        me = 4 * mx + 2 * my + mc

        def rows(dev):
            return o_ref.at[pl.ds(pl.multiple_of(dev * R, R), R), :]

        o_ref[pl.ds(pl.multiple_of(me * R, R), R), :] = x_ref[...]
        copies = []
        for t, (fx, fy, fc) in enumerate(flips):
            peer = (mx ^ fx, my ^ fy, mc ^ fc)
            copies.append(pltpu.make_async_remote_copy(
                src_ref=x_ref, dst_ref=rows(me), send_sem=send_sems.at[t], recv_sem=recv_sems.at[t],
                device_id=peer, device_id_type=MESH))
        for cp in copies:
            cp.start()
        for t, (fx, fy, fc) in enumerate(flips):
            peer_id = 4 * (mx ^ fx) + 2 * (my ^ fy) + (mc ^ fc)
            pltpu.make_async_remote_copy(
                src_ref=x_ref, dst_ref=rows(peer_id), send_sem=send_sems.at[t], recv_sem=recv_sems.at[t],
                device_id=(mx ^ fx, my ^ fy, mc ^ fc), device_id_type=MESH).wait_recv()
        for cp in copies:
            cp.wait_send()

    return pl.pallas_call(
        _blind_to(body, 1, len(deps)), name=name,
        in_specs=[pl.BlockSpec(memory_space=pltpu.VMEM)] + [ANY_SPEC] * len(deps),
        out_specs=pl.BlockSpec(memory_space=pltpu.VMEM),
        out_shape=jax.ShapeDtypeStruct((N_DEV * R, N), F32),
        scratch_shapes=[pltpu.SemaphoreType.DMA((7,)), pltpu.SemaphoreType.DMA((7,))],
        compiler_params=pltpu.CompilerParams(has_side_effects=True, vmem_limit_bytes=VMEM_LIMIT_BYTES),
    )(x, *deps)


def _half_rows(rows, half):
    return pl.ds(pl.multiple_of(half * (rows // 2), 8), rows // 2)


def _split_start(bufs, plan, n, name):
    nb = len(bufs)

    def body(*refs):
        send_sems, recv_sems, token = refs[nb], refs[nb + 1], refs[-1]
        for t, (src, dst, dev) in enumerate(plan(refs[:nb])):
            pltpu.make_async_remote_copy(src_ref=src, dst_ref=dst, send_sem=send_sems.at[t],
                                         recv_sem=recv_sems.at[t], device_id=dev, device_id_type=MESH).start()
        token[...] = jnp.zeros_like(token)

    out = pl.pallas_call(
        body, name=name,
        out_shape=(pltpu.SemaphoreType.DMA((n,)), pltpu.SemaphoreType.DMA((n,)),
                   *[pltpu.HBM(b.shape, b.dtype) for b in bufs], jax.ShapeDtypeStruct((8, 128), F32)),
        in_specs=[HBM_SPEC] * nb,
        out_specs=(SEM_SPEC, SEM_SPEC, *[HBM_SPEC] * nb, pl.BlockSpec(memory_space=pltpu.VMEM)),
        input_output_aliases={i: 2 + i for i in range(nb)},
        compiler_params=pltpu.CompilerParams(has_side_effects=DATAFLOW),
    )(*[pltpu.with_memory_space_constraint(b, pltpu.HBM) for b in bufs])
    return out[0], out[1], list(out[2:2 + nb]), out[-1]


def _split_wait(bufs, send_sems, recv_sems, plan, after, name):
    nb = len(bufs)
    after = _deps(after)

    def body(*refs):
        ss, rs = refs[nb], refs[nb + 1]
        for t, (src, dst, dev) in enumerate(plan(refs[:nb])):
            cp = pltpu.make_async_remote_copy(src_ref=src, dst_ref=dst, send_sem=ss.at[t], recv_sem=rs.at[t],
                                              device_id=dev, device_id_type=MESH)
            cp.wait_send()
            cp.wait_recv()

    out = pl.pallas_call(
        body, name=name,
        out_shape=tuple(pltpu.HBM(b.shape, b.dtype) for b in bufs),
        in_specs=[HBM_SPEC] * nb + [SEM_SPEC, SEM_SPEC] + [ANY_SPEC] * len(after),
        out_specs=tuple([HBM_SPEC] * nb),
        input_output_aliases={i: i for i in range(nb)},
        compiler_params=pltpu.CompilerParams(has_side_effects=DATAFLOW),
    )(*bufs, send_sems, recv_sems, *after)
    return list(out)


class _Exchange:
    def __init__(self, bufs, plan, n, name):
        self.plan, self.name = plan, name
        self.send_sems, self.recv_sems, self.bufs, self.token = _split_start(bufs, plan, n, name + "_start")

    def wait(self, after):
        return _split_wait(self.bufs, self.send_sems, self.recv_sems, self.plan, after, self.name + "_wait")


def _cast_place(w, chip_idx, name, deps=()):
    R, C = w.shape
    tr = _row_tile(R, 256) if R % 256 == 0 else R
    deps = _deps(deps)

    def body(k_ref, w_ref, o_ref):
        o_ref[...] = w_ref[...].astype(BF16)

    grid_spec = pltpu.PrefetchScalarGridSpec(
        num_scalar_prefetch=1, grid=(R // tr,),
        in_specs=[pl.BlockSpec((tr, C), lambda i, k: (i, 0))] + [ANY_SPEC] * len(deps),
        out_specs=pl.BlockSpec((None, tr, C), lambda i, k: (k[0], i, 0)),
    )
    return pl.pallas_call(
        _blind_to(body, 2, len(deps)), name=name, grid_spec=grid_spec,
        out_shape=jax.ShapeDtypeStruct((N_CHIPS, R, C), BF16),
        compiler_params=_params(("parallel",), streaming=True),
    )(chip_idx, w, *deps)


def _plan_gather(refs):
    mx, my, mc = _coords()
    me = 2 * mx + my
    plan = []
    for g in refs:
        mine = g.at[me, _half_rows(g.shape[1], mc), :]
        for fx, fy in XY_FLIPS:
            plan.append((mine, mine, (mx ^ fx, my ^ fy, mc)))
    return plan


def _plan_swap(refs):
    mx, my, mc = _coords()
    plan = []
    for g in refs:
        for fx, fy in XY_FLIPS:
            have = g.at[2 * (mx ^ fx) + (my ^ fy), _half_rows(g.shape[1], mc), :]
            plan.append((have, have, (mx, my, 1 - mc)))
    return plan


def _plan_other_halves(refs):
    n = len(refs) // 2
    mx, my, mc = _coords()
    return [(g.at[pl.ds(0, N_CHIPS), _half_rows(g.shape[1], 1 - mc), :], land, (mx, my, 1 - mc))
            for g, land in zip(refs[:n], refs[n:])]


def _plan_chunks(refs):
    n = len(refs) // 2
    mx, my, mc = _coords()
    plan = []
    for s, land in zip(refs[:n], refs[n:]):
        for t, (fx, fy) in enumerate(XY_FLIPS):
            plan.append((s.at[2 * (mx ^ fx) + (my ^ fy)], land.at[t], (mx ^ fx, my ^ fy, mc)))
    return plan


def _plan_share(refs):
    mx, my, mc = _coords()
    return [(full.at[mc], full.at[mc], (mx, my, 1 - mc)) for full in refs]


def _add_half(g, recv, core_idx, name):
    nk, R, C = g.shape
    rh = R // 2
    tr = _row_tile(rh, 128) if rh % 128 == 0 else rh
    nt = rh // tr

    def body(c_ref, g_ref, r_ref, o_ref):
        o_ref[...] = (g_ref[...] + r_ref[...]).astype(BF16)

    grid_spec = pltpu.PrefetchScalarGridSpec(
        num_scalar_prefetch=1, grid=(nk, nt),
        in_specs=[pl.BlockSpec((None, tr, C), lambda k, i, c: (k, c[0] * nt + i, 0)),
                  pl.BlockSpec((None, tr, C), lambda k, i, c: (k, i, 0))],
        out_specs=pl.BlockSpec((None, tr, C), lambda k, i, c: (k, i, 0)),
    )
    return pl.pallas_call(
        body, name=name, grid_spec=grid_spec, out_shape=jax.ShapeDtypeStruct((nk, rh, C), BF16),
        compiler_params=_params(("parallel", "parallel"), streaming=True),
    )(core_idx, g, recv)


def _sum_chips(s, land, chip_core, name):
    _, rh, C = s.shape
    tr = _row_tile(rh, 128) if rh % 128 == 0 else rh

    def body(p_ref, s_ref, l_ref, o_ref):
        me = p_ref[0]
        acc = None
        for j in range(N_CHIPS):
            t = jnp.maximum(jnp.bitwise_xor(me, j) - 1, 0)
            term = jnp.where(me == j, s_ref[...], l_ref[t]).astype(F32)
            acc = term if acc is None else acc + term
        o_ref[...] = acc

    grid_spec = pltpu.PrefetchScalarGridSpec(
        num_scalar_prefetch=1, grid=(rh // tr,),
        in_specs=[pl.BlockSpec((None, tr, C), lambda i, p: (p[0], i, 0)),
                  pl.BlockSpec((3, tr, C), lambda i, p: (0, i, 0))],
        out_specs=pl.BlockSpec((None, tr, C), lambda i, p: (p[1], i, 0)),
    )
    return pl.pallas_call(
        body, name=name, grid_spec=grid_spec, out_shape=jax.ShapeDtypeStruct((2, rh, C), F32),
        compiler_params=_params(("parallel",), streaming=True),
    )(chip_core, s, land)


def _rs_send_halves(grads, tag):
    lands = [lax.empty((g.shape[0], g.shape[1] // 2, g.shape[2]), g.dtype) for g in grads]
    return _Exchange(list(grads) + lands, _plan_other_halves, len(grads), f"rs_halves_{tag}")


def _rs_send_chunks(ex, after, core_idx, tag):
    bufs = ex.wait(after)
    n = len(bufs) // 2
    sums = [_add_half(g, r, core_idx, f"rs_add_{tag}_{i}") for i, (g, r) in enumerate(zip(bufs[:n], bufs[n:]))]
    lands = [lax.empty((3,) + s.shape[1:], s.dtype) for s in sums]
    return _Exchange(sums + lands, _plan_chunks, 3 * n, f"rs_chunks_{tag}")


def _rs_send_share(ex, after, chip_core, tag):
    bufs = ex.wait(after)
    n = len(bufs) // 2
    fulls = [_sum_chips(s, l, chip_core, f"rs_sum_{tag}_{i}") for i, (s, l) in enumerate(zip(bufs[:n], bufs[n:]))]
    return _Exchange(fulls, _plan_share, n, f"rs_share_{tag}")


def _rs_finish(ex, after):
    return [b.reshape(2 * b.shape[1], b.shape[2]) for b in ex.wait(after)]


def _ffn_forward(h, n, gate, wg, wu, wd, tag, next_norm=None):
    ga, up, act = _ffn_gate_up(n, wg, wu, f"{tag}_gate_up")
    h_out, f, *n_next = _mm_residual(act, wd, h, gate, 0.5, f"{tag}_down", next_norm=next_norm)
    return (h_out, *n_next), (h, n, ga, up, act, f)


def _ffn_backward(dh_out, df, saved, gain, sc, wg, wu, wd, core_idx, tag, prev=None, last=False):
    h, n, ga, up, act, _ = saved
    (dwd,) = _wgrad_chunk_lhs([act], df, f"{tag}_dwd")
    ex_d = _rs_send_halves([dwd], f"{tag}_d")
    dga, dup = _ffn_dact(df, wd, ga, up, f"{tag}_dact", deps=[ex_d.token])
    ex_d = _rs_send_chunks(ex_d, [dga], core_idx, f"{tag}_d")
    dwg, dwu = _wgrad_chunk_lhs([dga, dup], n, f"{tag}_dwgu", deps=[ex_d.token])
    ex_gu = _rs_send_halves([dwg, dwu], f"{tag}_gu")
    dn = _mm_reduce([dga, dup], [wg, wu], True, True, f"{tag}_dn", deps=[ex_gu.token])
    if not last:
        ex_gu = _rs_send_chunks(ex_gu, [dn], core_idx, f"{tag}_gu")
    outs = _norm_mod_bwd(dn, h, gain, sc, dh_out, f"{tag}_norm_bwd", prev=prev, deps=[ex_gu.token])
    return outs, (ex_d, ex_gu)


def _pad_cols(v, n):
    return jnp.pad(v, ((0, 0), (0, n - v.shape[1])))


def _mixer_forward(h1, n2, gt2, win, wout, conv_w, conv_dw_b, conv_ln_g, conv_ln_b, attn_out_g, conv_out_g,
                   next_norm=None):
    S, D = h1.shape
    cos, sin = _rope_tables(S)
    proj = _mm_cols_rope(n2, win, cos, sin, "mix_in")
    attn, lse = _attn_fwd(proj, "attn_fwd")
    u1, y = _mixer_merge(proj, attn, conv_w, conv_dw_b, conv_ln_g, conv_ln_b, attn_out_g, conv_out_g, "mix_merge")
    h2, mo, *n_next = _mm_residual(y[None], wout.reshape(1, D, D), h1, gt2, 1.0, "mix_out", next_norm=next_norm)
    return (h2, *n_next), (h1, n2, proj, cos, sin, attn, lse, u1, y, mo)


def _mixer_backward(dh2, dmo, saved, mix_norm_g, sc2, win, wout, conv_w, conv_ln_g, conv_ln_b, attn_out_g,
                    conv_out_g, core_idx, prev=None):
    h1, n2, proj, cos, sin, attn, lse, u1, y, _ = saved
    S, D = h1.shape
    (dwout,) = _wgrad_chunk_lhs([y[None]], dmo, "mix_dwout")
    dy = _mm_nt(dmo, wout.reshape(D, D), "mix_dy")
    dattn, delta, d_attn_g = _attn_out_bwd(dy, attn, attn_out_g, "attn_out_bwd")
    dq, dk, dv = _attn_bwd(proj, dattn, lse, delta, "attn_bwd")
    du1, d_gco, d_lng, d_lnb, d_cb = _conv_bwd_norms(dy, u1, conv_ln_g, conv_ln_b, conv_out_g, "conv_bwd_norms")
    dproj, d_cw = _dproj(du1, proj, conv_w, dq, dk, dv, cos, sin, "mix_dproj")
    (dwin,) = _wgrad_chunk_rhs(n2, [dproj], N_CHIPS, False, "mix_dwin")
    ex = _rs_send_halves([dwin, dwout.reshape(N_CHIPS, D // N_CHIPS, D)], "mix")
    dn2 = _mm_reduce([dproj], [win], False, False, "mix_dn", deps=[ex.token])
    ex = _rs_send_chunks(ex, [dn2], core_idx, "mix")
    outs = _norm_mod_bwd(dn2, h1, mix_norm_g, sc2, dh2, "mix_norm_bwd", prev=prev, deps=[ex.token])
    return outs, ex, (d_cb, d_lng, d_lnb, d_attn_g, d_gco, d_cw)


def kernel(x, c, w_ada, b_ada, ffn1_norm_g, ffn1_w_gate, ffn1_w_up, ffn1_w_down, mix_norm_g, w_in, conv_dw_w, conv_dw_b, conv_ln_g, conv_ln_b, attn_out_g, conv_out_g, w_out, ffn2_norm_g, ffn2_w_gate, ffn2_w_up, ffn2_w_down, final_norm_g, loss_target, m_w_ada, m_b_ada, m_ffn1_norm_g, m_ffn1_w_gate, m_ffn1_w_up, m_ffn1_w_down, m_mix_norm_g, m_w_in, m_conv_dw_w, m_conv_dw_b, m_conv_ln_g, m_conv_ln_b, m_attn_out_g, m_conv_out_g, m_w_out, m_ffn2_norm_g, m_ffn2_w_gate, m_ffn2_w_up, m_ffn2_w_down, m_final_norm_g, v_w_ada, v_b_ada, v_ffn1_norm_g, v_ffn1_w_gate, v_ffn1_w_up, v_ffn1_w_down, v_mix_norm_g, v_w_in, v_conv_dw_w, v_conv_dw_b, v_conv_ln_g, v_conv_ln_b, v_attn_out_g, v_conv_out_g, v_w_out, v_ffn2_norm_g, v_ffn2_w_gate, v_ffn2_w_up, v_ffn2_w_down, v_final_norm_g):
    S, D = x.shape[1], x.shape[2]
    mx, my, mc = _coords()
    chip = 2 * mx + my
    dev = 4 * mx + 2 * my + mc
    chip_idx = chip.astype(jnp.int32).reshape(1)
    core_idx = mc.astype(jnp.int32).reshape(1)
    chip_core = jnp.stack([chip, mc]).astype(jnp.int32)
    h0 = x[0]
    target = loss_target[0]

    ncw = CONV_KERNEL * 128
    n0 = -(-(D + ncw) // 1024) * 1024
    pk0 = _pad_cols(jnp.concatenate([c.reshape(1, D), conv_dw_w.reshape(1, ncw)], axis=1), n0)
    g0 = _all_gather8(pk0.reshape(8, n0 // 8), "gather_c").reshape(N_DEV, n0)
    c_all = g0[:, :D]
    conv_w = jnp.concatenate([g0[2 * kc, D:D + ncw].reshape(CONV_KERNEL, 128) for kc in range(N_CHIPS)], axis=1)
    conv_w = jnp.pad(conv_w, ((0, HALO - CONV_KERNEL), (0, 0)))
    nmod = w_ada.shape[2]
    b_shard = lax.dynamic_slice(b_ada, (0, chip * nmod), (1, nmod))
    mod_part = _ada_fwd(c_all, w_ada[0], b_shard, "ada_fwd")
    g1 = _all_gather8(mod_part, "gather_mod")
    mod_all = jnp.concatenate([g1[16 * kc:16 * kc + 8] for kc in range(N_CHIPS)], axis=1)
    mod = lax.dynamic_slice(mod_all, (dev, 0), (1, 9 * D))
    sh1, sc1, gt1, sh2, sc2, gt2, sh3, sc3, gt3 = [mod[:, i * D:(i + 1) * D] for i in range(9)]

    def gather_start(ws, tag, dep):
        slots = [_cast_place(w, chip_idx, f"cast_{tag}_{i}", deps=[dep]) for i, w in enumerate(ws)]
        return _Exchange(slots, _plan_gather, 3 * len(ws), f"gather_{tag}")

    def swap_start(ex, after, tag):
        return _Exchange(ex.wait(after), _plan_swap, 3 * len(ex.bufs), f"swap_{tag}")

    ex_gu1 = gather_start([ffn1_w_gate[0].T, ffn1_w_up[0].T], "ffn1_gu", g1)
    n1 = _norm_mod(h0, ffn1_norm_g, sc1, sh1, "ffn1_norm", deps=[ex_gu1.token])
    ex_d1 = gather_start([ffn1_w_down[0]], "ffn1_d", n1)
    ex_wm = gather_start([w_in[0], w_out[0]], "mix", ex_d1.token)
    ex_w2 = gather_start([ffn2_w_gate[0].T, ffn2_w_up[0].T, ffn2_w_down[0]], "ffn2", ex_wm.token)

    wg1, wu1 = swap_start(ex_gu1, [ex_w2.token], "ffn1_gu").wait([])
    ga1, up1, act1 = _ffn_gate_up(n1, wg1, wu1, "ffn1_gate_up")
    (wd1,) = swap_start(ex_d1, [act1], "ffn1_d").wait([])
    ex_wm = swap_start(ex_wm, [wd1], "mix")
    h1, f1, n2 = _mm_residual(act1, wd1, h0, gt1, 0.5, "ffn1_down", next_norm=(mix_norm_g, sc2, sh2))
    saved1 = (h0, n1, ga1, up1, act1, f1)
    win, wout = ex_wm.wait([h1])
    ex_w2 = swap_start(ex_w2, [h1], "ffn2")
    (h2, n3), saved2 = _mixer_forward(h1, n2, gt2, win, wout, conv_w, conv_dw_b, conv_ln_g, conv_ln_b,
                                      attn_out_g, conv_out_g, next_norm=(ffn2_norm_g, sc3, sh3))
    wg2, wu2, wd2 = ex_w2.wait([h2])
    (h3,), saved3 = _ffn_forward(h2, n3, gt3, wg2, wu2, wd2, "ffn2")
    loss_part, dh3, d_final_g, df3, d_gt3 = _loss_head(h3, final_norm_g.reshape(1, D), target, saved3[5], gt3, 0.5,
                                                       "loss_head")

    (dh2, d_sh3, d_sc3, d_gain3, dmo, d_gt2), (ex_d2, ex_gu2) = _ffn_backward(
        dh3, df3, saved3, ffn2_norm_g, sc3, wg2, wu2, wd2, core_idx, "ffn2", prev=(saved2[-1], gt2, 1.0))
    (dh1, d_sh2, d_sc2, d_gain2, df1, d_gt1), ex_mix, small_mix = _mixer_backward(
        dh2, dmo, saved2, mix_norm_g, sc2, win, wout, conv_w, conv_ln_g, conv_ln_b, attn_out_g, conv_out_g, core_idx,
        prev=(f1, gt1, 0.5))
    d_cb, d_lng, d_lnb, d_attn_g, d_gco, d_cw = small_mix
    (dh0, d_sh1, d_sc1, d_gain1), (ex_d1, ex_gu1) = _ffn_backward(
        dh1, df1, saved1, ffn1_norm_g, sc1, wg1, wu1, wd1, core_idx, "ffn1", last=True)

    dmod = jnp.concatenate([d_sh1, d_sc1, d_gt1, d_sh2, d_sc2, d_gt2, d_sh3, d_sc3, d_gt3], axis=1)
    small = [d_gain1, d_gain2, d_gain3, d_final_g, d_cb, d_lng, d_lnb, d_attn_g, d_gco,
             d_cw[:CONV_KERNEL].reshape(1, CONV_KERNEL * CONV_WIDTH), loss_part]
    pk1 = jnp.concatenate([dmod] + small, axis=1)
    n1_ = -(-pk1.shape[1] // 1024) * 1024
    gathered = _all_gather8(_pad_cols(pk1, n1_).reshape(8, n1_ // 8), "gather_small").reshape(N_DEV, n1_)
    ex_gu1 = _rs_send_chunks(ex_gu1, [gathered], core_idx, "ffn1_gu")
    tot = _sum_rows(gathered, "sum_small")
    off = [0]

    def take(nel):
        out = tot[:, off[0]:off[0] + nel]
        off[0] += nel
        return out

    g_b_ada = take(9 * D)
    g_ffn1_norm, g_mix_norm, g_ffn2_norm, g_final = take(D), take(D), take(D), take(D)
    g_cb, g_lng, g_lnb, g_attn_g, g_gco = take(512), take(512), take(512), take(512), take(512)
    g_cw_full = take(CONV_KERNEL * CONV_WIDTH).reshape(CONV_KERNEL, CONV_WIDTH)
    loss = take(128)[0, 0]
    g_cw = lax.dynamic_slice(g_cw_full, (0, chip * 128), (CONV_KERNEL, 128))

    dmod_shard = lax.dynamic_slice(gathered[:, :9 * D], (0, chip * nmod), (N_DEV, nmod))
    dmod16 = jnp.pad(dmod_shard, ((0, N_DEV), (0, 0)))
    c_t16 = jnp.pad(c_all.T, ((0, 0), (0, N_DEV)))
    g_w_ada = _ada_wgrad(c_t16, dmod16, "ada_wgrad")

    names = ["w_ada", "b_ada", "ffn1_norm_g", "ffn1_w_gate", "ffn1_w_up", "ffn1_w_down", "mix_norm_g", "w_in",
             "conv_dw_w", "conv_dw_b", "conv_ln_g", "conv_ln_b", "attn_out_g", "conv_out_g", "w_out", "ffn2_norm_g",
             "ffn2_w_gate", "ffn2_w_up", "ffn2_w_down", "final_norm_g"]
    weights = dict(zip(names, [w_ada, b_ada, ffn1_norm_g, ffn1_w_gate, ffn1_w_up, ffn1_w_down, mix_norm_g, w_in,
                               conv_dw_w, conv_dw_b, conv_ln_g, conv_ln_b, attn_out_g, conv_out_g, w_out,
                               ffn2_norm_g, ffn2_w_gate, ffn2_w_up, ffn2_w_down, final_norm_g]))
    ms = dict(zip(names, [m_w_ada, m_b_ada, m_ffn1_norm_g, m_ffn1_w_gate, m_ffn1_w_up, m_ffn1_w_down, m_mix_norm_g,
                          m_w_in, m_conv_dw_w, m_conv_dw_b, m_conv_ln_g, m_conv_ln_b, m_attn_out_g, m_conv_out_g,
                          m_w_out, m_ffn2_norm_g, m_ffn2_w_gate, m_ffn2_w_up, m_ffn2_w_down, m_final_norm_g]))
    vs = dict(zip(names, [v_w_ada, v_b_ada, v_ffn1_norm_g, v_ffn1_w_gate, v_ffn1_w_up, v_ffn1_w_down, v_mix_norm_g,
                          v_w_in, v_conv_dw_w, v_conv_dw_b, v_conv_ln_g, v_conv_ln_b, v_attn_out_g, v_conv_out_g,
                          v_w_out, v_ffn2_norm_g, v_ffn2_w_gate, v_ffn2_w_up, v_ffn2_w_down, v_final_norm_g]))
    grads, deltas, new_ms, new_vs = {}, {}, {}, {}

    def adamw_big(nm, g2d, deps=(), transposed=False):
        shape = weights[nm].shape
        two_d = (shape[-2], shape[-1])

        def view(t):
            return t.reshape(two_d).T if transposed else t.reshape(two_d)

        d_, m_, v_, g_ = _adamw(view(weights[nm]), g2d, view(ms[nm]), view(vs[nm]), f"adamw_{nm}", deps=deps)
        grads[nm], deltas[nm], new_ms[nm], new_vs[nm] = (
            (t.T if transposed else t).reshape(shape) for t in (g_, d_, m_, v_))
        return d_

    d_ada = adamw_big("w_ada", g_w_ada, deps=[ex_gu1.token])
    small_grads = {"b_ada": g_b_ada, "ffn1_norm_g": g_ffn1_norm, "mix_norm_g": g_mix_norm, "conv_dw_w": g_cw,
                   "conv_dw_b": g_cb, "conv_ln_g": g_lng, "conv_ln_b": g_lnb, "attn_out_g": g_attn_g,
                   "conv_out_g": g_gco, "ffn2_norm_g": g_ffn2_norm, "final_norm_g": g_final}
    small_names = [nm for nm in names if nm in small_grads]

    def pack_small(arrs):
        flat = jnp.concatenate([arrs[nm].reshape(1, -1) for nm in small_names], axis=1)
        npad = -(-flat.shape[1] // 1024) * 1024
        return _pad_cols(flat, npad).reshape(8, npad // 8)

    d_s, m_s, v_s, _ = _adamw(pack_small(weights), pack_small(small_grads), pack_small(ms), pack_small(vs),
                           "adamw_small")
    pos = 0
    for nm in small_names:
        shape, nel = weights[nm].shape, weights[nm].size
        grads[nm] = small_grads[nm].reshape(shape)
        deltas[nm], new_ms[nm], new_vs[nm] = (t.reshape(1, -1)[:, pos:pos + nel].reshape(shape)
                                              for t in (d_s, m_s, v_s))
        pos += nel

    ex_d2 = _rs_send_share(ex_d2, [d_ada, d_s], chip_core, "ffn2_d")
    ex_gu2 = _rs_send_share(ex_gu2, [ex_d2.token], chip_core, "ffn2_gu")
    ex_mix = _rs_send_share(ex_mix, [ex_gu2.token], chip_core, "mix")
    ex_d1 = _rs_send_share(ex_d1, [ex_mix.token], chip_core, "ffn1_d")
    (g_wd2,) = _rs_finish(ex_d2, [ex_d1.token])
    last = [adamw_big("ffn2_w_down", g_wd2)]
    g_wg2, g_wu2 = _rs_finish(ex_gu2, last)
    last = [adamw_big("ffn2_w_gate", g_wg2, transposed=True), adamw_big("ffn2_w_up", g_wu2, transposed=True)]
    g_win, g_wout = _rs_finish(ex_mix, last)
    last = [adamw_big("w_in", g_win), adamw_big("w_out", g_wout)]
    (g_wd1,) = _rs_finish(ex_d1, last)
    last = [adamw_big("ffn1_w_down", g_wd1)]
    ex_gu1 = _rs_send_share(ex_gu1, last, chip_core, "ffn1_gu")
    g_wg1, g_wu1 = _rs_finish(ex_gu1, [])
    adamw_big("ffn1_w_gate", g_wg1, transposed=True)
    adamw_big("ffn1_w_up", g_wu1, transposed=True)

    return (loss, dh0[None], *[grads[nm] for nm in names], *[deltas[nm] for nm in names],
            *[new_ms[nm] for nm in names], *[new_vs[nm] for nm in names])
```

```python
import jax
import jax.numpy as jnp
import numpy as np
from jax import lax
from jax.experimental import pallas as pl
from jax.experimental.pallas import tpu as pltpu

F32 = jnp.float32
BF16 = jnp.bfloat16
MESH = pl.DeviceIdType.MESH

RMS_EPS = 1e-6
LN_EPS = 1e-5
HEAD_DIM = 64
ATTN_WIDTH = 512
CONV_WIDTH = 512
ATTN_BLOCK = 128
DILATIONS = (1, 4, 16)
SUPER_ROWS = ATTN_BLOCK * 16
ROPE_THETA = 10000.0
CONV_KERNEL = 31
HALO = 32
N_CHIPS = 4
N_DEV = 8
ADAM_LR, ADAM_B1, ADAM_B2, ADAM_EPS, ADAM_WD, ADAM_STEP = 0.001, 0.9, 0.999, 1e-08, 0.01, 10
VMEM_LIMIT_BYTES = 48 * 1024 * 1024
VMEM_LIMIT_STREAMING = 62 * 1024 * 1024
NEG = -1e30

NT = (((1,), (1,)), ((), ()))
TN = (((0,), (0,)), ((), ()))

ANY_SPEC = pl.BlockSpec(memory_space=pl.ANY)
HBM_SPEC = pl.BlockSpec(memory_space=pltpu.HBM)
SEM_SPEC = pl.BlockSpec(memory_space=pltpu.SEMAPHORE)
DATAFLOW = pltpu.SideEffectType.DATAFLOW_SIDE_EFFECTING
XY_FLIPS = ((0, 1), (1, 0), (1, 1))


def _params(sem=None, streaming=False):
    return pltpu.CompilerParams(dimension_semantics=sem,
                                vmem_limit_bytes=VMEM_LIMIT_STREAMING if streaming else VMEM_LIMIT_BYTES)


def _row_tile(rows, want):
    t = min(rows, want)
    assert rows % t == 0
    return t


def _sigmoid(x):
    return 1.0 / (1.0 + jnp.exp(-x))


def _deps(deps):
    return [d for d in deps if d is not None]


def _blind_to(body, n_in, n_dep):
    def wrapped(*refs):
        return body(*refs[:n_in], *refs[n_in + n_dep:])
    return wrapped


def _vec_spec(d, ngrid):
    if ngrid == 1:
        return pl.BlockSpec((1, d), lambda i: (0, 0))
    return pl.BlockSpec((1, d), lambda i, j: (0, 0))


def _norm_mod(h, gain, sc, sh, name, deps=()):
    S, D = h.shape
    tr = _row_tile(S, 512)
    deps = _deps(deps)

    def body(h_ref, g_ref, sc_ref, sh_ref, n_ref):
        x = h_ref[...]
        r = lax.rsqrt(jnp.mean(x * x, axis=-1, keepdims=True) + RMS_EPS)
        y = (x * r) * g_ref[...]
        n_ref[...] = (y * (1.0 + sc_ref[...]) + sh_ref[...]).astype(BF16)

    row = pl.BlockSpec((tr, D), lambda i: (i, 0))
    return pl.pallas_call(
        _blind_to(body, 4, len(deps)), name=name, grid=(S // tr,),
        in_specs=[row, _vec_spec(D, 1), _vec_spec(D, 1), _vec_spec(D, 1)] + [ANY_SPEC] * len(deps),
        out_specs=row, out_shape=jax.ShapeDtypeStruct((S, D), BF16),
        compiler_params=_params(("parallel",)),
    )(h, gain, sc, sh, *deps)


def _norm_mod_bwd(dn, h_in, gain, sc, dh_out, name, prev=None, deps=()):
    S, D = h_in.shape
    tr = _row_tile(S, 512)
    deps = _deps(deps)
    n_in = 5 if prev is None else 7

    def body(*refs):
        dn_ref, h_ref, g_ref, sc_ref, dho_ref = refs[:5]
        dh_ref, dsh_ref, dsc_ref, dg_ref = refs[n_in:n_in + 4]

        @pl.when(pl.program_id(0) == 0)
        def _():
            for ref in refs[n_in + 1:n_in + 4] + refs[n_in + 5:]:
                ref[...] = jnp.zeros_like(ref)

        x = h_ref[...]
        dn_ = dn_ref[...]
        g = g_ref[...]
        one_sc = 1.0 + sc_ref[...]
        r = lax.rsqrt(jnp.mean(x * x, axis=-1, keepdims=True) + RMS_EPS)
        xh = x * r
        dsh_ref[...] += jnp.sum(dn_, axis=0, keepdims=True)
        dsc_ref[...] += jnp.sum(dn_ * (xh * g), axis=0, keepdims=True)
        dg_ref[...] += jnp.sum(dn_ * one_sc * xh, axis=0, keepdims=True)
        dxh = dn_ * (g * one_sc)
        dh = dho_ref[...] + r * (dxh - xh * jnp.mean(dxh * xh, axis=-1, keepdims=True))
        dh_ref[...] = dh
        if prev is not None:
            _gate_back(dh, refs[5], refs[6], prev[2], refs[n_in + 4], refs[n_in + 5])

    row = pl.BlockSpec((tr, D), lambda i: (i, 0))
    vec = _vec_spec(D, 1)
    extra_in = [] if prev is None else [row, vec]
    extra_out = [] if prev is None else [row, vec]
    extra_shape = [] if prev is None else [jax.ShapeDtypeStruct((S, D), BF16), jax.ShapeDtypeStruct((1, D), F32)]
    return pl.pallas_call(
        _blind_to(body, n_in, len(deps)), name=name, grid=(S // tr,),
        in_specs=[row, row, vec, vec, row] + extra_in + [ANY_SPEC] * len(deps),
        out_specs=[row, vec, vec, vec] + extra_out,
        out_shape=[jax.ShapeDtypeStruct((S, D), F32)] + [jax.ShapeDtypeStruct((1, D), F32)] * 3 + extra_shape,
        compiler_params=_params(("arbitrary",)),
    )(dn, h_in, gain, sc, dh_out, *([] if prev is None else prev[:2]), *deps)


def _gate_back(dh, f_ref, gate_ref, coef, df_ref, dgate_ref):
    df_ref[...] = ((coef * gate_ref[...]) * dh).astype(BF16)
    dgate_ref[...] += jnp.sum(coef * dh * f_ref[...].astype(F32), axis=0, keepdims=True)


def _loss_head(h, gain, target, f, gate, coef, name):
    S, D = h.shape
    tr = _row_tile(S, 512)

    def body(h_ref, g_ref, t_ref, f_ref, gate_ref, loss_ref, dh_ref, dg_ref, df_ref, dgate_ref):
        @pl.when(pl.program_id(0) == 0)
        def _():
            loss_ref[...] = jnp.zeros_like(loss_ref)
            dg_ref[...] = jnp.zeros_like(dg_ref)
            dgate_ref[...] = jnp.zeros_like(dgate_ref)

        x = h_ref[...]
        g = g_ref[...]
        r = lax.rsqrt(jnp.mean(x * x, axis=-1, keepdims=True) + RMS_EPS)
        xh = x * r
        err = xh * g - t_ref[...]
        part = 0.5 * jnp.sum(jnp.mean(err * err, axis=-1, keepdims=True), axis=0, keepdims=True)
        loss_ref[...] += jnp.broadcast_to(part, loss_ref.shape)
        dy = err * (1.0 / D)
        dg_ref[...] += jnp.sum(dy * xh, axis=0, keepdims=True)
        dxh = dy * g
        dh = r * (dxh - xh * jnp.mean(dxh * xh, axis=-1, keepdims=True))
        dh_ref[...] = dh
        _gate_back(dh, f_ref, gate_ref, coef, df_ref, dgate_ref)

    row = pl.BlockSpec((tr, D), lambda i: (i, 0))
    vec = _vec_spec(D, 1)
    return pl.pallas_call(
        body, name=name, grid=(S // tr,),
        in_specs=[row, vec, row, row, vec],
        out_specs=[pl.BlockSpec((1, 128), lambda i: (0, 0)), row, vec, row, vec],
        out_shape=[jax.ShapeDtypeStruct((1, 128), F32), jax.ShapeDtypeStruct((S, D), F32),
                   jax.ShapeDtypeStruct((1, D), F32), jax.ShapeDtypeStruct((S, D), BF16),
                   jax.ShapeDtypeStruct((1, D), F32)],
        compiler_params=_params(("arbitrary",)),
    )(h, gain, target, f, gate)


def _ffn_gate_up(n, wg_t, wu_t, name):
    S, D = n.shape
    nk, w, _ = wg_t.shape
    tm = _row_tile(S, 512)

    def body(n_ref, wg_ref, wu_ref, dga_ref, dup_ref, act_ref):
        x = n_ref[...]
        ga = lax.dot_general(x, wg_ref[...], NT, preferred_element_type=F32)
        up = lax.dot_general(x, wu_ref[...], NT, preferred_element_type=F32)
        sig = _sigmoid(ga)
        silu = ga * sig
        dga_ref[...] = (up * (sig * (1.0 + ga * (1.0 - sig)))).astype(BF16)
        dup_ref[...] = silu.astype(BF16)
        act_ref[...] = (silu * up).astype(BF16)

    wspec = pl.BlockSpec((None, w, D), lambda k, m: (k, 0, 0))
    ospec = pl.BlockSpec((None, tm, w), lambda k, m: (k, m, 0))
    out = jax.ShapeDtypeStruct((nk, S, w), BF16)
    return pl.pallas_call(
        body, name=name, grid=(nk, S // tm),
        in_specs=[pl.BlockSpec((tm, D), lambda k, m: (m, 0)), wspec, wspec],
        out_specs=[ospec, ospec, ospec], out_shape=[out, out, out],
        compiler_params=_params(("parallel", "parallel")),
    )(n, wg_t, wu_t)


def _mm_residual(lhs, w, h_in, gvec, coef, name, next_norm=None):
    nk, S, kc = lhs.shape
    D = w.shape[2]
    tm = _row_tile(S, 512)
    n_in = 4 if next_norm is None else 7

    def body(*refs):
        l_ref, w_ref, h_ref, g_ref = refs[:4]
        ho_ref, f_ref = refs[n_in:n_in + 2]
        acc_ref = refs[-1]
        k = pl.program_id(1)

        @pl.when(k == 0)
        def _():
            acc_ref[...] = jnp.zeros_like(acc_ref)

        acc_ref[...] += jnp.dot(l_ref[...], w_ref[...], preferred_element_type=F32)

        @pl.when(k == nk - 1)
        def _():
            f = acc_ref[...]
            f_ref[...] = f.astype(BF16)
            x = h_ref[...] + (coef * g_ref[...]) * f
            ho_ref[...] = x
            if next_norm is not None:
                ng_ref, sc_ref, sh_ref = refs[4:7]
                r = lax.rsqrt(jnp.mean(x * x, axis=-1, keepdims=True) + RMS_EPS)
                y = (x * r) * ng_ref[...]
                refs[n_in + 2][...] = (y * (1.0 + sc_ref[...]) + sh_ref[...]).astype(BF16)

    row = pl.BlockSpec((tm, D), lambda m, k: (m, 0))
    vec = _vec_spec(D, 2)
    with_n = next_norm is not None
    return pl.pallas_call(
        body, name=name, grid=(S // tm, nk),
        in_specs=[pl.BlockSpec((None, tm, kc), lambda m, k: (k, m, 0)),
                  pl.BlockSpec((None, kc, D), lambda m, k: (k, 0, 0)), row, vec] + [vec] * (3 * with_n),
        out_specs=[row, row] + [row] * with_n,
        out_shape=[jax.ShapeDtypeStruct((S, D), F32), jax.ShapeDtypeStruct((S, D), BF16)]
        + [jax.ShapeDtypeStruct((S, D), BF16)] * with_n,
        scratch_shapes=[pltpu.VMEM((tm, D), F32)],
        compiler_params=_params(("parallel", "arbitrary")),
    )(lhs, w, h_in, gvec, *(next_norm or ()))


def _mm_cols_rope(n, w, cos, sin, name):
    S, D = n.shape
    nk, _, wd = w.shape
    assert wd % 128 == 0
    tm = _row_tile(S, 512)
    q_groups = ATTN_WIDTH // 128
    scale = HEAD_DIM ** -0.5

    def body(n_ref, w_ref, cos_ref, sin_ref, o_ref):
        x = jnp.dot(n_ref[...], w_ref[...], preferred_element_type=F32)
        first = pl.program_id(0) * (wd // 128)
        c, s = cos_ref[...], sin_ref[...]
        for j in range(wd // 128):
            lanes = slice(j * 128, (j + 1) * 128)
            group = first + j
            xj = x[:, lanes]
            rot = _rotate(xj, c, s) * jnp.where(group < q_groups, scale, 1.0)
            o_ref[:, lanes] = jnp.where(group < 2 * q_groups, rot, xj)

    tab = pl.BlockSpec((tm, 128), lambda k, m: (m, 0))
    return pl.pallas_call(
        body, name=name, grid=(nk, S // tm),
        in_specs=[pl.BlockSpec((tm, D), lambda k, m: (m, 0)), pl.BlockSpec((None, D, wd), lambda k, m: (k, 0, 0)),
                  tab, tab],
        out_specs=pl.BlockSpec((tm, wd), lambda k, m: (m, k)),
        out_shape=jax.ShapeDtypeStruct((S, nk * wd), F32),
        compiler_params=_params(("parallel", "parallel")),
    )(n, w, cos, sin)


def _ffn_dact(df, wd, fga, fup, name, deps=()):
    S, D = df.shape
    nk, w, _ = wd.shape
    tm = _row_tile(S, 512)
    deps = _deps(deps)

    def body(df_ref, wd_ref, fga_ref, fup_ref, dga_ref, dup_ref):
        dact = lax.dot_general(df_ref[...], wd_ref[...], NT, preferred_element_type=F32)
        dga_ref[...] = (dact * fga_ref[...].astype(F32)).astype(BF16)
        dup_ref[...] = (dact * fup_ref[...].astype(F32)).astype(BF16)

    cspec = pl.BlockSpec((None, tm, w), lambda k, m: (k, m, 0))
    out = jax.ShapeDtypeStruct((nk, S, w), BF16)
    return pl.pallas_call(
        _blind_to(body, 4, len(deps)), name=name, grid=(nk, S // tm),
        in_specs=[pl.BlockSpec((tm, D), lambda k, m: (m, 0)), pl.BlockSpec((None, w, D), lambda k, m: (k, 0, 0)),
                  cspec, cspec] + [ANY_SPEC] * len(deps),
        out_specs=[cspec, cspec], out_shape=[out, out],
        compiler_params=_params(("parallel", "parallel")),
    )(df, wd, fga, fup, *deps)


def _mm_reduce(lhs_list, w_list, chunked3d, w_is_kd, name, deps=()):
    nk = w_list[0].shape[0]
    kc, D = w_list[0].shape[1:] if w_is_kd else w_list[0].shape[:0:-1]
    S = lhs_list[0].shape[1] if chunked3d else lhs_list[0].shape[0]
    tm = _row_tile(S, 512)
    npair = len(lhs_list)
    deps = _deps(deps)

    def body(*refs):
        l_refs, w_refs = refs[:npair], refs[npair:2 * npair]
        o_ref, acc_ref = refs[2 * npair], refs[2 * npair + 1]
        k = pl.program_id(1)

        @pl.when(k == 0)
        def _():
            acc_ref[...] = jnp.zeros_like(acc_ref)

        for l_ref, w_ref in zip(l_refs, w_refs):
            if w_is_kd:
                acc_ref[...] += jnp.dot(l_ref[...], w_ref[...], preferred_element_type=F32)
            else:
                acc_ref[...] += lax.dot_general(l_ref[...], w_ref[...], NT, preferred_element_type=F32)

        @pl.when(k == nk - 1)
        def _():
            o_ref[...] = acc_ref[...]

    if chunked3d:
        lspec = pl.BlockSpec((None, tm, kc), lambda m, k: (k, m, 0))
    else:
        lspec = pl.BlockSpec((tm, kc), lambda m, k: (m, k))
    wspec = pl.BlockSpec((None,) + tuple(w_list[0].shape[1:]), lambda m, k: (k, 0, 0))
    return pl.pallas_call(
        _blind_to(body, 2 * npair, len(deps)), name=name, grid=(S // tm, nk),
        in_specs=[lspec] * npair + [wspec] * npair + [ANY_SPEC] * len(deps),
        out_specs=pl.BlockSpec((tm, D), lambda m, k: (m, 0)),
        out_shape=jax.ShapeDtypeStruct((S, D), F32),
        scratch_shapes=[pltpu.VMEM((tm, D), F32)],
        compiler_params=_params(("parallel", "arbitrary")),
    )(*lhs_list, *w_list, *deps)


def _wgrad_chunk_lhs(lhs_list, rhs, name, deps=()):
    nk, S, w = lhs_list[0].shape
    D = rhs.shape[1]
    ts = _row_tile(S, 512)
    ns = S // ts
    nl = len(lhs_list)
    deps = _deps(deps)

    def body(*refs):
        l_refs, r_ref = refs[:nl], refs[nl]
        o_refs, acc_refs = refs[nl + 1:2 * nl + 1], refs[2 * nl + 1:]
        s = pl.program_id(1)

        @pl.when(s == 0)
        def _():
            for acc_ref in acc_refs:
                acc_ref[...] = jnp.zeros_like(acc_ref)

        x = r_ref[...]
        for l_ref, acc_ref in zip(l_refs, acc_refs):
            acc_ref[...] += lax.dot_general(l_ref[...], x, TN, preferred_element_type=F32)

        @pl.when(s == ns - 1)
        def _():
            for o_ref, acc_ref in zip(o_refs, acc_refs):
                o_ref[...] = acc_ref[...]

    return pl.pallas_call(
        _blind_to(body, nl + 1, len(deps)), name=name, grid=(nk, ns),
        in_specs=[pl.BlockSpec((None, ts, w), lambda k, s: (k, s, 0))] * nl
        + [pl.BlockSpec((ts, D), lambda k, s: (s, 0))] + [ANY_SPEC] * len(deps),
        out_specs=[pl.BlockSpec((None, w, D), lambda k, s: (k, 0, 0))] * nl,
        out_shape=[jax.ShapeDtypeStruct((nk, w, D), F32)] * nl,
        scratch_shapes=[pltpu.VMEM((w, D), F32)] * nl,
        compiler_params=_params(("parallel", "arbitrary")),
    )(*lhs_list, rhs, *deps)


def _wgrad_chunk_rhs(lhs, rhs_list, nk, chunked3d, name, deps=()):
    S, D = lhs.shape
    w = rhs_list[0].shape[2] if chunked3d else rhs_list[0].shape[1] // nk
    ts = _row_tile(S, 512)
    ns = S // ts
    nr = len(rhs_list)
    deps = _deps(deps)

    def body(*refs):
        l_ref, r_refs = refs[0], refs[1:1 + nr]
        o_refs, acc_refs = refs[1 + nr:1 + 2 * nr], refs[1 + 2 * nr:]
        s = pl.program_id(1)

        @pl.when(s == 0)
        def _():
            for acc_ref in acc_refs:
                acc_ref[...] = jnp.zeros_like(acc_ref)

        x = l_ref[...]
        for r_ref, acc_ref in zip(r_refs, acc_refs):
            acc_ref[...] += lax.dot_general(x, r_ref[...], TN, preferred_element_type=F32)

        @pl.when(s == ns - 1)
        def _():
            for o_ref, acc_ref in zip(o_refs, acc_refs):
                o_ref[...] = acc_ref[...]

    if chunked3d:
        rspec = pl.BlockSpec((None, ts, w), lambda k, s: (k, s, 0))
    else:
        rspec = pl.BlockSpec((ts, w), lambda k, s: (s, k))
    ospec = pl.BlockSpec((None, D, w), lambda k, s: (k, 0, 0))
    return pl.pallas_call(
        _blind_to(body, 1 + nr, len(deps)), name=name, grid=(nk, ns),
        in_specs=[pl.BlockSpec((ts, D), lambda k, s: (s, 0))] + [rspec] * nr + [ANY_SPEC] * len(deps),
        out_specs=[ospec] * nr,
        out_shape=[jax.ShapeDtypeStruct((nk, D, w), F32)] * nr,
        scratch_shapes=[pltpu.VMEM((D, w), F32)] * nr,
        compiler_params=_params(("parallel", "arbitrary")),
    )(lhs, *rhs_list, *deps)


def _rope_tables(S):
    pos = np.arange(S, dtype=np.float32)
    inv_freq = (ROPE_THETA ** (-np.arange(0, HEAD_DIM, 2, dtype=np.float32) / HEAD_DIM)).astype(np.float32)
    ang = (pos[:, None] * inv_freq[None, :]).astype(np.float64)
    cos, sin = np.cos(ang).astype(np.float32), np.sin(ang).astype(np.float32)
    cos2 = np.concatenate([cos, cos, cos, cos], axis=1)
    sin2 = np.concatenate([-sin, sin, -sin, sin], axis=1)
    return jnp.asarray(cos2), jnp.asarray(sin2)


def _rotate(t, cos, sin_signed):
    half = HEAD_DIM // 2
    lane = lax.broadcasted_iota(jnp.int32, t.shape, 1)
    first = (lane % HEAD_DIM) < half
    partner = jnp.where(first, pltpu.roll(t, 128 - half, 1), pltpu.roll(t, half, 1))
    return t * cos + partner * sin_signed


def _band_mask(T, has_prev):
    qi = lax.broadcasted_iota(jnp.int32, (T, 2 * T), 0)
    kj = lax.broadcasted_iota(jnp.int32, (T, 2 * T), 1)
    return ((kj < T) & (kj >= qi) & has_prev) | ((kj >= T) & (kj - T <= qi))


def _stack_heads(x, head0):
    zero = jnp.zeros_like(x)
    return jnp.concatenate([jnp.where(head0, x, zero), jnp.where(head0, zero, x)], axis=0)


def _branch_blocks(rows, dilation):
    T = min(ATTN_BLOCK, rows // dilation)
    return T, rows // T


def _block_rows(base, T, dilation):
    if dilation == 1:
        return pl.ds(pl.multiple_of(base, T), T)
    return pl.ds(base, T, stride=dilation)


def _qkv_specs(S):
    groups = ATTN_WIDTH // 128
    return [pl.BlockSpec((S, 128), lambda j, off=t * groups: (0, off + j)) for t in range(3)]


def _attn_fwd(proj, name):
    S, A = proj.shape[0], ATTN_WIDTH
    sup = min(S, SUPER_ROWS)
    nd = len(DILATIONS)
    assert S % sup == 0

    def body(q_ref, k_ref, v_ref, attn_ref, lse_ref, acc_s, m_s, l_s):
        lane = lax.broadcasted_iota(jnp.int32, (1, 128), 1)
        head0 = lane < HEAD_DIM

        def supertile(st, carry):
            row0 = st * sup
            for di, dil in enumerate(DILATIONS):
                T, nblk = _branch_blocks(sup, dil)
                span = T * dil
                assert T == ATTN_BLOCK or sup == S

                def block(idx, c2, di=di, dil=dil, T=T, span=span):
                    r = idx % dil
                    loc = (idx // dil) * span + r
                    base = row0 + loc
                    rows = _block_rows(base, T, dil)
                    prev = _block_rows(jnp.maximum(base - span, r), T, dil)
                    qb = q_ref[rows, :].astype(BF16)
                    k2 = jnp.concatenate([k_ref[prev, :], k_ref[rows, :]], axis=0).astype(BF16)
                    v2 = jnp.concatenate([v_ref[prev, :], v_ref[rows, :]], axis=0).astype(BF16)
                    valid = _band_mask(T, base >= span)
                    q2 = _stack_heads(qb, head0)
                    s = lax.dot_general(q2, k2, NT, preferred_element_type=F32)
                    s = jnp.where(jnp.concatenate([valid, valid], axis=0), s, NEG)
                    m = jnp.max(s, axis=-1, keepdims=True)
                    p = jnp.exp(s - m)
                    l = jnp.sum(p, axis=-1, keepdims=True)
                    acc = jnp.dot(p.astype(BF16), v2, preferred_element_type=F32)
                    lrows = _block_rows(di * sup + loc, T, dil)
                    acc_s[lrows, :] = jnp.where(head0, acc[:T], acc[T:])
                    m_s[lrows, :] = jnp.where(head0, m[:T], m[T:])
                    l_s[lrows, :] = jnp.where(head0, l[:T], l[T:])
                    return c2

                lax.fori_loop(0, nblk, block, 0, unroll=8)

            chunk = min(sup, 256)

            def merge(ci, c2):
                lr = [pl.ds(pl.multiple_of(di * sup + ci * chunk, chunk), chunk) for di in range(nd)]
                gr = pl.ds(pl.multiple_of(row0 + ci * chunk, chunk), chunk)
                m0, m1, m2 = m_s[lr[0], :], m_s[lr[1], :], m_s[lr[2], :]
                mm = jnp.maximum(jnp.maximum(m0, m1), m2)
                w0, w1, w2 = jnp.exp(m0 - mm), jnp.exp(m1 - mm), jnp.exp(m2 - mm)
                den = (w0 * l_s[lr[0], :] + w1 * l_s[lr[1], :]) + w2 * l_s[lr[2], :]
                num = (w0 * acc_s[lr[0], :] + w1 * acc_s[lr[1], :]) + w2 * acc_s[lr[2], :]
                attn_ref[gr, :] = num / den
                lse_ref[gr, :] = mm + jnp.log(den)
                return c2

            lax.fori_loop(0, sup // chunk, merge, 0)
            return carry

        lax.fori_loop(0, S // sup, supertile, 0)

    blk = pl.BlockSpec((S, 128), lambda j: (0, j))
    out = jax.ShapeDtypeStruct((S, A), F32)
    return pl.pallas_call(
        body, name=name, grid=(A // 128,),
        in_specs=_qkv_specs(S), out_specs=[blk, blk], out_shape=[out, out],
        scratch_shapes=[pltpu.VMEM((nd * sup, 128), F32)] * 3,
        compiler_params=_params(("parallel",)),
    )(proj, proj, proj)


def _attn_bwd(proj, da, lse, delta, name):
    S, A = da.shape

    def body(q_ref, k_ref, v_ref, da_ref, lse_ref, dl_ref, dq_ref, dk_ref, dv_ref):
        lane = lax.broadcasted_iota(jnp.int32, (1, 128), 1)
        head0 = lane < HEAD_DIM
        dq_ref[...] = jnp.zeros_like(dq_ref)
        dk_ref[...] = jnp.zeros_like(dk_ref)
        dv_ref[...] = jnp.zeros_like(dv_ref)
        for dil in DILATIONS:
            T, nblk = _branch_blocks(S, dil)
            span = T * dil

            def block(idx, carry, dil=dil, T=T, span=span):
                r = idx % dil
                base = (idx // dil) * span + r
                rows = _block_rows(base, T, dil)
                prev = _block_rows(jnp.maximum(base - span, r), T, dil)
                qb, dab = q_ref[rows, :].astype(BF16), da_ref[rows, :].astype(BF16)
                k2 = jnp.concatenate([k_ref[prev, :], k_ref[rows, :]], axis=0).astype(BF16)
                v2 = jnp.concatenate([v_ref[prev, :], v_ref[rows, :]], axis=0).astype(BF16)
                lse_b, dl_b = lse_ref[rows, :], dl_ref[rows, :]
                valid = _band_mask(T, base >= span)
                valid2 = jnp.concatenate([valid, valid], axis=0)
                q2, da2 = _stack_heads(qb, head0), _stack_heads(dab, head0)
                lse2 = jnp.concatenate([lse_b[:, 0:1], lse_b[:, HEAD_DIM:HEAD_DIM + 1]], axis=0)
                dl2 = jnp.concatenate([dl_b[:, 0:1], dl_b[:, HEAD_DIM:HEAD_DIM + 1]], axis=0)
                s = lax.dot_general(q2, k2, NT, preferred_element_type=F32)
                p = jnp.where(valid2, jnp.exp(s - lse2), 0.0)
                dp = lax.dot_general(da2, v2, NT, preferred_element_type=F32)
                ds = (p * (dp - dl2)).astype(BF16)
                dq2 = jnp.dot(ds, k2, preferred_element_type=F32)
                dk2 = lax.dot_general(ds, q2, TN, preferred_element_type=F32)
                dv2 = lax.dot_general(p.astype(BF16), da2, TN, preferred_element_type=F32)
                dq_ref[rows, :] += jnp.where(head0, dq2[:T], dq2[T:])
                dk_ref[rows, :] += dk2[T:]
                dv_ref[rows, :] += dv2[T:]
                dk_ref[prev, :] += dk2[:T]
                dv_ref[prev, :] += dv2[:T]
                return carry

            lax.fori_loop(0, nblk, block, 0, unroll=8)

    blk = pl.BlockSpec((S, 128), lambda j: (0, j))
    out = jax.ShapeDtypeStruct((S, A), F32)
    return pl.pallas_call(
        body, name=name, grid=(A // 128,),
        in_specs=_qkv_specs(S) + [blk] * 3, out_specs=[blk] * 3, out_shape=[out] * 3,
        compiler_params=_params(("parallel",)),
    )(proj, proj, proj, da, lse, delta)


SUBLANES = 8
CONV_CHUNK = 64
FIRST_TAP = HALO - (CONV_KERNEL - 1)


def _store_shifted(shift_s, win, rows):
    shift_s[0, pl.ds(0, rows), :] = win
    for b in range(1, SUBLANES):
        shift_s[b, pl.ds(0, rows - SUBLANES), :] = win[b:b + rows - SUBLANES, :]


def _glu_window(a_ref, b_ref, ah_ref, bh_ref, first):
    u0 = a_ref[...] * _sigmoid(b_ref[...])
    u0h = ah_ref[...] * _sigmoid(bh_ref[...])
    u0h = jnp.where(first, jnp.zeros_like(u0h), u0h)
    return jnp.concatenate([u0h, u0], axis=0)


def _conv_norms(u1, lng, lnb):
    mu = jnp.mean(u1, axis=-1, keepdims=True)
    xc = u1 - mu
    rstd = lax.rsqrt(jnp.mean(xc * xc, axis=-1, keepdims=True) + LN_EPS)
    u1h = xc * rstd
    u2 = u1h * lng + lnb
    sig = _sigmoid(u2)
    u3 = u2 * sig
    r = lax.rsqrt(jnp.mean(u3 * u3, axis=-1, keepdims=True) + RMS_EPS)
    return rstd, u1h, u2, sig, u3, r


def _conv_specs(tr, C, col_a, col_b):
    per = tr // HALO

    def tile(col):
        return pl.BlockSpec((tr, C), lambda i: (i, col))

    def halo(col):
        return pl.BlockSpec((HALO, C), lambda i: (jnp.maximum(i * per - 1, 0), col))

    return tile(col_a), tile(col_b), halo(col_a), halo(col_b)


def _mixer_merge(proj, attn, cw, cb, lng, lnb, gat, gco, name):
    S = proj.shape[0]
    C = CONV_WIDTH
    A = attn.shape[1]
    tr = _row_tile(S, 256)

    def body(a_ref, b_ref, ah_ref, bh_ref, at_ref, w_ref, cb_ref, lng_ref, lnb_ref, gat_ref, gco_ref, u1_ref, y_ref,
             shift_s):
        _store_shifted(shift_s, _glu_window(a_ref, b_ref, ah_ref, bh_ref, pl.program_id(0) == 0), tr + HALO)

        def chunk(rc, carry):
            r0 = pl.multiple_of(rc * CONV_CHUNK, CONV_CHUNK)
            for lb in range(C // 128):
                lanes = slice(lb * 128, (lb + 1) * 128)
                acc = jnp.broadcast_to(cb_ref[:, lanes], (CONV_CHUNK, 128))
                for j in range(CONV_KERNEL):
                    a8, b = divmod(FIRST_TAP + j, SUBLANES)
                    acc = acc + w_ref[j:j + 1, lanes] * shift_s[b, pl.ds(r0 + a8 * SUBLANES, CONV_CHUNK), lanes]
                u1_ref[pl.ds(r0, CONV_CHUNK), lanes] = acc
            return carry

        lax.fori_loop(0, tr // CONV_CHUNK, chunk, 0)
        _, _, _, _, u3, r = _conv_norms(u1_ref[...], lng_ref[...], lnb_ref[...])
        y_ref[:, A:] = ((u3 * r) * gco_ref[...]).astype(BF16)
        x = at_ref[...]
        ra = lax.rsqrt(jnp.mean(x * x, axis=-1, keepdims=True) + RMS_EPS)
        y_ref[:, :A] = ((x * ra) * gat_ref[...]).astype(BF16)

    ta, tb, ha, hb = _conv_specs(tr, C, 3, 4)
    row = pl.BlockSpec((tr, C), lambda i: (i, 0))
    vec = _vec_spec(C, 1)
    return pl.pallas_call(
        body, name=name, grid=(S // tr,),
        in_specs=[ta, tb, ha, hb, pl.BlockSpec((tr, A), lambda i: (i, 0)), pl.BlockSpec((HALO, C), lambda i: (0, 0)),
                  vec, vec, vec, _vec_spec(A, 1), vec],
        out_specs=[row, pl.BlockSpec((tr, A + C), lambda i: (i, 0))],
        out_shape=[jax.ShapeDtypeStruct((S, C), F32), jax.ShapeDtypeStruct((S, A + C), BF16)],
        scratch_shapes=[pltpu.VMEM((SUBLANES, tr + HALO, C), F32)],
        compiler_params=_params(("parallel",)),
    )(proj, proj, proj, proj, attn, cw, cb, lng, lnb, gat, gco)


def _mix_out_bwd(dmo, wout, attn, u1, gat, lng, lnb, gco, name):
    S, D = dmo.shape
    A, C = attn.shape[1], u1.shape[1]
    tr = _row_tile(S, 256)

    def body(dmo_ref, w_ref, a_ref, u1_ref, gat_ref, lng_ref, lnb_ref, gco_ref,
             da_ref, dl_ref, du1_ref, dgat_ref, dgco_ref, dlng_ref, dlnb_ref, dcb_ref):
        @pl.when(pl.program_id(0) == 0)
        def _():
            for ref in (dgat_ref, dgco_ref, dlng_ref, dlnb_ref, dcb_ref):
                ref[...] = jnp.zeros_like(ref)

        dy = lax.dot_general(dmo_ref[...], w_ref[...], NT, preferred_element_type=F32)
        dya, dyc = dy[:, :A], dy[:, A:]
        x = a_ref[...]
        r = lax.rsqrt(jnp.mean(x * x, axis=-1, keepdims=True) + RMS_EPS)
        xh = x * r
        dgat_ref[...] += jnp.sum(dya * xh, axis=0, keepdims=True)
        dxh = dya * gat_ref[...]
        dx = r * (dxh - xh * jnp.mean(dxh * xh, axis=-1, keepdims=True))
        da_ref[...] = dx
        hi = lax.broadcasted_iota(jnp.int32, (A, A), 0) // HEAD_DIM
        hj = lax.broadcasted_iota(jnp.int32, (A, A), 1) // HEAD_DIM
        same_head = (hi == hj).astype(F32)
        dl_ref[...] = jnp.dot(dx * x, same_head, preferred_element_type=F32, precision=lax.Precision.HIGHEST)

        lng = lng_ref[...]
        rstd, u1h, u2, sig, u3, rc = _conv_norms(u1_ref[...], lng, lnb_ref[...])
        u3h = u3 * rc
        dgco_ref[...] += jnp.sum(dyc * u3h, axis=0, keepdims=True)
        du3h = dyc * gco_ref[...]
        du3 = rc * (du3h - u3h * jnp.mean(du3h * u3h, axis=-1, keepdims=True))
        du2 = du3 * (sig * (1.0 + u2 * (1.0 - sig)))
        dlng_ref[...] += jnp.sum(du2 * u1h, axis=0, keepdims=True)
        dlnb_ref[...] += jnp.sum(du2, axis=0, keepdims=True)
        du1h = du2 * lng
        du1 = rstd * (du1h - jnp.mean(du1h, axis=-1, keepdims=True)
                      - u1h * jnp.mean(du1h * u1h, axis=-1, keepdims=True))
        du1_ref[...] = du1
        dcb_ref[...] += jnp.sum(du1, axis=0, keepdims=True)

    arow = pl.BlockSpec((tr, A), lambda i: (i, 0))
    crow = pl.BlockSpec((tr, C), lambda i: (i, 0))
    avec, cvec = _vec_spec(A, 1), _vec_spec(C, 1)
    return pl.pallas_call(
        body, name=name, grid=(S // tr,),
        in_specs=[pl.BlockSpec((tr, D), lambda i: (i, 0)), pl.BlockSpec((A + C, D), lambda i: (0, 0)), arow, crow,
                  avec, cvec, cvec, cvec],
        out_specs=[arow, arow, crow, avec, cvec, cvec, cvec, cvec],
        out_shape=[jax.ShapeDtypeStruct((S, A), F32)] * 2 + [jax.ShapeDtypeStruct((S, C), F32)]
        + [jax.ShapeDtypeStruct((1, A), F32)] + [jax.ShapeDtypeStruct((1, C), F32)] * 4,
        compiler_params=_params(("arbitrary",)),
    )(dmo, wout, attn, u1, gat, lng, lnb, gco)


def _dproj(du1, proj, cw, dq, dk, dv, cos, sin, name):
    S, C = du1.shape
    A = dq.shape[1]
    tr = _row_tile(S, 256)
    nt = S // tr
    per = tr // HALO
    scale = HEAD_DIM ** -0.5

    def body(du_ref, dun_ref, a_ref, b_ref, ah_ref, bh_ref, w_ref, dq_ref, dk_ref, dv_ref, cos_ref, sin_ref,
             dp_ref, dw_ref, win_s, dwin_s, du0_s, tap_s):
        i = pl.program_id(0)

        @pl.when(i == 0)
        def _():
            dw_ref[...] = jnp.zeros_like(dw_ref)

        _store_shifted(win_s, _glu_window(a_ref, b_ref, ah_ref, bh_ref, i == 0), tr + HALO)
        nxt = jnp.where(i == nt - 1, jnp.zeros_like(dun_ref[...]), dun_ref[...])
        _store_shifted(dwin_s, jnp.concatenate([du_ref[...], nxt], axis=0), tr + HALO)
        tap_s[...] = jnp.zeros_like(tap_s)

        def chunk(rc, carry):
            r0 = pl.multiple_of(rc * CONV_CHUNK, CONV_CHUNK)
            for lb in range(C // 128):
                lanes = slice(lb * 128, (lb + 1) * 128)
                du = du_ref[pl.ds(r0, CONV_CHUNK), lanes]
                acc = jnp.zeros((CONV_CHUNK, 128), F32)
                for j in range(CONV_KERNEL):
                    a8, b = divmod(CONV_KERNEL - 1 - j, SUBLANES)
                    acc = acc + w_ref[j:j + 1, lanes] * dwin_s[b, pl.ds(r0 + a8 * SUBLANES, CONV_CHUNK), lanes]
                    a8, b = divmod(FIRST_TAP + j, SUBLANES)
                    prod = du * win_s[b, pl.ds(r0 + a8 * SUBLANES, CONV_CHUNK), lanes]
                    part = prod[0:SUBLANES]
                    for g in range(1, CONV_CHUNK // SUBLANES):
                        part = part + prod[g * SUBLANES:(g + 1) * SUBLANES]
                    tap_s[j * SUBLANES:(j + 1) * SUBLANES, lanes] += part
                du0_s[pl.ds(r0, CONV_CHUNK), lanes] = acc
            return carry

        lax.fori_loop(0, tr // CONV_CHUNK, chunk, 0)
        taps = [jnp.sum(tap_s[j * SUBLANES:(j + 1) * SUBLANES, :], axis=0, keepdims=True)
                for j in range(CONV_KERNEL)]
        taps.append(jnp.zeros((HALO - CONV_KERNEL, C), F32))
        dw_ref[...] += jnp.concatenate(taps, axis=0)
        du0 = du0_s[...]
        a, sig = a_ref[...], _sigmoid(b_ref[...])
        dp_ref[:, 3 * A:3 * A + C] = (du0 * sig).astype(BF16)
        dp_ref[:, 3 * A + C:] = (du0 * a * sig * (1.0 - sig)).astype(BF16)
        cos_, nsin = cos_ref[...], -sin_ref[...]
        for j in range(A // 128):
            lanes = slice(j * 128, (j + 1) * 128)
            dp_ref[:, j * 128:(j + 1) * 128] = (_rotate(dq_ref[:, lanes], cos_, nsin) * scale).astype(BF16)
            dp_ref[:, A + j * 128:A + (j + 1) * 128] = _rotate(dk_ref[:, lanes], cos_, nsin).astype(BF16)
        dp_ref[:, 2 * A:3 * A] = dv_ref[...].astype(BF16)

    ta, tb, ha, hb = _conv_specs(tr, C, 3, 4)
    row = pl.BlockSpec((tr, C), lambda i: (i, 0))
    arow = pl.BlockSpec((tr, A), lambda i: (i, 0))
    tab = pl.BlockSpec((tr, 128), lambda i: (i, 0))
    nxt = pl.BlockSpec((HALO, C), lambda i: (jnp.minimum((i + 1) * per, S // HALO - 1), 0))
    wspec = pl.BlockSpec((HALO, C), lambda i: (0, 0))
    return pl.pallas_call(
        body, name=name, grid=(nt,),
        in_specs=[row, nxt, ta, tb, ha, hb, wspec, arow, arow, arow, tab, tab],
        out_specs=[pl.BlockSpec((tr, 3 * A + 2 * C), lambda i: (i, 0)), wspec],
        out_shape=[jax.ShapeDtypeStruct((S, 3 * A + 2 * C), BF16), jax.ShapeDtypeStruct((HALO, C), F32)],
        scratch_shapes=[pltpu.VMEM((SUBLANES, tr + HALO, C), F32), pltpu.VMEM((SUBLANES, tr + HALO, C), F32),
                        pltpu.VMEM((tr, C), F32), pltpu.VMEM((HALO * SUBLANES, C), F32)],
        compiler_params=_params(("arbitrary",)),
    )(du1, du1, proj, proj, proj, proj, cw, dq, dk, dv, cos, sin)


def _ada_fwd(c_all, w, b, name):
    B, D = c_all.shape
    N = w.shape[1]
    tn = 768 if N % 768 == 0 else N

    def body(c_ref, w_ref, b_ref, o_ref):
        c = c_ref[...]
        a = (c * _sigmoid(c)).astype(BF16)
        o_ref[...] = jnp.dot(a, w_ref[...].astype(BF16), preferred_element_type=F32) + b_ref[...]

    return pl.pallas_call(
        body, name=name, grid=(N // tn,),
        in_specs=[pl.BlockSpec((B, D), lambda j: (0, 0)), pl.BlockSpec((D, tn), lambda j: (0, j)),
                  pl.BlockSpec((1, tn), lambda j: (0, j))],
        out_specs=pl.BlockSpec((B, tn), lambda j: (0, j)),
        out_shape=jax.ShapeDtypeStruct((B, N), F32),
        compiler_params=_params(("parallel",)),
    )(c_all, w, b)


def _ada_wgrad(c_t, dmod, name):
    D, B = c_t.shape
    N = dmod.shape[1]
    tn = 768 if N % 768 == 0 else N

    def body(c_ref, d_ref, o_ref):
        c = c_ref[...]
        a = (c * _sigmoid(c)).astype(BF16)
        o_ref[...] = jnp.dot(a, d_ref[...].astype(BF16), preferred_element_type=F32)

    return pl.pallas_call(
        body, name=name, grid=(N // tn,),
        in_specs=[pl.BlockSpec((D, B), lambda j: (0, 0)), pl.BlockSpec((B, tn), lambda j: (0, j))],
        out_specs=pl.BlockSpec((D, tn), lambda j: (0, j)),
        out_shape=jax.ShapeDtypeStruct((D, N), F32),
        compiler_params=_params(("parallel",)),
    )(c_t, dmod)


def _sum_rows(x, name):
    R, N = x.shape

    def body(x_ref, o_ref):
        acc = x_ref[0:1, :]
        for r in range(1, R):
            acc = acc + x_ref[r:r + 1, :]
        o_ref[...] = acc

    return pl.pallas_call(
        body, name=name, out_shape=jax.ShapeDtypeStruct((1, N), F32),
        compiler_params=_params(),
    )(x)


def _adamw(w, g, m, v, name, deps=()):
    R, C = w.shape
    tr = _row_tile(R, 256) if R % 256 == 0 else R
    bc1 = 1.0 - ADAM_B1 ** ADAM_STEP
    bc2 = 1.0 - ADAM_B2 ** ADAM_STEP
    deps = _deps(deps)

    def body(w_ref, g_ref, m_ref, v_ref, d_ref, mo_ref, vo_ref, go_ref):
        g_ = g_ref[...]
        m_ = ADAM_B1 * m_ref[...] + (1.0 - ADAM_B1) * g_
        v_ = ADAM_B2 * v_ref[...] + (1.0 - ADAM_B2) * (g_ * g_)
        mo_ref[...] = m_
        vo_ref[...] = v_
        go_ref[...] = g_
        d_ref[...] = -ADAM_LR * ((m_ / bc1) / (jnp.sqrt(v_ / bc2) + ADAM_EPS) + ADAM_WD * w_ref[...])

    row = pl.BlockSpec((tr, C), lambda i: (i, 0))
    out = jax.ShapeDtypeStruct((R, C), F32)
    return pl.pallas_call(
        _blind_to(body, 4, len(deps)), name=name, grid=(R // tr,),
        in_specs=[row] * 4 + [ANY_SPEC] * len(deps), out_specs=[row] * 4, out_shape=[out] * 4,
        compiler_params=_params(("parallel",), streaming=True),
    )(w, g, m, v, *deps)


def _coords():
    return lax.axis_index("x"), lax.axis_index("y"), lax.axis_index("c")


def _all_gather8(x, name, deps=()):
    R, N = x.shape
    assert R == 8
    flips = [(fx, fy, fc) for fx in (0, 1) for fy in (0, 1) for fc in (0, 1)][1:]
    deps = _deps(deps)

    def body(x_ref, o_ref, send_sems, recv_sems):
        mx, my, mc = _coords()
        me = 4 * mx + 2 * my + mc

        def rows(dev):
            return o_ref.at[pl.ds(pl.multiple_of(dev * R, R), R), :]

        o_ref[pl.ds(pl.multiple_of(me * R, R), R), :] = x_ref[...]
        copies = []
        for t, (fx, fy, fc) in enumerate(flips):
            peer = (mx ^ fx, my ^ fy, mc ^ fc)
            copies.append(pltpu.make_async_remote_copy(
                src_ref=x_ref, dst_ref=rows(me), send_sem=send_sems.at[t], recv_sem=recv_sems.at[t],
                device_id=peer, device_id_type=MESH))
        for cp in copies:
            cp.start()
        for t, (fx, fy, fc) in enumerate(flips):
            peer_id = 4 * (mx ^ fx) + 2 * (my ^ fy) + (mc ^ fc)
            pltpu.make_async_remote_copy(
                src_ref=x_ref, dst_ref=rows(peer_id), send_sem=send_sems.at[t], recv_sem=recv_sems.at[t],
                device_id=(mx ^ fx, my ^ fy, mc ^ fc), device_id_type=MESH).wait_recv()
        for cp in copies:
            cp.wait_send()

    return pl.pallas_call(
        _blind_to(body, 1, len(deps)), name=name,
        in_specs=[pl.BlockSpec(memory_space=pltpu.VMEM)] + [ANY_SPEC] * len(deps),
        out_specs=pl.BlockSpec(memory_space=pltpu.VMEM),
        out_shape=jax.ShapeDtypeStruct((N_DEV * R, N), F32),
        scratch_shapes=[pltpu.SemaphoreType.DMA((7,)), pltpu.SemaphoreType.DMA((7,))],
        compiler_params=pltpu.CompilerParams(has_side_effects=True, vmem_limit_bytes=VMEM_LIMIT_BYTES),
    )(x, *deps)


def _half_rows(rows, half):
    return pl.ds(pl.multiple_of(half * (rows // 2), 8), rows // 2)


def _split_start(bufs, plan, n, name):
    nb = len(bufs)

    def body(*refs):
        send_sems, recv_sems, token = refs[nb], refs[nb + 1], refs[-1]
        for t, (src, dst, dev) in enumerate(plan(refs[:nb])):
            pltpu.make_async_remote_copy(src_ref=src, dst_ref=dst, send_sem=send_sems.at[t],
                                         recv_sem=recv_sems.at[t], device_id=dev, device_id_type=MESH).start()
        token[...] = jnp.zeros_like(token)

    out = pl.pallas_call(
        body, name=name,
        out_shape=(pltpu.SemaphoreType.DMA((n,)), pltpu.SemaphoreType.DMA((n,)),
                   *[pltpu.HBM(b.shape, b.dtype) for b in bufs], jax.ShapeDtypeStruct((8, 128), F32)),
        in_specs=[HBM_SPEC] * nb,
        out_specs=(SEM_SPEC, SEM_SPEC, *[HBM_SPEC] * nb, pl.BlockSpec(memory_space=pltpu.VMEM)),
        input_output_aliases={i: 2 + i for i in range(nb)},
        compiler_params=pltpu.CompilerParams(has_side_effects=DATAFLOW),
    )(*[pltpu.with_memory_space_constraint(b, pltpu.HBM) for b in bufs])
    return out[0], out[1], list(out[2:2 + nb]), out[-1]


def _split_wait(bufs, send_sems, recv_sems, plan, after, name):
    nb = len(bufs)
    after = _deps(after)

    def body(*refs):
        ss, rs = refs[nb], refs[nb + 1]
        for t, (src, dst, dev) in enumerate(plan(refs[:nb])):
            cp = pltpu.make_async_remote_copy(src_ref=src, dst_ref=dst, send_sem=ss.at[t], recv_sem=rs.at[t],
                                              device_id=dev, device_id_type=MESH)
            cp.wait_send()
            cp.wait_recv()

    out = pl.pallas_call(
        body, name=name,
        out_shape=tuple(pltpu.HBM(b.shape, b.dtype) for b in bufs),
        in_specs=[HBM_SPEC] * nb + [SEM_SPEC, SEM_SPEC] + [ANY_SPEC] * len(after),
        out_specs=tuple([HBM_SPEC] * nb),
        input_output_aliases={i: i for i in range(nb)},
        compiler_params=pltpu.CompilerParams(has_side_effects=DATAFLOW),
    )(*bufs, send_sems, recv_sems, *after)
    return list(out)


class _Exchange:
    def __init__(self, bufs, plan, n, name):
        self.plan, self.name = plan, name
        self.send_sems, self.recv_sems, self.bufs, self.token = _split_start(bufs, plan, n, name + "_start")

    def wait(self, after):
        return _split_wait(self.bufs, self.send_sems, self.recv_sems, self.plan, after, self.name + "_wait")


def _cast_place(w, chip_idx, name, deps=()):
    R, C = w.shape
    tr = _row_tile(R, 256) if R % 256 == 0 else R
    deps = _deps(deps)

    def body(k_ref, w_ref, o_ref):
        o_ref[...] = w_ref[...].astype(BF16)

    grid_spec = pltpu.PrefetchScalarGridSpec(
        num_scalar_prefetch=1, grid=(R // tr,),
        in_specs=[pl.BlockSpec((tr, C), lambda i, k: (i, 0))] + [ANY_SPEC] * len(deps),
        out_specs=pl.BlockSpec((None, tr, C), lambda i, k: (k[0], i, 0)),
    )
    return pl.pallas_call(
        _blind_to(body, 2, len(deps)), name=name, grid_spec=grid_spec,
        out_shape=jax.ShapeDtypeStruct((N_CHIPS, R, C), BF16),
        compiler_params=_params(("parallel",), streaming=True),
    )(chip_idx, w, *deps)


def _plan_gather(refs):
    mx, my, mc = _coords()
    me = 2 * mx + my
    plan = []
    for g in refs:
        mine = g.at[me, _half_rows(g.shape[1], mc), :]
        for fx, fy in XY_FLIPS:
            plan.append((mine, mine, (mx ^ fx, my ^ fy, mc)))
    return plan


def _plan_swap(refs):
    mx, my, mc = _coords()
    plan = []
    for g in refs:
        for fx, fy in XY_FLIPS:
            have = g.at[2 * (mx ^ fx) + (my ^ fy), _half_rows(g.shape[1], mc), :]
            plan.append((have, have, (mx, my, 1 - mc)))
    return plan


def _plan_other_halves(refs):
    n = len(refs) // 2
    mx, my, mc = _coords()
    return [(g.at[pl.ds(0, N_CHIPS), _half_rows(g.shape[1], 1 - mc), :], land, (mx, my, 1 - mc))
            for g, land in zip(refs[:n], refs[n:])]


def _plan_chunks(refs):
    n = len(refs) // 2
    mx, my, mc = _coords()
    plan = []
    for s, land in zip(refs[:n], refs[n:]):
        for t, (fx, fy) in enumerate(XY_FLIPS):
            plan.append((s.at[2 * (mx ^ fx) + (my ^ fy)], land.at[t], (mx ^ fx, my ^ fy, mc)))
    return plan


def _plan_share(refs):
    mx, my, mc = _coords()
    return [(full.at[mc], full.at[mc], (mx, my, 1 - mc)) for full in refs]


def _add_half(g, recv, core_idx, name):
    nk, R, C = g.shape
    rh = R // 2
    tr = _row_tile(rh, 128) if rh % 128 == 0 else rh
    nt = rh // tr

    def body(c_ref, g_ref, r_ref, o_ref):
        o_ref[...] = (g_ref[...] + r_ref[...]).astype(BF16)

    grid_spec = pltpu.PrefetchScalarGridSpec(
        num_scalar_prefetch=1, grid=(nk, nt),
        in_specs=[pl.BlockSpec((None, tr, C), lambda k, i, c: (k, c[0] * nt + i, 0)),
                  pl.BlockSpec((None, tr, C), lambda k, i, c: (k, i, 0))],
        out_specs=pl.BlockSpec((None, tr, C), lambda k, i, c: (k, i, 0)),
    )
    return pl.pallas_call(
        body, name=name, grid_spec=grid_spec, out_shape=jax.ShapeDtypeStruct((nk, rh, C), BF16),
        compiler_params=_params(("parallel", "parallel"), streaming=True),
    )(core_idx, g, recv)


def _sum_chips(s, land, chip_core, name):
    _, rh, C = s.shape
    tr = _row_tile(rh, 128) if rh % 128 == 0 else rh

    def body(p_ref, s_ref, l_ref, o_ref):
        me = p_ref[0]
        acc = None
        for j in range(N_CHIPS):
            t = jnp.maximum(jnp.bitwise_xor(me, j) - 1, 0)
            term = jnp.where(me == j, s_ref[...], l_ref[t]).astype(F32)
            acc = term if acc is None else acc + term
        o_ref[...] = acc

    grid_spec = pltpu.PrefetchScalarGridSpec(
        num_scalar_prefetch=1, grid=(rh // tr,),
        in_specs=[pl.BlockSpec((None, tr, C), lambda i, p: (p[0], i, 0)),
                  pl.BlockSpec((3, tr, C), lambda i, p: (0, i, 0))],
        out_specs=pl.BlockSpec((None, tr, C), lambda i, p: (p[1], i, 0)),
    )
    return pl.pallas_call(
        body, name=name, grid_spec=grid_spec, out_shape=jax.ShapeDtypeStruct((2, rh, C), F32),
        compiler_params=_params(("parallel",), streaming=True),
    )(chip_core, s, land)


def _rs_send_halves(grads, tag):
    lands = [lax.empty((g.shape[0], g.shape[1] // 2, g.shape[2]), g.dtype) for g in grads]
    return _Exchange(list(grads) + lands, _plan_other_halves, len(grads), f"rs_halves_{tag}")


def _rs_send_chunks(ex, after, core_idx, tag):
    bufs = ex.wait(after)
    n = len(bufs) // 2
    sums = [_add_half(g, r, core_idx, f"rs_add_{tag}_{i}") for i, (g, r) in enumerate(zip(bufs[:n], bufs[n:]))]
    lands = [lax.empty((3,) + s.shape[1:], s.dtype) for s in sums]
    return _Exchange(sums + lands, _plan_chunks, 3 * n, f"rs_chunks_{tag}")


def _rs_send_share(ex, after, chip_core, tag):
    bufs = ex.wait(after)
    n = len(bufs) // 2
    fulls = [_sum_chips(s, l, chip_core, f"rs_sum_{tag}_{i}") for i, (s, l) in enumerate(zip(bufs[:n], bufs[n:]))]
    return _Exchange(fulls, _plan_share, n, f"rs_share_{tag}")


def _rs_finish(ex, after):
    return [b.reshape(2 * b.shape[1], b.shape[2]) for b in ex.wait(after)]


def _ffn_forward(h, n, gate, wg, wu, wd, tag, next_norm=None):
    ga, up, act = _ffn_gate_up(n, wg, wu, f"{tag}_gate_up")
    h_out, f, *n_next = _mm_residual(act, wd, h, gate, 0.5, f"{tag}_down", next_norm=next_norm)
    return (h_out, *n_next), (h, n, ga, up, act, f)


def _ffn_backward(dh_out, df, saved, gain, sc, wg, wu, wd, core_idx, tag, prev=None, last=False):
    h, n, ga, up, act, _ = saved
    (dwd,) = _wgrad_chunk_lhs([act], df, f"{tag}_dwd")
    ex_d = _rs_send_halves([dwd], f"{tag}_d")
    dga, dup = _ffn_dact(df, wd, ga, up, f"{tag}_dact", deps=[ex_d.token])
    ex_d = _rs_send_chunks(ex_d, [dga], core_idx, f"{tag}_d")
    dwg, dwu = _wgrad_chunk_lhs([dga, dup], n, f"{tag}_dwgu", deps=[ex_d.token])
    ex_gu = _rs_send_halves([dwg, dwu], f"{tag}_gu")
    dn = _mm_reduce([dga, dup], [wg, wu], True, True, f"{tag}_dn", deps=[ex_gu.token])
    if not last:
        ex_gu = _rs_send_chunks(ex_gu, [dn], core_idx, f"{tag}_gu")
    outs = _norm_mod_bwd(dn, h, gain, sc, dh_out, f"{tag}_norm_bwd", prev=prev, deps=[ex_gu.token])
    return outs, (ex_d, ex_gu)


def _pad_cols(v, n):
    return jnp.pad(v, ((0, 0), (0, n - v.shape[1])))


def _mixer_forward(h1, n2, gt2, win, wout, conv_w, conv_dw_b, conv_ln_g, conv_ln_b, attn_out_g, conv_out_g,
                   next_norm=None):
    S, D = h1.shape
    cos, sin = _rope_tables(S)
    proj = _mm_cols_rope(n2, win, cos, sin, "mix_in")
    attn, lse = _attn_fwd(proj, "attn_fwd")
    u1, y = _mixer_merge(proj, attn, conv_w, conv_dw_b, conv_ln_g, conv_ln_b, attn_out_g, conv_out_g, "mix_merge")
    h2, mo, *n_next = _mm_residual(y[None], wout.reshape(1, D, D), h1, gt2, 1.0, "mix_out", next_norm=next_norm)
    return (h2, *n_next), (h1, n2, proj, cos, sin, attn, lse, u1, y, mo)


def _mixer_backward(dh2, dmo, saved, mix_norm_g, sc2, win, wout, conv_w, conv_ln_g, conv_ln_b, attn_out_g,
                    conv_out_g, core_idx, prev=None):
    h1, n2, proj, cos, sin, attn, lse, u1, y, _ = saved
    S, D = h1.shape
    (dwout,) = _wgrad_chunk_lhs([y[None]], dmo, "mix_dwout")
    dattn, delta, du1, d_attn_g, d_gco, d_lng, d_lnb, d_cb = _mix_out_bwd(
        dmo, wout.reshape(D, D), attn, u1, attn_out_g, conv_ln_g, conv_ln_b, conv_out_g, "mix_out_bwd")
    dq, dk, dv = _attn_bwd(proj, dattn, lse, delta, "attn_bwd")
    dproj, d_cw = _dproj(du1, proj, conv_w, dq, dk, dv, cos, sin, "mix_dproj")
    (dwin,) = _wgrad_chunk_rhs(n2, [dproj], N_CHIPS, False, "mix_dwin")
    ex = _rs_send_halves([dwin, dwout.reshape(N_CHIPS, D // N_CHIPS, D)], "mix")
    dn2 = _mm_reduce([dproj], [win], False, False, "mix_dn", deps=[ex.token])
    ex = _rs_send_chunks(ex, [dn2], core_idx, "mix")
    outs = _norm_mod_bwd(dn2, h1, mix_norm_g, sc2, dh2, "mix_norm_bwd", prev=prev, deps=[ex.token])
    return outs, ex, (d_cb, d_lng, d_lnb, d_attn_g, d_gco, d_cw)


def kernel(x, c, w_ada, b_ada, ffn1_norm_g, ffn1_w_gate, ffn1_w_up, ffn1_w_down, mix_norm_g, w_in, conv_dw_w, conv_dw_b, conv_ln_g, conv_ln_b, attn_out_g, conv_out_g, w_out, ffn2_norm_g, ffn2_w_gate, ffn2_w_up, ffn2_w_down, final_norm_g, loss_target, m_w_ada, m_b_ada, m_ffn1_norm_g, m_ffn1_w_gate, m_ffn1_w_up, m_ffn1_w_down, m_mix_norm_g, m_w_in, m_conv_dw_w, m_conv_dw_b, m_conv_ln_g, m_conv_ln_b, m_attn_out_g, m_conv_out_g, m_w_out, m_ffn2_norm_g, m_ffn2_w_gate, m_ffn2_w_up, m_ffn2_w_down, m_final_norm_g, v_w_ada, v_b_ada, v_ffn1_norm_g, v_ffn1_w_gate, v_ffn1_w_up, v_ffn1_w_down, v_mix_norm_g, v_w_in, v_conv_dw_w, v_conv_dw_b, v_conv_ln_g, v_conv_ln_b, v_attn_out_g, v_conv_out_g, v_w_out, v_ffn2_norm_g, v_ffn2_w_gate, v_ffn2_w_up, v_ffn2_w_down, v_final_norm_g):
    S, D = x.shape[1], x.shape[2]
    mx, my, mc = _coords()
    chip = 2 * mx + my
    dev = 4 * mx + 2 * my + mc
    chip_idx = chip.astype(jnp.int32).reshape(1)
    core_idx = mc.astype(jnp.int32).reshape(1)
    chip_core = jnp.stack([chip, mc]).astype(jnp.int32)
    h0 = x[0]
    target = loss_target[0]

    ncw = CONV_KERNEL * 128
    n0 = -(-(D + ncw) // 1024) * 1024
    pk0 = _pad_cols(jnp.concatenate([c.reshape(1, D), conv_dw_w.reshape(1, ncw)], axis=1), n0)
    g0 = _all_gather8(pk0.reshape(8, n0 // 8), "gather_c").reshape(N_DEV, n0)
    c_all = g0[:, :D]
    conv_w = jnp.concatenate([g0[2 * kc, D:D + ncw].reshape(CONV_KERNEL, 128) for kc in range(N_CHIPS)], axis=1)
    conv_w = jnp.pad(conv_w, ((0, HALO - CONV_KERNEL), (0, 0)))
    nmod = w_ada.shape[2]
    b_shard = lax.dynamic_slice(b_ada, (0, chip * nmod), (1, nmod))
    mod_part = _ada_fwd(c_all, w_ada[0], b_shard, "ada_fwd")
    g1 = _all_gather8(mod_part, "gather_mod")
    mod_all = jnp.concatenate([g1[16 * kc:16 * kc + 8] for kc in range(N_CHIPS)], axis=1)
    mod = lax.dynamic_slice(mod_all, (dev, 0), (1, 9 * D))
    sh1, sc1, gt1, sh2, sc2, gt2, sh3, sc3, gt3 = [mod[:, i * D:(i + 1) * D] for i in range(9)]

    def gather_start(ws, tag, dep):
        slots = [_cast_place(w, chip_idx, f"cast_{tag}_{i}", deps=[dep]) for i, w in enumerate(ws)]
        return _Exchange(slots, _plan_gather, 3 * len(ws), f"gather_{tag}")

    def swap_start(ex, after, tag):
        return _Exchange(ex.wait(after), _plan_swap, 3 * len(ex.bufs), f"swap_{tag}")

    ex_gu1 = gather_start([ffn1_w_gate[0].T, ffn1_w_up[0].T], "ffn1_gu", g1)
    n1 = _norm_mod(h0, ffn1_norm_g, sc1, sh1, "ffn1_norm", deps=[ex_gu1.token])
    ex_d1 = gather_start([ffn1_w_down[0]], "ffn1_d", n1)
    ex_wm = gather_start([w_in[0], w_out[0]], "mix", ex_d1.token)
    ex_w2 = gather_start([ffn2_w_gate[0].T, ffn2_w_up[0].T, ffn2_w_down[0]], "ffn2", ex_wm.token)

    wg1, wu1 = swap_start(ex_gu1, [ex_w2.token], "ffn1_gu").wait([])
    ga1, up1, act1 = _ffn_gate_up(n1, wg1, wu1, "ffn1_gate_up")
    (wd1,) = swap_start(ex_d1, [act1], "ffn1_d").wait([])
    ex_wm = swap_start(ex_wm, [wd1], "mix")
    h1, f1, n2 = _mm_residual(act1, wd1, h0, gt1, 0.5, "ffn1_down", next_norm=(mix_norm_g, sc2, sh2))
    saved1 = (h0, n1, ga1, up1, act1, f1)
    win, wout = ex_wm.wait([h1])
    ex_w2 = swap_start(ex_w2, [h1], "ffn2")
    (h2, n3), saved2 = _mixer_forward(h1, n2, gt2, win, wout, conv_w, conv_dw_b, conv_ln_g, conv_ln_b,
                                      attn_out_g, conv_out_g, next_norm=(ffn2_norm_g, sc3, sh3))
    wg2, wu2, wd2 = ex_w2.wait([h2])
    (h3,), saved3 = _ffn_forward(h2, n3, gt3, wg2, wu2, wd2, "ffn2")
    loss_part, dh3, d_final_g, df3, d_gt3 = _loss_head(h3, final_norm_g.reshape(1, D), target, saved3[5], gt3, 0.5,
                                                       "loss_head")

    (dh2, d_sh3, d_sc3, d_gain3, dmo, d_gt2), (ex_d2, ex_gu2) = _ffn_backward(
        dh3, df3, saved3, ffn2_norm_g, sc3, wg2, wu2, wd2, core_idx, "ffn2", prev=(saved2[-1], gt2, 1.0))
    (dh1, d_sh2, d_sc2, d_gain2, df1, d_gt1), ex_mix, small_mix = _mixer_backward(
        dh2, dmo, saved2, mix_norm_g, sc2, win, wout, conv_w, conv_ln_g, conv_ln_b, attn_out_g, conv_out_g, core_idx,
        prev=(f1, gt1, 0.5))
    d_cb, d_lng, d_lnb, d_attn_g, d_gco, d_cw = small_mix
    (dh0, d_sh1, d_sc1, d_gain1), (ex_d1, ex_gu1) = _ffn_backward(
        dh1, df1, saved1, ffn1_norm_g, sc1, wg1, wu1, wd1, core_idx, "ffn1", last=True)

    dmod = jnp.concatenate([d_sh1, d_sc1, d_gt1, d_sh2, d_sc2, d_gt2, d_sh3, d_sc3, d_gt3], axis=1)
    small = [d_gain1, d_gain2, d_gain3, d_final_g, d_cb, d_lng, d_lnb, d_attn_g, d_gco,
             d_cw[:CONV_KERNEL].reshape(1, CONV_KERNEL * CONV_WIDTH), loss_part]
    pk1 = jnp.concatenate([dmod] + small, axis=1)
    n1_ = -(-pk1.shape[1] // 1024) * 1024
    gathered = _all_gather8(_pad_cols(pk1, n1_).reshape(8, n1_ // 8), "gather_small").reshape(N_DEV, n1_)
    ex_gu1 = _rs_send_chunks(ex_gu1, [gathered], core_idx, "ffn1_gu")
    tot = _sum_rows(gathered, "sum_small")
    off = [0]

    def take(nel):
        out = tot[:, off[0]:off[0] + nel]
        off[0] += nel
        return out

    g_b_ada = take(9 * D)
    g_ffn1_norm, g_mix_norm, g_ffn2_norm, g_final = take(D), take(D), take(D), take(D)
    g_cb, g_lng, g_lnb, g_attn_g, g_gco = take(512), take(512), take(512), take(512), take(512)
    g_cw_full = take(CONV_KERNEL * CONV_WIDTH).reshape(CONV_KERNEL, CONV_WIDTH)
    loss = take(128)[0, 0]
    g_cw = lax.dynamic_slice(g_cw_full, (0, chip * 128), (CONV_KERNEL, 128))

    dmod_shard = lax.dynamic_slice(gathered[:, :9 * D], (0, chip * nmod), (N_DEV, nmod))
    dmod16 = jnp.pad(dmod_shard, ((0, N_DEV), (0, 0)))
    c_t16 = jnp.pad(c_all.T, ((0, 0), (0, N_DEV)))
    g_w_ada = _ada_wgrad(c_t16, dmod16, "ada_wgrad")

    names = ["w_ada", "b_ada", "ffn1_norm_g", "ffn1_w_gate", "ffn1_w_up", "ffn1_w_down", "mix_norm_g", "w_in",
             "conv_dw_w", "conv_dw_b", "conv_ln_g", "conv_ln_b", "attn_out_g", "conv_out_g", "w_out", "ffn2_norm_g",
             "ffn2_w_gate", "ffn2_w_up", "ffn2_w_down", "final_norm_g"]
    weights = dict(zip(names, [w_ada, b_ada, ffn1_norm_g, ffn1_w_gate, ffn1_w_up, ffn1_w_down, mix_norm_g, w_in,
                               conv_dw_w, conv_dw_b, conv_ln_g, conv_ln_b, attn_out_g, conv_out_g, w_out,
                               ffn2_norm_g, ffn2_w_gate, ffn2_w_up, ffn2_w_down, final_norm_g]))
    ms = dict(zip(names, [m_w_ada, m_b_ada, m_ffn1_norm_g, m_ffn1_w_gate, m_ffn1_w_up, m_ffn1_w_down, m_mix_norm_g,
                          m_w_in, m_conv_dw_w, m_conv_dw_b, m_conv_ln_g, m_conv_ln_b, m_attn_out_g, m_conv_out_g,
                          m_w_out, m_ffn2_norm_g, m_ffn2_w_gate, m_ffn2_w_up, m_ffn2_w_down, m_final_norm_g]))
    vs = dict(zip(names, [v_w_ada, v_b_ada, v_ffn1_norm_g, v_ffn1_w_gate, v_ffn1_w_up, v_ffn1_w_down, v_mix_norm_g,
                          v_w_in, v_conv_dw_w, v_conv_dw_b, v_conv_ln_g, v_conv_ln_b, v_attn_out_g, v_conv_out_g,
                          v_w_out, v_ffn2_norm_g, v_ffn2_w_gate, v_ffn2_w_up, v_ffn2_w_down, v_final_norm_g]))
    grads, deltas, new_ms, new_vs = {}, {}, {}, {}

    def adamw_big(nm, g2d, deps=(), transposed=False):
        shape = weights[nm].shape
        two_d = (shape[-2], shape[-1])

        def view(t):
            return t.reshape(two_d).T if transposed else t.reshape(two_d)

        d_, m_, v_, g_ = _adamw(view(weights[nm]), g2d, view(ms[nm]), view(vs[nm]), f"adamw_{nm}", deps=deps)
        grads[nm], deltas[nm], new_ms[nm], new_vs[nm] = (
            (t.T if transposed else t).reshape(shape) for t in (g_, d_, m_, v_))
        return d_

    d_ada = adamw_big("w_ada", g_w_ada, deps=[ex_gu1.token])
    small_grads = {"b_ada": g_b_ada, "ffn1_norm_g": g_ffn1_norm, "mix_norm_g": g_mix_norm, "conv_dw_w": g_cw,
                   "conv_dw_b": g_cb, "conv_ln_g": g_lng, "conv_ln_b": g_lnb, "attn_out_g": g_attn_g,
                   "conv_out_g": g_gco, "ffn2_norm_g": g_ffn2_norm, "final_norm_g": g_final}
    small_names = [nm for nm in names if nm in small_grads]

    def pack_small(arrs):
        flat = jnp.concatenate([arrs[nm].reshape(1, -1) for nm in small_names], axis=1)
        npad = -(-flat.shape[1] // 1024) * 1024
        return _pad_cols(flat, npad).reshape(8, npad // 8)

    d_s, m_s, v_s, _ = _adamw(pack_small(weights), pack_small(small_grads), pack_small(ms), pack_small(vs),
                           "adamw_small")
    pos = 0
    for nm in small_names:
        shape, nel = weights[nm].shape, weights[nm].size
        grads[nm] = small_grads[nm].reshape(shape)
        deltas[nm], new_ms[nm], new_vs[nm] = (t.reshape(1, -1)[:, pos:pos + nel].reshape(shape)
                                              for t in (d_s, m_s, v_s))
        pos += nel

    ex_d2 = _rs_send_share(ex_d2, [d_ada, d_s], chip_core, "ffn2_d")
    ex_gu2 = _rs_send_share(ex_gu2, [ex_d2.token], chip_core, "ffn2_gu")
    ex_mix = _rs_send_share(ex_mix, [ex_gu2.token], chip_core, "mix")
    ex_d1 = _rs_send_share(ex_d1, [ex_mix.token], chip_core, "ffn1_d")
    (g_wd2,) = _rs_finish(ex_d2, [ex_d1.token])
    last = [adamw_big("ffn2_w_down", g_wd2)]
    g_wg2, g_wu2 = _rs_finish(ex_gu2, last)
    last = [adamw_big("ffn2_w_gate", g_wg2, transposed=True), adamw_big("ffn2_w_up", g_wu2, transposed=True)]
    g_win, g_wout = _rs_finish(ex_mix, last)
    last = [adamw_big("w_in", g_win), adamw_big("w_out", g_wout)]
    (g_wd1,) = _rs_finish(ex_d1, last)
    last = [adamw_big("ffn1_w_down", g_wd1)]
    ex_gu1 = _rs_send_share(ex_gu1, last, chip_core, "ffn1_gu")
    g_wg1, g_wu1 = _rs_finish(ex_gu1, [])
    adamw_big("ffn1_w_gate", g_wg1, transposed=True)
    adamw_big("ffn1_w_up", g_wu1, transposed=True)

    return (loss, dh0[None], *[grads[nm] for nm in names], *[deltas[nm] for nm in names],
            *[new_ms[nm] for nm in names], *[new_vs[nm] for nm in names])
```

```python
import jax
import jax.numpy as jnp
import numpy as np
from jax import lax
from jax.experimental import pallas as pl
from jax.experimental.pallas import tpu as pltpu

F32 = jnp.float32
BF16 = jnp.bfloat16
MESH = pl.DeviceIdType.MESH

RMS_EPS = 1e-6
LN_EPS = 1e-5
HEAD_DIM = 64
ATTN_WIDTH = 512
CONV_WIDTH = 512
ATTN_BLOCK = 128
DILATIONS = (1, 4, 16)
SUPER_ROWS = ATTN_BLOCK * 16
ROPE_THETA = 10000.0
CONV_KERNEL = 31
HALO = 32
N_CHIPS = 4
N_DEV = 8
ADAM_LR, ADAM_B1, ADAM_B2, ADAM_EPS, ADAM_WD, ADAM_STEP = 0.001, 0.9, 0.999, 1e-08, 0.01, 10
VMEM_LIMIT_BYTES = 48 * 1024 * 1024
VMEM_LIMIT_STREAMING = 62 * 1024 * 1024
NEG = -1e30

NT = (((1,), (1,)), ((), ()))
TN = (((0,), (0,)), ((), ()))

ANY_SPEC = pl.BlockSpec(memory_space=pl.ANY)
HBM_SPEC = pl.BlockSpec(memory_space=pltpu.HBM)
SEM_SPEC = pl.BlockSpec(memory_space=pltpu.SEMAPHORE)
DATAFLOW = pltpu.SideEffectType.DATAFLOW_SIDE_EFFECTING
XY_FLIPS = ((0, 1), (1, 0), (1, 1))


def _params(sem=None, streaming=False):
    return pltpu.CompilerParams(dimension_semantics=sem,
                                vmem_limit_bytes=VMEM_LIMIT_STREAMING if streaming else VMEM_LIMIT_BYTES)


def _row_tile(rows, want):
    t = min(rows, want)
    assert rows % t == 0
    return t


def _sigmoid(x):
    return 1.0 / (1.0 + jnp.exp(-x))


def _deps(deps):
    return [d for d in deps if d is not None]


def _blind_to(body, n_in, n_dep):
    def wrapped(*refs):
        return body(*refs[:n_in], *refs[n_in + n_dep:])
    return wrapped


def _vec_spec(d, ngrid):
    if ngrid == 1:
        return pl.BlockSpec((1, d), lambda i: (0, 0))
    return pl.BlockSpec((1, d), lambda i, j: (0, 0))


def _norm_mod(h, gain, sc, sh, name, deps=()):
    S, D = h.shape
    tr = _row_tile(S, 512)
    deps = _deps(deps)

    def body(h_ref, g_ref, sc_ref, sh_ref, n_ref):
        x = h_ref[...]
        r = lax.rsqrt(jnp.mean(x * x, axis=-1, keepdims=True) + RMS_EPS)
        y = (x * r) * g_ref[...]
        n_ref[...] = (y * (1.0 + sc_ref[...]) + sh_ref[...]).astype(BF16)

    row = pl.BlockSpec((tr, D), lambda i: (i, 0))
    return pl.pallas_call(
        _blind_to(body, 4, len(deps)), name=name, grid=(S // tr,),
        in_specs=[row, _vec_spec(D, 1), _vec_spec(D, 1), _vec_spec(D, 1)] + [ANY_SPEC] * len(deps),
        out_specs=row, out_shape=jax.ShapeDtypeStruct((S, D), BF16),
        compiler_params=_params(("parallel",)),
    )(h, gain, sc, sh, *deps)


def _norm_mod_bwd(dn, h_in, gain, sc, dh_out, name, prev=None, deps=()):
    S, D = h_in.shape
    tr = _row_tile(S, 512)
    deps = _deps(deps)
    n_in = 5 if prev is None else 7

    def body(*refs):
        dn_ref, h_ref, g_ref, sc_ref, dho_ref = refs[:5]
        dh_ref, dsh_ref, dsc_ref, dg_ref = refs[n_in:n_in + 4]

        @pl.when(pl.program_id(0) == 0)
        def _():
            for ref in refs[n_in + 1:n_in + 4] + refs[n_in + 5:]:
                ref[...] = jnp.zeros_like(ref)

        x = h_ref[...]
        dn_ = dn_ref[...]
        g = g_ref[...]
        one_sc = 1.0 + sc_ref[...]
        r = lax.rsqrt(jnp.mean(x * x, axis=-1, keepdims=True) + RMS_EPS)
        xh = x * r
        dsh_ref[...] += jnp.sum(dn_, axis=0, keepdims=True)
        dsc_ref[...] += jnp.sum(dn_ * (xh * g), axis=0, keepdims=True)
        dg_ref[...] += jnp.sum(dn_ * one_sc * xh, axis=0, keepdims=True)
        dxh = dn_ * (g * one_sc)
        dh = dho_ref[...] + r * (dxh - xh * jnp.mean(dxh * xh, axis=-1, keepdims=True))
        dh_ref[...] = dh
        if prev is not None:
            _gate_back(dh, refs[5], refs[6], prev[2], refs[n_in + 4], refs[n_in + 5])

    row = pl.BlockSpec((tr, D), lambda i: (i, 0))
    vec = _vec_spec(D, 1)
    extra_in = [] if prev is None else [row, vec]
    extra_out = [] if prev is None else [row, vec]
    extra_shape = [] if prev is None else [jax.ShapeDtypeStruct((S, D), BF16), jax.ShapeDtypeStruct((1, D), F32)]
    return pl.pallas_call(
        _blind_to(body, n_in, len(deps)), name=name, grid=(S // tr,),
        in_specs=[row, row, vec, vec, row] + extra_in + [ANY_SPEC] * len(deps),
        out_specs=[row, vec, vec, vec] + extra_out,
        out_shape=[jax.ShapeDtypeStruct((S, D), F32)] + [jax.ShapeDtypeStruct((1, D), F32)] * 3 + extra_shape,
        compiler_params=_params(("arbitrary",)),
    )(dn, h_in, gain, sc, dh_out, *([] if prev is None else prev[:2]), *deps)


def _gate_back(dh, f_ref, gate_ref, coef, df_ref, dgate_ref):
    df_ref[...] = ((coef * gate_ref[...]) * dh).astype(BF16)
    dgate_ref[...] += jnp.sum(coef * dh * f_ref[...].astype(F32), axis=0, keepdims=True)


def _mm_residual_loss(lhs, w, h_in, gate, coef, gain, target, name):
    nk, S, kc = lhs.shape
    D = w.shape[2]
    tm = _row_tile(S, 512)

    def body(l_ref, w_ref, h_ref, gate_ref, g_ref, t_ref, loss_ref, dh_ref, dg_ref, df_ref, dgate_ref, acc_ref):
        k = pl.program_id(1)

        @pl.when((pl.program_id(0) == 0) & (k == 0))
        def _():
            loss_ref[...] = jnp.zeros_like(loss_ref)
            dg_ref[...] = jnp.zeros_like(dg_ref)
            dgate_ref[...] = jnp.zeros_like(dgate_ref)

        @pl.when(k == 0)
        def _():
            acc_ref[...] = jnp.zeros_like(acc_ref)

        acc_ref[...] += jnp.dot(l_ref[...], w_ref[...], preferred_element_type=F32)

        @pl.when(k == nk - 1)
        def _():
            f = acc_ref[...]
            cg = coef * gate_ref[...]
            x = h_ref[...] + cg * f
            g = g_ref[...]
            r = lax.rsqrt(jnp.mean(x * x, axis=-1, keepdims=True) + RMS_EPS)
            xh = x * r
            err = xh * g - t_ref[...]
            part = 0.5 * jnp.sum(jnp.mean(err * err, axis=-1, keepdims=True), axis=0, keepdims=True)
            loss_ref[...] += jnp.broadcast_to(part, loss_ref.shape)
            dy = err * (1.0 / D)
            dg_ref[...] += jnp.sum(dy * xh, axis=0, keepdims=True)
            dxh = dy * g
            dh = r * (dxh - xh * jnp.mean(dxh * xh, axis=-1, keepdims=True))
            dh_ref[...] = dh
            df_ref[...] = (cg * dh).astype(BF16)
            dgate_ref[...] += jnp.sum(coef * dh * f, axis=0, keepdims=True)

    row = pl.BlockSpec((tm, D), lambda m, k: (m, 0))
    vec = _vec_spec(D, 2)
    return pl.pallas_call(
        body, name=name, grid=(S // tm, nk),
        in_specs=[pl.BlockSpec((None, tm, kc), lambda m, k: (k, m, 0)),
                  pl.BlockSpec((None, kc, D), lambda m, k: (k, 0, 0)), row, vec, vec, row],
        out_specs=[pl.BlockSpec((1, 128), lambda m, k: (0, 0)), row, vec, row, vec],
        out_shape=[jax.ShapeDtypeStruct((1, 128), F32), jax.ShapeDtypeStruct((S, D), F32),
                   jax.ShapeDtypeStruct((1, D), F32), jax.ShapeDtypeStruct((S, D), BF16),
                   jax.ShapeDtypeStruct((1, D), F32)],
        scratch_shapes=[pltpu.VMEM((tm, D), F32)],
        compiler_params=_params(("arbitrary", "arbitrary")),
    )(lhs, w, h_in, gate, gain, target)


def _ffn_gate_up(n, wg_t, wu_t, name):
    S, D = n.shape
    nk, w, _ = wg_t.shape
    tm = _row_tile(S, 512)

    def body(n_ref, wg_ref, wu_ref, dga_ref, dup_ref, act_ref):
        x = n_ref[...]
        ga = lax.dot_general(x, wg_ref[...], NT, preferred_element_type=F32)
        up = lax.dot_general(x, wu_ref[...], NT, preferred_element_type=F32)
        sig = _sigmoid(ga)
        silu = ga * sig
        dga_ref[...] = (up * (sig * (1.0 + ga * (1.0 - sig)))).astype(BF16)
        dup_ref[...] = silu.astype(BF16)
        act_ref[...] = (silu * up).astype(BF16)

    wspec = pl.BlockSpec((None, w, D), lambda k, m: (k, 0, 0))
    ospec = pl.BlockSpec((None, tm, w), lambda k, m: (k, m, 0))
    out = jax.ShapeDtypeStruct((nk, S, w), BF16)
    return pl.pallas_call(
        body, name=name, grid=(nk, S // tm),
        in_specs=[pl.BlockSpec((tm, D), lambda k, m: (m, 0)), wspec, wspec],
        out_specs=[ospec, ospec, ospec], out_shape=[out, out, out],
        compiler_params=_params(("parallel", "parallel")),
    )(n, wg_t, wu_t)


def _mm_residual(lhs, w, h_in, gvec, coef, name, next_norm=None):
    nk, S, kc = lhs.shape
    D = w.shape[2]
    tm = _row_tile(S, 512)
    n_in = 4 if next_norm is None else 7

    def body(*refs):
        l_ref, w_ref, h_ref, g_ref = refs[:4]
        ho_ref, f_ref = refs[n_in:n_in + 2]
        acc_ref = refs[-1]
        k = pl.program_id(1)

        @pl.when(k == 0)
        def _():
            acc_ref[...] = jnp.zeros_like(acc_ref)

        acc_ref[...] += jnp.dot(l_ref[...], w_ref[...], preferred_element_type=F32)

        @pl.when(k == nk - 1)
        def _():
            f = acc_ref[...]
            f_ref[...] = f.astype(BF16)
            x = h_ref[...] + (coef * g_ref[...]) * f
            ho_ref[...] = x
            if next_norm is not None:
                ng_ref, sc_ref, sh_ref = refs[4:7]
                r = lax.rsqrt(jnp.mean(x * x, axis=-1, keepdims=True) + RMS_EPS)
                y = (x * r) * ng_ref[...]
                refs[n_in + 2][...] = (y * (1.0 + sc_ref[...]) + sh_ref[...]).astype(BF16)

    row = pl.BlockSpec((tm, D), lambda m, k: (m, 0))
    vec = _vec_spec(D, 2)
    with_n = next_norm is not None
    return pl.pallas_call(
        body, name=name, grid=(S // tm, nk),
        in_specs=[pl.BlockSpec((None, tm, kc), lambda m, k: (k, m, 0)),
                  pl.BlockSpec((None, kc, D), lambda m, k: (k, 0, 0)), row, vec] + [vec] * (3 * with_n),
        out_specs=[row, row] + [row] * with_n,
        out_shape=[jax.ShapeDtypeStruct((S, D), F32), jax.ShapeDtypeStruct((S, D), BF16)]
        + [jax.ShapeDtypeStruct((S, D), BF16)] * with_n,
        scratch_shapes=[pltpu.VMEM((tm, D), F32)],
        compiler_params=_params(("parallel", "arbitrary")),
    )(lhs, w, h_in, gvec, *(next_norm or ()))


def _mm_cols_rope(n, w, cos, sin, name):
    S, D = n.shape
    nk, _, wd = w.shape
    assert wd % 128 == 0
    tm = _row_tile(S, 512)
    q_groups = ATTN_WIDTH // 128
    scale = HEAD_DIM ** -0.5

    def body(n_ref, w_ref, cos_ref, sin_ref, o_ref):
        x = jnp.dot(n_ref[...], w_ref[...], preferred_element_type=F32)
        first = pl.program_id(0) * (wd // 128)
        c, s = cos_ref[...], sin_ref[...]
        for j in range(wd // 128):
            lanes = slice(j * 128, (j + 1) * 128)
            group = first + j
            xj = x[:, lanes]
            rot = _rotate(xj, c, s) * jnp.where(group < q_groups, scale, 1.0)
            o_ref[:, lanes] = jnp.where(group < 2 * q_groups, rot, xj)

    tab = pl.BlockSpec((tm, 128), lambda k, m: (m, 0))
    return pl.pallas_call(
        body, name=name, grid=(nk, S // tm),
        in_specs=[pl.BlockSpec((tm, D), lambda k, m: (m, 0)), pl.BlockSpec((None, D, wd), lambda k, m: (k, 0, 0)),
                  tab, tab],
        out_specs=pl.BlockSpec((tm, wd), lambda k, m: (m, k)),
        out_shape=jax.ShapeDtypeStruct((S, nk * wd), F32),
        compiler_params=_params(("parallel", "parallel")),
    )(n, w, cos, sin)


def _ffn_dact(df, wd, fga, fup, name, deps=()):
    S, D = df.shape
    nk, w, _ = wd.shape
    tm = _row_tile(S, 512)
    deps = _deps(deps)

    def body(df_ref, wd_ref, fga_ref, fup_ref, dga_ref, dup_ref):
        dact = lax.dot_general(df_ref[...], wd_ref[...], NT, preferred_element_type=F32)
        dga_ref[...] = (dact * fga_ref[...].astype(F32)).astype(BF16)
        dup_ref[...] = (dact * fup_ref[...].astype(F32)).astype(BF16)

    cspec = pl.BlockSpec((None, tm, w), lambda k, m: (k, m, 0))
    out = jax.ShapeDtypeStruct((nk, S, w), BF16)
    return pl.pallas_call(
        _blind_to(body, 4, len(deps)), name=name, grid=(nk, S // tm),
        in_specs=[pl.BlockSpec((tm, D), lambda k, m: (m, 0)), pl.BlockSpec((None, w, D), lambda k, m: (k, 0, 0)),
                  cspec, cspec] + [ANY_SPEC] * len(deps),
        out_specs=[cspec, cspec], out_shape=[out, out],
        compiler_params=_params(("parallel", "parallel")),
    )(df, wd, fga, fup, *deps)


def _mm_reduce(lhs_list, w_list, chunked3d, w_is_kd, name, deps=()):
    nk = w_list[0].shape[0]
    kc, D = w_list[0].shape[1:] if w_is_kd else w_list[0].shape[:0:-1]
    S = lhs_list[0].shape[1] if chunked3d else lhs_list[0].shape[0]
    tm = _row_tile(S, 512)
    npair = len(lhs_list)
    deps = _deps(deps)

    def body(*refs):
        l_refs, w_refs = refs[:npair], refs[npair:2 * npair]
        o_ref, acc_ref = refs[2 * npair], refs[2 * npair + 1]
        k = pl.program_id(1)

        @pl.when(k == 0)
        def _():
            acc_ref[...] = jnp.zeros_like(acc_ref)

        for l_ref, w_ref in zip(l_refs, w_refs):
            if w_is_kd:
                acc_ref[...] += jnp.dot(l_ref[...], w_ref[...], preferred_element_type=F32)
            else:
                acc_ref[...] += lax.dot_general(l_ref[...], w_ref[...], NT, preferred_element_type=F32)

        @pl.when(k == nk - 1)
        def _():
            o_ref[...] = acc_ref[...]

    if chunked3d:
        lspec = pl.BlockSpec((None, tm, kc), lambda m, k: (k, m, 0))
    else:
        lspec = pl.BlockSpec((tm, kc), lambda m, k: (m, k))
    wspec = pl.BlockSpec((None,) + tuple(w_list[0].shape[1:]), lambda m, k: (k, 0, 0))
    return pl.pallas_call(
        _blind_to(body, 2 * npair, len(deps)), name=name, grid=(S // tm, nk),
        in_specs=[lspec] * npair + [wspec] * npair + [ANY_SPEC] * len(deps),
        out_specs=pl.BlockSpec((tm, D), lambda m, k: (m, 0)),
        out_shape=jax.ShapeDtypeStruct((S, D), F32),
        scratch_shapes=[pltpu.VMEM((tm, D), F32)],
        compiler_params=_params(("parallel", "arbitrary")),
    )(*lhs_list, *w_list, *deps)


def _wgrad_chunk_lhs(lhs_list, rhs, name, deps=()):
    nk, S, w = lhs_list[0].shape
    D = rhs.shape[1]
    ts = _row_tile(S, 512)
    ns = S // ts
    nl = len(lhs_list)
    deps = _deps(deps)

    def body(*refs):
        l_refs, r_ref = refs[:nl], refs[nl]
        o_refs, acc_refs = refs[nl + 1:2 * nl + 1], refs[2 * nl + 1:]
        s = pl.program_id(1)

        @pl.when(s == 0)
        def _():
            for acc_ref in acc_refs:
                acc_ref[...] = jnp.zeros_like(acc_ref)

        x = r_ref[...]
        for l_ref, acc_ref in zip(l_refs, acc_refs):
            acc_ref[...] += lax.dot_general(l_ref[...], x, TN, preferred_element_type=F32)

        @pl.when(s == ns - 1)
        def _():
            for o_ref, acc_ref in zip(o_refs, acc_refs):
                o_ref[...] = acc_ref[...]

    return pl.pallas_call(
        _blind_to(body, nl + 1, len(deps)), name=name, grid=(nk, ns),
        in_specs=[pl.BlockSpec((None, ts, w), lambda k, s: (k, s, 0))] * nl
        + [pl.BlockSpec((ts, D), lambda k, s: (s, 0))] + [ANY_SPEC] * len(deps),
        out_specs=[pl.BlockSpec((None, w, D), lambda k, s: (k, 0, 0))] * nl,
        out_shape=[jax.ShapeDtypeStruct((nk, w, D), F32)] * nl,
        scratch_shapes=[pltpu.VMEM((w, D), F32)] * nl,
        compiler_params=_params(("parallel", "arbitrary")),
    )(*lhs_list, rhs, *deps)


def _wgrad_chunk_rhs(lhs, rhs_list, nk, chunked3d, name, deps=()):
    S, D = lhs.shape
    w = rhs_list[0].shape[2] if chunked3d else rhs_list[0].shape[1] // nk
    ts = _row_tile(S, 512)
    ns = S // ts
    nr = len(rhs_list)
    deps = _deps(deps)

    def body(*refs):
        l_ref, r_refs = refs[0], refs[1:1 + nr]
        o_refs, acc_refs = refs[1 + nr:1 + 2 * nr], refs[1 + 2 * nr:]
        s = pl.program_id(1)

        @pl.when(s == 0)
        def _():
            for acc_ref in acc_refs:
                acc_ref[...] = jnp.zeros_like(acc_ref)

        x = l_ref[...]
        for r_ref, acc_ref in zip(r_refs, acc_refs):
            acc_ref[...] += lax.dot_general(x, r_ref[...], TN, preferred_element_type=F32)

        @pl.when(s == ns - 1)
        def _():
            for o_ref, acc_ref in zip(o_refs, acc_refs):
                o_ref[...] = acc_ref[...]

    if chunked3d:
        rspec = pl.BlockSpec((None, ts, w), lambda k, s: (k, s, 0))
    else:
        rspec = pl.BlockSpec((ts, w), lambda k, s: (s, k))
    ospec = pl.BlockSpec((None, D, w), lambda k, s: (k, 0, 0))
    return pl.pallas_call(
        _blind_to(body, 1 + nr, len(deps)), name=name, grid=(nk, ns),
        in_specs=[pl.BlockSpec((ts, D), lambda k, s: (s, 0))] + [rspec] * nr + [ANY_SPEC] * len(deps),
        out_specs=[ospec] * nr,
        out_shape=[jax.ShapeDtypeStruct((nk, D, w), F32)] * nr,
        scratch_shapes=[pltpu.VMEM((D, w), F32)] * nr,
        compiler_params=_params(("parallel", "arbitrary")),
    )(lhs, *rhs_list, *deps)


def _rope_tables(S):
    pos = np.arange(S, dtype=np.float32)
    inv_freq = (ROPE_THETA ** (-np.arange(0, HEAD_DIM, 2, dtype=np.float32) / HEAD_DIM)).astype(np.float32)
    ang = (pos[:, None] * inv_freq[None, :]).astype(np.float64)
    cos, sin = np.cos(ang).astype(np.float32), np.sin(ang).astype(np.float32)
    cos2 = np.concatenate([cos, cos, cos, cos], axis=1)
    sin2 = np.concatenate([-sin, sin, -sin, sin], axis=1)
    return jnp.asarray(cos2), jnp.asarray(sin2)


def _rotate(t, cos, sin_signed):
    half = HEAD_DIM // 2
    lane = lax.broadcasted_iota(jnp.int32, t.shape, 1)
    first = (lane % HEAD_DIM) < half
    partner = jnp.where(first, pltpu.roll(t, 128 - half, 1), pltpu.roll(t, half, 1))
    return t * cos + partner * sin_signed


def _band_mask(T, has_prev):
    qi = lax.broadcasted_iota(jnp.int32, (T, 2 * T), 0)
    kj = lax.broadcasted_iota(jnp.int32, (T, 2 * T), 1)
    return ((kj < T) & (kj >= qi) & has_prev) | ((kj >= T) & (kj - T <= qi))


def _stack_heads(x, head0):
    zero = jnp.zeros_like(x)
    return jnp.concatenate([jnp.where(head0, x, zero), jnp.where(head0, zero, x)], axis=0)


def _branch_blocks(rows, dilation):
    T = min(ATTN_BLOCK, rows // dilation)
    return T, rows // T


def _block_rows(base, T, dilation):
    if dilation == 1:
        return pl.ds(pl.multiple_of(base, T), T)
    return pl.ds(base, T, stride=dilation)


def _qkv_specs(S):
    groups = ATTN_WIDTH // 128
    return [pl.BlockSpec((S, 128), lambda j, off=t * groups: (0, off + j)) for t in range(3)]


def _attn_fwd(proj, name):
    S, A = proj.shape[0], ATTN_WIDTH
    sup = min(S, SUPER_ROWS)
    nd = len(DILATIONS)
    assert S % sup == 0

    def body(q_ref, k_ref, v_ref, attn_ref, lse_ref, acc_s, m_s, l_s):
        lane = lax.broadcasted_iota(jnp.int32, (1, 128), 1)
        head0 = lane < HEAD_DIM

        def supertile(st, carry):
            row0 = st * sup
            for di, dil in enumerate(DILATIONS):
                T, nblk = _branch_blocks(sup, dil)
                span = T * dil
                assert T == ATTN_BLOCK or sup == S

                def block(idx, c2, di=di, dil=dil, T=T, span=span):
                    r = idx % dil
                    loc = (idx // dil) * span + r
                    base = row0 + loc
                    rows = _block_rows(base, T, dil)
                    prev = _block_rows(jnp.maximum(base - span, r), T, dil)
                    qb = q_ref[rows, :].astype(BF16)
                    k2 = jnp.concatenate([k_ref[prev, :], k_ref[rows, :]], axis=0).astype(BF16)
                    v2 = jnp.concatenate([v_ref[prev, :], v_ref[rows, :]], axis=0).astype(BF16)
                    valid = _band_mask(T, base >= span)
                    q2 = _stack_heads(qb, head0)
                    s = lax.dot_general(q2, k2, NT, preferred_element_type=F32)
                    s = jnp.where(jnp.concatenate([valid, valid], axis=0), s, NEG)
                    m = jnp.max(s, axis=-1, keepdims=True)
                    p = jnp.exp(s - m)
                    l = jnp.sum(p, axis=-1, keepdims=True)
                    acc = jnp.dot(p.astype(BF16), v2, preferred_element_type=F32)
                    lrows = _block_rows(di * sup + loc, T, dil)
                    acc_s[lrows, :] = jnp.where(head0, acc[:T], acc[T:])
                    m_s[lrows, :] = jnp.where(head0, m[:T], m[T:])
                    l_s[lrows, :] = jnp.where(head0, l[:T], l[T:])
                    return c2

                lax.fori_loop(0, nblk, block, 0, unroll=8)

            chunk = min(sup, 256)

            def merge(ci, c2):
                lr = [pl.ds(pl.multiple_of(di * sup + ci * chunk, chunk), chunk) for di in range(nd)]
                gr = pl.ds(pl.multiple_of(row0 + ci * chunk, chunk), chunk)
                m0, m1, m2 = m_s[lr[0], :], m_s[lr[1], :], m_s[lr[2], :]
                mm = jnp.maximum(jnp.maximum(m0, m1), m2)
                w0, w1, w2 = jnp.exp(m0 - mm), jnp.exp(m1 - mm), jnp.exp(m2 - mm)
                den = (w0 * l_s[lr[0], :] + w1 * l_s[lr[1], :]) + w2 * l_s[lr[2], :]
                num = (w0 * acc_s[lr[0], :] + w1 * acc_s[lr[1], :]) + w2 * acc_s[lr[2], :]
                attn_ref[gr, :] = num / den
                lse_ref[gr, :] = mm + jnp.log(den)
                return c2

            lax.fori_loop(0, sup // chunk, merge, 0)
            return carry

        lax.fori_loop(0, S // sup, supertile, 0)

    blk = pl.BlockSpec((S, 128), lambda j: (0, j))
    out = jax.ShapeDtypeStruct((S, A), F32)
    return pl.pallas_call(
        body, name=name, grid=(A // 128,),
        in_specs=_qkv_specs(S), out_specs=[blk, blk], out_shape=[out, out],
        scratch_shapes=[pltpu.VMEM((nd * sup, 128), F32)] * 3,
        compiler_params=_params(("parallel",)),
    )(proj, proj, proj)


def _attn_bwd(proj, da, lse, delta, name):
    S, A = da.shape

    def body(q_ref, k_ref, v_ref, da_ref, lse_ref, dl_ref, dq_ref, dk_ref, dv_ref):
        lane = lax.broadcasted_iota(jnp.int32, (1, 128), 1)
        head0 = lane < HEAD_DIM
        dq_ref[...] = jnp.zeros_like(dq_ref)
        dk_ref[...] = jnp.zeros_like(dk_ref)
        dv_ref[...] = jnp.zeros_like(dv_ref)
        for dil in DILATIONS:
            T, nblk = _branch_blocks(S, dil)
            span = T * dil

            def block(idx, carry, dil=dil, T=T, span=span):
                r = idx % dil
                base = (idx // dil) * span + r
                rows = _block_rows(base, T, dil)
                prev = _block_rows(jnp.maximum(base - span, r), T, dil)
                qb, dab = q_ref[rows, :].astype(BF16), da_ref[rows, :].astype(BF16)
                k2 = jnp.concatenate([k_ref[prev, :], k_ref[rows, :]], axis=0).astype(BF16)
                v2 = jnp.concatenate([v_ref[prev, :], v_ref[rows, :]], axis=0).astype(BF16)
                lse_b, dl_b = lse_ref[rows, :], dl_ref[rows, :]
                valid = _band_mask(T, base >= span)
                valid2 = jnp.concatenate([valid, valid], axis=0)
                q2, da2 = _stack_heads(qb, head0), _stack_heads(dab, head0)
                lse2 = jnp.concatenate([lse_b[:, 0:1], lse_b[:, HEAD_DIM:HEAD_DIM + 1]], axis=0)
                dl2 = jnp.concatenate([dl_b[:, 0:1], dl_b[:, HEAD_DIM:HEAD_DIM + 1]], axis=0)
                s = lax.dot_general(q2, k2, NT, preferred_element_type=F32)
                p = jnp.where(valid2, jnp.exp(s - lse2), 0.0)
                dp = lax.dot_general(da2, v2, NT, preferred_element_type=F32)
                ds = (p * (dp - dl2)).astype(BF16)
                dq2 = jnp.dot(ds, k2, preferred_element_type=F32)
                dk2 = lax.dot_general(ds, q2, TN, preferred_element_type=F32)
                dv2 = lax.dot_general(p.astype(BF16), da2, TN, preferred_element_type=F32)
                dq_ref[rows, :] += jnp.where(head0, dq2[:T], dq2[T:])
                dk_ref[rows, :] += dk2[T:]
                dv_ref[rows, :] += dv2[T:]
                dk_ref[prev, :] += dk2[:T]
                dv_ref[prev, :] += dv2[:T]
                return carry

            lax.fori_loop(0, nblk, block, 0, unroll=8)

    blk = pl.BlockSpec((S, 128), lambda j: (0, j))
    out = jax.ShapeDtypeStruct((S, A), F32)
    return pl.pallas_call(
        body, name=name, grid=(A // 128,),
        in_specs=_qkv_specs(S) + [blk] * 3, out_specs=[blk] * 3, out_shape=[out] * 3,
        compiler_params=_params(("parallel",)),
    )(proj, proj, proj, da, lse, delta)


SUBLANES = 8
CONV_CHUNK = 64
FIRST_TAP = HALO - (CONV_KERNEL - 1)


def _store_shifted(shift_s, win, rows):
    shift_s[0, pl.ds(0, rows), :] = win
    for b in range(1, SUBLANES):
        shift_s[b, pl.ds(0, rows - SUBLANES), :] = win[b:b + rows - SUBLANES, :]


def _glu_window(a_ref, b_ref, ah_ref, bh_ref, first):
    u0 = a_ref[...] * _sigmoid(b_ref[...])
    u0h = ah_ref[...] * _sigmoid(bh_ref[...])
    u0h = jnp.where(first, jnp.zeros_like(u0h), u0h)
    return jnp.concatenate([u0h, u0], axis=0)


def _conv_norms(u1, lng, lnb):
    mu = jnp.mean(u1, axis=-1, keepdims=True)
    xc = u1 - mu
    rstd = lax.rsqrt(jnp.mean(xc * xc, axis=-1, keepdims=True) + LN_EPS)
    u1h = xc * rstd
    u2 = u1h * lng + lnb
    sig = _sigmoid(u2)
    u3 = u2 * sig
    r = lax.rsqrt(jnp.mean(u3 * u3, axis=-1, keepdims=True) + RMS_EPS)
    return rstd, u1h, u2, sig, u3, r


def _conv_specs(tr, C, col_a, col_b):
    per = tr // HALO

    def tile(col):
        return pl.BlockSpec((tr, C), lambda i: (i, col))

    def halo(col):
        return pl.BlockSpec((HALO, C), lambda i: (jnp.maximum(i * per - 1, 0), col))

    return tile(col_a), tile(col_b), halo(col_a), halo(col_b)


def _mixer_merge(proj, attn, cw, cb, lng, lnb, gat, gco, name):
    S = proj.shape[0]
    C = CONV_WIDTH
    A = attn.shape[1]
    tr = _row_tile(S, 256)

    def body(a_ref, b_ref, ah_ref, bh_ref, at_ref, w_ref, cb_ref, lng_ref, lnb_ref, gat_ref, gco_ref, u1_ref, y_ref,
             shift_s):
        _store_shifted(shift_s, _glu_window(a_ref, b_ref, ah_ref, bh_ref, pl.program_id(0) == 0), tr + HALO)

        def chunk(rc, carry):
            r0 = pl.multiple_of(rc * CONV_CHUNK, CONV_CHUNK)
            for lb in range(C // 128):
                lanes = slice(lb * 128, (lb + 1) * 128)
                acc = jnp.broadcast_to(cb_ref[:, lanes], (CONV_CHUNK, 128))
                for j in range(CONV_KERNEL):
                    a8, b = divmod(FIRST_TAP + j, SUBLANES)
                    acc = acc + w_ref[j:j + 1, lanes] * shift_s[b, pl.ds(r0 + a8 * SUBLANES, CONV_CHUNK), lanes]
                u1_ref[pl.ds(r0, CONV_CHUNK), lanes] = acc
            return carry

        lax.fori_loop(0, tr // CONV_CHUNK, chunk, 0)
        _, _, _, _, u3, r = _conv_norms(u1_ref[...], lng_ref[...], lnb_ref[...])
        y_ref[:, A:] = ((u3 * r) * gco_ref[...]).astype(BF16)
        x = at_ref[...]
        ra = lax.rsqrt(jnp.mean(x * x, axis=-1, keepdims=True) + RMS_EPS)
        y_ref[:, :A] = ((x * ra) * gat_ref[...]).astype(BF16)

    ta, tb, ha, hb = _conv_specs(tr, C, 3, 4)
    row = pl.BlockSpec((tr, C), lambda i: (i, 0))
    vec = _vec_spec(C, 1)
    return pl.pallas_call(
        body, name=name, grid=(S // tr,),
        in_specs=[ta, tb, ha, hb, pl.BlockSpec((tr, A), lambda i: (i, 0)), pl.BlockSpec((HALO, C), lambda i: (0, 0)),
                  vec, vec, vec, _vec_spec(A, 1), vec],
        out_specs=[row, pl.BlockSpec((tr, A + C), lambda i: (i, 0))],
        out_shape=[jax.ShapeDtypeStruct((S, C), F32), jax.ShapeDtypeStruct((S, A + C), BF16)],
        scratch_shapes=[pltpu.VMEM((SUBLANES, tr + HALO, C), F32)],
        compiler_params=_params(("parallel",)),
    )(proj, proj, proj, proj, attn, cw, cb, lng, lnb, gat, gco)


def _mix_out_bwd(dmo, wout, attn, u1, gat, lng, lnb, gco, name):
    S, D = dmo.shape
    A, C = attn.shape[1], u1.shape[1]
    tr = _row_tile(S, 256)

    def body(dmo_ref, w_ref, a_ref, u1_ref, gat_ref, lng_ref, lnb_ref, gco_ref,
             da_ref, dl_ref, du1_ref, dgat_ref, dgco_ref, dlng_ref, dlnb_ref, dcb_ref):
        @pl.when(pl.program_id(0) == 0)
        def _():
            for ref in (dgat_ref, dgco_ref, dlng_ref, dlnb_ref, dcb_ref):
                ref[...] = jnp.zeros_like(ref)

        dy = lax.dot_general(dmo_ref[...], w_ref[...], NT, preferred_element_type=F32)
        dya, dyc = dy[:, :A], dy[:, A:]
        x = a_ref[...]
        r = lax.rsqrt(jnp.mean(x * x, axis=-1, keepdims=True) + RMS_EPS)
        xh = x * r
        dgat_ref[...] += jnp.sum(dya * xh, axis=0, keepdims=True)
        dxh = dya * gat_ref[...]
        dx = r * (dxh - xh * jnp.mean(dxh * xh, axis=-1, keepdims=True))
        da_ref[...] = dx
        hi = lax.broadcasted_iota(jnp.int32, (A, A), 0) // HEAD_DIM
        hj = lax.broadcasted_iota(jnp.int32, (A, A), 1) // HEAD_DIM
        same_head = (hi == hj).astype(F32)
        dl_ref[...] = jnp.dot(dx * x, same_head, preferred_element_type=F32, precision=lax.Precision.HIGHEST)

        lng = lng_ref[...]
        rstd, u1h, u2, sig, u3, rc = _conv_norms(u1_ref[...], lng, lnb_ref[...])
        u3h = u3 * rc
        dgco_ref[...] += jnp.sum(dyc * u3h, axis=0, keepdims=True)
        du3h = dyc * gco_ref[...]
        du3 = rc * (du3h - u3h * jnp.mean(du3h * u3h, axis=-1, keepdims=True))
        du2 = du3 * (sig * (1.0 + u2 * (1.0 - sig)))
        dlng_ref[...] += jnp.sum(du2 * u1h, axis=0, keepdims=True)
        dlnb_ref[...] += jnp.sum(du2, axis=0, keepdims=True)
        du1h = du2 * lng
        du1 = rstd * (du1h - jnp.mean(du1h, axis=-1, keepdims=True)
                      - u1h * jnp.mean(du1h * u1h, axis=-1, keepdims=True))
        du1_ref[...] = du1
        dcb_ref[...] += jnp.sum(du1, axis=0, keepdims=True)

    arow = pl.BlockSpec((tr, A), lambda i: (i, 0))
    crow = pl.BlockSpec((tr, C), lambda i: (i, 0))
    avec, cvec = _vec_spec(A, 1), _vec_spec(C, 1)
    return pl.pallas_call(
        body, name=name, grid=(S // tr,),
        in_specs=[pl.BlockSpec((tr, D), lambda i: (i, 0)), pl.BlockSpec((A + C, D), lambda i: (0, 0)), arow, crow,
                  avec, cvec, cvec, cvec],
        out_specs=[arow, arow, crow, avec, cvec, cvec, cvec, cvec],
        out_shape=[jax.ShapeDtypeStruct((S, A), F32)] * 2 + [jax.ShapeDtypeStruct((S, C), F32)]
        + [jax.ShapeDtypeStruct((1, A), F32)] + [jax.ShapeDtypeStruct((1, C), F32)] * 4,
        compiler_params=_params(("arbitrary",)),
    )(dmo, wout, attn, u1, gat, lng, lnb, gco)


def _dproj(du1, proj, cw, dq, dk, dv, cos, sin, name):
    S, C = du1.shape
    A = dq.shape[1]
    tr = _row_tile(S, 256)
    nt = S // tr
    per = tr // HALO
    scale = HEAD_DIM ** -0.5

    def body(du_ref, dun_ref, a_ref, b_ref, ah_ref, bh_ref, w_ref, dq_ref, dk_ref, dv_ref, cos_ref, sin_ref,
             dp_ref, dw_ref, win_s, dwin_s, du0_s, tap_s):
        i = pl.program_id(0)

        @pl.when(i == 0)
        def _():
            dw_ref[...] = jnp.zeros_like(dw_ref)

        _store_shifted(win_s, _glu_window(a_ref, b_ref, ah_ref, bh_ref, i == 0), tr + HALO)
        nxt = jnp.where(i == nt - 1, jnp.zeros_like(dun_ref[...]), dun_ref[...])
        _store_shifted(dwin_s, jnp.concatenate([du_ref[...], nxt], axis=0), tr + HALO)
        tap_s[...] = jnp.zeros_like(tap_s)

        def chunk(rc, carry):
            r0 = pl.multiple_of(rc * CONV_CHUNK, CONV_CHUNK)
            for lb in range(C // 128):
                lanes = slice(lb * 128, (lb + 1) * 128)
                du = du_ref[pl.ds(r0, CONV_CHUNK), lanes]
                acc = jnp.zeros((CONV_CHUNK, 128), F32)
                for j in range(CONV_KERNEL):
                    a8, b = divmod(CONV_KERNEL - 1 - j, SUBLANES)
                    acc = acc + w_ref[j:j + 1, lanes] * dwin_s[b, pl.ds(r0 + a8 * SUBLANES, CONV_CHUNK), lanes]
                    a8, b = divmod(FIRST_TAP + j, SUBLANES)
                    prod = du * win_s[b, pl.ds(r0 + a8 * SUBLANES, CONV_CHUNK), lanes]
                    part = prod[0:SUBLANES]
                    for g in range(1, CONV_CHUNK // SUBLANES):
                        part = part + prod[g * SUBLANES:(g + 1) * SUBLANES]
                    tap_s[j * SUBLANES:(j + 1) * SUBLANES, lanes] += part
                du0_s[pl.ds(r0, CONV_CHUNK), lanes] = acc
            return carry

        lax.fori_loop(0, tr // CONV_CHUNK, chunk, 0)
        taps = [jnp.sum(tap_s[j * SUBLANES:(j + 1) * SUBLANES, :], axis=0, keepdims=True)
                for j in range(CONV_KERNEL)]
        taps.append(jnp.zeros((HALO - CONV_KERNEL, C), F32))
        dw_ref[...] += jnp.concatenate(taps, axis=0)
        du0 = du0_s[...]
        a, sig = a_ref[...], _sigmoid(b_ref[...])
        dp_ref[:, 3 * A:3 * A + C] = (du0 * sig).astype(BF16)
        dp_ref[:, 3 * A + C:] = (du0 * a * sig * (1.0 - sig)).astype(BF16)
        cos_, nsin = cos_ref[...], -sin_ref[...]
        for j in range(A // 128):
            lanes = slice(j * 128, (j + 1) * 128)
            dp_ref[:, j * 128:(j + 1) * 128] = (_rotate(dq_ref[:, lanes], cos_, nsin) * scale).astype(BF16)
            dp_ref[:, A + j * 128:A + (j + 1) * 128] = _rotate(dk_ref[:, lanes], cos_, nsin).astype(BF16)
        dp_ref[:, 2 * A:3 * A] = dv_ref[...].astype(BF16)

    ta, tb, ha, hb = _conv_specs(tr, C, 3, 4)
    row = pl.BlockSpec((tr, C), lambda i: (i, 0))
    arow = pl.BlockSpec((tr, A), lambda i: (i, 0))
    tab = pl.BlockSpec((tr, 128), lambda i: (i, 0))
    nxt = pl.BlockSpec((HALO, C), lambda i: (jnp.minimum((i + 1) * per, S // HALO - 1), 0))
    wspec = pl.BlockSpec((HALO, C), lambda i: (0, 0))
    return pl.pallas_call(
        body, name=name, grid=(nt,),
        in_specs=[row, nxt, ta, tb, ha, hb, wspec, arow, arow, arow, tab, tab],
        out_specs=[pl.BlockSpec((tr, 3 * A + 2 * C), lambda i: (i, 0)), wspec],
        out_shape=[jax.ShapeDtypeStruct((S, 3 * A + 2 * C), BF16), jax.ShapeDtypeStruct((HALO, C), F32)],
        scratch_shapes=[pltpu.VMEM((SUBLANES, tr + HALO, C), F32), pltpu.VMEM((SUBLANES, tr + HALO, C), F32),
                        pltpu.VMEM((tr, C), F32), pltpu.VMEM((HALO * SUBLANES, C), F32)],
        compiler_params=_params(("arbitrary",)),
    )(du1, du1, proj, proj, proj, proj, cw, dq, dk, dv, cos, sin)


def _ada_fwd(c_all, w, b, name):
    B, D = c_all.shape
    N = w.shape[1]
    tn = 768 if N % 768 == 0 else N

    def body(c_ref, w_ref, b_ref, o_ref):
        c = c_ref[...]
        a = (c * _sigmoid(c)).astype(BF16)
        o_ref[...] = jnp.dot(a, w_ref[...].astype(BF16), preferred_element_type=F32) + b_ref[...]

    return pl.pallas_call(
        body, name=name, grid=(N // tn,),
        in_specs=[pl.BlockSpec((B, D), lambda j: (0, 0)), pl.BlockSpec((D, tn), lambda j: (0, j)),
                  pl.BlockSpec((1, tn), lambda j: (0, j))],
        out_specs=pl.BlockSpec((B, tn), lambda j: (0, j)),
        out_shape=jax.ShapeDtypeStruct((B, N), F32),
        compiler_params=_params(("parallel",)),
    )(c_all, w, b)


def _ada_wgrad(c_t, dmod, name):
    D, B = c_t.shape
    N = dmod.shape[1]
    tn = 768 if N % 768 == 0 else N

    def body(c_ref, d_ref, o_ref):
        c = c_ref[...]
        a = (c * _sigmoid(c)).astype(BF16)
        o_ref[...] = jnp.dot(a, d_ref[...].astype(BF16), preferred_element_type=F32)

    return pl.pallas_call(
        body, name=name, grid=(N // tn,),
        in_specs=[pl.BlockSpec((D, B), lambda j: (0, 0)), pl.BlockSpec((B, tn), lambda j: (0, j))],
        out_specs=pl.BlockSpec((D, tn), lambda j: (0, j)),
        out_shape=jax.ShapeDtypeStruct((D, N), F32),
        compiler_params=_params(("parallel",)),
    )(c_t, dmod)


def _sum_rows(x, name):
    R, N = x.shape

    def body(x_ref, o_ref):
        acc = x_ref[0:1, :]
        for r in range(1, R):
            acc = acc + x_ref[r:r + 1, :]
        o_ref[...] = acc

    return pl.pallas_call(
        body, name=name, out_shape=jax.ShapeDtypeStruct((1, N), F32),
        compiler_params=_params(),
    )(x)


def _adamw(w, g, m, v, name, deps=()):
    R, C = w.shape
    tr = _row_tile(R, 256) if R % 256 == 0 else R
    bc1 = 1.0 - ADAM_B1 ** ADAM_STEP
    bc2 = 1.0 - ADAM_B2 ** ADAM_STEP
    deps = _deps(deps)

    def body(w_ref, g_ref, m_ref, v_ref, d_ref, mo_ref, vo_ref, go_ref):
        g_ = g_ref[...]
        m_ = ADAM_B1 * m_ref[...] + (1.0 - ADAM_B1) * g_
        v_ = ADAM_B2 * v_ref[...] + (1.0 - ADAM_B2) * (g_ * g_)
        mo_ref[...] = m_
        vo_ref[...] = v_
        go_ref[...] = g_
        d_ref[...] = -ADAM_LR * ((m_ / bc1) / (jnp.sqrt(v_ / bc2) + ADAM_EPS) + ADAM_WD * w_ref[...])

    row = pl.BlockSpec((tr, C), lambda i: (i, 0))
    out = jax.ShapeDtypeStruct((R, C), F32)
    return pl.pallas_call(
        _blind_to(body, 4, len(deps)), name=name, grid=(R // tr,),
        in_specs=[row] * 4 + [ANY_SPEC] * len(deps), out_specs=[row] * 4, out_shape=[out] * 4,
        compiler_params=_params(("parallel",), streaming=True),
    )(w, g, m, v, *deps)


def _coords():
    return lax.axis_index("x"), lax.axis_index("y"), lax.axis_index("c")


def _all_gather8(x, name, deps=()):
    R, N = x.shape
    assert R == 8
    flips = [(fx, fy, fc) for fx in (0, 1) for fy in (0, 1) for fc in (0, 1)][1:]
    deps = _deps(deps)

    def body(x_ref, o_ref, send_sems, recv_sems):
        mx, my, mc = _coords()
        me = 4 * mx + 2 * my + mc

        def rows(dev):
            return o_ref.at[pl.ds(pl.multiple_of(dev * R, R), R), :]

        o_ref[pl.ds(pl.multiple_of(me * R, R), R), :] = x_ref[...]
        copies = []
        for t, (fx, fy, fc) in enumerate(flips):
            peer = (mx ^ fx, my ^ fy, mc ^ fc)
            copies.append(pltpu.make_async_remote_copy(
                src_ref=x_ref, dst_ref=rows(me), send_sem=send_sems.at[t], recv_sem=recv_sems.at[t],
                device_id=peer, device_id_type=MESH))
        for cp in copies:
            cp.start()
        for t, (fx, fy, fc) in enumerate(flips):
            peer_id = 4 * (mx ^ fx) + 2 * (my ^ fy) + (mc ^ fc)
            pltpu.make_async_remote_copy(
                src_ref=x_ref, dst_ref=rows(peer_id), send_sem=send_sems.at[t], recv_sem=recv_sems.at[t],
                device_id=(mx ^ fx, my ^ fy, mc ^ fc), device_id_type=MESH).wait_recv()
        for cp in copies:
            cp.wait_send()

    return pl.pallas_call(
        _blind_to(body, 1, len(deps)), name=name,
        in_specs=[pl.BlockSpec(memory_space=pltpu.VMEM)] + [ANY_SPEC] * len(deps),
        out_specs=pl.BlockSpec(memory_space=pltpu.VMEM),
        out_shape=jax.ShapeDtypeStruct((N_DEV * R, N), F32),
        scratch_shapes=[pltpu.SemaphoreType.DMA((7,)), pltpu.SemaphoreType.DMA((7,))],
        compiler_params=pltpu.CompilerParams(has_side_effects=True, vmem_limit_bytes=VMEM_LIMIT_BYTES),
    )(x, *deps)


def _half_rows(rows, half):
    return pl.ds(pl.multiple_of(half * (rows // 2), 8), rows // 2)


def _split_start(bufs, plan, n, name):
    nb = len(bufs)

    def body(*refs):
        send_sems, recv_sems, token = refs[nb], refs[nb + 1], refs[-1]
        for t, (src, dst, dev) in enumerate(plan(refs[:nb])):
            pltpu.make_async_remote_copy(src_ref=src, dst_ref=dst, send_sem=send_sems.at[t],
                                         recv_sem=recv_sems.at[t], device_id=dev, device_id_type=MESH).start()
        token[...] = jnp.zeros_like(token)

    out = pl.pallas_call(
        body, name=name,
        out_shape=(pltpu.SemaphoreType.DMA((n,)), pltpu.SemaphoreType.DMA((n,)),
                   *[pltpu.HBM(b.shape, b.dtype) for b in bufs], jax.ShapeDtypeStruct((8, 128), F32)),
        in_specs=[HBM_SPEC] * nb,
        out_specs=(SEM_SPEC, SEM_SPEC, *[HBM_SPEC] * nb, pl.BlockSpec(memory_space=pltpu.VMEM)),
        input_output_aliases={i: 2 + i for i in range(nb)},
        compiler_params=pltpu.CompilerParams(has_side_effects=DATAFLOW),
    )(*[pltpu.with_memory_space_constraint(b, pltpu.HBM) for b in bufs])
    return out[0], out[1], list(out[2:2 + nb]), out[-1]


def _split_wait(bufs, send_sems, recv_sems, plan, after, name):
    nb = len(bufs)
    after = _deps(after)

    def body(*refs):
        ss, rs = refs[nb], refs[nb + 1]
        for t, (src, dst, dev) in enumerate(plan(refs[:nb])):
            cp = pltpu.make_async_remote_copy(src_ref=src, dst_ref=dst, send_sem=ss.at[t], recv_sem=rs.at[t],
                                              device_id=dev, device_id_type=MESH)
            cp.wait_send()
            cp.wait_recv()

    out = pl.pallas_call(
        body, name=name,
        out_shape=tuple(pltpu.HBM(b.shape, b.dtype) for b in bufs),
        in_specs=[HBM_SPEC] * nb + [SEM_SPEC, SEM_SPEC] + [ANY_SPEC] * len(after),
        out_specs=tuple([HBM_SPEC] * nb),
        input_output_aliases={i: i for i in range(nb)},
        compiler_params=pltpu.CompilerParams(has_side_effects=DATAFLOW),
    )(*bufs, send_sems, recv_sems, *after)
    return list(out)


class _Exchange:
    def __init__(self, bufs, plan, n, name):
        self.plan, self.name = plan, name
        self.send_sems, self.recv_sems, self.bufs, self.token = _split_start(bufs, plan, n, name + "_start")

    def wait(self, after):
        return _split_wait(self.bufs, self.send_sems, self.recv_sems, self.plan, after, self.name + "_wait")


def _cast_place(w, chip_idx, name, deps=()):
    R, C = w.shape
    tr = _row_tile(R, 256) if R % 256 == 0 else R
    deps = _deps(deps)

    def body(k_ref, w_ref, o_ref):
        o_ref[...] = w_ref[...].astype(BF16)

    grid_spec = pltpu.PrefetchScalarGridSpec(
        num_scalar_prefetch=1, grid=(R // tr,),
        in_specs=[pl.BlockSpec((tr, C), lambda i, k: (i, 0))] + [ANY_SPEC] * len(deps),
        out_specs=pl.BlockSpec((None, tr, C), lambda i, k: (k[0], i, 0)),
    )
    return pl.pallas_call(
        _blind_to(body, 2, len(deps)), name=name, grid_spec=grid_spec,
        out_shape=jax.ShapeDtypeStruct((N_CHIPS, R, C), BF16),
        compiler_params=_params(("parallel",), streaming=True),
    )(chip_idx, w, *deps)


def _plan_gather(refs):
    mx, my, mc = _coords()
    me = 2 * mx + my
    plan = []
    for g in refs:
        mine = g.at[me, _half_rows(g.shape[1], mc), :]
        for fx, fy in XY_FLIPS:
            plan.append((mine, mine, (mx ^ fx, my ^ fy, mc)))
    return plan


def _plan_swap(refs):
    mx, my, mc = _coords()
    plan = []
    for g in refs:
        for fx, fy in XY_FLIPS:
            have = g.at[2 * (mx ^ fx) + (my ^ fy), _half_rows(g.shape[1], mc), :]
            plan.append((have, have, (mx, my, 1 - mc)))
    return plan


def _plan_other_halves(refs):
    n = len(refs) // 2
    mx, my, mc = _coords()
    return [(g.at[pl.ds(0, N_CHIPS), _half_rows(g.shape[1], 1 - mc), :], land, (mx, my, 1 - mc))
            for g, land in zip(refs[:n], refs[n:])]


def _plan_chunks(refs):
    n = len(refs) // 2
    mx, my, mc = _coords()
    plan = []
    for s, land in zip(refs[:n], refs[n:]):
        for t, (fx, fy) in enumerate(XY_FLIPS):
            plan.append((s.at[2 * (mx ^ fx) + (my ^ fy)], land.at[t], (mx ^ fx, my ^ fy, mc)))
    return plan


def _plan_share(refs):
    mx, my, mc = _coords()
    return [(full.at[mc], full.at[mc], (mx, my, 1 - mc)) for full in refs]


def _add_half(g, recv, core_idx, name):
    nk, R, C = g.shape
    rh = R // 2
    tr = _row_tile(rh, 128) if rh % 128 == 0 else rh
    nt = rh // tr

    def body(c_ref, g_ref, r_ref, o_ref):
        o_ref[...] = (g_ref[...] + r_ref[...]).astype(BF16)

    grid_spec = pltpu.PrefetchScalarGridSpec(
        num_scalar_prefetch=1, grid=(nk, nt),
        in_specs=[pl.BlockSpec((None, tr, C), lambda k, i, c: (k, c[0] * nt + i, 0)),
                  pl.BlockSpec((None, tr, C), lambda k, i, c: (k, i, 0))],
        out_specs=pl.BlockSpec((None, tr, C), lambda k, i, c: (k, i, 0)),
    )
    return pl.pallas_call(
        body, name=name, grid_spec=grid_spec, out_shape=jax.ShapeDtypeStruct((nk, rh, C), BF16),
        compiler_params=_params(("parallel", "parallel"), streaming=True),
    )(core_idx, g, recv)


def _sum_chips(s, land, chip_core, name):
    _, rh, C = s.shape
    tr = _row_tile(rh, 128) if rh % 128 == 0 else rh

    def body(p_ref, s_ref, l_ref, o_ref):
        me = p_ref[0]
        acc = None
        for j in range(N_CHIPS):
            t = jnp.maximum(jnp.bitwise_xor(me, j) - 1, 0)
            term = jnp.where(me == j, s_ref[...], l_ref[t]).astype(F32)
            acc = term if acc is None else acc + term
        o_ref[...] = acc

    grid_spec = pltpu.PrefetchScalarGridSpec(
        num_scalar_prefetch=1, grid=(rh // tr,),
        in_specs=[pl.BlockSpec((None, tr, C), lambda i, p: (p[0], i, 0)),
                  pl.BlockSpec((3, tr, C), lambda i, p: (0, i, 0))],
        out_specs=pl.BlockSpec((None, tr, C), lambda i, p: (p[1], i, 0)),
    )
    return pl.pallas_call(
        body, name=name, grid_spec=grid_spec, out_shape=jax.ShapeDtypeStruct((2, rh, C), F32),
        compiler_params=_params(("parallel",), streaming=True),
    )(chip_core, s, land)


def _rs_send_halves(grads, tag):
    lands = [lax.empty((g.shape[0], g.shape[1] // 2, g.shape[2]), g.dtype) for g in grads]
    return _Exchange(list(grads) + lands, _plan_other_halves, len(grads), f"rs_halves_{tag}")


def _rs_send_chunks(ex, after, core_idx, tag):
    bufs = ex.wait(after)
    n = len(bufs) // 2
    sums = [_add_half(g, r, core_idx, f"rs_add_{tag}_{i}") for i, (g, r) in enumerate(zip(bufs[:n], bufs[n:]))]
    lands = [lax.empty((3,) + s.shape[1:], s.dtype) for s in sums]
    return _Exchange(sums + lands, _plan_chunks, 3 * n, f"rs_chunks_{tag}")


def _rs_send_share(ex, after, chip_core, tag):
    bufs = ex.wait(after)
    n = len(bufs) // 2
    fulls = [_sum_chips(s, l, chip_core, f"rs_sum_{tag}_{i}") for i, (s, l) in enumerate(zip(bufs[:n], bufs[n:]))]
    return _Exchange(fulls, _plan_share, n, f"rs_share_{tag}")


def _rs_finish(ex, after):
    return [b.reshape(2 * b.shape[1], b.shape[2]) for b in ex.wait(after)]


def _ffn_backward(dh_out, df, saved, gain, sc, wg, wu, wd, core_idx, tag, prev=None, last=False):
    h, n, ga, up, act, _ = saved
    (dwd,) = _wgrad_chunk_lhs([act], df, f"{tag}_dwd")
    ex_d = _rs_send_halves([dwd], f"{tag}_d")
    dga, dup = _ffn_dact(df, wd, ga, up, f"{tag}_dact", deps=[ex_d.token])
    ex_d = _rs_send_chunks(ex_d, [dga], core_idx, f"{tag}_d")
    dwg, dwu = _wgrad_chunk_lhs([dga, dup], n, f"{tag}_dwgu", deps=[ex_d.token])
    ex_gu = _rs_send_halves([dwg, dwu], f"{tag}_gu")
    dn = _mm_reduce([dga, dup], [wg, wu], True, True, f"{tag}_dn", deps=[ex_gu.token])
    if not last:
        ex_gu = _rs_send_chunks(ex_gu, [dn], core_idx, f"{tag}_gu")
    outs = _norm_mod_bwd(dn, h, gain, sc, dh_out, f"{tag}_norm_bwd", prev=prev, deps=[ex_gu.token])
    return outs, (ex_d, ex_gu)


def _pad_cols(v, n):
    return jnp.pad(v, ((0, 0), (0, n - v.shape[1])))


def _mixer_forward(h1, n2, gt2, win, wout, conv_w, conv_dw_b, conv_ln_g, conv_ln_b, attn_out_g, conv_out_g,
                   next_norm=None):
    S, D = h1.shape
    cos, sin = _rope_tables(S)
    proj = _mm_cols_rope(n2, win, cos, sin, "mix_in")
    attn, lse = _attn_fwd(proj, "attn_fwd")
    u1, y = _mixer_merge(proj, attn, conv_w, conv_dw_b, conv_ln_g, conv_ln_b, attn_out_g, conv_out_g, "mix_merge")
    h2, mo, *n_next = _mm_residual(y[None], wout.reshape(1, D, D), h1, gt2, 1.0, "mix_out", next_norm=next_norm)
    return (h2, *n_next), (h1, n2, proj, cos, sin, attn, lse, u1, y, mo)


def _mixer_backward(dh2, dmo, saved, mix_norm_g, sc2, win, wout, conv_w, conv_ln_g, conv_ln_b, attn_out_g,
                    conv_out_g, core_idx, prev=None):
    h1, n2, proj, cos, sin, attn, lse, u1, y, _ = saved
    S, D = h1.shape
    (dwout,) = _wgrad_chunk_lhs([y[None]], dmo, "mix_dwout")
    dattn, delta, du1, d_attn_g, d_gco, d_lng, d_lnb, d_cb = _mix_out_bwd(
        dmo, wout.reshape(D, D), attn, u1, attn_out_g, conv_ln_g, conv_ln_b, conv_out_g, "mix_out_bwd")
    dq, dk, dv = _attn_bwd(proj, dattn, lse, delta, "attn_bwd")
    dproj, d_cw = _dproj(du1, proj, conv_w, dq, dk, dv, cos, sin, "mix_dproj")
    (dwin,) = _wgrad_chunk_rhs(n2, [dproj], N_CHIPS, False, "mix_dwin")
    ex = _rs_send_halves([dwin, dwout.reshape(N_CHIPS, D // N_CHIPS, D)], "mix")
    dn2 = _mm_reduce([dproj], [win], False, False, "mix_dn", deps=[ex.token])
    ex = _rs_send_chunks(ex, [dn2], core_idx, "mix")
    outs = _norm_mod_bwd(dn2, h1, mix_norm_g, sc2, dh2, "mix_norm_bwd", prev=prev, deps=[ex.token])
    return outs, ex, (d_cb, d_lng, d_lnb, d_attn_g, d_gco, d_cw)


def kernel(x, c, w_ada, b_ada, ffn1_norm_g, ffn1_w_gate, ffn1_w_up, ffn1_w_down, mix_norm_g, w_in, conv_dw_w, conv_dw_b, conv_ln_g, conv_ln_b, attn_out_g, conv_out_g, w_out, ffn2_norm_g, ffn2_w_gate, ffn2_w_up, ffn2_w_down, final_norm_g, loss_target, m_w_ada, m_b_ada, m_ffn1_norm_g, m_ffn1_w_gate, m_ffn1_w_up, m_ffn1_w_down, m_mix_norm_g, m_w_in, m_conv_dw_w, m_conv_dw_b, m_conv_ln_g, m_conv_ln_b, m_attn_out_g, m_conv_out_g, m_w_out, m_ffn2_norm_g, m_ffn2_w_gate, m_ffn2_w_up, m_ffn2_w_down, m_final_norm_g, v_w_ada, v_b_ada, v_ffn1_norm_g, v_ffn1_w_gate, v_ffn1_w_up, v_ffn1_w_down, v_mix_norm_g, v_w_in, v_conv_dw_w, v_conv_dw_b, v_conv_ln_g, v_conv_ln_b, v_attn_out_g, v_conv_out_g, v_w_out, v_ffn2_norm_g, v_ffn2_w_gate, v_ffn2_w_up, v_ffn2_w_down, v_final_norm_g):
    S, D = x.shape[1], x.shape[2]
    mx, my, mc = _coords()
    chip = 2 * mx + my
    dev = 4 * mx + 2 * my + mc
    chip_idx = chip.astype(jnp.int32).reshape(1)
    core_idx = mc.astype(jnp.int32).reshape(1)
    chip_core = jnp.stack([chip, mc]).astype(jnp.int32)
    h0 = x[0]
    target = loss_target[0]

    ncw = CONV_KERNEL * 128
    n0 = -(-(D + ncw) // 1024) * 1024
    pk0 = _pad_cols(jnp.concatenate([c.reshape(1, D), conv_dw_w.reshape(1, ncw)], axis=1), n0)
    g0 = _all_gather8(pk0.reshape(8, n0 // 8), "gather_c").reshape(N_DEV, n0)
    c_all = g0[:, :D]
    conv_w = jnp.concatenate([g0[2 * kc, D:D + ncw].reshape(CONV_KERNEL, 128) for kc in range(N_CHIPS)], axis=1)
    conv_w = jnp.pad(conv_w, ((0, HALO - CONV_KERNEL), (0, 0)))
    nmod = w_ada.shape[2]
    b_shard = lax.dynamic_slice(b_ada, (0, chip * nmod), (1, nmod))
    mod_part = _ada_fwd(c_all, w_ada[0], b_shard, "ada_fwd")
    g1 = _all_gather8(mod_part, "gather_mod")
    mod_all = jnp.concatenate([g1[16 * kc:16 * kc + 8] for kc in range(N_CHIPS)], axis=1)
    mod = lax.dynamic_slice(mod_all, (dev, 0), (1, 9 * D))
    sh1, sc1, gt1, sh2, sc2, gt2, sh3, sc3, gt3 = [mod[:, i * D:(i + 1) * D] for i in range(9)]

    def gather_start(ws, tag, dep):
        slots = [_cast_place(w, chip_idx, f"cast_{tag}_{i}", deps=[dep]) for i, w in enumerate(ws)]
        return _Exchange(slots, _plan_gather, 3 * len(ws), f"gather_{tag}")

    def swap_start(ex, after, tag):
        return _Exchange(ex.wait(after), _plan_swap, 3 * len(ex.bufs), f"swap_{tag}")

    ex_gu1 = gather_start([ffn1_w_gate[0].T, ffn1_w_up[0].T], "ffn1_gu", g1)
    n1 = _norm_mod(h0, ffn1_norm_g, sc1, sh1, "ffn1_norm", deps=[ex_gu1.token])
    ex_d1 = gather_start([ffn1_w_down[0]], "ffn1_d", n1)
    ex_wm = gather_start([w_in[0], w_out[0]], "mix", ex_d1.token)
    ex_w2 = gather_start([ffn2_w_gate[0].T, ffn2_w_up[0].T, ffn2_w_down[0]], "ffn2", ex_wm.token)

    wg1, wu1 = swap_start(ex_gu1, [ex_w2.token], "ffn1_gu").wait([])
    ga1, up1, act1 = _ffn_gate_up(n1, wg1, wu1, "ffn1_gate_up")
    (wd1,) = swap_start(ex_d1, [act1], "ffn1_d").wait([])
    ex_wm = swap_start(ex_wm, [wd1], "mix")
    h1, f1, n2 = _mm_residual(act1, wd1, h0, gt1, 0.5, "ffn1_down", next_norm=(mix_norm_g, sc2, sh2))
    saved1 = (h0, n1, ga1, up1, act1, f1)
    win, wout = ex_wm.wait([h1])
    ex_w2 = swap_start(ex_w2, [h1], "ffn2")
    (h2, n3), saved2 = _mixer_forward(h1, n2, gt2, win, wout, conv_w, conv_dw_b, conv_ln_g, conv_ln_b,
                                      attn_out_g, conv_out_g, next_norm=(ffn2_norm_g, sc3, sh3))
    wg2, wu2, wd2 = ex_w2.wait([h2])
    fga3, fup3, act3 = _ffn_gate_up(n3, wg2, wu2, "ffn2_gate_up")
    loss_part, dh3, d_final_g, df3, d_gt3 = _mm_residual_loss(
        act3, wd2, h2, gt3, 0.5, final_norm_g.reshape(1, D), target, "ffn2_down_loss")
    saved3 = (h2, n3, fga3, fup3, act3, None)

    (dh2, d_sh3, d_sc3, d_gain3, dmo, d_gt2), (ex_d2, ex_gu2) = _ffn_backward(
        dh3, df3, saved3, ffn2_norm_g, sc3, wg2, wu2, wd2, core_idx, "ffn2", prev=(saved2[-1], gt2, 1.0))
    (dh1, d_sh2, d_sc2, d_gain2, df1, d_gt1), ex_mix, small_mix = _mixer_backward(
        dh2, dmo, saved2, mix_norm_g, sc2, win, wout, conv_w, conv_ln_g, conv_ln_b, attn_out_g, conv_out_g, core_idx,
        prev=(f1, gt1, 0.5))
    d_cb, d_lng, d_lnb, d_attn_g, d_gco, d_cw = small_mix
    (dh0, d_sh1, d_sc1, d_gain1), (ex_d1, ex_gu1) = _ffn_backward(
        dh1, df1, saved1, ffn1_norm_g, sc1, wg1, wu1, wd1, core_idx, "ffn1", last=True)

    dmod = jnp.concatenate([d_sh1, d_sc1, d_gt1, d_sh2, d_sc2, d_gt2, d_sh3, d_sc3, d_gt3], axis=1)
    small = [d_gain1, d_gain2, d_gain3, d_final_g, d_cb, d_lng, d_lnb, d_attn_g, d_gco,
             d_cw[:CONV_KERNEL].reshape(1, CONV_KERNEL * CONV_WIDTH), loss_part]
    pk1 = jnp.concatenate([dmod] + small, axis=1)
    n1_ = -(-pk1.shape[1] // 1024) * 1024
    gathered = _all_gather8(_pad_cols(pk1, n1_).reshape(8, n1_ // 8), "gather_small").reshape(N_DEV, n1_)
    ex_gu1 = _rs_send_chunks(ex_gu1, [gathered], core_idx, "ffn1_gu")
    tot = _sum_rows(gathered, "sum_small")
    off = [0]

    def take(nel):
        out = tot[:, off[0]:off[0] + nel]
        off[0] += nel
        return out

    g_b_ada = take(9 * D)
    g_ffn1_norm, g_mix_norm, g_ffn2_norm, g_final = take(D), take(D), take(D), take(D)
    g_cb, g_lng, g_lnb, g_attn_g, g_gco = take(512), take(512), take(512), take(512), take(512)
    g_cw_full = take(CONV_KERNEL * CONV_WIDTH).reshape(CONV_KERNEL, CONV_WIDTH)
    loss = take(128)[0, 0]
    g_cw = lax.dynamic_slice(g_cw_full, (0, chip * 128), (CONV_KERNEL, 128))

    dmod_shard = lax.dynamic_slice(gathered[:, :9 * D], (0, chip * nmod), (N_DEV, nmod))
    dmod16 = jnp.pad(dmod_shard, ((0, N_DEV), (0, 0)))
    c_t16 = jnp.pad(c_all.T, ((0, 0), (0, N_DEV)))
    g_w_ada = _ada_wgrad(c_t16, dmod16, "ada_wgrad")

    names = ["w_ada", "b_ada", "ffn1_norm_g", "ffn1_w_gate", "ffn1_w_up", "ffn1_w_down", "mix_norm_g", "w_in",
             "conv_dw_w", "conv_dw_b", "conv_ln_g", "conv_ln_b", "attn_out_g", "conv_out_g", "w_out", "ffn2_norm_g",
             "ffn2_w_gate", "ffn2_w_up", "ffn2_w_down", "final_norm_g"]
    weights = dict(zip(names, [w_ada, b_ada, ffn1_norm_g, ffn1_w_gate, ffn1_w_up, ffn1_w_down, mix_norm_g, w_in,
                               conv_dw_w, conv_dw_b, conv_ln_g, conv_ln_b, attn_out_g, conv_out_g, w_out,
                               ffn2_norm_g, ffn2_w_gate, ffn2_w_up, ffn2_w_down, final_norm_g]))
    ms = dict(zip(names, [m_w_ada, m_b_ada, m_ffn1_norm_g, m_ffn1_w_gate, m_ffn1_w_up, m_ffn1_w_down, m_mix_norm_g,
                          m_w_in, m_conv_dw_w, m_conv_dw_b, m_conv_ln_g, m_conv_ln_b, m_attn_out_g, m_conv_out_g,
                          m_w_out, m_ffn2_norm_g, m_ffn2_w_gate, m_ffn2_w_up, m_ffn2_w_down, m_final_norm_g]))
    vs = dict(zip(names, [v_w_ada, v_b_ada, v_ffn1_norm_g, v_ffn1_w_gate, v_ffn1_w_up, v_ffn1_w_down, v_mix_norm_g,
                          v_w_in, v_conv_dw_w, v_conv_dw_b, v_conv_ln_g, v_conv_ln_b, v_attn_out_g, v_conv_out_g,
                          v_w_out, v_ffn2_norm_g, v_ffn2_w_gate, v_ffn2_w_up, v_ffn2_w_down, v_final_norm_g]))
    grads, deltas, new_ms, new_vs = {}, {}, {}, {}

    def adamw_big(nm, g2d, deps=(), transposed=False):
        shape = weights[nm].shape
        two_d = (shape[-2], shape[-1])

        def view(t):
            return t.reshape(two_d).T if transposed else t.reshape(two_d)

        d_, m_, v_, g_ = _adamw(view(weights[nm]), g2d, view(ms[nm]), view(vs[nm]), f"adamw_{nm}", deps=deps)
        grads[nm], deltas[nm], new_ms[nm], new_vs[nm] = (
            (t.T if transposed else t).reshape(shape) for t in (g_, d_, m_, v_))
        return d_

    d_ada = adamw_big("w_ada", g_w_ada, deps=[ex_gu1.token])
    small_grads = {"b_ada": g_b_ada, "ffn1_norm_g": g_ffn1_norm, "mix_norm_g": g_mix_norm, "conv_dw_w": g_cw,
                   "conv_dw_b": g_cb, "conv_ln_g": g_lng, "conv_ln_b": g_lnb, "attn_out_g": g_attn_g,
                   "conv_out_g": g_gco, "ffn2_norm_g": g_ffn2_norm, "final_norm_g": g_final}
    small_names = [nm for nm in names if nm in small_grads]

    def pack_small(arrs):
        flat = jnp.concatenate([arrs[nm].reshape(1, -1) for nm in small_names], axis=1)
        npad = -(-flat.shape[1] // 1024) * 1024
        return _pad_cols(flat, npad).reshape(8, npad // 8)

    d_s, m_s, v_s, _ = _adamw(pack_small(weights), pack_small(small_grads), pack_small(ms), pack_small(vs),
                           "adamw_small")
    pos = 0
    for nm in small_names:
        shape, nel = weights[nm].shape, weights[nm].size
        grads[nm] = small_grads[nm].reshape(shape)
        deltas[nm], new_ms[nm], new_vs[nm] = (t.reshape(1, -1)[:, pos:pos + nel].reshape(shape)
                                              for t in (d_s, m_s, v_s))
        pos += nel

    ex_d2 = _rs_send_share(ex_d2, [d_ada, d_s], chip_core, "ffn2_d")
    ex_gu2 = _rs_send_share(ex_gu2, [ex_d2.token], chip_core, "ffn2_gu")
    ex_mix = _rs_send_share(ex_mix, [ex_gu2.token], chip_core, "mix")
    ex_d1 = _rs_send_share(ex_d1, [ex_mix.token], chip_core, "ffn1_d")
    (g_wd2,) = _rs_finish(ex_d2, [ex_d1.token])
    last = [adamw_big("ffn2_w_down", g_wd2)]
    g_wg2, g_wu2 = _rs_finish(ex_gu2, last)
    last = [adamw_big("ffn2_w_gate", g_wg2, transposed=True), adamw_big("ffn2_w_up", g_wu2, transposed=True)]
    g_win, g_wout = _rs_finish(ex_mix, last)
    last = [adamw_big("w_in", g_win), adamw_big("w_out", g_wout)]
    (g_wd1,) = _rs_finish(ex_d1, last)
    last = [adamw_big("ffn1_w_down", g_wd1)]
    ex_gu1 = _rs_send_share(ex_gu1, last, chip_core, "ffn1_gu")
    g_wg1, g_wu1 = _rs_finish(ex_gu1, [])
    adamw_big("ffn1_w_gate", g_wg1, transposed=True)
    adamw_big("ffn1_w_up", g_wu1, transposed=True)

    return (loss, dh0[None], *[grads[nm] for nm in names], *[deltas[nm] for nm in names],
            *[new_ms[nm] for nm in names], *[new_vs[nm] for nm in names])
```

```python
import jax
import jax.numpy as jnp
import numpy as np
from jax import lax
from jax.experimental import pallas as pl
from jax.experimental.pallas import tpu as pltpu

F32 = jnp.float32
BF16 = jnp.bfloat16
MESH = pl.DeviceIdType.MESH

RMS_EPS = 1e-6
LN_EPS = 1e-5
HEAD_DIM = 64
ATTN_WIDTH = 512
CONV_WIDTH = 512
ATTN_BLOCK = 128
DILATIONS = (1, 4, 16)
SUPER_ROWS = ATTN_BLOCK * 16
ROPE_THETA = 10000.0
CONV_KERNEL = 31
HALO = 32
N_CHIPS = 4
N_DEV = 8
ADAM_LR, ADAM_B1, ADAM_B2, ADAM_EPS, ADAM_WD, ADAM_STEP = 0.001, 0.9, 0.999, 1e-08, 0.01, 10
VMEM_LIMIT_BYTES = 48 * 1024 * 1024
VMEM_LIMIT_STREAMING = 62 * 1024 * 1024
NEG = -1e30

NT = (((1,), (1,)), ((), ()))
TN = (((0,), (0,)), ((), ()))

ANY_SPEC = pl.BlockSpec(memory_space=pl.ANY)
HBM_SPEC = pl.BlockSpec(memory_space=pltpu.HBM)
SEM_SPEC = pl.BlockSpec(memory_space=pltpu.SEMAPHORE)
DATAFLOW = pltpu.SideEffectType.DATAFLOW_SIDE_EFFECTING
XY_FLIPS = ((0, 1), (1, 0), (1, 1))


def _params(sem=None, streaming=False):
    return pltpu.CompilerParams(dimension_semantics=sem,
                                vmem_limit_bytes=VMEM_LIMIT_STREAMING if streaming else VMEM_LIMIT_BYTES)


def _row_tile(rows, want):
    t = min(rows, want)
    assert rows % t == 0
    return t


def _sigmoid(x):
    return 1.0 / (1.0 + jnp.exp(-x))


def _deps(deps):
    return [d for d in deps if d is not None]


def _blind_to(body, n_in, n_dep):
    def wrapped(*refs):
        return body(*refs[:n_in], *refs[n_in + n_dep:])
    return wrapped


def _vec_spec(d, ngrid):
    if ngrid == 1:
        return pl.BlockSpec((1, d), lambda i: (0, 0))
    return pl.BlockSpec((1, d), lambda i, j: (0, 0))


def _norm_mod(h, gain, sc, sh, name, deps=()):
    S, D = h.shape
    tr = _row_tile(S, 512)
    deps = _deps(deps)

    def body(h_ref, g_ref, sc_ref, sh_ref, n_ref):
        x = h_ref[...]
        r = lax.rsqrt(jnp.mean(x * x, axis=-1, keepdims=True) + RMS_EPS)
        y = (x * r) * g_ref[...]
        n_ref[...] = (y * (1.0 + sc_ref[...]) + sh_ref[...]).astype(BF16)

    row = pl.BlockSpec((tr, D), lambda i: (i, 0))
    return pl.pallas_call(
        _blind_to(body, 4, len(deps)), name=name, grid=(S // tr,),
        in_specs=[row, _vec_spec(D, 1), _vec_spec(D, 1), _vec_spec(D, 1)] + [ANY_SPEC] * len(deps),
        out_specs=row, out_shape=jax.ShapeDtypeStruct((S, D), BF16),
        compiler_params=_params(("parallel",)),
    )(h, gain, sc, sh, *deps)


def _norm_mod_bwd(dn, h_in, gain, sc, dh_out, name, prev=None, deps=()):
    S, D = h_in.shape
    tr = _row_tile(S, 512)
    deps = _deps(deps)
    n_in = 5 if prev is None else 7

    def body(*refs):
        dn_ref, h_ref, g_ref, sc_ref, dho_ref = refs[:5]
        dh_ref, dsh_ref, dsc_ref, dg_ref = refs[n_in:n_in + 4]

        @pl.when(pl.program_id(0) == 0)
        def _():
            for ref in refs[n_in + 1:n_in + 4] + refs[n_in + 5:]:
                ref[...] = jnp.zeros_like(ref)

        x = h_ref[...]
        dn_ = dn_ref[...]
        g = g_ref[...]
        one_sc = 1.0 + sc_ref[...]
        r = lax.rsqrt(jnp.mean(x * x, axis=-1, keepdims=True) + RMS_EPS)
        xh = x * r
        dsh_ref[...] += jnp.sum(dn_, axis=0, keepdims=True)
        dsc_ref[...] += jnp.sum(dn_ * (xh * g), axis=0, keepdims=True)
        dg_ref[...] += jnp.sum(dn_ * one_sc * xh, axis=0, keepdims=True)
        dxh = dn_ * (g * one_sc)
        dh = dho_ref[...] + r * (dxh - xh * jnp.mean(dxh * xh, axis=-1, keepdims=True))
        dh_ref[...] = dh
        if prev is not None:
            _gate_back(dh, refs[5], refs[6], prev[2], refs[n_in + 4], refs[n_in + 5])

    row = pl.BlockSpec((tr, D), lambda i: (i, 0))
    vec = _vec_spec(D, 1)
    extra_in = [] if prev is None else [row, vec]
    extra_out = [] if prev is None else [row, vec]
    extra_shape = [] if prev is None else [jax.ShapeDtypeStruct((S, D), BF16), jax.ShapeDtypeStruct((1, D), F32)]
    return pl.pallas_call(
        _blind_to(body, n_in, len(deps)), name=name, grid=(S // tr,),
        in_specs=[row, row, vec, vec, row] + extra_in + [ANY_SPEC] * len(deps),
        out_specs=[row, vec, vec, vec] + extra_out,
        out_shape=[jax.ShapeDtypeStruct((S, D), F32)] + [jax.ShapeDtypeStruct((1, D), F32)] * 3 + extra_shape,
        compiler_params=_params(("arbitrary",)),
    )(dn, h_in, gain, sc, dh_out, *([] if prev is None else prev[:2]), *deps)


def _gate_back(dh, f_ref, gate_ref, coef, df_ref, dgate_ref):
    df_ref[...] = ((coef * gate_ref[...]) * dh).astype(BF16)
    dgate_ref[...] += jnp.sum(coef * dh * f_ref[...].astype(F32), axis=0, keepdims=True)


def _mm_residual_loss(lhs, w, h_in, gate, coef, gain, target, name):
    nk, S, kc = lhs.shape
    D = w.shape[2]
    tm = _row_tile(S, 512)

    def body(l_ref, w_ref, h_ref, gate_ref, g_ref, t_ref, loss_ref, dh_ref, dg_ref, df_ref, dgate_ref):
        @pl.when(pl.program_id(0) == 0)
        def _():
            loss_ref[...] = jnp.zeros_like(loss_ref)
            dg_ref[...] = jnp.zeros_like(dg_ref)
            dgate_ref[...] = jnp.zeros_like(dgate_ref)

        f = _chunk_dots(l_ref, w_ref, nk)
        cg = coef * gate_ref[...]
        x = h_ref[...] + cg * f
        g = g_ref[...]
        r = lax.rsqrt(jnp.mean(x * x, axis=-1, keepdims=True) + RMS_EPS)
        xh = x * r
        err = xh * g - t_ref[...]
        part = 0.5 * jnp.sum(jnp.mean(err * err, axis=-1, keepdims=True), axis=0, keepdims=True)
        loss_ref[...] += jnp.broadcast_to(part, loss_ref.shape)
        dy = err * (1.0 / D)
        dg_ref[...] += jnp.sum(dy * xh, axis=0, keepdims=True)
        dxh = dy * g
        dh = r * (dxh - xh * jnp.mean(dxh * xh, axis=-1, keepdims=True))
        dh_ref[...] = dh
        df_ref[...] = (cg * dh).astype(BF16)
        dgate_ref[...] += jnp.sum(coef * dh * f, axis=0, keepdims=True)

    row = pl.BlockSpec((tm, D), lambda m: (m, 0))
    vec = _vec_spec(D, 1)
    return pl.pallas_call(
        body, name=name, grid=(S // tm,),
        in_specs=[pl.BlockSpec((nk, tm, kc), lambda m: (0, m, 0)),
                  pl.BlockSpec((nk, kc, D), lambda m: (0, 0, 0)), row, vec, vec, row],
        out_specs=[pl.BlockSpec((1, 128), lambda m: (0, 0)), row, vec, row, vec],
        out_shape=[jax.ShapeDtypeStruct((1, 128), F32), jax.ShapeDtypeStruct((S, D), F32),
                   jax.ShapeDtypeStruct((1, D), F32), jax.ShapeDtypeStruct((S, D), BF16),
                   jax.ShapeDtypeStruct((1, D), F32)],
        compiler_params=_params(("arbitrary",)),
    )(lhs, w, h_in, gate, gain, target)


def _ffn_gate_up(n, wg_t, wu_t, name):
    S, D = n.shape
    nk, w, _ = wg_t.shape
    tm = _row_tile(S, 512)

    def body(n_ref, wg_ref, wu_ref, dga_ref, dup_ref, act_ref):
        x = n_ref[...]
        ga = lax.dot_general(x, wg_ref[...], NT, preferred_element_type=F32)
        up = lax.dot_general(x, wu_ref[...], NT, preferred_element_type=F32)
        sig = _sigmoid(ga)
        silu = ga * sig
        dga_ref[...] = (up * (sig * (1.0 + ga * (1.0 - sig)))).astype(BF16)
        dup_ref[...] = silu.astype(BF16)
        act_ref[...] = (silu * up).astype(BF16)

    wspec = pl.BlockSpec((None, w, D), lambda k, m: (k, 0, 0))
    ospec = pl.BlockSpec((None, tm, w), lambda k, m: (k, m, 0))
    out = jax.ShapeDtypeStruct((nk, S, w), BF16)
    return pl.pallas_call(
        body, name=name, grid=(nk, S // tm),
        in_specs=[pl.BlockSpec((tm, D), lambda k, m: (m, 0)), wspec, wspec],
        out_specs=[ospec, ospec, ospec], out_shape=[out, out, out],
        compiler_params=_params(("parallel", "parallel")),
    )(n, wg_t, wu_t)


def _chunk_dots(l_ref, w_ref, nk):
    f = jnp.dot(l_ref[0], w_ref[0], preferred_element_type=F32)
    for k in range(1, nk):
        f = f + jnp.dot(l_ref[k], w_ref[k], preferred_element_type=F32)
    return f


def _mm_residual(lhs, w, h_in, gvec, coef, name, next_norm=None):
    nk, S, kc = lhs.shape
    D = w.shape[2]
    tm = _row_tile(S, 512)
    n_in = 4 if next_norm is None else 7

    def body(*refs):
        l_ref, w_ref, h_ref, g_ref = refs[:4]
        ho_ref, f_ref = refs[n_in:n_in + 2]
        f = _chunk_dots(l_ref, w_ref, nk)
        f_ref[...] = f.astype(BF16)
        x = h_ref[...] + (coef * g_ref[...]) * f
        ho_ref[...] = x
        if next_norm is not None:
            ng_ref, sc_ref, sh_ref = refs[4:7]
            r = lax.rsqrt(jnp.mean(x * x, axis=-1, keepdims=True) + RMS_EPS)
            y = (x * r) * ng_ref[...]
            refs[n_in + 2][...] = (y * (1.0 + sc_ref[...]) + sh_ref[...]).astype(BF16)

    row = pl.BlockSpec((tm, D), lambda m: (m, 0))
    vec = _vec_spec(D, 1)
    with_n = next_norm is not None
    return pl.pallas_call(
        body, name=name, grid=(S // tm,),
        in_specs=[pl.BlockSpec((nk, tm, kc), lambda m: (0, m, 0)),
                  pl.BlockSpec((nk, kc, D), lambda m: (0, 0, 0)), row, vec] + [vec] * (3 * with_n),
        out_specs=[row, row] + [row] * with_n,
        out_shape=[jax.ShapeDtypeStruct((S, D), F32), jax.ShapeDtypeStruct((S, D), BF16)]
        + [jax.ShapeDtypeStruct((S, D), BF16)] * with_n,
        compiler_params=_params(("parallel",)),
    )(lhs, w, h_in, gvec, *(next_norm or ()))


def _mm_cols_rope(n, w, cos, sin, name):
    S, D = n.shape
    nk, _, wd = w.shape
    assert wd % 128 == 0
    tm = _row_tile(S, 512)
    q_groups = ATTN_WIDTH // 128
    scale = HEAD_DIM ** -0.5

    def body(n_ref, w_ref, cos_ref, sin_ref, o_ref):
        x = jnp.dot(n_ref[...], w_ref[...], preferred_element_type=F32)
        first = pl.program_id(0) * (wd // 128)
        c, s = cos_ref[...], sin_ref[...]
        for j in range(wd // 128):
            lanes = slice(j * 128, (j + 1) * 128)
            group = first + j
            xj = x[:, lanes]
            rot = _rotate(xj, c, s) * jnp.where(group < q_groups, scale, 1.0)
            o_ref[:, lanes] = jnp.where(group < 2 * q_groups, rot, xj)

    tab = pl.BlockSpec((tm, 128), lambda k, m: (m, 0))
    return pl.pallas_call(
        body, name=name, grid=(nk, S // tm),
        in_specs=[pl.BlockSpec((tm, D), lambda k, m: (m, 0)), pl.BlockSpec((None, D, wd), lambda k, m: (k, 0, 0)),
                  tab, tab],
        out_specs=pl.BlockSpec((tm, wd), lambda k, m: (m, k)),
        out_shape=jax.ShapeDtypeStruct((S, nk * wd), F32),
        compiler_params=_params(("parallel", "parallel")),
    )(n, w, cos, sin)


def _ffn_dact(df, wd, fga, fup, name, deps=()):
    S, D = df.shape
    nk, w, _ = wd.shape
    tm = _row_tile(S, 512)
    deps = _deps(deps)

    def body(df_ref, wd_ref, fga_ref, fup_ref, dga_ref, dup_ref):
        dact = lax.dot_general(df_ref[...], wd_ref[...], NT, preferred_element_type=F32)
        dga_ref[...] = (dact * fga_ref[...].astype(F32)).astype(BF16)
        dup_ref[...] = (dact * fup_ref[...].astype(F32)).astype(BF16)

    cspec = pl.BlockSpec((None, tm, w), lambda k, m: (k, m, 0))
    out = jax.ShapeDtypeStruct((nk, S, w), BF16)
    return pl.pallas_call(
        _blind_to(body, 4, len(deps)), name=name, grid=(nk, S // tm),
        in_specs=[pl.BlockSpec((tm, D), lambda k, m: (m, 0)), pl.BlockSpec((None, w, D), lambda k, m: (k, 0, 0)),
                  cspec, cspec] + [ANY_SPEC] * len(deps),
        out_specs=[cspec, cspec], out_shape=[out, out],
        compiler_params=_params(("parallel", "parallel")),
    )(df, wd, fga, fup, *deps)


def _mm_reduce(lhs_list, w_list, chunked3d, w_is_kd, name, deps=()):
    nk = w_list[0].shape[0]
    kc, D = w_list[0].shape[1:] if w_is_kd else w_list[0].shape[:0:-1]
    S = lhs_list[0].shape[1] if chunked3d else lhs_list[0].shape[0]
    tm = _row_tile(S, 512)
    npair = len(lhs_list)
    deps = _deps(deps)

    def body(*refs):
        l_refs, w_refs = refs[:npair], refs[npair:2 * npair]
        o_ref, acc_ref = refs[2 * npair], refs[2 * npair + 1]
        k = pl.program_id(1)

        @pl.when(k == 0)
        def _():
            acc_ref[...] = jnp.zeros_like(acc_ref)

        for l_ref, w_ref in zip(l_refs, w_refs):
            if w_is_kd:
                acc_ref[...] += jnp.dot(l_ref[...], w_ref[...], preferred_element_type=F32)
            else:
                acc_ref[...] += lax.dot_general(l_ref[...], w_ref[...], NT, preferred_element_type=F32)

        @pl.when(k == nk - 1)
        def _():
            o_ref[...] = acc_ref[...]

    if chunked3d:
        lspec = pl.BlockSpec((None, tm, kc), lambda m, k: (k, m, 0))
    else:
        lspec = pl.BlockSpec((tm, kc), lambda m, k: (m, k))
    wspec = pl.BlockSpec((None,) + tuple(w_list[0].shape[1:]), lambda m, k: (k, 0, 0))
    return pl.pallas_call(
        _blind_to(body, 2 * npair, len(deps)), name=name, grid=(S // tm, nk),
        in_specs=[lspec] * npair + [wspec] * npair + [ANY_SPEC] * len(deps),
        out_specs=pl.BlockSpec((tm, D), lambda m, k: (m, 0)),
        out_shape=jax.ShapeDtypeStruct((S, D), F32),
        scratch_shapes=[pltpu.VMEM((tm, D), F32)],
        compiler_params=_params(("parallel", "arbitrary")),
    )(*lhs_list, *w_list, *deps)


def _wgrad_chunk_lhs(lhs_list, rhs, name, deps=()):
    nk, S, w = lhs_list[0].shape
    D = rhs.shape[1]
    ts = _row_tile(S, 1024)
    ns = S // ts
    nl = len(lhs_list)
    deps = _deps(deps)

    def body(*refs):
        l_refs, r_ref = refs[:nl], refs[nl]
        o_refs, acc_refs = refs[nl + 1:2 * nl + 1], refs[2 * nl + 1:]
        s = pl.program_id(1)

        @pl.when(s == 0)
        def _():
            for acc_ref in acc_refs:
                acc_ref[...] = jnp.zeros_like(acc_ref)

        x = r_ref[...]
        for l_ref, acc_ref in zip(l_refs, acc_refs):
            acc_ref[...] += lax.dot_general(l_ref[...], x, TN, preferred_element_type=F32)

        @pl.when(s == ns - 1)
        def _():
            for o_ref, acc_ref in zip(o_refs, acc_refs):
                o_ref[...] = acc_ref[...]

    return pl.pallas_call(
        _blind_to(body, nl + 1, len(deps)), name=name, grid=(nk, ns),
        in_specs=[pl.BlockSpec((None, ts, w), lambda k, s: (k, s, 0))] * nl
        + [pl.BlockSpec((ts, D), lambda k, s: (s, 0))] + [ANY_SPEC] * len(deps),
        out_specs=[pl.BlockSpec((None, w, D), lambda k, s: (k, 0, 0))] * nl,
        out_shape=[jax.ShapeDtypeStruct((nk, w, D), F32)] * nl,
        scratch_shapes=[pltpu.VMEM((w, D), F32)] * nl,
        compiler_params=_params(("parallel", "arbitrary")),
    )(*lhs_list, rhs, *deps)


def _wgrad_chunk_rhs(lhs, rhs_list, nk, chunked3d, name, deps=()):
    S, D = lhs.shape
    w = rhs_list[0].shape[2] if chunked3d else rhs_list[0].shape[1] // nk
    ts = _row_tile(S, 1024)
    ns = S // ts
    nr = len(rhs_list)
    deps = _deps(deps)

    def body(*refs):
        l_ref, r_refs = refs[0], refs[1:1 + nr]
        o_refs, acc_refs = refs[1 + nr:1 + 2 * nr], refs[1 + 2 * nr:]
        s = pl.program_id(1)

        @pl.when(s == 0)
        def _():
            for acc_ref in acc_refs:
                acc_ref[...] = jnp.zeros_like(acc_ref)

        x = l_ref[...]
        for r_ref, acc_ref in zip(r_refs, acc_refs):
            acc_ref[...] += lax.dot_general(x, r_ref[...], TN, preferred_element_type=F32)

        @pl.when(s == ns - 1)
        def _():
            for o_ref, acc_ref in zip(o_refs, acc_refs):
                o_ref[...] = acc_ref[...]

    if chunked3d:
        rspec = pl.BlockSpec((None, ts, w), lambda k, s: (k, s, 0))
    else:
        rspec = pl.BlockSpec((ts, w), lambda k, s: (s, k))
    ospec = pl.BlockSpec((None, D, w), lambda k, s: (k, 0, 0))
    return pl.pallas_call(
        _blind_to(body, 1 + nr, len(deps)), name=name, grid=(nk, ns),
        in_specs=[pl.BlockSpec((ts, D), lambda k, s: (s, 0))] + [rspec] * nr + [ANY_SPEC] * len(deps),
        out_specs=[ospec] * nr,
        out_shape=[jax.ShapeDtypeStruct((nk, D, w), F32)] * nr,
        scratch_shapes=[pltpu.VMEM((D, w), F32)] * nr,
        compiler_params=_params(("parallel", "arbitrary")),
    )(lhs, *rhs_list, *deps)


def _rope_tables(S):
    pos = np.arange(S, dtype=np.float32)
    inv_freq = (ROPE_THETA ** (-np.arange(0, HEAD_DIM, 2, dtype=np.float32) / HEAD_DIM)).astype(np.float32)
    ang = (pos[:, None] * inv_freq[None, :]).astype(np.float64)
    cos, sin = np.cos(ang).astype(np.float32), np.sin(ang).astype(np.float32)
    cos2 = np.concatenate([cos, cos, cos, cos], axis=1)
    sin2 = np.concatenate([-sin, sin, -sin, sin], axis=1)
    return jnp.asarray(cos2), jnp.asarray(sin2)


def _rotate(t, cos, sin_signed):
    half = HEAD_DIM // 2
    lane = lax.broadcasted_iota(jnp.int32, t.shape, 1)
    first = (lane % HEAD_DIM) < half
    partner = jnp.where(first, pltpu.roll(t, 128 - half, 1), pltpu.roll(t, half, 1))
    return t * cos + partner * sin_signed


def _band_mask(T, has_prev):
    qi = lax.broadcasted_iota(jnp.int32, (T, 2 * T), 0)
    kj = lax.broadcasted_iota(jnp.int32, (T, 2 * T), 1)
    return ((kj < T) & (kj >= qi) & has_prev) | ((kj >= T) & (kj - T <= qi))


def _stack_heads(x, head0):
    zero = jnp.zeros_like(x)
    return jnp.concatenate([jnp.where(head0, x, zero), jnp.where(head0, zero, x)], axis=0)


def _branch_blocks(rows, dilation):
    T = min(ATTN_BLOCK, rows // dilation)
    return T, rows // T


def _block_rows(base, T, dilation):
    if dilation == 1:
        return pl.ds(pl.multiple_of(base, T), T)
    return pl.ds(base, T, stride=dilation)


def _qkv_specs(S):
    groups = ATTN_WIDTH // 128
    return [pl.BlockSpec((S, 128), lambda j, off=t * groups: (0, off + j)) for t in range(3)]


def _attn_fwd(proj, name):
    S, A = proj.shape[0], ATTN_WIDTH
    sup = min(S, SUPER_ROWS)
    nd = len(DILATIONS)
    assert S % sup == 0

    def body(q_ref, k_ref, v_ref, attn_ref, lse_ref, acc_s, m_s, l_s):
        lane = lax.broadcasted_iota(jnp.int32, (1, 128), 1)
        head0 = lane < HEAD_DIM

        def supertile(st, carry):
            row0 = st * sup
            for di, dil in enumerate(DILATIONS):
                T, nblk = _branch_blocks(sup, dil)
                span = T * dil
                assert T == ATTN_BLOCK or sup == S

                def block(idx, c2, di=di, dil=dil, T=T, span=span):
                    r = idx % dil
                    loc = (idx // dil) * span + r
                    base = row0 + loc
                    rows = _block_rows(base, T, dil)
                    prev = _block_rows(jnp.maximum(base - span, r), T, dil)
                    qb = q_ref[rows, :].astype(BF16)
                    k2 = jnp.concatenate([k_ref[prev, :], k_ref[rows, :]], axis=0).astype(BF16)
                    v2 = jnp.concatenate([v_ref[prev, :], v_ref[rows, :]], axis=0).astype(BF16)
                    valid = _band_mask(T, base >= span)
                    q2 = _stack_heads(qb, head0)
                    s = lax.dot_general(q2, k2, NT, preferred_element_type=F32)
                    s = jnp.where(jnp.concatenate([valid, valid], axis=0), s, NEG)
                    m = jnp.max(s, axis=-1, keepdims=True)
                    p = jnp.exp(s - m)
                    l = jnp.sum(p, axis=-1, keepdims=True)
                    acc = jnp.dot(p.astype(BF16), v2, preferred_element_type=F32)
                    lrows = _block_rows(di * sup + loc, T, dil)
                    acc_s[lrows, :] = jnp.where(head0, acc[:T], acc[T:])
                    m_s[lrows, :] = jnp.where(head0, m[:T], m[T:])
                    l_s[lrows, :] = jnp.where(head0, l[:T], l[T:])
                    return c2

                lax.fori_loop(0, nblk, block, 0, unroll=8)

            chunk = min(sup, 256)

            def merge(ci, c2):
                lr = [pl.ds(pl.multiple_of(di * sup + ci * chunk, chunk), chunk) for di in range(nd)]
                gr = pl.ds(pl.multiple_of(row0 + ci * chunk, chunk), chunk)
                m0, m1, m2 = m_s[lr[0], :], m_s[lr[1], :], m_s[lr[2], :]
                mm = jnp.maximum(jnp.maximum(m0, m1), m2)
                w0, w1, w2 = jnp.exp(m0 - mm), jnp.exp(m1 - mm), jnp.exp(m2 - mm)
                den = (w0 * l_s[lr[0], :] + w1 * l_s[lr[1], :]) + w2 * l_s[lr[2], :]
                num = (w0 * acc_s[lr[0], :] + w1 * acc_s[lr[1], :]) + w2 * acc_s[lr[2], :]
                attn_ref[gr, :] = num / den
                lse_ref[gr, :] = mm + jnp.log(den)
                return c2

            lax.fori_loop(0, sup // chunk, merge, 0)
            return carry

        lax.fori_loop(0, S // sup, supertile, 0)

    blk = pl.BlockSpec((S, 128), lambda j: (0, j))
    out = jax.ShapeDtypeStruct((S, A), F32)
    return pl.pallas_call(
        body, name=name, grid=(A // 128,),
        in_specs=_qkv_specs(S), out_specs=[blk, blk], out_shape=[out, out],
        scratch_shapes=[pltpu.VMEM((nd * sup, 128), F32)] * 3,
        compiler_params=_params(("parallel",)),
    )(proj, proj, proj)


def _attn_bwd(proj, da, lse, delta, name):
    S, A = da.shape

    def body(q_ref, k_ref, v_ref, da_ref, lse_ref, dl_ref, dq_ref, dk_ref, dv_ref):
        lane = lax.broadcasted_iota(jnp.int32, (1, 128), 1)
        head0 = lane < HEAD_DIM
        dq_ref[...] = jnp.zeros_like(dq_ref)
        dk_ref[...] = jnp.zeros_like(dk_ref)
        dv_ref[...] = jnp.zeros_like(dv_ref)
        for dil in DILATIONS:
            T, nblk = _branch_blocks(S, dil)
            span = T * dil

            def block(idx, carry, dil=dil, T=T, span=span):
                r = idx % dil
                base = (idx // dil) * span + r
                rows = _block_rows(base, T, dil)
                prev = _block_rows(jnp.maximum(base - span, r), T, dil)
                qb, dab = q_ref[rows, :].astype(BF16), da_ref[rows, :].astype(BF16)
                k2 = jnp.concatenate([k_ref[prev, :], k_ref[rows, :]], axis=0).astype(BF16)
                v2 = jnp.concatenate([v_ref[prev, :], v_ref[rows, :]], axis=0).astype(BF16)
                lse_b, dl_b = lse_ref[rows, :], dl_ref[rows, :]
                valid = _band_mask(T, base >= span)
                valid2 = jnp.concatenate([valid, valid], axis=0)
                q2, da2 = _stack_heads(qb, head0), _stack_heads(dab, head0)
                lse2 = jnp.concatenate([lse_b[:, 0:1], lse_b[:, HEAD_DIM:HEAD_DIM + 1]], axis=0)
                dl2 = jnp.concatenate([dl_b[:, 0:1], dl_b[:, HEAD_DIM:HEAD_DIM + 1]], axis=0)
                s = lax.dot_general(q2, k2, NT, preferred_element_type=F32)
                p = jnp.where(valid2, jnp.exp(s - lse2), 0.0)
                dp = lax.dot_general(da2, v2, NT, preferred_element_type=F32)
                ds = (p * (dp - dl2)).astype(BF16)
                dq2 = jnp.dot(ds, k2, preferred_element_type=F32)
                dk2 = lax.dot_general(ds, q2, TN, preferred_element_type=F32)
                dv2 = lax.dot_general(p.astype(BF16), da2, TN, preferred_element_type=F32)
                dq_ref[rows, :] += jnp.where(head0, dq2[:T], dq2[T:])
                dk_ref[rows, :] += dk2[T:]
                dv_ref[rows, :] += dv2[T:]
                dk_ref[prev, :] += dk2[:T]
                dv_ref[prev, :] += dv2[:T]
                return carry

            lax.fori_loop(0, nblk, block, 0, unroll=8)

    blk = pl.BlockSpec((S, 128), lambda j: (0, j))
    out = jax.ShapeDtypeStruct((S, A), F32)
    return pl.pallas_call(
        body, name=name, grid=(A // 128,),
        in_specs=_qkv_specs(S) + [blk] * 3, out_specs=[blk] * 3, out_shape=[out] * 3,
        compiler_params=_params(("parallel",)),
    )(proj, proj, proj, da, lse, delta)


SUBLANES = 8
CONV_CHUNK = 64
FIRST_TAP = HALO - (CONV_KERNEL - 1)


def _store_shifted(shift_s, win, rows):
    shift_s[0, pl.ds(0, rows), :] = win
    for b in range(1, SUBLANES):
        shift_s[b, pl.ds(0, rows - SUBLANES), :] = win[b:b + rows - SUBLANES, :]


def _glu_window(a_ref, b_ref, ah_ref, bh_ref, first):
    u0 = a_ref[...] * _sigmoid(b_ref[...])
    u0h = ah_ref[...] * _sigmoid(bh_ref[...])
    u0h = jnp.where(first, jnp.zeros_like(u0h), u0h)
    return jnp.concatenate([u0h, u0], axis=0)


def _conv_norms(u1, lng, lnb):
    mu = jnp.mean(u1, axis=-1, keepdims=True)
    xc = u1 - mu
    rstd = lax.rsqrt(jnp.mean(xc * xc, axis=-1, keepdims=True) + LN_EPS)
    u1h = xc * rstd
    u2 = u1h * lng + lnb
    sig = _sigmoid(u2)
    u3 = u2 * sig
    r = lax.rsqrt(jnp.mean(u3 * u3, axis=-1, keepdims=True) + RMS_EPS)
    return rstd, u1h, u2, sig, u3, r


def _conv_specs(tr, C, col_a, col_b):
    per = tr // HALO

    def tile(col):
        return pl.BlockSpec((tr, C), lambda i: (i, col))

    def halo(col):
        return pl.BlockSpec((HALO, C), lambda i: (jnp.maximum(i * per - 1, 0), col))

    return tile(col_a), tile(col_b), halo(col_a), halo(col_b)


def _mixer_merge(proj, attn, cw, cb, lng, lnb, gat, gco, name):
    S = proj.shape[0]
    C = CONV_WIDTH
    A = attn.shape[1]
    tr = _row_tile(S, 256)

    def body(a_ref, b_ref, ah_ref, bh_ref, at_ref, w_ref, cb_ref, lng_ref, lnb_ref, gat_ref, gco_ref, u1_ref, y_ref,
             shift_s):
        _store_shifted(shift_s, _glu_window(a_ref, b_ref, ah_ref, bh_ref, pl.program_id(0) == 0), tr + HALO)

        def chunk(rc, carry):
            r0 = pl.multiple_of(rc * CONV_CHUNK, CONV_CHUNK)
            for lb in range(C // 128):
                lanes = slice(lb * 128, (lb + 1) * 128)
                acc = jnp.broadcast_to(cb_ref[:, lanes], (CONV_CHUNK, 128))
                for j in range(CONV_KERNEL):
                    a8, b = divmod(FIRST_TAP + j, SUBLANES)
                    acc = acc + w_ref[j:j + 1, lanes] * shift_s[b, pl.ds(r0 + a8 * SUBLANES, CONV_CHUNK), lanes]
                u1_ref[pl.ds(r0, CONV_CHUNK), lanes] = acc
            return carry

        lax.fori_loop(0, tr // CONV_CHUNK, chunk, 0)
        _, _, _, _, u3, r = _conv_norms(u1_ref[...], lng_ref[...], lnb_ref[...])
        y_ref[:, A:] = ((u3 * r) * gco_ref[...]).astype(BF16)
        x = at_ref[...]
        ra = lax.rsqrt(jnp.mean(x * x, axis=-1, keepdims=True) + RMS_EPS)
        y_ref[:, :A] = ((x * ra) * gat_ref[...]).astype(BF16)

    ta, tb, ha, hb = _conv_specs(tr, C, 3, 4)
    row = pl.BlockSpec((tr, C), lambda i: (i, 0))
    vec = _vec_spec(C, 1)
    return pl.pallas_call(
        body, name=name, grid=(S // tr,),
        in_specs=[ta, tb, ha, hb, pl.BlockSpec((tr, A), lambda i: (i, 0)), pl.BlockSpec((HALO, C), lambda i: (0, 0)),
                  vec, vec, vec, _vec_spec(A, 1), vec],
        out_specs=[row, pl.BlockSpec((tr, A + C), lambda i: (i, 0))],
        out_shape=[jax.ShapeDtypeStruct((S, C), F32), jax.ShapeDtypeStruct((S, A + C), BF16)],
        scratch_shapes=[pltpu.VMEM((SUBLANES, tr + HALO, C), F32)],
        compiler_params=_params(("parallel",)),
    )(proj, proj, proj, proj, attn, cw, cb, lng, lnb, gat, gco)


def _mix_out_bwd(dmo, wout, attn, u1, gat, lng, lnb, gco, name):
    S, D = dmo.shape
    A, C = attn.shape[1], u1.shape[1]
    tr = _row_tile(S, 256)

    def body(dmo_ref, w_ref, a_ref, u1_ref, gat_ref, lng_ref, lnb_ref, gco_ref,
             da_ref, dl_ref, du1_ref, dgat_ref, dgco_ref, dlng_ref, dlnb_ref, dcb_ref):
        @pl.when(pl.program_id(0) == 0)
        def _():
            for ref in (dgat_ref, dgco_ref, dlng_ref, dlnb_ref, dcb_ref):
                ref[...] = jnp.zeros_like(ref)

        dy = lax.dot_general(dmo_ref[...], w_ref[...], NT, preferred_element_type=F32)
        dya, dyc = dy[:, :A], dy[:, A:]
        x = a_ref[...]
        r = lax.rsqrt(jnp.mean(x * x, axis=-1, keepdims=True) + RMS_EPS)
        xh = x * r
        dgat_ref[...] += jnp.sum(dya * xh, axis=0, keepdims=True)
        dxh = dya * gat_ref[...]
        dx = r * (dxh - xh * jnp.mean(dxh * xh, axis=-1, keepdims=True))
        da_ref[...] = dx
        hi = lax.broadcasted_iota(jnp.int32, (A, A), 0) // HEAD_DIM
        hj = lax.broadcasted_iota(jnp.int32, (A, A), 1) // HEAD_DIM
        same_head = (hi == hj).astype(F32)
        dl_ref[...] = jnp.dot(dx * x, same_head, preferred_element_type=F32, precision=lax.Precision.HIGHEST)

        lng = lng_ref[...]
        rstd, u1h, u2, sig, u3, rc = _conv_norms(u1_ref[...], lng, lnb_ref[...])
        u3h = u3 * rc
        dgco_ref[...] += jnp.sum(dyc * u3h, axis=0, keepdims=True)
        du3h = dyc * gco_ref[...]
        du3 = rc * (du3h - u3h * jnp.mean(du3h * u3h, axis=-1, keepdims=True))
        du2 = du3 * (sig * (1.0 + u2 * (1.0 - sig)))
        dlng_ref[...] += jnp.sum(du2 * u1h, axis=0, keepdims=True)
        dlnb_ref[...] += jnp.sum(du2, axis=0, keepdims=True)
        du1h = du2 * lng
        du1 = rstd * (du1h - jnp.mean(du1h, axis=-1, keepdims=True)
                      - u1h * jnp.mean(du1h * u1h, axis=-1, keepdims=True))
        du1_ref[...] = du1
        dcb_ref[...] += jnp.sum(du1, axis=0, keepdims=True)

    arow = pl.BlockSpec((tr, A), lambda i: (i, 0))
    crow = pl.BlockSpec((tr, C), lambda i: (i, 0))
    avec, cvec = _vec_spec(A, 1), _vec_spec(C, 1)
    return pl.pallas_call(
        body, name=name, grid=(S // tr,),
        in_specs=[pl.BlockSpec((tr, D), lambda i: (i, 0)), pl.BlockSpec((A + C, D), lambda i: (0, 0)), arow, crow,
                  avec, cvec, cvec, cvec],
        out_specs=[arow, arow, crow, avec, cvec, cvec, cvec, cvec],
        out_shape=[jax.ShapeDtypeStruct((S, A), F32)] * 2 + [jax.ShapeDtypeStruct((S, C), F32)]
        + [jax.ShapeDtypeStruct((1, A), F32)] + [jax.ShapeDtypeStruct((1, C), F32)] * 4,
        compiler_params=_params(("arbitrary",)),
    )(dmo, wout, attn, u1, gat, lng, lnb, gco)


def _dproj(du1, proj, cw, dq, dk, dv, cos, sin, name):
    S, C = du1.shape
    A = dq.shape[1]
    tr = _row_tile(S, 256)
    nt = S // tr
    per = tr // HALO
    scale = HEAD_DIM ** -0.5

    def body(du_ref, dun_ref, a_ref, b_ref, ah_ref, bh_ref, w_ref, dq_ref, dk_ref, dv_ref, cos_ref, sin_ref,
             dp_ref, dw_ref, win_s, dwin_s, du0_s, tap_s):
        i = pl.program_id(0)

        @pl.when(i == 0)
        def _():
            dw_ref[...] = jnp.zeros_like(dw_ref)

        _store_shifted(win_s, _glu_window(a_ref, b_ref, ah_ref, bh_ref, i == 0), tr + HALO)
        nxt = jnp.where(i == nt - 1, jnp.zeros_like(dun_ref[...]), dun_ref[...])
        _store_shifted(dwin_s, jnp.concatenate([du_ref[...], nxt], axis=0), tr + HALO)
        tap_s[...] = jnp.zeros_like(tap_s)

        def chunk(rc, carry):
            r0 = pl.multiple_of(rc * CONV_CHUNK, CONV_CHUNK)
            for lb in range(C // 128):
                lanes = slice(lb * 128, (lb + 1) * 128)
                du = du_ref[pl.ds(r0, CONV_CHUNK), lanes]
                acc = jnp.zeros((CONV_CHUNK, 128), F32)
                for j in range(CONV_KERNEL):
                    a8, b = divmod(CONV_KERNEL - 1 - j, SUBLANES)
                    acc = acc + w_ref[j:j + 1, lanes] * dwin_s[b, pl.ds(r0 + a8 * SUBLANES, CONV_CHUNK), lanes]
                    a8, b = divmod(FIRST_TAP + j, SUBLANES)
                    prod = du * win_s[b, pl.ds(r0 + a8 * SUBLANES, CONV_CHUNK), lanes]
                    part = prod[0:SUBLANES]
                    for g in range(1, CONV_CHUNK // SUBLANES):
                        part = part + prod[g * SUBLANES:(g + 1) * SUBLANES]
                    tap_s[j * SUBLANES:(j + 1) * SUBLANES, lanes] += part
                du0_s[pl.ds(r0, CONV_CHUNK), lanes] = acc
            return carry

        lax.fori_loop(0, tr // CONV_CHUNK, chunk, 0)
        taps = [jnp.sum(tap_s[j * SUBLANES:(j + 1) * SUBLANES, :], axis=0, keepdims=True)
                for j in range(CONV_KERNEL)]
        taps.append(jnp.zeros((HALO - CONV_KERNEL, C), F32))
        dw_ref[...] += jnp.concatenate(taps, axis=0)
        du0 = du0_s[...]
        a, sig = a_ref[...], _sigmoid(b_ref[...])
        dp_ref[:, 3 * A:3 * A + C] = (du0 * sig).astype(BF16)
        dp_ref[:, 3 * A + C:] = (du0 * a * sig * (1.0 - sig)).astype(BF16)
        cos_, nsin = cos_ref[...], -sin_ref[...]
        for j in range(A // 128):
            lanes = slice(j * 128, (j + 1) * 128)
            dp_ref[:, j * 128:(j + 1) * 128] = (_rotate(dq_ref[:, lanes], cos_, nsin) * scale).astype(BF16)
            dp_ref[:, A + j * 128:A + (j + 1) * 128] = _rotate(dk_ref[:, lanes], cos_, nsin).astype(BF16)
        dp_ref[:, 2 * A:3 * A] = dv_ref[...].astype(BF16)

    ta, tb, ha, hb = _conv_specs(tr, C, 3, 4)
    row = pl.BlockSpec((tr, C), lambda i: (i, 0))
    arow = pl.BlockSpec((tr, A), lambda i: (i, 0))
    tab = pl.BlockSpec((tr, 128), lambda i: (i, 0))
    nxt = pl.BlockSpec((HALO, C), lambda i: (jnp.minimum((i + 1) * per, S // HALO - 1), 0))
    wspec = pl.BlockSpec((HALO, C), lambda i: (0, 0))
    return pl.pallas_call(
        body, name=name, grid=(nt,),
        in_specs=[row, nxt, ta, tb, ha, hb, wspec, arow, arow, arow, tab, tab],
        out_specs=[pl.BlockSpec((tr, 3 * A + 2 * C), lambda i: (i, 0)), wspec],
        out_shape=[jax.ShapeDtypeStruct((S, 3 * A + 2 * C), BF16), jax.ShapeDtypeStruct((HALO, C), F32)],
        scratch_shapes=[pltpu.VMEM((SUBLANES, tr + HALO, C), F32), pltpu.VMEM((SUBLANES, tr + HALO, C), F32),
                        pltpu.VMEM((tr, C), F32), pltpu.VMEM((HALO * SUBLANES, C), F32)],
        compiler_params=_params(("arbitrary",)),
    )(du1, du1, proj, proj, proj, proj, cw, dq, dk, dv, cos, sin)


def _ada_fwd(c_all, w, b, name):
    B, D = c_all.shape
    N = w.shape[1]
    tn = 768 if N % 768 == 0 else N

    def body(c_ref, w_ref, b_ref, o_ref):
        c = c_ref[...]
        a = (c * _sigmoid(c)).astype(BF16)
        o_ref[...] = jnp.dot(a, w_ref[...].astype(BF16), preferred_element_type=F32) + b_ref[...]

    return pl.pallas_call(
        body, name=name, grid=(N // tn,),
        in_specs=[pl.BlockSpec((B, D), lambda j: (0, 0)), pl.BlockSpec((D, tn), lambda j: (0, j)),
                  pl.BlockSpec((1, tn), lambda j: (0, j))],
        out_specs=pl.BlockSpec((B, tn), lambda j: (0, j)),
        out_shape=jax.ShapeDtypeStruct((B, N), F32),
        compiler_params=_params(("parallel",)),
    )(c_all, w, b)


def _ada_wgrad(c_t, dmod, name):
    D, B = c_t.shape
    N = dmod.shape[1]
    tn = 768 if N % 768 == 0 else N

    def body(c_ref, d_ref, o_ref):
        c = c_ref[...]
        a = (c * _sigmoid(c)).astype(BF16)
        o_ref[...] = jnp.dot(a, d_ref[...].astype(BF16), preferred_element_type=F32)

    return pl.pallas_call(
        body, name=name, grid=(N // tn,),
        in_specs=[pl.BlockSpec((D, B), lambda j: (0, 0)), pl.BlockSpec((B, tn), lambda j: (0, j))],
        out_specs=pl.BlockSpec((D, tn), lambda j: (0, j)),
        out_shape=jax.ShapeDtypeStruct((D, N), F32),
        compiler_params=_params(("parallel",)),
    )(c_t, dmod)


def _sum_rows(x, name):
    R, N = x.shape

    def body(x_ref, o_ref):
        acc = x_ref[0:1, :]
        for r in range(1, R):
            acc = acc + x_ref[r:r + 1, :]
        o_ref[...] = acc

    return pl.pallas_call(
        body, name=name, out_shape=jax.ShapeDtypeStruct((1, N), F32),
        compiler_params=_params(),
    )(x)


def _adamw(w, g, m, v, name, deps=()):
    R, C = w.shape
    tr = _row_tile(R, 256) if R % 256 == 0 else R
    bc1 = 1.0 - ADAM_B1 ** ADAM_STEP
    bc2 = 1.0 - ADAM_B2 ** ADAM_STEP
    deps = _deps(deps)

    def body(w_ref, g_ref, m_ref, v_ref, d_ref, mo_ref, vo_ref, go_ref):
        g_ = g_ref[...]
        m_ = ADAM_B1 * m_ref[...] + (1.0 - ADAM_B1) * g_
        v_ = ADAM_B2 * v_ref[...] + (1.0 - ADAM_B2) * (g_ * g_)
        mo_ref[...] = m_
        vo_ref[...] = v_
        go_ref[...] = g_
        d_ref[...] = -ADAM_LR * ((m_ / bc1) / (jnp.sqrt(v_ / bc2) + ADAM_EPS) + ADAM_WD * w_ref[...])

    row = pl.BlockSpec((tr, C), lambda i: (i, 0))
    out = jax.ShapeDtypeStruct((R, C), F32)
    return pl.pallas_call(
        _blind_to(body, 4, len(deps)), name=name, grid=(R // tr,),
        in_specs=[row] * 4 + [ANY_SPEC] * len(deps), out_specs=[row] * 4, out_shape=[out] * 4,
        compiler_params=_params(("parallel",), streaming=True),
    )(w, g, m, v, *deps)


def _coords():
    return lax.axis_index("x"), lax.axis_index("y"), lax.axis_index("c")


def _all_gather8(x, name, deps=()):
    R, N = x.shape
    assert R == 8
    flips = [(fx, fy, fc) for fx in (0, 1) for fy in (0, 1) for fc in (0, 1)][1:]
    deps = _deps(deps)

    def body(x_ref, o_ref, send_sems, recv_sems):
        mx, my, mc = _coords()
        me = 4 * mx + 2 * my + mc

        def rows(dev):
            return o_ref.at[pl.ds(pl.multiple_of(dev * R, R), R), :]

        o_ref[pl.ds(pl.multiple_of(me * R, R), R), :] = x_ref[...]
        copies = []
        for t, (fx, fy, fc) in enumerate(flips):
            peer = (mx ^ fx, my ^ fy, mc ^ fc)
            copies.append(pltpu.make_async_remote_copy(
                src_ref=x_ref, dst_ref=rows(me), send_sem=send_sems.at[t], recv_sem=recv_sems.at[t],
                device_id=peer, device_id_type=MESH))
        for cp in copies:
            cp.start()
        for t, (fx, fy, fc) in enumerate(flips):
            peer_id = 4 * (mx ^ fx) + 2 * (my ^ fy) + (mc ^ fc)
            pltpu.make_async_remote_copy(
                src_ref=x_ref, dst_ref=rows(peer_id), send_sem=send_sems.at[t], recv_sem=recv_sems.at[t],
                device_id=(mx ^ fx, my ^ fy, mc ^ fc), device_id_type=MESH).wait_recv()
        for cp in copies:
            cp.wait_send()

    return pl.pallas_call(
        _blind_to(body, 1, len(deps)), name=name,
        in_specs=[pl.BlockSpec(memory_space=pltpu.VMEM)] + [ANY_SPEC] * len(deps),
        out_specs=pl.BlockSpec(memory_space=pltpu.VMEM),
        out_shape=jax.ShapeDtypeStruct((N_DEV * R, N), F32),
        scratch_shapes=[pltpu.SemaphoreType.DMA((7,)), pltpu.SemaphoreType.DMA((7,))],
        compiler_params=pltpu.CompilerParams(has_side_effects=True, vmem_limit_bytes=VMEM_LIMIT_BYTES),
    )(x, *deps)


def _half_rows(rows, half):
    return pl.ds(pl.multiple_of(half * (rows // 2), 8), rows // 2)


def _split_start(bufs, plan, n, name):
    nb = len(bufs)

    def body(*refs):
        send_sems, recv_sems, token = refs[nb], refs[nb + 1], refs[-1]
        for t, (src, dst, dev) in enumerate(plan(refs[:nb])):
            pltpu.make_async_remote_copy(src_ref=src, dst_ref=dst, send_sem=send_sems.at[t],
                                         recv_sem=recv_sems.at[t], device_id=dev, device_id_type=MESH).start()
        token[...] = jnp.zeros_like(token)

    out = pl.pallas_call(
        body, name=name,
        out_shape=(pltpu.SemaphoreType.DMA((n,)), pltpu.SemaphoreType.DMA((n,)),
                   *[pltpu.HBM(b.shape, b.dtype) for b in bufs], jax.ShapeDtypeStruct((8, 128), F32)),
        in_specs=[HBM_SPEC] * nb,
        out_specs=(SEM_SPEC, SEM_SPEC, *[HBM_SPEC] * nb, pl.BlockSpec(memory_space=pltpu.VMEM)),
        input_output_aliases={i: 2 + i for i in range(nb)},
        compiler_params=pltpu.CompilerParams(has_side_effects=DATAFLOW),
    )(*[pltpu.with_memory_space_constraint(b, pltpu.HBM) for b in bufs])
    return out[0], out[1], list(out[2:2 + nb]), out[-1]


def _split_wait(bufs, send_sems, recv_sems, plan, after, name):
    nb = len(bufs)
    after = _deps(after)

    def body(*refs):
        ss, rs = refs[nb], refs[nb + 1]
        for t, (src, dst, dev) in enumerate(plan(refs[:nb])):
            cp = pltpu.make_async_remote_copy(src_ref=src, dst_ref=dst, send_sem=ss.at[t], recv_sem=rs.at[t],
                                              device_id=dev, device_id_type=MESH)
            cp.wait_send()
            cp.wait_recv()

    out = pl.pallas_call(
        body, name=name,
        out_shape=tuple(pltpu.HBM(b.shape, b.dtype) for b in bufs),
        in_specs=[HBM_SPEC] * nb + [SEM_SPEC, SEM_SPEC] + [ANY_SPEC] * len(after),
        out_specs=tuple([HBM_SPEC] * nb),
        input_output_aliases={i: i for i in range(nb)},
        compiler_params=pltpu.CompilerParams(has_side_effects=DATAFLOW),
    )(*bufs, send_sems, recv_sems, *after)
    return list(out)


class _Exchange:
    def __init__(self, bufs, plan, n, name):
        self.plan, self.name = plan, name
        self.send_sems, self.recv_sems, self.bufs, self.token = _split_start(bufs, plan, n, name + "_start")

    def wait(self, after):
        return _split_wait(self.bufs, self.send_sems, self.recv_sems, self.plan, after, self.name + "_wait")


def _cast_place(w, chip_idx, name, deps=()):
    R, C = w.shape
    tr = _row_tile(R, 256) if R % 256 == 0 else R
    deps = _deps(deps)

    def body(k_ref, w_ref, o_ref):
        o_ref[...] = w_ref[...].astype(BF16)

    grid_spec = pltpu.PrefetchScalarGridSpec(
        num_scalar_prefetch=1, grid=(R // tr,),
        in_specs=[pl.BlockSpec((tr, C), lambda i, k: (i, 0))] + [ANY_SPEC] * len(deps),
        out_specs=pl.BlockSpec((None, tr, C), lambda i, k: (k[0], i, 0)),
    )
    return pl.pallas_call(
        _blind_to(body, 2, len(deps)), name=name, grid_spec=grid_spec,
        out_shape=jax.ShapeDtypeStruct((N_CHIPS, R, C), BF16),
        compiler_params=_params(("parallel",), streaming=True),
    )(chip_idx, w, *deps)


def _plan_gather(refs):
    mx, my, mc = _coords()
    me = 2 * mx + my
    plan = []
    for g in refs:
        mine = g.at[me, _half_rows(g.shape[1], mc), :]
        for fx, fy in XY_FLIPS:
            plan.append((mine, mine, (mx ^ fx, my ^ fy, mc)))
    return plan


def _plan_swap(refs):
    mx, my, mc = _coords()
    plan = []
    for g in refs:
        for fx, fy in XY_FLIPS:
            have = g.at[2 * (mx ^ fx) + (my ^ fy), _half_rows(g.shape[1], mc), :]
            plan.append((have, have, (mx, my, 1 - mc)))
    return plan


def _plan_other_halves(refs):
    n = len(refs) // 2
    mx, my, mc = _coords()
    return [(g.at[pl.ds(0, N_CHIPS), _half_rows(g.shape[1], 1 - mc), :], land, (mx, my, 1 - mc))
            for g, land in zip(refs[:n], refs[n:])]


def _plan_chunks(refs):
    n = len(refs) // 2
    mx, my, mc = _coords()
    plan = []
    for s, land in zip(refs[:n], refs[n:]):
        for t, (fx, fy) in enumerate(XY_FLIPS):
            plan.append((s.at[2 * (mx ^ fx) + (my ^ fy)], land.at[t], (mx ^ fx, my ^ fy, mc)))
    return plan


def _plan_share(refs):
    mx, my, mc = _coords()
    return [(full.at[mc], full.at[mc], (mx, my, 1 - mc)) for full in refs]


def _add_half(g, recv, core_idx, name):
    nk, R, C = g.shape
    rh = R // 2
    tr = _row_tile(rh, 128) if rh % 128 == 0 else rh
    nt = rh // tr

    def body(c_ref, g_ref, r_ref, o_ref):
        o_ref[...] = (g_ref[...] + r_ref[...]).astype(BF16)

    grid_spec = pltpu.PrefetchScalarGridSpec(
        num_scalar_prefetch=1, grid=(nk, nt),
        in_specs=[pl.BlockSpec((None, tr, C), lambda k, i, c: (k, c[0] * nt + i, 0)),
                  pl.BlockSpec((None, tr, C), lambda k, i, c: (k, i, 0))],
        out_specs=pl.BlockSpec((None, tr, C), lambda k, i, c: (k, i, 0)),
    )
    return pl.pallas_call(
        body, name=name, grid_spec=grid_spec, out_shape=jax.ShapeDtypeStruct((nk, rh, C), BF16),
        compiler_params=_params(("parallel", "parallel"), streaming=True),
    )(core_idx, g, recv)


def _sum_chips(s, land, chip_core, name):
    _, rh, C = s.shape
    tr = _row_tile(rh, 128) if rh % 128 == 0 else rh

    def body(p_ref, s_ref, l_ref, o_ref):
        me = p_ref[0]
        acc = None
        for j in range(N_CHIPS):
            t = jnp.maximum(jnp.bitwise_xor(me, j) - 1, 0)
            term = jnp.where(me == j, s_ref[...], l_ref[t]).astype(F32)
            acc = term if acc is None else acc + term
        o_ref[...] = acc

    grid_spec = pltpu.PrefetchScalarGridSpec(
        num_scalar_prefetch=1, grid=(rh // tr,),
        in_specs=[pl.BlockSpec((None, tr, C), lambda i, p: (p[0], i, 0)),
                  pl.BlockSpec((3, tr, C), lambda i, p: (0, i, 0))],
        out_specs=pl.BlockSpec((None, tr, C), lambda i, p: (p[1], i, 0)),
    )
    return pl.pallas_call(
        body, name=name, grid_spec=grid_spec, out_shape=jax.ShapeDtypeStruct((2, rh, C), F32),
        compiler_params=_params(("parallel",), streaming=True),
    )(chip_core, s, land)


def _rs_send_halves(grads, tag):
    lands = [lax.empty((g.shape[0], g.shape[1] // 2, g.shape[2]), g.dtype) for g in grads]
    return _Exchange(list(grads) + lands, _plan_other_halves, len(grads), f"rs_halves_{tag}")


def _rs_send_chunks(ex, after, core_idx, tag):
    bufs = ex.wait(after)
    n = len(bufs) // 2
    sums = [_add_half(g, r, core_idx, f"rs_add_{tag}_{i}") for i, (g, r) in enumerate(zip(bufs[:n], bufs[n:]))]
    lands = [lax.empty((3,) + s.shape[1:], s.dtype) for s in sums]
    return _Exchange(sums + lands, _plan_chunks, 3 * n, f"rs_chunks_{tag}")


def _rs_send_share(ex, after, chip_core, tag):
    bufs = ex.wait(after)
    n = len(bufs) // 2
    fulls = [_sum_chips(s, l, chip_core, f"rs_sum_{tag}_{i}") for i, (s, l) in enumerate(zip(bufs[:n], bufs[n:]))]
    return _Exchange(fulls, _plan_share, n, f"rs_share_{tag}")


def _rs_finish(ex, after):
    return [b.reshape(2 * b.shape[1], b.shape[2]) for b in ex.wait(after)]


def _ffn_backward(dh_out, df, saved, gain, sc, wg, wu, wd, core_idx, tag, prev=None, last=False):
    h, n, ga, up, act, _ = saved
    (dwd,) = _wgrad_chunk_lhs([act], df, f"{tag}_dwd")
    ex_d = _rs_send_halves([dwd], f"{tag}_d")
    dga, dup = _ffn_dact(df, wd, ga, up, f"{tag}_dact", deps=[ex_d.token])
    ex_d = _rs_send_chunks(ex_d, [dga], core_idx, f"{tag}_d")
    dwg, dwu = _wgrad_chunk_lhs([dga, dup], n, f"{tag}_dwgu", deps=[ex_d.token])
    ex_gu = _rs_send_halves([dwg, dwu], f"{tag}_gu")
    dn = _mm_reduce([dga, dup], [wg, wu], True, True, f"{tag}_dn", deps=[ex_gu.token])
    if not last:
        ex_gu = _rs_send_chunks(ex_gu, [dn], core_idx, f"{tag}_gu")
    outs = _norm_mod_bwd(dn, h, gain, sc, dh_out, f"{tag}_norm_bwd", prev=prev, deps=[ex_gu.token])
    return outs, (ex_d, ex_gu)


def _pad_cols(v, n):
    return jnp.pad(v, ((0, 0), (0, n - v.shape[1])))


def _mixer_forward(h1, n2, gt2, win, wout, conv_w, conv_dw_b, conv_ln_g, conv_ln_b, attn_out_g, conv_out_g,
                   next_norm=None):
    S, D = h1.shape
    cos, sin = _rope_tables(S)
    proj = _mm_cols_rope(n2, win, cos, sin, "mix_in")
    attn, lse = _attn_fwd(proj, "attn_fwd")
    u1, y = _mixer_merge(proj, attn, conv_w, conv_dw_b, conv_ln_g, conv_ln_b, attn_out_g, conv_out_g, "mix_merge")
    h2, mo, *n_next = _mm_residual(y[None], wout.reshape(1, D, D), h1, gt2, 1.0, "mix_out", next_norm=next_norm)
    return (h2, *n_next), (h1, n2, proj, cos, sin, attn, lse, u1, y, mo)


def _mixer_backward(dh2, dmo, saved, mix_norm_g, sc2, win, wout, conv_w, conv_ln_g, conv_ln_b, attn_out_g,
                    conv_out_g, core_idx, prev=None):
    h1, n2, proj, cos, sin, attn, lse, u1, y, _ = saved
    S, D = h1.shape
    (dwout,) = _wgrad_chunk_lhs([y[None]], dmo, "mix_dwout")
    dattn, delta, du1, d_attn_g, d_gco, d_lng, d_lnb, d_cb = _mix_out_bwd(
        dmo, wout.reshape(D, D), attn, u1, attn_out_g, conv_ln_g, conv_ln_b, conv_out_g, "mix_out_bwd")
    dq, dk, dv = _attn_bwd(proj, dattn, lse, delta, "attn_bwd")
    dproj, d_cw = _dproj(du1, proj, conv_w, dq, dk, dv, cos, sin, "mix_dproj")
    (dwin,) = _wgrad_chunk_rhs(n2, [dproj], N_CHIPS, False, "mix_dwin")
    ex = _rs_send_halves([dwin, dwout.reshape(N_CHIPS, D // N_CHIPS, D)], "mix")
    dn2 = _mm_reduce([dproj], [win], False, False, "mix_dn", deps=[ex.token])
    ex = _rs_send_chunks(ex, [dn2], core_idx, "mix")
    outs = _norm_mod_bwd(dn2, h1, mix_norm_g, sc2, dh2, "mix_norm_bwd", prev=prev, deps=[ex.token])
    return outs, ex, (d_cb, d_lng, d_lnb, d_attn_g, d_gco, d_cw)


def kernel(x, c, w_ada, b_ada, ffn1_norm_g, ffn1_w_gate, ffn1_w_up, ffn1_w_down, mix_norm_g, w_in, conv_dw_w, conv_dw_b, conv_ln_g, conv_ln_b, attn_out_g, conv_out_g, w_out, ffn2_norm_g, ffn2_w_gate, ffn2_w_up, ffn2_w_down, final_norm_g, loss_target, m_w_ada, m_b_ada, m_ffn1_norm_g, m_ffn1_w_gate, m_ffn1_w_up, m_ffn1_w_down, m_mix_norm_g, m_w_in, m_conv_dw_w, m_conv_dw_b, m_conv_ln_g, m_conv_ln_b, m_attn_out_g, m_conv_out_g, m_w_out, m_ffn2_norm_g, m_ffn2_w_gate, m_ffn2_w_up, m_ffn2_w_down, m_final_norm_g, v_w_ada, v_b_ada, v_ffn1_norm_g, v_ffn1_w_gate, v_ffn1_w_up, v_ffn1_w_down, v_mix_norm_g, v_w_in, v_conv_dw_w, v_conv_dw_b, v_conv_ln_g, v_conv_ln_b, v_attn_out_g, v_conv_out_g, v_w_out, v_ffn2_norm_g, v_ffn2_w_gate, v_ffn2_w_up, v_ffn2_w_down, v_final_norm_g):
    S, D = x.shape[1], x.shape[2]
    mx, my, mc = _coords()
    chip = 2 * mx + my
    dev = 4 * mx + 2 * my + mc
    chip_idx = chip.astype(jnp.int32).reshape(1)
    core_idx = mc.astype(jnp.int32).reshape(1)
    chip_core = jnp.stack([chip, mc]).astype(jnp.int32)
    h0 = x[0]
    target = loss_target[0]

    ncw = CONV_KERNEL * 128
    n0 = -(-(D + ncw) // 1024) * 1024
    pk0 = _pad_cols(jnp.concatenate([c.reshape(1, D), conv_dw_w.reshape(1, ncw)], axis=1), n0)
    g0 = _all_gather8(pk0.reshape(8, n0 // 8), "gather_c").reshape(N_DEV, n0)
    c_all = g0[:, :D]
    conv_w = jnp.concatenate([g0[2 * kc, D:D + ncw].reshape(CONV_KERNEL, 128) for kc in range(N_CHIPS)], axis=1)
    conv_w = jnp.pad(conv_w, ((0, HALO - CONV_KERNEL), (0, 0)))
    nmod = w_ada.shape[2]
    b_shard = lax.dynamic_slice(b_ada, (0, chip * nmod), (1, nmod))
    mod_part = _ada_fwd(c_all, w_ada[0], b_shard, "ada_fwd")
    g1 = _all_gather8(mod_part, "gather_mod")
    mod_all = jnp.concatenate([g1[16 * kc:16 * kc + 8] for kc in range(N_CHIPS)], axis=1)
    mod = lax.dynamic_slice(mod_all, (dev, 0), (1, 9 * D))
    sh1, sc1, gt1, sh2, sc2, gt2, sh3, sc3, gt3 = [mod[:, i * D:(i + 1) * D] for i in range(9)]

    def gather_start(ws, tag, dep):
        slots = [_cast_place(w, chip_idx, f"cast_{tag}_{i}", deps=[dep]) for i, w in enumerate(ws)]
        return _Exchange(slots, _plan_gather, 3 * len(ws), f"gather_{tag}")

    def swap_start(ex, after, tag):
        return _Exchange(ex.wait(after), _plan_swap, 3 * len(ex.bufs), f"swap_{tag}")

    ex_gu1 = gather_start([ffn1_w_gate[0].T, ffn1_w_up[0].T], "ffn1_gu", g1)
    n1 = _norm_mod(h0, ffn1_norm_g, sc1, sh1, "ffn1_norm", deps=[ex_gu1.token])
    ex_d1 = gather_start([ffn1_w_down[0]], "ffn1_d", n1)
    ex_wm = gather_start([w_in[0], w_out[0]], "mix", ex_d1.token)
    ex_w2 = gather_start([ffn2_w_gate[0].T, ffn2_w_up[0].T, ffn2_w_down[0]], "ffn2", ex_wm.token)

    wg1, wu1 = swap_start(ex_gu1, [ex_w2.token], "ffn1_gu").wait([])
    ga1, up1, act1 = _ffn_gate_up(n1, wg1, wu1, "ffn1_gate_up")
    (wd1,) = swap_start(ex_d1, [act1], "ffn1_d").wait([])
    ex_wm = swap_start(ex_wm, [wd1], "mix")
    h1, f1, n2 = _mm_residual(act1, wd1, h0, gt1, 0.5, "ffn1_down", next_norm=(mix_norm_g, sc2, sh2))
    saved1 = (h0, n1, ga1, up1, act1, f1)
    win, wout = ex_wm.wait([h1])
    ex_w2 = swap_start(ex_w2, [h1], "ffn2")
    (h2, n3), saved2 = _mixer_forward(h1, n2, gt2, win, wout, conv_w, conv_dw_b, conv_ln_g, conv_ln_b,
                                      attn_out_g, conv_out_g, next_norm=(ffn2_norm_g, sc3, sh3))
    wg2, wu2, wd2 = ex_w2.wait([h2])
    fga3, fup3, act3 = _ffn_gate_up(n3, wg2, wu2, "ffn2_gate_up")
    loss_part, dh3, d_final_g, df3, d_gt3 = _mm_residual_loss(
        act3, wd2, h2, gt3, 0.5, final_norm_g.reshape(1, D), target, "ffn2_down_loss")
    saved3 = (h2, n3, fga3, fup3, act3, None)

    (dh2, d_sh3, d_sc3, d_gain3, dmo, d_gt2), (ex_d2, ex_gu2) = _ffn_backward(
        dh3, df3, saved3, ffn2_norm_g, sc3, wg2, wu2, wd2, core_idx, "ffn2", prev=(saved2[-1], gt2, 1.0))
    (dh1, d_sh2, d_sc2, d_gain2, df1, d_gt1), ex_mix, small_mix = _mixer_backward(
        dh2, dmo, saved2, mix_norm_g, sc2, win, wout, conv_w, conv_ln_g, conv_ln_b, attn_out_g, conv_out_g, core_idx,
        prev=(f1, gt1, 0.5))
    d_cb, d_lng, d_lnb, d_attn_g, d_gco, d_cw = small_mix
    (dh0, d_sh1, d_sc1, d_gain1), (ex_d1, ex_gu1) = _ffn_backward(
        dh1, df1, saved1, ffn1_norm_g, sc1, wg1, wu1, wd1, core_idx, "ffn1", last=True)

    dmod = jnp.concatenate([d_sh1, d_sc1, d_gt1, d_sh2, d_sc2, d_gt2, d_sh3, d_sc3, d_gt3], axis=1)
    small = [d_gain1, d_gain2, d_gain3, d_final_g, d_cb, d_lng, d_lnb, d_attn_g, d_gco,
             d_cw[:CONV_KERNEL].reshape(1, CONV_KERNEL * CONV_WIDTH), loss_part]
    pk1 = jnp.concatenate([dmod] + small, axis=1)
    n1_ = -(-pk1.shape[1] // 1024) * 1024
    gathered = _all_gather8(_pad_cols(pk1, n1_).reshape(8, n1_ // 8), "gather_small").reshape(N_DEV, n1_)
    ex_gu1 = _rs_send_chunks(ex_gu1, [gathered], core_idx, "ffn1_gu")
    tot = _sum_rows(gathered, "sum_small")
    off = [0]

    def take(nel):
        out = tot[:, off[0]:off[0] + nel]
        off[0] += nel
        return out

    g_b_ada = take(9 * D)
    g_ffn1_norm, g_mix_norm, g_ffn2_norm, g_final = take(D), take(D), take(D), take(D)
    g_cb, g_lng, g_lnb, g_attn_g, g_gco = take(512), take(512), take(512), take(512), take(512)
    g_cw_full = take(CONV_KERNEL * CONV_WIDTH).reshape(CONV_KERNEL, CONV_WIDTH)
    loss = take(128)[0, 0]
    g_cw = lax.dynamic_slice(g_cw_full, (0, chip * 128), (CONV_KERNEL, 128))

    dmod_shard = lax.dynamic_slice(gathered[:, :9 * D], (0, chip * nmod), (N_DEV, nmod))
    dmod16 = jnp.pad(dmod_shard, ((0, N_DEV), (0, 0)))
    c_t16 = jnp.pad(c_all.T, ((0, 0), (0, N_DEV)))
    g_w_ada = _ada_wgrad(c_t16, dmod16, "ada_wgrad")

    names = ["w_ada", "b_ada", "ffn1_norm_g", "ffn1_w_gate", "ffn1_w_up", "ffn1_w_down", "mix_norm_g", "w_in",
             "conv_dw_w", "conv_dw_b", "conv_ln_g", "conv_ln_b", "attn_out_g", "conv_out_g", "w_out", "ffn2_norm_g",
             "ffn2_w_gate", "ffn2_w_up", "ffn2_w_down", "final_norm_g"]
    weights = dict(zip(names, [w_ada, b_ada, ffn1_norm_g, ffn1_w_gate, ffn1_w_up, ffn1_w_down, mix_norm_g, w_in,
                               conv_dw_w, conv_dw_b, conv_ln_g, conv_ln_b, attn_out_g, conv_out_g, w_out,
                               ffn2_norm_g, ffn2_w_gate, ffn2_w_up, ffn2_w_down, final_norm_g]))
    ms = dict(zip(names, [m_w_ada, m_b_ada, m_ffn1_norm_g, m_ffn1_w_gate, m_ffn1_w_up, m_ffn1_w_down, m_mix_norm_g,
                          m_w_in, m_conv_dw_w, m_conv_dw_b, m_conv_ln_g, m_conv_ln_b, m_attn_out_g, m_conv_out_g,
                          m_w_out, m_ffn2_norm_g, m_ffn2_w_gate, m_ffn2_w_up, m_ffn2_w_down, m_final_norm_g]))
    vs = dict(zip(names, [v_w_ada, v_b_ada, v_ffn1_norm_g, v_ffn1_w_gate, v_ffn1_w_up, v_ffn1_w_down, v_mix_norm_g,
                          v_w_in, v_conv_dw_w, v_conv_dw_b, v_conv_ln_g, v_conv_ln_b, v_attn_out_g, v_conv_out_g,
                          v_w_out, v_ffn2_norm_g, v_ffn2_w_gate, v_ffn2_w_up, v_ffn2_w_down, v_final_norm_g]))
    grads, deltas, new_ms, new_vs = {}, {}, {}, {}

    def adamw_big(nm, g2d, deps=(), transposed=False):
        shape = weights[nm].shape
        two_d = (shape[-2], shape[-1])

        def view(t):
            return t.reshape(two_d).T if transposed else t.reshape(two_d)

        d_, m_, v_, g_ = _adamw(view(weights[nm]), g2d, view(ms[nm]), view(vs[nm]), f"adamw_{nm}", deps=deps)
        grads[nm], deltas[nm], new_ms[nm], new_vs[nm] = (
            (t.T if transposed else t).reshape(shape) for t in (g_, d_, m_, v_))
        return d_

    d_ada = adamw_big("w_ada", g_w_ada, deps=[ex_gu1.token])
    small_grads = {"b_ada": g_b_ada, "ffn1_norm_g": g_ffn1_norm, "mix_norm_g": g_mix_norm, "conv_dw_w": g_cw,
                   "conv_dw_b": g_cb, "conv_ln_g": g_lng, "conv_ln_b": g_lnb, "attn_out_g": g_attn_g,
                   "conv_out_g": g_gco, "ffn2_norm_g": g_ffn2_norm, "final_norm_g": g_final}
    small_names = [nm for nm in names if nm in small_grads]

    def pack_small(arrs):
        flat = jnp.concatenate([arrs[nm].reshape(1, -1) for nm in small_names], axis=1)
        npad = -(-flat.shape[1] // 1024) * 1024
        return _pad_cols(flat, npad).reshape(8, npad // 8)

    d_s, m_s, v_s, _ = _adamw(pack_small(weights), pack_small(small_grads), pack_small(ms), pack_small(vs),
                           "adamw_small")
    pos = 0
    for nm in small_names:
        shape, nel = weights[nm].shape, weights[nm].size
        grads[nm] = small_grads[nm].reshape(shape)
        deltas[nm], new_ms[nm], new_vs[nm] = (t.reshape(1, -1)[:, pos:pos + nel].reshape(shape)
                                              for t in (d_s, m_s, v_s))
        pos += nel

    ex_d2 = _rs_send_share(ex_d2, [d_ada, d_s], chip_core, "ffn2_d")
    ex_gu2 = _rs_send_share(ex_gu2, [ex_d2.token], chip_core, "ffn2_gu")
    ex_mix = _rs_send_share(ex_mix, [ex_gu2.token], chip_core, "mix")
    ex_d1 = _rs_send_share(ex_d1, [ex_mix.token], chip_core, "ffn1_d")
    (g_wd2,) = _rs_finish(ex_d2, [ex_d1.token])
    last = [adamw_big("ffn2_w_down", g_wd2)]
    g_wg2, g_wu2 = _rs_finish(ex_gu2, last)
    last = [adamw_big("ffn2_w_gate", g_wg2, transposed=True), adamw_big("ffn2_w_up", g_wu2, transposed=True)]
    g_win, g_wout = _rs_finish(ex_mix, last)
    last = [adamw_big("w_in", g_win), adamw_big("w_out", g_wout)]
    (g_wd1,) = _rs_finish(ex_d1, last)
    last = [adamw_big("ffn1_w_down", g_wd1)]
    ex_gu1 = _rs_send_share(ex_gu1, last, chip_core, "ffn1_gu")
    g_wg1, g_wu1 = _rs_finish(ex_gu1, [])
    adamw_big("ffn1_w_gate", g_wg1, transposed=True)
    adamw_big("ffn1_w_up", g_wu1, transposed=True)

    return (loss, dh0[None], *[grads[nm] for nm in names], *[deltas[nm] for nm in names],
            *[new_ms[nm] for nm in names], *[new_vs[nm] for nm in names])
```

```python
import jax
import jax.numpy as jnp
import numpy as np
from jax import lax
from jax.experimental import pallas as pl
from jax.experimental.pallas import tpu as pltpu

F32 = jnp.float32
BF16 = jnp.bfloat16
MESH = pl.DeviceIdType.MESH

RMS_EPS = 1e-6
LN_EPS = 1e-5
HEAD_DIM = 64
ATTN_WIDTH = 512
CONV_WIDTH = 512
ATTN_BLOCK = 128
DILATIONS = (1, 4, 16)
SUPER_ROWS = ATTN_BLOCK * 16
ROPE_THETA = 10000.0
CONV_KERNEL = 31
HALO = 32
N_CHIPS = 4
N_DEV = 8
ADAM_LR, ADAM_B1, ADAM_B2, ADAM_EPS, ADAM_WD, ADAM_STEP = 0.001, 0.9, 0.999, 1e-08, 0.01, 10
VMEM_LIMIT_BYTES = 48 * 1024 * 1024
VMEM_LIMIT_STREAMING = 62 * 1024 * 1024
NEG = -1e30

NT = (((1,), (1,)), ((), ()))
TN = (((0,), (0,)), ((), ()))

ANY_SPEC = pl.BlockSpec(memory_space=pl.ANY)
HBM_SPEC = pl.BlockSpec(memory_space=pltpu.HBM)
SEM_SPEC = pl.BlockSpec(memory_space=pltpu.SEMAPHORE)
DATAFLOW = pltpu.SideEffectType.DATAFLOW_SIDE_EFFECTING
XY_FLIPS = ((0, 1), (1, 0), (1, 1))


def _params(sem=None, streaming=False):
    return pltpu.CompilerParams(dimension_semantics=sem,
                                vmem_limit_bytes=VMEM_LIMIT_STREAMING if streaming else VMEM_LIMIT_BYTES)


def _row_tile(rows, want):
    t = min(rows, want)
    assert rows % t == 0
    return t


def _sigmoid(x):
    return 1.0 / (1.0 + jnp.exp(-x))


def _deps(deps):
    return [d for d in deps if d is not None]


def _blind_to(body, n_in, n_dep):
    def wrapped(*refs):
        return body(*refs[:n_in], *refs[n_in + n_dep:])
    return wrapped


def _vec_spec(d, ngrid):
    if ngrid == 1:
        return pl.BlockSpec((1, d), lambda i: (0, 0))
    return pl.BlockSpec((1, d), lambda i, j: (0, 0))


def _norm_mod(h, gain, sc, sh, name, deps=()):
    S, D = h.shape
    tr = _row_tile(S, 512)
    deps = _deps(deps)

    def body(h_ref, g_ref, sc_ref, sh_ref, n_ref):
        x = h_ref[...]
        r = lax.rsqrt(jnp.mean(x * x, axis=-1, keepdims=True) + RMS_EPS)
        y = (x * r) * g_ref[...]
        n_ref[...] = (y * (1.0 + sc_ref[...]) + sh_ref[...]).astype(BF16)

    row = pl.BlockSpec((tr, D), lambda i: (i, 0))
    return pl.pallas_call(
        _blind_to(body, 4, len(deps)), name=name, grid=(S // tr,),
        in_specs=[row, _vec_spec(D, 1), _vec_spec(D, 1), _vec_spec(D, 1)] + [ANY_SPEC] * len(deps),
        out_specs=row, out_shape=jax.ShapeDtypeStruct((S, D), BF16),
        compiler_params=_params(("parallel",)),
    )(h, gain, sc, sh, *deps)


def _norm_mod_bwd(dn, h_in, gain, sc, dh_out, name, prev=None, deps=()):
    S, D = h_in.shape
    tr = _row_tile(S, 512)
    deps = _deps(deps)
    n_in = 5 if prev is None else 7

    def body(*refs):
        dn_ref, h_ref, g_ref, sc_ref, dho_ref = refs[:5]
        dh_ref, dsh_ref, dsc_ref, dg_ref = refs[n_in:n_in + 4]

        @pl.when(pl.program_id(0) == 0)
        def _():
            for ref in refs[n_in + 1:n_in + 4] + refs[n_in + 5:]:
                ref[...] = jnp.zeros_like(ref)

        x = h_ref[...]
        dn_ = dn_ref[...]
        g = g_ref[...]
        one_sc = 1.0 + sc_ref[...]
        r = lax.rsqrt(jnp.mean(x * x, axis=-1, keepdims=True) + RMS_EPS)
        xh = x * r
        dsh_ref[...] += jnp.sum(dn_, axis=0, keepdims=True)
        dsc_ref[...] += jnp.sum(dn_ * (xh * g), axis=0, keepdims=True)
        dg_ref[...] += jnp.sum(dn_ * one_sc * xh, axis=0, keepdims=True)
        dxh = dn_ * (g * one_sc)
        dh = dho_ref[...] + r * (dxh - xh * jnp.mean(dxh * xh, axis=-1, keepdims=True))
        dh_ref[...] = dh
        if prev is not None:
            _gate_back(dh, refs[5], refs[6], prev[2], refs[n_in + 4], refs[n_in + 5])

    row = pl.BlockSpec((tr, D), lambda i: (i, 0))
    vec = _vec_spec(D, 1)
    extra_in = [] if prev is None else [row, vec]
    extra_out = [] if prev is None else [row, vec]
    extra_shape = [] if prev is None else [jax.ShapeDtypeStruct((S, D), BF16), jax.ShapeDtypeStruct((1, D), F32)]
    return pl.pallas_call(
        _blind_to(body, n_in, len(deps)), name=name, grid=(S // tr,),
        in_specs=[row, row, vec, vec, row] + extra_in + [ANY_SPEC] * len(deps),
        out_specs=[row, vec, vec, vec] + extra_out,
        out_shape=[jax.ShapeDtypeStruct((S, D), F32)] + [jax.ShapeDtypeStruct((1, D), F32)] * 3 + extra_shape,
        compiler_params=_params(("arbitrary",)),
    )(dn, h_in, gain, sc, dh_out, *([] if prev is None else prev[:2]), *deps)


def _gate_back(dh, f_ref, gate_ref, coef, df_ref, dgate_ref):
    df_ref[...] = ((coef * gate_ref[...]) * dh).astype(BF16)
    dgate_ref[...] += jnp.sum(coef * dh * f_ref[...].astype(F32), axis=0, keepdims=True)


def _mm_residual_loss(lhs, w, h_in, gate, coef, gain, target, name):
    nk, S, kc = lhs.shape
    D = w.shape[2]
    tm = _row_tile(S, 512)

    def body(l_ref, w_ref, h_ref, gate_ref, g_ref, t_ref, loss_ref, dh_ref, dg_ref, df_ref, dgate_ref):
        @pl.when(pl.program_id(0) == 0)
        def _():
            loss_ref[...] = jnp.zeros_like(loss_ref)
            dg_ref[...] = jnp.zeros_like(dg_ref)
            dgate_ref[...] = jnp.zeros_like(dgate_ref)

        f = _chunk_dots(l_ref, w_ref, nk)
        cg = coef * gate_ref[...]
        x = h_ref[...] + cg * f
        g = g_ref[...]
        r = lax.rsqrt(jnp.mean(x * x, axis=-1, keepdims=True) + RMS_EPS)
        xh = x * r
        err = xh * g - t_ref[...]
        part = 0.5 * jnp.sum(jnp.mean(err * err, axis=-1, keepdims=True), axis=0, keepdims=True)
        loss_ref[...] += jnp.broadcast_to(part, loss_ref.shape)
        dy = err * (1.0 / D)
        dg_ref[...] += jnp.sum(dy * xh, axis=0, keepdims=True)
        dxh = dy * g
        dh = r * (dxh - xh * jnp.mean(dxh * xh, axis=-1, keepdims=True))
        dh_ref[...] = dh
        df_ref[...] = (cg * dh).astype(BF16)
        dgate_ref[...] += jnp.sum(coef * dh * f, axis=0, keepdims=True)

    row = pl.BlockSpec((tm, D), lambda m: (m, 0))
    vec = _vec_spec(D, 1)
    return pl.pallas_call(
        body, name=name, grid=(S // tm,),
        in_specs=[pl.BlockSpec((nk, tm, kc), lambda m: (0, m, 0)),
                  pl.BlockSpec((nk, kc, D), lambda m: (0, 0, 0)), row, vec, vec, row],
        out_specs=[pl.BlockSpec((1, 128), lambda m: (0, 0)), row, vec, row, vec],
        out_shape=[jax.ShapeDtypeStruct((1, 128), F32), jax.ShapeDtypeStruct((S, D), F32),
                   jax.ShapeDtypeStruct((1, D), F32), jax.ShapeDtypeStruct((S, D), BF16),
                   jax.ShapeDtypeStruct((1, D), F32)],
        compiler_params=_params(("arbitrary",)),
    )(lhs, w, h_in, gate, gain, target)


def _ffn_gate_up(n, wg_t, wu_t, name):
    S, D = n.shape
    nk, w, _ = wg_t.shape
    tm = _row_tile(S, 512)

    def body(n_ref, wg_ref, wu_ref, dga_ref, dup_ref, act_ref):
        x = n_ref[...]
        ga = lax.dot_general(x, wg_ref[...], NT, preferred_element_type=F32)
        up = lax.dot_general(x, wu_ref[...], NT, preferred_element_type=F32)
        sig = _sigmoid(ga)
        silu = ga * sig
        dga_ref[...] = (up * (sig * (1.0 + ga * (1.0 - sig)))).astype(BF16)
        dup_ref[...] = silu.astype(BF16)
        act_ref[...] = (silu * up).astype(BF16)

    wspec = pl.BlockSpec((None, w, D), lambda k, m: (k, 0, 0))
    ospec = pl.BlockSpec((None, tm, w), lambda k, m: (k, m, 0))
    out = jax.ShapeDtypeStruct((nk, S, w), BF16)
    return pl.pallas_call(
        body, name=name, grid=(nk, S // tm),
        in_specs=[pl.BlockSpec((tm, D), lambda k, m: (m, 0)), wspec, wspec],
        out_specs=[ospec, ospec, ospec], out_shape=[out, out, out],
        compiler_params=_params(("parallel", "parallel")),
    )(n, wg_t, wu_t)


def _chunk_dots(l_ref, w_ref, nk):
    f = jnp.dot(l_ref[0], w_ref[0], preferred_element_type=F32)
    for k in range(1, nk):
        f = f + jnp.dot(l_ref[k], w_ref[k], preferred_element_type=F32)
    return f


def _mm_residual(lhs, w, h_in, gvec, coef, name, next_norm=None):
    nk, S, kc = lhs.shape
    D = w.shape[2]
    tm = _row_tile(S, 512)
    n_in = 4 if next_norm is None else 7

    def body(*refs):
        l_ref, w_ref, h_ref, g_ref = refs[:4]
        ho_ref, f_ref = refs[n_in:n_in + 2]
        f = _chunk_dots(l_ref, w_ref, nk)
        f_ref[...] = f.astype(BF16)
        x = h_ref[...] + (coef * g_ref[...]) * f
        ho_ref[...] = x
        if next_norm is not None:
            ng_ref, sc_ref, sh_ref = refs[4:7]
            r = lax.rsqrt(jnp.mean(x * x, axis=-1, keepdims=True) + RMS_EPS)
            y = (x * r) * ng_ref[...]
            refs[n_in + 2][...] = (y * (1.0 + sc_ref[...]) + sh_ref[...]).astype(BF16)

    row = pl.BlockSpec((tm, D), lambda m: (m, 0))
    vec = _vec_spec(D, 1)
    with_n = next_norm is not None
    return pl.pallas_call(
        body, name=name, grid=(S // tm,),
        in_specs=[pl.BlockSpec((nk, tm, kc), lambda m: (0, m, 0)),
                  pl.BlockSpec((nk, kc, D), lambda m: (0, 0, 0)), row, vec] + [vec] * (3 * with_n),
        out_specs=[row, row] + [row] * with_n,
        out_shape=[jax.ShapeDtypeStruct((S, D), F32), jax.ShapeDtypeStruct((S, D), BF16)]
        + [jax.ShapeDtypeStruct((S, D), BF16)] * with_n,
        compiler_params=_params(("parallel",)),
    )(lhs, w, h_in, gvec, *(next_norm or ()))


def _mm_cols_rope(n, w, cos, sin, name):
    S, D = n.shape
    nk, _, wd = w.shape
    assert wd % 128 == 0
    tm = _row_tile(S, 512)
    q_groups = ATTN_WIDTH // 128
    scale = HEAD_DIM ** -0.5

    def body(n_ref, w_ref, cos_ref, sin_ref, o_ref):
        x = jnp.dot(n_ref[...], w_ref[...], preferred_element_type=F32)
        first = pl.program_id(0) * (wd // 128)
        c, s = cos_ref[...], sin_ref[...]
        for j in range(wd // 128):
            lanes = slice(j * 128, (j + 1) * 128)
            group = first + j
            xj = x[:, lanes]
            rot = _rotate(xj, c, s) * jnp.where(group < q_groups, scale, 1.0)
            o_ref[:, lanes] = jnp.where(group < 2 * q_groups, rot, xj)

    tab = pl.BlockSpec((tm, 128), lambda k, m: (m, 0))
    return pl.pallas_call(
        body, name=name, grid=(nk, S // tm),
        in_specs=[pl.BlockSpec((tm, D), lambda k, m: (m, 0)), pl.BlockSpec((None, D, wd), lambda k, m: (k, 0, 0)),
                  tab, tab],
        out_specs=pl.BlockSpec((tm, wd), lambda k, m: (m, k)),
        out_shape=jax.ShapeDtypeStruct((S, nk * wd), F32),
        compiler_params=_params(("parallel", "parallel")),
    )(n, w, cos, sin)


def _ffn_dact(df, wd, fga, fup, name, deps=()):
    S, D = df.shape
    nk, w, _ = wd.shape
    tm = _row_tile(S, 512)
    deps = _deps(deps)

    def body(df_ref, wd_ref, fga_ref, fup_ref, dga_ref, dup_ref):
        dact = lax.dot_general(df_ref[...], wd_ref[...], NT, preferred_element_type=F32)
        dga_ref[...] = (dact * fga_ref[...].astype(F32)).astype(BF16)
        dup_ref[...] = (dact * fup_ref[...].astype(F32)).astype(BF16)

    cspec = pl.BlockSpec((None, tm, w), lambda k, m: (k, m, 0))
    out = jax.ShapeDtypeStruct((nk, S, w), BF16)
    return pl.pallas_call(
        _blind_to(body, 4, len(deps)), name=name, grid=(nk, S // tm),
        in_specs=[pl.BlockSpec((tm, D), lambda k, m: (m, 0)), pl.BlockSpec((None, w, D), lambda k, m: (k, 0, 0)),
                  cspec, cspec] + [ANY_SPEC] * len(deps),
        out_specs=[cspec, cspec], out_shape=[out, out],
        compiler_params=_params(("parallel", "parallel")),
    )(df, wd, fga, fup, *deps)


def _mm_reduce(lhs_list, w_list, chunked3d, w_is_kd, name, deps=()):
    nk = w_list[0].shape[0]
    kc, D = w_list[0].shape[1:] if w_is_kd else w_list[0].shape[:0:-1]
    S = lhs_list[0].shape[1] if chunked3d else lhs_list[0].shape[0]
    tm = _row_tile(S, 512)
    npair = len(lhs_list)
    deps = _deps(deps)

    def body(*refs):
        l_refs, w_refs = refs[:npair], refs[npair:2 * npair]
        o_ref, acc_ref = refs[2 * npair], refs[2 * npair + 1]
        k = pl.program_id(1)

        @pl.when(k == 0)
        def _():
            acc_ref[...] = jnp.zeros_like(acc_ref)

        for l_ref, w_ref in zip(l_refs, w_refs):
            if w_is_kd:
                acc_ref[...] += jnp.dot(l_ref[...], w_ref[...], preferred_element_type=F32)
            else:
                acc_ref[...] += lax.dot_general(l_ref[...], w_ref[...], NT, preferred_element_type=F32)

        @pl.when(k == nk - 1)
        def _():
            o_ref[...] = acc_ref[...]

    if chunked3d:
        lspec = pl.BlockSpec((None, tm, kc), lambda m, k: (k, m, 0))
    else:
        lspec = pl.BlockSpec((tm, kc), lambda m, k: (m, k))
    wspec = pl.BlockSpec((None,) + tuple(w_list[0].shape[1:]), lambda m, k: (k, 0, 0))
    return pl.pallas_call(
        _blind_to(body, 2 * npair, len(deps)), name=name, grid=(S // tm, nk),
        in_specs=[lspec] * npair + [wspec] * npair + [ANY_SPEC] * len(deps),
        out_specs=pl.BlockSpec((tm, D), lambda m, k: (m, 0)),
        out_shape=jax.ShapeDtypeStruct((S, D), F32),
        scratch_shapes=[pltpu.VMEM((tm, D), F32)],
        compiler_params=_params(("parallel", "arbitrary")),
    )(*lhs_list, *w_list, *deps)


def _wgrad_chunk_lhs(lhs_list, rhs, name, deps=()):
    nk, S, w = lhs_list[0].shape
    D = rhs.shape[1]
    ts = _row_tile(S, 2048)
    ns = S // ts
    nl = len(lhs_list)
    deps = _deps(deps)

    def body(*refs):
        l_refs, r_ref = refs[:nl], refs[nl]
        o_refs, acc_refs = refs[nl + 1:2 * nl + 1], refs[2 * nl + 1:]
        s = pl.program_id(1)

        @pl.when(s == 0)
        def _():
            for acc_ref in acc_refs:
                acc_ref[...] = jnp.zeros_like(acc_ref)

        x = r_ref[...]
        for l_ref, acc_ref in zip(l_refs, acc_refs):
            acc_ref[...] += lax.dot_general(l_ref[...], x, TN, preferred_element_type=F32)

        @pl.when(s == ns - 1)
        def _():
            for o_ref, acc_ref in zip(o_refs, acc_refs):
                o_ref[...] = acc_ref[...]

    return pl.pallas_call(
        _blind_to(body, nl + 1, len(deps)), name=name, grid=(nk, ns),
        in_specs=[pl.BlockSpec((None, ts, w), lambda k, s: (k, s, 0))] * nl
        + [pl.BlockSpec((ts, D), lambda k, s: (s, 0))] + [ANY_SPEC] * len(deps),
        out_specs=[pl.BlockSpec((None, w, D), lambda k, s: (k, 0, 0))] * nl,
        out_shape=[jax.ShapeDtypeStruct((nk, w, D), F32)] * nl,
        scratch_shapes=[pltpu.VMEM((w, D), F32)] * nl,
        compiler_params=_params(("parallel", "arbitrary")),
    )(*lhs_list, rhs, *deps)


def _wgrad_chunk_rhs(lhs, rhs_list, nk, chunked3d, name, deps=()):
    S, D = lhs.shape
    w = rhs_list[0].shape[2] if chunked3d else rhs_list[0].shape[1] // nk
    ts = _row_tile(S, 2048)
    ns = S // ts
    nr = len(rhs_list)
    deps = _deps(deps)

    def body(*refs):
        l_ref, r_refs = refs[0], refs[1:1 + nr]
        o_refs, acc_refs = refs[1 + nr:1 + 2 * nr], refs[1 + 2 * nr:]
        s = pl.program_id(1)

        @pl.when(s == 0)
        def _():
            for acc_ref in acc_refs:
                acc_ref[...] = jnp.zeros_like(acc_ref)

        x = l_ref[...]
        for r_ref, acc_ref in zip(r_refs, acc_refs):
            acc_ref[...] += lax.dot_general(x, r_ref[...], TN, preferred_element_type=F32)

        @pl.when(s == ns - 1)
        def _():
            for o_ref, acc_ref in zip(o_refs, acc_refs):
                o_ref[...] = acc_ref[...]

    if chunked3d:
        rspec = pl.BlockSpec((None, ts, w), lambda k, s: (k, s, 0))
    else:
        rspec = pl.BlockSpec((ts, w), lambda k, s: (s, k))
    ospec = pl.BlockSpec((None, D, w), lambda k, s: (k, 0, 0))
    return pl.pallas_call(
        _blind_to(body, 1 + nr, len(deps)), name=name, grid=(nk, ns),
        in_specs=[pl.BlockSpec((ts, D), lambda k, s: (s, 0))] + [rspec] * nr + [ANY_SPEC] * len(deps),
        out_specs=[ospec] * nr,
        out_shape=[jax.ShapeDtypeStruct((nk, D, w), F32)] * nr,
        scratch_shapes=[pltpu.VMEM((D, w), F32)] * nr,
        compiler_params=_params(("parallel", "arbitrary")),
    )(lhs, *rhs_list, *deps)


def _rope_tables(S):
    pos = np.arange(S, dtype=np.float32)
    inv_freq = (ROPE_THETA ** (-np.arange(0, HEAD_DIM, 2, dtype=np.float32) / HEAD_DIM)).astype(np.float32)
    ang = (pos[:, None] * inv_freq[None, :]).astype(np.float64)
    cos, sin = np.cos(ang).astype(np.float32), np.sin(ang).astype(np.float32)
    cos2 = np.concatenate([cos, cos, cos, cos], axis=1)
    sin2 = np.concatenate([-sin, sin, -sin, sin], axis=1)
    return jnp.asarray(cos2), jnp.asarray(sin2)


def _rotate(t, cos, sin_signed):
    half = HEAD_DIM // 2
    lane = lax.broadcasted_iota(jnp.int32, t.shape, 1)
    first = (lane % HEAD_DIM) < half
    partner = jnp.where(first, pltpu.roll(t, 128 - half, 1), pltpu.roll(t, half, 1))
    return t * cos + partner * sin_signed


def _band_mask(T, has_prev):
    qi = lax.broadcasted_iota(jnp.int32, (T, 2 * T), 0)
    kj = lax.broadcasted_iota(jnp.int32, (T, 2 * T), 1)
    return ((kj < T) & (kj >= qi) & has_prev) | ((kj >= T) & (kj - T <= qi))


def _stack_heads(x, head0):
    zero = jnp.zeros_like(x)
    return jnp.concatenate([jnp.where(head0, x, zero), jnp.where(head0, zero, x)], axis=0)


def _branch_blocks(rows, dilation):
    T = min(ATTN_BLOCK, rows // dilation)
    return T, rows // T


def _block_rows(base, T, dilation):
    if dilation == 1:
        return pl.ds(pl.multiple_of(base, T), T)
    return pl.ds(base, T, stride=dilation)


def _qkv_specs(S):
    groups = ATTN_WIDTH // 128
    return [pl.BlockSpec((S, 128), lambda j, off=t * groups: (0, off + j)) for t in range(3)]


def _attn_fwd(proj, name):
    S, A = proj.shape[0], ATTN_WIDTH
    sup = min(S, SUPER_ROWS)
    nd = len(DILATIONS)
    assert S % sup == 0

    def body(q_ref, k_ref, v_ref, attn_ref, lse_ref, acc_s, m_s, l_s):
        lane = lax.broadcasted_iota(jnp.int32, (1, 128), 1)
        head0 = lane < HEAD_DIM

        def supertile(st, carry):
            row0 = st * sup
            for di, dil in enumerate(DILATIONS):
                T, nblk = _branch_blocks(sup, dil)
                span = T * dil
                assert T == ATTN_BLOCK or sup == S

                def block(idx, c2, di=di, dil=dil, T=T, span=span):
                    r = idx % dil
                    loc = (idx // dil) * span + r
                    base = row0 + loc
                    rows = _block_rows(base, T, dil)
                    prev = _block_rows(jnp.maximum(base - span, r), T, dil)
                    qb = q_ref[rows, :].astype(BF16)
                    k2 = jnp.concatenate([k_ref[prev, :], k_ref[rows, :]], axis=0).astype(BF16)
                    v2 = jnp.concatenate([v_ref[prev, :], v_ref[rows, :]], axis=0).astype(BF16)
                    valid = _band_mask(T, base >= span)
                    q2 = _stack_heads(qb, head0)
                    s = lax.dot_general(q2, k2, NT, preferred_element_type=F32)
                    s = jnp.where(jnp.concatenate([valid, valid], axis=0), s, NEG)
                    m = jnp.max(s, axis=-1, keepdims=True)
                    p = jnp.exp(s - m)
                    l = jnp.sum(p, axis=-1, keepdims=True)
                    acc = jnp.dot(p.astype(BF16), v2, preferred_element_type=F32)
                    lrows = _block_rows(di * sup + loc, T, dil)
                    acc_s[lrows, :] = jnp.where(head0, acc[:T], acc[T:])
                    m_s[lrows, :] = jnp.where(head0, m[:T], m[T:])
                    l_s[lrows, :] = jnp.where(head0, l[:T], l[T:])
                    return c2

                lax.fori_loop(0, nblk, block, 0, unroll=8)

            chunk = min(sup, 256)

            def merge(ci, c2):
                lr = [pl.ds(pl.multiple_of(di * sup + ci * chunk, chunk), chunk) for di in range(nd)]
                gr = pl.ds(pl.multiple_of(row0 + ci * chunk, chunk), chunk)
                m0, m1, m2 = m_s[lr[0], :], m_s[lr[1], :], m_s[lr[2], :]
                mm = jnp.maximum(jnp.maximum(m0, m1), m2)
                w0, w1, w2 = jnp.exp(m0 - mm), jnp.exp(m1 - mm), jnp.exp(m2 - mm)
                den = (w0 * l_s[lr[0], :] + w1 * l_s[lr[1], :]) + w2 * l_s[lr[2], :]
                num = (w0 * acc_s[lr[0], :] + w1 * acc_s[lr[1], :]) + w2 * acc_s[lr[2], :]
                attn_ref[gr, :] = num / den
                lse_ref[gr, :] = mm + jnp.log(den)
                return c2

            lax.fori_loop(0, sup // chunk, merge, 0)
            return carry

        lax.fori_loop(0, S // sup, supertile, 0)

    blk = pl.BlockSpec((S, 128), lambda j: (0, j))
    out = jax.ShapeDtypeStruct((S, A), F32)
    return pl.pallas_call(
        body, name=name, grid=(A // 128,),
        in_specs=_qkv_specs(S), out_specs=[blk, blk], out_shape=[out, out],
        scratch_shapes=[pltpu.VMEM((nd * sup, 128), F32)] * 3,
        compiler_params=_params(("parallel",)),
    )(proj, proj, proj)


def _attn_bwd(proj, da, lse, delta, name):
    S, A = da.shape

    def body(q_ref, k_ref, v_ref, da_ref, lse_ref, dl_ref, dq_ref, dk_ref, dv_ref):
        lane = lax.broadcasted_iota(jnp.int32, (1, 128), 1)
        head0 = lane < HEAD_DIM
        dq_ref[...] = jnp.zeros_like(dq_ref)
        dk_ref[...] = jnp.zeros_like(dk_ref)
        dv_ref[...] = jnp.zeros_like(dv_ref)
        for dil in DILATIONS:
            T, nblk = _branch_blocks(S, dil)
            span = T * dil

            def block(idx, carry, dil=dil, T=T, span=span):
                r = idx % dil
                base = (idx // dil) * span + r
                rows = _block_rows(base, T, dil)
                prev = _block_rows(jnp.maximum(base - span, r), T, dil)
                qb, dab = q_ref[rows, :].astype(BF16), da_ref[rows, :].astype(BF16)
                k2 = jnp.concatenate([k_ref[prev, :], k_ref[rows, :]], axis=0).astype(BF16)
                v2 = jnp.concatenate([v_ref[prev, :], v_ref[rows, :]], axis=0).astype(BF16)
                lse_b, dl_b = lse_ref[rows, :], dl_ref[rows, :]
                valid = _band_mask(T, base >= span)
                valid2 = jnp.concatenate([valid, valid], axis=0)
                q2, da2 = _stack_heads(qb, head0), _stack_heads(dab, head0)
                lse2 = jnp.concatenate([lse_b[:, 0:1], lse_b[:, HEAD_DIM:HEAD_DIM + 1]], axis=0)
                dl2 = jnp.concatenate([dl_b[:, 0:1], dl_b[:, HEAD_DIM:HEAD_DIM + 1]], axis=0)
                s = lax.dot_general(q2, k2, NT, preferred_element_type=F32)
                p = jnp.where(valid2, jnp.exp(s - lse2), 0.0)
                dp = lax.dot_general(da2, v2, NT, preferred_element_type=F32)
                ds = (p * (dp - dl2)).astype(BF16)
                dq2 = jnp.dot(ds, k2, preferred_element_type=F32)
                dk2 = lax.dot_general(ds, q2, TN, preferred_element_type=F32)
                dv2 = lax.dot_general(p.astype(BF16), da2, TN, preferred_element_type=F32)
                dq_ref[rows, :] += jnp.where(head0, dq2[:T], dq2[T:])
                dk_ref[rows, :] += dk2[T:]
                dv_ref[rows, :] += dv2[T:]
                dk_ref[prev, :] += dk2[:T]
                dv_ref[prev, :] += dv2[:T]
                return carry

            lax.fori_loop(0, nblk, block, 0, unroll=8)

    blk = pl.BlockSpec((S, 128), lambda j: (0, j))
    out = jax.ShapeDtypeStruct((S, A), F32)
    return pl.pallas_call(
        body, name=name, grid=(A // 128,),
        in_specs=_qkv_specs(S) + [blk] * 3, out_specs=[blk] * 3, out_shape=[out] * 3,
        compiler_params=_params(("parallel",)),
    )(proj, proj, proj, da, lse, delta)


SUBLANES = 8
CONV_CHUNK = 64
FIRST_TAP = HALO - (CONV_KERNEL - 1)


def _store_shifted(shift_s, win, rows):
    shift_s[0, pl.ds(0, rows), :] = win
    for b in range(1, SUBLANES):
        shift_s[b, pl.ds(0, rows - SUBLANES), :] = win[b:b + rows - SUBLANES, :]


def _glu_window(a_ref, b_ref, ah_ref, bh_ref, first):
    u0 = a_ref[...] * _sigmoid(b_ref[...])
    u0h = ah_ref[...] * _sigmoid(bh_ref[...])
    u0h = jnp.where(first, jnp.zeros_like(u0h), u0h)
    return jnp.concatenate([u0h, u0], axis=0)


def _conv_norms(u1, lng, lnb):
    mu = jnp.mean(u1, axis=-1, keepdims=True)
    xc = u1 - mu
    rstd = lax.rsqrt(jnp.mean(xc * xc, axis=-1, keepdims=True) + LN_EPS)
    u1h = xc * rstd
    u2 = u1h * lng + lnb
    sig = _sigmoid(u2)
    u3 = u2 * sig
    r = lax.rsqrt(jnp.mean(u3 * u3, axis=-1, keepdims=True) + RMS_EPS)
    return rstd, u1h, u2, sig, u3, r


def _conv_specs(tr, C, col_a, col_b):
    per = tr // HALO

    def tile(col):
        return pl.BlockSpec((tr, C), lambda i: (i, col))

    def halo(col):
        return pl.BlockSpec((HALO, C), lambda i: (jnp.maximum(i * per - 1, 0), col))

    return tile(col_a), tile(col_b), halo(col_a), halo(col_b)


def _mixer_merge(proj, attn, cw, cb, lng, lnb, gat, gco, name):
    S = proj.shape[0]
    C = CONV_WIDTH
    A = attn.shape[1]
    tr = _row_tile(S, 256)

    def body(a_ref, b_ref, ah_ref, bh_ref, at_ref, w_ref, cb_ref, lng_ref, lnb_ref, gat_ref, gco_ref, u1_ref, y_ref,
             shift_s):
        _store_shifted(shift_s, _glu_window(a_ref, b_ref, ah_ref, bh_ref, pl.program_id(0) == 0), tr + HALO)

        def chunk(rc, carry):
            r0 = pl.multiple_of(rc * CONV_CHUNK, CONV_CHUNK)
            for lb in range(C // 128):
                lanes = slice(lb * 128, (lb + 1) * 128)
                acc = jnp.broadcast_to(cb_ref[:, lanes], (CONV_CHUNK, 128))
                for j in range(CONV_KERNEL):
                    a8, b = divmod(FIRST_TAP + j, SUBLANES)
                    acc = acc + w_ref[j:j + 1, lanes] * shift_s[b, pl.ds(r0 + a8 * SUBLANES, CONV_CHUNK), lanes]
                u1_ref[pl.ds(r0, CONV_CHUNK), lanes] = acc
            return carry

        lax.fori_loop(0, tr // CONV_CHUNK, chunk, 0)
        _, _, _, _, u3, r = _conv_norms(u1_ref[...], lng_ref[...], lnb_ref[...])
        y_ref[:, A:] = ((u3 * r) * gco_ref[...]).astype(BF16)
        x = at_ref[...]
        ra = lax.rsqrt(jnp.mean(x * x, axis=-1, keepdims=True) + RMS_EPS)
        y_ref[:, :A] = ((x * ra) * gat_ref[...]).astype(BF16)

    ta, tb, ha, hb = _conv_specs(tr, C, 3, 4)
    row = pl.BlockSpec((tr, C), lambda i: (i, 0))
    vec = _vec_spec(C, 1)
    return pl.pallas_call(
        body, name=name, grid=(S // tr,),
        in_specs=[ta, tb, ha, hb, pl.BlockSpec((tr, A), lambda i: (i, 0)), pl.BlockSpec((HALO, C), lambda i: (0, 0)),
                  vec, vec, vec, _vec_spec(A, 1), vec],
        out_specs=[row, pl.BlockSpec((tr, A + C), lambda i: (i, 0))],
        out_shape=[jax.ShapeDtypeStruct((S, C), F32), jax.ShapeDtypeStruct((S, A + C), BF16)],
        scratch_shapes=[pltpu.VMEM((SUBLANES, tr + HALO, C), F32)],
        compiler_params=_params(("parallel",)),
    )(proj, proj, proj, proj, attn, cw, cb, lng, lnb, gat, gco)


def _mix_out_bwd(dmo, wout, attn, u1, gat, lng, lnb, gco, name):
    S, D = dmo.shape
    A, C = attn.shape[1], u1.shape[1]
    tr = _row_tile(S, 256)

    def body(dmo_ref, w_ref, a_ref, u1_ref, gat_ref, lng_ref, lnb_ref, gco_ref,
             da_ref, dl_ref, du1_ref, dgat_ref, dgco_ref, dlng_ref, dlnb_ref, dcb_ref):
        @pl.when(pl.program_id(0) == 0)
        def _():
            for ref in (dgat_ref, dgco_ref, dlng_ref, dlnb_ref, dcb_ref):
                ref[...] = jnp.zeros_like(ref)

        dy = lax.dot_general(dmo_ref[...], w_ref[...], NT, preferred_element_type=F32)
        dya, dyc = dy[:, :A], dy[:, A:]
        x = a_ref[...]
        r = lax.rsqrt(jnp.mean(x * x, axis=-1, keepdims=True) + RMS_EPS)
        xh = x * r
        dgat_ref[...] += jnp.sum(dya * xh, axis=0, keepdims=True)
        dxh = dya * gat_ref[...]
        dx = r * (dxh - xh * jnp.mean(dxh * xh, axis=-1, keepdims=True))
        da_ref[...] = dx
        hi = lax.broadcasted_iota(jnp.int32, (A, A), 0) // HEAD_DIM
        hj = lax.broadcasted_iota(jnp.int32, (A, A), 1) // HEAD_DIM
        same_head = (hi == hj).astype(F32)
        dl_ref[...] = jnp.dot(dx * x, same_head, preferred_element_type=F32, precision=lax.Precision.HIGHEST)

        lng = lng_ref[...]
        rstd, u1h, u2, sig, u3, rc = _conv_norms(u1_ref[...], lng, lnb_ref[...])
        u3h = u3 * rc
        dgco_ref[...] += jnp.sum(dyc * u3h, axis=0, keepdims=True)
        du3h = dyc * gco_ref[...]
        du3 = rc * (du3h - u3h * jnp.mean(du3h * u3h, axis=-1, keepdims=True))
        du2 = du3 * (sig * (1.0 + u2 * (1.0 - sig)))
        dlng_ref[...] += jnp.sum(du2 * u1h, axis=0, keepdims=True)
        dlnb_ref[...] += jnp.sum(du2, axis=0, keepdims=True)
        du1h = du2 * lng
        du1 = rstd * (du1h - jnp.mean(du1h, axis=-1, keepdims=True)
                      - u1h * jnp.mean(du1h * u1h, axis=-1, keepdims=True))
        du1_ref[...] = du1
        dcb_ref[...] += jnp.sum(du1, axis=0, keepdims=True)

    arow = pl.BlockSpec((tr, A), lambda i: (i, 0))
    crow = pl.BlockSpec((tr, C), lambda i: (i, 0))
    avec, cvec = _vec_spec(A, 1), _vec_spec(C, 1)
    return pl.pallas_call(
        body, name=name, grid=(S // tr,),
        in_specs=[pl.BlockSpec((tr, D), lambda i: (i, 0)), pl.BlockSpec((A + C, D), lambda i: (0, 0)), arow, crow,
                  avec, cvec, cvec, cvec],
        out_specs=[arow, arow, crow, avec, cvec, cvec, cvec, cvec],
        out_shape=[jax.ShapeDtypeStruct((S, A), F32)] * 2 + [jax.ShapeDtypeStruct((S, C), F32)]
        + [jax.ShapeDtypeStruct((1, A), F32)] + [jax.ShapeDtypeStruct((1, C), F32)] * 4,
        compiler_params=_params(("arbitrary",)),
    )(dmo, wout, attn, u1, gat, lng, lnb, gco)


def _dproj(du1, proj, cw, dq, dk, dv, cos, sin, name):
    S, C = du1.shape
    A = dq.shape[1]
    tr = _row_tile(S, 256)
    nt = S // tr
    per = tr // HALO
    scale = HEAD_DIM ** -0.5

    def body(du_ref, dun_ref, a_ref, b_ref, ah_ref, bh_ref, w_ref, dq_ref, dk_ref, dv_ref, cos_ref, sin_ref,
             dp_ref, dw_ref, win_s, dwin_s, du0_s, tap_s):
        i = pl.program_id(0)

        @pl.when(i == 0)
        def _():
            dw_ref[...] = jnp.zeros_like(dw_ref)

        _store_shifted(win_s, _glu_window(a_ref, b_ref, ah_ref, bh_ref, i == 0), tr + HALO)
        nxt = jnp.where(i == nt - 1, jnp.zeros_like(dun_ref[...]), dun_ref[...])
        _store_shifted(dwin_s, jnp.concatenate([du_ref[...], nxt], axis=0), tr + HALO)
        tap_s[...] = jnp.zeros_like(tap_s)

        def chunk(rc, carry):
            r0 = pl.multiple_of(rc * CONV_CHUNK, CONV_CHUNK)
            for lb in range(C // 128):
                lanes = slice(lb * 128, (lb + 1) * 128)
                du = du_ref[pl.ds(r0, CONV_CHUNK), lanes]
                acc = jnp.zeros((CONV_CHUNK, 128), F32)
                for j in range(CONV_KERNEL):
                    a8, b = divmod(CONV_KERNEL - 1 - j, SUBLANES)
                    acc = acc + w_ref[j:j + 1, lanes] * dwin_s[b, pl.ds(r0 + a8 * SUBLANES, CONV_CHUNK), lanes]
                    a8, b = divmod(FIRST_TAP + j, SUBLANES)
                    prod = du * win_s[b, pl.ds(r0 + a8 * SUBLANES, CONV_CHUNK), lanes]
                    part = prod[0:SUBLANES]
                    for g in range(1, CONV_CHUNK // SUBLANES):
                        part = part + prod[g * SUBLANES:(g + 1) * SUBLANES]
                    tap_s[j * SUBLANES:(j + 1) * SUBLANES, lanes] += part
                du0_s[pl.ds(r0, CONV_CHUNK), lanes] = acc
            return carry

        lax.fori_loop(0, tr // CONV_CHUNK, chunk, 0)
        taps = [jnp.sum(tap_s[j * SUBLANES:(j + 1) * SUBLANES, :], axis=0, keepdims=True)
                for j in range(CONV_KERNEL)]
        taps.append(jnp.zeros((HALO - CONV_KERNEL, C), F32))
        dw_ref[...] += jnp.concatenate(taps, axis=0)
        du0 = du0_s[...]
        a, sig = a_ref[...], _sigmoid(b_ref[...])
        dp_ref[:, 3 * A:3 * A + C] = (du0 * sig).astype(BF16)
        dp_ref[:, 3 * A + C:] = (du0 * a * sig * (1.0 - sig)).astype(BF16)
        cos_, nsin = cos_ref[...], -sin_ref[...]
        for j in range(A // 128):
            lanes = slice(j * 128, (j + 1) * 128)
            dp_ref[:, j * 128:(j + 1) * 128] = (_rotate(dq_ref[:, lanes], cos_, nsin) * scale).astype(BF16)
            dp_ref[:, A + j * 128:A + (j + 1) * 128] = _rotate(dk_ref[:, lanes], cos_, nsin).astype(BF16)
        dp_ref[:, 2 * A:3 * A] = dv_ref[...].astype(BF16)

    ta, tb, ha, hb = _conv_specs(tr, C, 3, 4)
    row = pl.BlockSpec((tr, C), lambda i: (i, 0))
    arow = pl.BlockSpec((tr, A), lambda i: (i, 0))
    tab = pl.BlockSpec((tr, 128), lambda i: (i, 0))
    nxt = pl.BlockSpec((HALO, C), lambda i: (jnp.minimum((i + 1) * per, S // HALO - 1), 0))
    wspec = pl.BlockSpec((HALO, C), lambda i: (0, 0))
    return pl.pallas_call(
        body, name=name, grid=(nt,),
        in_specs=[row, nxt, ta, tb, ha, hb, wspec, arow, arow, arow, tab, tab],
        out_specs=[pl.BlockSpec((tr, 3 * A + 2 * C), lambda i: (i, 0)), wspec],
        out_shape=[jax.ShapeDtypeStruct((S, 3 * A + 2 * C), BF16), jax.ShapeDtypeStruct((HALO, C), F32)],
        scratch_shapes=[pltpu.VMEM((SUBLANES, tr + HALO, C), F32), pltpu.VMEM((SUBLANES, tr + HALO, C), F32),
                        pltpu.VMEM((tr, C), F32), pltpu.VMEM((HALO * SUBLANES, C), F32)],
        compiler_params=_params(("arbitrary",)),
    )(du1, du1, proj, proj, proj, proj, cw, dq, dk, dv, cos, sin)


def _ada_fwd(c_all, w, b, name):
    B, D = c_all.shape
    N = w.shape[1]
    tn = 768 if N % 768 == 0 else N

    def body(c_ref, w_ref, b_ref, o_ref):
        c = c_ref[...]
        a = (c * _sigmoid(c)).astype(BF16)
        o_ref[...] = jnp.dot(a, w_ref[...].astype(BF16), preferred_element_type=F32) + b_ref[...]

    return pl.pallas_call(
        body, name=name, grid=(N // tn,),
        in_specs=[pl.BlockSpec((B, D), lambda j: (0, 0)), pl.BlockSpec((D, tn), lambda j: (0, j)),
                  pl.BlockSpec((1, tn), lambda j: (0, j))],
        out_specs=pl.BlockSpec((B, tn), lambda j: (0, j)),
        out_shape=jax.ShapeDtypeStruct((B, N), F32),
        compiler_params=_params(("parallel",)),
    )(c_all, w, b)


def _ada_wgrad(c_t, dmod, name):
    D, B = c_t.shape
    N = dmod.shape[1]
    tn = 768 if N % 768 == 0 else N

    def body(c_ref, d_ref, o_ref):
        c = c_ref[...]
        a = (c * _sigmoid(c)).astype(BF16)
        o_ref[...] = jnp.dot(a, d_ref[...].astype(BF16), preferred_element_type=F32)

    return pl.pallas_call(
        body, name=name, grid=(N // tn,),
        in_specs=[pl.BlockSpec((D, B), lambda j: (0, 0)), pl.BlockSpec((B, tn), lambda j: (0, j))],
        out_specs=pl.BlockSpec((D, tn), lambda j: (0, j)),
        out_shape=jax.ShapeDtypeStruct((D, N), F32),
        compiler_params=_params(("parallel",)),
    )(c_t, dmod)


def _sum_rows(x, name):
    R, N = x.shape

    def body(x_ref, o_ref):
        acc = x_ref[0:1, :]
        for r in range(1, R):
            acc = acc + x_ref[r:r + 1, :]
        o_ref[...] = acc

    return pl.pallas_call(
        body, name=name, out_shape=jax.ShapeDtypeStruct((1, N), F32),
        compiler_params=_params(),
    )(x)


def _adamw(w, g, m, v, name, deps=()):
    R, C = w.shape
    tr = _row_tile(R, 256) if R % 256 == 0 else R
    bc1 = 1.0 - ADAM_B1 ** ADAM_STEP
    bc2 = 1.0 - ADAM_B2 ** ADAM_STEP
    deps = _deps(deps)

    def body(w_ref, g_ref, m_ref, v_ref, d_ref, mo_ref, vo_ref, go_ref):
        g_ = g_ref[...]
        m_ = ADAM_B1 * m_ref[...] + (1.0 - ADAM_B1) * g_
        v_ = ADAM_B2 * v_ref[...] + (1.0 - ADAM_B2) * (g_ * g_)
        mo_ref[...] = m_
        vo_ref[...] = v_
        go_ref[...] = g_
        d_ref[...] = -ADAM_LR * ((m_ / bc1) / (jnp.sqrt(v_ / bc2) + ADAM_EPS) + ADAM_WD * w_ref[...])

    row = pl.BlockSpec((tr, C), lambda i: (i, 0))
    out = jax.ShapeDtypeStruct((R, C), F32)
    return pl.pallas_call(
        _blind_to(body, 4, len(deps)), name=name, grid=(R // tr,),
        in_specs=[row] * 4 + [ANY_SPEC] * len(deps), out_specs=[row] * 4, out_shape=[out] * 4,
        compiler_params=_params(("parallel",), streaming=True),
    )(w, g, m, v, *deps)


def _coords():
    return lax.axis_index("x"), lax.axis_index("y"), lax.axis_index("c")


def _all_gather8(x, name, deps=()):
    R, N = x.shape
    assert R == 8
    flips = [(fx, fy, fc) for fx in (0, 1) for fy in (0, 1) for fc in (0, 1)][1:]
    deps = _deps(deps)

    def body(x_ref, o_ref, send_sems, recv_sems):
        mx, my, mc = _coords()
        me = 4 * mx + 2 * my + mc

        def rows(dev):
            return o_ref.at[pl.ds(pl.multiple_of(dev * R, R), R), :]

        o_ref[pl.ds(pl.multiple_of(me * R, R), R), :] = x_ref[...]
        copies = []
        for t, (fx, fy, fc) in enumerate(flips):
            peer = (mx ^ fx, my ^ fy, mc ^ fc)
            copies.append(pltpu.make_async_remote_copy(
                src_ref=x_ref, dst_ref=rows(me), send_sem=send_sems.at[t], recv_sem=recv_sems.at[t],
                device_id=peer, device_id_type=MESH))
        for cp in copies:
            cp.start()
        for t, (fx, fy, fc) in enumerate(flips):
            peer_id = 4 * (mx ^ fx) + 2 * (my ^ fy) + (mc ^ fc)
            pltpu.make_async_remote_copy(
                src_ref=x_ref, dst_ref=rows(peer_id), send_sem=send_sems.at[t], recv_sem=recv_sems.at[t],
                device_id=(mx ^ fx, my ^ fy, mc ^ fc), device_id_type=MESH).wait_recv()
        for cp in copies:
            cp.wait_send()

    return pl.pallas_call(
        _blind_to(body, 1, len(deps)), name=name,
        in_specs=[pl.BlockSpec(memory_space=pltpu.VMEM)] + [ANY_SPEC] * len(deps),
        out_specs=pl.BlockSpec(memory_space=pltpu.VMEM),
        out_shape=jax.ShapeDtypeStruct((N_DEV * R, N), F32),
        scratch_shapes=[pltpu.SemaphoreType.DMA((7,)), pltpu.SemaphoreType.DMA((7,))],
        compiler_params=pltpu.CompilerParams(has_side_effects=True, vmem_limit_bytes=VMEM_LIMIT_BYTES),
    )(x, *deps)


def _half_rows(rows, half):
    return pl.ds(pl.multiple_of(half * (rows // 2), 8), rows // 2)


def _split_start(bufs, plan, n, name):
    nb = len(bufs)

    def body(*refs):
        send_sems, recv_sems, token = refs[nb], refs[nb + 1], refs[-1]
        for t, (src, dst, dev) in enumerate(plan(refs[:nb])):
            pltpu.make_async_remote_copy(src_ref=src, dst_ref=dst, send_sem=send_sems.at[t],
                                         recv_sem=recv_sems.at[t], device_id=dev, device_id_type=MESH).start()
        token[...] = jnp.zeros_like(token)

    out = pl.pallas_call(
        body, name=name,
        out_shape=(pltpu.SemaphoreType.DMA((n,)), pltpu.SemaphoreType.DMA((n,)),
                   *[pltpu.HBM(b.shape, b.dtype) for b in bufs], jax.ShapeDtypeStruct((8, 128), F32)),
        in_specs=[HBM_SPEC] * nb,
        out_specs=(SEM_SPEC, SEM_SPEC, *[HBM_SPEC] * nb, pl.BlockSpec(memory_space=pltpu.VMEM)),
        input_output_aliases={i: 2 + i for i in range(nb)},
        compiler_params=pltpu.CompilerParams(has_side_effects=DATAFLOW),
    )(*[pltpu.with_memory_space_constraint(b, pltpu.HBM) for b in bufs])
    return out[0], out[1], list(out[2:2 + nb]), out[-1]


def _split_wait(bufs, send_sems, recv_sems, plan, after, name):
    nb = len(bufs)
    after = _deps(after)

    def body(*refs):
        ss, rs = refs[nb], refs[nb + 1]
        for t, (src, dst, dev) in enumerate(plan(refs[:nb])):
            cp = pltpu.make_async_remote_copy(src_ref=src, dst_ref=dst, send_sem=ss.at[t], recv_sem=rs.at[t],
                                              device_id=dev, device_id_type=MESH)
            cp.wait_send()
            cp.wait_recv()

    out = pl.pallas_call(
        body, name=name,
        out_shape=tuple(pltpu.HBM(b.shape, b.dtype) for b in bufs),
        in_specs=[HBM_SPEC] * nb + [SEM_SPEC, SEM_SPEC] + [ANY_SPEC] * len(after),
        out_specs=tuple([HBM_SPEC] * nb),
        input_output_aliases={i: i for i in range(nb)},
        compiler_params=pltpu.CompilerParams(has_side_effects=DATAFLOW),
    )(*bufs, send_sems, recv_sems, *after)
    return list(out)


class _Exchange:
    def __init__(self, bufs, plan, n, name):
        self.plan, self.name = plan, name
        self.send_sems, self.recv_sems, self.bufs, self.token = _split_start(bufs, plan, n, name + "_start")

    def wait(self, after):
        return _split_wait(self.bufs, self.send_sems, self.recv_sems, self.plan, after, self.name + "_wait")


def _cast_place(w, chip_idx, name, deps=()):
    R, C = w.shape
    tr = _row_tile(R, 256) if R % 256 == 0 else R
    deps = _deps(deps)

    def body(k_ref, w_ref, o_ref):
        o_ref[...] = w_ref[...].astype(BF16)

    grid_spec = pltpu.PrefetchScalarGridSpec(
        num_scalar_prefetch=1, grid=(R // tr,),
        in_specs=[pl.BlockSpec((tr, C), lambda i, k: (i, 0))] + [ANY_SPEC] * len(deps),
        out_specs=pl.BlockSpec((None, tr, C), lambda i, k: (k[0], i, 0)),
    )
    return pl.pallas_call(
        _blind_to(body, 2, len(deps)), name=name, grid_spec=grid_spec,
        out_shape=jax.ShapeDtypeStruct((N_CHIPS, R, C), BF16),
        compiler_params=_params(("parallel",), streaming=True),
    )(chip_idx, w, *deps)


def _plan_gather(refs):
    mx, my, mc = _coords()
    me = 2 * mx + my
    plan = []
    for g in refs:
        mine = g.at[me, _half_rows(g.shape[1], mc), :]
        for fx, fy in XY_FLIPS:
            plan.append((mine, mine, (mx ^ fx, my ^ fy, mc)))
    return plan


def _plan_swap(refs):
    mx, my, mc = _coords()
    plan = []
    for g in refs:
        for fx, fy in XY_FLIPS:
            have = g.at[2 * (mx ^ fx) + (my ^ fy), _half_rows(g.shape[1], mc), :]
            plan.append((have, have, (mx, my, 1 - mc)))
    return plan


def _plan_other_halves(refs):
    n = len(refs) // 2
    mx, my, mc = _coords()
    return [(g.at[pl.ds(0, N_CHIPS), _half_rows(g.shape[1], 1 - mc), :], land, (mx, my, 1 - mc))
            for g, land in zip(refs[:n], refs[n:])]


def _plan_chunks(refs):
    n = len(refs) // 2
    mx, my, mc = _coords()
    plan = []
    for s, land in zip(refs[:n], refs[n:]):
        for t, (fx, fy) in enumerate(XY_FLIPS):
            plan.append((s.at[2 * (mx ^ fx) + (my ^ fy)], land.at[t], (mx ^ fx, my ^ fy, mc)))
    return plan


def _plan_share(refs):
    mx, my, mc = _coords()
    return [(full.at[mc], full.at[mc], (mx, my, 1 - mc)) for full in refs]


def _add_half(g, recv, core_idx, name):
    nk, R, C = g.shape
    rh = R // 2
    tr = _row_tile(rh, 128) if rh % 128 == 0 else rh
    nt = rh // tr

    def body(c_ref, g_ref, r_ref, o_ref):
        o_ref[...] = (g_ref[...] + r_ref[...]).astype(BF16)

    grid_spec = pltpu.PrefetchScalarGridSpec(
        num_scalar_prefetch=1, grid=(nk, nt),
        in_specs=[pl.BlockSpec((None, tr, C), lambda k, i, c: (k, c[0] * nt + i, 0)),
                  pl.BlockSpec((None, tr, C), lambda k, i, c: (k, i, 0))],
        out_specs=pl.BlockSpec((None, tr, C), lambda k, i, c: (k, i, 0)),
    )
    return pl.pallas_call(
        body, name=name, grid_spec=grid_spec, out_shape=jax.ShapeDtypeStruct((nk, rh, C), BF16),
        compiler_params=_params(("parallel", "parallel"), streaming=True),
    )(core_idx, g, recv)


def _sum_chips(s, land, chip_core, name):
    _, rh, C = s.shape
    tr = _row_tile(rh, 128) if rh % 128 == 0 else rh

    def body(p_ref, s_ref, l_ref, o_ref):
        me = p_ref[0]
        acc = None
        for j in range(N_CHIPS):
            t = jnp.maximum(jnp.bitwise_xor(me, j) - 1, 0)
            term = jnp.where(me == j, s_ref[...], l_ref[t]).astype(F32)
            acc = term if acc is None else acc + term
        o_ref[...] = acc

    grid_spec = pltpu.PrefetchScalarGridSpec(
        num_scalar_prefetch=1, grid=(rh // tr,),
        in_specs=[pl.BlockSpec((None, tr, C), lambda i, p: (p[0], i, 0)),
                  pl.BlockSpec((3, tr, C), lambda i, p: (0, i, 0))],
        out_specs=pl.BlockSpec((None, tr, C), lambda i, p: (p[1], i, 0)),
    )
    return pl.pallas_call(
        body, name=name, grid_spec=grid_spec, out_shape=jax.ShapeDtypeStruct((2, rh, C), F32),
        compiler_params=_params(("parallel",), streaming=True),
    )(chip_core, s, land)


def _rs_send_halves(grads, tag):
    lands = [lax.empty((g.shape[0], g.shape[1] // 2, g.shape[2]), g.dtype) for g in grads]
    return _Exchange(list(grads) + lands, _plan_other_halves, len(grads), f"rs_halves_{tag}")


def _rs_send_chunks(ex, after, core_idx, tag):
    bufs = ex.wait(after)
    n = len(bufs) // 2
    sums = [_add_half(g, r, core_idx, f"rs_add_{tag}_{i}") for i, (g, r) in enumerate(zip(bufs[:n], bufs[n:]))]
    lands = [lax.empty((3,) + s.shape[1:], s.dtype) for s in sums]
    return _Exchange(sums + lands, _plan_chunks, 3 * n, f"rs_chunks_{tag}")


def _rs_send_share(ex, after, chip_core, tag):
    bufs = ex.wait(after)
    n = len(bufs) // 2
    fulls = [_sum_chips(s, l, chip_core, f"rs_sum_{tag}_{i}") for i, (s, l) in enumerate(zip(bufs[:n], bufs[n:]))]
    return _Exchange(fulls, _plan_share, n, f"rs_share_{tag}")


def _rs_finish(ex, after):
    return [b.reshape(2 * b.shape[1], b.shape[2]) for b in ex.wait(after)]


def _ffn_backward(dh_out, df, saved, gain, sc, wg, wu, wd, core_idx, tag, prev=None, last=False):
    h, n, ga, up, act, _ = saved
    (dwd,) = _wgrad_chunk_lhs([act], df, f"{tag}_dwd")
    ex_d = _rs_send_halves([dwd], f"{tag}_d")
    dga, dup = _ffn_dact(df, wd, ga, up, f"{tag}_dact", deps=[ex_d.token])
    ex_d = _rs_send_chunks(ex_d, [dga], core_idx, f"{tag}_d")
    dwg, dwu = _wgrad_chunk_lhs([dga, dup], n, f"{tag}_dwgu", deps=[ex_d.token])
    ex_gu = _rs_send_halves([dwg, dwu], f"{tag}_gu")
    dn = _mm_reduce([dga, dup], [wg, wu], True, True, f"{tag}_dn", deps=[ex_gu.token])
    if not last:
        ex_gu = _rs_send_chunks(ex_gu, [dn], core_idx, f"{tag}_gu")
    outs = _norm_mod_bwd(dn, h, gain, sc, dh_out, f"{tag}_norm_bwd", prev=prev, deps=[ex_gu.token])
    return outs, (ex_d, ex_gu)


def _pad_cols(v, n):
    return jnp.pad(v, ((0, 0), (0, n - v.shape[1])))


def _mixer_forward(h1, n2, gt2, win, wout, conv_w, conv_dw_b, conv_ln_g, conv_ln_b, attn_out_g, conv_out_g,
                   next_norm=None):
    S, D = h1.shape
    cos, sin = _rope_tables(S)
    proj = _mm_cols_rope(n2, win, cos, sin, "mix_in")
    attn, lse = _attn_fwd(proj, "attn_fwd")
    u1, y = _mixer_merge(proj, attn, conv_w, conv_dw_b, conv_ln_g, conv_ln_b, attn_out_g, conv_out_g, "mix_merge")
    h2, mo, *n_next = _mm_residual(y[None], wout.reshape(1, D, D), h1, gt2, 1.0, "mix_out", next_norm=next_norm)
    return (h2, *n_next), (h1, n2, proj, cos, sin, attn, lse, u1, y, mo)


def _mixer_backward(dh2, dmo, saved, mix_norm_g, sc2, win, wout, conv_w, conv_ln_g, conv_ln_b, attn_out_g,
                    conv_out_g, core_idx, prev=None):
    h1, n2, proj, cos, sin, attn, lse, u1, y, _ = saved
    S, D = h1.shape
    (dwout,) = _wgrad_chunk_lhs([y[None]], dmo, "mix_dwout")
    dattn, delta, du1, d_attn_g, d_gco, d_lng, d_lnb, d_cb = _mix_out_bwd(
        dmo, wout.reshape(D, D), attn, u1, attn_out_g, conv_ln_g, conv_ln_b, conv_out_g, "mix_out_bwd")
    dq, dk, dv = _attn_bwd(proj, dattn, lse, delta, "attn_bwd")
    dproj, d_cw = _dproj(du1, proj, conv_w, dq, dk, dv, cos, sin, "mix_dproj")
    (dwin,) = _wgrad_chunk_rhs(n2, [dproj], N_CHIPS, False, "mix_dwin")
    ex = _rs_send_halves([dwin, dwout.reshape(N_CHIPS, D // N_CHIPS, D)], "mix")
    dn2 = _mm_reduce([dproj], [win], False, False, "mix_dn", deps=[ex.token])
    ex = _rs_send_chunks(ex, [dn2], core_idx, "mix")
    outs = _norm_mod_bwd(dn2, h1, mix_norm_g, sc2, dh2, "mix_norm_bwd", prev=prev, deps=[ex.token])
    return outs, ex, (d_cb, d_lng, d_lnb, d_attn_g, d_gco, d_cw)


def kernel(x, c, w_ada, b_ada, ffn1_norm_g, ffn1_w_gate, ffn1_w_up, ffn1_w_down, mix_norm_g, w_in, conv_dw_w, conv_dw_b, conv_ln_g, conv_ln_b, attn_out_g, conv_out_g, w_out, ffn2_norm_g, ffn2_w_gate, ffn2_w_up, ffn2_w_down, final_norm_g, loss_target, m_w_ada, m_b_ada, m_ffn1_norm_g, m_ffn1_w_gate, m_ffn1_w_up, m_ffn1_w_down, m_mix_norm_g, m_w_in, m_conv_dw_w, m_conv_dw_b, m_conv_ln_g, m_conv_ln_b, m_attn_out_g, m_conv_out_g, m_w_out, m_ffn2_norm_g, m_ffn2_w_gate, m_ffn2_w_up, m_ffn2_w_down, m_final_norm_g, v_w_ada, v_b_ada, v_ffn1_norm_g, v_ffn1_w_gate, v_ffn1_w_up, v_ffn1_w_down, v_mix_norm_g, v_w_in, v_conv_dw_w, v_conv_dw_b, v_conv_ln_g, v_conv_ln_b, v_attn_out_g, v_conv_out_g, v_w_out, v_ffn2_norm_g, v_ffn2_w_gate, v_ffn2_w_up, v_ffn2_w_down, v_final_norm_g):
    S, D = x.shape[1], x.shape[2]
    mx, my, mc = _coords()
    chip = 2 * mx + my
    dev = 4 * mx + 2 * my + mc
    chip_idx = chip.astype(jnp.int32).reshape(1)
    core_idx = mc.astype(jnp.int32).reshape(1)
    chip_core = jnp.stack([chip, mc]).astype(jnp.int32)
    h0 = x[0]
    target = loss_target[0]

    ncw = CONV_KERNEL * 128
    n0 = -(-(D + ncw) // 1024) * 1024
    pk0 = _pad_cols(jnp.concatenate([c.reshape(1, D), conv_dw_w.reshape(1, ncw)], axis=1), n0)
    g0 = _all_gather8(pk0.reshape(8, n0 // 8), "gather_c").reshape(N_DEV, n0)
    c_all = g0[:, :D]
    conv_w = jnp.concatenate([g0[2 * kc, D:D + ncw].reshape(CONV_KERNEL, 128) for kc in range(N_CHIPS)], axis=1)
    conv_w = jnp.pad(conv_w, ((0, HALO - CONV_KERNEL), (0, 0)))
    nmod = w_ada.shape[2]
    b_shard = lax.dynamic_slice(b_ada, (0, chip * nmod), (1, nmod))
    mod_part = _ada_fwd(c_all, w_ada[0], b_shard, "ada_fwd")
    g1 = _all_gather8(mod_part, "gather_mod")
    mod_all = jnp.concatenate([g1[16 * kc:16 * kc + 8] for kc in range(N_CHIPS)], axis=1)
    mod = lax.dynamic_slice(mod_all, (dev, 0), (1, 9 * D))
    sh1, sc1, gt1, sh2, sc2, gt2, sh3, sc3, gt3 = [mod[:, i * D:(i + 1) * D] for i in range(9)]

    def gather_start(ws, tag, dep):
        slots = [_cast_place(w, chip_idx, f"cast_{tag}_{i}", deps=[dep]) for i, w in enumerate(ws)]
        return _Exchange(slots, _plan_gather, 3 * len(ws), f"gather_{tag}")

    def swap_start(ex, after, tag):
        return _Exchange(ex.wait(after), _plan_swap, 3 * len(ex.bufs), f"swap_{tag}")

    ex_gu1 = gather_start([ffn1_w_gate[0].T, ffn1_w_up[0].T], "ffn1_gu", g1)
    n1 = _norm_mod(h0, ffn1_norm_g, sc1, sh1, "ffn1_norm", deps=[ex_gu1.token])
    ex_d1 = gather_start([ffn1_w_down[0]], "ffn1_d", n1)
    ex_wm = gather_start([w_in[0], w_out[0]], "mix", ex_d1.token)
    ex_w2 = gather_start([ffn2_w_gate[0].T, ffn2_w_up[0].T, ffn2_w_down[0]], "ffn2", ex_wm.token)

    wg1, wu1 = swap_start(ex_gu1, [ex_w2.token], "ffn1_gu").wait([])
    ga1, up1, act1 = _ffn_gate_up(n1, wg1, wu1, "ffn1_gate_up")
    (wd1,) = swap_start(ex_d1, [act1], "ffn1_d").wait([])
    ex_wm = swap_start(ex_wm, [wd1], "mix")
    h1, f1, n2 = _mm_residual(act1, wd1, h0, gt1, 0.5, "ffn1_down", next_norm=(mix_norm_g, sc2, sh2))
    saved1 = (h0, n1, ga1, up1, act1, f1)
    win, wout = ex_wm.wait([h1])
    ex_w2 = swap_start(ex_w2, [h1], "ffn2")
    (h2, n3), saved2 = _mixer_forward(h1, n2, gt2, win, wout, conv_w, conv_dw_b, conv_ln_g, conv_ln_b,
                                      attn_out_g, conv_out_g, next_norm=(ffn2_norm_g, sc3, sh3))
    wg2, wu2, wd2 = ex_w2.wait([h2])
    fga3, fup3, act3 = _ffn_gate_up(n3, wg2, wu2, "ffn2_gate_up")
    loss_part, dh3, d_final_g, df3, d_gt3 = _mm_residual_loss(
        act3, wd2, h2, gt3, 0.5, final_norm_g.reshape(1, D), target, "ffn2_down_loss")
    saved3 = (h2, n3, fga3, fup3, act3, None)

    (dh2, d_sh3, d_sc3, d_gain3, dmo, d_gt2), (ex_d2, ex_gu2) = _ffn_backward(
        dh3, df3, saved3, ffn2_norm_g, sc3, wg2, wu2, wd2, core_idx, "ffn2", prev=(saved2[-1], gt2, 1.0))
    (dh1, d_sh2, d_sc2, d_gain2, df1, d_gt1), ex_mix, small_mix = _mixer_backward(
        dh2, dmo, saved2, mix_norm_g, sc2, win, wout, conv_w, conv_ln_g, conv_ln_b, attn_out_g, conv_out_g, core_idx,
        prev=(f1, gt1, 0.5))
    d_cb, d_lng, d_lnb, d_attn_g, d_gco, d_cw = small_mix
    (dh0, d_sh1, d_sc1, d_gain1), (ex_d1, ex_gu1) = _ffn_backward(
        dh1, df1, saved1, ffn1_norm_g, sc1, wg1, wu1, wd1, core_idx, "ffn1", last=True)

    dmod = jnp.concatenate([d_sh1, d_sc1, d_gt1, d_sh2, d_sc2, d_gt2, d_sh3, d_sc3, d_gt3], axis=1)
    small = [d_gain1, d_gain2, d_gain3, d_final_g, d_cb, d_lng, d_lnb, d_attn_g, d_gco,
             d_cw[:CONV_KERNEL].reshape(1, CONV_KERNEL * CONV_WIDTH), loss_part]
    pk1 = jnp.concatenate([dmod] + small, axis=1)
    n1_ = -(-pk1.shape[1] // 1024) * 1024
    gathered = _all_gather8(_pad_cols(pk1, n1_).reshape(8, n1_ // 8), "gather_small").reshape(N_DEV, n1_)
    ex_gu1 = _rs_send_chunks(ex_gu1, [gathered], core_idx, "ffn1_gu")
    tot = _sum_rows(gathered, "sum_small")
    off = [0]

    def take(nel):
        out = tot[:, off[0]:off[0] + nel]
        off[0] += nel
        return out

    g_b_ada = take(9 * D)
    g_ffn1_norm, g_mix_norm, g_ffn2_norm, g_final = take(D), take(D), take(D), take(D)
    g_cb, g_lng, g_lnb, g_attn_g, g_gco = take(512), take(512), take(512), take(512), take(512)
    g_cw_full = take(CONV_KERNEL * CONV_WIDTH).reshape(CONV_KERNEL, CONV_WIDTH)
    loss = take(128)[0, 0]
    g_cw = lax.dynamic_slice(g_cw_full, (0, chip * 128), (CONV_KERNEL, 128))

    dmod_shard = lax.dynamic_slice(gathered[:, :9 * D], (0, chip * nmod), (N_DEV, nmod))
    dmod16 = jnp.pad(dmod_shard, ((0, N_DEV), (0, 0)))
    c_t16 = jnp.pad(c_all.T, ((0, 0), (0, N_DEV)))
    g_w_ada = _ada_wgrad(c_t16, dmod16, "ada_wgrad")

    names = ["w_ada", "b_ada", "ffn1_norm_g", "ffn1_w_gate", "ffn1_w_up", "ffn1_w_down", "mix_norm_g", "w_in",
             "conv_dw_w", "conv_dw_b", "conv_ln_g", "conv_ln_b", "attn_out_g", "conv_out_g", "w_out", "ffn2_norm_g",
             "ffn2_w_gate", "ffn2_w_up", "ffn2_w_down", "final_norm_g"]
    weights = dict(zip(names, [w_ada, b_ada, ffn1_norm_g, ffn1_w_gate, ffn1_w_up, ffn1_w_down, mix_norm_g, w_in,
                               conv_dw_w, conv_dw_b, conv_ln_g, conv_ln_b, attn_out_g, conv_out_g, w_out,
                               ffn2_norm_g, ffn2_w_gate, ffn2_w_up, ffn2_w_down, final_norm_g]))
    ms = dict(zip(names, [m_w_ada, m_b_ada, m_ffn1_norm_g, m_ffn1_w_gate, m_ffn1_w_up, m_ffn1_w_down, m_mix_norm_g,
                          m_w_in, m_conv_dw_w, m_conv_dw_b, m_conv_ln_g, m_conv_ln_b, m_attn_out_g, m_conv_out_g,
                          m_w_out, m_ffn2_norm_g, m_ffn2_w_gate, m_ffn2_w_up, m_ffn2_w_down, m_final_norm_g]))
    vs = dict(zip(names, [v_w_ada, v_b_ada, v_ffn1_norm_g, v_ffn1_w_gate, v_ffn1_w_up, v_ffn1_w_down, v_mix_norm_g,
                          v_w_in, v_conv_dw_w, v_conv_dw_b, v_conv_ln_g, v_conv_ln_b, v_attn_out_g, v_conv_out_g,
                          v_w_out, v_ffn2_norm_g, v_ffn2_w_gate, v_ffn2_w_up, v_ffn2_w_down, v_final_norm_g]))
    grads, deltas, new_ms, new_vs = {}, {}, {}, {}

    def adamw_big(nm, g2d, deps=(), transposed=False):
        shape = weights[nm].shape
        two_d = (shape[-2], shape[-1])

        def view(t):
            return t.reshape(two_d).T if transposed else t.reshape(two_d)

        d_, m_, v_, g_ = _adamw(view(weights[nm]), g2d, view(ms[nm]), view(vs[nm]), f"adamw_{nm}", deps=deps)
        grads[nm], deltas[nm], new_ms[nm], new_vs[nm] = (
            (t.T if transposed else t).reshape(shape) for t in (g_, d_, m_, v_))
        return d_

    d_ada = adamw_big("w_ada", g_w_ada, deps=[ex_gu1.token])
    small_grads = {"b_ada": g_b_ada, "ffn1_norm_g": g_ffn1_norm, "mix_norm_g": g_mix_norm, "conv_dw_w": g_cw,
                   "conv_dw_b": g_cb, "conv_ln_g": g_lng, "conv_ln_b": g_lnb, "attn_out_g": g_attn_g,
                   "conv_out_g": g_gco, "ffn2_norm_g": g_ffn2_norm, "final_norm_g": g_final}
    small_names = [nm for nm in names if nm in small_grads]

    def pack_small(arrs):
        flat = jnp.concatenate([arrs[nm].reshape(1, -1) for nm in small_names], axis=1)
        npad = -(-flat.shape[1] // 1024) * 1024
        return _pad_cols(flat, npad).reshape(8, npad // 8)

    d_s, m_s, v_s, _ = _adamw(pack_small(weights), pack_small(small_grads), pack_small(ms), pack_small(vs),
                           "adamw_small")
    pos = 0
    for nm in small_names:
        shape, nel = weights[nm].shape, weights[nm].size
        grads[nm] = small_grads[nm].reshape(shape)
        deltas[nm], new_ms[nm], new_vs[nm] = (t.reshape(1, -1)[:, pos:pos + nel].reshape(shape)
                                              for t in (d_s, m_s, v_s))
        pos += nel

    ex_d2 = _rs_send_share(ex_d2, [d_ada, d_s], chip_core, "ffn2_d")
    ex_gu2 = _rs_send_share(ex_gu2, [ex_d2.token], chip_core, "ffn2_gu")
    ex_mix = _rs_send_share(ex_mix, [ex_gu2.token], chip_core, "mix")
    ex_d1 = _rs_send_share(ex_d1, [ex_mix.token], chip_core, "ffn1_d")
    (g_wd2,) = _rs_finish(ex_d2, [ex_d1.token])
    last = [adamw_big("ffn2_w_down", g_wd2)]
    g_wg2, g_wu2 = _rs_finish(ex_gu2, last)
    last = [adamw_big("ffn2_w_gate", g_wg2, transposed=True), adamw_big("ffn2_w_up", g_wu2, transposed=True)]
    g_win, g_wout = _rs_finish(ex_mix, last)
    last = [adamw_big("w_in", g_win), adamw_big("w_out", g_wout)]
    (g_wd1,) = _rs_finish(ex_d1, last)
    last = [adamw_big("ffn1_w_down", g_wd1)]
    ex_gu1 = _rs_send_share(ex_gu1, last, chip_core, "ffn1_gu")
    g_wg1, g_wu1 = _rs_finish(ex_gu1, [])
    adamw_big("ffn1_w_gate", g_wg1, transposed=True)
    adamw_big("ffn1_w_up", g_wu1, transposed=True)

    return (loss, dh0[None], *[grads[nm] for nm in names], *[deltas[nm] for nm in names],
            *[new_ms[nm] for nm in names], *[new_vs[nm] for nm in names])
```

```python
import jax
import jax.numpy as jnp
import numpy as np
from jax import lax
from jax.experimental import pallas as pl
from jax.experimental.pallas import tpu as pltpu

F32 = jnp.float32
BF16 = jnp.bfloat16
MESH = pl.DeviceIdType.MESH

RMS_EPS = 1e-6
LN_EPS = 1e-5
HEAD_DIM = 64
ATTN_WIDTH = 512
CONV_WIDTH = 512
ATTN_BLOCK = 128
DILATIONS = (1, 4, 16)
SUPER_ROWS = ATTN_BLOCK * 16
ROPE_THETA = 10000.0
CONV_KERNEL = 31
HALO = 32
N_CHIPS = 4
N_DEV = 8
ADAM_LR, ADAM_B1, ADAM_B2, ADAM_EPS, ADAM_WD, ADAM_STEP = 0.001, 0.9, 0.999, 1e-08, 0.01, 10
VMEM_LIMIT_BYTES = 48 * 1024 * 1024
VMEM_LIMIT_STREAMING = 62 * 1024 * 1024
NEG = -1e30

NT = (((1,), (1,)), ((), ()))
TN = (((0,), (0,)), ((), ()))

ANY_SPEC = pl.BlockSpec(memory_space=pl.ANY)
HBM_SPEC = pl.BlockSpec(memory_space=pltpu.HBM)
SEM_SPEC = pl.BlockSpec(memory_space=pltpu.SEMAPHORE)
DATAFLOW = pltpu.SideEffectType.DATAFLOW_SIDE_EFFECTING
XY_FLIPS = ((0, 1), (1, 0), (1, 1))


def _params(sem=None, streaming=False):
    return pltpu.CompilerParams(dimension_semantics=sem,
                                vmem_limit_bytes=VMEM_LIMIT_STREAMING if streaming else VMEM_LIMIT_BYTES)


def _row_tile(rows, want):
    t = min(rows, want)
    assert rows % t == 0
    return t


def _sigmoid(x):
    return 1.0 / (1.0 + jnp.exp(-x))


def _deps(deps):
    return [d for d in deps if d is not None]


def _blind_to(body, n_in, n_dep):
    def wrapped(*refs):
        return body(*refs[:n_in], *refs[n_in + n_dep:])
    return wrapped


def _vec_spec(d, ngrid):
    if ngrid == 1:
        return pl.BlockSpec((1, d), lambda i: (0, 0))
    return pl.BlockSpec((1, d), lambda i, j: (0, 0))


def _norm_mod(h, gain, sc, sh, name, deps=()):
    S, D = h.shape
    tr = _row_tile(S, 512)
    deps = _deps(deps)

    def body(h_ref, g_ref, sc_ref, sh_ref, n_ref):
        x = h_ref[...]
        r = lax.rsqrt(jnp.mean(x * x, axis=-1, keepdims=True) + RMS_EPS)
        y = (x * r) * g_ref[...]
        n_ref[...] = (y * (1.0 + sc_ref[...]) + sh_ref[...]).astype(BF16)

    row = pl.BlockSpec((tr, D), lambda i: (i, 0))
    return pl.pallas_call(
        _blind_to(body, 4, len(deps)), name=name, grid=(S // tr,),
        in_specs=[row, _vec_spec(D, 1), _vec_spec(D, 1), _vec_spec(D, 1)] + [ANY_SPEC] * len(deps),
        out_specs=row, out_shape=jax.ShapeDtypeStruct((S, D), BF16),
        compiler_params=_params(("parallel",)),
    )(h, gain, sc, sh, *deps)


def _norm_mod_bwd(dn, h_in, gain, sc, dh_out, name, prev=None, deps=()):
    S, D = h_in.shape
    tr = _row_tile(S, 512)
    deps = _deps(deps)
    n_in = 5 if prev is None else 7

    def body(*refs):
        dn_ref, h_ref, g_ref, sc_ref, dho_ref = refs[:5]
        dh_ref, dsh_ref, dsc_ref, dg_ref = refs[n_in:n_in + 4]

        @pl.when(pl.program_id(0) == 0)
        def _():
            for ref in refs[n_in + 1:n_in + 4] + refs[n_in + 5:]:
                ref[...] = jnp.zeros_like(ref)

        x = h_ref[...]
        dn_ = dn_ref[...]
        g = g_ref[...]
        one_sc = 1.0 + sc_ref[...]
        r = lax.rsqrt(jnp.mean(x * x, axis=-1, keepdims=True) + RMS_EPS)
        xh = x * r
        dsh_ref[...] += jnp.sum(dn_, axis=0, keepdims=True)
        dsc_ref[...] += jnp.sum(dn_ * (xh * g), axis=0, keepdims=True)
        dg_ref[...] += jnp.sum(dn_ * one_sc * xh, axis=0, keepdims=True)
        dxh = dn_ * (g * one_sc)
        dh = dho_ref[...] + r * (dxh - xh * jnp.mean(dxh * xh, axis=-1, keepdims=True))
        dh_ref[...] = dh
        if prev is not None:
            _gate_back(dh, refs[5], refs[6], prev[2], refs[n_in + 4], refs[n_in + 5])

    row = pl.BlockSpec((tr, D), lambda i: (i, 0))
    vec = _vec_spec(D, 1)
    extra_in = [] if prev is None else [row, vec]
    extra_out = [] if prev is None else [row, vec]
    extra_shape = [] if prev is None else [jax.ShapeDtypeStruct((S, D), BF16), jax.ShapeDtypeStruct((1, D), F32)]
    return pl.pallas_call(
        _blind_to(body, n_in, len(deps)), name=name, grid=(S // tr,),
        in_specs=[row, row, vec, vec, row] + extra_in + [ANY_SPEC] * len(deps),
        out_specs=[row, vec, vec, vec] + extra_out,
        out_shape=[jax.ShapeDtypeStruct((S, D), F32)] + [jax.ShapeDtypeStruct((1, D), F32)] * 3 + extra_shape,
        compiler_params=_params(("arbitrary",)),
    )(dn, h_in, gain, sc, dh_out, *([] if prev is None else prev[:2]), *deps)


def _gate_back(dh, f_ref, gate_ref, coef, df_ref, dgate_ref):
    df_ref[...] = ((coef * gate_ref[...]) * dh).astype(BF16)
    dgate_ref[...] += jnp.sum(coef * dh * f_ref[...].astype(F32), axis=0, keepdims=True)


def _mm_residual_loss(lhs, w, h_in, gate, coef, gain, target, name):
    nk, S, kc = lhs.shape
    D = w.shape[2]
    tm = _row_tile(S, 512)

    def body(l_ref, w_ref, h_ref, gate_ref, g_ref, t_ref, loss_ref, dh_ref, dg_ref, df_ref, dgate_ref):
        @pl.when(pl.program_id(0) == 0)
        def _():
            loss_ref[...] = jnp.zeros_like(loss_ref)
            dg_ref[...] = jnp.zeros_like(dg_ref)
            dgate_ref[...] = jnp.zeros_like(dgate_ref)

        f = _chunk_dots(l_ref, w_ref, nk)
        cg = coef * gate_ref[...]
        x = h_ref[...] + cg * f
        g = g_ref[...]
        r = lax.rsqrt(jnp.mean(x * x, axis=-1, keepdims=True) + RMS_EPS)
        xh = x * r
        err = xh * g - t_ref[...]
        part = 0.5 * jnp.sum(jnp.mean(err * err, axis=-1, keepdims=True), axis=0, keepdims=True)
        loss_ref[...] += jnp.broadcast_to(part, loss_ref.shape)
        dy = err * (1.0 / D)
        dg_ref[...] += jnp.sum(dy * xh, axis=0, keepdims=True)
        dxh = dy * g
        dh = r * (dxh - xh * jnp.mean(dxh * xh, axis=-1, keepdims=True))
        dh_ref[...] = dh
        df_ref[...] = (cg * dh).astype(BF16)
        dgate_ref[...] += jnp.sum(coef * dh * f, axis=0, keepdims=True)

    row = pl.BlockSpec((tm, D), lambda m: (m, 0))
    vec = _vec_spec(D, 1)
    return pl.pallas_call(
        body, name=name, grid=(S // tm,),
        in_specs=[pl.BlockSpec((nk, tm, kc), lambda m: (0, m, 0)),
                  pl.BlockSpec((nk, kc, D), lambda m: (0, 0, 0)), row, vec, vec, row],
        out_specs=[pl.BlockSpec((1, 128), lambda m: (0, 0)), row, vec, row, vec],
        out_shape=[jax.ShapeDtypeStruct((1, 128), F32), jax.ShapeDtypeStruct((S, D), F32),
                   jax.ShapeDtypeStruct((1, D), F32), jax.ShapeDtypeStruct((S, D), BF16),
                   jax.ShapeDtypeStruct((1, D), F32)],
        compiler_params=_params(("arbitrary",)),
    )(lhs, w, h_in, gate, gain, target)


def _ffn_gate_up(n, wg_t, wu_t, name):
    S, D = n.shape
    nk, w, _ = wg_t.shape
    tm = _row_tile(S, 512)

    def body(n_ref, wg_ref, wu_ref, dga_ref, dup_ref, act_ref):
        x = n_ref[...]
        ga = lax.dot_general(x, wg_ref[...], NT, preferred_element_type=F32)
        up = lax.dot_general(x, wu_ref[...], NT, preferred_element_type=F32)
        sig = _sigmoid(ga)
        silu = ga * sig
        dga_ref[...] = (up * (sig * (1.0 + ga * (1.0 - sig)))).astype(BF16)
        dup_ref[...] = silu.astype(BF16)
        act_ref[...] = (silu * up).astype(BF16)

    wspec = pl.BlockSpec((None, w, D), lambda k, m: (k, 0, 0))
    ospec = pl.BlockSpec((None, tm, w), lambda k, m: (k, m, 0))
    out = jax.ShapeDtypeStruct((nk, S, w), BF16)
    return pl.pallas_call(
        body, name=name, grid=(nk, S // tm),
        in_specs=[pl.BlockSpec((tm, D), lambda k, m: (m, 0)), wspec, wspec],
        out_specs=[ospec, ospec, ospec], out_shape=[out, out, out],
        compiler_params=_params(("parallel", "parallel")),
    )(n, wg_t, wu_t)


def _chunk_dots(l_ref, w_ref, nk):
    f = jnp.dot(l_ref[0], w_ref[0], preferred_element_type=F32)
    for k in range(1, nk):
        f = f + jnp.dot(l_ref[k], w_ref[k], preferred_element_type=F32)
    return f


def _mm_residual(lhs, w, h_in, gvec, coef, name, next_norm=None):
    nk, S, kc = lhs.shape
    D = w.shape[2]
    tm = _row_tile(S, 512)
    n_in = 4 if next_norm is None else 7

    def body(*refs):
        l_ref, w_ref, h_ref, g_ref = refs[:4]
        ho_ref, f_ref = refs[n_in:n_in + 2]
        f = _chunk_dots(l_ref, w_ref, nk)
        f_ref[...] = f.astype(BF16)
        x = h_ref[...] + (coef * g_ref[...]) * f
        ho_ref[...] = x
        if next_norm is not None:
            ng_ref, sc_ref, sh_ref = refs[4:7]
            r = lax.rsqrt(jnp.mean(x * x, axis=-1, keepdims=True) + RMS_EPS)
            y = (x * r) * ng_ref[...]
            refs[n_in + 2][...] = (y * (1.0 + sc_ref[...]) + sh_ref[...]).astype(BF16)

    row = pl.BlockSpec((tm, D), lambda m: (m, 0))
    vec = _vec_spec(D, 1)
    with_n = next_norm is not None
    return pl.pallas_call(
        body, name=name, grid=(S // tm,),
        in_specs=[pl.BlockSpec((nk, tm, kc), lambda m: (0, m, 0)),
                  pl.BlockSpec((nk, kc, D), lambda m: (0, 0, 0)), row, vec] + [vec] * (3 * with_n),
        out_specs=[row, row] + [row] * with_n,
        out_shape=[jax.ShapeDtypeStruct((S, D), F32), jax.ShapeDtypeStruct((S, D), BF16)]
        + [jax.ShapeDtypeStruct((S, D), BF16)] * with_n,
        compiler_params=_params(("parallel",)),
    )(lhs, w, h_in, gvec, *(next_norm or ()))


def _mm_cols_rope(n, w, cos, sin, name):
    S, D = n.shape
    nk, _, wd = w.shape
    assert wd % 128 == 0
    tm = _row_tile(S, 512)
    q_groups = ATTN_WIDTH // 128
    scale = HEAD_DIM ** -0.5

    def body(n_ref, w_ref, cos_ref, sin_ref, o_ref):
        x = jnp.dot(n_ref[...], w_ref[...], preferred_element_type=F32)
        first = pl.program_id(0) * (wd // 128)
        c, s = cos_ref[...], sin_ref[...]
        for j in range(wd // 128):
            lanes = slice(j * 128, (j + 1) * 128)
            group = first + j
            xj = x[:, lanes]
            rot = _rotate(xj, c, s) * jnp.where(group < q_groups, scale, 1.0)
            o_ref[:, lanes] = jnp.where(group < 2 * q_groups, rot, xj)

    tab = pl.BlockSpec((tm, 128), lambda k, m: (m, 0))
    return pl.pallas_call(
        body, name=name, grid=(nk, S // tm),
        in_specs=[pl.BlockSpec((tm, D), lambda k, m: (m, 0)), pl.BlockSpec((None, D, wd), lambda k, m: (k, 0, 0)),
                  tab, tab],
        out_specs=pl.BlockSpec((tm, wd), lambda k, m: (m, k)),
        out_shape=jax.ShapeDtypeStruct((S, nk * wd), F32),
        compiler_params=_params(("parallel", "parallel")),
    )(n, w, cos, sin)


def _ffn_dact(df, wd, fga, fup, name, deps=()):
    S, D = df.shape
    nk, w, _ = wd.shape
    tm = _row_tile(S, 512)
    deps = _deps(deps)

    def body(df_ref, wd_ref, fga_ref, fup_ref, dga_ref, dup_ref):
        dact = lax.dot_general(df_ref[...], wd_ref[...], NT, preferred_element_type=F32)
        dga_ref[...] = (dact * fga_ref[...].astype(F32)).astype(BF16)
        dup_ref[...] = (dact * fup_ref[...].astype(F32)).astype(BF16)

    cspec = pl.BlockSpec((None, tm, w), lambda k, m: (k, m, 0))
    out = jax.ShapeDtypeStruct((nk, S, w), BF16)
    return pl.pallas_call(
        _blind_to(body, 4, len(deps)), name=name, grid=(nk, S // tm),
        in_specs=[pl.BlockSpec((tm, D), lambda k, m: (m, 0)), pl.BlockSpec((None, w, D), lambda k, m: (k, 0, 0)),
                  cspec, cspec] + [ANY_SPEC] * len(deps),
        out_specs=[cspec, cspec], out_shape=[out, out],
        compiler_params=_params(("parallel", "parallel")),
    )(df, wd, fga, fup, *deps)


def _mm_reduce(lhs_list, w_list, chunked3d, w_is_kd, name, deps=()):
    nk = w_list[0].shape[0]
    kc, D = w_list[0].shape[1:] if w_is_kd else w_list[0].shape[:0:-1]
    S = lhs_list[0].shape[1] if chunked3d else lhs_list[0].shape[0]
    tm = _row_tile(S, 256)
    npair = len(lhs_list)
    deps = _deps(deps)

    def body(*refs):
        l_refs, w_refs, o_ref = refs[:npair], refs[npair:2 * npair], refs[2 * npair]
        acc = None
        for l_ref, w_ref in zip(l_refs, w_refs):
            for k in range(nk):
                lk = l_ref[k] if chunked3d else l_ref[:, k * kc:(k + 1) * kc]
                if w_is_kd:
                    part = jnp.dot(lk, w_ref[k], preferred_element_type=F32)
                else:
                    part = lax.dot_general(lk, w_ref[k], NT, preferred_element_type=F32)
                acc = part if acc is None else acc + part
        o_ref[...] = acc

    if chunked3d:
        lspec = pl.BlockSpec((nk, tm, kc), lambda m: (0, m, 0))
    else:
        lspec = pl.BlockSpec((tm, nk * kc), lambda m: (m, 0))
    wspec = pl.BlockSpec(tuple(w_list[0].shape), lambda m: (0, 0, 0))
    return pl.pallas_call(
        _blind_to(body, 2 * npair, len(deps)), name=name, grid=(S // tm,),
        in_specs=[lspec] * npair + [wspec] * npair + [ANY_SPEC] * len(deps),
        out_specs=pl.BlockSpec((tm, D), lambda m: (m, 0)),
        out_shape=jax.ShapeDtypeStruct((S, D), F32),
        compiler_params=_params(("parallel",)),
    )(*lhs_list, *w_list, *deps)


def _wgrad_chunk_lhs(lhs_list, rhs, name, deps=()):
    nk, S, w = lhs_list[0].shape
    D = rhs.shape[1]
    ts = _row_tile(S, 2048)
    ns = S // ts
    nl = len(lhs_list)
    deps = _deps(deps)

    def body(*refs):
        l_refs, r_ref = refs[:nl], refs[nl]
        o_refs, acc_refs = refs[nl + 1:2 * nl + 1], refs[2 * nl + 1:]
        s = pl.program_id(1)

        @pl.when(s == 0)
        def _():
            for acc_ref in acc_refs:
                acc_ref[...] = jnp.zeros_like(acc_ref)

        x = r_ref[...]
        for l_ref, acc_ref in zip(l_refs, acc_refs):
            acc_ref[...] += lax.dot_general(l_ref[...], x, TN, preferred_element_type=F32)

        @pl.when(s == ns - 1)
        def _():
            for o_ref, acc_ref in zip(o_refs, acc_refs):
                o_ref[...] = acc_ref[...]

    return pl.pallas_call(
        _blind_to(body, nl + 1, len(deps)), name=name, grid=(nk, ns),
        in_specs=[pl.BlockSpec((None, ts, w), lambda k, s: (k, s, 0))] * nl
        + [pl.BlockSpec((ts, D), lambda k, s: (s, 0))] + [ANY_SPEC] * len(deps),
        out_specs=[pl.BlockSpec((None, w, D), lambda k, s: (k, 0, 0))] * nl,
        out_shape=[jax.ShapeDtypeStruct((nk, w, D), F32)] * nl,
        scratch_shapes=[pltpu.VMEM((w, D), F32)] * nl,
        compiler_params=_params(("parallel", "arbitrary")),
    )(*lhs_list, rhs, *deps)


def _wgrad_chunk_rhs(lhs, rhs_list, nk, chunked3d, name, deps=()):
    S, D = lhs.shape
    w = rhs_list[0].shape[2] if chunked3d else rhs_list[0].shape[1] // nk
    ts = _row_tile(S, 2048)
    ns = S // ts
    nr = len(rhs_list)
    deps = _deps(deps)

    def body(*refs):
        l_ref, r_refs = refs[0], refs[1:1 + nr]
        o_refs, acc_refs = refs[1 + nr:1 + 2 * nr], refs[1 + 2 * nr:]
        s = pl.program_id(1)

        @pl.when(s == 0)
        def _():
            for acc_ref in acc_refs:
                acc_ref[...] = jnp.zeros_like(acc_ref)

        x = l_ref[...]
        for r_ref, acc_ref in zip(r_refs, acc_refs):
            acc_ref[...] += lax.dot_general(x, r_ref[...], TN, preferred_element_type=F32)

        @pl.when(s == ns - 1)
        def _():
            for o_ref, acc_ref in zip(o_refs, acc_refs):
                o_ref[...] = acc_ref[...]

    if chunked3d:
        rspec = pl.BlockSpec((None, ts, w), lambda k, s: (k, s, 0))
    else:
        rspec = pl.BlockSpec((ts, w), lambda k, s: (s, k))
    ospec = pl.BlockSpec((None, D, w), lambda k, s: (k, 0, 0))
    return pl.pallas_call(
        _blind_to(body, 1 + nr, len(deps)), name=name, grid=(nk, ns),
        in_specs=[pl.BlockSpec((ts, D), lambda k, s: (s, 0))] + [rspec] * nr + [ANY_SPEC] * len(deps),
        out_specs=[ospec] * nr,
        out_shape=[jax.ShapeDtypeStruct((nk, D, w), F32)] * nr,
        scratch_shapes=[pltpu.VMEM((D, w), F32)] * nr,
        compiler_params=_params(("parallel", "arbitrary")),
    )(lhs, *rhs_list, *deps)


def _rope_tables(S):
    pos = np.arange(S, dtype=np.float32)
    inv_freq = (ROPE_THETA ** (-np.arange(0, HEAD_DIM, 2, dtype=np.float32) / HEAD_DIM)).astype(np.float32)
    ang = (pos[:, None] * inv_freq[None, :]).astype(np.float64)
    cos, sin = np.cos(ang).astype(np.float32), np.sin(ang).astype(np.float32)
    cos2 = np.concatenate([cos, cos, cos, cos], axis=1)
    sin2 = np.concatenate([-sin, sin, -sin, sin], axis=1)
    return jnp.asarray(cos2), jnp.asarray(sin2)


def _rotate(t, cos, sin_signed):
    half = HEAD_DIM // 2
    lane = lax.broadcasted_iota(jnp.int32, t.shape, 1)
    first = (lane % HEAD_DIM) < half
    partner = jnp.where(first, pltpu.roll(t, 128 - half, 1), pltpu.roll(t, half, 1))
    return t * cos + partner * sin_signed


def _band_mask(T, has_prev):
    qi = lax.broadcasted_iota(jnp.int32, (T, 2 * T), 0)
    kj = lax.broadcasted_iota(jnp.int32, (T, 2 * T), 1)
    return ((kj < T) & (kj >= qi) & has_prev) | ((kj >= T) & (kj - T <= qi))


def _stack_heads(x, head0):
    zero = jnp.zeros_like(x)
    return jnp.concatenate([jnp.where(head0, x, zero), jnp.where(head0, zero, x)], axis=0)


def _branch_blocks(rows, dilation):
    T = min(ATTN_BLOCK, rows // dilation)
    return T, rows // T


def _block_rows(base, T, dilation):
    if dilation == 1:
        return pl.ds(pl.multiple_of(base, T), T)
    return pl.ds(base, T, stride=dilation)


def _qkv_specs(S):
    groups = ATTN_WIDTH // 128
    return [pl.BlockSpec((S, 128), lambda j, off=t * groups: (0, off + j)) for t in range(3)]


def _attn_fwd(proj, name):
    S, A = proj.shape[0], ATTN_WIDTH
    sup = min(S, SUPER_ROWS)
    nd = len(DILATIONS)
    assert S % sup == 0

    def body(q_ref, k_ref, v_ref, attn_ref, lse_ref, acc_s, m_s, l_s):
        lane = lax.broadcasted_iota(jnp.int32, (1, 128), 1)
        head0 = lane < HEAD_DIM

        def supertile(st, carry):
            row0 = st * sup
            for di, dil in enumerate(DILATIONS):
                T, nblk = _branch_blocks(sup, dil)
                span = T * dil
                assert T == ATTN_BLOCK or sup == S

                def block(idx, c2, di=di, dil=dil, T=T, span=span):
                    r = idx % dil
                    loc = (idx // dil) * span + r
                    base = row0 + loc
                    rows = _block_rows(base, T, dil)
                    prev = _block_rows(jnp.maximum(base - span, r), T, dil)
                    qb = q_ref[rows, :].astype(BF16)
                    k2 = jnp.concatenate([k_ref[prev, :], k_ref[rows, :]], axis=0).astype(BF16)
                    v2 = jnp.concatenate([v_ref[prev, :], v_ref[rows, :]], axis=0).astype(BF16)
                    valid = _band_mask(T, base >= span)
                    q2 = _stack_heads(qb, head0)
                    s = lax.dot_general(q2, k2, NT, preferred_element_type=F32)
                    s = jnp.where(jnp.concatenate([valid, valid], axis=0), s, NEG)
                    m = jnp.max(s, axis=-1, keepdims=True)
                    p = jnp.exp(s - m)
                    l = jnp.sum(p, axis=-1, keepdims=True)
                    acc = jnp.dot(p.astype(BF16), v2, preferred_element_type=F32)
                    lrows = _block_rows(di * sup + loc, T, dil)
                    acc_s[lrows, :] = jnp.where(head0, acc[:T], acc[T:])
                    m_s[lrows, :] = jnp.where(head0, m[:T], m[T:])
                    l_s[lrows, :] = jnp.where(head0, l[:T], l[T:])
                    return c2

                lax.fori_loop(0, nblk, block, 0, unroll=8)

            chunk = min(sup, 256)

            def merge(ci, c2):
                lr = [pl.ds(pl.multiple_of(di * sup + ci * chunk, chunk), chunk) for di in range(nd)]
                gr = pl.ds(pl.multiple_of(row0 + ci * chunk, chunk), chunk)
                m0, m1, m2 = m_s[lr[0], :], m_s[lr[1], :], m_s[lr[2], :]
                mm = jnp.maximum(jnp.maximum(m0, m1), m2)
                w0, w1, w2 = jnp.exp(m0 - mm), jnp.exp(m1 - mm), jnp.exp(m2 - mm)
                den = (w0 * l_s[lr[0], :] + w1 * l_s[lr[1], :]) + w2 * l_s[lr[2], :]
                num = (w0 * acc_s[lr[0], :] + w1 * acc_s[lr[1], :]) + w2 * acc_s[lr[2], :]
                attn_ref[gr, :] = num / den
                lse_ref[gr, :] = mm + jnp.log(den)
                return c2

            lax.fori_loop(0, sup // chunk, merge, 0)
            return carry

        lax.fori_loop(0, S // sup, supertile, 0)

    blk = pl.BlockSpec((S, 128), lambda j: (0, j))
    out = jax.ShapeDtypeStruct((S, A), F32)
    return pl.pallas_call(
        body, name=name, grid=(A // 128,),
        in_specs=_qkv_specs(S), out_specs=[blk, blk], out_shape=[out, out],
        scratch_shapes=[pltpu.VMEM((nd * sup, 128), F32)] * 3,
        compiler_params=_params(("parallel",)),
    )(proj, proj, proj)


def _attn_bwd(proj, da, lse, delta, name):
    S, A = da.shape

    def body(q_ref, k_ref, v_ref, da_ref, lse_ref, dl_ref, dq_ref, dk_ref, dv_ref):
        lane = lax.broadcasted_iota(jnp.int32, (1, 128), 1)
        head0 = lane < HEAD_DIM
        dq_ref[...] = jnp.zeros_like(dq_ref)
        dk_ref[...] = jnp.zeros_like(dk_ref)
        dv_ref[...] = jnp.zeros_like(dv_ref)
        for dil in DILATIONS:
            T, nblk = _branch_blocks(S, dil)
            span = T * dil

            def block(idx, carry, dil=dil, T=T, span=span):
                r = idx % dil
                base = (idx // dil) * span + r
                rows = _block_rows(base, T, dil)
                prev = _block_rows(jnp.maximum(base - span, r), T, dil)
                qb, dab = q_ref[rows, :].astype(BF16), da_ref[rows, :].astype(BF16)
                k2 = jnp.concatenate([k_ref[prev, :], k_ref[rows, :]], axis=0).astype(BF16)
                v2 = jnp.concatenate([v_ref[prev, :], v_ref[rows, :]], axis=0).astype(BF16)
                lse_b, dl_b = lse_ref[rows, :], dl_ref[rows, :]
                valid = _band_mask(T, base >= span)
                valid2 = jnp.concatenate([valid, valid], axis=0)
                q2, da2 = _stack_heads(qb, head0), _stack_heads(dab, head0)
                lse2 = jnp.concatenate([lse_b[:, 0:1], lse_b[:, HEAD_DIM:HEAD_DIM + 1]], axis=0)
                dl2 = jnp.concatenate([dl_b[:, 0:1], dl_b[:, HEAD_DIM:HEAD_DIM + 1]], axis=0)
                s = lax.dot_general(q2, k2, NT, preferred_element_type=F32)
                p = jnp.where(valid2, jnp.exp(s - lse2), 0.0)
                dp = lax.dot_general(da2, v2, NT, preferred_element_type=F32)
                ds = (p * (dp - dl2)).astype(BF16)
                dq2 = jnp.dot(ds, k2, preferred_element_type=F32)
                dk2 = lax.dot_general(ds, q2, TN, preferred_element_type=F32)
                dv2 = lax.dot_general(p.astype(BF16), da2, TN, preferred_element_type=F32)
                dq_ref[rows, :] += jnp.where(head0, dq2[:T], dq2[T:])
                dk_ref[rows, :] += dk2[T:]
                dv_ref[rows, :] += dv2[T:]
                dk_ref[prev, :] += dk2[:T]
                dv_ref[prev, :] += dv2[:T]
                return carry

            lax.fori_loop(0, nblk, block, 0, unroll=8)

    blk = pl.BlockSpec((S, 128), lambda j: (0, j))
    out = jax.ShapeDtypeStruct((S, A), F32)
    return pl.pallas_call(
        body, name=name, grid=(A // 128,),
        in_specs=_qkv_specs(S) + [blk] * 3, out_specs=[blk] * 3, out_shape=[out] * 3,
        compiler_params=_params(("parallel",)),
    )(proj, proj, proj, da, lse, delta)


SUBLANES = 8
CONV_CHUNK = 64
FIRST_TAP = HALO - (CONV_KERNEL - 1)


def _store_shifted(shift_s, win, rows):
    shift_s[0, pl.ds(0, rows), :] = win
    for b in range(1, SUBLANES):
        shift_s[b, pl.ds(0, rows - SUBLANES), :] = win[b:b + rows - SUBLANES, :]


def _glu_window(a_ref, b_ref, ah_ref, bh_ref, first):
    u0 = a_ref[...] * _sigmoid(b_ref[...])
    u0h = ah_ref[...] * _sigmoid(bh_ref[...])
    u0h = jnp.where(first, jnp.zeros_like(u0h), u0h)
    return jnp.concatenate([u0h, u0], axis=0)


def _conv_norms(u1, lng, lnb):
    mu = jnp.mean(u1, axis=-1, keepdims=True)
    xc = u1 - mu
    rstd = lax.rsqrt(jnp.mean(xc * xc, axis=-1, keepdims=True) + LN_EPS)
    u1h = xc * rstd
    u2 = u1h * lng + lnb
    sig = _sigmoid(u2)
    u3 = u2 * sig
    r = lax.rsqrt(jnp.mean(u3 * u3, axis=-1, keepdims=True) + RMS_EPS)
    return rstd, u1h, u2, sig, u3, r


def _conv_specs(tr, C, col_a, col_b):
    per = tr // HALO

    def tile(col):
        return pl.BlockSpec((tr, C), lambda i: (i, col))

    def halo(col):
        return pl.BlockSpec((HALO, C), lambda i: (jnp.maximum(i * per - 1, 0), col))

    return tile(col_a), tile(col_b), halo(col_a), halo(col_b)


def _mixer_merge(proj, attn, cw, cb, lng, lnb, gat, gco, name):
    S = proj.shape[0]
    C = CONV_WIDTH
    A = attn.shape[1]
    tr = _row_tile(S, 256)

    def body(a_ref, b_ref, ah_ref, bh_ref, at_ref, w_ref, cb_ref, lng_ref, lnb_ref, gat_ref, gco_ref, u1_ref, y_ref,
             shift_s):
        _store_shifted(shift_s, _glu_window(a_ref, b_ref, ah_ref, bh_ref, pl.program_id(0) == 0), tr + HALO)

        def chunk(rc, carry):
            r0 = pl.multiple_of(rc * CONV_CHUNK, CONV_CHUNK)
            for lb in range(C // 128):
                lanes = slice(lb * 128, (lb + 1) * 128)
                acc = jnp.broadcast_to(cb_ref[:, lanes], (CONV_CHUNK, 128))
                for j in range(CONV_KERNEL):
                    a8, b = divmod(FIRST_TAP + j, SUBLANES)
                    acc = acc + w_ref[j:j + 1, lanes] * shift_s[b, pl.ds(r0 + a8 * SUBLANES, CONV_CHUNK), lanes]
                u1_ref[pl.ds(r0, CONV_CHUNK), lanes] = acc
            return carry

        lax.fori_loop(0, tr // CONV_CHUNK, chunk, 0)
        _, _, _, _, u3, r = _conv_norms(u1_ref[...], lng_ref[...], lnb_ref[...])
        y_ref[:, A:] = ((u3 * r) * gco_ref[...]).astype(BF16)
        x = at_ref[...]
        ra = lax.rsqrt(jnp.mean(x * x, axis=-1, keepdims=True) + RMS_EPS)
        y_ref[:, :A] = ((x * ra) * gat_ref[...]).astype(BF16)

    ta, tb, ha, hb = _conv_specs(tr, C, 3, 4)
    row = pl.BlockSpec((tr, C), lambda i: (i, 0))
    vec = _vec_spec(C, 1)
    return pl.pallas_call(
        body, name=name, grid=(S // tr,),
        in_specs=[ta, tb, ha, hb, pl.BlockSpec((tr, A), lambda i: (i, 0)), pl.BlockSpec((HALO, C), lambda i: (0, 0)),
                  vec, vec, vec, _vec_spec(A, 1), vec],
        out_specs=[row, pl.BlockSpec((tr, A + C), lambda i: (i, 0))],
        out_shape=[jax.ShapeDtypeStruct((S, C), F32), jax.ShapeDtypeStruct((S, A + C), BF16)],
        scratch_shapes=[pltpu.VMEM((SUBLANES, tr + HALO, C), F32)],
        compiler_params=_params(("parallel",)),
    )(proj, proj, proj, proj, attn, cw, cb, lng, lnb, gat, gco)


def _mix_out_bwd(dmo, wout, attn, u1, gat, lng, lnb, gco, name):
    S, D = dmo.shape
    A, C = attn.shape[1], u1.shape[1]
    tr = _row_tile(S, 256)

    def body(dmo_ref, w_ref, a_ref, u1_ref, gat_ref, lng_ref, lnb_ref, gco_ref,
             da_ref, dl_ref, du1_ref, dgat_ref, dgco_ref, dlng_ref, dlnb_ref, dcb_ref):
        @pl.when(pl.program_id(0) == 0)
        def _():
            for ref in (dgat_ref, dgco_ref, dlng_ref, dlnb_ref, dcb_ref):
                ref[...] = jnp.zeros_like(ref)

        dy = lax.dot_general(dmo_ref[...], w_ref[...], NT, preferred_element_type=F32)
        dya, dyc = dy[:, :A], dy[:, A:]
        x = a_ref[...]
        r = lax.rsqrt(jnp.mean(x * x, axis=-1, keepdims=True) + RMS_EPS)
        xh = x * r
        dgat_ref[...] += jnp.sum(dya * xh, axis=0, keepdims=True)
        dxh = dya * gat_ref[...]
        dx = r * (dxh - xh * jnp.mean(dxh * xh, axis=-1, keepdims=True))
        da_ref[...] = dx
        hi = lax.broadcasted_iota(jnp.int32, (A, A), 0) // HEAD_DIM
        hj = lax.broadcasted_iota(jnp.int32, (A, A), 1) // HEAD_DIM
        same_head = (hi == hj).astype(F32)
        dl_ref[...] = jnp.dot(dx * x, same_head, preferred_element_type=F32, precision=lax.Precision.HIGHEST)

        lng = lng_ref[...]
        rstd, u1h, u2, sig, u3, rc = _conv_norms(u1_ref[...], lng, lnb_ref[...])
        u3h = u3 * rc
        dgco_ref[...] += jnp.sum(dyc * u3h, axis=0, keepdims=True)
        du3h = dyc * gco_ref[...]
        du3 = rc * (du3h - u3h * jnp.mean(du3h * u3h, axis=-1, keepdims=True))
        du2 = du3 * (sig * (1.0 + u2 * (1.0 - sig)))
        dlng_ref[...] += jnp.sum(du2 * u1h, axis=0, keepdims=True)
        dlnb_ref[...] += jnp.sum(du2, axis=0, keepdims=True)
        du1h = du2 * lng
        du1 = rstd * (du1h - jnp.mean(du1h, axis=-1, keepdims=True)
                      - u1h * jnp.mean(du1h * u1h, axis=-1, keepdims=True))
        du1_ref[...] = du1
        dcb_ref[...] += jnp.sum(du1, axis=0, keepdims=True)

    arow = pl.BlockSpec((tr, A), lambda i: (i, 0))
    crow = pl.BlockSpec((tr, C), lambda i: (i, 0))
    avec, cvec = _vec_spec(A, 1), _vec_spec(C, 1)
    return pl.pallas_call(
        body, name=name, grid=(S // tr,),
        in_specs=[pl.BlockSpec((tr, D), lambda i: (i, 0)), pl.BlockSpec((A + C, D), lambda i: (0, 0)), arow, crow,
                  avec, cvec, cvec, cvec],
        out_specs=[arow, arow, crow, avec, cvec, cvec, cvec, cvec],
        out_shape=[jax.ShapeDtypeStruct((S, A), F32)] * 2 + [jax.ShapeDtypeStruct((S, C), F32)]
        + [jax.ShapeDtypeStruct((1, A), F32)] + [jax.ShapeDtypeStruct((1, C), F32)] * 4,
        compiler_params=_params(("arbitrary",)),
    )(dmo, wout, attn, u1, gat, lng, lnb, gco)


def _dproj(du1, proj, cw, dq, dk, dv, cos, sin, name):
    S, C = du1.shape
    A = dq.shape[1]
    tr = _row_tile(S, 256)
    nt = S // tr
    per = tr // HALO
    scale = HEAD_DIM ** -0.5

    def body(du_ref, dun_ref, a_ref, b_ref, ah_ref, bh_ref, w_ref, dq_ref, dk_ref, dv_ref, cos_ref, sin_ref,
             dp_ref, dw_ref, win_s, dwin_s, du0_s, tap_s):
        i = pl.program_id(0)

        @pl.when(i == 0)
        def _():
            dw_ref[...] = jnp.zeros_like(dw_ref)

        _store_shifted(win_s, _glu_window(a_ref, b_ref, ah_ref, bh_ref, i == 0), tr + HALO)
        nxt = jnp.where(i == nt - 1, jnp.zeros_like(dun_ref[...]), dun_ref[...])
        _store_shifted(dwin_s, jnp.concatenate([du_ref[...], nxt], axis=0), tr + HALO)
        tap_s[...] = jnp.zeros_like(tap_s)

        def chunk(rc, carry):
            r0 = pl.multiple_of(rc * CONV_CHUNK, CONV_CHUNK)
            for lb in range(C // 128):
                lanes = slice(lb * 128, (lb + 1) * 128)
                du = du_ref[pl.ds(r0, CONV_CHUNK), lanes]
                acc = jnp.zeros((CONV_CHUNK, 128), F32)
                for j in range(CONV_KERNEL):
                    a8, b = divmod(CONV_KERNEL - 1 - j, SUBLANES)
                    acc = acc + w_ref[j:j + 1, lanes] * dwin_s[b, pl.ds(r0 + a8 * SUBLANES, CONV_CHUNK), lanes]
                    a8, b = divmod(FIRST_TAP + j, SUBLANES)
                    prod = du * win_s[b, pl.ds(r0 + a8 * SUBLANES, CONV_CHUNK), lanes]
                    part = prod[0:SUBLANES]
                    for g in range(1, CONV_CHUNK // SUBLANES):
                        part = part + prod[g * SUBLANES:(g + 1) * SUBLANES]
                    tap_s[j * SUBLANES:(j + 1) * SUBLANES, lanes] += part
                du0_s[pl.ds(r0, CONV_CHUNK), lanes] = acc
            return carry

        lax.fori_loop(0, tr // CONV_CHUNK, chunk, 0)
        taps = [jnp.sum(tap_s[j * SUBLANES:(j + 1) * SUBLANES, :], axis=0, keepdims=True)
                for j in range(CONV_KERNEL)]
        taps.append(jnp.zeros((HALO - CONV_KERNEL, C), F32))
        dw_ref[...] += jnp.concatenate(taps, axis=0)
        du0 = du0_s[...]
        a, sig = a_ref[...], _sigmoid(b_ref[...])
        dp_ref[:, 3 * A:3 * A + C] = (du0 * sig).astype(BF16)
        dp_ref[:, 3 * A + C:] = (du0 * a * sig * (1.0 - sig)).astype(BF16)
        cos_, nsin = cos_ref[...], -sin_ref[...]
        for j in range(A // 128):
            lanes = slice(j * 128, (j + 1) * 128)
            dp_ref[:, j * 128:(j + 1) * 128] = (_rotate(dq_ref[:, lanes], cos_, nsin) * scale).astype(BF16)
            dp_ref[:, A + j * 128:A + (j + 1) * 128] = _rotate(dk_ref[:, lanes], cos_, nsin).astype(BF16)
        dp_ref[:, 2 * A:3 * A] = dv_ref[...].astype(BF16)

    ta, tb, ha, hb = _conv_specs(tr, C, 3, 4)
    row = pl.BlockSpec((tr, C), lambda i: (i, 0))
    arow = pl.BlockSpec((tr, A), lambda i: (i, 0))
    tab = pl.BlockSpec((tr, 128), lambda i: (i, 0))
    nxt = pl.BlockSpec((HALO, C), lambda i: (jnp.minimum((i + 1) * per, S // HALO - 1), 0))
    wspec = pl.BlockSpec((HALO, C), lambda i: (0, 0))
    return pl.pallas_call(
        body, name=name, grid=(nt,),
        in_specs=[row, nxt, ta, tb, ha, hb, wspec, arow, arow, arow, tab, tab],
        out_specs=[pl.BlockSpec((tr, 3 * A + 2 * C), lambda i: (i, 0)), wspec],
        out_shape=[jax.ShapeDtypeStruct((S, 3 * A + 2 * C), BF16), jax.ShapeDtypeStruct((HALO, C), F32)],
        scratch_shapes=[pltpu.VMEM((SUBLANES, tr + HALO, C), F32), pltpu.VMEM((SUBLANES, tr + HALO, C), F32),
                        pltpu.VMEM((tr, C), F32), pltpu.VMEM((HALO * SUBLANES, C), F32)],
        compiler_params=_params(("arbitrary",)),
    )(du1, du1, proj, proj, proj, proj, cw, dq, dk, dv, cos, sin)


def _ada_fwd(c_all, w, b, name):
    B, D = c_all.shape
    N = w.shape[1]
    tn = 768 if N % 768 == 0 else N

    def body(c_ref, w_ref, b_ref, o_ref):
        c = c_ref[...]
        a = (c * _sigmoid(c)).astype(BF16)
        o_ref[...] = jnp.dot(a, w_ref[...].astype(BF16), preferred_element_type=F32) + b_ref[...]

    return pl.pallas_call(
        body, name=name, grid=(N // tn,),
        in_specs=[pl.BlockSpec((B, D), lambda j: (0, 0)), pl.BlockSpec((D, tn), lambda j: (0, j)),
                  pl.BlockSpec((1, tn), lambda j: (0, j))],
        out_specs=pl.BlockSpec((B, tn), lambda j: (0, j)),
        out_shape=jax.ShapeDtypeStruct((B, N), F32),
        compiler_params=_params(("parallel",)),
    )(c_all, w, b)


def _ada_wgrad(c_t, dmod, name):
    D, B = c_t.shape
    N = dmod.shape[1]
    tn = 768 if N % 768 == 0 else N

    def body(c_ref, d_ref, o_ref):
        c = c_ref[...]
        a = (c * _sigmoid(c)).astype(BF16)
        o_ref[...] = jnp.dot(a, d_ref[...].astype(BF16), preferred_element_type=F32)

    return pl.pallas_call(
        body, name=name, grid=(N // tn,),
        in_specs=[pl.BlockSpec((D, B), lambda j: (0, 0)), pl.BlockSpec((B, tn), lambda j: (0, j))],
        out_specs=pl.BlockSpec((D, tn), lambda j: (0, j)),
        out_shape=jax.ShapeDtypeStruct((D, N), F32),
        compiler_params=_params(("parallel",)),
    )(c_t, dmod)


def _sum_rows(x, name):
    R, N = x.shape

    def body(x_ref, o_ref):
        acc = x_ref[0:1, :]
        for r in range(1, R):
            acc = acc + x_ref[r:r + 1, :]
        o_ref[...] = acc

    return pl.pallas_call(
        body, name=name, out_shape=jax.ShapeDtypeStruct((1, N), F32),
        compiler_params=_params(),
    )(x)


def _adamw(w, g, m, v, name, deps=()):
    R, C = w.shape
    tr = _row_tile(R, 256) if R % 256 == 0 else R
    bc1 = 1.0 - ADAM_B1 ** ADAM_STEP
    bc2 = 1.0 - ADAM_B2 ** ADAM_STEP
    deps = _deps(deps)

    def body(w_ref, g_ref, m_ref, v_ref, d_ref, mo_ref, vo_ref, go_ref):
        g_ = g_ref[...]
        m_ = ADAM_B1 * m_ref[...] + (1.0 - ADAM_B1) * g_
        v_ = ADAM_B2 * v_ref[...] + (1.0 - ADAM_B2) * (g_ * g_)
        mo_ref[...] = m_
        vo_ref[...] = v_
        go_ref[...] = g_
        d_ref[...] = -ADAM_LR * ((m_ / bc1) / (jnp.sqrt(v_ / bc2) + ADAM_EPS) + ADAM_WD * w_ref[...])

    row = pl.BlockSpec((tr, C), lambda i: (i, 0))
    out = jax.ShapeDtypeStruct((R, C), F32)
    return pl.pallas_call(
        _blind_to(body, 4, len(deps)), name=name, grid=(R // tr,),
        in_specs=[row] * 4 + [ANY_SPEC] * len(deps), out_specs=[row] * 4, out_shape=[out] * 4,
        compiler_params=_params(("parallel",), streaming=True),
    )(w, g, m, v, *deps)


def _coords():
    return lax.axis_index("x"), lax.axis_index("y"), lax.axis_index("c")


def _all_gather8(x, name, deps=()):
    R, N = x.shape
    assert R == 8
    flips = [(fx, fy, fc) for fx in (0, 1) for fy in (0, 1) for fc in (0, 1)][1:]
    deps = _deps(deps)

    def body(x_ref, o_ref, send_sems, recv_sems):
        mx, my, mc = _coords()
        me = 4 * mx + 2 * my + mc

        def rows(dev):
            return o_ref.at[pl.ds(pl.multiple_of(dev * R, R), R), :]

        o_ref[pl.ds(pl.multiple_of(me * R, R), R), :] = x_ref[...]
        copies = []
        for t, (fx, fy, fc) in enumerate(flips):
            peer = (mx ^ fx, my ^ fy, mc ^ fc)
            copies.append(pltpu.make_async_remote_copy(
                src_ref=x_ref, dst_ref=rows(me), send_sem=send_sems.at[t], recv_sem=recv_sems.at[t],
                device_id=peer, device_id_type=MESH))
        for cp in copies:
            cp.start()
        for t, (fx, fy, fc) in enumerate(flips):
            peer_id = 4 * (mx ^ fx) + 2 * (my ^ fy) + (mc ^ fc)
            pltpu.make_async_remote_copy(
                src_ref=x_ref, dst_ref=rows(peer_id), send_sem=send_sems.at[t], recv_sem=recv_sems.at[t],
                device_id=(mx ^ fx, my ^ fy, mc ^ fc), device_id_type=MESH).wait_recv()
        for cp in copies:
            cp.wait_send()

    return pl.pallas_call(
        _blind_to(body, 1, len(deps)), name=name,
        in_specs=[pl.BlockSpec(memory_space=pltpu.VMEM)] + [ANY_SPEC] * len(deps),
        out_specs=pl.BlockSpec(memory_space=pltpu.VMEM),
        out_shape=jax.ShapeDtypeStruct((N_DEV * R, N), F32),
        scratch_shapes=[pltpu.SemaphoreType.DMA((7,)), pltpu.SemaphoreType.DMA((7,))],
        compiler_params=pltpu.CompilerParams(has_side_effects=True, vmem_limit_bytes=VMEM_LIMIT_BYTES),
    )(x, *deps)


def _half_rows(rows, half):
    return pl.ds(pl.multiple_of(half * (rows // 2), 8), rows // 2)


def _split_start(bufs, plan, n, name):
    nb = len(bufs)

    def body(*refs):
        send_sems, recv_sems, token = refs[nb], refs[nb + 1], refs[-1]
        for t, (src, dst, dev) in enumerate(plan(refs[:nb])):
            pltpu.make_async_remote_copy(src_ref=src, dst_ref=dst, send_sem=send_sems.at[t],
                                         recv_sem=recv_sems.at[t], device_id=dev, device_id_type=MESH).start()
        token[...] = jnp.zeros_like(token)

    out = pl.pallas_call(
        body, name=name,
        out_shape=(pltpu.SemaphoreType.DMA((n,)), pltpu.SemaphoreType.DMA((n,)),
                   *[pltpu.HBM(b.shape, b.dtype) for b in bufs], jax.ShapeDtypeStruct((8, 128), F32)),
        in_specs=[HBM_SPEC] * nb,
        out_specs=(SEM_SPEC, SEM_SPEC, *[HBM_SPEC] * nb, pl.BlockSpec(memory_space=pltpu.VMEM)),
        input_output_aliases={i: 2 + i for i in range(nb)},
        compiler_params=pltpu.CompilerParams(has_side_effects=DATAFLOW),
    )(*[pltpu.with_memory_space_constraint(b, pltpu.HBM) for b in bufs])
    return out[0], out[1], list(out[2:2 + nb]), out[-1]


def _split_wait(bufs, send_sems, recv_sems, plan, after, name):
    nb = len(bufs)
    after = _deps(after)

    def body(*refs):
        ss, rs = refs[nb], refs[nb + 1]
        for t, (src, dst, dev) in enumerate(plan(refs[:nb])):
            cp = pltpu.make_async_remote_copy(src_ref=src, dst_ref=dst, send_sem=ss.at[t], recv_sem=rs.at[t],
                                              device_id=dev, device_id_type=MESH)
            cp.wait_send()
            cp.wait_recv()

    out = pl.pallas_call(
        body, name=name,
        out_shape=tuple(pltpu.HBM(b.shape, b.dtype) for b in bufs),
        in_specs=[HBM_SPEC] * nb + [SEM_SPEC, SEM_SPEC] + [ANY_SPEC] * len(after),
        out_specs=tuple([HBM_SPEC] * nb),
        input_output_aliases={i: i for i in range(nb)},
        compiler_params=pltpu.CompilerParams(has_side_effects=DATAFLOW),
    )(*bufs, send_sems, recv_sems, *after)
    return list(out)


class _Exchange:
    def __init__(self, bufs, plan, n, name):
        self.plan, self.name = plan, name
        self.send_sems, self.recv_sems, self.bufs, self.token = _split_start(bufs, plan, n, name + "_start")

    def wait(self, after):
        return _split_wait(self.bufs, self.send_sems, self.recv_sems, self.plan, after, self.name + "_wait")


def _cast_place(w, chip_idx, name, deps=()):
    R, C = w.shape
    tr = _row_tile(R, 256) if R % 256 == 0 else R
    deps = _deps(deps)

    def body(k_ref, w_ref, o_ref):
        o_ref[...] = w_ref[...].astype(BF16)

    grid_spec = pltpu.PrefetchScalarGridSpec(
        num_scalar_prefetch=1, grid=(R // tr,),
        in_specs=[pl.BlockSpec((tr, C), lambda i, k: (i, 0))] + [ANY_SPEC] * len(deps),
        out_specs=pl.BlockSpec((None, tr, C), lambda i, k: (k[0], i, 0)),
    )
    return pl.pallas_call(
        _blind_to(body, 2, len(deps)), name=name, grid_spec=grid_spec,
        out_shape=jax.ShapeDtypeStruct((N_CHIPS, R, C), BF16),
        compiler_params=_params(("parallel",), streaming=True),
    )(chip_idx, w, *deps)


def _plan_gather(refs):
    mx, my, mc = _coords()
    me = 2 * mx + my
    plan = []
    for g in refs:
        mine = g.at[me, _half_rows(g.shape[1], mc), :]
        for fx, fy in XY_FLIPS:
            plan.append((mine, mine, (mx ^ fx, my ^ fy, mc)))
    return plan


def _plan_swap(refs):
    mx, my, mc = _coords()
    plan = []
    for g in refs:
        for fx, fy in XY_FLIPS:
            have = g.at[2 * (mx ^ fx) + (my ^ fy), _half_rows(g.shape[1], mc), :]
            plan.append((have, have, (mx, my, 1 - mc)))
    return plan


def _plan_other_halves(refs):
    n = len(refs) // 2
    mx, my, mc = _coords()
    return [(g.at[pl.ds(0, N_CHIPS), _half_rows(g.shape[1], 1 - mc), :], land, (mx, my, 1 - mc))
            for g, land in zip(refs[:n], refs[n:])]


def _plan_chunks(refs):
    n = len(refs) // 2
    mx, my, mc = _coords()
    plan = []
    for s, land in zip(refs[:n], refs[n:]):
        for t, (fx, fy) in enumerate(XY_FLIPS):
            plan.append((s.at[2 * (mx ^ fx) + (my ^ fy)], land.at[t], (mx ^ fx, my ^ fy, mc)))
    return plan


def _plan_share(refs):
    mx, my, mc = _coords()
    return [(full.at[mc], full.at[mc], (mx, my, 1 - mc)) for full in refs]


def _add_half(g, recv, core_idx, name):
    nk, R, C = g.shape
    rh = R // 2
    tr = _row_tile(rh, 128) if rh % 128 == 0 else rh
    nt = rh // tr

    def body(c_ref, g_ref, r_ref, o_ref):
        o_ref[...] = (g_ref[...] + r_ref[...]).astype(BF16)

    grid_spec = pltpu.PrefetchScalarGridSpec(
        num_scalar_prefetch=1, grid=(nk, nt),
        in_specs=[pl.BlockSpec((None, tr, C), lambda k, i, c: (k, c[0] * nt + i, 0)),
                  pl.BlockSpec((None, tr, C), lambda k, i, c: (k, i, 0))],
        out_specs=pl.BlockSpec((None, tr, C), lambda k, i, c: (k, i, 0)),
    )
    return pl.pallas_call(
        body, name=name, grid_spec=grid_spec, out_shape=jax.ShapeDtypeStruct((nk, rh, C), BF16),
        compiler_params=_params(("parallel", "parallel"), streaming=True),
    )(core_idx, g, recv)


def _sum_chips(s, land, chip_core, name):
    _, rh, C = s.shape
    tr = _row_tile(rh, 128) if rh % 128 == 0 else rh

    def body(p_ref, s_ref, l_ref, o_ref):
        me = p_ref[0]
        acc = None
        for j in range(N_CHIPS):
            t = jnp.maximum(jnp.bitwise_xor(me, j) - 1, 0)
            term = jnp.where(me == j, s_ref[...], l_ref[t]).astype(F32)
            acc = term if acc is None else acc + term
        o_ref[...] = acc

    grid_spec = pltpu.PrefetchScalarGridSpec(
        num_scalar_prefetch=1, grid=(rh // tr,),
        in_specs=[pl.BlockSpec((None, tr, C), lambda i, p: (p[0], i, 0)),
                  pl.BlockSpec((3, tr, C), lambda i, p: (0, i, 0))],
        out_specs=pl.BlockSpec((None, tr, C), lambda i, p: (p[1], i, 0)),
    )
    return pl.pallas_call(
        body, name=name, grid_spec=grid_spec, out_shape=jax.ShapeDtypeStruct((2, rh, C), F32),
        compiler_params=_params(("parallel",), streaming=True),
    )(chip_core, s, land)


def _rs_send_halves(grads, tag):
    lands = [lax.empty((g.shape[0], g.shape[1] // 2, g.shape[2]), g.dtype) for g in grads]
    return _Exchange(list(grads) + lands, _plan_other_halves, len(grads), f"rs_halves_{tag}")


def _rs_send_chunks(ex, after, core_idx, tag):
    bufs = ex.wait(after)
    n = len(bufs) // 2
    sums = [_add_half(g, r, core_idx, f"rs_add_{tag}_{i}") for i, (g, r) in enumerate(zip(bufs[:n], bufs[n:]))]
    lands = [lax.empty((3,) + s.shape[1:], s.dtype) for s in sums]
    return _Exchange(sums + lands, _plan_chunks, 3 * n, f"rs_chunks_{tag}")


def _rs_send_share(ex, after, chip_core, tag):
    bufs = ex.wait(after)
    n = len(bufs) // 2
    fulls = [_sum_chips(s, l, chip_core, f"rs_sum_{tag}_{i}") for i, (s, l) in enumerate(zip(bufs[:n], bufs[n:]))]
    return _Exchange(fulls, _plan_share, n, f"rs_share_{tag}")


def _rs_finish(ex, after):
    return [b.reshape(2 * b.shape[1], b.shape[2]) for b in ex.wait(after)]


def _ffn_backward(dh_out, df, saved, gain, sc, wg, wu, wd, core_idx, tag, prev=None, last=False):
    h, n, ga, up, act, _ = saved
    (dwd,) = _wgrad_chunk_lhs([act], df, f"{tag}_dwd")
    ex_d = _rs_send_halves([dwd], f"{tag}_d")
    dga, dup = _ffn_dact(df, wd, ga, up, f"{tag}_dact", deps=[ex_d.token])
    ex_d = _rs_send_chunks(ex_d, [dga], core_idx, f"{tag}_d")
    dwg, dwu = _wgrad_chunk_lhs([dga, dup], n, f"{tag}_dwgu", deps=[ex_d.token])
    ex_gu = _rs_send_halves([dwg, dwu], f"{tag}_gu")
    dn = _mm_reduce([dga, dup], [wg, wu], True, True, f"{tag}_dn", deps=[ex_gu.token])
    if not last:
        ex_gu = _rs_send_chunks(ex_gu, [dn], core_idx, f"{tag}_gu")
    outs = _norm_mod_bwd(dn, h, gain, sc, dh_out, f"{tag}_norm_bwd", prev=prev, deps=[ex_gu.token])
    return outs, (ex_d, ex_gu)


def _pad_cols(v, n):
    return jnp.pad(v, ((0, 0), (0, n - v.shape[1])))


def _mixer_forward(h1, n2, gt2, win, wout, conv_w, conv_dw_b, conv_ln_g, conv_ln_b, attn_out_g, conv_out_g,
                   next_norm=None):
    S, D = h1.shape
    cos, sin = _rope_tables(S)
    proj = _mm_cols_rope(n2, win, cos, sin, "mix_in")
    attn, lse = _attn_fwd(proj, "attn_fwd")
    u1, y = _mixer_merge(proj, attn, conv_w, conv_dw_b, conv_ln_g, conv_ln_b, attn_out_g, conv_out_g, "mix_merge")
    h2, mo, *n_next = _mm_residual(y[None], wout.reshape(1, D, D), h1, gt2, 1.0, "mix_out", next_norm=next_norm)
    return (h2, *n_next), (h1, n2, proj, cos, sin, attn, lse, u1, y, mo)


def _mixer_backward(dh2, dmo, saved, mix_norm_g, sc2, win, wout, conv_w, conv_ln_g, conv_ln_b, attn_out_g,
                    conv_out_g, core_idx, prev=None):
    h1, n2, proj, cos, sin, attn, lse, u1, y, _ = saved
    S, D = h1.shape
    (dwout,) = _wgrad_chunk_lhs([y[None]], dmo, "mix_dwout")
    dattn, delta, du1, d_attn_g, d_gco, d_lng, d_lnb, d_cb = _mix_out_bwd(
        dmo, wout.reshape(D, D), attn, u1, attn_out_g, conv_ln_g, conv_ln_b, conv_out_g, "mix_out_bwd")
    dq, dk, dv = _attn_bwd(proj, dattn, lse, delta, "attn_bwd")
    dproj, d_cw = _dproj(du1, proj, conv_w, dq, dk, dv, cos, sin, "mix_dproj")
    (dwin,) = _wgrad_chunk_rhs(n2, [dproj], N_CHIPS, False, "mix_dwin")
    ex = _rs_send_halves([dwin, dwout.reshape(N_CHIPS, D // N_CHIPS, D)], "mix")
    dn2 = _mm_reduce([dproj], [win], False, False, "mix_dn", deps=[ex.token])
    ex = _rs_send_chunks(ex, [dn2], core_idx, "mix")
    outs = _norm_mod_bwd(dn2, h1, mix_norm_g, sc2, dh2, "mix_norm_bwd", prev=prev, deps=[ex.token])
    return outs, ex, (d_cb, d_lng, d_lnb, d_attn_g, d_gco, d_cw)


def kernel(x, c, w_ada, b_ada, ffn1_norm_g, ffn1_w_gate, ffn1_w_up, ffn1_w_down, mix_norm_g, w_in, conv_dw_w, conv_dw_b, conv_ln_g, conv_ln_b, attn_out_g, conv_out_g, w_out, ffn2_norm_g, ffn2_w_gate, ffn2_w_up, ffn2_w_down, final_norm_g, loss_target, m_w_ada, m_b_ada, m_ffn1_norm_g, m_ffn1_w_gate, m_ffn1_w_up, m_ffn1_w_down, m_mix_norm_g, m_w_in, m_conv_dw_w, m_conv_dw_b, m_conv_ln_g, m_conv_ln_b, m_attn_out_g, m_conv_out_g, m_w_out, m_ffn2_norm_g, m_ffn2_w_gate, m_ffn2_w_up, m_ffn2_w_down, m_final_norm_g, v_w_ada, v_b_ada, v_ffn1_norm_g, v_ffn1_w_gate, v_ffn1_w_up, v_ffn1_w_down, v_mix_norm_g, v_w_in, v_conv_dw_w, v_conv_dw_b, v_conv_ln_g, v_conv_ln_b, v_attn_out_g, v_conv_out_g, v_w_out, v_ffn2_norm_g, v_ffn2_w_gate, v_ffn2_w_up, v_ffn2_w_down, v_final_norm_g):
    S, D = x.shape[1], x.shape[2]
    mx, my, mc = _coords()
    chip = 2 * mx + my
    dev = 4 * mx + 2 * my + mc
    chip_idx = chip.astype(jnp.int32).reshape(1)
    core_idx = mc.astype(jnp.int32).reshape(1)
    chip_core = jnp.stack([chip, mc]).astype(jnp.int32)
    h0 = x[0]
    target = loss_target[0]

    ncw = CONV_KERNEL * 128
    n0 = -(-(D + ncw) // 1024) * 1024
    pk0 = _pad_cols(jnp.concatenate([c.reshape(1, D), conv_dw_w.reshape(1, ncw)], axis=1), n0)
    g0 = _all_gather8(pk0.reshape(8, n0 // 8), "gather_c").reshape(N_DEV, n0)
    c_all = g0[:, :D]
    conv_w = jnp.concatenate([g0[2 * kc, D:D + ncw].reshape(CONV_KERNEL, 128) for kc in range(N_CHIPS)], axis=1)
    conv_w = jnp.pad(conv_w, ((0, HALO - CONV_KERNEL), (0, 0)))
    nmod = w_ada.shape[2]
    b_shard = lax.dynamic_slice(b_ada, (0, chip * nmod), (1, nmod))
    mod_part = _ada_fwd(c_all, w_ada[0], b_shard, "ada_fwd")
    g1 = _all_gather8(mod_part, "gather_mod")
    mod_all = jnp.concatenate([g1[16 * kc:16 * kc + 8] for kc in range(N_CHIPS)], axis=1)
    mod = lax.dynamic_slice(mod_all, (dev, 0), (1, 9 * D))
    sh1, sc1, gt1, sh2, sc2, gt2, sh3, sc3, gt3 = [mod[:, i * D:(i + 1) * D] for i in range(9)]

    def gather_start(ws, tag, dep):
        slots = [_cast_place(w, chip_idx, f"cast_{tag}_{i}", deps=[dep]) for i, w in enumerate(ws)]
        return _Exchange(slots, _plan_gather, 3 * len(ws), f"gather_{tag}")

    def swap_start(ex, after, tag):
        return _Exchange(ex.wait(after), _plan_swap, 3 * len(ex.bufs), f"swap_{tag}")

    ex_gu1 = gather_start([ffn1_w_gate[0].T, ffn1_w_up[0].T], "ffn1_gu", g1)
    n1 = _norm_mod(h0, ffn1_norm_g, sc1, sh1, "ffn1_norm", deps=[ex_gu1.token])
    ex_d1 = gather_start([ffn1_w_down[0]], "ffn1_d", n1)
    ex_wm = gather_start([w_in[0], w_out[0]], "mix", ex_d1.token)
    ex_w2 = gather_start([ffn2_w_gate[0].T, ffn2_w_up[0].T, ffn2_w_down[0]], "ffn2", ex_wm.token)

    wg1, wu1 = swap_start(ex_gu1, [ex_w2.token], "ffn1_gu").wait([])
    ga1, up1, act1 = _ffn_gate_up(n1, wg1, wu1, "ffn1_gate_up")
    (wd1,) = swap_start(ex_d1, [act1], "ffn1_d").wait([])
    ex_wm = swap_start(ex_wm, [wd1], "mix")
    h1, f1, n2 = _mm_residual(act1, wd1, h0, gt1, 0.5, "ffn1_down", next_norm=(mix_norm_g, sc2, sh2))
    saved1 = (h0, n1, ga1, up1, act1, f1)
    win, wout = ex_wm.wait([h1])
    ex_w2 = swap_start(ex_w2, [h1], "ffn2")
    (h2, n3), saved2 = _mixer_forward(h1, n2, gt2, win, wout, conv_w, conv_dw_b, conv_ln_g, conv_ln_b,
                                      attn_out_g, conv_out_g, next_norm=(ffn2_norm_g, sc3, sh3))
    wg2, wu2, wd2 = ex_w2.wait([h2])
    fga3, fup3, act3 = _ffn_gate_up(n3, wg2, wu2, "ffn2_gate_up")
    loss_part, dh3, d_final_g, df3, d_gt3 = _mm_residual_loss(
        act3, wd2, h2, gt3, 0.5, final_norm_g.reshape(1, D), target, "ffn2_down_loss")
    saved3 = (h2, n3, fga3, fup3, act3, None)

    (dh2, d_sh3, d_sc3, d_gain3, dmo, d_gt2), (ex_d2, ex_gu2) = _ffn_backward(
        dh3, df3, saved3, ffn2_norm_g, sc3, wg2, wu2, wd2, core_idx, "ffn2", prev=(saved2[-1], gt2, 1.0))
    (dh1, d_sh2, d_sc2, d_gain2, df1, d_gt1), ex_mix, small_mix = _mixer_backward(
        dh2, dmo, saved2, mix_norm_g, sc2, win, wout, conv_w, conv_ln_g, conv_ln_b, attn_out_g, conv_out_g, core_idx,
        prev=(f1, gt1, 0.5))
    d_cb, d_lng, d_lnb, d_attn_g, d_gco, d_cw = small_mix
    (dh0, d_sh1, d_sc1, d_gain1), (ex_d1, ex_gu1) = _ffn_backward(
        dh1, df1, saved1, ffn1_norm_g, sc1, wg1, wu1, wd1, core_idx, "ffn1", last=True)

    dmod = jnp.concatenate([d_sh1, d_sc1, d_gt1, d_sh2, d_sc2, d_gt2, d_sh3, d_sc3, d_gt3], axis=1)
    small = [d_gain1, d_gain2, d_gain3, d_final_g, d_cb, d_lng, d_lnb, d_attn_g, d_gco,
             d_cw[:CONV_KERNEL].reshape(1, CONV_KERNEL * CONV_WIDTH), loss_part]
    pk1 = jnp.concatenate([dmod] + small, axis=1)
    n1_ = -(-pk1.shape[1] // 1024) * 1024
    gathered = _all_gather8(_pad_cols(pk1, n1_).reshape(8, n1_ // 8), "gather_small").reshape(N_DEV, n1_)
    ex_gu1 = _rs_send_chunks(ex_gu1, [gathered], core_idx, "ffn1_gu")
    tot = _sum_rows(gathered, "sum_small")
    off = [0]

    def take(nel):
        out = tot[:, off[0]:off[0] + nel]
        off[0] += nel
        return out

    g_b_ada = take(9 * D)
    g_ffn1_norm, g_mix_norm, g_ffn2_norm, g_final = take(D), take(D), take(D), take(D)
    g_cb, g_lng, g_lnb, g_attn_g, g_gco = take(512), take(512), take(512), take(512), take(512)
    g_cw_full = take(CONV_KERNEL * CONV_WIDTH).reshape(CONV_KERNEL, CONV_WIDTH)
    loss = take(128)[0, 0]
    g_cw = lax.dynamic_slice(g_cw_full, (0, chip * 128), (CONV_KERNEL, 128))

    dmod_shard = lax.dynamic_slice(gathered[:, :9 * D], (0, chip * nmod), (N_DEV, nmod))
    dmod16 = jnp.pad(dmod_shard, ((0, N_DEV), (0, 0)))
    c_t16 = jnp.pad(c_all.T, ((0, 0), (0, N_DEV)))
    g_w_ada = _ada_wgrad(c_t16, dmod16, "ada_wgrad")

    names = ["w_ada", "b_ada", "ffn1_norm_g", "ffn1_w_gate", "ffn1_w_up", "ffn1_w_down", "mix_norm_g", "w_in",
             "conv_dw_w", "conv_dw_b", "conv_ln_g", "conv_ln_b", "attn_out_g", "conv_out_g", "w_out", "ffn2_norm_g",
             "ffn2_w_gate", "ffn2_w_up", "ffn2_w_down", "final_norm_g"]
    weights = dict(zip(names, [w_ada, b_ada, ffn1_norm_g, ffn1_w_gate, ffn1_w_up, ffn1_w_down, mix_norm_g, w_in,
                               conv_dw_w, conv_dw_b, conv_ln_g, conv_ln_b, attn_out_g, conv_out_g, w_out,
                               ffn2_norm_g, ffn2_w_gate, ffn2_w_up, ffn2_w_down, final_norm_g]))
    ms = dict(zip(names, [m_w_ada, m_b_ada, m_ffn1_norm_g, m_ffn1_w_gate, m_ffn1_w_up, m_ffn1_w_down, m_mix_norm_g,
                          m_w_in, m_conv_dw_w, m_conv_dw_b, m_conv_ln_g, m_conv_ln_b, m_attn_out_g, m_conv_out_g,
                          m_w_out, m_ffn2_norm_g, m_ffn2_w_gate, m_ffn2_w_up, m_ffn2_w_down, m_final_norm_g]))
    vs = dict(zip(names, [v_w_ada, v_b_ada, v_ffn1_norm_g, v_ffn1_w_gate, v_ffn1_w_up, v_ffn1_w_down, v_mix_norm_g,
                          v_w_in, v_conv_dw_w, v_conv_dw_b, v_conv_ln_g, v_conv_ln_b, v_attn_out_g, v_conv_out_g,
                          v_w_out, v_ffn2_norm_g, v_ffn2_w_gate, v_ffn2_w_up, v_ffn2_w_down, v_final_norm_g]))
    grads, deltas, new_ms, new_vs = {}, {}, {}, {}

    def adamw_big(nm, g2d, deps=(), transposed=False):
        shape = weights[nm].shape
        two_d = (shape[-2], shape[-1])

        def view(t):
            return t.reshape(two_d).T if transposed else t.reshape(two_d)

        d_, m_, v_, g_ = _adamw(view(weights[nm]), g2d, view(ms[nm]), view(vs[nm]), f"adamw_{nm}", deps=deps)
        grads[nm], deltas[nm], new_ms[nm], new_vs[nm] = (
            (t.T if transposed else t).reshape(shape) for t in (g_, d_, m_, v_))
        return d_

    d_ada = adamw_big("w_ada", g_w_ada, deps=[ex_gu1.token])
    small_grads = {"b_ada": g_b_ada, "ffn1_norm_g": g_ffn1_norm, "mix_norm_g": g_mix_norm, "conv_dw_w": g_cw,
                   "conv_dw_b": g_cb, "conv_ln_g": g_lng, "conv_ln_b": g_lnb, "attn_out_g": g_attn_g,
                   "conv_out_g": g_gco, "ffn2_norm_g": g_ffn2_norm, "final_norm_g": g_final}
    small_names = [nm for nm in names if nm in small_grads]

    def pack_small(arrs):
        flat = jnp.concatenate([arrs[nm].reshape(1, -1) for nm in small_names], axis=1)
        npad = -(-flat.shape[1] // 1024) * 1024
        return _pad_cols(flat, npad).reshape(8, npad // 8)

    d_s, m_s, v_s, _ = _adamw(pack_small(weights), pack_small(small_grads), pack_small(ms), pack_small(vs),
                           "adamw_small")
    pos = 0
    for nm in small_names:
        shape, nel = weights[nm].shape, weights[nm].size
        grads[nm] = small_grads[nm].reshape(shape)
        deltas[nm], new_ms[nm], new_vs[nm] = (t.reshape(1, -1)[:, pos:pos + nel].reshape(shape)
                                              for t in (d_s, m_s, v_s))
        pos += nel

    ex_d2 = _rs_send_share(ex_d2, [d_ada, d_s], chip_core, "ffn2_d")
    ex_gu2 = _rs_send_share(ex_gu2, [ex_d2.token], chip_core, "ffn2_gu")
    ex_mix = _rs_send_share(ex_mix, [ex_gu2.token], chip_core, "mix")
    ex_d1 = _rs_send_share(ex_d1, [ex_mix.token], chip_core, "ffn1_d")
    (g_wd2,) = _rs_finish(ex_d2, [ex_d1.token])
    last = [adamw_big("ffn2_w_down", g_wd2)]
    g_wg2, g_wu2 = _rs_finish(ex_gu2, last)
    last = [adamw_big("ffn2_w_gate", g_wg2, transposed=True), adamw_big("ffn2_w_up", g_wu2, transposed=True)]
    g_win, g_wout = _rs_finish(ex_mix, last)
    last = [adamw_big("w_in", g_win), adamw_big("w_out", g_wout)]
    (g_wd1,) = _rs_finish(ex_d1, last)
    last = [adamw_big("ffn1_w_down", g_wd1)]
    ex_gu1 = _rs_send_share(ex_gu1, last, chip_core, "ffn1_gu")
    g_wg1, g_wu1 = _rs_finish(ex_gu1, [])
    adamw_big("ffn1_w_gate", g_wg1, transposed=True)
    adamw_big("ffn1_w_up", g_wu1, transposed=True)

    return (loss, dh0[None], *[grads[nm] for nm in names], *[deltas[nm] for nm in names],
            *[new_ms[nm] for nm in names], *[new_vs[nm] for nm in names])
```
